```python
import jax, jax.numpy as jnp
from jax import lax
import numpy as np

D_MODEL = 1024
BATCH = 8
SEQ = 8192
DEPTH = 2

N_MIXERS = 2
N_HGRN = (DEPTH + N_MIXERS - 1) // N_MIXERS
N_CONV = DEPTH // N_MIXERS
HGRN_EXPAND = 128
HGRN_HEADS = D_MODEL // HGRN_EXPAND
HGRN_DK = HGRN_EXPAND
HGRN_FDIM = HGRN_HEADS * HGRN_DK
HGRN_DV = D_MODEL // HGRN_HEADS
HGRN_CHUNK = 64
CONV_WIDTH = 31
FFN_CONV_WIDTH = 3
D_FF = 2816
N_MOD = 6
EPS = 1e-6

kernel_name = "hybrid_hgrn2_conformer_convffn_adaln"


def rmsnorm(x, g):
    xf = x.astype(jnp.float32)
    y = xf * lax.rsqrt(jnp.mean(xf * xf, axis=-1, keepdims=True) + EPS)
    return (y * g).astype(x.dtype)


def layernorm(x, g, b):
    xf = x.astype(jnp.float32)
    mu = jnp.mean(xf, axis=-1, keepdims=True)
    var = jnp.mean(jnp.square(xf - mu), axis=-1, keepdims=True)
    y = (xf - mu) * lax.rsqrt(var + EPS)
    return (y * g + b).astype(x.dtype)


def causal_dwconv(x, w, b):
    width, ch = w.shape
    y = lax.conv_general_dilated(
        x, w[:, None, :].astype(x.dtype), window_strides=(1,), padding=[(width - 1, 0)],
        dimension_numbers=("NWC", "WIO", "NWC"), feature_group_count=ch)
    return y + b


def hgrn2_mixer(h, w_in, lb, gnorm_g, w_out):
    B, T, _ = h.shape
    nc = T // HGRN_CHUNK
    proj = h @ w_in
    q, fz, i, g = jnp.split(proj, [HGRN_FDIM, 2 * HGRN_FDIM, 2 * HGRN_FDIM + D_MODEL], axis=-1)
    q = jax.nn.silu(q.astype(jnp.float32))
    f = lb + (1.0 - lb) * jax.nn.sigmoid(fz.astype(jnp.float32))
    k = 1.0 - f
    logf = jnp.log(f)
    v = i.astype(jnp.float32)

    def to_chunks(a, d):
        a = a.reshape(B, nc, HGRN_CHUNK, HGRN_HEADS, d)
        return jnp.transpose(a, (1, 0, 3, 2, 4))

    qc, kc, gc = to_chunks(q, HGRN_DK), to_chunks(k, HGRN_DK), to_chunks(logf, HGRN_DK)
    vc = to_chunks(v, HGRN_DV)
    causal = jnp.tril(jnp.ones((HGRN_CHUNK, HGRN_CHUNK), dtype=bool))[:, :, None]

    def step(S, inp):
        qb, kb, vb, gb = inp
        bcum = jnp.cumsum(gb, axis=2)
        diff = bcum[:, :, :, None, :] - bcum[:, :, None, :, :]
        decay = jnp.exp(jnp.where(causal, diff, -jnp.inf))
        scores = jnp.einsum("bhtd,bhtsd,bhsd->bhts", qb, decay, kb)
        o = jnp.einsum("bhts,bhsv->bhtv", scores, vb)
        o = o + jnp.einsum("bhtd,bhdv->bhtv", qb * jnp.exp(bcum), S)
        bend = bcum[:, :, -1]
        S_new = jnp.exp(bend)[..., None] * S + jnp.einsum(
            "bhsd,bhsv->bhdv", kb * jnp.exp(bend[:, :, None, :] - bcum), vb)
        return S_new, o

    S0 = jnp.zeros((B, HGRN_HEADS, HGRN_DK, HGRN_DV), jnp.float32)
    _, oc = lax.scan(step, S0, (qc, kc, vc, gc))
    o = jnp.transpose(oc, (1, 0, 3, 2, 4)).reshape(B, T, HGRN_HEADS, HGRN_DV)
    o = rmsnorm(o, gnorm_g).reshape(B, T, D_MODEL)
    o = o * jax.nn.silu(g.astype(jnp.float32))
    return o.astype(h.dtype) @ w_out


def conformer_conv_mixer(h, w_in, b_in, dw_w, dw_b, ln_g, ln_b, w_out, b_out):
    u = h @ w_in + b_in
    a, gt = jnp.split(u, 2, axis=-1)
    u = a * jax.nn.sigmoid(gt)
    u = causal_dwconv(u, dw_w, dw_b)
    u = jax.nn.silu(layernorm(u, ln_g, ln_b))
    return u @ w_out + b_out


def conv_ffn(h, w_up, dw_w, dw_b, w_down):
    u = causal_dwconv(h @ w_up, dw_w, dw_b)
    a, b = jnp.split(u, 2, axis=-1)
    return (jax.nn.silu(a) * b) @ w_down


def _fwd_setup_inputs(seed: int = 0) -> dict:
    key = jax.random.key(seed)
    ks = jax.random.split(key, 24)
    D, F = D_MODEL, D_FF
    nrm = lambda k, shape, s: jax.random.normal(k, shape, jnp.float32) * s
    return {
        "x": nrm(ks[0], (BATCH, SEQ, D), 1.0),
        "c": nrm(ks[1], (BATCH, D), 1.0),
        "ada_w": nrm(ks[2], (DEPTH, D, N_MOD * D), 0.5 * D ** -0.5),
        "ada_b": nrm(ks[3], (DEPTH, N_MOD * D), 0.02),
        "pre_mix_g": 1.0 + nrm(ks[4], (DEPTH, D), 0.02),
        "post_mix_g": 1.0 + nrm(ks[5], (DEPTH, D), 0.02),
        "pre_ffn_g": 1.0 + nrm(ks[6], (DEPTH, D), 0.02),
        "post_ffn_g": 1.0 + nrm(ks[7], (DEPTH, D), 0.02),
        "hgrn_w_in": nrm(ks[8], (N_HGRN, D, 2 * HGRN_FDIM + 2 * D), D ** -0.5),
        "hgrn_lb_logits": nrm(ks[9], (DEPTH + 1, HGRN_FDIM), 1.0),
        "hgrn_gnorm_g": 1.0 + nrm(ks[10], (N_HGRN, HGRN_DV), 0.02),
        "hgrn_w_out": nrm(ks[11], (N_HGRN, D, D), D ** -0.5),
        "conv_w_in": nrm(ks[12], (N_CONV, D, 2 * D), D ** -0.5),
        "conv_b_in": nrm(ks[13], (N_CONV, 2 * D), 0.02),
        "conv_dw_w": nrm(ks[14], (N_CONV, CONV_WIDTH, D), CONV_WIDTH ** -0.5),
        "conv_dw_b": nrm(ks[15], (N_CONV, D), 0.02),
        "conv_ln_g": 1.0 + nrm(ks[16], (N_CONV, D), 0.02),
        "conv_ln_b": nrm(ks[17], (N_CONV, D), 0.02),
        "conv_w_out": nrm(ks[18], (N_CONV, D, D), D ** -0.5),
        "conv_b_out": nrm(ks[19], (N_CONV, D), 0.02),
        "ffn_w_up": nrm(ks[20], (DEPTH, D, 2 * F), D ** -0.5),
        "ffn_dw_w": nrm(ks[21], (DEPTH, FFN_CONV_WIDTH, 2 * F), FFN_CONV_WIDTH ** -0.5),
        "ffn_dw_b": nrm(ks[22], (DEPTH, 2 * F), 0.02),
        "ffn_w_down": nrm(ks[23], (DEPTH, F, D), F ** -0.5),
    }


def _fwd_reference(x, c, ada_w, ada_b, pre_mix_g, post_mix_g, pre_ffn_g, post_ffn_g,
              hgrn_w_in, hgrn_lb_logits, hgrn_gnorm_g, hgrn_w_out,
              conv_w_in, conv_b_in, conv_dw_w, conv_dw_b, conv_ln_g, conv_ln_b,
              conv_w_out, conv_b_out, ffn_w_up, ffn_dw_w, ffn_dw_b, ffn_w_down):
    lb_all = jnp.cumsum(jax.nn.softmax(hgrn_lb_logits.astype(jnp.float32), axis=0), axis=0)
    cond = jax.nn.silu(c)
    for i in range(DEPTH):
        mod = (cond @ ada_w[i] + ada_b[i])[:, None, :]
        sh1, sc1, g1, sh2, sc2, g2 = jnp.split(mod, N_MOD, axis=-1)
        h = rmsnorm(x, pre_mix_g[i]) * (1.0 + sc1) + sh1
        j = i // N_MIXERS
        if i % N_MIXERS == 0:
            y = hgrn2_mixer(h, hgrn_w_in[j], lb_all[i], hgrn_gnorm_g[j], hgrn_w_out[j])
        else:
            y = conformer_conv_mixer(h, conv_w_in[j], conv_b_in[j], conv_dw_w[j], conv_dw_b[j],
                                     conv_ln_g[j], conv_ln_b[j], conv_w_out[j], conv_b_out[j])
        x = x + g1 * rmsnorm(y, post_mix_g[i])
        h = rmsnorm(x, pre_ffn_g[i]) * (1.0 + sc2) + sh2
        y = conv_ffn(h, ffn_w_up[i], ffn_dw_w[i], ffn_dw_b[i], ffn_w_down[i])
        x = x + g2 * rmsnorm(y, post_ffn_g[i])
    return x


import jax as _jax
import jax.numpy as _jnp

TWIN_FORMAT = 'train_step'
FWD_PARAMS = ['x', 'c', 'ada_w', 'ada_b', 'pre_mix_g', 'post_mix_g', 'pre_ffn_g', 'post_ffn_g', 'hgrn_w_in', 'hgrn_lb_logits', 'hgrn_gnorm_g', 'hgrn_w_out', 'conv_w_in', 'conv_b_in', 'conv_dw_w', 'conv_dw_b', 'conv_ln_g', 'conv_ln_b', 'conv_w_out', 'conv_b_out', 'ffn_w_up', 'ffn_dw_w', 'ffn_dw_b', 'ffn_w_down']
TWIN_WEIGHTS = ['ada_w', 'ada_b', 'pre_mix_g', 'post_mix_g', 'pre_ffn_g', 'post_ffn_g', 'hgrn_w_in', 'hgrn_lb_logits', 'hgrn_gnorm_g', 'hgrn_w_out', 'conv_w_in', 'conv_b_in', 'conv_dw_w', 'conv_dw_b', 'conv_ln_g', 'conv_ln_b', 'conv_w_out', 'conv_b_out', 'ffn_w_up', 'ffn_dw_w', 'ffn_dw_b', 'ffn_w_down']
TWIN_DIFF_INPUT = 'x'
TWIN_INPUTS = ['x', 'c', 'ada_w', 'ada_b', 'pre_mix_g', 'post_mix_g', 'pre_ffn_g', 'post_ffn_g', 'hgrn_w_in', 'hgrn_lb_logits', 'hgrn_gnorm_g', 'hgrn_w_out', 'conv_w_in', 'conv_b_in', 'conv_dw_w', 'conv_dw_b', 'conv_ln_g', 'conv_ln_b', 'conv_w_out', 'conv_b_out', 'ffn_w_up', 'ffn_dw_w', 'ffn_dw_b', 'ffn_w_down', 'loss_target', 'm_ada_w', 'm_ada_b', 'm_pre_mix_g', 'm_post_mix_g', 'm_pre_ffn_g', 'm_post_ffn_g', 'm_hgrn_w_in', 'm_hgrn_lb_logits', 'm_hgrn_gnorm_g', 'm_hgrn_w_out', 'm_conv_w_in', 'm_conv_b_in', 'm_conv_dw_w', 'm_conv_dw_b', 'm_conv_ln_g', 'm_conv_ln_b', 'm_conv_w_out', 'm_conv_b_out', 'm_ffn_w_up', 'm_ffn_dw_w', 'm_ffn_dw_b', 'm_ffn_w_down', 'v_ada_w', 'v_ada_b', 'v_pre_mix_g', 'v_post_mix_g', 'v_pre_ffn_g', 'v_post_ffn_g', 'v_hgrn_w_in', 'v_hgrn_lb_logits', 'v_hgrn_gnorm_g', 'v_hgrn_w_out', 'v_conv_w_in', 'v_conv_b_in', 'v_conv_dw_w', 'v_conv_dw_b', 'v_conv_ln_g', 'v_conv_ln_b', 'v_conv_w_out', 'v_conv_b_out', 'v_ffn_w_up', 'v_ffn_dw_w', 'v_ffn_dw_b', 'v_ffn_w_down']
TWIN_OUTPUTS = ['loss', 'grad_x', 'grad_ada_w', 'grad_ada_b', 'grad_pre_mix_g', 'grad_post_mix_g', 'grad_pre_ffn_g', 'grad_post_ffn_g', 'grad_hgrn_w_in', 'grad_hgrn_lb_logits', 'grad_hgrn_gnorm_g', 'grad_hgrn_w_out', 'grad_conv_w_in', 'grad_conv_b_in', 'grad_conv_dw_w', 'grad_conv_dw_b', 'grad_conv_ln_g', 'grad_conv_ln_b', 'grad_conv_w_out', 'grad_conv_b_out', 'grad_ffn_w_up', 'grad_ffn_dw_w', 'grad_ffn_dw_b', 'grad_ffn_w_down', 'delta_ada_w', 'delta_ada_b', 'delta_pre_mix_g', 'delta_post_mix_g', 'delta_pre_ffn_g', 'delta_post_ffn_g', 'delta_hgrn_w_in', 'delta_hgrn_lb_logits', 'delta_hgrn_gnorm_g', 'delta_hgrn_w_out', 'delta_conv_w_in', 'delta_conv_b_in', 'delta_conv_dw_w', 'delta_conv_dw_b', 'delta_conv_ln_g', 'delta_conv_ln_b', 'delta_conv_w_out', 'delta_conv_b_out', 'delta_ffn_w_up', 'delta_ffn_dw_w', 'delta_ffn_dw_b', 'delta_ffn_w_down', 'new_m_ada_w', 'new_m_ada_b', 'new_m_pre_mix_g', 'new_m_post_mix_g', 'new_m_pre_ffn_g', 'new_m_post_ffn_g', 'new_m_hgrn_w_in', 'new_m_hgrn_lb_logits', 'new_m_hgrn_gnorm_g', 'new_m_hgrn_w_out', 'new_m_conv_w_in', 'new_m_conv_b_in', 'new_m_conv_dw_w', 'new_m_conv_dw_b', 'new_m_conv_ln_g', 'new_m_conv_ln_b', 'new_m_conv_w_out', 'new_m_conv_b_out', 'new_m_ffn_w_up', 'new_m_ffn_dw_w', 'new_m_ffn_dw_b', 'new_m_ffn_w_down', 'new_v_ada_w', 'new_v_ada_b', 'new_v_pre_mix_g', 'new_v_post_mix_g', 'new_v_pre_ffn_g', 'new_v_post_ffn_g', 'new_v_hgrn_w_in', 'new_v_hgrn_lb_logits', 'new_v_hgrn_gnorm_g', 'new_v_hgrn_w_out', 'new_v_conv_w_in', 'new_v_conv_b_in', 'new_v_conv_dw_w', 'new_v_conv_dw_b', 'new_v_conv_ln_g', 'new_v_conv_ln_b', 'new_v_conv_w_out', 'new_v_conv_b_out', 'new_v_ffn_w_up', 'new_v_ffn_dw_w', 'new_v_ffn_dw_b', 'new_v_ffn_w_down']
TWIN_LEAF_KINDS = {'loss': 'loss', 'grad_x': 'grad_x', 'grad_ada_w': 'grad_w', 'grad_ada_b': 'grad_w', 'grad_pre_mix_g': 'grad_w', 'grad_post_mix_g': 'grad_w', 'grad_pre_ffn_g': 'grad_w', 'grad_post_ffn_g': 'grad_w', 'grad_hgrn_w_in': 'grad_w', 'grad_hgrn_lb_logits': 'grad_w', 'grad_hgrn_gnorm_g': 'grad_w', 'grad_hgrn_w_out': 'grad_w', 'grad_conv_w_in': 'grad_w', 'grad_conv_b_in': 'grad_w', 'grad_conv_dw_w': 'grad_w', 'grad_conv_dw_b': 'grad_w', 'grad_conv_ln_g': 'grad_w', 'grad_conv_ln_b': 'grad_w', 'grad_conv_w_out': 'grad_w', 'grad_conv_b_out': 'grad_w', 'grad_ffn_w_up': 'grad_w', 'grad_ffn_dw_w': 'grad_w', 'grad_ffn_dw_b': 'grad_w', 'grad_ffn_w_down': 'grad_w', 'delta_ada_w': 'delta_w', 'delta_ada_b': 'delta_w', 'delta_pre_mix_g': 'delta_w', 'delta_post_mix_g': 'delta_w', 'delta_pre_ffn_g': 'delta_w', 'delta_post_ffn_g': 'delta_w', 'delta_hgrn_w_in': 'delta_w', 'delta_hgrn_lb_logits': 'delta_w', 'delta_hgrn_gnorm_g': 'delta_w', 'delta_hgrn_w_out': 'delta_w', 'delta_conv_w_in': 'delta_w', 'delta_conv_b_in': 'delta_w', 'delta_conv_dw_w': 'delta_w', 'delta_conv_dw_b': 'delta_w', 'delta_conv_ln_g': 'delta_w', 'delta_conv_ln_b': 'delta_w', 'delta_conv_w_out': 'delta_w', 'delta_conv_b_out': 'delta_w', 'delta_ffn_w_up': 'delta_w', 'delta_ffn_dw_w': 'delta_w', 'delta_ffn_dw_b': 'delta_w', 'delta_ffn_w_down': 'delta_w', 'new_m_ada_w': 'new_m', 'new_m_ada_b': 'new_m', 'new_m_pre_mix_g': 'new_m', 'new_m_post_mix_g': 'new_m', 'new_m_pre_ffn_g': 'new_m', 'new_m_post_ffn_g': 'new_m', 'new_m_hgrn_w_in': 'new_m', 'new_m_hgrn_lb_logits': 'new_m', 'new_m_hgrn_gnorm_g': 'new_m', 'new_m_hgrn_w_out': 'new_m', 'new_m_conv_w_in': 'new_m', 'new_m_conv_b_in': 'new_m', 'new_m_conv_dw_w': 'new_m', 'new_m_conv_dw_b': 'new_m', 'new_m_conv_ln_g': 'new_m', 'new_m_conv_ln_b': 'new_m', 'new_m_conv_w_out': 'new_m', 'new_m_conv_b_out': 'new_m', 'new_m_ffn_w_up': 'new_m', 'new_m_ffn_dw_w': 'new_m', 'new_m_ffn_dw_b': 'new_m', 'new_m_ffn_w_down': 'new_m', 'new_v_ada_w': 'new_v', 'new_v_ada_b': 'new_v', 'new_v_pre_mix_g': 'new_v', 'new_v_post_mix_g': 'new_v', 'new_v_pre_ffn_g': 'new_v', 'new_v_post_ffn_g': 'new_v', 'new_v_hgrn_w_in': 'new_v', 'new_v_hgrn_lb_logits': 'new_v', 'new_v_hgrn_gnorm_g': 'new_v', 'new_v_hgrn_w_out': 'new_v', 'new_v_conv_w_in': 'new_v', 'new_v_conv_b_in': 'new_v', 'new_v_conv_dw_w': 'new_v', 'new_v_conv_dw_b': 'new_v', 'new_v_conv_ln_g': 'new_v', 'new_v_conv_ln_b': 'new_v', 'new_v_conv_w_out': 'new_v', 'new_v_conv_b_out': 'new_v', 'new_v_ffn_w_up': 'new_v', 'new_v_ffn_dw_w': 'new_v', 'new_v_ffn_dw_b': 'new_v', 'new_v_ffn_w_down': 'new_v'}


def _forward(args):
    return _fwd_reference(*[args[k] for k in FWD_PARAMS])


def _output_shape():
    def fwd():
        inp = _fwd_setup_inputs(0)
        return _fwd_reference(*[inp[k] for k in FWD_PARAMS])
    out = _jax.eval_shape(fwd)
    return out.shape, out.dtype

N_MICROBATCH = 1
ADAM_LR = 0.001
ADAM_B1 = 0.9
ADAM_B2 = 0.999
ADAM_EPS = 1e-08
ADAM_WD = 0.01
ADAM_STEP = 10
PER_EXAMPLE_BATCH_AXIS = {'x': 0, 'c': 0, 'loss_target': 0}
SHARED_INPUTS = []
_WEIGHT_DTYPES = {'ada_w': _jnp.float32, 'ada_b': _jnp.float32, 'pre_mix_g': _jnp.float32, 'post_mix_g': _jnp.float32, 'pre_ffn_g': _jnp.float32, 'post_ffn_g': _jnp.float32, 'hgrn_w_in': _jnp.float32, 'hgrn_lb_logits': _jnp.float32, 'hgrn_gnorm_g': _jnp.float32, 'hgrn_w_out': _jnp.float32, 'conv_w_in': _jnp.float32, 'conv_b_in': _jnp.float32, 'conv_dw_w': _jnp.float32, 'conv_dw_b': _jnp.float32, 'conv_ln_g': _jnp.float32, 'conv_ln_b': _jnp.float32, 'conv_w_out': _jnp.float32, 'conv_b_out': _jnp.float32, 'ffn_w_up': _jnp.float32, 'ffn_dw_w': _jnp.float32, 'ffn_dw_b': _jnp.float32, 'ffn_w_down': _jnp.float32}
MOMENT_SCALE = {'ada_w': 2.435662e+00, 'ada_b': 5.344472e+00, 'pre_mix_g': 1.662613e-01, 'post_mix_g': 6.701304e+00, 'pre_ffn_g': 1.762840e-01, 'post_ffn_g': 6.583096e+00, 'hgrn_w_in': 1.524265e-01, 'hgrn_lb_logits': 5.998458e-03, 'hgrn_gnorm_g': 5.812133e-01, 'hgrn_w_out': 2.550532e-01, 'conv_w_in': 1.688281e-01, 'conv_b_in': 6.450515e-01, 'conv_dw_w': 2.571855e-01, 'conv_dw_b': 1.579939e+00, 'conv_ln_g': 6.703760e-01, 'conv_ln_b': 1.019216e+00, 'conv_w_out': 4.205870e-01, 'conv_b_out': 1.996247e+00, 'ffn_w_up': 9.264977e-02, 'ffn_dw_w': 9.741245e-02, 'ffn_dw_b': 1.869318e-01, 'ffn_w_down': 1.773880e-01}


def _to_microbatches(a, axis):
    t = _jnp.moveaxis(a, axis, 0)
    t = t.reshape((N_MICROBATCH, t.shape[0] // N_MICROBATCH) + t.shape[1:])
    return _jnp.moveaxis(t, 1, axis + 1)


def setup_inputs(seed: int = 0) -> dict:
    inp = _fwd_setup_inputs(seed)
    key = _jax.random.fold_in(_jax.random.key(seed), 7919)
    shape, _ = _output_shape()
    out = dict(inp)
    out["loss_target"] = _jax.random.normal(_jax.random.fold_in(key, 0), shape, _jnp.float32)
    for i, name in enumerate(TWIN_WEIGHTS):
        w = inp[name].astype(_jnp.float32)
        if MOMENT_SCALE is None:
            s = _jnp.sqrt(_jnp.mean(_jnp.square(w)) + 1e-30)
        else:
            s = MOMENT_SCALE[name]
        km, kv = _jax.random.split(_jax.random.fold_in(key, i + 1))
        out[name] = w
        out["m_" + name] = s * _jax.random.normal(km, w.shape, _jnp.float32)
        out["v_" + name] = (s * s) * _jax.random.uniform(kv, w.shape, _jnp.float32, 0.5, 1.5)
    if N_MICROBATCH > 1:
        for name, axis in PER_EXAMPLE_BATCH_AXIS.items():
            out[name] = _to_microbatches(out[name], axis)
    return {'x': out['x'], 'c': out['c'], 'ada_w': out['ada_w'], 'ada_b': out['ada_b'], 'pre_mix_g': out['pre_mix_g'], 'post_mix_g': out['post_mix_g'], 'pre_ffn_g': out['pre_ffn_g'], 'post_ffn_g': out['post_ffn_g'], 'hgrn_w_in': out['hgrn_w_in'], 'hgrn_lb_logits': out['hgrn_lb_logits'], 'hgrn_gnorm_g': out['hgrn_gnorm_g'], 'hgrn_w_out': out['hgrn_w_out'], 'conv_w_in': out['conv_w_in'], 'conv_b_in': out['conv_b_in'], 'conv_dw_w': out['conv_dw_w'], 'conv_dw_b': out['conv_dw_b'], 'conv_ln_g': out['conv_ln_g'], 'conv_ln_b': out['conv_ln_b'], 'conv_w_out': out['conv_w_out'], 'conv_b_out': out['conv_b_out'], 'ffn_w_up': out['ffn_w_up'], 'ffn_dw_w': out['ffn_dw_w'], 'ffn_dw_b': out['ffn_dw_b'], 'ffn_w_down': out['ffn_w_down'], 'loss_target': out['loss_target'], 'm_ada_w': out['m_ada_w'], 'm_ada_b': out['m_ada_b'], 'm_pre_mix_g': out['m_pre_mix_g'], 'm_post_mix_g': out['m_post_mix_g'], 'm_pre_ffn_g': out['m_pre_ffn_g'], 'm_post_ffn_g': out['m_post_ffn_g'], 'm_hgrn_w_in': out['m_hgrn_w_in'], 'm_hgrn_lb_logits': out['m_hgrn_lb_logits'], 'm_hgrn_gnorm_g': out['m_hgrn_gnorm_g'], 'm_hgrn_w_out': out['m_hgrn_w_out'], 'm_conv_w_in': out['m_conv_w_in'], 'm_conv_b_in': out['m_conv_b_in'], 'm_conv_dw_w': out['m_conv_dw_w'], 'm_conv_dw_b': out['m_conv_dw_b'], 'm_conv_ln_g': out['m_conv_ln_g'], 'm_conv_ln_b': out['m_conv_ln_b'], 'm_conv_w_out': out['m_conv_w_out'], 'm_conv_b_out': out['m_conv_b_out'], 'm_ffn_w_up': out['m_ffn_w_up'], 'm_ffn_dw_w': out['m_ffn_dw_w'], 'm_ffn_dw_b': out['m_ffn_dw_b'], 'm_ffn_w_down': out['m_ffn_w_down'], 'v_ada_w': out['v_ada_w'], 'v_ada_b': out['v_ada_b'], 'v_pre_mix_g': out['v_pre_mix_g'], 'v_post_mix_g': out['v_post_mix_g'], 'v_pre_ffn_g': out['v_pre_ffn_g'], 'v_post_ffn_g': out['v_post_ffn_g'], 'v_hgrn_w_in': out['v_hgrn_w_in'], 'v_hgrn_lb_logits': out['v_hgrn_lb_logits'], 'v_hgrn_gnorm_g': out['v_hgrn_gnorm_g'], 'v_hgrn_w_out': out['v_hgrn_w_out'], 'v_conv_w_in': out['v_conv_w_in'], 'v_conv_b_in': out['v_conv_b_in'], 'v_conv_dw_w': out['v_conv_dw_w'], 'v_conv_dw_b': out['v_conv_dw_b'], 'v_conv_ln_g': out['v_conv_ln_g'], 'v_conv_ln_b': out['v_conv_ln_b'], 'v_conv_w_out': out['v_conv_w_out'], 'v_conv_b_out': out['v_conv_b_out'], 'v_ffn_w_up': out['v_ffn_w_up'], 'v_ffn_dw_w': out['v_ffn_dw_w'], 'v_ffn_dw_b': out['v_ffn_dw_b'], 'v_ffn_w_down': out['v_ffn_w_down']}


def _loss(weights, diff, rest, loss_target):
    with _jax.named_scope("forward"):
        args = {**rest, TWIN_DIFF_INPUT: diff, **{k: w.astype(_WEIGHT_DTYPES[k]) for k, w in weights.items()}}
        y = _forward(args)
    with _jax.named_scope("loss_head"):
        err = _jnp.square(y.astype(_jnp.float32) - loss_target)
        return 0.5 * _jnp.sum(_jnp.mean(err, axis=-1)) if err.ndim else 0.5 * err


def _adamw(w, g, m, v):
    m = ADAM_B1 * m + (1.0 - ADAM_B1) * g
    v = ADAM_B2 * v + (1.0 - ADAM_B2) * _jnp.square(g)
    m_hat = m / (1.0 - ADAM_B1 ** ADAM_STEP)
    v_hat = v / (1.0 - ADAM_B2 ** ADAM_STEP)
    delta = -ADAM_LR * (m_hat / (_jnp.sqrt(v_hat) + ADAM_EPS) + ADAM_WD * w)
    return delta, m, v


def reference(x, c, ada_w, ada_b, pre_mix_g, post_mix_g, pre_ffn_g, post_ffn_g, hgrn_w_in, hgrn_lb_logits, hgrn_gnorm_g, hgrn_w_out, conv_w_in, conv_b_in, conv_dw_w, conv_dw_b, conv_ln_g, conv_ln_b, conv_w_out, conv_b_out, ffn_w_up, ffn_dw_w, ffn_dw_b, ffn_w_down, loss_target, m_ada_w, m_ada_b, m_pre_mix_g, m_post_mix_g, m_pre_ffn_g, m_post_ffn_g, m_hgrn_w_in, m_hgrn_lb_logits, m_hgrn_gnorm_g, m_hgrn_w_out, m_conv_w_in, m_conv_b_in, m_conv_dw_w, m_conv_dw_b, m_conv_ln_g, m_conv_ln_b, m_conv_w_out, m_conv_b_out, m_ffn_w_up, m_ffn_dw_w, m_ffn_dw_b, m_ffn_w_down, v_ada_w, v_ada_b, v_pre_mix_g, v_post_mix_g, v_pre_ffn_g, v_post_ffn_g, v_hgrn_w_in, v_hgrn_lb_logits, v_hgrn_gnorm_g, v_hgrn_w_out, v_conv_w_in, v_conv_b_in, v_conv_dw_w, v_conv_dw_b, v_conv_ln_g, v_conv_ln_b, v_conv_w_out, v_conv_b_out, v_ffn_w_up, v_ffn_dw_w, v_ffn_dw_b, v_ffn_w_down):
    given = dict(x=x, c=c, ada_w=ada_w, ada_b=ada_b, pre_mix_g=pre_mix_g, post_mix_g=post_mix_g, pre_ffn_g=pre_ffn_g, post_ffn_g=post_ffn_g, hgrn_w_in=hgrn_w_in, hgrn_lb_logits=hgrn_lb_logits, hgrn_gnorm_g=hgrn_gnorm_g, hgrn_w_out=hgrn_w_out, conv_w_in=conv_w_in, conv_b_in=conv_b_in, conv_dw_w=conv_dw_w, conv_dw_b=conv_dw_b, conv_ln_g=conv_ln_g, conv_ln_b=conv_ln_b, conv_w_out=conv_w_out, conv_b_out=conv_b_out, ffn_w_up=ffn_w_up, ffn_dw_w=ffn_dw_w, ffn_dw_b=ffn_dw_b, ffn_w_down=ffn_w_down, loss_target=loss_target, m_ada_w=m_ada_w, m_ada_b=m_ada_b, m_pre_mix_g=m_pre_mix_g, m_post_mix_g=m_post_mix_g, m_pre_ffn_g=m_pre_ffn_g, m_post_ffn_g=m_post_ffn_g, m_hgrn_w_in=m_hgrn_w_in, m_hgrn_lb_logits=m_hgrn_lb_logits, m_hgrn_gnorm_g=m_hgrn_gnorm_g, m_hgrn_w_out=m_hgrn_w_out, m_conv_w_in=m_conv_w_in, m_conv_b_in=m_conv_b_in, m_conv_dw_w=m_conv_dw_w, m_conv_dw_b=m_conv_dw_b, m_conv_ln_g=m_conv_ln_g, m_conv_ln_b=m_conv_ln_b, m_conv_w_out=m_conv_w_out, m_conv_b_out=m_conv_b_out, m_ffn_w_up=m_ffn_w_up, m_ffn_dw_w=m_ffn_dw_w, m_ffn_dw_b=m_ffn_dw_b, m_ffn_w_down=m_ffn_w_down, v_ada_w=v_ada_w, v_ada_b=v_ada_b, v_pre_mix_g=v_pre_mix_g, v_post_mix_g=v_post_mix_g, v_pre_ffn_g=v_pre_ffn_g, v_post_ffn_g=v_post_ffn_g, v_hgrn_w_in=v_hgrn_w_in, v_hgrn_lb_logits=v_hgrn_lb_logits, v_hgrn_gnorm_g=v_hgrn_gnorm_g, v_hgrn_w_out=v_hgrn_w_out, v_conv_w_in=v_conv_w_in, v_conv_b_in=v_conv_b_in, v_conv_dw_w=v_conv_dw_w, v_conv_dw_b=v_conv_dw_b, v_conv_ln_g=v_conv_ln_g, v_conv_ln_b=v_conv_ln_b, v_conv_w_out=v_conv_w_out, v_conv_b_out=v_conv_b_out, v_ffn_w_up=v_ffn_w_up, v_ffn_dw_w=v_ffn_dw_w, v_ffn_dw_b=v_ffn_dw_b, v_ffn_w_down=v_ffn_w_down)
    weights = {n: given[n] for n in TWIN_WEIGHTS}
    shared = {n: given[n] for n in SHARED_INPUTS}
    per_example = {n: given[n] for n in ['x', 'c']}
    grad_fn = _jax.value_and_grad(_loss, argnums=(0, 1))

    def one_microbatch(ex, loss_target):
        ex = dict(ex)
        diff = ex.pop(TWIN_DIFF_INPUT)
        return grad_fn(weights, diff, {**shared, **ex}, loss_target)

    if N_MICROBATCH == 1:
        loss, (grad_w, grad_x) = one_microbatch(per_example, given["loss_target"])
    else:
        def body(carry, xs):
            loss_sum, grad_sum = carry
            l_k, (gw_k, gx_k) = one_microbatch(xs[0], xs[1])
            with _jax.named_scope("update"):
                return (loss_sum + l_k, _jax.tree.map(_jnp.add, grad_sum, gw_k)), gx_k

        init = (_jnp.zeros((), _jnp.float32), _jax.tree.map(_jnp.zeros_like, weights))
        (loss, grad_w), grad_x = _jax.lax.scan(body, init, (per_example, given["loss_target"]))
    with _jax.named_scope("update"):
        delta_w, new_m, new_v = {}, {}, {}
        for n in TWIN_WEIGHTS:
            delta_w[n], new_m[n], new_v[n] = _adamw(weights[n], grad_w[n], given["m_" + n], given["v_" + n])
    return (loss, grad_x, *[grad_w[n] for n in TWIN_WEIGHTS], *[delta_w[n] for n in TWIN_WEIGHTS],
            *[new_m[n] for n in TWIN_WEIGHTS], *[new_v[n] for n in TWIN_WEIGHTS])
```

```python
import jax
import jax.numpy as jnp
from jax import lax
from jax.experimental import pallas as pl
from jax.experimental.pallas import tpu as pltpu

F32 = jnp.float32
BF16 = jnp.bfloat16
EPS = 1e-6
HEAD = 128
BLK = 16
BLK_SHIFT = 4
NEG = -1e30
CONV_W = 31
FFN_W = 3
N_CHIPS = 4
N_DEV = 8
V7X_VMEM_LIMIT = 56 * 1024 * 1024
MESH = pl.DeviceIdType.MESH
HBM = pl.BlockSpec(memory_space=pltpu.HBM)
VMEM_SPEC = pl.BlockSpec(memory_space=pltpu.VMEM)

ADAM_LR = 0.001
ADAM_B1 = 0.9
ADAM_B2 = 0.999
ADAM_EPS = 1e-08
ADAM_WD = 0.01
ADAM_STEP = 10


def _cp(*sem):
    return pltpu.CompilerParams(dimension_semantics=sem, vmem_limit_bytes=V7X_VMEM_LIMIT)


def _sig(x):
    return 1.0 / (1.0 + jnp.exp(-x))


def _silu(x):
    return x * _sig(x)


def _dsilu(x):
    s = _sig(x)
    return s * (1.0 + x * (1.0 - s))


def _dot(a, b):
    return jnp.dot(a, b, preferred_element_type=F32)


def _dot_nt(a, b):
    return lax.dot_general(a, b, (((1,), (1,)), ((), ())), preferred_element_type=F32)


def _dot_tn(a, b):
    return lax.dot_general(a, b, (((0,), (0,)), ((), ())), preferred_element_type=F32)


def _colsum(x):
    return jnp.sum(x, axis=0, keepdims=True)


def _rowmean(x):
    return jnp.mean(x, axis=-1, keepdims=True)


def _ffn_perm(j):
    return (j % 2) * 2 + j // 2


def mm_nn(a, w, *, name, bias=None, out_dtype=F32, perm=None, tm=512):
    T, K = a.shape
    J, _, nb = w.shape
    tm = min(tm, T)
    col = (lambda j: j) if perm is None else perm

    def body(a_ref, w_ref, *rest):
        acc = _dot(a_ref[...], w_ref[...])
        if bias is not None:
            acc = acc + rest[0][...]
        rest[-1][...] = acc.astype(out_dtype)

    in_specs = [pl.BlockSpec((tm, K), lambda j, i: (i, 0)), pl.BlockSpec((None, K, nb), lambda j, i: (j, 0, 0))]
    args = [a, w]
    if bias is not None:
        in_specs.append(pl.BlockSpec((1, nb), lambda j, i: (0, j)))
        args.append(bias)
    return pl.pallas_call(
        body, grid=(J, T // tm), in_specs=in_specs,
        out_specs=pl.BlockSpec((tm, nb), lambda j, i: (i, col(j))),
        out_shape=jax.ShapeDtypeStruct((T, J * nb), out_dtype), name=name,
        compiler_params=_cp("parallel", "parallel"))(*args)


def mm_nt(a, w, *, name, out_dtype=F32, perm=None, tm=512):
    T = a.shape[0]
    J, K, nb = w.shape
    tm = min(tm, T)
    col = (lambda j: j) if perm is None else perm

    def body(a_ref, w_ref, o_ref, acc_ref):
        j = pl.program_id(1)

        @pl.when(j == 0)
        def _():
            acc_ref[...] = jnp.zeros_like(acc_ref)

        acc_ref[...] += _dot_nt(a_ref[...], w_ref[...])

        @pl.when(j == J - 1)
        def _():
            o_ref[...] = acc_ref[...].astype(out_dtype)

    return pl.pallas_call(
        body, grid=(T // tm, J),
        in_specs=[pl.BlockSpec((tm, nb), lambda i, j: (i, col(j))), pl.BlockSpec((None, K, nb), lambda i, j: (j, 0, 0))],
        out_specs=pl.BlockSpec((tm, K), lambda i, j: (i, 0)),
        out_shape=jax.ShapeDtypeStruct((T, K), out_dtype),
        scratch_shapes=[pltpu.VMEM((tm, K), F32)], name=name,
        compiler_params=_cp("parallel", "arbitrary"))(a, w)


def mm_tn(a, b, *, name, J, block, chips_per_block, perm=None, tk=512):
    T = a.shape[0]
    tk = min(tk, T)
    col = (lambda j: j) if perm is None else perm
    if block == "b":
        rows, nb = a.shape[1], b.shape[1] // J
        a_spec = pl.BlockSpec((tk, rows), lambda j, t: (t, 0))
        b_spec = pl.BlockSpec((tk, nb), lambda j, t: (t, col(j)))
    else:
        rows, nb = a.shape[1] // J, b.shape[1]
        a_spec = pl.BlockSpec((tk, rows), lambda j, t: (t, col(j)))
        b_spec = pl.BlockSpec((tk, nb), lambda j, t: (t, 0))
    cpb = chips_per_block
    rh = rows // (2 * cpb)

    def body(a_ref, b_ref, o_ref):
        @pl.when(pl.program_id(1) == 0)
        def _():
            o_ref[...] = jnp.zeros_like(o_ref)

        acc = _dot_tn(a_ref[...], b_ref[...])
        for ch in range(cpb):
            for hf in range(2):
                r0 = (ch * 2 + hf) * rh
                o_ref[hf, ch] += acc[r0:r0 + rh, :]

    return pl.pallas_call(
        body, grid=(J, T // tk), in_specs=[a_spec, b_spec],
        out_specs=pl.BlockSpec((2, cpb, rh, nb), lambda j, t: (0, j, 0, 0)),
        out_shape=jax.ShapeDtypeStruct((2, J * cpb, rh, nb), F32), name=name,
        compiler_params=_cp("parallel", "arbitrary"))(a, b)


def _row(tm, w):
    return pl.BlockSpec((tm, w), lambda i: (i, 0))


def _full(r, w):
    return pl.BlockSpec((r, w), lambda i: (0, 0))


def _acc_init(i, *refs):
    @pl.when(i == 0)
    def _():
        for r in refs:
            r[...] = jnp.zeros_like(r)


def prenorm(x, g, sc, sh, *, name, tm=512):
    T, D = x.shape
    tm = min(tm, T)

    def body(x_ref, g_ref, sc_ref, sh_ref, h_ref):
        xv = x_ref[...]
        r = lax.rsqrt(_rowmean(xv * xv) + EPS)
        h_ref[...] = ((xv * r) * g_ref[...] * (1.0 + sc_ref[...]) + sh_ref[...]).astype(BF16)

    return pl.pallas_call(
        body, grid=(T // tm,), in_specs=[_row(tm, D), _full(1, D), _full(1, D), _full(1, D)],
        out_specs=_row(tm, D), out_shape=jax.ShapeDtypeStruct((T, D), BF16), name=name,
        compiler_params=_cp("parallel"))(x, g, sc, sh)


def post_residual(x, y, g, gate, *, name, tm=512):
    T, D = x.shape
    tm = min(tm, T)

    def body(x_ref, y_ref, g_ref, gate_ref, o_ref):
        yv = y_ref[...]
        r = lax.rsqrt(_rowmean(yv * yv) + EPS)
        o_ref[...] = x_ref[...] + gate_ref[...] * ((yv * r) * g_ref[...])

    return pl.pallas_call(
        body, grid=(T // tm,), in_specs=[_row(tm, D), _row(tm, D), _full(1, D), _full(1, D)],
        out_specs=_row(tm, D), out_shape=jax.ShapeDtypeStruct((T, D), F32), name=name,
        compiler_params=_cp("parallel"))(x, y, g, gate)


def loss_grad(x, tgt, *, name, tm=512):
    T, D = x.shape
    tm = min(tm, T)

    def body(x_ref, t_ref, dx_ref, l_ref):
        _acc_init(pl.program_id(0), l_ref)
        e = x_ref[...] - t_ref[...]
        dx_ref[...] = e * (1.0 / D)
        l_ref[...] += _colsum(e * e)

    return pl.pallas_call(
        body, grid=(T // tm,), in_specs=[_row(tm, D), _row(tm, D)],
        out_specs=[_row(tm, D), _full(1, D)],
        out_shape=[jax.ShapeDtypeStruct((T, D), F32), jax.ShapeDtypeStruct((1, D), F32)], name=name,
        compiler_params=_cp("arbitrary"))(x, tgt)


def post_bwd(dx, y, g, gate, *, name, tm=512):
    T, D = dx.shape
    tm = min(tm, T)

    def body(dx_ref, y_ref, g_ref, gate_ref, dy_ref, dgate_ref, dg_ref, dbias_ref):
        _acc_init(pl.program_id(0), dgate_ref, dg_ref, dbias_ref)
        yv = y_ref[...]
        dxv = dx_ref[...]
        r = lax.rsqrt(_rowmean(yv * yv) + EPS)
        yn = yv * r
        gv = g_ref[...]
        gt = gate_ref[...]
        dgate_ref[...] += _colsum(dxv * (yn * gv))
        dg_ref[...] += _colsum(dxv * gt * yn)
        dyn = dxv * gt * gv
        dy = r * (dyn - yn * _rowmean(dyn * yn))
        dbias_ref[...] += _colsum(dy)
        dy_ref[...] = dy.astype(BF16)

    return pl.pallas_call(
        body, grid=(T // tm,), in_specs=[_row(tm, D), _row(tm, D), _full(1, D), _full(1, D)],
        out_specs=[_row(tm, D), _full(1, D), _full(1, D), _full(1, D)],
        out_shape=[jax.ShapeDtypeStruct((T, D), BF16)] + [jax.ShapeDtypeStruct((1, D), F32)] * 3, name=name,
        compiler_params=_cp("arbitrary"))(dx, y, g, gate)


def prenorm_bwd(dh, x, dres, g, sc, *, name, tm=512):
    T, D = x.shape
    tm = min(tm, T)

    def body(dh_ref, x_ref, dres_ref, g_ref, sc_ref, dx_ref, dsh_ref, dsc_ref, dg_ref):
        _acc_init(pl.program_id(0), dsh_ref, dsc_ref, dg_ref)
        xv = x_ref[...]
        dhv = dh_ref[...]
        r = lax.rsqrt(_rowmean(xv * xv) + EPS)
        xn = xv * r
        gv = g_ref[...]
        one_sc = 1.0 + sc_ref[...]
        dsh_ref[...] += _colsum(dhv)
        dsc_ref[...] += _colsum(dhv * (xn * gv))
        dg_ref[...] += _colsum(dhv * one_sc * xn)
        dxn = dhv * one_sc * gv
        dx_ref[...] = dres_ref[...] + r * (dxn - xn * _rowmean(dxn * xn))

    return pl.pallas_call(
        body, grid=(T // tm,), in_specs=[_row(tm, D), _row(tm, D), _row(tm, D), _full(1, D), _full(1, D)],
        out_specs=[_row(tm, D), _full(1, D), _full(1, D), _full(1, D)],
        out_shape=[jax.ShapeDtypeStruct((T, D), F32)] + [jax.ShapeDtypeStruct((1, D), F32)] * 3, name=name,
        compiler_params=_cp("arbitrary"))(dh, x, dres, g, sc)


HALO = 16


def ffn_act(u0p, dw_w, dw_b, *, name, tm=256):
    T, W = u0p.shape
    nb = W // 4
    tm = min(tm, T)
    hb = tm // HALO

    def body(u_ref, halo_ref, wa_ref, wb_ref, ba_ref, bb_ref, z_ref, buf):
        i = pl.program_id(1)
        buf[0:HALO, :] = jnp.where(i == 0, 0.0, halo_ref[...].astype(F32))
        buf[HALO:HALO + tm, :] = u_ref[...].astype(F32)

        def conv(w_ref, b_ref, c0):
            acc = b_ref[...] + w_ref[0:1, :] * buf[HALO - 2:HALO - 2 + tm, c0:c0 + nb]
            acc = acc + w_ref[1:2, :] * buf[HALO - 1:HALO - 1 + tm, c0:c0 + nb]
            return acc + w_ref[2:3, :] * buf[HALO:HALO + tm, c0:c0 + nb]

        a = conv(wa_ref, ba_ref, 0)
        b = conv(wb_ref, bb_ref, nb)
        z_ref[...] = (_silu(a) * b).astype(BF16)

    return pl.pallas_call(
        body, grid=(2, T // tm),
        in_specs=[pl.BlockSpec((tm, 2 * nb), lambda jc, i: (i, jc)),
                  pl.BlockSpec((HALO, 2 * nb), lambda jc, i: (jnp.maximum(i * hb - 1, 0), jc)),
                  pl.BlockSpec((FFN_W, nb), lambda jc, i: (0, jc)),
                  pl.BlockSpec((FFN_W, nb), lambda jc, i: (0, jc + 2)),
                  pl.BlockSpec((1, nb), lambda jc, i: (0, jc)),
                  pl.BlockSpec((1, nb), lambda jc, i: (0, jc + 2))],
        out_specs=pl.BlockSpec((tm, nb), lambda jc, i: (i, jc)),
        out_shape=jax.ShapeDtypeStruct((T, 2 * nb), BF16),
        scratch_shapes=[pltpu.VMEM((tm + HALO, 2 * nb), F32)], name=name,
        compiler_params=_cp("parallel", "arbitrary"))(u0p, u0p, dw_w, dw_w, dw_b, dw_b)


def ffn_act_bwd(dz, u0p, dw_w, dw_b, *, name, tm=256):
    T, W = u0p.shape
    nb = W // 4
    tm = min(tm, T)
    nt = T // tm
    hb = tm // HALO

    def body(dz_ref, u_ref, halo_ref, wa_ref, wb_ref, ba_ref, bb_ref, du0_ref, dw_ref, buf, dbuf, carry):
        i = pl.program_id(1)
        _acc_init(i, dw_ref)
        first_tile = i == nt - 1
        buf[0:HALO, :] = jnp.where(first_tile, 0.0, halo_ref[...].astype(F32))
        buf[HALO:HALO + tm, :] = u_ref[...].astype(F32)
        dzv = dz_ref[...]

        def conv(w_ref, b_ref, c0):
            acc = b_ref[...] + w_ref[0:1, :] * buf[HALO - 2:HALO - 2 + tm, c0:c0 + nb]
            acc = acc + w_ref[1:2, :] * buf[HALO - 1:HALO - 1 + tm, c0:c0 + nb]
            return acc + w_ref[2:3, :] * buf[HALO:HALO + tm, c0:c0 + nb]

        a = conv(wa_ref, ba_ref, 0)
        b = conv(wb_ref, bb_ref, nb)
        sa = _sig(a)
        dbuf[0:tm, 0:nb] = dzv * b * (sa * (1.0 + a * (1.0 - sa)))
        dbuf[0:tm, nb:2 * nb] = dzv * (a * sa)
        dbuf[tm:tm + HALO, :] = jnp.where(i == 0, 0.0, carry[...])
        du = dbuf[0:tm, :]
        carry[...] = dbuf[0:HALO, :]
        dw_ref[3:4, :] += _colsum(du)
        for k in range(FFN_W):
            dw_ref[k:k + 1, :] += _colsum(du * buf[HALO - 2 + k:HALO - 2 + k + tm, :])
        for w_ref, c0 in ((wa_ref, 0), (wb_ref, nb)):
            acc = w_ref[2:3, :] * dbuf[0:tm, c0:c0 + nb]
            acc = acc + w_ref[1:2, :] * dbuf[1:1 + tm, c0:c0 + nb]
            acc = acc + w_ref[0:1, :] * dbuf[2:2 + tm, c0:c0 + nb]
            du0_ref[:, c0:c0 + nb] = acc.astype(BF16)

    rev = lambda i: nt - 1 - i
    return pl.pallas_call(
        body, grid=(2, nt),
        in_specs=[pl.BlockSpec((tm, nb), lambda jc, i: (rev(i), jc)),
                  pl.BlockSpec((tm, 2 * nb), lambda jc, i: (rev(i), jc)),
                  pl.BlockSpec((HALO, 2 * nb), lambda jc, i: (jnp.maximum(rev(i) * hb - 1, 0), jc)),
                  pl.BlockSpec((FFN_W, nb), lambda jc, i: (0, jc)),
                  pl.BlockSpec((FFN_W, nb), lambda jc, i: (0, jc + 2)),
                  pl.BlockSpec((1, nb), lambda jc, i: (0, jc)),
                  pl.BlockSpec((1, nb), lambda jc, i: (0, jc + 2))],
        out_specs=[pl.BlockSpec((tm, 2 * nb), lambda jc, i: (rev(i), jc)),
                   pl.BlockSpec((8, 2 * nb), lambda jc, i: (0, jc))],
        out_shape=[jax.ShapeDtypeStruct((T, W), BF16), jax.ShapeDtypeStruct((8, W), F32)],
        scratch_shapes=[pltpu.VMEM((tm + HALO, 2 * nb), F32), pltpu.VMEM((tm + HALO, 2 * nb), F32),
                        pltpu.VMEM((HALO, 2 * nb), F32)], name=name,
        compiler_params=_cp("parallel", "arbitrary"))(dz, u0p, u0p, dw_w, dw_w, dw_b, dw_b)


CHALO = 32
CCOL = 256


def conv_act(u, dw_w, dw_b, ln_g, ln_b, *, name, tm=128):
    T, D2 = u.shape
    D = D2 // 2
    tm = min(tm, T)
    hb = tm // CHALO

    def body(u_ref, halo_ref, w_ref, b_ref, g_ref, be_ref, s_ref, cv_ref, gbuf):
        i = pl.program_id(0)
        hv = halo_ref[...]
        gbuf[0:CHALO, :] = jnp.where(i == 0, 0.0, hv[:, 0:D] * _sig(hv[:, D:D2]))
        uv = u_ref[...]
        gbuf[CHALO:CHALO + tm, :] = uv[:, 0:D] * _sig(uv[:, D:D2])
        for c0 in range(0, D, CCOL):
            acc = jnp.zeros((tm, CCOL), F32) + b_ref[:, c0:c0 + CCOL]
            for k in range(CONV_W):
                r = CHALO - (CONV_W - 1) + k
                acc = acc + w_ref[k:k + 1, c0:c0 + CCOL] * gbuf[r:r + tm, c0:c0 + CCOL]
            cv_ref[:, c0:c0 + CCOL] = acc
        cv = cv_ref[...]
        mu = _rowmean(cv)
        xc = cv - mu
        nh = xc * lax.rsqrt(_rowmean(xc * xc) + EPS)
        s_ref[...] = _silu(nh * g_ref[...] + be_ref[...]).astype(BF16)

    return pl.pallas_call(
        body, grid=(T // tm,),
        in_specs=[_row(tm, D2), pl.BlockSpec((CHALO, D2), lambda i: (jnp.maximum(i * hb - 1, 0), 0)),
                  _full(CONV_W, D), _full(1, D), _full(1, D), _full(1, D)],
        out_specs=[_row(tm, D), _row(tm, D)],
        out_shape=[jax.ShapeDtypeStruct((T, D), BF16), jax.ShapeDtypeStruct((T, D), F32)],
        scratch_shapes=[pltpu.VMEM((tm + CHALO, D), F32)], name=name,
        compiler_params=_cp("arbitrary"))(u, u, dw_w, dw_b, ln_g, ln_b)


def conv_norm_bwd(ds, cv, ln_g, ln_b, *, name, tm=512):
    T, D = cv.shape
    tm = min(tm, T)

    def body(ds_ref, cv_ref, g_ref, be_ref, dcv_ref, dg_ref, dbe_ref, dcb_ref):
        _acc_init(pl.program_id(0), dg_ref, dbe_ref, dcb_ref)
        cv_ = cv_ref[...]
        mu = _rowmean(cv_)
        xc = cv_ - mu
        rstd = lax.rsqrt(_rowmean(xc * xc) + EPS)
        nh = xc * rstd
        gv = g_ref[...]
        dln = ds_ref[...] * _dsilu(nh * gv + be_ref[...])
        dg_ref[...] += _colsum(dln * nh)
        dbe_ref[...] += _colsum(dln)
        dnh = dln * gv
        dcv = rstd * (dnh - _rowmean(dnh) - nh * _rowmean(dnh * nh))
        dcb_ref[...] += _colsum(dcv)
        dcv_ref[...] = dcv

    return pl.pallas_call(
        body, grid=(T // tm,), in_specs=[_row(tm, D), _row(tm, D), _full(1, D), _full(1, D)],
        out_specs=[_row(tm, D), _full(1, D), _full(1, D), _full(1, D)],
        out_shape=[jax.ShapeDtypeStruct((T, D), F32)] + [jax.ShapeDtypeStruct((1, D), F32)] * 3, name=name,
        compiler_params=_cp("arbitrary"))(ds, cv, ln_g, ln_b)


def conv_glu_bwd(dcv, u, dw_w, *, name, tm=128):
    T, D2 = u.shape
    D = D2 // 2
    tm = min(tm, T)
    nt = T // tm
    hb = tm // CHALO

    def body(dcv_ref, dnext_ref, u_ref, halo_ref, w_ref, du_ref, dw_ref, dbin_ref, gbuf, dbuf):
        i = pl.program_id(0)
        _acc_init(i, dw_ref, dbin_ref)
        hv = halo_ref[...]
        gbuf[0:CHALO, :] = jnp.where(i == 0, 0.0, hv[:, 0:D] * _sig(hv[:, D:D2]))
        uv = u_ref[...]
        av = uv[:, 0:D]
        sg = _sig(uv[:, D:D2])
        gbuf[CHALO:CHALO + tm, :] = av * sg
        dbuf[0:tm, :] = dcv_ref[...]
        dbuf[tm:tm + CHALO, :] = jnp.where(i == nt - 1, 0.0, dnext_ref[...])
        for c0 in range(0, D, CCOL):
            dc = dbuf[0:tm, c0:c0 + CCOL]
            acc = jnp.zeros((tm, CCOL), F32)
            for k in range(CONV_W):
                r = CHALO - (CONV_W - 1) + k
                dw_ref[k:k + 1, c0:c0 + CCOL] += _colsum(dc * gbuf[r:r + tm, c0:c0 + CCOL])
                rr = CONV_W - 1 - k
                acc = acc + w_ref[k:k + 1, c0:c0 + CCOL] * dbuf[rr:rr + tm, c0:c0 + CCOL]
            a_c = av[:, c0:c0 + CCOL]
            s_c = sg[:, c0:c0 + CCOL]
            da = acc * s_c
            dgt = acc * a_c * s_c * (1.0 - s_c)
            dbin_ref[:, c0:c0 + CCOL] += _colsum(da)
            dbin_ref[:, D + c0:D + c0 + CCOL] += _colsum(dgt)
            du_ref[:, c0:c0 + CCOL] = da.astype(BF16)
            du_ref[:, D + c0:D + c0 + CCOL] = dgt.astype(BF16)

    return pl.pallas_call(
        body, grid=(nt,),
        in_specs=[_row(tm, D), pl.BlockSpec((CHALO, D), lambda i: (jnp.minimum((i + 1) * hb, T // CHALO - 1), 0)),
                  _row(tm, D2), pl.BlockSpec((CHALO, D2), lambda i: (jnp.maximum(i * hb - 1, 0), 0)),
                  _full(CONV_W, D)],
        out_specs=[_row(tm, D2), _full(CHALO, D), _full(1, D2)],
        out_shape=[jax.ShapeDtypeStruct((T, D2), BF16), jax.ShapeDtypeStruct((CHALO, D), F32),
                   jax.ShapeDtypeStruct((1, D2), F32)],
        scratch_shapes=[pltpu.VMEM((tm + CHALO, D), F32), pltpu.VMEM((tm + CHALO, D), F32)], name=name,
        compiler_params=_cp("arbitrary"))(dcv, dcv, u, u, dw_w)


def _lb0(lg_ref):
    l0, l1, l2 = lg_ref[0:1, :], lg_ref[1:2, :], lg_ref[2:3, :]
    m = jnp.maximum(jnp.maximum(l0, l1), l2)
    e0 = jnp.exp(l0 - m)
    return e0 / (e0 + jnp.exp(l1 - m) + jnp.exp(l2 - m))


def _block_tri(n, lower):
    r = lax.broadcasted_iota(jnp.int32, (n, n), 0)
    c = lax.broadcasted_iota(jnp.int32, (n, n), 1)
    same = lax.shift_right_logical(r, BLK_SHIFT) == lax.shift_right_logical(c, BLK_SHIFT)
    tri = (c <= r) if lower else (c >= r)
    return jnp.where(same & tri, 1.0, 0.0).astype(BF16)


def _mm_exact(m01, x):
    hi = x.astype(BF16)
    r1 = x - hi.astype(F32)
    mid = r1.astype(BF16)
    lo = (r1 - mid.astype(F32)).astype(BF16)
    return _dot(m01, hi) + _dot(m01, mid) + _dot(m01, lo)


def _hgrn_specs(H, tm, idx):
    return [pl.BlockSpec((tm, HEAD), lambda h, i: (idx(i), h)),
            pl.BlockSpec((tm, HEAD), lambda h, i: (idx(i), H + h)),
            pl.BlockSpec((tm, HEAD), lambda h, i: (idx(i), 2 * H + h)),
            pl.BlockSpec((3, HEAD), lambda h, i: (0, h))]


def hgrn_fwd(proj, lb_logits, *, name, tm=128):
    T = proj.shape[0]
    H = proj.shape[1] // (4 * HEAD)
    tm = min(tm, T)
    nt = T // tm
    nblk = tm // BLK

    def body(qp_ref, fz_ref, v_ref, lg_ref, o_ref, st_ref, S_ref, q_s, k_s, b_s):
        @pl.when(pl.program_id(1) == 0)
        def _():
            S_ref[...] = jnp.zeros_like(S_ref)

        st_ref[...] = S_ref[...]
        lb = _lb0(lg_ref)
        f = lb + (1.0 - lb) * _sig(fz_ref[...])
        q_s[...] = _silu(qp_ref[...])
        k_s[...] = 1.0 - f
        b_s[...] = _mm_exact(_block_tri(tm, True), jnp.log(f))
        rows = lax.broadcasted_iota(jnp.int32, (BLK, HEAD), 0)
        S = S_ref[...]
        for nb in range(nblk):
            r0 = nb * BLK
            qb = q_s[r0:r0 + BLK, :]
            kb = k_s[r0:r0 + BLK, :]
            bb = b_s[r0:r0 + BLK, :]
            vb = v_ref[r0:r0 + BLK, :]
            o = _dot_nt((qb * jnp.exp(bb)).astype(BF16), S.astype(BF16))
            for s in range(BLK):
                r = r0 + s
                dec = jnp.exp(jnp.where(rows >= s, bb - b_s[r:r + 1, :], NEG))
                a = jnp.sum(qb * k_s[r:r + 1, :] * dec, axis=-1, keepdims=True)
                o = o + a * v_ref[r:r + 1, :]
            o_ref[r0:r0 + BLK, :] = o
            bc = b_s[r0 + BLK - 1:r0 + BLK, :]
            kd = kb * jnp.exp(bc - bb)
            S = S * jnp.exp(bc) + _dot_tn(vb.astype(BF16), kd.astype(BF16))
        S_ref[...] = S

    return pl.pallas_call(
        body, grid=(H, nt), in_specs=_hgrn_specs(H, tm, lambda i: i),
        out_specs=[pl.BlockSpec((tm, HEAD), lambda h, i: (i, h)),
                   pl.BlockSpec((None, None, HEAD, HEAD), lambda h, i: (i, h, 0, 0))],
        out_shape=[jax.ShapeDtypeStruct((T, H * HEAD), F32), jax.ShapeDtypeStruct((nt, H, HEAD, HEAD), F32)],
        scratch_shapes=[pltpu.VMEM((HEAD, HEAD), F32)] + [pltpu.VMEM((tm, HEAD), F32)] * 3, name=name,
        compiler_params=_cp("parallel", "arbitrary"))(proj, proj, proj, lb_logits)


def hgrn_bwd(proj, lb_logits, states, do, *, name, tm=128):
    T = proj.shape[0]
    H = proj.shape[1] // (4 * HEAD)
    tm = min(tm, T)
    nt = T // tm
    nblk = tm // BLK

    def body(qp_ref, fz_ref, v_ref, lg_ref, st_ref, do_ref, dqp_ref, dfz_ref, dv_ref, dlb_ref,
             dS_ref, Sb_ref, q_s, k_s, b_s, dq_s, dk_s, dv_s, db_s):
        i = pl.program_id(1)

        @pl.when(i == 0)
        def _():
            dS_ref[...] = jnp.zeros_like(dS_ref)
            dlb_ref[...] = jnp.zeros_like(dlb_ref)

        lb = _lb0(lg_ref)
        qp = qp_ref[...]
        sg = _sig(fz_ref[...])
        f = lb + (1.0 - lb) * sg
        q_s[...] = _silu(qp)
        k_s[...] = 1.0 - f
        b_s[...] = _mm_exact(_block_tri(tm, True), jnp.log(f))
        rows = lax.broadcasted_iota(jnp.int32, (BLK, HEAD), 0)
        rows1 = lax.broadcasted_iota(jnp.int32, (BLK, 1), 0)

        S = st_ref[...]
        for nb in range(nblk):
            r0 = nb * BLK
            Sb_ref[nb] = S
            if nb < nblk - 1:
                bb = b_s[r0:r0 + BLK, :]
                bc = b_s[r0 + BLK - 1:r0 + BLK, :]
                kd = k_s[r0:r0 + BLK, :] * jnp.exp(bc - bb)
                S = S * jnp.exp(bc) + _dot_tn(v_ref[r0:r0 + BLK, :].astype(BF16), kd.astype(BF16))

        dS = dS_ref[...]
        for nb in reversed(range(nblk)):
            r0 = nb * BLK
            S0 = Sb_ref[nb]
            qb = q_s[r0:r0 + BLK, :]
            kb = k_s[r0:r0 + BLK, :]
            bb = b_s[r0:r0 + BLK, :]
            vb = v_ref[r0:r0 + BLK, :]
            dob = do_ref[r0:r0 + BLK, :]
            bc = b_s[r0 + BLK - 1:r0 + BLK, :]
            eb = jnp.exp(bb)
            ekd = jnp.exp(bc - bb)
            ebc = jnp.exp(bc)
            qe = qb * eb
            kd = kb * ekd
            dS16 = dS.astype(BF16)
            dob16 = dob.astype(BF16)
            dq = _dot(dob16, S0.astype(BF16)) * eb
            dki = _dot(vb.astype(BF16), dS16) * ekd
            dk_s[r0:r0 + BLK, :] = dki
            dv_s[r0:r0 + BLK, :] = _dot_nt(kd.astype(BF16), dS16)
            dbc = _colsum(dS * S0) * ebc + _colsum(kb * dki)
            for s in range(BLK):
                r = r0 + s
                ks = k_s[r:r + 1, :]
                dec = jnp.exp(jnp.where(rows >= s, bb - b_s[r:r + 1, :], NEG))
                w = qb * dec
                a = jnp.sum(w * ks, axis=-1, keepdims=True)
                da = jnp.where(rows1 >= s, jnp.sum(dob * v_ref[r:r + 1, :], axis=-1, keepdims=True), 0.0)
                dq = dq + (da * ks) * dec
                dk_s[r:r + 1, :] += _colsum(da * w)
                dv_s[r:r + 1, :] += _colsum(a * dob)
            dq_s[r0:r0 + BLK, :] = dq
            db_s[r0:r0 + BLK, :] = qb * dq - kb * dk_s[r0:r0 + BLK, :]
            db_s[r0 + BLK - 1:r0 + BLK, :] += dbc
            dS = dS * ebc + _dot_tn(dob16, qe.astype(BF16))
        dS_ref[...] = dS

        dlf = _mm_exact(_block_tri(tm, False), db_s[...])
        df = dlf / f - dk_s[...]
        dfz_ref[...] = (df * (1.0 - lb) * sg * (1.0 - sg)).astype(BF16)
        dlb_ref[...] += _colsum(df * (1.0 - sg))
        dqp_ref[...] = (dq_s[...] * _dsilu(qp)).astype(BF16)
        dv_ref[...] = dv_s[...].astype(BF16)

    rev = lambda i: nt - 1 - i
    out_blk = pl.BlockSpec((tm, HEAD), lambda h, i: (rev(i), h))
    return pl.pallas_call(
        body, grid=(H, nt),
        in_specs=_hgrn_specs(H, tm, rev) + [pl.BlockSpec((None, None, HEAD, HEAD), lambda h, i: (rev(i), h, 0, 0)),
                                            pl.BlockSpec((tm, HEAD), lambda h, i: (rev(i), h))],
        out_specs=[out_blk, out_blk, out_blk, pl.BlockSpec((1, HEAD), lambda h, i: (0, h))],
        out_shape=[jax.ShapeDtypeStruct((T, H * HEAD), BF16)] * 3 + [jax.ShapeDtypeStruct((1, H * HEAD), F32)],
        scratch_shapes=[pltpu.VMEM((HEAD, HEAD), F32), pltpu.VMEM((nblk, HEAD, HEAD), F32)]
        + [pltpu.VMEM((tm, HEAD), F32)] * 7, name=name,
        compiler_params=_cp("parallel", "arbitrary"))(proj, proj, proj, lb_logits, states, do)


def hgrn_gate(o, proj, gn, *, name, tm=512):
    T, D = o.shape
    H = D // HEAD
    tm = min(tm, T)

    def body(o_ref, gp_ref, gn_ref, og_ref):
        gn_ = gn_ref[...]
        for h in range(H):
            c = slice(h * HEAD, (h + 1) * HEAD)
            oh = o_ref[:, c]
            r = lax.rsqrt(_rowmean(oh * oh) + EPS)
            og_ref[:, c] = ((oh * r) * gn_ * _silu(gp_ref[:, c])).astype(BF16)

    return pl.pallas_call(
        body, grid=(T // tm,),
        in_specs=[_row(tm, D), pl.BlockSpec((tm, D), lambda i: (i, 3)), _full(1, HEAD)],
        out_specs=_row(tm, D), out_shape=jax.ShapeDtypeStruct((T, D), BF16), name=name,
        compiler_params=_cp("parallel"))(o, proj, gn)


def hgrn_gate_bwd(dog, o, proj, gn, *, name, tm=512):
    T, D = o.shape
    H = D // HEAD
    tm = min(tm, T)

    def body(dog_ref, o_ref, gp_ref, gn_ref, do_ref, dgp_ref, dgn_ref):
        _acc_init(pl.program_id(0), dgn_ref)
        gn_ = gn_ref[...]
        for h in range(H):
            c = slice(h * HEAD, (h + 1) * HEAD)
            oh = o_ref[:, c]
            gp = gp_ref[:, c]
            dg = dog_ref[:, c]
            r = lax.rsqrt(_rowmean(oh * oh) + EPS)
            on = oh * r
            dgp_ref[:, c] = (dg * (on * gn_) * _dsilu(gp)).astype(BF16)
            don = dg * _silu(gp)
            dgn_ref[...] += _colsum(don * on)
            dn = don * gn_
            do_ref[:, c] = r * (dn - on * _rowmean(dn * on))

    return pl.pallas_call(
        body, grid=(T // tm,),
        in_specs=[_row(tm, D), _row(tm, D), pl.BlockSpec((tm, D), lambda i: (i, 3)), _full(1, HEAD)],
        out_specs=[_row(tm, D), _row(tm, D), _full(1, HEAD)],
        out_shape=[jax.ShapeDtypeStruct((T, D), F32), jax.ShapeDtypeStruct((T, D), BF16),
                   jax.ShapeDtypeStruct((1, HEAD), F32)], name=name,
        compiler_params=_cp("arbitrary"))(dog, o, proj, gn)


def _split2(x):
    hi = x.astype(BF16)
    return hi, (x - hi.astype(F32)).astype(BF16)


def ada_mod(c_all, ada_w, *, name):
    L, D, N = ada_w.shape
    B = c_all.shape[0]

    def body(c_ref, w_ref, o_ref):
        chi, clo = _split2(_silu(c_ref[...]))
        whi, wlo = _split2(w_ref[...])
        o_ref[...] = _dot(chi, whi) + _dot(chi, wlo) + _dot(clo, whi)

    return pl.pallas_call(
        body, grid=(L,), in_specs=[_full(B, D), pl.BlockSpec((None, D, N), lambda l: (l, 0, 0))],
        out_specs=pl.BlockSpec((None, B, N), lambda l: (l, 0, 0)),
        out_shape=jax.ShapeDtypeStruct((L, B, N), F32), name=name, compiler_params=_cp("parallel"))(c_all, ada_w)


def ada_wgrad(c_all_t, dmod, *, name, tr=256):
    D, B = c_all_t.shape
    L, _, N = dmod.shape
    tr = min(tr, D)

    def body(c_ref, d_ref, o_ref):
        cond = _silu(c_ref[...])
        acc = cond[:, 0:1] * d_ref[0:1, :]
        for b in range(1, B):
            acc = acc + cond[:, b:b + 1] * d_ref[b:b + 1, :]
        o_ref[...] = acc

    return pl.pallas_call(
        body, grid=(L, D // tr),
        in_specs=[pl.BlockSpec((tr, B), lambda l, r: (r, 0)), pl.BlockSpec((None, B, N), lambda l, r: (l, 0, 0))],
        out_specs=pl.BlockSpec((None, tr, N), lambda l, r: (l, r, 0)),
        out_shape=jax.ShapeDtypeStruct((L, D, N), F32), name=name,
        compiler_params=_cp("parallel", "parallel"))(c_all_t, dmod)


def sum_devices(parts, *, name):
    n, R, C = parts.shape

    def body(p_ref, o_ref):
        acc = p_ref[0]
        for d in range(1, n):
            acc = acc + p_ref[d]
        o_ref[...] = acc

    return pl.pallas_call(body, in_specs=[VMEM_SPEC], out_specs=VMEM_SPEC,
                          out_shape=jax.ShapeDtypeStruct((R, C), F32), name=name)(parts)


def lb_logits_grad(lb_logits, dlb, *, name):
    def body(lg_ref, d_ref, o_ref):
        l0, l1, l2 = lg_ref[0:1, :], lg_ref[1:2, :], lg_ref[2:3, :]
        m = jnp.maximum(jnp.maximum(l0, l1), l2)
        e0, e1, e2 = jnp.exp(l0 - m), jnp.exp(l1 - m), jnp.exp(l2 - m)
        z = e0 + e1 + e2
        p0, p1, p2 = e0 / z, e1 / z, e2 / z
        g = d_ref[...] * p0
        o_ref[0:1, :] = g * (1.0 - p0)
        o_ref[1:2, :] = -g * p1
        o_ref[2:3, :] = -g * p2

    return pl.pallas_call(body, in_specs=[VMEM_SPEC, VMEM_SPEC], out_specs=VMEM_SPEC,
                          out_shape=jax.ShapeDtypeStruct(lb_logits.shape, F32), name=name)(lb_logits, dlb)


def _tile(n, pref):
    if n <= pref:
        return n
    t = pref - pref % 8
    while n % t:
        t -= 8
    return t


def adamw(w, g, m, v, *, name, tr=256):
    R, C = w.shape
    tr = _tile(R, tr)

    def body(w_ref, g_ref, m_ref, v_ref, d_ref, nm_ref, nv_ref):
        gv = g_ref[...]
        nm = ADAM_B1 * m_ref[...] + (1.0 - ADAM_B1) * gv
        nv = ADAM_B2 * v_ref[...] + (1.0 - ADAM_B2) * (gv * gv)
        m_hat = nm / (1.0 - ADAM_B1 ** ADAM_STEP)
        v_hat = nv / (1.0 - ADAM_B2 ** ADAM_STEP)
        d_ref[...] = -ADAM_LR * (m_hat / (jnp.sqrt(v_hat) + ADAM_EPS) + ADAM_WD * w_ref[...])
        nm_ref[...] = nm
        nv_ref[...] = nv

    spec = pl.BlockSpec((tr, C), lambda i: (i, 0))
    return pl.pallas_call(
        body, grid=(R // tr,), in_specs=[spec] * 4, out_specs=[spec] * 3,
        out_shape=[jax.ShapeDtypeStruct((R, C), F32)] * 3, name=name, compiler_params=_cp("parallel"))(w, g, m, v)


def _place():
    return lax.axis_index("x"), lax.axis_index("y"), lax.axis_index("c")


def _flip(v, bit):
    return 1 - v if bit else v


def allgather_devices(v, *, name):
    R, C = v.shape

    def body(v_ref, out_ref, send_sems, recv_sems, local_sem):
        x, y, c = _place()
        me = 4 * x + 2 * y + c
        mine = pltpu.make_async_copy(v_ref, out_ref.at[me], local_sem)
        mine.start()
        sends = []
        for k in range(1, N_DEV):
            peer = (_flip(x, k & 4), _flip(y, k & 2), _flip(c, k & 1))
            cp = pltpu.make_async_remote_copy(src_ref=v_ref, dst_ref=out_ref.at[me], send_sem=send_sems.at[k - 1],
                                              recv_sem=recv_sems.at[k - 1], device_id=peer, device_id_type=MESH)
            cp.start()
            sends.append(cp)
        for k in range(1, N_DEV):
            px, py, pc = _flip(x, k & 4), _flip(y, k & 2), _flip(c, k & 1)
            pltpu.make_async_remote_copy(src_ref=v_ref, dst_ref=out_ref.at[4 * px + 2 * py + pc],
                                         send_sem=send_sems.at[k - 1], recv_sem=recv_sems.at[k - 1],
                                         device_id=(px, py, pc), device_id_type=MESH).wait_recv()
        for cp in sends:
            cp.wait_send()
        mine.wait()

    return pl.pallas_call(
        body, in_specs=[VMEM_SPEC], out_specs=VMEM_SPEC, out_shape=jax.ShapeDtypeStruct((N_DEV, R, C), v.dtype),
        scratch_shapes=[pltpu.SemaphoreType.DMA((N_DEV - 1,)), pltpu.SemaphoreType.DMA((N_DEV - 1,)),
                        pltpu.SemaphoreType.DMA], name=name)(v)


def _other_chips(x, y):
    return [(1 - x, y), (x, 1 - y), (1 - x, 1 - y)]


def allgather_chips(shards, *, name):
    n = len(shards)

    def body(*refs):
        ins, outs = refs[:n], refs[n:2 * n]
        send_sems, recv_sems, local_sems = refs[2 * n:]
        x, y, c = _place()
        q = 2 * x + y
        chips = _other_chips(x, y)
        local = [pltpu.make_async_copy(ins[a], outs[a].at[q], local_sems.at[a]) for a in range(n)]
        for cp in local:
            cp.start()

        def copy(a, k, block, half, to, src=None):
            dst = outs[a].at[block, half]
            return pltpu.make_async_remote_copy(src_ref=dst if src is None else src, dst_ref=dst,
                                                send_sem=send_sems.at[a, k], recv_sem=recv_sems.at[a, k],
                                                device_id=to, device_id_type=MESH)

        first = [copy(a, j, q, c, (*chips[j], c), src=ins[a].at[c]) for a in range(n) for j in range(3)]
        for cp in first:
            cp.start()
        passed = []
        for a in range(n):
            for j, (px, py) in enumerate(chips):
                copy(a, j, 2 * px + py, c, (x, y, c)).wait_recv()
                fw = copy(a, 3 + j, 2 * px + py, c, (x, y, 1 - c))
                fw.start()
                passed.append(fw)
        for a in range(n):
            for j, (px, py) in enumerate(chips):
                copy(a, 3 + j, 2 * px + py, 1 - c, (x, y, c)).wait_recv()
        for cp in first + passed:
            cp.wait_send()
        for cp in local:
            cp.wait()

    return pl.pallas_call(
        body, in_specs=[HBM] * n, out_specs=[HBM] * n,
        out_shape=[jax.ShapeDtypeStruct((N_CHIPS,) + s.shape, s.dtype) for s in shards],
        scratch_shapes=[pltpu.SemaphoreType.DMA((n, 6)), pltpu.SemaphoreType.DMA((n, 6)),
                        pltpu.SemaphoreType.DMA((n,))], name=name)(*shards)


def pair_exchange(grads, *, name):
    n = len(grads)

    def body(*refs):
        ins, outs = refs[:n], refs[n:2 * n]
        send_sems, recv_sems = refs[2 * n:]
        x, y, c = _place()
        cps = [pltpu.make_async_remote_copy(src_ref=ins[a].at[1 - c], dst_ref=outs[a], send_sem=send_sems.at[a],
                                            recv_sem=recv_sems.at[a], device_id=(x, y, 1 - c), device_id_type=MESH)
               for a in range(n)]
        for cp in cps:
            cp.start()
        for cp in cps:
            cp.wait_recv()
        for cp in cps:
            cp.wait_send()

    return pl.pallas_call(
        body, in_specs=[HBM] * n, out_specs=[HBM] * n,
        out_shape=[jax.ShapeDtypeStruct(g.shape[1:], g.dtype) for g in grads],
        scratch_shapes=[pltpu.SemaphoreType.DMA((n,)), pltpu.SemaphoreType.DMA((n,))], name=name)(*grads)


def pair_add(g, other, c_idx, *, name, tr=256):
    _, Q, R, C = g.shape
    tr = _tile(R, tr)

    def body(c_ref, g_ref, o_ref, out_ref):
        out_ref[...] = (g_ref[...] + o_ref[...]).astype(BF16)

    return pl.pallas_call(
        body,
        grid_spec=pltpu.PrefetchScalarGridSpec(
            num_scalar_prefetch=1, grid=(Q, R // tr),
            in_specs=[pl.BlockSpec((None, None, tr, C), lambda q, r, c_ref: (c_ref[0], q, r, 0)),
                      pl.BlockSpec((None, tr, C), lambda q, r, c_ref: (q, r, 0))],
            out_specs=pl.BlockSpec((None, tr, C), lambda q, r, c_ref: (q, r, 0))),
        out_shape=jax.ShapeDtypeStruct((Q, R, C), BF16), name=name,
        compiler_params=_cp("parallel", "parallel"))(c_idx, g, other)


def chip_exchange(sums, *, name):
    n = len(sums)

    def body(*refs):
        ins, outs = refs[:n], refs[n:2 * n]
        send_sems, recv_sems, local_sems = refs[2 * n:]
        x, y, c = _place()
        q = 2 * x + y
        chips = _other_chips(x, y)
        local = [pltpu.make_async_copy(ins[a].at[q], outs[a].at[q], local_sems.at[a]) for a in range(n)]
        for cp in local:
            cp.start()
        sends = [pltpu.make_async_remote_copy(src_ref=ins[a].at[2 * px + py], dst_ref=outs[a].at[q],
                                              send_sem=send_sems.at[a, j], recv_sem=recv_sems.at[a, j],
                                              device_id=(px, py, c), device_id_type=MESH)
                 for a in range(n) for j, (px, py) in enumerate(chips)]
        for cp in sends:
            cp.start()
        for a in range(n):
            for j, (px, py) in enumerate(chips):
                pltpu.make_async_remote_copy(src_ref=ins[a].at[q], dst_ref=outs[a].at[2 * px + py],
                                             send_sem=send_sems.at[a, j], recv_sem=recv_sems.at[a, j],
                                             device_id=(px, py, c), device_id_type=MESH).wait_recv()
        for cp in sends:
            cp.wait_send()
        for cp in local:
            cp.wait()

    return pl.pallas_call(
        body, in_specs=[HBM] * n, out_specs=[HBM] * n,
        out_shape=[jax.ShapeDtypeStruct(s.shape, s.dtype) for s in sums],
        scratch_shapes=[pltpu.SemaphoreType.DMA((n, 3)), pltpu.SemaphoreType.DMA((n, 3)),
                        pltpu.SemaphoreType.DMA((n,))], name=name)(*sums)


def chip_sum(parts, *, name, tr=256):
    Q, R, C = parts.shape
    tr = _tile(R, tr)

    def body(p_ref, o_ref):
        acc = p_ref[0].astype(F32)
        for k in range(1, Q):
            acc = acc + p_ref[k].astype(F32)
        o_ref[...] = acc

    return pl.pallas_call(
        body, grid=(R // tr,), in_specs=[pl.BlockSpec((Q, tr, C), lambda r: (0, r, 0))],
        out_specs=pl.BlockSpec((tr, C), lambda r: (r, 0)), out_shape=jax.ShapeDtypeStruct((R, C), F32), name=name,
        compiler_params=_cp("parallel"))(parts)


def half_swap(halves, *, name):
    n = len(halves)

    def body(*refs):
        ins, outs = refs[:n], refs[n:2 * n]
        send_sems, recv_sems, local_sems = refs[2 * n:]
        x, y, c = _place()
        local = [pltpu.make_async_copy(ins[a], outs[a].at[c], local_sems.at[a]) for a in range(n)]
        for cp in local:
            cp.start()
        cps = [pltpu.make_async_remote_copy(src_ref=ins[a], dst_ref=outs[a].at[c], send_sem=send_sems.at[a],
                                            recv_sem=recv_sems.at[a], device_id=(x, y, 1 - c), device_id_type=MESH)
               for a in range(n)]
        for cp in cps:
            cp.start()
        for a in range(n):
            pltpu.make_async_remote_copy(src_ref=ins[a], dst_ref=outs[a].at[1 - c], send_sem=send_sems.at[a],
                                         recv_sem=recv_sems.at[a], device_id=(x, y, 1 - c),
                                         device_id_type=MESH).wait_recv()
        for cp in cps:
            cp.wait_send()
        for cp in local:
            cp.wait()

    return pl.pallas_call(
        body, in_specs=[HBM] * n, out_specs=[HBM] * n,
        out_shape=[jax.ShapeDtypeStruct((2,) + h.shape, h.dtype) for h in halves],
        scratch_shapes=[pltpu.SemaphoreType.DMA((n,)), pltpu.SemaphoreType.DMA((n,)),
                        pltpu.SemaphoreType.DMA((n,))], name=name)(*halves)


def reduce_weight_grads(grads, c_idx):
    others = pair_exchange(grads, name="grad_pair_exchange")
    sums = [pair_add(g, o, c_idx, name=f"grad_pair_add_{a}") for a, (g, o) in enumerate(zip(grads, others))]
    landed = chip_exchange(sums, name="grad_chip_exchange")
    halves = [chip_sum(p, name=f"grad_chip_sum_{a}") for a, p in enumerate(landed)]
    full = half_swap(halves, name="grad_half_swap")
    return [f.reshape(2 * f.shape[1], f.shape[2]) for f in full]


def _ffn_forward(x, mod, pre_g, post_g, w_up, w_down, dw_w, dw_b, tag):
    sh, sc, gate = mod
    h = prenorm(x, pre_g, sc, sh, name=f"{tag}_prenorm")
    u0 = mm_nn(h, w_up, name=f"{tag}_up", out_dtype=BF16, perm=_ffn_perm)
    z = ffn_act(u0, dw_w, dw_b, name=f"{tag}_act")
    y = mm_nn(z, w_down, name=f"{tag}_down")
    x_new = post_residual(x, y, post_g, gate, name=f"{tag}_post")
    return x_new, (x, h, u0, z, y)


def _ffn_backward(dx, saved, mod, pre_g, post_g, w_up, w_down, dw_w, dw_b, tag):
    x, h, u0, z, y = saved
    sh, sc, gate = mod
    dy, dgate, dpost, _ = post_bwd(dx, y, post_g, gate, name=f"{tag}_post_bwd")
    dz = mm_nt(dy, w_down, name=f"{tag}_down_dx")
    g_down = mm_tn(z, dy, name=f"{tag}_down_dw", J=2, block="a", chips_per_block=2)
    du0, dconv = ffn_act_bwd(dz, u0, dw_w, dw_b, name=f"{tag}_act_bwd")
    dh = mm_nt(du0, w_up, name=f"{tag}_up_dx", perm=_ffn_perm)
    g_up = mm_tn(h, du0, name=f"{tag}_up_dw", J=4, block="b", chips_per_block=1, perm=_ffn_perm)
    dx_in, dsh, dsc, dpre = prenorm_bwd(dh, x, dx, pre_g, sc, name=f"{tag}_prenorm_bwd")
    nb = u0.shape[1] // 4
    dconv = dconv.reshape(8, 2, 2, nb).transpose(0, 2, 1, 3).reshape(8, 4 * nb)
    return dx_in, dict(dsh=dsh, dsc=dsc, dgate=dgate, dpre=dpre, dpost=dpost, g_up=g_up, g_down=g_down,
                       d_dw_w=dconv[0:FFN_W], d_dw_b=dconv[3:4])


def _local_step(x, tgt, mods, P):
    D = x.shape[1]
    m0, m1 = mods
    h1 = prenorm(x, P["pre_mix_g"][0:1], m0[1], m0[0], name="hgrn_prenorm")
    proj = mm_nn(h1, P["hgrn_w_in"], name="hgrn_in")
    o, states = hgrn_fwd(proj, P["hgrn_lb_logits"], name="hgrn_scan")
    og = hgrn_gate(o, proj, P["hgrn_gnorm_g"], name="hgrn_gate")
    y1 = mm_nn(og, P["hgrn_w_out"], name="hgrn_out")
    x1 = post_residual(x, y1, P["post_mix_g"][0:1], m0[2], name="hgrn_post")
    x2, ffn0 = _ffn_forward(x1, m0[3:6], P["pre_ffn_g"][0:1], P["post_ffn_g"][0:1], P["ffn_w_up"][0],
                            P["ffn_w_down"][0], P["ffn_dw_w"][0], P["ffn_dw_b"][0:1], "ffn0")
    h3 = prenorm(x2, P["pre_mix_g"][1:2], m1[1], m1[0], name="conv_prenorm")
    u = mm_nn(h3, P["conv_w_in"], name="conv_in", bias=P["conv_b_in"])
    s, cv = conv_act(u, P["conv_dw_w"], P["conv_dw_b"], P["conv_ln_g"], P["conv_ln_b"], name="conv_act")
    y3 = mm_nn(s, P["conv_w_out"], name="conv_out", bias=P["conv_b_out"])
    x3 = post_residual(x2, y3, P["post_mix_g"][1:2], m1[2], name="conv_post")
    x4, ffn1 = _ffn_forward(x3, m1[3:6], P["pre_ffn_g"][1:2], P["post_ffn_g"][1:2], P["ffn_w_up"][1],
                            P["ffn_w_down"][1], P["ffn_dw_w"][1], P["ffn_dw_b"][1:2], "ffn1")
    dx4, lcols = loss_grad(x4, tgt, name="loss")
    dx3, f1 = _ffn_backward(dx4, ffn1, m1[3:6], P["pre_ffn_g"][1:2], P["post_ffn_g"][1:2], P["ffn_w_up"][1],
                            P["ffn_w_down"][1], P["ffn_dw_w"][1], P["ffn_dw_b"][1:2], "ffn1")
    dy3, dg1_1, dpostmix1, d_b_out = post_bwd(dx3, y3, P["post_mix_g"][1:2], m1[2], name="conv_post_bwd")
    ds = mm_nt(dy3, P["conv_w_out"], name="conv_out_dx")
    g_conv_out = mm_tn(s, dy3, name="conv_out_dw", J=1, block="a", chips_per_block=4)
    dcv, d_ln_g, d_ln_b, d_dw_b = conv_norm_bwd(ds, cv, P["conv_ln_g"], P["conv_ln_b"], name="conv_norm_bwd")
    du, d_dw_w, d_b_in = conv_glu_bwd(dcv, u, P["conv_dw_w"], name="conv_glu_bwd")
    dh3 = mm_nt(du, P["conv_w_in"], name="conv_in_dx")
    g_conv_in = mm_tn(h3, du, name="conv_in_dw", J=4, block="b", chips_per_block=1)
    dx2, dsh1_1, dsc1_1, dpremix1 = prenorm_bwd(dh3, x2, dx3, P["pre_mix_g"][1:2], m1[1], name="conv_prenorm_bwd")
    dx1, f0 = _ffn_backward(dx2, ffn0, m0[3:6], P["pre_ffn_g"][0:1], P["post_ffn_g"][0:1], P["ffn_w_up"][0],
                            P["ffn_w_down"][0], P["ffn_dw_w"][0], P["ffn_dw_b"][0:1], "ffn0")
    dy1, dg1_0, dpostmix0, _ = post_bwd(dx1, y1, P["post_mix_g"][0:1], m0[2], name="hgrn_post_bwd")
    dog = mm_nt(dy1, P["hgrn_w_out"], name="hgrn_out_dx")
    g_hgrn_out = mm_tn(og, dy1, name="hgrn_out_dw", J=1, block="a", chips_per_block=4)
    do, dgp, d_gn = hgrn_gate_bwd(dog, o, proj, P["hgrn_gnorm_g"], name="hgrn_gate_bwd")
    dqp, dfz, dv, dlb = hgrn_bwd(proj, P["hgrn_lb_logits"], states, do, name="hgrn_scan_bwd")
    dproj = jnp.concatenate([dqp, dfz, dv, dgp], axis=1)
    dh1 = mm_nt(dproj, P["hgrn_w_in"], name="hgrn_in_dx")
    g_hgrn_in = mm_tn(h1, dproj, name="hgrn_in_dw", J=4, block="b", chips_per_block=1)
    dx0, dsh1_0, dsc1_0, dpremix0 = prenorm_bwd(dh1, x, dx1, P["pre_mix_g"][0:1], m0[1], name="hgrn_prenorm_bwd")

    dmod = jnp.stack([
        jnp.concatenate([dsh1_0, dsc1_0, dg1_0, f0["dsh"], f0["dsc"], f0["dgate"]], axis=1)[0],
        jnp.concatenate([dsh1_1, dsc1_1, dg1_1, f1["dsh"], f1["dsc"], f1["dgate"]], axis=1)[0]])
    small = dict(
        loss=lcols,
        pre_mix_g=jnp.concatenate([dpremix0, dpremix1]), post_mix_g=jnp.concatenate([dpostmix0, dpostmix1]),
        pre_ffn_g=jnp.concatenate([f0["dpre"], f1["dpre"]]), post_ffn_g=jnp.concatenate([f0["dpost"], f1["dpost"]]),
        lb=dlb, hgrn_gnorm_g=d_gn, ffn_dw_b=jnp.concatenate([f0["d_dw_b"], f1["d_dw_b"]]), dmod=dmod,
        conv_b_in=d_b_in, conv_dw_w=d_dw_w[0:CONV_W], conv_dw_b=d_dw_b, conv_ln_g=d_ln_g, conv_ln_b=d_ln_b,
        conv_b_out=d_b_out, ffn_dw_w=jnp.stack([f0["d_dw_w"], f1["d_dw_w"]]))
    big = [g_hgrn_in, g_hgrn_out, g_conv_in, g_conv_out, f0["g_up"], f1["g_up"], f0["g_down"], f1["g_down"]]
    return dx0, small, big


def _pack(parts, rows=8):
    flat = jnp.concatenate([p.reshape(-1).astype(F32) for p in parts])
    per = rows * 128
    pad = (-flat.shape[0]) % per
    return jnp.pad(flat, (0, pad)).reshape(rows, -1)


def _unpack(flat, shapes):
    out, off = [], 0
    for s in shapes:
        n = 1
        for d in s:
            n *= d
        out.append(flat[..., off:off + n].reshape(flat.shape[:-1] + tuple(s)))
        off += n
    return out


def _from_chips(stacked, axis):
    moved = jnp.moveaxis(stacked, 0, axis)
    shape = list(moved.shape)
    return moved.reshape(shape[:axis] + [shape[axis] * shape[axis + 1]] + shape[axis + 2:])


def _my_shard(full, axis, q):
    n = full.shape[axis] // N_CHIPS
    return lax.dynamic_slice_in_dim(full, q * n, n, axis=axis)


def kernel(x, c, ada_w, ada_b, pre_mix_g, post_mix_g, pre_ffn_g, post_ffn_g, hgrn_w_in, hgrn_lb_logits, hgrn_gnorm_g, hgrn_w_out, conv_w_in, conv_b_in, conv_dw_w, conv_dw_b, conv_ln_g, conv_ln_b, conv_w_out, conv_b_out, ffn_w_up, ffn_dw_w, ffn_dw_b, ffn_w_down, loss_target, m_ada_w, m_ada_b, m_pre_mix_g, m_post_mix_g, m_pre_ffn_g, m_post_ffn_g, m_hgrn_w_in, m_hgrn_lb_logits, m_hgrn_gnorm_g, m_hgrn_w_out, m_conv_w_in, m_conv_b_in, m_conv_dw_w, m_conv_dw_b, m_conv_ln_g, m_conv_ln_b, m_conv_w_out, m_conv_b_out, m_ffn_w_up, m_ffn_dw_w, m_ffn_dw_b, m_ffn_w_down, v_ada_w, v_ada_b, v_pre_mix_g, v_post_mix_g, v_pre_ffn_g, v_post_ffn_g, v_hgrn_w_in, v_hgrn_lb_logits, v_hgrn_gnorm_g, v_hgrn_w_out, v_conv_w_in, v_conv_b_in, v_conv_dw_w, v_conv_dw_b, v_conv_ln_g, v_conv_ln_b, v_conv_w_out, v_conv_b_out, v_ffn_w_up, v_ffn_dw_w, v_ffn_dw_b, v_ffn_w_down):
    W = dict(ada_w=ada_w, ada_b=ada_b, pre_mix_g=pre_mix_g, post_mix_g=post_mix_g, pre_ffn_g=pre_ffn_g,
             post_ffn_g=post_ffn_g, hgrn_w_in=hgrn_w_in, hgrn_lb_logits=hgrn_lb_logits, hgrn_gnorm_g=hgrn_gnorm_g,
             hgrn_w_out=hgrn_w_out, conv_w_in=conv_w_in, conv_b_in=conv_b_in, conv_dw_w=conv_dw_w,
             conv_dw_b=conv_dw_b, conv_ln_g=conv_ln_g, conv_ln_b=conv_ln_b, conv_w_out=conv_w_out,
             conv_b_out=conv_b_out, ffn_w_up=ffn_w_up, ffn_dw_w=ffn_dw_w, ffn_dw_b=ffn_dw_b, ffn_w_down=ffn_w_down)
    M = dict(ada_w=m_ada_w, ada_b=m_ada_b, pre_mix_g=m_pre_mix_g, post_mix_g=m_post_mix_g, pre_ffn_g=m_pre_ffn_g,
             post_ffn_g=m_post_ffn_g, hgrn_w_in=m_hgrn_w_in, hgrn_lb_logits=m_hgrn_lb_logits,
             hgrn_gnorm_g=m_hgrn_gnorm_g, hgrn_w_out=m_hgrn_w_out, conv_w_in=m_conv_w_in, conv_b_in=m_conv_b_in,
             conv_dw_w=m_conv_dw_w, conv_dw_b=m_conv_dw_b, conv_ln_g=m_conv_ln_g, conv_ln_b=m_conv_ln_b,
             conv_w_out=m_conv_w_out, conv_b_out=m_conv_b_out, ffn_w_up=m_ffn_w_up, ffn_dw_w=m_ffn_dw_w,
             ffn_dw_b=m_ffn_dw_b, ffn_w_down=m_ffn_w_down)
    V = dict(ada_w=v_ada_w, ada_b=v_ada_b, pre_mix_g=v_pre_mix_g, post_mix_g=v_post_mix_g, pre_ffn_g=v_pre_ffn_g,
             post_ffn_g=v_post_ffn_g, hgrn_w_in=v_hgrn_w_in, hgrn_lb_logits=v_hgrn_lb_logits,
             hgrn_gnorm_g=v_hgrn_gnorm_g, hgrn_w_out=v_hgrn_w_out, conv_w_in=v_conv_w_in, conv_b_in=v_conv_b_in,
             conv_dw_w=v_conv_dw_w, conv_dw_b=v_conv_dw_b, conv_ln_g=v_conv_ln_g, conv_ln_b=v_conv_ln_b,
             conv_w_out=v_conv_w_out, conv_b_out=v_conv_b_out, ffn_w_up=v_ffn_w_up, ffn_dw_w=v_ffn_dw_w,
             ffn_dw_b=v_ffn_dw_b, ffn_w_down=v_ffn_w_down)
    names = list(W)
    xi, yi, ci = lax.axis_index("x"), lax.axis_index("y"), lax.axis_index("c")
    q = 2 * xi + yi
    me = 2 * q + ci
    D = x.shape[-1]
    L = ada_w.shape[0]

    small_w = ["conv_b_in", "conv_dw_w", "conv_dw_b", "conv_ln_g", "conv_ln_b", "conv_b_out", "ffn_dw_w"]
    small_axis = dict(conv_b_in=1, conv_dw_w=2, conv_dw_b=1, conv_ln_g=1, conv_ln_b=1, conv_b_out=1, ffn_dw_w=2)
    packed = _pack([c] + [W[n] for n in small_w])
    gathered = allgather_devices(packed, name="gather_small_params").reshape(N_DEV, -1)
    parts = _unpack(gathered, [c.shape] + [W[n].shape for n in small_w])
    c_all = parts[0][:, 0, :]
    P = {n: _from_chips(p[0::2], small_axis[n]) for n, p in zip(small_w, parts[1:])}
    P["conv_dw_w"] = P["conv_dw_w"][0]
    for n in ("pre_mix_g", "post_mix_g", "pre_ffn_g", "post_ffn_g", "hgrn_lb_logits", "hgrn_gnorm_g", "ffn_dw_b"):
        P[n] = W[n]

    modp = ada_mod(c_all, ada_w, name="ada_mod")
    ncol = modp.shape[-1]
    mod_all = allgather_devices(modp.reshape(L * N_DEV, ncol), name="gather_mod")
    mod_all = mod_all[0::2].reshape(N_CHIPS, L, N_DEV, ncol)
    mod_me = lax.dynamic_index_in_dim(mod_all, me, axis=2, keepdims=False)
    mod = mod_me.transpose(1, 0, 2).reshape(L, N_CHIPS * ncol) + ada_b
    mods = [tuple(mod[l:l + 1, k * D:(k + 1) * D] for k in range(6)) for l in range(L)]

    def halves(w):
        return w.astype(BF16).reshape(2, w.shape[0] // 2, w.shape[1])

    big_shards = [hgrn_w_in[0], hgrn_w_out[0], conv_w_in[0], conv_w_out[0], ffn_w_up[0], ffn_w_up[1],
                  ffn_w_down[0], ffn_w_down[1]]
    g = allgather_chips([halves(w) for w in big_shards], name="gather_weights")
    stack = lambda t: t.reshape(N_CHIPS, t.shape[1] * t.shape[2], t.shape[3])
    rowsh = lambda t: t.reshape(1, N_CHIPS * t.shape[1] * t.shape[2], t.shape[3])
    P["hgrn_w_in"], P["hgrn_w_out"] = stack(g[0]), rowsh(g[1])
    P["conv_w_in"], P["conv_w_out"] = stack(g[2]), rowsh(g[3])
    P["ffn_w_up"] = [stack(g[4]), stack(g[5])]
    P["ffn_w_down"] = [rowsh(g[6]), rowsh(g[7])]

    grad_x, small, big = _local_step(x[0], loss_target[0], mods, P)

    small_names = list(small)
    gs = allgather_devices(_pack([small[n] for n in small_names]), name="gather_small_grads")
    dmod_all = _unpack(gs.reshape(N_DEV, -1), [small[n].shape for n in small_names])[small_names.index("dmod")]
    tot = sum_devices(gs, name="sum_small_grads").reshape(1, -1)
    S = dict(zip(small_names, _unpack(tot, [small[n].shape for n in small_names])))
    S = {n: v[0] for n, v in S.items()}
    loss = 0.5 * jnp.sum(S["loss"]) / D

    G = {}
    dmod_q = lax.dynamic_slice_in_dim(dmod_all, q * ncol, ncol, axis=2)
    G["ada_w"] = ada_wgrad(c_all.T, dmod_q.transpose(1, 0, 2), name="ada_wgrad")
    G["ada_b"] = S["dmod"]
    for n in ("pre_mix_g", "post_mix_g", "pre_ffn_g", "post_ffn_g", "hgrn_gnorm_g", "ffn_dw_b"):
        G[n] = S[n]
    G["hgrn_lb_logits"] = lb_logits_grad(hgrn_lb_logits, S["lb"], name="lb_logits_grad")
    G["conv_b_in"] = _my_shard(S["conv_b_in"], 1, q)
    G["conv_dw_w"] = _my_shard(S["conv_dw_w"], 1, q)[None]
    for n in ("conv_dw_b", "conv_ln_g", "conv_ln_b", "conv_b_out"):
        G[n] = _my_shard(S[n], 1, q)
    G["ffn_dw_w"] = _my_shard(S["ffn_dw_w"], 2, q)

    red = reduce_weight_grads(big, ci.astype(jnp.int32).reshape(1))
    G["hgrn_w_in"], G["hgrn_w_out"], G["conv_w_in"], G["conv_w_out"] = red[0][None], red[1][None], red[2][None], red[3][None]
    G["ffn_w_up"] = jnp.stack([red[4], red[5]])
    G["ffn_w_down"] = jnp.stack([red[6], red[7]])

    delta, new_m, new_v = {}, {}, {}
    big_names = ["ada_w", "hgrn_w_in", "hgrn_w_out", "conv_w_in", "conv_w_out", "ffn_w_up", "ffn_w_down"]
    for n in big_names:
        shp = W[n].shape
        two = lambda t: t.reshape(-1, shp[-1])
        d_, m_, v_ = adamw(two(W[n]), two(G[n]), two(M[n]), two(V[n]), name=f"adamw_{n}")
        delta[n], new_m[n], new_v[n] = d_.reshape(shp), m_.reshape(shp), v_.reshape(shp)
    rest = [n for n in names if n not in big_names]
    d_, m_, v_ = adamw(_pack([W[n] for n in rest]), _pack([G[n] for n in rest]), _pack([M[n] for n in rest]),
                       _pack([V[n] for n in rest]), name="adamw_small")
    shapes = [W[n].shape for n in rest]
    for n, a, b_, c_ in zip(rest, _unpack(d_.reshape(-1), shapes), _unpack(m_.reshape(-1), shapes),
                            _unpack(v_.reshape(-1), shapes)):
        delta[n], new_m[n], new_v[n] = a, b_, c_

    return (loss, grad_x[None], *[G[n].reshape(W[n].shape) for n in names], *[delta[n] for n in names],
            *[new_m[n] for n in names], *[new_v[n] for n in names])
```

```python
import jax
import jax.numpy as jnp
from jax import lax
from jax.experimental import pallas as pl
from jax.experimental.pallas import tpu as pltpu

F32 = jnp.float32
BF16 = jnp.bfloat16
EPS = 1e-6
HEAD = 128
BLK = 16
BLK_SHIFT = 4
NEG = -1e30
CONV_W = 31
FFN_W = 3
N_CHIPS = 4
N_DEV = 8
V7X_VMEM_LIMIT = 56 * 1024 * 1024
MESH = pl.DeviceIdType.MESH
HBM = pl.BlockSpec(memory_space=pltpu.HBM)
VMEM_SPEC = pl.BlockSpec(memory_space=pltpu.VMEM)

ADAM_LR = 0.001
ADAM_B1 = 0.9
ADAM_B2 = 0.999
ADAM_EPS = 1e-08
ADAM_WD = 0.01
ADAM_STEP = 10


def _cp(*sem):
    return pltpu.CompilerParams(dimension_semantics=sem, vmem_limit_bytes=V7X_VMEM_LIMIT)


def _sig(x):
    return 0.5 * jnp.tanh(0.5 * x) + 0.5


def _silu(x):
    return x * _sig(x)


def _dsilu(x):
    s = _sig(x)
    return s * (1.0 + x * (1.0 - s))


def _dot(a, b):
    return jnp.dot(a, b, preferred_element_type=F32)


def _dot_nt(a, b):
    return lax.dot_general(a, b, (((1,), (1,)), ((), ())), preferred_element_type=F32)


def _dot_tn(a, b):
    return lax.dot_general(a, b, (((0,), (0,)), ((), ())), preferred_element_type=F32)


def _colsum(x):
    return jnp.sum(x, axis=0, keepdims=True)


def _rowmean(x):
    return jnp.mean(x, axis=-1, keepdims=True)


def _ffn_perm(j):
    return (j % 2) * 2 + j // 2


def mm_nn(a, w, *, name, bias=None, out_dtype=F32, perm=None, tm=512):
    T, K = a.shape
    J, _, nb = w.shape
    tm = min(tm, T)
    col = (lambda j: j) if perm is None else perm

    def body(a_ref, w_ref, *rest):
        acc = _dot(a_ref[...], w_ref[...])
        if bias is not None:
            acc = acc + rest[0][...]
        rest[-1][...] = acc.astype(out_dtype)

    in_specs = [pl.BlockSpec((tm, K), lambda j, i: (i, 0)), pl.BlockSpec((None, K, nb), lambda j, i: (j, 0, 0))]
    args = [a, w]
    if bias is not None:
        in_specs.append(pl.BlockSpec((1, nb), lambda j, i: (0, j)))
        args.append(bias)
    return pl.pallas_call(
        body, grid=(J, T // tm), in_specs=in_specs,
        out_specs=pl.BlockSpec((tm, nb), lambda j, i: (i, col(j))),
        out_shape=jax.ShapeDtypeStruct((T, J * nb), out_dtype), name=name,
        compiler_params=_cp("parallel", "parallel"))(*args)


def mm_nt(a, w, *, name, out_dtype=F32, perm=None, tm=512):
    T = a.shape[0]
    J, K, nb = w.shape
    tm = min(tm, T)
    col = (lambda j: j) if perm is None else perm

    def body(a_ref, w_ref, o_ref, acc_ref):
        j = pl.program_id(1)

        @pl.when(j == 0)
        def _():
            acc_ref[...] = jnp.zeros_like(acc_ref)

        acc_ref[...] += _dot_nt(a_ref[...], w_ref[...])

        @pl.when(j == J - 1)
        def _():
            o_ref[...] = acc_ref[...].astype(out_dtype)

    return pl.pallas_call(
        body, grid=(T // tm, J),
        in_specs=[pl.BlockSpec((tm, nb), lambda i, j: (i, col(j))), pl.BlockSpec((None, K, nb), lambda i, j: (j, 0, 0))],
        out_specs=pl.BlockSpec((tm, K), lambda i, j: (i, 0)),
        out_shape=jax.ShapeDtypeStruct((T, K), out_dtype),
        scratch_shapes=[pltpu.VMEM((tm, K), F32)], name=name,
        compiler_params=_cp("parallel", "arbitrary"))(a, w)


def mm_tn(a, b, *, name, J, block, chips_per_block, perm=None, tk=512):
    T = a.shape[0]
    tk = min(tk, T)
    col = (lambda j: j) if perm is None else perm
    if block == "b":
        rows, nb = a.shape[1], b.shape[1] // J
        a_spec = pl.BlockSpec((tk, rows), lambda j, t: (t, 0))
        b_spec = pl.BlockSpec((tk, nb), lambda j, t: (t, col(j)))
    else:
        rows, nb = a.shape[1] // J, b.shape[1]
        a_spec = pl.BlockSpec((tk, rows), lambda j, t: (t, col(j)))
        b_spec = pl.BlockSpec((tk, nb), lambda j, t: (t, 0))
    cpb = chips_per_block
    rh = rows // (2 * cpb)

    def body(a_ref, b_ref, o_ref):
        @pl.when(pl.program_id(1) == 0)
        def _():
            o_ref[...] = jnp.zeros_like(o_ref)

        acc = _dot_tn(a_ref[...], b_ref[...])
        for ch in range(cpb):
            for hf in range(2):
                r0 = (ch * 2 + hf) * rh
                o_ref[hf, ch] += acc[r0:r0 + rh, :]

    return pl.pallas_call(
        body, grid=(J, T // tk), in_specs=[a_spec, b_spec],
        out_specs=pl.BlockSpec((2, cpb, rh, nb), lambda j, t: (0, j, 0, 0)),
        out_shape=jax.ShapeDtypeStruct((2, J * cpb, rh, nb), F32), name=name,
        compiler_params=_cp("parallel", "arbitrary"))(a, b)


def _row(tm, w):
    return pl.BlockSpec((tm, w), lambda i: (i, 0))


def _full(r, w):
    return pl.BlockSpec((r, w), lambda i: (0, 0))


def _acc_init(i, *refs):
    @pl.when(i == 0)
    def _():
        for r in refs:
            r[...] = jnp.zeros_like(r)


def prenorm(x, g, sc, sh, *, name, tm=512):
    T, D = x.shape
    tm = min(tm, T)

    def body(x_ref, g_ref, sc_ref, sh_ref, h_ref):
        xv = x_ref[...]
        r = lax.rsqrt(_rowmean(xv * xv) + EPS)
        h_ref[...] = ((xv * r) * g_ref[...] * (1.0 + sc_ref[...]) + sh_ref[...]).astype(BF16)

    return pl.pallas_call(
        body, grid=(T // tm,), in_specs=[_row(tm, D), _full(1, D), _full(1, D), _full(1, D)],
        out_specs=_row(tm, D), out_shape=jax.ShapeDtypeStruct((T, D), BF16), name=name,
        compiler_params=_cp("parallel"))(x, g, sc, sh)


def post_residual(x, y, g, gate, *, name, tm=512):
    T, D = x.shape
    tm = min(tm, T)

    def body(x_ref, y_ref, g_ref, gate_ref, o_ref):
        yv = y_ref[...]
        r = lax.rsqrt(_rowmean(yv * yv) + EPS)
        o_ref[...] = x_ref[...] + gate_ref[...] * ((yv * r) * g_ref[...])

    return pl.pallas_call(
        body, grid=(T // tm,), in_specs=[_row(tm, D), _row(tm, D), _full(1, D), _full(1, D)],
        out_specs=_row(tm, D), out_shape=jax.ShapeDtypeStruct((T, D), F32), name=name,
        compiler_params=_cp("parallel"))(x, y, g, gate)


def loss_grad(x, tgt, *, name, tm=512):
    T, D = x.shape
    tm = min(tm, T)

    def body(x_ref, t_ref, dx_ref, l_ref):
        _acc_init(pl.program_id(0), l_ref)
        e = x_ref[...] - t_ref[...]
        dx_ref[...] = e * (1.0 / D)
        l_ref[...] += _colsum(e * e)

    return pl.pallas_call(
        body, grid=(T // tm,), in_specs=[_row(tm, D), _row(tm, D)],
        out_specs=[_row(tm, D), _full(1, D)],
        out_shape=[jax.ShapeDtypeStruct((T, D), F32), jax.ShapeDtypeStruct((1, D), F32)], name=name,
        compiler_params=_cp("arbitrary"))(x, tgt)


def post_bwd(dx, y, g, gate, *, name, tm=512):
    T, D = dx.shape
    tm = min(tm, T)

    def body(dx_ref, y_ref, g_ref, gate_ref, dy_ref, dgate_ref, dg_ref, dbias_ref):
        _acc_init(pl.program_id(0), dgate_ref, dg_ref, dbias_ref)
        yv = y_ref[...]
        dxv = dx_ref[...]
        r = lax.rsqrt(_rowmean(yv * yv) + EPS)
        yn = yv * r
        gv = g_ref[...]
        gt = gate_ref[...]
        dgate_ref[...] += _colsum(dxv * (yn * gv))
        dg_ref[...] += _colsum(dxv * gt * yn)
        dyn = dxv * gt * gv
        dy = r * (dyn - yn * _rowmean(dyn * yn))
        dbias_ref[...] += _colsum(dy)
        dy_ref[...] = dy.astype(BF16)

    return pl.pallas_call(
        body, grid=(T // tm,), in_specs=[_row(tm, D), _row(tm, D), _full(1, D), _full(1, D)],
        out_specs=[_row(tm, D), _full(1, D), _full(1, D), _full(1, D)],
        out_shape=[jax.ShapeDtypeStruct((T, D), BF16)] + [jax.ShapeDtypeStruct((1, D), F32)] * 3, name=name,
        compiler_params=_cp("arbitrary"))(dx, y, g, gate)


def prenorm_bwd(dh, x, dres, g, sc, *, name, tm=512):
    T, D = x.shape
    tm = min(tm, T)

    def body(dh_ref, x_ref, dres_ref, g_ref, sc_ref, dx_ref, dsh_ref, dsc_ref, dg_ref):
        _acc_init(pl.program_id(0), dsh_ref, dsc_ref, dg_ref)
        xv = x_ref[...]
        dhv = dh_ref[...]
        r = lax.rsqrt(_rowmean(xv * xv) + EPS)
        xn = xv * r
        gv = g_ref[...]
        one_sc = 1.0 + sc_ref[...]
        dsh_ref[...] += _colsum(dhv)
        dsc_ref[...] += _colsum(dhv * (xn * gv))
        dg_ref[...] += _colsum(dhv * one_sc * xn)
        dxn = dhv * one_sc * gv
        dx_ref[...] = dres_ref[...] + r * (dxn - xn * _rowmean(dxn * xn))

    return pl.pallas_call(
        body, grid=(T // tm,), in_specs=[_row(tm, D), _row(tm, D), _row(tm, D), _full(1, D), _full(1, D)],
        out_specs=[_row(tm, D), _full(1, D), _full(1, D), _full(1, D)],
        out_shape=[jax.ShapeDtypeStruct((T, D), F32)] + [jax.ShapeDtypeStruct((1, D), F32)] * 3, name=name,
        compiler_params=_cp("arbitrary"))(dh, x, dres, g, sc)


HALO = 16


def ffn_act(u0p, dw_w, dw_b, *, name, tm=256):
    T, W = u0p.shape
    nb = W // 4
    tm = min(tm, T)
    hb = tm // HALO

    def body(u_ref, halo_ref, wa_ref, wb_ref, ba_ref, bb_ref, z_ref, buf):
        i = pl.program_id(1)
        buf[0:HALO, :] = jnp.where(i == 0, 0.0, halo_ref[...].astype(F32))
        buf[HALO:HALO + tm, :] = u_ref[...].astype(F32)

        def conv(w_ref, b_ref, c0):
            acc = b_ref[...] + w_ref[0:1, :] * buf[HALO - 2:HALO - 2 + tm, c0:c0 + nb]
            acc = acc + w_ref[1:2, :] * buf[HALO - 1:HALO - 1 + tm, c0:c0 + nb]
            return acc + w_ref[2:3, :] * buf[HALO:HALO + tm, c0:c0 + nb]

        a = conv(wa_ref, ba_ref, 0)
        b = conv(wb_ref, bb_ref, nb)
        z_ref[...] = (_silu(a) * b).astype(BF16)

    return pl.pallas_call(
        body, grid=(2, T // tm),
        in_specs=[pl.BlockSpec((tm, 2 * nb), lambda jc, i: (i, jc)),
                  pl.BlockSpec((HALO, 2 * nb), lambda jc, i: (jnp.maximum(i * hb - 1, 0), jc)),
                  pl.BlockSpec((FFN_W, nb), lambda jc, i: (0, jc)),
                  pl.BlockSpec((FFN_W, nb), lambda jc, i: (0, jc + 2)),
                  pl.BlockSpec((1, nb), lambda jc, i: (0, jc)),
                  pl.BlockSpec((1, nb), lambda jc, i: (0, jc + 2))],
        out_specs=pl.BlockSpec((tm, nb), lambda jc, i: (i, jc)),
        out_shape=jax.ShapeDtypeStruct((T, 2 * nb), BF16),
        scratch_shapes=[pltpu.VMEM((tm + HALO, 2 * nb), F32)], name=name,
        compiler_params=_cp("parallel", "arbitrary"))(u0p, u0p, dw_w, dw_w, dw_b, dw_b)


def ffn_act_bwd(dz, u0p, dw_w, dw_b, *, name, tm=256):
    T, W = u0p.shape
    nb = W // 4
    tm = min(tm, T)
    nt = T // tm
    hb = tm // HALO

    def body(dz_ref, u_ref, halo_ref, wa_ref, wb_ref, ba_ref, bb_ref, du0_ref, dw_ref, buf, dbuf, carry):
        i = pl.program_id(1)
        _acc_init(i, dw_ref)
        first_tile = i == nt - 1
        buf[0:HALO, :] = jnp.where(first_tile, 0.0, halo_ref[...].astype(F32))
        buf[HALO:HALO + tm, :] = u_ref[...].astype(F32)
        dzv = dz_ref[...]

        def conv(w_ref, b_ref, c0):
            acc = b_ref[...] + w_ref[0:1, :] * buf[HALO - 2:HALO - 2 + tm, c0:c0 + nb]
            acc = acc + w_ref[1:2, :] * buf[HALO - 1:HALO - 1 + tm, c0:c0 + nb]
            return acc + w_ref[2:3, :] * buf[HALO:HALO + tm, c0:c0 + nb]

        a = conv(wa_ref, ba_ref, 0)
        b = conv(wb_ref, bb_ref, nb)
        sa = _sig(a)
        dbuf[0:tm, 0:nb] = dzv * b * (sa * (1.0 + a * (1.0 - sa)))
        dbuf[0:tm, nb:2 * nb] = dzv * (a * sa)
        dbuf[tm:tm + HALO, :] = jnp.where(i == 0, 0.0, carry[...])
        du = dbuf[0:tm, :]
        carry[...] = dbuf[0:HALO, :]
        dw_ref[3:4, :] += _colsum(du)
        for k in range(FFN_W):
            dw_ref[k:k + 1, :] += _colsum(du * buf[HALO - 2 + k:HALO - 2 + k + tm, :])
        for w_ref, c0 in ((wa_ref, 0), (wb_ref, nb)):
            acc = w_ref[2:3, :] * dbuf[0:tm, c0:c0 + nb]
            acc = acc + w_ref[1:2, :] * dbuf[1:1 + tm, c0:c0 + nb]
            acc = acc + w_ref[0:1, :] * dbuf[2:2 + tm, c0:c0 + nb]
            du0_ref[:, c0:c0 + nb] = acc.astype(BF16)

    rev = lambda i: nt - 1 - i
    return pl.pallas_call(
        body, grid=(2, nt),
        in_specs=[pl.BlockSpec((tm, nb), lambda jc, i: (rev(i), jc)),
                  pl.BlockSpec((tm, 2 * nb), lambda jc, i: (rev(i), jc)),
                  pl.BlockSpec((HALO, 2 * nb), lambda jc, i: (jnp.maximum(rev(i) * hb - 1, 0), jc)),
                  pl.BlockSpec((FFN_W, nb), lambda jc, i: (0, jc)),
                  pl.BlockSpec((FFN_W, nb), lambda jc, i: (0, jc + 2)),
                  pl.BlockSpec((1, nb), lambda jc, i: (0, jc)),
                  pl.BlockSpec((1, nb), lambda jc, i: (0, jc + 2))],
        out_specs=[pl.BlockSpec((tm, 2 * nb), lambda jc, i: (rev(i), jc)),
                   pl.BlockSpec((8, 2 * nb), lambda jc, i: (0, jc))],
        out_shape=[jax.ShapeDtypeStruct((T, W), BF16), jax.ShapeDtypeStruct((8, W), F32)],
        scratch_shapes=[pltpu.VMEM((tm + HALO, 2 * nb), F32), pltpu.VMEM((tm + HALO, 2 * nb), F32),
                        pltpu.VMEM((HALO, 2 * nb), F32)], name=name,
        compiler_params=_cp("parallel", "arbitrary"))(dz, u0p, u0p, dw_w, dw_w, dw_b, dw_b)


CHALO = 32
CCOL = 256


def conv_act(u, dw_w, dw_b, ln_g, ln_b, *, name, tm=128):
    T, D2 = u.shape
    D = D2 // 2
    tm = min(tm, T)
    hb = tm // CHALO

    def body(u_ref, halo_ref, w_ref, b_ref, g_ref, be_ref, s_ref, cv_ref, gbuf):
        i = pl.program_id(0)
        hv = halo_ref[...]
        gbuf[0:CHALO, :] = jnp.where(i == 0, 0.0, hv[:, 0:D] * _sig(hv[:, D:D2]))
        uv = u_ref[...]
        gbuf[CHALO:CHALO + tm, :] = uv[:, 0:D] * _sig(uv[:, D:D2])
        for c0 in range(0, D, CCOL):
            acc = jnp.zeros((tm, CCOL), F32) + b_ref[:, c0:c0 + CCOL]
            for k in range(CONV_W):
                r = CHALO - (CONV_W - 1) + k
                acc = acc + w_ref[k:k + 1, c0:c0 + CCOL] * gbuf[r:r + tm, c0:c0 + CCOL]
            cv_ref[:, c0:c0 + CCOL] = acc
        cv = cv_ref[...]
        mu = _rowmean(cv)
        xc = cv - mu
        nh = xc * lax.rsqrt(_rowmean(xc * xc) + EPS)
        s_ref[...] = _silu(nh * g_ref[...] + be_ref[...]).astype(BF16)

    return pl.pallas_call(
        body, grid=(T // tm,),
        in_specs=[_row(tm, D2), pl.BlockSpec((CHALO, D2), lambda i: (jnp.maximum(i * hb - 1, 0), 0)),
                  _full(CONV_W, D), _full(1, D), _full(1, D), _full(1, D)],
        out_specs=[_row(tm, D), _row(tm, D)],
        out_shape=[jax.ShapeDtypeStruct((T, D), BF16), jax.ShapeDtypeStruct((T, D), F32)],
        scratch_shapes=[pltpu.VMEM((tm + CHALO, D), F32)], name=name,
        compiler_params=_cp("arbitrary"))(u, u, dw_w, dw_b, ln_g, ln_b)


def conv_norm_bwd(ds, cv, ln_g, ln_b, *, name, tm=512):
    T, D = cv.shape
    tm = min(tm, T)

    def body(ds_ref, cv_ref, g_ref, be_ref, dcv_ref, dg_ref, dbe_ref, dcb_ref):
        _acc_init(pl.program_id(0), dg_ref, dbe_ref, dcb_ref)
        cv_ = cv_ref[...]
        mu = _rowmean(cv_)
        xc = cv_ - mu
        rstd = lax.rsqrt(_rowmean(xc * xc) + EPS)
        nh = xc * rstd
        gv = g_ref[...]
        dln = ds_ref[...] * _dsilu(nh * gv + be_ref[...])
        dg_ref[...] += _colsum(dln * nh)
        dbe_ref[...] += _colsum(dln)
        dnh = dln * gv
        dcv = rstd * (dnh - _rowmean(dnh) - nh * _rowmean(dnh * nh))
        dcb_ref[...] += _colsum(dcv)
        dcv_ref[...] = dcv

    return pl.pallas_call(
        body, grid=(T // tm,), in_specs=[_row(tm, D), _row(tm, D), _full(1, D), _full(1, D)],
        out_specs=[_row(tm, D), _full(1, D), _full(1, D), _full(1, D)],
        out_shape=[jax.ShapeDtypeStruct((T, D), F32)] + [jax.ShapeDtypeStruct((1, D), F32)] * 3, name=name,
        compiler_params=_cp("arbitrary"))(ds, cv, ln_g, ln_b)


def conv_glu_bwd(dcv, u, dw_w, *, name, tm=128):
    T, D2 = u.shape
    D = D2 // 2
    tm = min(tm, T)
    nt = T // tm
    hb = tm // CHALO

    def body(dcv_ref, dnext_ref, u_ref, halo_ref, w_ref, du_ref, dw_ref, dbin_ref, gbuf, dbuf):
        i = pl.program_id(0)
        _acc_init(i, dw_ref, dbin_ref)
        hv = halo_ref[...]
        gbuf[0:CHALO, :] = jnp.where(i == 0, 0.0, hv[:, 0:D] * _sig(hv[:, D:D2]))
        uv = u_ref[...]
        av = uv[:, 0:D]
        sg = _sig(uv[:, D:D2])
        gbuf[CHALO:CHALO + tm, :] = av * sg
        dbuf[0:tm, :] = dcv_ref[...]
        dbuf[tm:tm + CHALO, :] = jnp.where(i == nt - 1, 0.0, dnext_ref[...])
        for c0 in range(0, D, CCOL):
            dc = dbuf[0:tm, c0:c0 + CCOL]
            acc = jnp.zeros((tm, CCOL), F32)
            for k in range(CONV_W):
                r = CHALO - (CONV_W - 1) + k
                dw_ref[k:k + 1, c0:c0 + CCOL] += _colsum(dc * gbuf[r:r + tm, c0:c0 + CCOL])
                rr = CONV_W - 1 - k
                acc = acc + w_ref[k:k + 1, c0:c0 + CCOL] * dbuf[rr:rr + tm, c0:c0 + CCOL]
            a_c = av[:, c0:c0 + CCOL]
            s_c = sg[:, c0:c0 + CCOL]
            da = acc * s_c
            dgt = acc * a_c * s_c * (1.0 - s_c)
            dbin_ref[:, c0:c0 + CCOL] += _colsum(da)
            dbin_ref[:, D + c0:D + c0 + CCOL] += _colsum(dgt)
            du_ref[:, c0:c0 + CCOL] = da.astype(BF16)
            du_ref[:, D + c0:D + c0 + CCOL] = dgt.astype(BF16)

    return pl.pallas_call(
        body, grid=(nt,),
        in_specs=[_row(tm, D), pl.BlockSpec((CHALO, D), lambda i: (jnp.minimum((i + 1) * hb, T // CHALO - 1), 0)),
                  _row(tm, D2), pl.BlockSpec((CHALO, D2), lambda i: (jnp.maximum(i * hb - 1, 0), 0)),
                  _full(CONV_W, D)],
        out_specs=[_row(tm, D2), _full(CHALO, D), _full(1, D2)],
        out_shape=[jax.ShapeDtypeStruct((T, D2), BF16), jax.ShapeDtypeStruct((CHALO, D), F32),
                   jax.ShapeDtypeStruct((1, D2), F32)],
        scratch_shapes=[pltpu.VMEM((tm + CHALO, D), F32), pltpu.VMEM((tm + CHALO, D), F32)], name=name,
        compiler_params=_cp("arbitrary"))(dcv, dcv, u, u, dw_w)


def _lb0(lg_ref):
    l0, l1, l2 = lg_ref[0:1, :], lg_ref[1:2, :], lg_ref[2:3, :]
    m = jnp.maximum(jnp.maximum(l0, l1), l2)
    e0 = jnp.exp(l0 - m)
    return e0 / (e0 + jnp.exp(l1 - m) + jnp.exp(l2 - m))


def _mm_exact(m01, x):
    hi = x.astype(BF16)
    r1 = x - hi.astype(F32)
    mid = r1.astype(BF16)
    lo = (r1 - mid.astype(F32)).astype(BF16)
    return _dot(m01, hi) + _dot(m01, mid) + _dot(m01, lo)


def _hgrn_specs(H, tm, idx):
    return [pl.BlockSpec((tm, HEAD), lambda h, i: (idx(i), h)),
            pl.BlockSpec((tm, HEAD), lambda h, i: (idx(i), H + h)),
            pl.BlockSpec((tm, HEAD), lambda h, i: (idx(i), 2 * H + h)),
            pl.BlockSpec((3, HEAD), lambda h, i: (0, h))]


def _hgrn_consts(tm):
    r = jnp.arange(tm)[:, None]
    c = jnp.arange(tm)[None, :]
    same = (r // BLK) == (c // BLK)
    tril = (same & (c <= r)).astype(BF16)
    triu = (same & (c >= r)).astype(BF16)
    sel = (jnp.arange(tm * BLK)[None, :] // BLK == r).astype(BF16)
    return jnp.ones((HEAD, HEAD), BF16), tril, triu, sel


def _const_spec(shape):
    return pl.BlockSpec(shape, lambda h, i: (0, 0))


def hgrn_fwd(proj, lb_logits, *, name, tm=128):
    T = proj.shape[0]
    H = proj.shape[1] // (4 * HEAD)
    tm = min(tm, T)
    nt = T // tm
    nblk = tm // BLK
    ones, tril, _, _ = _hgrn_consts(tm)

    def body(qp_ref, fz_ref, v_ref, lg_ref, ones_ref, tril_ref, o_ref, st_ref, S_ref, q_s, k_s, b_s, m_s, r_s):
        @pl.when(pl.program_id(1) == 0)
        def _():
            S_ref[...] = jnp.zeros_like(S_ref)

        st_ref[...] = S_ref[...]
        lb = _lb0(lg_ref)
        f = lb + (1.0 - lb) * _sig(fz_ref[...])
        q_s[...] = _silu(qp_ref[...])
        k_s[...] = 1.0 - f
        b_s[...] = _mm_exact(tril_ref[...], jnp.log(f))
        rows = lax.broadcasted_iota(jnp.int32, (BLK, HEAD), 0)
        for nb in range(nblk):
            r0 = nb * BLK
            qb = q_s[r0:r0 + BLK, :]
            bb = b_s[r0:r0 + BLK, :]
            for s in range(BLK):
                r = r0 + s
                dec = jnp.exp(jnp.where(rows >= s, bb - b_s[r:r + 1, :], NEG))
                m_s[r * BLK:(r + 1) * BLK, :] = (qb * k_s[r:r + 1, :] * dec).astype(BF16)
        r_s[...] = _dot(m_s[...], ones_ref[...])
        S = S_ref[...]
        for nb in range(nblk):
            r0 = nb * BLK
            qb = q_s[r0:r0 + BLK, :]
            kb = k_s[r0:r0 + BLK, :]
            bb = b_s[r0:r0 + BLK, :]
            vb = v_ref[r0:r0 + BLK, :]
            o = _dot_nt((qb * jnp.exp(bb)).astype(BF16), S.astype(BF16))
            for s in range(BLK):
                r = r0 + s
                o = o + r_s[r * BLK:(r + 1) * BLK, :] * v_ref[r:r + 1, :]
            o_ref[r0:r0 + BLK, :] = o
            bc = b_s[r0 + BLK - 1:r0 + BLK, :]
            kd = kb * jnp.exp(bc - bb)
            S = S * jnp.exp(bc) + _dot_tn(vb.astype(BF16), kd.astype(BF16))
        S_ref[...] = S

    return pl.pallas_call(
        body, grid=(H, nt),
        in_specs=_hgrn_specs(H, tm, lambda i: i) + [_const_spec((HEAD, HEAD)), _const_spec((tm, tm))],
        out_specs=[pl.BlockSpec((tm, HEAD), lambda h, i: (i, h)),
                   pl.BlockSpec((None, None, HEAD, HEAD), lambda h, i: (i, h, 0, 0))],
        out_shape=[jax.ShapeDtypeStruct((T, H * HEAD), F32), jax.ShapeDtypeStruct((nt, H, HEAD, HEAD), F32)],
        scratch_shapes=[pltpu.VMEM((HEAD, HEAD), F32)] + [pltpu.VMEM((tm, HEAD), F32)] * 3
        + [pltpu.VMEM((tm * BLK, HEAD), BF16), pltpu.VMEM((tm * BLK, HEAD), F32)], name=name,
        compiler_params=_cp("parallel", "arbitrary"))(proj, proj, proj, lb_logits, ones, tril)


def hgrn_bwd(proj, lb_logits, states, do, *, name, tm=128):
    T = proj.shape[0]
    H = proj.shape[1] // (4 * HEAD)
    tm = min(tm, T)
    nt = T // tm
    nblk = tm // BLK

    ones, tril, triu, sel = _hgrn_consts(tm)

    def body(qp_ref, fz_ref, v_ref, lg_ref, st_ref, do_ref, ones_ref, tril_ref, triu_ref, sel_ref,
             dqp_ref, dfz_ref, dv_ref, dlb_ref,
             dS_ref, Sb_ref, q_s, k_s, b_s, dq_s, dk_s, dv_s, db_s, dec_s, m1_s, m2_s, m3_s, r1_s, r2_s):
        i = pl.program_id(1)

        @pl.when(i == 0)
        def _():
            dS_ref[...] = jnp.zeros_like(dS_ref)
            dlb_ref[...] = jnp.zeros_like(dlb_ref)

        lb = _lb0(lg_ref)
        qp = qp_ref[...]
        sg = _sig(fz_ref[...])
        f = lb + (1.0 - lb) * sg
        q_s[...] = _silu(qp)
        k_s[...] = 1.0 - f
        b_s[...] = _mm_exact(tril_ref[...], jnp.log(f))
        rows = lax.broadcasted_iota(jnp.int32, (BLK, HEAD), 0)

        S = st_ref[...]
        for nb in range(nblk):
            r0 = nb * BLK
            Sb_ref[nb] = S
            if nb < nblk - 1:
                bb = b_s[r0:r0 + BLK, :]
                bc = b_s[r0 + BLK - 1:r0 + BLK, :]
                kd = k_s[r0:r0 + BLK, :] * jnp.exp(bc - bb)
                S = S * jnp.exp(bc) + _dot_tn(v_ref[r0:r0 + BLK, :].astype(BF16), kd.astype(BF16))

        for nb in range(nblk):
            r0 = nb * BLK
            qb = q_s[r0:r0 + BLK, :]
            bb = b_s[r0:r0 + BLK, :]
            dob = do_ref[r0:r0 + BLK, :]
            for s in range(BLK):
                r = r0 + s
                slab = slice(r * BLK, (r + 1) * BLK)
                dec = jnp.exp(jnp.where(rows >= s, bb - b_s[r:r + 1, :], NEG))
                dec_s[slab, :] = dec
                m1_s[slab, :] = (qb * dec * k_s[r:r + 1, :]).astype(BF16)
                m2_s[slab, :], m3_s[slab, :] = _split2(jnp.where(rows >= s, dob * v_ref[r:r + 1, :], 0.0))
        r1_s[...] = _dot(m1_s[...], ones_ref[...])
        r2_s[...] = _dot(m2_s[...], ones_ref[...]) + _dot(m3_s[...], ones_ref[...])
        for nb in range(nblk):
            r0 = nb * BLK
            qb = q_s[r0:r0 + BLK, :]
            dob = do_ref[r0:r0 + BLK, :]
            dq = jnp.zeros((BLK, HEAD), F32)
            for s in range(BLK):
                r = r0 + s
                slab = slice(r * BLK, (r + 1) * BLK)
                dec = dec_s[slab, :]
                da = r2_s[slab, :]
                dq = dq + da * k_s[r:r + 1, :] * dec
                m1_s[slab, :] = (r1_s[slab, :] * dob).astype(BF16)
                m2_s[slab, :], m3_s[slab, :] = _split2(da * (qb * dec))
            dq_s[r0:r0 + BLK, :] = dq
        dv_s[...] = _dot(sel_ref[...], m1_s[...])
        dk_s[...] = _dot(sel_ref[...], m2_s[...]) + _dot(sel_ref[...], m3_s[...])

        dS = dS_ref[...]
        for nb in reversed(range(nblk)):
            r0 = nb * BLK
            blk = slice(r0, r0 + BLK)
            S0 = Sb_ref[nb]
            qb = q_s[blk, :]
            kb = k_s[blk, :]
            bb = b_s[blk, :]
            vb = v_ref[blk, :]
            bc = b_s[r0 + BLK - 1:r0 + BLK, :]
            eb = jnp.exp(bb)
            ekd = jnp.exp(bc - bb)
            ebc = jnp.exp(bc)
            dS16 = dS.astype(BF16)
            dob16 = do_ref[blk, :].astype(BF16)
            dq = dq_s[blk, :] + _dot(dob16, S0.astype(BF16)) * eb
            dki = _dot(vb.astype(BF16), dS16) * ekd
            dk = dk_s[blk, :] + dki
            dq_s[blk, :] = dq
            dk_s[blk, :] = dk
            dv_s[blk, :] += _dot_nt((kb * ekd).astype(BF16), dS16)
            db_s[blk, :] = qb * dq - kb * dk
            db_s[r0 + BLK - 1:r0 + BLK, :] += _colsum(dS * S0) * ebc + _colsum(kb * dki)
            dS = dS * ebc + _dot_tn(dob16, (qb * eb).astype(BF16))
        dS_ref[...] = dS

        dlf = _mm_exact(triu_ref[...], db_s[...])
        df = dlf / f - dk_s[...]
        dfz_ref[...] = (df * (1.0 - lb) * sg * (1.0 - sg)).astype(BF16)
        dlb_ref[...] += _colsum(df * (1.0 - sg))
        dqp_ref[...] = (dq_s[...] * _dsilu(qp)).astype(BF16)
        dv_ref[...] = dv_s[...].astype(BF16)

    rev = lambda i: nt - 1 - i
    out_blk = pl.BlockSpec((tm, HEAD), lambda h, i: (rev(i), h))
    return pl.pallas_call(
        body, grid=(H, nt),
        in_specs=_hgrn_specs(H, tm, rev) + [pl.BlockSpec((None, None, HEAD, HEAD), lambda h, i: (rev(i), h, 0, 0)),
                                            pl.BlockSpec((tm, HEAD), lambda h, i: (rev(i), h)),
                                            _const_spec((HEAD, HEAD)), _const_spec((tm, tm)), _const_spec((tm, tm)),
                                            _const_spec((tm, tm * BLK))],
        out_specs=[out_blk, out_blk, out_blk, pl.BlockSpec((1, HEAD), lambda h, i: (0, h))],
        out_shape=[jax.ShapeDtypeStruct((T, H * HEAD), BF16)] * 3 + [jax.ShapeDtypeStruct((1, H * HEAD), F32)],
        scratch_shapes=[pltpu.VMEM((HEAD, HEAD), F32), pltpu.VMEM((nblk, HEAD, HEAD), F32)]
        + [pltpu.VMEM((tm, HEAD), F32)] * 7
        + [pltpu.VMEM((tm * BLK, HEAD), F32)] + [pltpu.VMEM((tm * BLK, HEAD), BF16)] * 3
        + [pltpu.VMEM((tm * BLK, HEAD), F32)] * 2, name=name,
        compiler_params=_cp("parallel", "arbitrary"))(proj, proj, proj, lb_logits, states, do, ones, tril, triu, sel)


HB = 2


def _hgrn_mh_specs(H, tm, idx):
    g = H // HB
    return [pl.BlockSpec((tm, HB * HEAD), lambda h, i: (idx(i), h)),
            pl.BlockSpec((tm, HB * HEAD), lambda h, i: (idx(i), g + h)),
            pl.BlockSpec((tm, HB * HEAD), lambda h, i: (idx(i), 2 * g + h)),
            pl.BlockSpec((3, HB * HEAD), lambda h, i: (0, h))]


def hgrn_scan(proj, lb_logits, *, name, tm=128):
    T = proj.shape[0]
    H = proj.shape[1] // (4 * HEAD)
    tm = min(tm, T)
    nt = T // tm
    nblk = tm // BLK
    _, tril, _, _ = _hgrn_consts(tm)
    heads = [slice(hh * HEAD, (hh + 1) * HEAD) for hh in range(HB)]

    def body(qp_ref, fz_ref, v_ref, lg_ref, tril_ref, o_ref, st_ref, S_ref, q_s, k_s, b_s):
        @pl.when(pl.program_id(1) == 0)
        def _():
            S_ref[...] = jnp.zeros_like(S_ref)

        st_ref[...] = S_ref[...]
        lb = _lb0(lg_ref)
        f = lb + (1.0 - lb) * _sig(fz_ref[...])
        q_s[...] = _silu(qp_ref[...])
        k_s[...] = 1.0 - f
        b_s[...] = _mm_exact(tril_ref[...], jnp.log(f))
        rows = lax.broadcasted_iota(jnp.int32, (BLK, HEAD), 0)
        S = [S_ref[hh] for hh in range(HB)]
        for nb in range(nblk):
            blk = slice(nb * BLK, (nb + 1) * BLK)
            last = slice(nb * BLK + BLK - 1, nb * BLK + BLK)
            qb = [q_s[blk, c] for c in heads]
            bb = [b_s[blk, c] for c in heads]
            o = [_dot_nt((qb[hh] * jnp.exp(bb[hh])).astype(BF16), S[hh].astype(BF16)) for hh in range(HB)]
            for hh, c in enumerate(heads):
                bc = b_s[last, c]
                kd = k_s[blk, c] * jnp.exp(bc - bb[hh])
                S[hh] = S[hh] * jnp.exp(bc) + _dot_tn(v_ref[blk, c].astype(BF16), kd.astype(BF16))
            for s in range(BLK):
                r = slice(nb * BLK + s, nb * BLK + s + 1)
                for hh, c in enumerate(heads):
                    dec = jnp.exp(jnp.where(rows >= s, bb[hh] - b_s[r, c], NEG))
                    a = jnp.sum(qb[hh] * k_s[r, c] * dec, axis=-1, keepdims=True)
                    o[hh] = o[hh] + a * v_ref[r, c]
            for hh, c in enumerate(heads):
                o_ref[blk, c] = o[hh]
        for hh in range(HB):
            S_ref[hh] = S[hh]

    return pl.pallas_call(
        body, grid=(H // HB, nt),
        in_specs=_hgrn_mh_specs(H, tm, lambda i: i) + [_const_spec((tm, tm))],
        out_specs=[pl.BlockSpec((tm, HB * HEAD), lambda h, i: (i, h)),
                   pl.BlockSpec((None, HB, HEAD, HEAD), lambda h, i: (i, h, 0, 0))],
        out_shape=[jax.ShapeDtypeStruct((T, H * HEAD), F32), jax.ShapeDtypeStruct((nt, H, HEAD, HEAD), F32)],
        scratch_shapes=[pltpu.VMEM((HB, HEAD, HEAD), F32)] + [pltpu.VMEM((tm, HB * HEAD), F32)] * 3, name=name,
        compiler_params=_cp("parallel", "arbitrary"))(proj, proj, proj, lb_logits, tril)


def hgrn_scan_bwd(proj, lb_logits, states, do, *, name, tm=128):
    T = proj.shape[0]
    H = proj.shape[1] // (4 * HEAD)
    tm = min(tm, T)
    nt = T // tm
    nblk = tm // BLK
    _, tril, triu, _ = _hgrn_consts(tm)
    heads = [slice(hh * HEAD, (hh + 1) * HEAD) for hh in range(HB)]

    def body(qp_ref, fz_ref, v_ref, lg_ref, st_ref, do_ref, tril_ref, triu_ref, dqp_ref, dfz_ref, dv_ref, dlb_ref,
             dS_ref, Sb_ref, q_s, k_s, b_s, dq_s, dk_s, dv_s, db_s):
        i = pl.program_id(1)

        @pl.when(i == 0)
        def _():
            dS_ref[...] = jnp.zeros_like(dS_ref)
            dlb_ref[...] = jnp.zeros_like(dlb_ref)

        lb = _lb0(lg_ref)
        qp = qp_ref[...]
        sg = _sig(fz_ref[...])
        f = lb + (1.0 - lb) * sg
        q_s[...] = _silu(qp)
        k_s[...] = 1.0 - f
        b_s[...] = _mm_exact(tril_ref[...], jnp.log(f))
        rows = lax.broadcasted_iota(jnp.int32, (BLK, HEAD), 0)
        rows1 = lax.broadcasted_iota(jnp.int32, (BLK, 1), 0)

        S = [st_ref[hh] for hh in range(HB)]
        for nb in range(nblk):
            blk = slice(nb * BLK, (nb + 1) * BLK)
            last = slice(nb * BLK + BLK - 1, nb * BLK + BLK)
            for hh, c in enumerate(heads):
                Sb_ref[nb * HB + hh] = S[hh]
                if nb < nblk - 1:
                    bc = b_s[last, c]
                    kd = k_s[blk, c] * jnp.exp(bc - b_s[blk, c])
                    S[hh] = S[hh] * jnp.exp(bc) + _dot_tn(v_ref[blk, c].astype(BF16), kd.astype(BF16))

        dS = [dS_ref[hh] for hh in range(HB)]
        for nb in reversed(range(nblk)):
            blk = slice(nb * BLK, (nb + 1) * BLK)
            last = slice(nb * BLK + BLK - 1, nb * BLK + BLK)
            qb, kb, bb, dob, dq, dbc, ebc = [], [], [], [], [], [], []
            for hh, c in enumerate(heads):
                S0 = Sb_ref[nb * HB + hh]
                qb.append(q_s[blk, c])
                kb.append(k_s[blk, c])
                bb.append(b_s[blk, c])
                dob.append(do_ref[blk, c])
                bc = b_s[last, c]
                eb = jnp.exp(bb[hh])
                ekd = jnp.exp(bc - bb[hh])
                ebc.append(jnp.exp(bc))
                dS16 = dS[hh].astype(BF16)
                dob16 = dob[hh].astype(BF16)
                dq.append(_dot(dob16, S0.astype(BF16)) * eb)
                dki = _dot(v_ref[blk, c].astype(BF16), dS16) * ekd
                dk_s[blk, c] = dki
                dv_s[blk, c] = _dot_nt((kb[hh] * ekd).astype(BF16), dS16)
                dbc.append(_colsum(dS[hh] * S0) * ebc[hh] + _colsum(kb[hh] * dki))
                dS[hh] = dS[hh] * ebc[hh] + _dot_tn(dob16, (qb[hh] * eb).astype(BF16))
            for s in range(BLK):
                r = slice(nb * BLK + s, nb * BLK + s + 1)
                for hh, c in enumerate(heads):
                    ks = k_s[r, c]
                    dec = jnp.exp(jnp.where(rows >= s, bb[hh] - b_s[r, c], NEG))
                    w = qb[hh] * dec
                    a = jnp.sum(w * ks, axis=-1, keepdims=True)
                    da = jnp.where(rows1 >= s, jnp.sum(dob[hh] * v_ref[r, c], axis=-1, keepdims=True), 0.0)
                    dq[hh] = dq[hh] + (da * ks) * dec
                    dk_s[r, c] += _colsum(da * w)
                    dv_s[r, c] += _colsum(a * dob[hh])
            for hh, c in enumerate(heads):
                dq_s[blk, c] = dq[hh]
                db_s[blk, c] = qb[hh] * dq[hh] - kb[hh] * dk_s[blk, c]
                db_s[last, c] += dbc[hh]
        for hh in range(HB):
            dS_ref[hh] = dS[hh]

        dlf = _mm_exact(triu_ref[...], db_s[...])
        df = dlf / f - dk_s[...]
        dfz_ref[...] = (df * (1.0 - lb) * sg * (1.0 - sg)).astype(BF16)
        dlb_ref[...] += _colsum(df * (1.0 - sg))
        dqp_ref[...] = (dq_s[...] * _dsilu(qp)).astype(BF16)
        dv_ref[...] = dv_s[...].astype(BF16)

    rev = lambda i: nt - 1 - i
    out_blk = pl.BlockSpec((tm, HB * HEAD), lambda h, i: (rev(i), h))
    return pl.pallas_call(
        body, grid=(H // HB, nt),
        in_specs=_hgrn_mh_specs(H, tm, rev) + [pl.BlockSpec((None, HB, HEAD, HEAD), lambda h, i: (rev(i), h, 0, 0)),
                                               out_blk, _const_spec((tm, tm)), _const_spec((tm, tm))],
        out_specs=[out_blk, out_blk, out_blk, pl.BlockSpec((1, HB * HEAD), lambda h, i: (0, h))],
        out_shape=[jax.ShapeDtypeStruct((T, H * HEAD), BF16)] * 3 + [jax.ShapeDtypeStruct((1, H * HEAD), F32)],
        scratch_shapes=[pltpu.VMEM((HB, HEAD, HEAD), F32), pltpu.VMEM((nblk * HB, HEAD, HEAD), F32)]
        + [pltpu.VMEM((tm, HB * HEAD), F32)] * 7, name=name,
        compiler_params=_cp("parallel", "arbitrary"))(proj, proj, proj, lb_logits, states, do, tril, triu)


def hgrn_gate(o, proj, gn, *, name, tm=512):
    T, D = o.shape
    H = D // HEAD
    tm = min(tm, T)

    def body(o_ref, gp_ref, gn_ref, og_ref):
        gn_ = gn_ref[...]
        for h in range(H):
            c = slice(h * HEAD, (h + 1) * HEAD)
            oh = o_ref[:, c]
            r = lax.rsqrt(_rowmean(oh * oh) + EPS)
            og_ref[:, c] = ((oh * r) * gn_ * _silu(gp_ref[:, c])).astype(BF16)

    return pl.pallas_call(
        body, grid=(T // tm,),
        in_specs=[_row(tm, D), pl.BlockSpec((tm, D), lambda i: (i, 3)), _full(1, HEAD)],
        out_specs=_row(tm, D), out_shape=jax.ShapeDtypeStruct((T, D), BF16), name=name,
        compiler_params=_cp("parallel"))(o, proj, gn)


def hgrn_gate_bwd(dog, o, proj, gn, *, name, tm=512):
    T, D = o.shape
    H = D // HEAD
    tm = min(tm, T)

    def body(dog_ref, o_ref, gp_ref, gn_ref, do_ref, dgp_ref, dgn_ref):
        _acc_init(pl.program_id(0), dgn_ref)
        gn_ = gn_ref[...]
        for h in range(H):
            c = slice(h * HEAD, (h + 1) * HEAD)
            oh = o_ref[:, c]
            gp = gp_ref[:, c]
            dg = dog_ref[:, c]
            r = lax.rsqrt(_rowmean(oh * oh) + EPS)
            on = oh * r
            dgp_ref[:, c] = (dg * (on * gn_) * _dsilu(gp)).astype(BF16)
            don = dg * _silu(gp)
            dgn_ref[...] += _colsum(don * on)
            dn = don * gn_
            do_ref[:, c] = r * (dn - on * _rowmean(dn * on))

    return pl.pallas_call(
        body, grid=(T // tm,),
        in_specs=[_row(tm, D), _row(tm, D), pl.BlockSpec((tm, D), lambda i: (i, 3)), _full(1, HEAD)],
        out_specs=[_row(tm, D), _row(tm, D), _full(1, HEAD)],
        out_shape=[jax.ShapeDtypeStruct((T, D), F32), jax.ShapeDtypeStruct((T, D), BF16),
                   jax.ShapeDtypeStruct((1, HEAD), F32)], name=name,
        compiler_params=_cp("arbitrary"))(dog, o, proj, gn)


def _split2(x):
    hi = x.astype(BF16)
    return hi, (x - hi.astype(F32)).astype(BF16)


def ada_mod(c_all, ada_w, *, name):
    L, D, N = ada_w.shape
    B = c_all.shape[0]

    def body(c_ref, w_ref, o_ref):
        chi, clo = _split2(_silu(c_ref[...]))
        whi, wlo = _split2(w_ref[...])
        o_ref[...] = _dot(chi, whi) + _dot(chi, wlo) + _dot(clo, whi)

    return pl.pallas_call(
        body, grid=(L,), in_specs=[_full(B, D), pl.BlockSpec((None, D, N), lambda l: (l, 0, 0))],
        out_specs=pl.BlockSpec((None, B, N), lambda l: (l, 0, 0)),
        out_shape=jax.ShapeDtypeStruct((L, B, N), F32), name=name, compiler_params=_cp("parallel"))(c_all, ada_w)


def ada_wgrad(c_all_t, dmod, *, name, tr=256):
    D, B = c_all_t.shape
    L, _, N = dmod.shape
    tr = min(tr, D)

    def body(c_ref, d_ref, o_ref):
        cond = _silu(c_ref[...])
        acc = cond[:, 0:1] * d_ref[0:1, :]
        for b in range(1, B):
            acc = acc + cond[:, b:b + 1] * d_ref[b:b + 1, :]
        o_ref[...] = acc

    return pl.pallas_call(
        body, grid=(L, D // tr),
        in_specs=[pl.BlockSpec((tr, B), lambda l, r: (r, 0)), pl.BlockSpec((None, B, N), lambda l, r: (l, 0, 0))],
        out_specs=pl.BlockSpec((None, tr, N), lambda l, r: (l, r, 0)),
        out_shape=jax.ShapeDtypeStruct((L, D, N), F32), name=name,
        compiler_params=_cp("parallel", "parallel"))(c_all_t, dmod)


def sum_devices(parts, *, name):
    n, R, C = parts.shape

    def body(p_ref, o_ref):
        acc = p_ref[0]
        for d in range(1, n):
            acc = acc + p_ref[d]
        o_ref[...] = acc

    return pl.pallas_call(body, in_specs=[VMEM_SPEC], out_specs=VMEM_SPEC,
                          out_shape=jax.ShapeDtypeStruct((R, C), F32), name=name)(parts)


def lb_logits_grad(lb_logits, dlb, *, name):
    def body(lg_ref, d_ref, o_ref):
        l0, l1, l2 = lg_ref[0:1, :], lg_ref[1:2, :], lg_ref[2:3, :]
        m = jnp.maximum(jnp.maximum(l0, l1), l2)
        e0, e1, e2 = jnp.exp(l0 - m), jnp.exp(l1 - m), jnp.exp(l2 - m)
        z = e0 + e1 + e2
        p0, p1, p2 = e0 / z, e1 / z, e2 / z
        g = d_ref[...] * p0
        o_ref[0:1, :] = g * (1.0 - p0)
        o_ref[1:2, :] = -g * p1
        o_ref[2:3, :] = -g * p2

    return pl.pallas_call(body, in_specs=[VMEM_SPEC, VMEM_SPEC], out_specs=VMEM_SPEC,
                          out_shape=jax.ShapeDtypeStruct(lb_logits.shape, F32), name=name)(lb_logits, dlb)


def _tile(n, pref):
    if n <= pref:
        return n
    t = pref - pref % 8
    while n % t:
        t -= 8
    return t


def adamw(w, g, m, v, *, name, tr=256):
    R, C = w.shape
    tr = _tile(R, tr)

    def body(w_ref, g_ref, m_ref, v_ref, d_ref, nm_ref, nv_ref):
        gv = g_ref[...]
        nm = ADAM_B1 * m_ref[...] + (1.0 - ADAM_B1) * gv
        nv = ADAM_B2 * v_ref[...] + (1.0 - ADAM_B2) * (gv * gv)
        m_hat = nm / (1.0 - ADAM_B1 ** ADAM_STEP)
        v_hat = nv / (1.0 - ADAM_B2 ** ADAM_STEP)
        d_ref[...] = -ADAM_LR * (m_hat / (jnp.sqrt(v_hat) + ADAM_EPS) + ADAM_WD * w_ref[...])
        nm_ref[...] = nm
        nv_ref[...] = nv

    spec = pl.BlockSpec((tr, C), lambda i: (i, 0))
    return pl.pallas_call(
        body, grid=(R // tr,), in_specs=[spec] * 4, out_specs=[spec] * 3,
        out_shape=[jax.ShapeDtypeStruct((R, C), F32)] * 3, name=name, compiler_params=_cp("parallel"))(w, g, m, v)


def _place():
    return lax.axis_index("x"), lax.axis_index("y"), lax.axis_index("c")


def _flip(v, bit):
    return 1 - v if bit else v


def allgather_devices(v, *, name):
    R, C = v.shape

    def body(v_ref, out_ref, send_sems, recv_sems, local_sem):
        x, y, c = _place()
        me = 4 * x + 2 * y + c
        mine = pltpu.make_async_copy(v_ref, out_ref.at[me], local_sem)
        mine.start()
        sends = []
        for k in range(1, N_DEV):
            peer = (_flip(x, k & 4), _flip(y, k & 2), _flip(c, k & 1))
            cp = pltpu.make_async_remote_copy(src_ref=v_ref, dst_ref=out_ref.at[me], send_sem=send_sems.at[k - 1],
                                              recv_sem=recv_sems.at[k - 1], device_id=peer, device_id_type=MESH)
            cp.start()
            sends.append(cp)
        for k in range(1, N_DEV):
            px, py, pc = _flip(x, k & 4), _flip(y, k & 2), _flip(c, k & 1)
            pltpu.make_async_remote_copy(src_ref=v_ref, dst_ref=out_ref.at[4 * px + 2 * py + pc],
                                         send_sem=send_sems.at[k - 1], recv_sem=recv_sems.at[k - 1],
                                         device_id=(px, py, pc), device_id_type=MESH).wait_recv()
        for cp in sends:
            cp.wait_send()
        mine.wait()

    return pl.pallas_call(
        body, in_specs=[VMEM_SPEC], out_specs=VMEM_SPEC, out_shape=jax.ShapeDtypeStruct((N_DEV, R, C), v.dtype),
        scratch_shapes=[pltpu.SemaphoreType.DMA((N_DEV - 1,)), pltpu.SemaphoreType.DMA((N_DEV - 1,)),
                        pltpu.SemaphoreType.DMA], name=name)(v)


def _other_chips(x, y):
    return [(1 - x, y), (x, 1 - y), (1 - x, 1 - y)]


def allgather_chips(bufs, *, name):
    n = len(bufs)

    def body(*refs):
        outs = refs[n:2 * n]
        send_sems, recv_sems = refs[2 * n:]
        x, y, c = _place()
        q = 2 * x + y
        chips = _other_chips(x, y)

        def copy(a, k, block, half, to):
            slab = outs[a].at[block, half]
            return pltpu.make_async_remote_copy(src_ref=slab, dst_ref=slab, send_sem=send_sems.at[a, k],
                                                recv_sem=recv_sems.at[a, k], device_id=to, device_id_type=MESH)

        first = [copy(a, j, q, c, (*chips[j], c)) for a in range(n) for j in range(3)]
        for cp in first:
            cp.start()
        passed = []
        for a in range(n):
            for j, (px, py) in enumerate(chips):
                copy(a, j, 2 * px + py, c, (x, y, c)).wait_recv()
                fw = copy(a, 3 + j, 2 * px + py, c, (x, y, 1 - c))
                fw.start()
                passed.append(fw)
        for a in range(n):
            for j, (px, py) in enumerate(chips):
                copy(a, 3 + j, 2 * px + py, 1 - c, (x, y, c)).wait_recv()
        for cp in first + passed:
            cp.wait_send()

    return pl.pallas_call(
        body, in_specs=[HBM] * n, out_specs=[HBM] * n,
        out_shape=[jax.ShapeDtypeStruct(b.shape, b.dtype) for b in bufs],
        input_output_aliases={a: a for a in range(n)},
        scratch_shapes=[pltpu.SemaphoreType.DMA((n, 6)), pltpu.SemaphoreType.DMA((n, 6))], name=name)(*bufs)


def pair_exchange(grads, *, name):
    n = len(grads)

    def body(*refs):
        ins, outs = refs[:n], refs[n:2 * n]
        send_sems, recv_sems = refs[2 * n:]
        x, y, c = _place()
        cps = [pltpu.make_async_remote_copy(src_ref=ins[a].at[1 - c], dst_ref=outs[a], send_sem=send_sems.at[a],
                                            recv_sem=recv_sems.at[a], device_id=(x, y, 1 - c), device_id_type=MESH)
               for a in range(n)]
        for cp in cps:
            cp.start()
        for cp in cps:
            cp.wait_recv()
        for cp in cps:
            cp.wait_send()

    return pl.pallas_call(
        body, in_specs=[HBM] * n, out_specs=[HBM] * n,
        out_shape=[jax.ShapeDtypeStruct(g.shape[1:], g.dtype) for g in grads],
        scratch_shapes=[pltpu.SemaphoreType.DMA((n,)), pltpu.SemaphoreType.DMA((n,))], name=name)(*grads)


def pair_add(g, other, c_idx, *, name, tr=256):
    _, Q, R, C = g.shape
    tr = _tile(R, tr)

    def body(c_ref, g_ref, o_ref, out_ref):
        out_ref[...] = (g_ref[...] + o_ref[...]).astype(BF16)

    return pl.pallas_call(
        body,
        grid_spec=pltpu.PrefetchScalarGridSpec(
            num_scalar_prefetch=1, grid=(Q, R // tr),
            in_specs=[pl.BlockSpec((None, None, tr, C), lambda q, r, c_ref: (c_ref[0], q, r, 0)),
                      pl.BlockSpec((None, tr, C), lambda q, r, c_ref: (q, r, 0))],
            out_specs=pl.BlockSpec((None, tr, C), lambda q, r, c_ref: (q, r, 0))),
        out_shape=jax.ShapeDtypeStruct((Q, R, C), BF16), name=name,
        compiler_params=_cp("parallel", "parallel"))(c_idx, g, other)


def chip_exchange(sums, *, name):
    n = len(sums)

    def body(*refs):
        ins, outs = refs[:n], refs[n:2 * n]
        send_sems, recv_sems = refs[2 * n:]
        x, y, c = _place()
        chips = _other_chips(x, y)
        sends = [pltpu.make_async_remote_copy(src_ref=ins[a].at[2 * px + py], dst_ref=outs[a].at[j],
                                              send_sem=send_sems.at[a, j], recv_sem=recv_sems.at[a, j],
                                              device_id=(px, py, c), device_id_type=MESH)
                 for a in range(n) for j, (px, py) in enumerate(chips)]
        for cp in sends:
            cp.start()
        for cp in sends:
            cp.wait_recv()
        for cp in sends:
            cp.wait_send()

    return pl.pallas_call(
        body, in_specs=[HBM] * n, out_specs=[HBM] * n,
        out_shape=[jax.ShapeDtypeStruct((3,) + s.shape[1:], s.dtype) for s in sums],
        scratch_shapes=[pltpu.SemaphoreType.DMA((n, 3)), pltpu.SemaphoreType.DMA((n, 3))], name=name)(*sums)


def chip_sum(sums, landed, qc_idx, *, name, tr=256):
    _, R, C = sums.shape
    tr = _tile(R, tr)

    def body(qc_ref, own_ref, l_ref, o_ref):
        acc = own_ref[...].astype(F32)
        for k in range(3):
            acc = acc + l_ref[k].astype(F32)
        o_ref[...] = acc

    return pl.pallas_call(
        body,
        grid_spec=pltpu.PrefetchScalarGridSpec(
            num_scalar_prefetch=1, grid=(R // tr,),
            in_specs=[pl.BlockSpec((None, tr, C), lambda r, qc: (qc[0], r, 0)),
                      pl.BlockSpec((3, tr, C), lambda r, qc: (0, r, 0))],
            out_specs=pl.BlockSpec((None, tr, C), lambda r, qc: (qc[1], r, 0))),
        out_shape=jax.ShapeDtypeStruct((2, R, C), F32), name=name,
        compiler_params=_cp("parallel"))(qc_idx, sums, landed)


def half_swap(bufs, *, name):
    n = len(bufs)

    def body(*refs):
        outs = refs[n:2 * n]
        send_sems, recv_sems = refs[2 * n:]
        x, y, c = _place()
        cps = [pltpu.make_async_remote_copy(src_ref=outs[a].at[c], dst_ref=outs[a].at[c], send_sem=send_sems.at[a],
                                            recv_sem=recv_sems.at[a], device_id=(x, y, 1 - c), device_id_type=MESH)
               for a in range(n)]
        for cp in cps:
            cp.start()
        for a in range(n):
            pltpu.make_async_remote_copy(src_ref=outs[a].at[c], dst_ref=outs[a].at[1 - c], send_sem=send_sems.at[a],
                                         recv_sem=recv_sems.at[a], device_id=(x, y, 1 - c),
                                         device_id_type=MESH).wait_recv()
        for cp in cps:
            cp.wait_send()

    return pl.pallas_call(
        body, in_specs=[HBM] * n, out_specs=[HBM] * n,
        out_shape=[jax.ShapeDtypeStruct(b.shape, b.dtype) for b in bufs],
        input_output_aliases={a: a for a in range(n)},
        scratch_shapes=[pltpu.SemaphoreType.DMA((n,)), pltpu.SemaphoreType.DMA((n,))], name=name)(*bufs)


def reduce_weight_grads(grads, q, c):
    c_idx = c.astype(jnp.int32).reshape(1)
    qc_idx = jnp.stack([q, c]).astype(jnp.int32)
    others = pair_exchange(grads, name="grad_pair_exchange")
    sums = [pair_add(g, o, c_idx, name=f"grad_pair_add_{a}") for a, (g, o) in enumerate(zip(grads, others))]
    landed = chip_exchange(sums, name="grad_chip_exchange")
    halves = [chip_sum(s, l, qc_idx, name=f"grad_chip_sum_{a}") for a, (s, l) in enumerate(zip(sums, landed))]
    full = half_swap(halves, name="grad_half_swap")
    return [f.reshape(2 * f.shape[1], f.shape[2]) for f in full]


def _ffn_forward(x, mod, pre_g, post_g, w_up, w_down, dw_w, dw_b, tag):
    sh, sc, gate = mod
    h = prenorm(x, pre_g, sc, sh, name=f"{tag}_prenorm")
    u0 = mm_nn(h, w_up, name=f"{tag}_up", out_dtype=BF16, perm=_ffn_perm)
    z = ffn_act(u0, dw_w, dw_b, name=f"{tag}_act")
    y = mm_nn(z, w_down, name=f"{tag}_down")
    x_new = post_residual(x, y, post_g, gate, name=f"{tag}_post")
    return x_new, (x, h, u0, z, y)


def _ffn_backward(dx, saved, mod, pre_g, post_g, w_up, w_down, dw_w, dw_b, tag):
    x, h, u0, z, y = saved
    sh, sc, gate = mod
    dy, dgate, dpost, _ = post_bwd(dx, y, post_g, gate, name=f"{tag}_post_bwd")
    dz = mm_nt(dy, w_down, name=f"{tag}_down_dx")
    g_down = mm_tn(z, dy, name=f"{tag}_down_dw", J=2, block="a", chips_per_block=2)
    du0, dconv = ffn_act_bwd(dz, u0, dw_w, dw_b, name=f"{tag}_act_bwd")
    dh = mm_nt(du0, w_up, name=f"{tag}_up_dx", perm=_ffn_perm)
    g_up = mm_tn(h, du0, name=f"{tag}_up_dw", J=4, block="b", chips_per_block=1, perm=_ffn_perm)
    dx_in, dsh, dsc, dpre = prenorm_bwd(dh, x, dx, pre_g, sc, name=f"{tag}_prenorm_bwd")
    nb = u0.shape[1] // 4
    dconv = dconv.reshape(8, 2, 2, nb).transpose(0, 2, 1, 3).reshape(8, 4 * nb)
    return dx_in, dict(dsh=dsh, dsc=dsc, dgate=dgate, dpre=dpre, dpost=dpost, g_up=g_up, g_down=g_down,
                       d_dw_w=dconv[0:FFN_W], d_dw_b=dconv[3:4])


def _local_step(x, tgt, mods, P):
    D = x.shape[1]
    m0, m1 = mods
    h1 = prenorm(x, P["pre_mix_g"][0:1], m0[1], m0[0], name="hgrn_prenorm")
    proj = mm_nn(h1, P["hgrn_w_in"], name="hgrn_in")
    o, states = hgrn_scan(proj, P["hgrn_lb_logits"], name="hgrn_scan")
    og = hgrn_gate(o, proj, P["hgrn_gnorm_g"], name="hgrn_gate")
    y1 = mm_nn(og, P["hgrn_w_out"], name="hgrn_out")
    x1 = post_residual(x, y1, P["post_mix_g"][0:1], m0[2], name="hgrn_post")
    x2, ffn0 = _ffn_forward(x1, m0[3:6], P["pre_ffn_g"][0:1], P["post_ffn_g"][0:1], P["ffn_w_up"][0],
                            P["ffn_w_down"][0], P["ffn_dw_w"][0], P["ffn_dw_b"][0:1], "ffn0")
    h3 = prenorm(x2, P["pre_mix_g"][1:2], m1[1], m1[0], name="conv_prenorm")
    u = mm_nn(h3, P["conv_w_in"], name="conv_in", bias=P["conv_b_in"])
    s, cv = conv_act(u, P["conv_dw_w"], P["conv_dw_b"], P["conv_ln_g"], P["conv_ln_b"], name="conv_act")
    y3 = mm_nn(s, P["conv_w_out"], name="conv_out", bias=P["conv_b_out"])
    x3 = post_residual(x2, y3, P["post_mix_g"][1:2], m1[2], name="conv_post")
    x4, ffn1 = _ffn_forward(x3, m1[3:6], P["pre_ffn_g"][1:2], P["post_ffn_g"][1:2], P["ffn_w_up"][1],
                            P["ffn_w_down"][1], P["ffn_dw_w"][1], P["ffn_dw_b"][1:2], "ffn1")
    dx4, lcols = loss_grad(x4, tgt, name="loss")
    dx3, f1 = _ffn_backward(dx4, ffn1, m1[3:6], P["pre_ffn_g"][1:2], P["post_ffn_g"][1:2], P["ffn_w_up"][1],
                            P["ffn_w_down"][1], P["ffn_dw_w"][1], P["ffn_dw_b"][1:2], "ffn1")
    dy3, dg1_1, dpostmix1, d_b_out = post_bwd(dx3, y3, P["post_mix_g"][1:2], m1[2], name="conv_post_bwd")
    ds = mm_nt(dy3, P["conv_w_out"], name="conv_out_dx")
    g_conv_out = mm_tn(s, dy3, name="conv_out_dw", J=1, block="a", chips_per_block=4)
    dcv, d_ln_g, d_ln_b, d_dw_b = conv_norm_bwd(ds, cv, P["conv_ln_g"], P["conv_ln_b"], name="conv_norm_bwd")
    du, d_dw_w, d_b_in = conv_glu_bwd(dcv, u, P["conv_dw_w"], name="conv_glu_bwd")
    dh3 = mm_nt(du, P["conv_w_in"], name="conv_in_dx")
    g_conv_in = mm_tn(h3, du, name="conv_in_dw", J=4, block="b", chips_per_block=1)
    dx2, dsh1_1, dsc1_1, dpremix1 = prenorm_bwd(dh3, x2, dx3, P["pre_mix_g"][1:2], m1[1], name="conv_prenorm_bwd")
    dx1, f0 = _ffn_backward(dx2, ffn0, m0[3:6], P["pre_ffn_g"][0:1], P["post_ffn_g"][0:1], P["ffn_w_up"][0],
                            P["ffn_w_down"][0], P["ffn_dw_w"][0], P["ffn_dw_b"][0:1], "ffn0")
    dy1, dg1_0, dpostmix0, _ = post_bwd(dx1, y1, P["post_mix_g"][0:1], m0[2], name="hgrn_post_bwd")
    dog = mm_nt(dy1, P["hgrn_w_out"], name="hgrn_out_dx")
    g_hgrn_out = mm_tn(og, dy1, name="hgrn_out_dw", J=1, block="a", chips_per_block=4)
    do, dgp, d_gn = hgrn_gate_bwd(dog, o, proj, P["hgrn_gnorm_g"], name="hgrn_gate_bwd")
    dqp, dfz, dv, dlb = hgrn_scan_bwd(proj, P["hgrn_lb_logits"], states, do, name="hgrn_scan_bwd")
    dproj = jnp.concatenate([dqp, dfz, dv, dgp], axis=1)
    dh1 = mm_nt(dproj, P["hgrn_w_in"], name="hgrn_in_dx")
    g_hgrn_in = mm_tn(h1, dproj, name="hgrn_in_dw", J=4, block="b", chips_per_block=1)
    dx0, dsh1_0, dsc1_0, dpremix0 = prenorm_bwd(dh1, x, dx1, P["pre_mix_g"][0:1], m0[1], name="hgrn_prenorm_bwd")

    dmod = jnp.stack([
        jnp.concatenate([dsh1_0, dsc1_0, dg1_0, f0["dsh"], f0["dsc"], f0["dgate"]], axis=1)[0],
        jnp.concatenate([dsh1_1, dsc1_1, dg1_1, f1["dsh"], f1["dsc"], f1["dgate"]], axis=1)[0]])
    small = dict(
        loss=lcols,
        pre_mix_g=jnp.concatenate([dpremix0, dpremix1]), post_mix_g=jnp.concatenate([dpostmix0, dpostmix1]),
        pre_ffn_g=jnp.concatenate([f0["dpre"], f1["dpre"]]), post_ffn_g=jnp.concatenate([f0["dpost"], f1["dpost"]]),
        lb=dlb, hgrn_gnorm_g=d_gn, ffn_dw_b=jnp.concatenate([f0["d_dw_b"], f1["d_dw_b"]]), dmod=dmod,
        conv_b_in=d_b_in, conv_dw_w=d_dw_w[0:CONV_W], conv_dw_b=d_dw_b, conv_ln_g=d_ln_g, conv_ln_b=d_ln_b,
        conv_b_out=d_b_out, ffn_dw_w=jnp.stack([f0["d_dw_w"], f1["d_dw_w"]]))
    big = [g_hgrn_in, g_hgrn_out, g_conv_in, g_conv_out, f0["g_up"], f1["g_up"], f0["g_down"], f1["g_down"]]
    return dx0, small, big


def _pack(parts, rows=8):
    flat = jnp.concatenate([p.reshape(-1).astype(F32) for p in parts])
    per = rows * 128
    pad = (-flat.shape[0]) % per
    return jnp.pad(flat, (0, pad)).reshape(rows, -1)


def _unpack(flat, shapes):
    out, off = [], 0
    for s in shapes:
        n = 1
        for d in s:
            n *= d
        out.append(flat[..., off:off + n].reshape(flat.shape[:-1] + tuple(s)))
        off += n
    return out


def _from_chips(stacked, axis):
    moved = jnp.moveaxis(stacked, 0, axis)
    shape = list(moved.shape)
    return moved.reshape(shape[:axis] + [shape[axis] * shape[axis + 1]] + shape[axis + 2:])


def _my_shard(full, axis, q):
    n = full.shape[axis] // N_CHIPS
    return lax.dynamic_slice_in_dim(full, q * n, n, axis=axis)


def kernel(x, c, ada_w, ada_b, pre_mix_g, post_mix_g, pre_ffn_g, post_ffn_g, hgrn_w_in, hgrn_lb_logits, hgrn_gnorm_g, hgrn_w_out, conv_w_in, conv_b_in, conv_dw_w, conv_dw_b, conv_ln_g, conv_ln_b, conv_w_out, conv_b_out, ffn_w_up, ffn_dw_w, ffn_dw_b, ffn_w_down, loss_target, m_ada_w, m_ada_b, m_pre_mix_g, m_post_mix_g, m_pre_ffn_g, m_post_ffn_g, m_hgrn_w_in, m_hgrn_lb_logits, m_hgrn_gnorm_g, m_hgrn_w_out, m_conv_w_in, m_conv_b_in, m_conv_dw_w, m_conv_dw_b, m_conv_ln_g, m_conv_ln_b, m_conv_w_out, m_conv_b_out, m_ffn_w_up, m_ffn_dw_w, m_ffn_dw_b, m_ffn_w_down, v_ada_w, v_ada_b, v_pre_mix_g, v_post_mix_g, v_pre_ffn_g, v_post_ffn_g, v_hgrn_w_in, v_hgrn_lb_logits, v_hgrn_gnorm_g, v_hgrn_w_out, v_conv_w_in, v_conv_b_in, v_conv_dw_w, v_conv_dw_b, v_conv_ln_g, v_conv_ln_b, v_conv_w_out, v_conv_b_out, v_ffn_w_up, v_ffn_dw_w, v_ffn_dw_b, v_ffn_w_down):
    W = dict(ada_w=ada_w, ada_b=ada_b, pre_mix_g=pre_mix_g, post_mix_g=post_mix_g, pre_ffn_g=pre_ffn_g,
             post_ffn_g=post_ffn_g, hgrn_w_in=hgrn_w_in, hgrn_lb_logits=hgrn_lb_logits, hgrn_gnorm_g=hgrn_gnorm_g,
             hgrn_w_out=hgrn_w_out, conv_w_in=conv_w_in, conv_b_in=conv_b_in, conv_dw_w=conv_dw_w,
             conv_dw_b=conv_dw_b, conv_ln_g=conv_ln_g, conv_ln_b=conv_ln_b, conv_w_out=conv_w_out,
             conv_b_out=conv_b_out, ffn_w_up=ffn_w_up, ffn_dw_w=ffn_dw_w, ffn_dw_b=ffn_dw_b, ffn_w_down=ffn_w_down)
    M = dict(ada_w=m_ada_w, ada_b=m_ada_b, pre_mix_g=m_pre_mix_g, post_mix_g=m_post_mix_g, pre_ffn_g=m_pre_ffn_g,
             post_ffn_g=m_post_ffn_g, hgrn_w_in=m_hgrn_w_in, hgrn_lb_logits=m_hgrn_lb_logits,
             hgrn_gnorm_g=m_hgrn_gnorm_g, hgrn_w_out=m_hgrn_w_out, conv_w_in=m_conv_w_in, conv_b_in=m_conv_b_in,
             conv_dw_w=m_conv_dw_w, conv_dw_b=m_conv_dw_b, conv_ln_g=m_conv_ln_g, conv_ln_b=m_conv_ln_b,
             conv_w_out=m_conv_w_out, conv_b_out=m_conv_b_out, ffn_w_up=m_ffn_w_up, ffn_dw_w=m_ffn_dw_w,
             ffn_dw_b=m_ffn_dw_b, ffn_w_down=m_ffn_w_down)
    V = dict(ada_w=v_ada_w, ada_b=v_ada_b, pre_mix_g=v_pre_mix_g, post_mix_g=v_post_mix_g, pre_ffn_g=v_pre_ffn_g,
             post_ffn_g=v_post_ffn_g, hgrn_w_in=v_hgrn_w_in, hgrn_lb_logits=v_hgrn_lb_logits,
             hgrn_gnorm_g=v_hgrn_gnorm_g, hgrn_w_out=v_hgrn_w_out, conv_w_in=v_conv_w_in, conv_b_in=v_conv_b_in,
             conv_dw_w=v_conv_dw_w, conv_dw_b=v_conv_dw_b, conv_ln_g=v_conv_ln_g, conv_ln_b=v_conv_ln_b,
             conv_w_out=v_conv_w_out, conv_b_out=v_conv_b_out, ffn_w_up=v_ffn_w_up, ffn_dw_w=v_ffn_dw_w,
             ffn_dw_b=v_ffn_dw_b, ffn_w_down=v_ffn_w_down)
    names = list(W)
    xi, yi, ci = lax.axis_index("x"), lax.axis_index("y"), lax.axis_index("c")
    q = 2 * xi + yi
    me = 2 * q + ci
    D = x.shape[-1]
    L = ada_w.shape[0]

    small_w = ["conv_b_in", "conv_dw_w", "conv_dw_b", "conv_ln_g", "conv_ln_b", "conv_b_out", "ffn_dw_w"]
    small_axis = dict(conv_b_in=1, conv_dw_w=2, conv_dw_b=1, conv_ln_g=1, conv_ln_b=1, conv_b_out=1, ffn_dw_w=2)
    packed = _pack([c] + [W[n] for n in small_w])
    gathered = allgather_devices(packed, name="gather_small_params").reshape(N_DEV, -1)
    parts = _unpack(gathered, [c.shape] + [W[n].shape for n in small_w])
    c_all = parts[0][:, 0, :]
    P = {n: _from_chips(p[0::2], small_axis[n]) for n, p in zip(small_w, parts[1:])}
    P["conv_dw_w"] = P["conv_dw_w"][0]
    for n in ("pre_mix_g", "post_mix_g", "pre_ffn_g", "post_ffn_g", "hgrn_lb_logits", "hgrn_gnorm_g", "ffn_dw_b"):
        P[n] = W[n]

    modp = ada_mod(c_all, ada_w, name="ada_mod")
    ncol = modp.shape[-1]
    mod_all = allgather_devices(modp.reshape(L * N_DEV, ncol), name="gather_mod")
    mod_all = mod_all[0::2].reshape(N_CHIPS, L, N_DEV, ncol)
    mod_me = lax.dynamic_index_in_dim(mod_all, me, axis=2, keepdims=False)
    mod = mod_me.transpose(1, 0, 2).reshape(L, N_CHIPS * ncol) + ada_b
    mods = [tuple(mod[l:l + 1, k * D:(k + 1) * D] for k in range(6)) for l in range(L)]

    def halves(w):
        shard = w.astype(BF16).reshape(1, 2, w.shape[0] // 2, w.shape[1])
        buf = lax.empty((N_CHIPS,) + shard.shape[1:], BF16)
        return lax.dynamic_update_slice_in_dim(buf, shard, q, axis=0)

    big_shards = [hgrn_w_in[0], hgrn_w_out[0], conv_w_in[0], conv_w_out[0], ffn_w_up[0], ffn_w_up[1],
                  ffn_w_down[0], ffn_w_down[1]]
    g = allgather_chips([halves(w) for w in big_shards], name="gather_weights")
    stack = lambda t: t.reshape(N_CHIPS, t.shape[1] * t.shape[2], t.shape[3])
    rowsh = lambda t: t.reshape(1, N_CHIPS * t.shape[1] * t.shape[2], t.shape[3])
    P["hgrn_w_in"], P["hgrn_w_out"] = stack(g[0]), rowsh(g[1])
    P["conv_w_in"], P["conv_w_out"] = stack(g[2]), rowsh(g[3])
    P["ffn_w_up"] = [stack(g[4]), stack(g[5])]
    P["ffn_w_down"] = [rowsh(g[6]), rowsh(g[7])]

    grad_x, small, big = _local_step(x[0], loss_target[0], mods, P)

    small_names = list(small)
    gs = allgather_devices(_pack([small[n] for n in small_names]), name="gather_small_grads")
    dmod_all = _unpack(gs.reshape(N_DEV, -1), [small[n].shape for n in small_names])[small_names.index("dmod")]
    tot = sum_devices(gs, name="sum_small_grads").reshape(1, -1)
    S = dict(zip(small_names, _unpack(tot, [small[n].shape for n in small_names])))
    S = {n: v[0] for n, v in S.items()}
    loss = 0.5 * jnp.sum(S["loss"]) / D

    G = {}
    dmod_q = lax.dynamic_slice_in_dim(dmod_all, q * ncol, ncol, axis=2)
    G["ada_w"] = ada_wgrad(c_all.T, dmod_q.transpose(1, 0, 2), name="ada_wgrad")
    G["ada_b"] = S["dmod"]
    for n in ("pre_mix_g", "post_mix_g", "pre_ffn_g", "post_ffn_g", "hgrn_gnorm_g", "ffn_dw_b"):
        G[n] = S[n]
    G["hgrn_lb_logits"] = lb_logits_grad(hgrn_lb_logits, S["lb"], name="lb_logits_grad")
    G["conv_b_in"] = _my_shard(S["conv_b_in"], 1, q)
    G["conv_dw_w"] = _my_shard(S["conv_dw_w"], 1, q)[None]
    for n in ("conv_dw_b", "conv_ln_g", "conv_ln_b", "conv_b_out"):
        G[n] = _my_shard(S[n], 1, q)
    G["ffn_dw_w"] = _my_shard(S["ffn_dw_w"], 2, q)

    red = reduce_weight_grads(big, q, ci)
    G["hgrn_w_in"], G["hgrn_w_out"], G["conv_w_in"], G["conv_w_out"] = red[0][None], red[1][None], red[2][None], red[3][None]
    G["ffn_w_up"] = jnp.stack([red[4], red[5]])
    G["ffn_w_down"] = jnp.stack([red[6], red[7]])

    delta, new_m, new_v = {}, {}, {}
    big_names = ["ada_w", "hgrn_w_in", "hgrn_w_out", "conv_w_in", "conv_w_out", "ffn_w_up", "ffn_w_down"]
    for n in big_names:
        shp = W[n].shape
        two = lambda t: t.reshape(-1, shp[-1])
        d_, m_, v_ = adamw(two(W[n]), two(G[n]), two(M[n]), two(V[n]), name=f"adamw_{n}")
        delta[n], new_m[n], new_v[n] = d_.reshape(shp), m_.reshape(shp), v_.reshape(shp)
    rest = [n for n in names if n not in big_names]
    d_, m_, v_ = adamw(_pack([W[n] for n in rest]), _pack([G[n] for n in rest]), _pack([M[n] for n in rest]),
                       _pack([V[n] for n in rest]), name="adamw_small")
    shapes = [W[n].shape for n in rest]
    for n, a, b_, c_ in zip(rest, _unpack(d_.reshape(-1), shapes), _unpack(m_.reshape(-1), shapes),
                            _unpack(v_.reshape(-1), shapes)):
        delta[n], new_m[n], new_v[n] = a, b_, c_

    return (loss, grad_x[None], *[G[n].reshape(W[n].shape) for n in names], *[delta[n] for n in names],
            *[new_m[n] for n in names], *[new_v[n] for n in names])
```

```python
import jax
import jax.numpy as jnp
from jax import lax
from jax.experimental import pallas as pl
from jax.experimental.pallas import tpu as pltpu

F32 = jnp.float32
BF16 = jnp.bfloat16
EPS = 1e-6
HEAD = 128
BLK = 16
NEG = -1e30
CONV_W = 31
FFN_W = 3
N_CHIPS = 4
N_DEV = 8
SUB = 8
LANE = 128
V7X_VMEM_LIMIT = 56 * 1024 * 1024
MESH = pl.DeviceIdType.MESH
HBM = pl.BlockSpec(memory_space=pltpu.HBM)
VMEM_SPEC = pl.BlockSpec(memory_space=pltpu.VMEM)

ADAM_LR = 0.001
ADAM_B1 = 0.9
ADAM_B2 = 0.999
ADAM_EPS = 1e-08
ADAM_WD = 0.01
ADAM_STEP = 10


def _cp(*sem):
    return pltpu.CompilerParams(dimension_semantics=sem, vmem_limit_bytes=V7X_VMEM_LIMIT)


def _sig(x):
    return 0.5 * jnp.tanh(0.5 * x) + 0.5


def _silu(x):
    return x * _sig(x)


def _dsilu(x):
    s = _sig(x)
    return s * (1.0 + x * (1.0 - s))


def _dot(a, b):
    return jnp.dot(a, b, preferred_element_type=F32)


def _dot_nt(a, b):
    return lax.dot_general(a, b, (((1,), (1,)), ((), ())), preferred_element_type=F32)


def _dot_tn(a, b):
    return lax.dot_general(a, b, (((0,), (0,)), ((), ())), preferred_element_type=F32)


def _colsum(x):
    return jnp.sum(x, axis=0, keepdims=True)


def _rowmean(x):
    return jnp.mean(x, axis=-1, keepdims=True)


def _ffn_perm(j):
    return (j % 2) * 2 + j // 2


def _tile(n, pref):
    if n <= pref:
        return n
    t = pref - pref % 8
    while n % t:
        t -= 8
    return t


def mm_nn(a, w, *, name, bias=None, out_dtype=F32, perm=None, tm=512):
    T, K = a.shape
    J, _, nb = w.shape
    tm = min(tm, T)
    col = (lambda j: j) if perm is None else perm

    def body(a_ref, w_ref, *rest):
        acc = _dot(a_ref[...], w_ref[...])
        if bias is not None:
            acc = acc + rest[0][...]
        rest[-1][...] = acc.astype(out_dtype)

    in_specs = [pl.BlockSpec((tm, K), lambda j, i: (i, 0)), pl.BlockSpec((None, K, nb), lambda j, i: (j, 0, 0))]
    args = [a, w]
    if bias is not None:
        in_specs.append(pl.BlockSpec((1, nb), lambda j, i: (0, j)))
        args.append(bias)
    return pl.pallas_call(
        body, grid=(J, T // tm), in_specs=in_specs,
        out_specs=pl.BlockSpec((tm, nb), lambda j, i: (i, col(j))),
        out_shape=jax.ShapeDtypeStruct((T, J * nb), out_dtype), name=name,
        compiler_params=_cp("parallel", "parallel"))(*args)


def mm_nt(a, w, *, name, out_dtype=F32, perm=None, tm=512):
    T = a.shape[0]
    J, K, nb = w.shape
    tm = min(tm, T)
    col = (lambda j: j) if perm is None else perm

    def body(a_ref, w_ref, o_ref, acc_ref):
        j = pl.program_id(1)

        @pl.when(j == 0)
        def _():
            acc_ref[...] = jnp.zeros_like(acc_ref)

        acc_ref[...] += _dot_nt(a_ref[...], w_ref[...])

        @pl.when(j == J - 1)
        def _():
            o_ref[...] = acc_ref[...].astype(out_dtype)

    return pl.pallas_call(
        body, grid=(T // tm, J),
        in_specs=[pl.BlockSpec((tm, nb), lambda i, j: (i, col(j))), pl.BlockSpec((None, K, nb), lambda i, j: (j, 0, 0))],
        out_specs=pl.BlockSpec((tm, K), lambda i, j: (i, 0)),
        out_shape=jax.ShapeDtypeStruct((T, K), out_dtype),
        scratch_shapes=[pltpu.VMEM((tm, K), F32)], name=name,
        compiler_params=_cp("parallel", "arbitrary"))(a, w)


def mm_tn(a, b, *, name, J, block, chips_per_block, perm=None, tk=512):
    T = a.shape[0]
    tk = min(tk, T)
    col = (lambda j: j) if perm is None else perm
    if block == "b":
        rows, nb = a.shape[1], b.shape[1] // J
        a_spec = pl.BlockSpec((tk, rows), lambda j, t: (t, 0))
        b_spec = pl.BlockSpec((tk, nb), lambda j, t: (t, col(j)))
    else:
        rows, nb = a.shape[1] // J, b.shape[1]
        a_spec = pl.BlockSpec((tk, rows), lambda j, t: (t, col(j)))
        b_spec = pl.BlockSpec((tk, nb), lambda j, t: (t, 0))
    cpb = chips_per_block
    rh = rows // (2 * cpb)

    def body(a_ref, b_ref, o_ref):
        @pl.when(pl.program_id(1) == 0)
        def _():
            o_ref[...] = jnp.zeros_like(o_ref)

        acc = _dot_tn(a_ref[...], b_ref[...])
        for ch in range(cpb):
            for hf in range(2):
                r0 = (ch * 2 + hf) * rh
                o_ref[hf, ch] += acc[r0:r0 + rh, :]

    return pl.pallas_call(
        body, grid=(J, T // tk), in_specs=[a_spec, b_spec],
        out_specs=pl.BlockSpec((2, cpb, rh, nb), lambda j, t: (0, j, 0, 0)),
        out_shape=jax.ShapeDtypeStruct((2, J * cpb, rh, nb), F32), name=name,
        compiler_params=_cp("parallel", "arbitrary"))(a, b)


def _row(tm, w):
    return pl.BlockSpec((tm, w), lambda i: (i, 0))


def _full(r, w):
    return pl.BlockSpec((r, w), lambda i: (0, 0))


def _acc_init(i, *refs):
    @pl.when(i == 0)
    def _():
        for r in refs:
            r[...] = jnp.zeros_like(r)


def prenorm(x, g, sc, sh, *, name, tm=512):
    T, D = x.shape
    tm = min(tm, T)

    def body(x_ref, g_ref, sc_ref, sh_ref, h_ref):
        xv = x_ref[...]
        r = lax.rsqrt(_rowmean(xv * xv) + EPS)
        h_ref[...] = ((xv * r) * g_ref[...] * (1.0 + sc_ref[...]) + sh_ref[...]).astype(BF16)

    return pl.pallas_call(
        body, grid=(T // tm,), in_specs=[_row(tm, D), _full(1, D), _full(1, D), _full(1, D)],
        out_specs=_row(tm, D), out_shape=jax.ShapeDtypeStruct((T, D), BF16), name=name,
        compiler_params=_cp("parallel"))(x, g, sc, sh)


def post_residual(x, y, g, gate, *, name, tm=512):
    T, D = x.shape
    tm = min(tm, T)

    def body(x_ref, y_ref, g_ref, gate_ref, o_ref):
        yv = y_ref[...]
        r = lax.rsqrt(_rowmean(yv * yv) + EPS)
        o_ref[...] = x_ref[...] + gate_ref[...] * ((yv * r) * g_ref[...])

    return pl.pallas_call(
        body, grid=(T // tm,), in_specs=[_row(tm, D), _row(tm, D), _full(1, D), _full(1, D)],
        out_specs=_row(tm, D), out_shape=jax.ShapeDtypeStruct((T, D), F32), name=name,
        compiler_params=_cp("parallel"))(x, y, g, gate)


def loss_grad(x, tgt, *, name, tm=512):
    T, D = x.shape
    tm = min(tm, T)

    def body(x_ref, t_ref, dx_ref, l_ref):
        _acc_init(pl.program_id(0), l_ref)
        e = x_ref[...] - t_ref[...]
        dx_ref[...] = e * (1.0 / D)
        l_ref[...] += _colsum(e * e)

    return pl.pallas_call(
        body, grid=(T // tm,), in_specs=[_row(tm, D), _row(tm, D)],
        out_specs=[_row(tm, D), _full(1, D)],
        out_shape=[jax.ShapeDtypeStruct((T, D), F32), jax.ShapeDtypeStruct((1, D), F32)], name=name,
        compiler_params=_cp("arbitrary"))(x, tgt)


def post_bwd(dx, y, g, gate, *, name, tm=512):
    T, D = dx.shape
    tm = min(tm, T)

    def body(dx_ref, y_ref, g_ref, gate_ref, dy_ref, dgate_ref, dg_ref, dbias_ref):
        _acc_init(pl.program_id(0), dgate_ref, dg_ref, dbias_ref)
        yv = y_ref[...]
        dxv = dx_ref[...]
        r = lax.rsqrt(_rowmean(yv * yv) + EPS)
        yn = yv * r
        gv = g_ref[...]
        gt = gate_ref[...]
        dgate_ref[...] += _colsum(dxv * (yn * gv))
        dg_ref[...] += _colsum(dxv * gt * yn)
        dyn = dxv * gt * gv
        dy = r * (dyn - yn * _rowmean(dyn * yn))
        dbias_ref[...] += _colsum(dy)
        dy_ref[...] = dy.astype(BF16)

    return pl.pallas_call(
        body, grid=(T // tm,), in_specs=[_row(tm, D), _row(tm, D), _full(1, D), _full(1, D)],
        out_specs=[_row(tm, D), _full(1, D), _full(1, D), _full(1, D)],
        out_shape=[jax.ShapeDtypeStruct((T, D), BF16)] + [jax.ShapeDtypeStruct((1, D), F32)] * 3, name=name,
        compiler_params=_cp("arbitrary"))(dx, y, g, gate)


def prenorm_bwd(dh, x, dres, g, sc, *, name, tm=512):
    T, D = x.shape
    tm = min(tm, T)

    def body(dh_ref, x_ref, dres_ref, g_ref, sc_ref, dx_ref, dsh_ref, dsc_ref, dg_ref):
        _acc_init(pl.program_id(0), dsh_ref, dsc_ref, dg_ref)
        xv = x_ref[...]
        dhv = dh_ref[...]
        r = lax.rsqrt(_rowmean(xv * xv) + EPS)
        xn = xv * r
        gv = g_ref[...]
        one_sc = 1.0 + sc_ref[...]
        dsh_ref[...] += _colsum(dhv)
        dsc_ref[...] += _colsum(dhv * (xn * gv))
        dg_ref[...] += _colsum(dhv * one_sc * xn)
        dxn = dhv * one_sc * gv
        dx_ref[...] = dres_ref[...] + r * (dxn - xn * _rowmean(dxn * xn))

    return pl.pallas_call(
        body, grid=(T // tm,), in_specs=[_row(tm, D), _row(tm, D), _row(tm, D), _full(1, D), _full(1, D)],
        out_specs=[_row(tm, D), _full(1, D), _full(1, D), _full(1, D)],
        out_shape=[jax.ShapeDtypeStruct((T, D), F32)] + [jax.ShapeDtypeStruct((1, D), F32)] * 3, name=name,
        compiler_params=_cp("arbitrary"))(dh, x, dres, g, sc)


HALO = 16


def _shift_helpers():
    rid = lax.broadcasted_iota(jnp.int32, (SUB, LANE), 0)

    def down(cur, prev, k):
        return pltpu.roll(jnp.where(rid >= SUB - k, prev, cur), k, 0)

    def up(cur, nxt, k):
        return pltpu.roll(jnp.where(rid < k, nxt, cur), SUB - k, 0)

    return down, up


def _ffn_sides(c, nb, wa_ref, wb_ref, ba_ref, bb_ref):
    cols = slice(c * LANE, (c + 1) * LANE)
    return [(cols, [wa_ref[k:k + 1, cols] for k in range(FFN_W)], ba_ref[:, cols]),
            (slice(nb + c * LANE, nb + (c + 1) * LANE), [wb_ref[k:k + 1, cols] for k in range(FFN_W)],
             bb_ref[:, cols])]


def _ffn_specs(tm, nb, hb, idx):
    return [pl.BlockSpec((tm, 2 * nb), lambda jc, i: (idx(i), jc)),
            pl.BlockSpec((HALO, 2 * nb), lambda jc, i: (jnp.maximum(idx(i) * hb - 1, 0), jc)),
            pl.BlockSpec((FFN_W, nb), lambda jc, i: (0, jc)),
            pl.BlockSpec((FFN_W, nb), lambda jc, i: (0, jc + 2)),
            pl.BlockSpec((1, nb), lambda jc, i: (0, jc)),
            pl.BlockSpec((1, nb), lambda jc, i: (0, jc + 2))]


def ffn_act(u0p, dw_w, dw_b, *, name, tm=256):
    T, W = u0p.shape
    nb = W // 4
    tm = min(tm, T)
    unroll = 4
    rows16 = 2 * SUB

    def body(u_ref, halo_ref, wa_ref, wb_ref, ba_ref, bb_ref, z_ref):
        i = pl.program_id(1)
        down, _ = _shift_helpers()
        for c in range(nb // LANE):
            cols = slice(c * LANE, (c + 1) * LANE)
            side = _ffn_sides(c, nb, wa_ref, wb_ref, ba_ref, bb_ref)

            def rows(j, prev):
                prev = list(prev)
                for m in range(unroll):
                    r0 = pl.multiple_of((j * unroll + m) * rows16, rows16)
                    x = [u_ref[pl.ds(r0, rows16), cs].astype(F32) for cs, _, _ in side]
                    zs = []
                    for hf in range(2):
                        conv = []
                        for n, (_, w, b) in enumerate(side):
                            cur = x[n][hf * SUB:(hf + 1) * SUB, :]
                            conv.append(b + w[2] * cur + w[1] * down(cur, prev[n], 1) + w[0] * down(cur, prev[n], 2))
                            prev[n] = cur
                        zs.append(_silu(conv[0]) * conv[1])
                    z_ref[pl.ds(r0, rows16), cols] = jnp.concatenate(zs, axis=0).astype(BF16)
                return tuple(prev)

            first = [jnp.where(i == 0, 0.0, halo_ref[:, cs].astype(F32)[SUB:2 * SUB, :]) for cs, _, _ in side]
            lax.fori_loop(0, tm // (rows16 * unroll), rows, tuple(first))

    return pl.pallas_call(
        body, grid=(2, T // tm), in_specs=_ffn_specs(tm, nb, tm // HALO, lambda i: i),
        out_specs=pl.BlockSpec((tm, nb), lambda jc, i: (i, jc)),
        out_shape=jax.ShapeDtypeStruct((T, 2 * nb), BF16), name=name,
        compiler_params=_cp("parallel", "arbitrary"))(u0p, u0p, dw_w, dw_w, dw_b, dw_b)


def ffn_act_bwd(dz, u0p, dw_w, dw_b, *, name, tm=256):
    T, W = u0p.shape
    nb = W // 4
    tm = min(tm, T)
    nt = T // tm
    unroll = 2
    rows16 = 2 * SUB
    n_it = tm // (rows16 * unroll)

    def body(dz_ref, u_ref, halo_ref, wa_ref, wb_ref, ba_ref, bb_ref, du0_ref, dw_ref, carry):
        i = pl.program_id(1)
        _acc_init(i, dw_ref)
        down, up = _shift_helpers()
        for c in range(nb // LANE):
            cols = slice(c * LANE, (c + 1) * LANE)
            side = _ffn_sides(c, nb, wa_ref, wb_ref, ba_ref, bb_ref)
            halo = [jnp.where(i == nt - 1, 0.0, halo_ref[:, cs].astype(F32)[SUB:2 * SUB, :]) for cs, _, _ in side]

            def rows(j, st):
                nxt, acc, x = list(st[0:2]), list(st[2:10]), list(st[10:12])
                for m in range(unroll):
                    r0 = pl.multiple_of(((n_it - 1 - j) * unroll + unroll - 1 - m) * rows16, rows16)
                    rp = pl.multiple_of(jnp.maximum(r0 - rows16, 0), rows16)
                    dzv = dz_ref[pl.ds(r0, rows16), cols].astype(F32)
                    chunk = []
                    for n, (cs, _, _) in enumerate(side):
                        before = u_ref[pl.ds(rp, rows16), cs].astype(F32)
                        chunk.append([jnp.where(r0 == 0, halo[n], before[SUB:2 * SUB, :]), x[n][0:SUB, :],
                                      x[n][SUB:2 * SUB, :]])
                        x[n] = before
                    out = [[None, None], [None, None]]
                    for hf in (1, 0):
                        cur = [chunk[n][hf + 1] for n in range(2)]
                        s1 = [down(cur[n], chunk[n][hf], 1) for n in range(2)]
                        s2 = [down(cur[n], chunk[n][hf], 2) for n in range(2)]
                        a, b = [side[n][2] + side[n][1][2] * cur[n] + side[n][1][1] * s1[n] + side[n][1][0] * s2[n]
                                for n in range(2)]
                        sa = _sig(a)
                        dzh = dzv[hf * SUB:(hf + 1) * SUB, :]
                        d = [dzh * b * (sa * (1.0 + a * (1.0 - sa))), dzh * (a * sa)]
                        for n in range(2):
                            w = side[n][1]
                            acc[4 * n + 0] = acc[4 * n + 0] + d[n] * s2[n]
                            acc[4 * n + 1] = acc[4 * n + 1] + d[n] * s1[n]
                            acc[4 * n + 2] = acc[4 * n + 2] + d[n] * cur[n]
                            acc[4 * n + 3] = acc[4 * n + 3] + d[n]
                            out[n][hf] = w[2] * d[n] + w[1] * up(d[n], nxt[n], 1) + w[0] * up(d[n], nxt[n], 2)
                            nxt[n] = d[n]
                    for n in range(2):
                        du0_ref[pl.ds(r0, rows16), side[n][0]] = jnp.concatenate(out[n], axis=0).astype(BF16)
                return (*nxt, *acc, *x)

            init = ([jnp.where(i == 0, 0.0, carry[:, cs]) for cs, _, _ in side] + [jnp.zeros((SUB, LANE), F32)] * 8
                    + [u_ref[tm - rows16:tm, cs].astype(F32) for cs, _, _ in side])
            st = lax.fori_loop(0, n_it, rows, tuple(init))
            for n in range(2):
                carry[:, side[n][0]] = st[n]
                for k in range(4):
                    dw_ref[k, :, side[n][0]] += st[2 + 4 * n + k]

        @pl.when(i == nt - 1)
        def _():
            for k in range(4):
                dw_ref[k, 0:1, :] = _colsum(dw_ref[k])

    rev = lambda i: nt - 1 - i
    return pl.pallas_call(
        body, grid=(2, nt),
        in_specs=[pl.BlockSpec((tm, nb), lambda jc, i: (rev(i), jc))] + _ffn_specs(tm, nb, tm // HALO, rev),
        out_specs=[pl.BlockSpec((tm, 2 * nb), lambda jc, i: (rev(i), jc)),
                   pl.BlockSpec((4, SUB, 2 * nb), lambda jc, i: (0, 0, jc))],
        out_shape=[jax.ShapeDtypeStruct((T, W), BF16), jax.ShapeDtypeStruct((4, SUB, W), F32)],
        scratch_shapes=[pltpu.VMEM((SUB, 2 * nb), F32)], name=name,
        compiler_params=_cp("parallel", "arbitrary"))(dz, u0p, u0p, dw_w, dw_w, dw_b, dw_b)


CHALO = 32
CCOL = 256


def _phase_copies(buf, shifted, tm):
    n = tm + CHALO - SUB
    for p in range(1, SUB):
        shifted[p - 1, 0:n, :] = buf[p:p + n, :]


def _shifted(buf, shifted, r, tm, c0):
    m, p = divmod(r, SUB)
    src = buf if p == 0 else shifted.at[p - 1]
    return src[m * SUB:m * SUB + tm, c0:c0 + CCOL]


def conv_act(u, dw_w, dw_b, ln_g, ln_b, *, name, tm=128):
    T, D2 = u.shape
    D = D2 // 2
    tm = min(tm, T)
    hb = tm // CHALO

    def body(u_ref, halo_ref, w_ref, b_ref, g_ref, be_ref, s_ref, cv_ref, gbuf, gsh):
        i = pl.program_id(0)
        hv = halo_ref[...]
        gbuf[0:CHALO, :] = jnp.where(i == 0, 0.0, hv[:, 0:D] * _sig(hv[:, D:D2]))
        uv = u_ref[...]
        gbuf[CHALO:CHALO + tm, :] = uv[:, 0:D] * _sig(uv[:, D:D2])
        _phase_copies(gbuf, gsh, tm)
        for c0 in range(0, D, CCOL):
            acc = jnp.zeros((tm, CCOL), F32) + b_ref[:, c0:c0 + CCOL]
            for k in range(CONV_W):
                acc = acc + w_ref[k:k + 1, c0:c0 + CCOL] * _shifted(gbuf, gsh, CHALO - (CONV_W - 1) + k, tm, c0)
            cv_ref[:, c0:c0 + CCOL] = acc
        cv = cv_ref[...]
        mu = _rowmean(cv)
        xc = cv - mu
        nh = xc * lax.rsqrt(_rowmean(xc * xc) + EPS)
        s_ref[...] = _silu(nh * g_ref[...] + be_ref[...]).astype(BF16)

    return pl.pallas_call(
        body, grid=(T // tm,),
        in_specs=[_row(tm, D2), pl.BlockSpec((CHALO, D2), lambda i: (jnp.maximum(i * hb - 1, 0), 0)),
                  _full(CONV_W, D), _full(1, D), _full(1, D), _full(1, D)],
        out_specs=[_row(tm, D), _row(tm, D)],
        out_shape=[jax.ShapeDtypeStruct((T, D), BF16), jax.ShapeDtypeStruct((T, D), F32)],
        scratch_shapes=[pltpu.VMEM((tm + CHALO, D), F32), pltpu.VMEM((SUB - 1, tm + CHALO, D), F32)], name=name,
        compiler_params=_cp("arbitrary"))(u, u, dw_w, dw_b, ln_g, ln_b)


def conv_norm_bwd(ds, cv, ln_g, ln_b, *, name, tm=512):
    T, D = cv.shape
    tm = min(tm, T)

    def body(ds_ref, cv_ref, g_ref, be_ref, dcv_ref, dg_ref, dbe_ref, dcb_ref):
        _acc_init(pl.program_id(0), dg_ref, dbe_ref, dcb_ref)
        cv_ = cv_ref[...]
        mu = _rowmean(cv_)
        xc = cv_ - mu
        rstd = lax.rsqrt(_rowmean(xc * xc) + EPS)
        nh = xc * rstd
        gv = g_ref[...]
        dln = ds_ref[...] * _dsilu(nh * gv + be_ref[...])
        dg_ref[...] += _colsum(dln * nh)
        dbe_ref[...] += _colsum(dln)
        dnh = dln * gv
        dcv = rstd * (dnh - _rowmean(dnh) - nh * _rowmean(dnh * nh))
        dcb_ref[...] += _colsum(dcv)
        dcv_ref[...] = dcv

    return pl.pallas_call(
        body, grid=(T // tm,), in_specs=[_row(tm, D), _row(tm, D), _full(1, D), _full(1, D)],
        out_specs=[_row(tm, D), _full(1, D), _full(1, D), _full(1, D)],
        out_shape=[jax.ShapeDtypeStruct((T, D), F32)] + [jax.ShapeDtypeStruct((1, D), F32)] * 3, name=name,
        compiler_params=_cp("arbitrary"))(ds, cv, ln_g, ln_b)


def conv_glu_bwd(dcv, u, dw_w, *, name, tm=128):
    T, D2 = u.shape
    D = D2 // 2
    tm = min(tm, T)
    nt = T // tm
    hb = tm // CHALO

    def body(dcv_ref, dnext_ref, u_ref, halo_ref, w_ref, du_ref, dw_ref, dbin_ref, gbuf, dbuf, gsh, dsh):
        i = pl.program_id(0)
        _acc_init(i, dw_ref, dbin_ref)
        hv = halo_ref[...]
        gbuf[0:CHALO, :] = jnp.where(i == 0, 0.0, hv[:, 0:D] * _sig(hv[:, D:D2]))
        uv = u_ref[...]
        av = uv[:, 0:D]
        sg = _sig(uv[:, D:D2])
        gbuf[CHALO:CHALO + tm, :] = av * sg
        dbuf[0:tm, :] = dcv_ref[...]
        dbuf[tm:tm + CHALO, :] = jnp.where(i == nt - 1, 0.0, dnext_ref[...])
        _phase_copies(gbuf, gsh, tm)
        _phase_copies(dbuf, dsh, tm)
        for c0 in range(0, D, CCOL):
            dc = dbuf[0:tm, c0:c0 + CCOL]
            acc = jnp.zeros((tm, CCOL), F32)
            for k in range(CONV_W):
                dw_ref[k:k + 1, c0:c0 + CCOL] += _colsum(dc * _shifted(gbuf, gsh, CHALO - (CONV_W - 1) + k, tm, c0))
                acc = acc + w_ref[k:k + 1, c0:c0 + CCOL] * _shifted(dbuf, dsh, CONV_W - 1 - k, tm, c0)
            a_c = av[:, c0:c0 + CCOL]
            s_c = sg[:, c0:c0 + CCOL]
            da = acc * s_c
            dgt = acc * a_c * s_c * (1.0 - s_c)
            dbin_ref[:, c0:c0 + CCOL] += _colsum(da)
            dbin_ref[:, D + c0:D + c0 + CCOL] += _colsum(dgt)
            du_ref[:, c0:c0 + CCOL] = da.astype(BF16)
            du_ref[:, D + c0:D + c0 + CCOL] = dgt.astype(BF16)

    return pl.pallas_call(
        body, grid=(nt,),
        in_specs=[_row(tm, D), pl.BlockSpec((CHALO, D), lambda i: (jnp.minimum((i + 1) * hb, T // CHALO - 1), 0)),
                  _row(tm, D2), pl.BlockSpec((CHALO, D2), lambda i: (jnp.maximum(i * hb - 1, 0), 0)),
                  _full(CONV_W, D)],
        out_specs=[_row(tm, D2), _full(CHALO, D), _full(1, D2)],
        out_shape=[jax.ShapeDtypeStruct((T, D2), BF16), jax.ShapeDtypeStruct((CHALO, D), F32),
                   jax.ShapeDtypeStruct((1, D2), F32)],
        scratch_shapes=[pltpu.VMEM((tm + CHALO, D), F32)] * 2 + [pltpu.VMEM((SUB - 1, tm + CHALO, D), F32)] * 2,
        name=name, compiler_params=_cp("arbitrary"))(dcv, dcv, u, u, dw_w)


HB = 4


def _lb0(lg_ref):
    l0, l1, l2 = lg_ref[0:1, :], lg_ref[1:2, :], lg_ref[2:3, :]
    m = jnp.maximum(jnp.maximum(l0, l1), l2)
    e0 = jnp.exp(l0 - m)
    return e0 / (e0 + jnp.exp(l1 - m) + jnp.exp(l2 - m))


def _mm_exact(m01, x):
    hi = x.astype(BF16)
    r1 = x - hi.astype(F32)
    mid = r1.astype(BF16)
    lo = (r1 - mid.astype(F32)).astype(BF16)
    return _dot(m01, hi) + _dot(m01, mid) + _dot(m01, lo)


def _block_tri(tm):
    r = jnp.arange(tm)[:, None]
    c = jnp.arange(tm)[None, :]
    same = (r // BLK) == (c // BLK)
    return (same & (c <= r)).astype(BF16), (same & (c >= r)).astype(BF16)


def _const_spec(shape):
    return pl.BlockSpec(shape, lambda h, i: (0, 0))


def _hgrn_specs(H, hb, tm, idx):
    g = H // hb
    return [pl.BlockSpec((tm, hb * HEAD), lambda h, i: (idx(i), h)),
            pl.BlockSpec((tm, hb * HEAD), lambda h, i: (idx(i), g + h)),
            pl.BlockSpec((tm, hb * HEAD), lambda h, i: (idx(i), 2 * g + h)),
            pl.BlockSpec((3, hb * HEAD), lambda h, i: (0, h))]


def hgrn_scan(proj, lb_logits, *, name, tm=128):
    T = proj.shape[0]
    H = proj.shape[1] // (4 * HEAD)
    hb = min(HB, H)
    tm = min(tm, T)
    nt = T // tm
    nblk = tm // BLK
    tril, _ = _block_tri(tm)
    heads = [slice(hh * HEAD, (hh + 1) * HEAD) for hh in range(hb)]

    def body(qp_ref, fz_ref, v_ref, lg_ref, tril_ref, o_ref, st_ref, S_ref, q_s, k_s, b_s):
        @pl.when(pl.program_id(1) == 0)
        def _():
            S_ref[...] = jnp.zeros_like(S_ref)

        st_ref[...] = S_ref[...]
        lb = _lb0(lg_ref)
        f = lb + (1.0 - lb) * _sig(fz_ref[...])
        q_s[...] = _silu(qp_ref[...])
        k_s[...] = 1.0 - f
        b_s[...] = _mm_exact(tril_ref[...], jnp.log(f))
        rows = lax.broadcasted_iota(jnp.int32, (BLK, HEAD), 0)
        S = [S_ref[hh] for hh in range(hb)]
        for nb in range(nblk):
            blk = slice(nb * BLK, (nb + 1) * BLK)
            last = slice(nb * BLK + BLK - 1, nb * BLK + BLK)
            qb = [q_s[blk, c] for c in heads]
            bb = [b_s[blk, c] for c in heads]
            o = [_dot_nt((qb[hh] * jnp.exp(bb[hh])).astype(BF16), S[hh].astype(BF16)) for hh in range(hb)]
            for hh, c in enumerate(heads):
                bc = b_s[last, c]
                kd = k_s[blk, c] * jnp.exp(bc - bb[hh])
                S[hh] = S[hh] * jnp.exp(bc) + _dot_tn(v_ref[blk, c].astype(BF16), kd.astype(BF16))
            for s in range(BLK):
                r = slice(nb * BLK + s, nb * BLK + s + 1)
                for hh, c in enumerate(heads):
                    dec = jnp.exp(jnp.where(rows >= s, bb[hh] - b_s[r, c], NEG))
                    a = jnp.sum(qb[hh] * k_s[r, c] * dec, axis=-1, keepdims=True)
                    o[hh] = o[hh] + a * v_ref[r, c]
            for hh, c in enumerate(heads):
                o_ref[blk, c] = o[hh]
        for hh in range(hb):
            S_ref[hh] = S[hh]

    return pl.pallas_call(
        body, grid=(H // hb, nt),
        in_specs=_hgrn_specs(H, hb, tm, lambda i: i) + [_const_spec((tm, tm))],
        out_specs=[pl.BlockSpec((tm, hb * HEAD), lambda h, i: (i, h)),
                   pl.BlockSpec((None, hb, HEAD, HEAD), lambda h, i: (i, h, 0, 0))],
        out_shape=[jax.ShapeDtypeStruct((T, H * HEAD), F32), jax.ShapeDtypeStruct((nt, H, HEAD, HEAD), F32)],
        scratch_shapes=[pltpu.VMEM((hb, HEAD, HEAD), F32)] + [pltpu.VMEM((tm, hb * HEAD), F32)] * 3, name=name,
        compiler_params=_cp("parallel", "arbitrary"))(proj, proj, proj, lb_logits, tril)


def hgrn_scan_bwd(proj, lb_logits, states, do, *, name, tm=128):
    T = proj.shape[0]
    H = proj.shape[1] // (4 * HEAD)
    hb = min(HB, H)
    tm = min(tm, T)
    nt = T // tm
    nblk = tm // BLK
    tril, triu = _block_tri(tm)
    heads = [slice(hh * HEAD, (hh + 1) * HEAD) for hh in range(hb)]

    def body(qp_ref, fz_ref, v_ref, lg_ref, st_ref, do_ref, tril_ref, triu_ref, dqp_ref, dfz_ref, dv_ref, dlb_ref,
             dS_ref, Sb_ref, q_s, k_s, b_s, dq_s, dk_s, dv_s, db_s):
        i = pl.program_id(1)

        @pl.when(i == 0)
        def _():
            dS_ref[...] = jnp.zeros_like(dS_ref)
            dlb_ref[...] = jnp.zeros_like(dlb_ref)

        lb = _lb0(lg_ref)
        qp = qp_ref[...]
        sg = _sig(fz_ref[...])
        f = lb + (1.0 - lb) * sg
        q_s[...] = _silu(qp)
        k_s[...] = 1.0 - f
        b_s[...] = _mm_exact(tril_ref[...], jnp.log(f))
        rows = lax.broadcasted_iota(jnp.int32, (BLK, HEAD), 0)
        rows1 = lax.broadcasted_iota(jnp.int32, (BLK, 1), 0)

        S = [st_ref[hh] for hh in range(hb)]
        for nb in range(nblk):
            blk = slice(nb * BLK, (nb + 1) * BLK)
            last = slice(nb * BLK + BLK - 1, nb * BLK + BLK)
            for hh, c in enumerate(heads):
                Sb_ref[nb * hb + hh] = S[hh]
                if nb < nblk - 1:
                    bc = b_s[last, c]
                    kd = k_s[blk, c] * jnp.exp(bc - b_s[blk, c])
                    S[hh] = S[hh] * jnp.exp(bc) + _dot_tn(v_ref[blk, c].astype(BF16), kd.astype(BF16))

        dS = [dS_ref[hh] for hh in range(hb)]
        for nb in reversed(range(nblk)):
            blk = slice(nb * BLK, (nb + 1) * BLK)
            last = slice(nb * BLK + BLK - 1, nb * BLK + BLK)
            qb, kb, bb, dob, dq, dbc, ebc = [], [], [], [], [], [], []
            for hh, c in enumerate(heads):
                S0 = Sb_ref[nb * hb + hh]
                qb.append(q_s[blk, c])
                kb.append(k_s[blk, c])
                bb.append(b_s[blk, c])
                dob.append(do_ref[blk, c])
                bc = b_s[last, c]
                eb = jnp.exp(bb[hh])
                ekd = jnp.exp(bc - bb[hh])
                ebc.append(jnp.exp(bc))
                dS16 = dS[hh].astype(BF16)
                dob16 = dob[hh].astype(BF16)
                dq.append(_dot(dob16, S0.astype(BF16)) * eb)
                dki = _dot(v_ref[blk, c].astype(BF16), dS16) * ekd
                dk_s[blk, c] = dki
                dv_s[blk, c] = _dot_nt((kb[hh] * ekd).astype(BF16), dS16)
                dbc.append(_colsum(dS[hh] * S0) * ebc[hh] + _colsum(kb[hh] * dki))
                dS[hh] = dS[hh] * ebc[hh] + _dot_tn(dob16, (qb[hh] * eb).astype(BF16))
            for s in range(BLK):
                r = slice(nb * BLK + s, nb * BLK + s + 1)
                for hh, c in enumerate(heads):
                    ks = k_s[r, c]
                    dec = jnp.exp(jnp.where(rows >= s, bb[hh] - b_s[r, c], NEG))
                    w = qb[hh] * dec
                    a = jnp.sum(w * ks, axis=-1, keepdims=True)
                    da = jnp.where(rows1 >= s, jnp.sum(dob[hh] * v_ref[r, c], axis=-1, keepdims=True), 0.0)
                    dq[hh] = dq[hh] + (da * ks) * dec
                    dk_s[r, c] += _colsum(da * w)
                    dv_s[r, c] += _colsum(a * dob[hh])
            for hh, c in enumerate(heads):
                dq_s[blk, c] = dq[hh]
                db_s[blk, c] = qb[hh] * dq[hh] - kb[hh] * dk_s[blk, c]
                db_s[last, c] += dbc[hh]
        for hh in range(hb):
            dS_ref[hh] = dS[hh]

        dlf = _mm_exact(triu_ref[...], db_s[...])
        df = dlf / f - dk_s[...]
        dfz_ref[...] = (df * (1.0 - lb) * sg * (1.0 - sg)).astype(BF16)
        dlb_ref[...] += _colsum(df * (1.0 - sg))
        dqp_ref[...] = (dq_s[...] * _dsilu(qp)).astype(BF16)
        dv_ref[...] = dv_s[...].astype(BF16)

    rev = lambda i: nt - 1 - i
    out_blk = pl.BlockSpec((tm, hb * HEAD), lambda h, i: (rev(i), h))
    return pl.pallas_call(
        body, grid=(H // hb, nt),
        in_specs=_hgrn_specs(H, hb, tm, rev) + [pl.BlockSpec((None, hb, HEAD, HEAD), lambda h, i: (rev(i), h, 0, 0)),
                                                out_blk, _const_spec((tm, tm)), _const_spec((tm, tm))],
        out_specs=[out_blk, out_blk, out_blk, pl.BlockSpec((1, hb * HEAD), lambda h, i: (0, h))],
        out_shape=[jax.ShapeDtypeStruct((T, H * HEAD), BF16)] * 3 + [jax.ShapeDtypeStruct((1, H * HEAD), F32)],
        scratch_shapes=[pltpu.VMEM((hb, HEAD, HEAD), F32), pltpu.VMEM((nblk * hb, HEAD, HEAD), F32)]
        + [pltpu.VMEM((tm, hb * HEAD), F32)] * 7, name=name,
        compiler_params=_cp("parallel", "arbitrary"))(proj, proj, proj, lb_logits, states, do, tril, triu)


def hgrn_gate(o, proj, gn, *, name, tm=512):
    T, D = o.shape
    H = D // HEAD
    tm = min(tm, T)

    def body(o_ref, gp_ref, gn_ref, og_ref):
        gn_ = gn_ref[...]
        for h in range(H):
            c = slice(h * HEAD, (h + 1) * HEAD)
            oh = o_ref[:, c]
            r = lax.rsqrt(_rowmean(oh * oh) + EPS)
            og_ref[:, c] = ((oh * r) * gn_ * _silu(gp_ref[:, c])).astype(BF16)

    return pl.pallas_call(
        body, grid=(T // tm,),
        in_specs=[_row(tm, D), pl.BlockSpec((tm, D), lambda i: (i, 3)), _full(1, HEAD)],
        out_specs=_row(tm, D), out_shape=jax.ShapeDtypeStruct((T, D), BF16), name=name,
        compiler_params=_cp("parallel"))(o, proj, gn)


def hgrn_gate_bwd(dog, o, proj, gn, *, name, tm=512):
    T, D = o.shape
    H = D // HEAD
    tm = min(tm, T)

    def body(dog_ref, o_ref, gp_ref, gn_ref, do_ref, dgp_ref, dgn_ref):
        _acc_init(pl.program_id(0), dgn_ref)
        gn_ = gn_ref[...]
        for h in range(H):
            c = slice(h * HEAD, (h + 1) * HEAD)
            oh = o_ref[:, c]
            gp = gp_ref[:, c]
            dg = dog_ref[:, c]
            r = lax.rsqrt(_rowmean(oh * oh) + EPS)
            on = oh * r
            dgp_ref[:, c] = (dg * (on * gn_) * _dsilu(gp)).astype(BF16)
            don = dg * _silu(gp)
            dgn_ref[...] += _colsum(don * on)
            dn = don * gn_
            do_ref[:, c] = r * (dn - on * _rowmean(dn * on))

    return pl.pallas_call(
        body, grid=(T // tm,),
        in_specs=[_row(tm, D), _row(tm, D), pl.BlockSpec((tm, D), lambda i: (i, 3)), _full(1, HEAD)],
        out_specs=[_row(tm, D), _row(tm, D), _full(1, HEAD)],
        out_shape=[jax.ShapeDtypeStruct((T, D), F32), jax.ShapeDtypeStruct((T, D), BF16),
                   jax.ShapeDtypeStruct((1, HEAD), F32)], name=name,
        compiler_params=_cp("arbitrary"))(dog, o, proj, gn)


def _split2(x):
    hi = x.astype(BF16)
    return hi, (x - hi.astype(F32)).astype(BF16)


def ada_mod(c_all, ada_w, *, name):
    L, D, N = ada_w.shape
    B = c_all.shape[0]

    def body(c_ref, w_ref, o_ref):
        chi, clo = _split2(_silu(c_ref[...]))
        whi, wlo = _split2(w_ref[...])
        o_ref[...] = _dot(chi, whi) + _dot(chi, wlo) + _dot(clo, whi)

    return pl.pallas_call(
        body, grid=(L,), in_specs=[_full(B, D), pl.BlockSpec((None, D, N), lambda l: (l, 0, 0))],
        out_specs=pl.BlockSpec((None, B, N), lambda l: (l, 0, 0)),
        out_shape=jax.ShapeDtypeStruct((L, B, N), F32), name=name, compiler_params=_cp("parallel"))(c_all, ada_w)


def ada_wgrad(c_all_t, dmod, *, name, tr=256):
    D, B = c_all_t.shape
    L, _, N = dmod.shape
    tr = min(tr, D)

    def body(c_ref, d_ref, o_ref):
        cond = _silu(c_ref[...])
        acc = cond[:, 0:1] * d_ref[0:1, :]
        for b in range(1, B):
            acc = acc + cond[:, b:b + 1] * d_ref[b:b + 1, :]
        o_ref[...] = acc

    return pl.pallas_call(
        body, grid=(L, D // tr),
        in_specs=[pl.BlockSpec((tr, B), lambda l, r: (r, 0)), pl.BlockSpec((None, B, N), lambda l, r: (l, 0, 0))],
        out_specs=pl.BlockSpec((None, tr, N), lambda l, r: (l, r, 0)),
        out_shape=jax.ShapeDtypeStruct((L, D, N), F32), name=name,
        compiler_params=_cp("parallel", "parallel"))(c_all_t, dmod)


def sum_devices(parts, *, name):
    n, R, C = parts.shape

    def body(p_ref, o_ref):
        acc = p_ref[0]
        for d in range(1, n):
            acc = acc + p_ref[d]
        o_ref[...] = acc

    return pl.pallas_call(body, in_specs=[VMEM_SPEC], out_specs=VMEM_SPEC,
                          out_shape=jax.ShapeDtypeStruct((R, C), F32), name=name)(parts)


def lb_logits_grad(lb_logits, dlb, *, name):
    def body(lg_ref, d_ref, o_ref):
        l0, l1, l2 = lg_ref[0:1, :], lg_ref[1:2, :], lg_ref[2:3, :]
        m = jnp.maximum(jnp.maximum(l0, l1), l2)
        e0, e1, e2 = jnp.exp(l0 - m), jnp.exp(l1 - m), jnp.exp(l2 - m)
        z = e0 + e1 + e2
        p0, p1, p2 = e0 / z, e1 / z, e2 / z
        g = d_ref[...] * p0
        o_ref[0:1, :] = g * (1.0 - p0)
        o_ref[1:2, :] = -g * p1
        o_ref[2:3, :] = -g * p2

    return pl.pallas_call(body, in_specs=[VMEM_SPEC, VMEM_SPEC], out_specs=VMEM_SPEC,
                          out_shape=jax.ShapeDtypeStruct(lb_logits.shape, F32), name=name)(lb_logits, dlb)


def adamw(w, g, m, v, *, name, tr=256):
    R, C = w.shape
    tr = _tile(R, tr)

    def body(w_ref, g_ref, m_ref, v_ref, d_ref, nm_ref, nv_ref):
        gv = g_ref[...]
        nm = ADAM_B1 * m_ref[...] + (1.0 - ADAM_B1) * gv
        nv = ADAM_B2 * v_ref[...] + (1.0 - ADAM_B2) * (gv * gv)
        m_hat = nm / (1.0 - ADAM_B1 ** ADAM_STEP)
        v_hat = nv / (1.0 - ADAM_B2 ** ADAM_STEP)
        d_ref[...] = -ADAM_LR * (m_hat / (jnp.sqrt(v_hat) + ADAM_EPS) + ADAM_WD * w_ref[...])
        nm_ref[...] = nm
        nv_ref[...] = nv

    spec = pl.BlockSpec((tr, C), lambda i: (i, 0))
    return pl.pallas_call(
        body, grid=(R // tr,), in_specs=[spec] * 4, out_specs=[spec] * 3,
        out_shape=[jax.ShapeDtypeStruct((R, C), F32)] * 3, name=name, compiler_params=_cp("parallel"))(w, g, m, v)


def _place():
    return lax.axis_index("x"), lax.axis_index("y"), lax.axis_index("c")


def _flip(v, bit):
    return 1 - v if bit else v


def allgather_devices(v, *, name):
    R, C = v.shape

    def body(v_ref, out_ref, send_sems, recv_sems, local_sem):
        x, y, c = _place()
        me = 4 * x + 2 * y + c
        mine = pltpu.make_async_copy(v_ref, out_ref.at[me], local_sem)
        mine.start()
        sends = []
        for k in range(1, N_DEV):
            peer = (_flip(x, k & 4), _flip(y, k & 2), _flip(c, k & 1))
            cp = pltpu.make_async_remote_copy(src_ref=v_ref, dst_ref=out_ref.at[me], send_sem=send_sems.at[k - 1],
                                              recv_sem=recv_sems.at[k - 1], device_id=peer, device_id_type=MESH)
            cp.start()
            sends.append(cp)
        for k in range(1, N_DEV):
            px, py, pc = _flip(x, k & 4), _flip(y, k & 2), _flip(c, k & 1)
            pltpu.make_async_remote_copy(src_ref=v_ref, dst_ref=out_ref.at[4 * px + 2 * py + pc],
                                         send_sem=send_sems.at[k - 1], recv_sem=recv_sems.at[k - 1],
                                         device_id=(px, py, pc), device_id_type=MESH).wait_recv()
        for cp in sends:
            cp.wait_send()
        mine.wait()

    return pl.pallas_call(
        body, in_specs=[VMEM_SPEC], out_specs=VMEM_SPEC, out_shape=jax.ShapeDtypeStruct((N_DEV, R, C), v.dtype),
        scratch_shapes=[pltpu.SemaphoreType.DMA((N_DEV - 1,)), pltpu.SemaphoreType.DMA((N_DEV - 1,)),
                        pltpu.SemaphoreType.DMA], name=name)(v)


def _other_chips(x, y):
    return [(1 - x, y), (x, 1 - y), (1 - x, 1 - y)]


def allgather_chips(bufs, *, name):
    n = len(bufs)

    def body(*refs):
        outs = refs[n:2 * n]
        send_sems, recv_sems = refs[2 * n:]
        x, y, c = _place()
        q = 2 * x + y
        chips = _other_chips(x, y)

        def copy(a, k, block, half, to):
            slab = outs[a].at[block, half]
            return pltpu.make_async_remote_copy(src_ref=slab, dst_ref=slab, send_sem=send_sems.at[a, k],
                                                recv_sem=recv_sems.at[a, k], device_id=to, device_id_type=MESH)

        first = [copy(a, j, q, c, (*chips[j], c)) for a in range(n) for j in range(3)]
        for cp in first:
            cp.start()
        passed = []
        for a in range(n):
            for j, (px, py) in enumerate(chips):
                copy(a, j, 2 * px + py, c, (x, y, c)).wait_recv()
                fw = copy(a, 3 + j, 2 * px + py, c, (x, y, 1 - c))
                fw.start()
                passed.append(fw)
        for a in range(n):
            for j, (px, py) in enumerate(chips):
                copy(a, 3 + j, 2 * px + py, 1 - c, (x, y, c)).wait_recv()
        for cp in first + passed:
            cp.wait_send()

    return pl.pallas_call(
        body, in_specs=[HBM] * n, out_specs=[HBM] * n,
        out_shape=[jax.ShapeDtypeStruct(b.shape, b.dtype) for b in bufs],
        input_output_aliases={a: a for a in range(n)},
        scratch_shapes=[pltpu.SemaphoreType.DMA((n, 6)), pltpu.SemaphoreType.DMA((n, 6))], name=name)(*bufs)


def pair_exchange(grads, *, name):
    n = len(grads)

    def body(*refs):
        ins, outs = refs[:n], refs[n:2 * n]
        send_sems, recv_sems = refs[2 * n:]
        x, y, c = _place()
        cps = [pltpu.make_async_remote_copy(src_ref=ins[a].at[1 - c], dst_ref=outs[a], send_sem=send_sems.at[a],
                                            recv_sem=recv_sems.at[a], device_id=(x, y, 1 - c), device_id_type=MESH)
               for a in range(n)]
        for cp in cps:
            cp.start()
        for cp in cps:
            cp.wait_recv()
        for cp in cps:
            cp.wait_send()

    return pl.pallas_call(
        body, in_specs=[HBM] * n, out_specs=[HBM] * n,
        out_shape=[jax.ShapeDtypeStruct(g.shape[1:], g.dtype) for g in grads],
        scratch_shapes=[pltpu.SemaphoreType.DMA((n,)), pltpu.SemaphoreType.DMA((n,))], name=name)(*grads)


def pair_add(g, other, c_idx, *, name, tr=256):
    _, Q, R, C = g.shape
    tr = _tile(R, tr)

    def body(c_ref, g_ref, o_ref, out_ref):
        out_ref[...] = (g_ref[...] + o_ref[...]).astype(BF16)

    return pl.pallas_call(
        body,
        grid_spec=pltpu.PrefetchScalarGridSpec(
            num_scalar_prefetch=1, grid=(Q, R // tr),
            in_specs=[pl.BlockSpec((None, None, tr, C), lambda q, r, c_ref: (c_ref[0], q, r, 0)),
                      pl.BlockSpec((None, tr, C), lambda q, r, c_ref: (q, r, 0))],
            out_specs=pl.BlockSpec((None, tr, C), lambda q, r, c_ref: (q, r, 0))),
        out_shape=jax.ShapeDtypeStruct((Q, R, C), BF16), name=name,
        compiler_params=_cp("parallel", "parallel"))(c_idx, g, other)


def chip_exchange(sums, *, name):
    n = len(sums)

    def body(*refs):
        ins, outs = refs[:n], refs[n:2 * n]
        send_sems, recv_sems = refs[2 * n:]
        x, y, c = _place()
        chips = _other_chips(x, y)
        sends = [pltpu.make_async_remote_copy(src_ref=ins[a].at[2 * px + py], dst_ref=outs[a].at[j],
                                              send_sem=send_sems.at[a, j], recv_sem=recv_sems.at[a, j],
                                              device_id=(px, py, c), device_id_type=MESH)
                 for a in range(n) for j, (px, py) in enumerate(chips)]
        for cp in sends:
            cp.start()
        for cp in sends:
            cp.wait_recv()
        for cp in sends:
            cp.wait_send()

    return pl.pallas_call(
        body, in_specs=[HBM] * n, out_specs=[HBM] * n,
        out_shape=[jax.ShapeDtypeStruct((3,) + s.shape[1:], s.dtype) for s in sums],
        scratch_shapes=[pltpu.SemaphoreType.DMA((n, 3)), pltpu.SemaphoreType.DMA((n, 3))], name=name)(*sums)


def chip_sum(sums, landed, qc_idx, *, name, tr=256):
    _, R, C = sums.shape
    tr = _tile(R, tr)

    def body(qc_ref, own_ref, l_ref, o_ref):
        acc = own_ref[...].astype(F32)
        for k in range(3):
            acc = acc + l_ref[k].astype(F32)
        o_ref[...] = acc

    return pl.pallas_call(
        body,
        grid_spec=pltpu.PrefetchScalarGridSpec(
            num_scalar_prefetch=1, grid=(R // tr,),
            in_specs=[pl.BlockSpec((None, tr, C), lambda r, qc: (qc[0], r, 0)),
                      pl.BlockSpec((3, tr, C), lambda r, qc: (0, r, 0))],
            out_specs=pl.BlockSpec((None, tr, C), lambda r, qc: (qc[1], r, 0))),
        out_shape=jax.ShapeDtypeStruct((2, R, C), F32), name=name,
        compiler_params=_cp("parallel"))(qc_idx, sums, landed)


def half_swap(bufs, *, name):
    n = len(bufs)

    def body(*refs):
        outs = refs[n:2 * n]
        send_sems, recv_sems = refs[2 * n:]
        x, y, c = _place()
        cps = [pltpu.make_async_remote_copy(src_ref=outs[a].at[c], dst_ref=outs[a].at[c], send_sem=send_sems.at[a],
                                            recv_sem=recv_sems.at[a], device_id=(x, y, 1 - c), device_id_type=MESH)
               for a in range(n)]
        for cp in cps:
            cp.start()
        for a in range(n):
            pltpu.make_async_remote_copy(src_ref=outs[a].at[c], dst_ref=outs[a].at[1 - c], send_sem=send_sems.at[a],
                                         recv_sem=recv_sems.at[a], device_id=(x, y, 1 - c),
                                         device_id_type=MESH).wait_recv()
        for cp in cps:
            cp.wait_send()

    return pl.pallas_call(
        body, in_specs=[HBM] * n, out_specs=[HBM] * n,
        out_shape=[jax.ShapeDtypeStruct(b.shape, b.dtype) for b in bufs],
        input_output_aliases={a: a for a in range(n)},
        scratch_shapes=[pltpu.SemaphoreType.DMA((n,)), pltpu.SemaphoreType.DMA((n,))], name=name)(*bufs)


def reduce_weight_grads(grads, q, c):
    c_idx = c.astype(jnp.int32).reshape(1)
    qc_idx = jnp.stack([q, c]).astype(jnp.int32)
    others = pair_exchange(grads, name="grad_pair_exchange")
    sums = [pair_add(g, o, c_idx, name=f"grad_pair_add_{a}") for a, (g, o) in enumerate(zip(grads, others))]
    landed = chip_exchange(sums, name="grad_chip_exchange")
    halves = [chip_sum(s, l, qc_idx, name=f"grad_chip_sum_{a}") for a, (s, l) in enumerate(zip(sums, landed))]
    full = half_swap(halves, name="grad_half_swap")
    return [f.reshape(2 * f.shape[1], f.shape[2]) for f in full]


def _ffn_forward(x, mod, pre_g, post_g, w_up, w_down, dw_w, dw_b, tag):
    sh, sc, gate = mod
    h = prenorm(x, pre_g, sc, sh, name=f"{tag}_prenorm")
    u0 = mm_nn(h, w_up, name=f"{tag}_up", out_dtype=BF16, perm=_ffn_perm)
    z = ffn_act(u0, dw_w, dw_b, name=f"{tag}_act")
    y = mm_nn(z, w_down, name=f"{tag}_down")
    x_new = post_residual(x, y, post_g, gate, name=f"{tag}_post")
    return x_new, (x, h, u0, z, y)


def _ffn_backward(dx, saved, mod, pre_g, post_g, w_up, w_down, dw_w, dw_b, tag):
    x, h, u0, z, y = saved
    sh, sc, gate = mod
    dy, dgate, dpost, _ = post_bwd(dx, y, post_g, gate, name=f"{tag}_post_bwd")
    dz = mm_nt(dy, w_down, name=f"{tag}_down_dx", out_dtype=BF16)
    g_down = mm_tn(z, dy, name=f"{tag}_down_dw", J=2, block="a", chips_per_block=2)
    du0, dconv = ffn_act_bwd(dz, u0, dw_w, dw_b, name=f"{tag}_act_bwd")
    dh = mm_nt(du0, w_up, name=f"{tag}_up_dx", perm=_ffn_perm)
    g_up = mm_tn(h, du0, name=f"{tag}_up_dw", J=4, block="b", chips_per_block=1, perm=_ffn_perm)
    dx_in, dsh, dsc, dpre = prenorm_bwd(dh, x, dx, pre_g, sc, name=f"{tag}_prenorm_bwd")
    nb = u0.shape[1] // 4
    dconv = dconv[:, 0].reshape(4, 2, 2, nb).transpose(0, 2, 1, 3).reshape(4, 4 * nb)
    return dx_in, dict(dsh=dsh, dsc=dsc, dgate=dgate, dpre=dpre, dpost=dpost, g_up=g_up, g_down=g_down,
                       d_dw_w=dconv[0:FFN_W], d_dw_b=dconv[3:4])


def _local_step(x, tgt, mods, P):
    m0, m1 = mods
    h1 = prenorm(x, P["pre_mix_g"][0:1], m0[1], m0[0], name="hgrn_prenorm")
    proj = mm_nn(h1, P["hgrn_w_in"], name="hgrn_in")
    o, states = hgrn_scan(proj, P["hgrn_lb_logits"], name="hgrn_scan")
    og = hgrn_gate(o, proj, P["hgrn_gnorm_g"], name="hgrn_gate")
    y1 = mm_nn(og, P["hgrn_w_out"], name="hgrn_out")
    x1 = post_residual(x, y1, P["post_mix_g"][0:1], m0[2], name="hgrn_post")
    x2, ffn0 = _ffn_forward(x1, m0[3:6], P["pre_ffn_g"][0:1], P["post_ffn_g"][0:1], P["ffn_w_up"][0],
                            P["ffn_w_down"][0], P["ffn_dw_w"][0], P["ffn_dw_b"][0:1], "ffn0")
    h3 = prenorm(x2, P["pre_mix_g"][1:2], m1[1], m1[0], name="conv_prenorm")
    u = mm_nn(h3, P["conv_w_in"], name="conv_in", bias=P["conv_b_in"])
    s, cv = conv_act(u, P["conv_dw_w"], P["conv_dw_b"], P["conv_ln_g"], P["conv_ln_b"], name="conv_act")
    y3 = mm_nn(s, P["conv_w_out"], name="conv_out", bias=P["conv_b_out"])
    x3 = post_residual(x2, y3, P["post_mix_g"][1:2], m1[2], name="conv_post")
    x4, ffn1 = _ffn_forward(x3, m1[3:6], P["pre_ffn_g"][1:2], P["post_ffn_g"][1:2], P["ffn_w_up"][1],
                            P["ffn_w_down"][1], P["ffn_dw_w"][1], P["ffn_dw_b"][1:2], "ffn1")
    dx4, lcols = loss_grad(x4, tgt, name="loss")
    dx3, f1 = _ffn_backward(dx4, ffn1, m1[3:6], P["pre_ffn_g"][1:2], P["post_ffn_g"][1:2], P["ffn_w_up"][1],
                            P["ffn_w_down"][1], P["ffn_dw_w"][1], P["ffn_dw_b"][1:2], "ffn1")
    dy3, dg1_1, dpostmix1, d_b_out = post_bwd(dx3, y3, P["post_mix_g"][1:2], m1[2], name="conv_post_bwd")
    ds = mm_nt(dy3, P["conv_w_out"], name="conv_out_dx")
    g_conv_out = mm_tn(s, dy3, name="conv_out_dw", J=1, block="a", chips_per_block=4)
    dcv, d_ln_g, d_ln_b, d_dw_b = conv_norm_bwd(ds, cv, P["conv_ln_g"], P["conv_ln_b"], name="conv_norm_bwd")
    du, d_dw_w, d_b_in = conv_glu_bwd(dcv, u, P["conv_dw_w"], name="conv_glu_bwd")
    dh3 = mm_nt(du, P["conv_w_in"], name="conv_in_dx")
    g_conv_in = mm_tn(h3, du, name="conv_in_dw", J=4, block="b", chips_per_block=1)
    dx2, dsh1_1, dsc1_1, dpremix1 = prenorm_bwd(dh3, x2, dx3, P["pre_mix_g"][1:2], m1[1], name="conv_prenorm_bwd")
    dx1, f0 = _ffn_backward(dx2, ffn0, m0[3:6], P["pre_ffn_g"][0:1], P["post_ffn_g"][0:1], P["ffn_w_up"][0],
                            P["ffn_w_down"][0], P["ffn_dw_w"][0], P["ffn_dw_b"][0:1], "ffn0")
    dy1, dg1_0, dpostmix0, _ = post_bwd(dx1, y1, P["post_mix_g"][0:1], m0[2], name="hgrn_post_bwd")
    dog = mm_nt(dy1, P["hgrn_w_out"], name="hgrn_out_dx")
    g_hgrn_out = mm_tn(og, dy1, name="hgrn_out_dw", J=1, block="a", chips_per_block=4)
    do, dgp, d_gn = hgrn_gate_bwd(dog, o, proj, P["hgrn_gnorm_g"], name="hgrn_gate_bwd")
    dqp, dfz, dv, dlb = hgrn_scan_bwd(proj, P["hgrn_lb_logits"], states, do, name="hgrn_scan_bwd")
    dproj = jnp.concatenate([dqp, dfz, dv, dgp], axis=1)
    dh1 = mm_nt(dproj, P["hgrn_w_in"], name="hgrn_in_dx")
    g_hgrn_in = mm_tn(h1, dproj, name="hgrn_in_dw", J=4, block="b", chips_per_block=1)
    dx0, dsh1_0, dsc1_0, dpremix0 = prenorm_bwd(dh1, x, dx1, P["pre_mix_g"][0:1], m0[1], name="hgrn_prenorm_bwd")

    dmod = jnp.stack([
        jnp.concatenate([dsh1_0, dsc1_0, dg1_0, f0["dsh"], f0["dsc"], f0["dgate"]], axis=1)[0],
        jnp.concatenate([dsh1_1, dsc1_1, dg1_1, f1["dsh"], f1["dsc"], f1["dgate"]], axis=1)[0]])
    small = dict(
        loss=lcols,
        pre_mix_g=jnp.concatenate([dpremix0, dpremix1]), post_mix_g=jnp.concatenate([dpostmix0, dpostmix1]),
        pre_ffn_g=jnp.concatenate([f0["dpre"], f1["dpre"]]), post_ffn_g=jnp.concatenate([f0["dpost"], f1["dpost"]]),
        lb=dlb, hgrn_gnorm_g=d_gn, ffn_dw_b=jnp.concatenate([f0["d_dw_b"], f1["d_dw_b"]]), dmod=dmod,
        conv_b_in=d_b_in, conv_dw_w=d_dw_w[0:CONV_W], conv_dw_b=d_dw_b, conv_ln_g=d_ln_g, conv_ln_b=d_ln_b,
        conv_b_out=d_b_out, ffn_dw_w=jnp.stack([f0["d_dw_w"], f1["d_dw_w"]]))
    big = [g_hgrn_in, g_hgrn_out, g_conv_in, g_conv_out, f0["g_up"], f1["g_up"], f0["g_down"], f1["g_down"]]
    return dx0, small, big


def _pack(parts, rows=8):
    flat = jnp.concatenate([p.reshape(-1).astype(F32) for p in parts])
    per = rows * 128
    pad = (-flat.shape[0]) % per
    return jnp.pad(flat, (0, pad)).reshape(rows, -1)


def _unpack(flat, shapes):
    out, off = [], 0
    for s in shapes:
        n = 1
        for d in s:
            n *= d
        out.append(flat[..., off:off + n].reshape(flat.shape[:-1] + tuple(s)))
        off += n
    return out


def _from_chips(stacked, axis):
    moved = jnp.moveaxis(stacked, 0, axis)
    shape = list(moved.shape)
    return moved.reshape(shape[:axis] + [shape[axis] * shape[axis + 1]] + shape[axis + 2:])


def _my_shard(full, axis, q):
    n = full.shape[axis] // N_CHIPS
    return lax.dynamic_slice_in_dim(full, q * n, n, axis=axis)


def kernel(x, c, ada_w, ada_b, pre_mix_g, post_mix_g, pre_ffn_g, post_ffn_g, hgrn_w_in, hgrn_lb_logits, hgrn_gnorm_g, hgrn_w_out, conv_w_in, conv_b_in, conv_dw_w, conv_dw_b, conv_ln_g, conv_ln_b, conv_w_out, conv_b_out, ffn_w_up, ffn_dw_w, ffn_dw_b, ffn_w_down, loss_target, m_ada_w, m_ada_b, m_pre_mix_g, m_post_mix_g, m_pre_ffn_g, m_post_ffn_g, m_hgrn_w_in, m_hgrn_lb_logits, m_hgrn_gnorm_g, m_hgrn_w_out, m_conv_w_in, m_conv_b_in, m_conv_dw_w, m_conv_dw_b, m_conv_ln_g, m_conv_ln_b, m_conv_w_out, m_conv_b_out, m_ffn_w_up, m_ffn_dw_w, m_ffn_dw_b, m_ffn_w_down, v_ada_w, v_ada_b, v_pre_mix_g, v_post_mix_g, v_pre_ffn_g, v_post_ffn_g, v_hgrn_w_in, v_hgrn_lb_logits, v_hgrn_gnorm_g, v_hgrn_w_out, v_conv_w_in, v_conv_b_in, v_conv_dw_w, v_conv_dw_b, v_conv_ln_g, v_conv_ln_b, v_conv_w_out, v_conv_b_out, v_ffn_w_up, v_ffn_dw_w, v_ffn_dw_b, v_ffn_w_down):
    W = dict(ada_w=ada_w, ada_b=ada_b, pre_mix_g=pre_mix_g, post_mix_g=post_mix_g, pre_ffn_g=pre_ffn_g,
             post_ffn_g=post_ffn_g, hgrn_w_in=hgrn_w_in, hgrn_lb_logits=hgrn_lb_logits, hgrn_gnorm_g=hgrn_gnorm_g,
             hgrn_w_out=hgrn_w_out, conv_w_in=conv_w_in, conv_b_in=conv_b_in, conv_dw_w=conv_dw_w,
             conv_dw_b=conv_dw_b, conv_ln_g=conv_ln_g, conv_ln_b=conv_ln_b, conv_w_out=conv_w_out,
             conv_b_out=conv_b_out, ffn_w_up=ffn_w_up, ffn_dw_w=ffn_dw_w, ffn_dw_b=ffn_dw_b, ffn_w_down=ffn_w_down)
    M = dict(ada_w=m_ada_w, ada_b=m_ada_b, pre_mix_g=m_pre_mix_g, post_mix_g=m_post_mix_g, pre_ffn_g=m_pre_ffn_g,
             post_ffn_g=m_post_ffn_g, hgrn_w_in=m_hgrn_w_in, hgrn_lb_logits=m_hgrn_lb_logits,
             hgrn_gnorm_g=m_hgrn_gnorm_g, hgrn_w_out=m_hgrn_w_out, conv_w_in=m_conv_w_in, conv_b_in=m_conv_b_in,
             conv_dw_w=m_conv_dw_w, conv_dw_b=m_conv_dw_b, conv_ln_g=m_conv_ln_g, conv_ln_b=m_conv_ln_b,
             conv_w_out=m_conv_w_out, conv_b_out=m_conv_b_out, ffn_w_up=m_ffn_w_up, ffn_dw_w=m_ffn_dw_w,
             ffn_dw_b=m_ffn_dw_b, ffn_w_down=m_ffn_w_down)
    V = dict(ada_w=v_ada_w, ada_b=v_ada_b, pre_mix_g=v_pre_mix_g, post_mix_g=v_post_mix_g, pre_ffn_g=v_pre_ffn_g,
             post_ffn_g=v_post_ffn_g, hgrn_w_in=v_hgrn_w_in, hgrn_lb_logits=v_hgrn_lb_logits,
             hgrn_gnorm_g=v_hgrn_gnorm_g, hgrn_w_out=v_hgrn_w_out, conv_w_in=v_conv_w_in, conv_b_in=v_conv_b_in,
             conv_dw_w=v_conv_dw_w, conv_dw_b=v_conv_dw_b, conv_ln_g=v_conv_ln_g, conv_ln_b=v_conv_ln_b,
             conv_w_out=v_conv_w_out, conv_b_out=v_conv_b_out, ffn_w_up=v_ffn_w_up, ffn_dw_w=v_ffn_dw_w,
             ffn_dw_b=v_ffn_dw_b, ffn_w_down=v_ffn_w_down)
    names = list(W)
    xi, yi, ci = lax.axis_index("x"), lax.axis_index("y"), lax.axis_index("c")
    q = 2 * xi + yi
    me = 2 * q + ci
    D = x.shape[-1]
    L = ada_w.shape[0]

    small_w = ["conv_b_in", "conv_dw_w", "conv_dw_b", "conv_ln_g", "conv_ln_b", "conv_b_out", "ffn_dw_w"]
    small_axis = dict(conv_b_in=1, conv_dw_w=2, conv_dw_b=1, conv_ln_g=1, conv_ln_b=1, conv_b_out=1, ffn_dw_w=2)
    packed = _pack([c] + [W[n] for n in small_w])
    gathered = allgather_devices(packed, name="gather_small_params").reshape(N_DEV, -1)
    c_all = gathered[:, 0:D]
    per_chip = gathered.reshape(N_CHIPS, 2, -1)[:, 0, D:]
    parts = _unpack(per_chip, [W[n].shape for n in small_w])
    P = {n: _from_chips(p, small_axis[n]) for n, p in zip(small_w, parts)}
    P["conv_dw_w"] = P["conv_dw_w"][0]
    for n in ("pre_mix_g", "post_mix_g", "pre_ffn_g", "post_ffn_g", "hgrn_lb_logits", "hgrn_gnorm_g", "ffn_dw_b"):
        P[n] = W[n]

    modp = ada_mod(c_all, ada_w, name="ada_mod")
    ncol = modp.shape[-1]
    mod_all = allgather_devices(modp.reshape(L * N_DEV, ncol), name="gather_mod")
    mod_all = mod_all.reshape(N_CHIPS, 2, L, N_DEV, ncol)[:, 0]
    mod_me = lax.dynamic_index_in_dim(mod_all, me, axis=2, keepdims=False)
    mod = mod_me.transpose(1, 0, 2).reshape(L, N_CHIPS * ncol) + ada_b
    mods = [tuple(mod[l:l + 1, k * D:(k + 1) * D] for k in range(6)) for l in range(L)]

    def halves(w):
        shard = w.astype(BF16).reshape(1, 2, w.shape[0] // 2, w.shape[1])
        buf = lax.empty((N_CHIPS,) + shard.shape[1:], BF16)
        return lax.dynamic_update_slice_in_dim(buf, shard, q, axis=0)

    big_shards = [hgrn_w_in[0], hgrn_w_out[0], conv_w_in[0], conv_w_out[0], ffn_w_up[0], ffn_w_up[1],
                  ffn_w_down[0], ffn_w_down[1]]
    g = allgather_chips([halves(w) for w in big_shards], name="gather_weights")
    stack = lambda t: t.reshape(N_CHIPS, t.shape[1] * t.shape[2], t.shape[3])
    rowsh = lambda t: t.reshape(1, N_CHIPS * t.shape[1] * t.shape[2], t.shape[3])
    P["hgrn_w_in"], P["hgrn_w_out"] = stack(g[0]), rowsh(g[1])
    P["conv_w_in"], P["conv_w_out"] = stack(g[2]), rowsh(g[3])
    P["ffn_w_up"] = [stack(g[4]), stack(g[5])]
    P["ffn_w_down"] = [rowsh(g[6]), rowsh(g[7])]

    grad_x, small, big = _local_step(x[0], loss_target[0], mods, P)

    small_names = list(small)
    gs = allgather_devices(_pack([small[n] for n in small_names]), name="gather_small_grads")
    dmod_all = _unpack(gs.reshape(N_DEV, -1), [small[n].shape for n in small_names])[small_names.index("dmod")]
    tot = sum_devices(gs, name="sum_small_grads").reshape(1, -1)
    S = dict(zip(small_names, _unpack(tot, [small[n].shape for n in small_names])))
    S = {n: v[0] for n, v in S.items()}
    loss = 0.5 * jnp.sum(S["loss"]) / D

    G = {}
    dmod_q = lax.dynamic_slice_in_dim(dmod_all, q * ncol, ncol, axis=2)
    G["ada_w"] = ada_wgrad(c_all.T, dmod_q.transpose(1, 0, 2), name="ada_wgrad")
    G["ada_b"] = S["dmod"]
    for n in ("pre_mix_g", "post_mix_g", "pre_ffn_g", "post_ffn_g", "hgrn_gnorm_g", "ffn_dw_b"):
        G[n] = S[n]
    G["hgrn_lb_logits"] = lb_logits_grad(hgrn_lb_logits, S["lb"], name="lb_logits_grad")
    G["conv_b_in"] = _my_shard(S["conv_b_in"], 1, q)
    G["conv_dw_w"] = _my_shard(S["conv_dw_w"], 1, q)[None]
    for n in ("conv_dw_b", "conv_ln_g", "conv_ln_b", "conv_b_out"):
        G[n] = _my_shard(S[n], 1, q)
    G["ffn_dw_w"] = _my_shard(S["ffn_dw_w"], 2, q)

    red = reduce_weight_grads(big, q, ci)
    G["hgrn_w_in"], G["hgrn_w_out"], G["conv_w_in"], G["conv_w_out"] = red[0][None], red[1][None], red[2][None], red[3][None]
    G["ffn_w_up"] = jnp.stack([red[4], red[5]])
    G["ffn_w_down"] = jnp.stack([red[6], red[7]])

    delta, new_m, new_v = {}, {}, {}
    big_names = ["ada_w", "hgrn_w_in", "hgrn_w_out", "conv_w_in", "conv_w_out", "ffn_w_up", "ffn_w_down"]
    for n in big_names:
        shp = W[n].shape
        two = lambda t: t.reshape(-1, shp[-1])
        d_, m_, v_ = adamw(two(W[n]), two(G[n]), two(M[n]), two(V[n]), name=f"adamw_{n}")
        delta[n], new_m[n], new_v[n] = d_.reshape(shp), m_.reshape(shp), v_.reshape(shp)
    rest = [n for n in names if n not in big_names]
    d_, m_, v_ = adamw(_pack([W[n] for n in rest]), _pack([G[n] for n in rest]), _pack([M[n] for n in rest]),
                       _pack([V[n] for n in rest]), name="adamw_small")
    shapes = [W[n].shape for n in rest]
    for n, a, b_, c_ in zip(rest, _unpack(d_.reshape(-1), shapes), _unpack(m_.reshape(-1), shapes),
                            _unpack(v_.reshape(-1), shapes)):
        delta[n], new_m[n], new_v[n] = a, b_, c_

    return (loss, grad_x[None], *[G[n].reshape(W[n].shape) for n in names], *[delta[n] for n in names],
            *[new_m[n] for n in names], *[new_v[n] for n in names])
```

```python
import jax
import jax.numpy as jnp
from jax import lax
from jax.experimental import pallas as pl
from jax.experimental.pallas import tpu as pltpu

F32 = jnp.float32
BF16 = jnp.bfloat16
EPS = 1e-6
HEAD = 128
BLK = 16
NEG = -1e30
CONV_W = 31
FFN_W = 3
N_CHIPS = 4
N_DEV = 8
SUB = 8
LANE = 128
V7X_VMEM_LIMIT = 56 * 1024 * 1024
MESH = pl.DeviceIdType.MESH
HBM = pl.BlockSpec(memory_space=pltpu.HBM)
VMEM_SPEC = pl.BlockSpec(memory_space=pltpu.VMEM)

ADAM_LR = 0.001
ADAM_B1 = 0.9
ADAM_B2 = 0.999
ADAM_EPS = 1e-08
ADAM_WD = 0.01
ADAM_STEP = 10


def _cp(*sem):
    return pltpu.CompilerParams(dimension_semantics=sem, vmem_limit_bytes=V7X_VMEM_LIMIT)


def _sig(x):
    return 0.5 * jnp.tanh(0.5 * x) + 0.5


def _silu(x):
    return x * _sig(x)


def _dsilu(x):
    s = _sig(x)
    return s * (1.0 + x * (1.0 - s))


def _dot(a, b):
    return jnp.dot(a, b, preferred_element_type=F32)


def _dot_nt(a, b):
    return lax.dot_general(a, b, (((1,), (1,)), ((), ())), preferred_element_type=F32)


def _dot_tn(a, b):
    return lax.dot_general(a, b, (((0,), (0,)), ((), ())), preferred_element_type=F32)


def _colsum(x):
    return jnp.sum(x, axis=0, keepdims=True)


def _rowmean(x):
    return jnp.mean(x, axis=-1, keepdims=True)


def _ffn_perm(j):
    return (j % 2) * 2 + j // 2


def _tile(n, pref):
    if n <= pref:
        return n
    t = pref - pref % 8
    while n % t:
        t -= 8
    return t


def mm_nn(a, w, *, name, bias=None, out_dtype=F32, perm=None, tm=512):
    T, K = a.shape
    J, _, nb = w.shape
    tm = min(tm, T)
    col = (lambda j: j) if perm is None else perm

    def body(a_ref, w_ref, *rest):
        acc = _dot(a_ref[...], w_ref[...])
        if bias is not None:
            acc = acc + rest[0][...]
        rest[-1][...] = acc.astype(out_dtype)

    in_specs = [pl.BlockSpec((tm, K), lambda j, i: (i, 0)), pl.BlockSpec((None, K, nb), lambda j, i: (j, 0, 0))]
    args = [a, w]
    if bias is not None:
        in_specs.append(pl.BlockSpec((1, nb), lambda j, i: (0, j)))
        args.append(bias)
    return pl.pallas_call(
        body, grid=(J, T // tm), in_specs=in_specs,
        out_specs=pl.BlockSpec((tm, nb), lambda j, i: (i, col(j))),
        out_shape=jax.ShapeDtypeStruct((T, J * nb), out_dtype), name=name,
        compiler_params=_cp("parallel", "parallel"))(*args)


def mm_nt(a, w, *, name, out_dtype=F32, perm=None, tm=512):
    T = a.shape[0]
    J, K, nb = w.shape
    tm = min(tm, T)
    col = (lambda j: j) if perm is None else perm

    def body(a_ref, w_ref, o_ref, acc_ref):
        j = pl.program_id(1)

        @pl.when(j == 0)
        def _():
            acc_ref[...] = jnp.zeros_like(acc_ref)

        acc_ref[...] += _dot_nt(a_ref[...], w_ref[...])

        @pl.when(j == J - 1)
        def _():
            o_ref[...] = acc_ref[...].astype(out_dtype)

    return pl.pallas_call(
        body, grid=(T // tm, J),
        in_specs=[pl.BlockSpec((tm, nb), lambda i, j: (i, col(j))), pl.BlockSpec((None, K, nb), lambda i, j: (j, 0, 0))],
        out_specs=pl.BlockSpec((tm, K), lambda i, j: (i, 0)),
        out_shape=jax.ShapeDtypeStruct((T, K), out_dtype),
        scratch_shapes=[pltpu.VMEM((tm, K), F32)], name=name,
        compiler_params=_cp("parallel", "arbitrary"))(a, w)


def mm_tn(a, b, *, name, J, block, chips_per_block, perm=None, tk=512):
    T = a.shape[0]
    tk = min(tk, T)
    col = (lambda j: j) if perm is None else perm
    if block == "b":
        rows, nb = a.shape[1], b.shape[1] // J
        a_spec = pl.BlockSpec((tk, rows), lambda j, t: (t, 0))
        b_spec = pl.BlockSpec((tk, nb), lambda j, t: (t, col(j)))
    else:
        rows, nb = a.shape[1] // J, b.shape[1]
        a_spec = pl.BlockSpec((tk, rows), lambda j, t: (t, col(j)))
        b_spec = pl.BlockSpec((tk, nb), lambda j, t: (t, 0))
    cpb = chips_per_block
    rh = rows // (2 * cpb)

    def body(a_ref, b_ref, o_ref):
        @pl.when(pl.program_id(1) == 0)
        def _():
            o_ref[...] = jnp.zeros_like(o_ref)

        acc = _dot_tn(a_ref[...], b_ref[...])
        for ch in range(cpb):
            for hf in range(2):
                r0 = (ch * 2 + hf) * rh
                o_ref[hf, ch] += acc[r0:r0 + rh, :]

    return pl.pallas_call(
        body, grid=(J, T // tk), in_specs=[a_spec, b_spec],
        out_specs=pl.BlockSpec((2, cpb, rh, nb), lambda j, t: (0, j, 0, 0)),
        out_shape=jax.ShapeDtypeStruct((2, J * cpb, rh, nb), F32), name=name,
        compiler_params=_cp("parallel", "arbitrary"))(a, b)


def _row(tm, w):
    return pl.BlockSpec((tm, w), lambda i: (i, 0))


def _full(r, w):
    return pl.BlockSpec((r, w), lambda i: (0, 0))


def _acc_init(i, *refs):
    @pl.when(i == 0)
    def _():
        for r in refs:
            r[...] = jnp.zeros_like(r)


def prenorm(x, g, sc, sh, *, name, tm=512):
    T, D = x.shape
    tm = min(tm, T)

    def body(x_ref, g_ref, sc_ref, sh_ref, h_ref):
        xv = x_ref[...]
        r = lax.rsqrt(_rowmean(xv * xv) + EPS)
        h_ref[...] = ((xv * r) * g_ref[...] * (1.0 + sc_ref[...]) + sh_ref[...]).astype(BF16)

    return pl.pallas_call(
        body, grid=(T // tm,), in_specs=[_row(tm, D), _full(1, D), _full(1, D), _full(1, D)],
        out_specs=_row(tm, D), out_shape=jax.ShapeDtypeStruct((T, D), BF16), name=name,
        compiler_params=_cp("parallel"))(x, g, sc, sh)


def post_residual(x, y, g, gate, *, name, tm=512):
    T, D = x.shape
    tm = min(tm, T)

    def body(x_ref, y_ref, g_ref, gate_ref, o_ref):
        yv = y_ref[...]
        r = lax.rsqrt(_rowmean(yv * yv) + EPS)
        o_ref[...] = x_ref[...] + gate_ref[...] * ((yv * r) * g_ref[...])

    return pl.pallas_call(
        body, grid=(T // tm,), in_specs=[_row(tm, D), _row(tm, D), _full(1, D), _full(1, D)],
        out_specs=_row(tm, D), out_shape=jax.ShapeDtypeStruct((T, D), F32), name=name,
        compiler_params=_cp("parallel"))(x, y, g, gate)


def loss_grad(x, tgt, *, name, tm=512):
    T, D = x.shape
    tm = min(tm, T)

    def body(x_ref, t_ref, dx_ref, l_ref):
        _acc_init(pl.program_id(0), l_ref)
        e = x_ref[...] - t_ref[...]
        dx_ref[...] = e * (1.0 / D)
        l_ref[...] += _colsum(e * e)

    return pl.pallas_call(
        body, grid=(T // tm,), in_specs=[_row(tm, D), _row(tm, D)],
        out_specs=[_row(tm, D), _full(1, D)],
        out_shape=[jax.ShapeDtypeStruct((T, D), F32), jax.ShapeDtypeStruct((1, D), F32)], name=name,
        compiler_params=_cp("arbitrary"))(x, tgt)


def post_bwd(dx, y, g, gate, *, name, tm=512):
    T, D = dx.shape
    tm = min(tm, T)

    def body(dx_ref, y_ref, g_ref, gate_ref, dy_ref, dgate_ref, dg_ref, dbias_ref):
        _acc_init(pl.program_id(0), dgate_ref, dg_ref, dbias_ref)
        yv = y_ref[...]
        dxv = dx_ref[...]
        r = lax.rsqrt(_rowmean(yv * yv) + EPS)
        yn = yv * r
        gv = g_ref[...]
        gt = gate_ref[...]
        dgate_ref[...] += _colsum(dxv * (yn * gv))
        dg_ref[...] += _colsum(dxv * gt * yn)
        dyn = dxv * gt * gv
        dy = r * (dyn - yn * _rowmean(dyn * yn))
        dbias_ref[...] += _colsum(dy)
        dy_ref[...] = dy.astype(BF16)

    return pl.pallas_call(
        body, grid=(T // tm,), in_specs=[_row(tm, D), _row(tm, D), _full(1, D), _full(1, D)],
        out_specs=[_row(tm, D), _full(1, D), _full(1, D), _full(1, D)],
        out_shape=[jax.ShapeDtypeStruct((T, D), BF16)] + [jax.ShapeDtypeStruct((1, D), F32)] * 3, name=name,
        compiler_params=_cp("arbitrary"))(dx, y, g, gate)


def prenorm_bwd(dh, x, dres, g, sc, *, name, tm=512):
    T, D = x.shape
    tm = min(tm, T)

    def body(dh_ref, x_ref, dres_ref, g_ref, sc_ref, dx_ref, dsh_ref, dsc_ref, dg_ref):
        _acc_init(pl.program_id(0), dsh_ref, dsc_ref, dg_ref)
        xv = x_ref[...]
        dhv = dh_ref[...]
        r = lax.rsqrt(_rowmean(xv * xv) + EPS)
        xn = xv * r
        gv = g_ref[...]
        one_sc = 1.0 + sc_ref[...]
        dsh_ref[...] += _colsum(dhv)
        dsc_ref[...] += _colsum(dhv * (xn * gv))
        dg_ref[...] += _colsum(dhv * one_sc * xn)
        dxn = dhv * one_sc * gv
        dx_ref[...] = dres_ref[...] + r * (dxn - xn * _rowmean(dxn * xn))

    return pl.pallas_call(
        body, grid=(T // tm,), in_specs=[_row(tm, D), _row(tm, D), _row(tm, D), _full(1, D), _full(1, D)],
        out_specs=[_row(tm, D), _full(1, D), _full(1, D), _full(1, D)],
        out_shape=[jax.ShapeDtypeStruct((T, D), F32)] + [jax.ShapeDtypeStruct((1, D), F32)] * 3, name=name,
        compiler_params=_cp("arbitrary"))(dh, x, dres, g, sc)


HALO = 16


def _shift_helpers():
    rid = lax.broadcasted_iota(jnp.int32, (SUB, LANE), 0)

    def down(cur, prev, k):
        return pltpu.roll(jnp.where(rid >= SUB - k, prev, cur), k, 0)

    def up(cur, nxt, k):
        return pltpu.roll(jnp.where(rid < k, nxt, cur), SUB - k, 0)

    return down, up


def _ffn_sides(c, nb, wa_ref, wb_ref, ba_ref, bb_ref):
    cols = slice(c * LANE, (c + 1) * LANE)
    return [(cols, [wa_ref[k:k + 1, cols] for k in range(FFN_W)], ba_ref[:, cols]),
            (slice(nb + c * LANE, nb + (c + 1) * LANE), [wb_ref[k:k + 1, cols] for k in range(FFN_W)],
             bb_ref[:, cols])]


def _ffn_specs(tm, nb, hb, idx):
    return [pl.BlockSpec((tm, 2 * nb), lambda jc, i: (idx(i), jc)),
            pl.BlockSpec((HALO, 2 * nb), lambda jc, i: (jnp.maximum(idx(i) * hb - 1, 0), jc)),
            pl.BlockSpec((FFN_W, nb), lambda jc, i: (0, jc)),
            pl.BlockSpec((FFN_W, nb), lambda jc, i: (0, jc + 2)),
            pl.BlockSpec((1, nb), lambda jc, i: (0, jc)),
            pl.BlockSpec((1, nb), lambda jc, i: (0, jc + 2))]


def ffn_act(u0p, dw_w, dw_b, *, name, tm=256):
    T, W = u0p.shape
    nb = W // 4
    tm = min(tm, T)
    unroll = 4
    rows16 = 2 * SUB

    def body(u_ref, halo_ref, wa_ref, wb_ref, ba_ref, bb_ref, z_ref):
        i = pl.program_id(1)
        down, _ = _shift_helpers()
        for c in range(nb // LANE):
            cols = slice(c * LANE, (c + 1) * LANE)
            side = _ffn_sides(c, nb, wa_ref, wb_ref, ba_ref, bb_ref)

            def rows(j, prev):
                prev = list(prev)
                for m in range(unroll):
                    r0 = pl.multiple_of((j * unroll + m) * rows16, rows16)
                    x = [u_ref[pl.ds(r0, rows16), cs].astype(F32) for cs, _, _ in side]
                    zs = []
                    for hf in range(2):
                        conv = []
                        for n, (_, w, b) in enumerate(side):
                            cur = x[n][hf * SUB:(hf + 1) * SUB, :]
                            conv.append(b + w[2] * cur + w[1] * down(cur, prev[n], 1) + w[0] * down(cur, prev[n], 2))
                            prev[n] = cur
                        zs.append(_silu(conv[0]) * conv[1])
                    z_ref[pl.ds(r0, rows16), cols] = jnp.concatenate(zs, axis=0).astype(BF16)
                return tuple(prev)

            first = [jnp.where(i == 0, 0.0, halo_ref[:, cs].astype(F32)[SUB:2 * SUB, :]) for cs, _, _ in side]
            lax.fori_loop(0, tm // (rows16 * unroll), rows, tuple(first))

    return pl.pallas_call(
        body, grid=(2, T // tm), in_specs=_ffn_specs(tm, nb, tm // HALO, lambda i: i),
        out_specs=pl.BlockSpec((tm, nb), lambda jc, i: (i, jc)),
        out_shape=jax.ShapeDtypeStruct((T, 2 * nb), BF16), name=name,
        compiler_params=_cp("parallel", "arbitrary"))(u0p, u0p, dw_w, dw_w, dw_b, dw_b)


def ffn_act_bwd(dz, u0p, dw_w, dw_b, *, name, tm=256):
    T, W = u0p.shape
    nb = W // 4
    tm = min(tm, T)
    nt = T // tm
    unroll = 2
    rows16 = 2 * SUB
    n_it = tm // (rows16 * unroll)

    def body(dz_ref, u_ref, halo_ref, wa_ref, wb_ref, ba_ref, bb_ref, du0_ref, dw_ref, carry):
        i = pl.program_id(1)
        _acc_init(i, dw_ref)
        down, up = _shift_helpers()
        for c in range(nb // LANE):
            cols = slice(c * LANE, (c + 1) * LANE)
            side = _ffn_sides(c, nb, wa_ref, wb_ref, ba_ref, bb_ref)
            halo = [jnp.where(i == nt - 1, 0.0, halo_ref[:, cs].astype(F32)[SUB:2 * SUB, :]) for cs, _, _ in side]

            def rows(j, st):
                nxt, acc, x = list(st[0:2]), list(st[2:10]), list(st[10:12])
                for m in range(unroll):
                    r0 = pl.multiple_of(((n_it - 1 - j) * unroll + unroll - 1 - m) * rows16, rows16)
                    rp = pl.multiple_of(jnp.maximum(r0 - rows16, 0), rows16)
                    dzv = dz_ref[pl.ds(r0, rows16), cols].astype(F32)
                    chunk = []
                    for n, (cs, _, _) in enumerate(side):
                        before = u_ref[pl.ds(rp, rows16), cs].astype(F32)
                        chunk.append([jnp.where(r0 == 0, halo[n], before[SUB:2 * SUB, :]), x[n][0:SUB, :],
                                      x[n][SUB:2 * SUB, :]])
                        x[n] = before
                    out = [[None, None], [None, None]]
                    for hf in (1, 0):
                        cur = [chunk[n][hf + 1] for n in range(2)]
                        s1 = [down(cur[n], chunk[n][hf], 1) for n in range(2)]
                        s2 = [down(cur[n], chunk[n][hf], 2) for n in range(2)]
                        a, b = [side[n][2] + side[n][1][2] * cur[n] + side[n][1][1] * s1[n] + side[n][1][0] * s2[n]
                                for n in range(2)]
                        sa = _sig(a)
                        dzh = dzv[hf * SUB:(hf + 1) * SUB, :]
                        d = [dzh * b * (sa * (1.0 + a * (1.0 - sa))), dzh * (a * sa)]
                        for n in range(2):
                            w = side[n][1]
                            acc[4 * n + 0] = acc[4 * n + 0] + d[n] * s2[n]
                            acc[4 * n + 1] = acc[4 * n + 1] + d[n] * s1[n]
                            acc[4 * n + 2] = acc[4 * n + 2] + d[n] * cur[n]
                            acc[4 * n + 3] = acc[4 * n + 3] + d[n]
                            out[n][hf] = w[2] * d[n] + w[1] * up(d[n], nxt[n], 1) + w[0] * up(d[n], nxt[n], 2)
                            nxt[n] = d[n]
                    for n in range(2):
                        du0_ref[pl.ds(r0, rows16), side[n][0]] = jnp.concatenate(out[n], axis=0).astype(BF16)
                return (*nxt, *acc, *x)

            init = ([jnp.where(i == 0, 0.0, carry[:, cs]) for cs, _, _ in side] + [jnp.zeros((SUB, LANE), F32)] * 8
                    + [u_ref[tm - rows16:tm, cs].astype(F32) for cs, _, _ in side])
            st = lax.fori_loop(0, n_it, rows, tuple(init))
            for n in range(2):
                carry[:, side[n][0]] = st[n]
                for k in range(4):
                    dw_ref[k, :, side[n][0]] += st[2 + 4 * n + k]

        @pl.when(i == nt - 1)
        def _():
            for k in range(4):
                dw_ref[k, 0:1, :] = _colsum(dw_ref[k])

    rev = lambda i: nt - 1 - i
    return pl.pallas_call(
        body, grid=(2, nt),
        in_specs=[pl.BlockSpec((tm, nb), lambda jc, i: (rev(i), jc))] + _ffn_specs(tm, nb, tm // HALO, rev),
        out_specs=[pl.BlockSpec((tm, 2 * nb), lambda jc, i: (rev(i), jc)),
                   pl.BlockSpec((4, SUB, 2 * nb), lambda jc, i: (0, 0, jc))],
        out_shape=[jax.ShapeDtypeStruct((T, W), BF16), jax.ShapeDtypeStruct((4, SUB, W), F32)],
        scratch_shapes=[pltpu.VMEM((SUB, 2 * nb), F32)], name=name,
        compiler_params=_cp("parallel", "arbitrary"))(dz, u0p, u0p, dw_w, dw_w, dw_b, dw_b)


CHALO = 32
CCOL = 256


def _phase_copies(buf, shifted, tm):
    n = tm + CHALO - SUB
    for p in range(1, SUB):
        shifted[p - 1, 0:n, :] = buf[p:p + n, :]


def _shifted(buf, shifted, r, tm, c0):
    m, p = divmod(r, SUB)
    src = buf if p == 0 else shifted.at[p - 1]
    return src[m * SUB:m * SUB + tm, c0:c0 + CCOL]


def conv_act(u, dw_w, dw_b, ln_g, ln_b, *, name, tm=128):
    T, D2 = u.shape
    D = D2 // 2
    tm = min(tm, T)
    hb = tm // CHALO

    def body(u_ref, halo_ref, w_ref, b_ref, g_ref, be_ref, s_ref, cv_ref, gbuf, gsh):
        i = pl.program_id(0)
        hv = halo_ref[...]
        gbuf[0:CHALO, :] = jnp.where(i == 0, 0.0, hv[:, 0:D] * _sig(hv[:, D:D2]))
        uv = u_ref[...]
        gbuf[CHALO:CHALO + tm, :] = uv[:, 0:D] * _sig(uv[:, D:D2])
        _phase_copies(gbuf, gsh, tm)
        for c0 in range(0, D, CCOL):
            acc = jnp.zeros((tm, CCOL), F32) + b_ref[:, c0:c0 + CCOL]
            for k in range(CONV_W):
                acc = acc + w_ref[k:k + 1, c0:c0 + CCOL] * _shifted(gbuf, gsh, CHALO - (CONV_W - 1) + k, tm, c0)
            cv_ref[:, c0:c0 + CCOL] = acc
        cv = cv_ref[...]
        mu = _rowmean(cv)
        xc = cv - mu
        nh = xc * lax.rsqrt(_rowmean(xc * xc) + EPS)
        s_ref[...] = _silu(nh * g_ref[...] + be_ref[...]).astype(BF16)

    return pl.pallas_call(
        body, grid=(T // tm,),
        in_specs=[_row(tm, D2), pl.BlockSpec((CHALO, D2), lambda i: (jnp.maximum(i * hb - 1, 0), 0)),
                  _full(CONV_W, D), _full(1, D), _full(1, D), _full(1, D)],
        out_specs=[_row(tm, D), _row(tm, D)],
        out_shape=[jax.ShapeDtypeStruct((T, D), BF16), jax.ShapeDtypeStruct((T, D), F32)],
        scratch_shapes=[pltpu.VMEM((tm + CHALO, D), F32), pltpu.VMEM((SUB - 1, tm + CHALO, D), F32)], name=name,
        compiler_params=_cp("arbitrary"))(u, u, dw_w, dw_b, ln_g, ln_b)


def conv_norm_bwd(ds, cv, ln_g, ln_b, *, name, tm=512):
    T, D = cv.shape
    tm = min(tm, T)

    def body(ds_ref, cv_ref, g_ref, be_ref, dcv_ref, dg_ref, dbe_ref, dcb_ref):
        _acc_init(pl.program_id(0), dg_ref, dbe_ref, dcb_ref)
        cv_ = cv_ref[...]
        mu = _rowmean(cv_)
        xc = cv_ - mu
        rstd = lax.rsqrt(_rowmean(xc * xc) + EPS)
        nh = xc * rstd
        gv = g_ref[...]
        dln = ds_ref[...] * _dsilu(nh * gv + be_ref[...])
        dg_ref[...] += _colsum(dln * nh)
        dbe_ref[...] += _colsum(dln)
        dnh = dln * gv
        dcv = rstd * (dnh - _rowmean(dnh) - nh * _rowmean(dnh * nh))
        dcb_ref[...] += _colsum(dcv)
        dcv_ref[...] = dcv

    return pl.pallas_call(
        body, grid=(T // tm,), in_specs=[_row(tm, D), _row(tm, D), _full(1, D), _full(1, D)],
        out_specs=[_row(tm, D), _full(1, D), _full(1, D), _full(1, D)],
        out_shape=[jax.ShapeDtypeStruct((T, D), F32)] + [jax.ShapeDtypeStruct((1, D), F32)] * 3, name=name,
        compiler_params=_cp("arbitrary"))(ds, cv, ln_g, ln_b)


def conv_glu_bwd(dcv, u, dw_w, *, name, tm=128):
    T, D2 = u.shape
    D = D2 // 2
    tm = min(tm, T)
    nt = T // tm
    hb = tm // CHALO

    def body(dcv_ref, dnext_ref, u_ref, halo_ref, w_ref, du_ref, dw_ref, dbin_ref, gbuf, dbuf, gsh, dsh):
        i = pl.program_id(0)
        _acc_init(i, dw_ref, dbin_ref)
        hv = halo_ref[...]
        gbuf[0:CHALO, :] = jnp.where(i == 0, 0.0, hv[:, 0:D] * _sig(hv[:, D:D2]))
        uv = u_ref[...]
        av = uv[:, 0:D]
        sg = _sig(uv[:, D:D2])
        gbuf[CHALO:CHALO + tm, :] = av * sg
        dbuf[0:tm, :] = dcv_ref[...]
        dbuf[tm:tm + CHALO, :] = jnp.where(i == nt - 1, 0.0, dnext_ref[...])
        _phase_copies(gbuf, gsh, tm)
        _phase_copies(dbuf, dsh, tm)
        for c0 in range(0, D, CCOL):
            dc = dbuf[0:tm, c0:c0 + CCOL]
            acc = jnp.zeros((tm, CCOL), F32)
            for k in range(CONV_W):
                dw_ref[k:k + 1, c0:c0 + CCOL] += _colsum(dc * _shifted(gbuf, gsh, CHALO - (CONV_W - 1) + k, tm, c0))
                acc = acc + w_ref[k:k + 1, c0:c0 + CCOL] * _shifted(dbuf, dsh, CONV_W - 1 - k, tm, c0)
            a_c = av[:, c0:c0 + CCOL]
            s_c = sg[:, c0:c0 + CCOL]
            da = acc * s_c
            dgt = acc * a_c * s_c * (1.0 - s_c)
            dbin_ref[:, c0:c0 + CCOL] += _colsum(da)
            dbin_ref[:, D + c0:D + c0 + CCOL] += _colsum(dgt)
            du_ref[:, c0:c0 + CCOL] = da.astype(BF16)
            du_ref[:, D + c0:D + c0 + CCOL] = dgt.astype(BF16)

    return pl.pallas_call(
        body, grid=(nt,),
        in_specs=[_row(tm, D), pl.BlockSpec((CHALO, D), lambda i: (jnp.minimum((i + 1) * hb, T // CHALO - 1), 0)),
                  _row(tm, D2), pl.BlockSpec((CHALO, D2), lambda i: (jnp.maximum(i * hb - 1, 0), 0)),
                  _full(CONV_W, D)],
        out_specs=[_row(tm, D2), _full(CHALO, D), _full(1, D2)],
        out_shape=[jax.ShapeDtypeStruct((T, D2), BF16), jax.ShapeDtypeStruct((CHALO, D), F32),
                   jax.ShapeDtypeStruct((1, D2), F32)],
        scratch_shapes=[pltpu.VMEM((tm + CHALO, D), F32)] * 2 + [pltpu.VMEM((SUB - 1, tm + CHALO, D), F32)] * 2,
        name=name, compiler_params=_cp("arbitrary"))(dcv, dcv, u, u, dw_w)


HB = 4


def _lb0(lg_ref):
    l0, l1, l2 = lg_ref[0:1, :], lg_ref[1:2, :], lg_ref[2:3, :]
    m = jnp.maximum(jnp.maximum(l0, l1), l2)
    e0 = jnp.exp(l0 - m)
    return e0 / (e0 + jnp.exp(l1 - m) + jnp.exp(l2 - m))


def _mm_exact(m01, x):
    hi = x.astype(BF16)
    r1 = x - hi.astype(F32)
    mid = r1.astype(BF16)
    lo = (r1 - mid.astype(F32)).astype(BF16)
    return _dot(m01, hi) + _dot(m01, mid) + _dot(m01, lo)


def _block_tri(tm):
    r = jnp.arange(tm)[:, None]
    c = jnp.arange(tm)[None, :]
    same = (r // BLK) == (c // BLK)
    return (same & (c <= r)).astype(BF16), (same & (c >= r)).astype(BF16)


def _const_spec(shape):
    return pl.BlockSpec(shape, lambda h, i: (0, 0))


def _hgrn_specs(H, hb, tm, idx):
    g = H // hb
    return [pl.BlockSpec((tm, hb * HEAD), lambda h, i: (idx(i), h)),
            pl.BlockSpec((tm, hb * HEAD), lambda h, i: (idx(i), g + h)),
            pl.BlockSpec((tm, hb * HEAD), lambda h, i: (idx(i), 2 * g + h)),
            pl.BlockSpec((3, hb * HEAD), lambda h, i: (0, h))]


def hgrn_scan(proj, lb_logits, *, name, tm=128):
    T = proj.shape[0]
    H = proj.shape[1] // (4 * HEAD)
    hb = min(HB, H)
    tm = min(tm, T)
    nt = T // tm
    nblk = tm // BLK
    tril, _ = _block_tri(tm)
    heads = [slice(hh * HEAD, (hh + 1) * HEAD) for hh in range(hb)]

    def body(qp_ref, fz_ref, v_ref, lg_ref, tril_ref, o_ref, st_ref, S_ref, q_s, k_s, b_s):
        @pl.when(pl.program_id(1) == 0)
        def _():
            S_ref[...] = jnp.zeros_like(S_ref)

        st_ref[...] = S_ref[...]
        lb = _lb0(lg_ref)
        f = lb + (1.0 - lb) * _sig(fz_ref[...])
        q_s[...] = _silu(qp_ref[...])
        k_s[...] = 1.0 - f
        b_s[...] = _mm_exact(tril_ref[...], jnp.log(f))
        rows = lax.broadcasted_iota(jnp.int32, (BLK, HEAD), 0)
        S = [S_ref[hh] for hh in range(hb)]
        for nb in range(nblk):
            blk = slice(nb * BLK, (nb + 1) * BLK)
            last = slice(nb * BLK + BLK - 1, nb * BLK + BLK)
            qb = [q_s[blk, c] for c in heads]
            bb = [b_s[blk, c] for c in heads]
            o = [_dot_nt((qb[hh] * jnp.exp(bb[hh])).astype(BF16), S[hh].astype(BF16)) for hh in range(hb)]
            for hh, c in enumerate(heads):
                bc = b_s[last, c]
                kd = k_s[blk, c] * jnp.exp(bc - bb[hh])
                S[hh] = S[hh] * jnp.exp(bc) + _dot_tn(v_ref[blk, c].astype(BF16), kd.astype(BF16))
            for s in range(BLK):
                r = slice(nb * BLK + s, nb * BLK + s + 1)
                for hh, c in enumerate(heads):
                    dec = jnp.exp(jnp.where(rows >= s, bb[hh] - b_s[r, c], NEG))
                    a = jnp.sum(qb[hh] * k_s[r, c] * dec, axis=-1, keepdims=True)
                    o[hh] = o[hh] + a * v_ref[r, c]
            for hh, c in enumerate(heads):
                o_ref[blk, c] = o[hh]
        for hh in range(hb):
            S_ref[hh] = S[hh]

    return pl.pallas_call(
        body, grid=(H // hb, nt),
        in_specs=_hgrn_specs(H, hb, tm, lambda i: i) + [_const_spec((tm, tm))],
        out_specs=[pl.BlockSpec((tm, hb * HEAD), lambda h, i: (i, h)),
                   pl.BlockSpec((None, hb, HEAD, HEAD), lambda h, i: (i, h, 0, 0))],
        out_shape=[jax.ShapeDtypeStruct((T, H * HEAD), F32), jax.ShapeDtypeStruct((nt, H, HEAD, HEAD), F32)],
        scratch_shapes=[pltpu.VMEM((hb, HEAD, HEAD), F32)] + [pltpu.VMEM((tm, hb * HEAD), F32)] * 3, name=name,
        compiler_params=_cp("parallel", "arbitrary"))(proj, proj, proj, lb_logits, tril)


def hgrn_scan_bwd(proj, lb_logits, states, do, *, name, tm=128):
    T = proj.shape[0]
    H = proj.shape[1] // (4 * HEAD)
    hb = min(HB, H)
    tm = min(tm, T)
    nt = T // tm
    nblk = tm // BLK
    tril, triu = _block_tri(tm)
    heads = [slice(hh * HEAD, (hh + 1) * HEAD) for hh in range(hb)]

    def body(qp_ref, fz_ref, v_ref, lg_ref, st_ref, do_ref, tril_ref, triu_ref, dqp_ref, dfz_ref, dv_ref, dlb_ref,
             dS_ref, Sb_ref, q_s, k_s, b_s, dq_s, dk_s, dv_s, db_s):
        i = pl.program_id(1)

        @pl.when(i == 0)
        def _():
            dS_ref[...] = jnp.zeros_like(dS_ref)
            dlb_ref[...] = jnp.zeros_like(dlb_ref)

        lb = _lb0(lg_ref)
        qp = qp_ref[...]
        sg = _sig(fz_ref[...])
        f = lb + (1.0 - lb) * sg
        q_s[...] = _silu(qp)
        k_s[...] = 1.0 - f
        b_s[...] = _mm_exact(tril_ref[...], jnp.log(f))
        rows = lax.broadcasted_iota(jnp.int32, (BLK, HEAD), 0)
        rows1 = lax.broadcasted_iota(jnp.int32, (BLK, 1), 0)

        S = [st_ref[hh] for hh in range(hb)]
        for nb in range(nblk):
            blk = slice(nb * BLK, (nb + 1) * BLK)
            last = slice(nb * BLK + BLK - 1, nb * BLK + BLK)
            for hh, c in enumerate(heads):
                Sb_ref[nb * hb + hh] = S[hh]
                if nb < nblk - 1:
                    bc = b_s[last, c]
                    kd = k_s[blk, c] * jnp.exp(bc - b_s[blk, c])
                    S[hh] = S[hh] * jnp.exp(bc) + _dot_tn(v_ref[blk, c].astype(BF16), kd.astype(BF16))

        dS = [dS_ref[hh] for hh in range(hb)]
        for nb in reversed(range(nblk)):
            blk = slice(nb * BLK, (nb + 1) * BLK)
            last = slice(nb * BLK + BLK - 1, nb * BLK + BLK)
            qb, kb, bb, dob, dq, dbc, ebc = [], [], [], [], [], [], []
            for hh, c in enumerate(heads):
                S0 = Sb_ref[nb * hb + hh]
                qb.append(q_s[blk, c])
                kb.append(k_s[blk, c])
                bb.append(b_s[blk, c])
                dob.append(do_ref[blk, c])
                bc = b_s[last, c]
                eb = jnp.exp(bb[hh])
                ekd = jnp.exp(bc - bb[hh])
                ebc.append(jnp.exp(bc))
                dS16 = dS[hh].astype(BF16)
                dob16 = dob[hh].astype(BF16)
                dq.append(_dot(dob16, S0.astype(BF16)) * eb)
                dki = _dot(v_ref[blk, c].astype(BF16), dS16) * ekd
                dk_s[blk, c] = dki
                dv_s[blk, c] = _dot_nt((kb[hh] * ekd).astype(BF16), dS16)
                dbc.append(_colsum(dS[hh] * S0) * ebc[hh] + _colsum(kb[hh] * dki))
                dS[hh] = dS[hh] * ebc[hh] + _dot_tn(dob16, (qb[hh] * eb).astype(BF16))
            for s in range(BLK):
                r = slice(nb * BLK + s, nb * BLK + s + 1)
                for hh, c in enumerate(heads):
                    ks = k_s[r, c]
                    dec = jnp.exp(jnp.where(rows >= s, bb[hh] - b_s[r, c], NEG))
                    w = qb[hh] * dec
                    a = jnp.sum(w * ks, axis=-1, keepdims=True)
                    da = jnp.where(rows1 >= s, jnp.sum(dob[hh] * v_ref[r, c], axis=-1, keepdims=True), 0.0)
                    dq[hh] = dq[hh] + (da * ks) * dec
                    dk_s[r, c] += _colsum(da * w)
                    dv_s[r, c] += _colsum(a * dob[hh])
            for hh, c in enumerate(heads):
                dq_s[blk, c] = dq[hh]
                db_s[blk, c] = qb[hh] * dq[hh] - kb[hh] * dk_s[blk, c]
                db_s[last, c] += dbc[hh]
        for hh in range(hb):
            dS_ref[hh] = dS[hh]

        dlf = _mm_exact(triu_ref[...], db_s[...])
        df = dlf / f - dk_s[...]
        dfz_ref[...] = (df * (1.0 - lb) * sg * (1.0 - sg)).astype(BF16)
        dlb_ref[...] += _colsum(df * (1.0 - sg))
        dqp_ref[...] = (dq_s[...] * _dsilu(qp)).astype(BF16)
        dv_ref[...] = dv_s[...].astype(BF16)

    rev = lambda i: nt - 1 - i
    out_blk = pl.BlockSpec((tm, hb * HEAD), lambda h, i: (rev(i), h))
    return pl.pallas_call(
        body, grid=(H // hb, nt),
        in_specs=_hgrn_specs(H, hb, tm, rev) + [pl.BlockSpec((None, hb, HEAD, HEAD), lambda h, i: (rev(i), h, 0, 0)),
                                                out_blk, _const_spec((tm, tm)), _const_spec((tm, tm))],
        out_specs=[out_blk, out_blk, out_blk, pl.BlockSpec((1, hb * HEAD), lambda h, i: (0, h))],
        out_shape=[jax.ShapeDtypeStruct((T, H * HEAD), BF16)] * 3 + [jax.ShapeDtypeStruct((1, H * HEAD), F32)],
        scratch_shapes=[pltpu.VMEM((hb, HEAD, HEAD), F32), pltpu.VMEM((nblk * hb, HEAD, HEAD), F32)]
        + [pltpu.VMEM((tm, hb * HEAD), F32)] * 7, name=name,
        compiler_params=_cp("parallel", "arbitrary"))(proj, proj, proj, lb_logits, states, do, tril, triu)


def hgrn_gate(o, proj, gn, *, name, tm=512):
    T, D = o.shape
    H = D // HEAD
    tm = min(tm, T)

    def body(o_ref, gp_ref, gn_ref, og_ref):
        gn_ = gn_ref[...]
        for h in range(H):
            c = slice(h * HEAD, (h + 1) * HEAD)
            oh = o_ref[:, c]
            r = lax.rsqrt(_rowmean(oh * oh) + EPS)
            og_ref[:, c] = ((oh * r) * gn_ * _silu(gp_ref[:, c])).astype(BF16)

    return pl.pallas_call(
        body, grid=(T // tm,),
        in_specs=[_row(tm, D), pl.BlockSpec((tm, D), lambda i: (i, 3)), _full(1, HEAD)],
        out_specs=_row(tm, D), out_shape=jax.ShapeDtypeStruct((T, D), BF16), name=name,
        compiler_params=_cp("parallel"))(o, proj, gn)


def hgrn_gate_bwd(dog, o, proj, gn, *, name, tm=512):
    T, D = o.shape
    H = D // HEAD
    tm = min(tm, T)

    def body(dog_ref, o_ref, gp_ref, gn_ref, do_ref, dgp_ref, dgn_ref):
        _acc_init(pl.program_id(0), dgn_ref)
        gn_ = gn_ref[...]
        for h in range(H):
            c = slice(h * HEAD, (h + 1) * HEAD)
            oh = o_ref[:, c]
            gp = gp_ref[:, c]
            dg = dog_ref[:, c]
            r = lax.rsqrt(_rowmean(oh * oh) + EPS)
            on = oh * r
            dgp_ref[:, c] = (dg * (on * gn_) * _dsilu(gp)).astype(BF16)
            don = dg * _silu(gp)
            dgn_ref[...] += _colsum(don * on)
            dn = don * gn_
            do_ref[:, c] = r * (dn - on * _rowmean(dn * on))

    return pl.pallas_call(
        body, grid=(T // tm,),
        in_specs=[_row(tm, D), _row(tm, D), pl.BlockSpec((tm, D), lambda i: (i, 3)), _full(1, HEAD)],
        out_specs=[_row(tm, D), _row(tm, D), _full(1, HEAD)],
        out_shape=[jax.ShapeDtypeStruct((T, D), F32), jax.ShapeDtypeStruct((T, D), BF16),
                   jax.ShapeDtypeStruct((1, HEAD), F32)], name=name,
        compiler_params=_cp("arbitrary"))(dog, o, proj, gn)


def _split2(x):
    hi = x.astype(BF16)
    return hi, (x - hi.astype(F32)).astype(BF16)


def ada_mod(c_all, ada_w, *, name):
    L, D, N = ada_w.shape
    B = c_all.shape[0]

    def body(c_ref, w_ref, o_ref):
        chi, clo = _split2(_silu(c_ref[...]))
        whi, wlo = _split2(w_ref[...])
        o_ref[...] = _dot(chi, whi) + _dot(chi, wlo) + _dot(clo, whi)

    return pl.pallas_call(
        body, grid=(L,), in_specs=[_full(B, D), pl.BlockSpec((None, D, N), lambda l: (l, 0, 0))],
        out_specs=pl.BlockSpec((None, B, N), lambda l: (l, 0, 0)),
        out_shape=jax.ShapeDtypeStruct((L, B, N), F32), name=name, compiler_params=_cp("parallel"))(c_all, ada_w)


def ada_wgrad(c_all_t, dmod, *, name, tr=256):
    D, B = c_all_t.shape
    L, _, N = dmod.shape
    tr = min(tr, D)

    def body(c_ref, d_ref, o_ref):
        cond = _silu(c_ref[...])
        acc = cond[:, 0:1] * d_ref[0:1, :]
        for b in range(1, B):
            acc = acc + cond[:, b:b + 1] * d_ref[b:b + 1, :]
        o_ref[...] = acc

    return pl.pallas_call(
        body, grid=(L, D // tr),
        in_specs=[pl.BlockSpec((tr, B), lambda l, r: (r, 0)), pl.BlockSpec((None, B, N), lambda l, r: (l, 0, 0))],
        out_specs=pl.BlockSpec((None, tr, N), lambda l, r: (l, r, 0)),
        out_shape=jax.ShapeDtypeStruct((L, D, N), F32), name=name,
        compiler_params=_cp("parallel", "parallel"))(c_all_t, dmod)


def sum_devices(parts, *, name):
    n, R, C = parts.shape

    def body(p_ref, o_ref):
        acc = p_ref[0]
        for d in range(1, n):
            acc = acc + p_ref[d]
        o_ref[...] = acc

    return pl.pallas_call(body, in_specs=[VMEM_SPEC], out_specs=VMEM_SPEC,
                          out_shape=jax.ShapeDtypeStruct((R, C), F32), name=name)(parts)


def lb_logits_grad(lb_logits, dlb, *, name):
    def body(lg_ref, d_ref, o_ref):
        l0, l1, l2 = lg_ref[0:1, :], lg_ref[1:2, :], lg_ref[2:3, :]
        m = jnp.maximum(jnp.maximum(l0, l1), l2)
        e0, e1, e2 = jnp.exp(l0 - m), jnp.exp(l1 - m), jnp.exp(l2 - m)
        z = e0 + e1 + e2
        p0, p1, p2 = e0 / z, e1 / z, e2 / z
        g = d_ref[...] * p0
        o_ref[0:1, :] = g * (1.0 - p0)
        o_ref[1:2, :] = -g * p1
        o_ref[2:3, :] = -g * p2

    return pl.pallas_call(body, in_specs=[VMEM_SPEC, VMEM_SPEC], out_specs=VMEM_SPEC,
                          out_shape=jax.ShapeDtypeStruct(lb_logits.shape, F32), name=name)(lb_logits, dlb)


def adamw(w, g, m, v, *, name, tr=256):
    R, C = w.shape
    tr = _tile(R, tr)

    def body(w_ref, g_ref, m_ref, v_ref, d_ref, nm_ref, nv_ref):
        gv = g_ref[...]
        nm = ADAM_B1 * m_ref[...] + (1.0 - ADAM_B1) * gv
        nv = ADAM_B2 * v_ref[...] + (1.0 - ADAM_B2) * (gv * gv)
        m_hat = nm / (1.0 - ADAM_B1 ** ADAM_STEP)
        v_hat = nv / (1.0 - ADAM_B2 ** ADAM_STEP)
        d_ref[...] = -ADAM_LR * (m_hat / (jnp.sqrt(v_hat) + ADAM_EPS) + ADAM_WD * w_ref[...])
        nm_ref[...] = nm
        nv_ref[...] = nv

    spec = pl.BlockSpec((tr, C), lambda i: (i, 0))
    return pl.pallas_call(
        body, grid=(R // tr,), in_specs=[spec] * 4, out_specs=[spec] * 3,
        out_shape=[jax.ShapeDtypeStruct((R, C), F32)] * 3, name=name, compiler_params=_cp("parallel"))(w, g, m, v)


def _place():
    return lax.axis_index("x"), lax.axis_index("y"), lax.axis_index("c")


def _flip(v, bit):
    return 1 - v if bit else v


def allgather_devices(v, *, name):
    R, C = v.shape

    def body(v_ref, out_ref, send_sems, recv_sems, local_sem):
        x, y, c = _place()
        me = 4 * x + 2 * y + c
        mine = pltpu.make_async_copy(v_ref, out_ref.at[me], local_sem)
        mine.start()
        sends = []
        for k in range(1, N_DEV):
            peer = (_flip(x, k & 4), _flip(y, k & 2), _flip(c, k & 1))
            cp = pltpu.make_async_remote_copy(src_ref=v_ref, dst_ref=out_ref.at[me], send_sem=send_sems.at[k - 1],
                                              recv_sem=recv_sems.at[k - 1], device_id=peer, device_id_type=MESH)
            cp.start()
            sends.append(cp)
        for k in range(1, N_DEV):
            px, py, pc = _flip(x, k & 4), _flip(y, k & 2), _flip(c, k & 1)
            pltpu.make_async_remote_copy(src_ref=v_ref, dst_ref=out_ref.at[4 * px + 2 * py + pc],
                                         send_sem=send_sems.at[k - 1], recv_sem=recv_sems.at[k - 1],
                                         device_id=(px, py, pc), device_id_type=MESH).wait_recv()
        for cp in sends:
            cp.wait_send()
        mine.wait()

    return pl.pallas_call(
        body, in_specs=[VMEM_SPEC], out_specs=VMEM_SPEC, out_shape=jax.ShapeDtypeStruct((N_DEV, R, C), v.dtype),
        scratch_shapes=[pltpu.SemaphoreType.DMA((N_DEV - 1,)), pltpu.SemaphoreType.DMA((N_DEV - 1,)),
                        pltpu.SemaphoreType.DMA], name=name)(v)


def _other_chips(x, y):
    return [(1 - x, y), (x, 1 - y), (1 - x, 1 - y)]


def allgather_chips(bufs, *, name):
    n = len(bufs)

    def body(*refs):
        outs = refs[n:2 * n]
        send_sems, recv_sems = refs[2 * n:]
        x, y, c = _place()
        q = 2 * x + y
        chips = _other_chips(x, y)

        def copy(a, k, block, half, to):
            slab = outs[a].at[block, half]
            return pltpu.make_async_remote_copy(src_ref=slab, dst_ref=slab, send_sem=send_sems.at[a, k],
                                                recv_sem=recv_sems.at[a, k], device_id=to, device_id_type=MESH)

        first = [copy(a, j, q, c, (*chips[j], c)) for a in range(n) for j in range(3)]
        for cp in first:
            cp.start()
        passed = []
        for a in range(n):
            for j, (px, py) in enumerate(chips):
                copy(a, j, 2 * px + py, c, (x, y, c)).wait_recv()
                fw = copy(a, 3 + j, 2 * px + py, c, (x, y, 1 - c))
                fw.start()
                passed.append(fw)
        for a in range(n):
            for j, (px, py) in enumerate(chips):
                copy(a, 3 + j, 2 * px + py, 1 - c, (x, y, c)).wait_recv()
        for cp in first + passed:
            cp.wait_send()

    return pl.pallas_call(
        body, in_specs=[HBM] * n, out_specs=[HBM] * n,
        out_shape=[jax.ShapeDtypeStruct(b.shape, b.dtype) for b in bufs],
        input_output_aliases={a: a for a in range(n)},
        scratch_shapes=[pltpu.SemaphoreType.DMA((n, 6)), pltpu.SemaphoreType.DMA((n, 6))], name=name)(*bufs)


SEM = pl.BlockSpec(memory_space=pltpu.SEMAPHORE)
DATAFLOW = pltpu.SideEffectType.DATAFLOW_SIDE_EFFECTING


def _chip_copy(buf, a, j, q, c, chips, send_sems, recv_sems):
    px, py = chips[j]
    return pltpu.make_async_remote_copy(src_ref=buf.at[q, c], dst_ref=buf.at[q, c], send_sem=send_sems.at[3 * a + j],
                                        recv_sem=recv_sems.at[3 * a + j], device_id=(px, py, c), device_id_type=MESH)


def allgather_chips_start(bufs, *, name):
    n = len(bufs)

    def body(*refs):
        send_sems, recv_sems = refs[n], refs[n + 1]
        outs = refs[n + 2:2 * n + 2]
        token = refs[2 * n + 2]
        x, y, c = _place()
        chips = _other_chips(x, y)
        for a in range(n):
            for j in range(3):
                _chip_copy(outs[a], a, j, 2 * x + y, c, chips, send_sems, recv_sems).start()
        token[...] = jnp.zeros_like(token)

    res = pl.pallas_call(
        body, name=name, in_specs=[HBM] * n,
        out_specs=(SEM, SEM, *([HBM] * n), VMEM_SPEC),
        out_shape=(pltpu.SemaphoreType.DMA((3 * n,)), pltpu.SemaphoreType.DMA((3 * n,)),
                   *[pltpu.HBM(b.shape, b.dtype) for b in bufs], jax.ShapeDtypeStruct((SUB, LANE), F32)),
        input_output_aliases={a: a + 2 for a in range(n)},
        compiler_params=pltpu.CompilerParams(has_side_effects=DATAFLOW),
    )(*[pltpu.with_memory_space_constraint(b, pltpu.HBM) for b in bufs])
    return res[0], res[1], list(res[2:2 + n]), res[2 + n]


def allgather_chips_wait(send_sems, recv_sems, bufs, after, *, name):
    n = len(bufs)

    def body(*refs):
        ins = refs[:n]
        send_sems, recv_sems = refs[n], refs[n + 1]
        x, y, c = _place()
        chips = _other_chips(x, y)
        for a in range(n):
            for j, (px, py) in enumerate(chips):
                _chip_copy(ins[a], a, j, 2 * x + y, c, chips, send_sems, recv_sems).wait_send()
                _chip_copy(ins[a], a, j, 2 * px + py, c, chips, send_sems, recv_sems).wait_recv()

    return list(pl.pallas_call(
        body, name=name, in_specs=[HBM] * n + [SEM, SEM, pl.BlockSpec(memory_space=pl.ANY)],
        out_specs=[HBM] * n, out_shape=[pltpu.HBM(b.shape, b.dtype) for b in bufs],
        input_output_aliases={a: a for a in range(n)},
        compiler_params=pltpu.CompilerParams(has_side_effects=DATAFLOW),
    )(*bufs, send_sems, recv_sems, after))


def forward_to_sibling(bufs, *, name):
    n = len(bufs)

    def body(*refs):
        outs = refs[n:2 * n]
        send_sems, recv_sems = refs[2 * n:]
        x, y, c = _place()
        chips = _other_chips(x, y)

        def copy(a, j, half, to):
            px, py = chips[j]
            slab = outs[a].at[2 * px + py, half]
            return pltpu.make_async_remote_copy(src_ref=slab, dst_ref=slab, send_sem=send_sems.at[a, j],
                                                recv_sem=recv_sems.at[a, j], device_id=to, device_id_type=MESH)

        sends = [copy(a, j, c, (x, y, 1 - c)) for a in range(n) for j in range(3)]
        for cp in sends:
            cp.start()
        for a in range(n):
            for j in range(3):
                copy(a, j, 1 - c, (x, y, c)).wait_recv()
        for cp in sends:
            cp.wait_send()

    return pl.pallas_call(
        body, in_specs=[HBM] * n, out_specs=[HBM] * n,
        out_shape=[jax.ShapeDtypeStruct(b.shape, b.dtype) for b in bufs],
        input_output_aliases={a: a for a in range(n)},
        scratch_shapes=[pltpu.SemaphoreType.DMA((n, 3)), pltpu.SemaphoreType.DMA((n, 3))], name=name)(*bufs)


def pair_exchange(grads, *, name):
    n = len(grads)

    def body(*refs):
        ins, outs = refs[:n], refs[n:2 * n]
        send_sems, recv_sems = refs[2 * n:]
        x, y, c = _place()
        cps = [pltpu.make_async_remote_copy(src_ref=ins[a].at[1 - c], dst_ref=outs[a], send_sem=send_sems.at[a],
                                            recv_sem=recv_sems.at[a], device_id=(x, y, 1 - c), device_id_type=MESH)
               for a in range(n)]
        for cp in cps:
            cp.start()
        for cp in cps:
            cp.wait_recv()
        for cp in cps:
            cp.wait_send()

    return pl.pallas_call(
        body, in_specs=[HBM] * n, out_specs=[HBM] * n,
        out_shape=[jax.ShapeDtypeStruct(g.shape[1:], g.dtype) for g in grads],
        scratch_shapes=[pltpu.SemaphoreType.DMA((n,)), pltpu.SemaphoreType.DMA((n,))], name=name)(*grads)


def pair_add(g, other, c_idx, *, name, tr=256):
    _, Q, R, C = g.shape
    tr = _tile(R, tr)

    def body(c_ref, g_ref, o_ref, out_ref):
        out_ref[...] = (g_ref[...] + o_ref[...]).astype(BF16)

    return pl.pallas_call(
        body,
        grid_spec=pltpu.PrefetchScalarGridSpec(
            num_scalar_prefetch=1, grid=(Q, R // tr),
            in_specs=[pl.BlockSpec((None, None, tr, C), lambda q, r, c_ref: (c_ref[0], q, r, 0)),
                      pl.BlockSpec((None, tr, C), lambda q, r, c_ref: (q, r, 0))],
            out_specs=pl.BlockSpec((None, tr, C), lambda q, r, c_ref: (q, r, 0))),
        out_shape=jax.ShapeDtypeStruct((Q, R, C), BF16), name=name,
        compiler_params=_cp("parallel", "parallel"))(c_idx, g, other)


def chip_exchange(sums, *, name):
    n = len(sums)

    def body(*refs):
        ins, outs = refs[:n], refs[n:2 * n]
        send_sems, recv_sems = refs[2 * n:]
        x, y, c = _place()
        chips = _other_chips(x, y)
        sends = [pltpu.make_async_remote_copy(src_ref=ins[a].at[2 * px + py], dst_ref=outs[a].at[j],
                                              send_sem=send_sems.at[a, j], recv_sem=recv_sems.at[a, j],
                                              device_id=(px, py, c), device_id_type=MESH)
                 for a in range(n) for j, (px, py) in enumerate(chips)]
        for cp in sends:
            cp.start()
        for cp in sends:
            cp.wait_recv()
        for cp in sends:
            cp.wait_send()

    return pl.pallas_call(
        body, in_specs=[HBM] * n, out_specs=[HBM] * n,
        out_shape=[jax.ShapeDtypeStruct((3,) + s.shape[1:], s.dtype) for s in sums],
        scratch_shapes=[pltpu.SemaphoreType.DMA((n, 3)), pltpu.SemaphoreType.DMA((n, 3))], name=name)(*sums)


def chip_sum(sums, landed, qc_idx, *, name, tr=256):
    _, R, C = sums.shape
    tr = _tile(R, tr)

    def body(qc_ref, own_ref, l_ref, o_ref):
        acc = own_ref[...].astype(F32)
        for k in range(3):
            acc = acc + l_ref[k].astype(F32)
        o_ref[...] = acc

    return pl.pallas_call(
        body,
        grid_spec=pltpu.PrefetchScalarGridSpec(
            num_scalar_prefetch=1, grid=(R // tr,),
            in_specs=[pl.BlockSpec((None, tr, C), lambda r, qc: (qc[0], r, 0)),
                      pl.BlockSpec((3, tr, C), lambda r, qc: (0, r, 0))],
            out_specs=pl.BlockSpec((None, tr, C), lambda r, qc: (qc[1], r, 0))),
        out_shape=jax.ShapeDtypeStruct((2, R, C), F32), name=name,
        compiler_params=_cp("parallel"))(qc_idx, sums, landed)


def half_swap(bufs, *, name):
    n = len(bufs)

    def body(*refs):
        outs = refs[n:2 * n]
        send_sems, recv_sems = refs[2 * n:]
        x, y, c = _place()
        cps = [pltpu.make_async_remote_copy(src_ref=outs[a].at[c], dst_ref=outs[a].at[c], send_sem=send_sems.at[a],
                                            recv_sem=recv_sems.at[a], device_id=(x, y, 1 - c), device_id_type=MESH)
               for a in range(n)]
        for cp in cps:
            cp.start()
        for a in range(n):
            pltpu.make_async_remote_copy(src_ref=outs[a].at[c], dst_ref=outs[a].at[1 - c], send_sem=send_sems.at[a],
                                         recv_sem=recv_sems.at[a], device_id=(x, y, 1 - c),
                                         device_id_type=MESH).wait_recv()
        for cp in cps:
            cp.wait_send()

    return pl.pallas_call(
        body, in_specs=[HBM] * n, out_specs=[HBM] * n,
        out_shape=[jax.ShapeDtypeStruct(b.shape, b.dtype) for b in bufs],
        input_output_aliases={a: a for a in range(n)},
        scratch_shapes=[pltpu.SemaphoreType.DMA((n,)), pltpu.SemaphoreType.DMA((n,))], name=name)(*bufs)


def _exchange_copy(sums, landed, a, j, c, chips, send_sems, recv_sems):
    px, py = chips[j]
    return pltpu.make_async_remote_copy(src_ref=sums.at[2 * px + py], dst_ref=landed.at[j],
                                        send_sem=send_sems.at[3 * a + j], recv_sem=recv_sems.at[3 * a + j],
                                        device_id=(px, py, c), device_id_type=MESH)


def chip_exchange_start(sums, *, name):
    n = len(sums)
    landing = [lax.empty((3,) + s.shape[1:], s.dtype) for s in sums]

    def body(*refs):
        send_sems, recv_sems = refs[2 * n], refs[2 * n + 1]
        src, dst = refs[2 * n + 2:3 * n + 2], refs[3 * n + 2:4 * n + 2]
        token = refs[4 * n + 2]
        x, y, c = _place()
        chips = _other_chips(x, y)
        for a in range(n):
            for j in range(3):
                _exchange_copy(src[a], dst[a], a, j, c, chips, send_sems, recv_sems).start()
        token[...] = jnp.zeros_like(token)

    res = pl.pallas_call(
        body, name=name, in_specs=[HBM] * (2 * n),
        out_specs=(SEM, SEM, *([HBM] * (2 * n)), VMEM_SPEC),
        out_shape=(pltpu.SemaphoreType.DMA((3 * n,)), pltpu.SemaphoreType.DMA((3 * n,)),
                   *[pltpu.HBM(b.shape, b.dtype) for b in sums + landing], jax.ShapeDtypeStruct((SUB, LANE), F32)),
        input_output_aliases={a: a + 2 for a in range(2 * n)},
        compiler_params=pltpu.CompilerParams(has_side_effects=DATAFLOW),
    )(*[pltpu.with_memory_space_constraint(b, pltpu.HBM) for b in sums + landing])
    return res[0], res[1], list(res[2:2 + n]), list(res[2 + n:2 + 2 * n]), res[2 + 2 * n]


def chip_exchange_wait(send_sems, recv_sems, sums, landed, after, *, name):
    n = len(sums)

    def body(*refs):
        src, dst = refs[:n], refs[n:2 * n]
        send_sems, recv_sems = refs[2 * n], refs[2 * n + 1]
        x, y, c = _place()
        chips = _other_chips(x, y)
        for a in range(n):
            for j in range(3):
                cp = _exchange_copy(src[a], dst[a], a, j, c, chips, send_sems, recv_sems)
                cp.wait_send()
                cp.wait_recv()

    res = pl.pallas_call(
        body, name=name, in_specs=[HBM] * (2 * n) + [SEM, SEM, pl.BlockSpec(memory_space=pl.ANY)],
        out_specs=[HBM] * (2 * n), out_shape=[pltpu.HBM(b.shape, b.dtype) for b in sums + landed],
        input_output_aliases={a: a for a in range(2 * n)},
        compiler_params=pltpu.CompilerParams(has_side_effects=DATAFLOW),
    )(*sums, *landed, send_sems, recv_sems, after)
    return list(res[:n]), list(res[n:])


def pair_reduce(grads, c, tag):
    c_idx = c.astype(jnp.int32).reshape(1)
    others = pair_exchange(grads, name=f"grad_pair_exchange_{tag}")
    return [pair_add(g, o, c_idx, name=f"grad_pair_add_{tag}{a}") for a, (g, o) in enumerate(zip(grads, others))]


def finish_reduce(sums, landed, q, c, tag):
    qc_idx = jnp.stack([q, c]).astype(jnp.int32)
    return [chip_sum(s, l, qc_idx, name=f"grad_chip_sum_{tag}{a}") for a, (s, l) in enumerate(zip(sums, landed))]


def _ffn_forward(x, mod, pre_g, post_g, w_up, w_down, dw_w, dw_b, tag):
    sh, sc, gate = mod
    h = prenorm(x, pre_g, sc, sh, name=f"{tag}_prenorm")
    u0 = mm_nn(h, w_up, name=f"{tag}_up", out_dtype=BF16, perm=_ffn_perm)
    z = ffn_act(u0, dw_w, dw_b, name=f"{tag}_act")
    y = mm_nn(z, w_down, name=f"{tag}_down")
    x_new = post_residual(x, y, post_g, gate, name=f"{tag}_post")
    return x_new, (x, h, u0, z, y)


def _ffn_backward(dx, saved, mod, pre_g, post_g, w_up, w_down, dw_w, dw_b, tag):
    x, h, u0, z, y = saved
    sh, sc, gate = mod
    dy, dgate, dpost, _ = post_bwd(dx, y, post_g, gate, name=f"{tag}_post_bwd")
    dz = mm_nt(dy, w_down, name=f"{tag}_down_dx", out_dtype=BF16)
    g_down = mm_tn(z, dy, name=f"{tag}_down_dw", J=2, block="a", chips_per_block=2)
    du0, dconv = ffn_act_bwd(dz, u0, dw_w, dw_b, name=f"{tag}_act_bwd")
    dh = mm_nt(du0, w_up, name=f"{tag}_up_dx", perm=_ffn_perm)
    g_up = mm_tn(h, du0, name=f"{tag}_up_dw", J=4, block="b", chips_per_block=1, perm=_ffn_perm)
    dx_in, dsh, dsc, dpre = prenorm_bwd(dh, x, dx, pre_g, sc, name=f"{tag}_prenorm_bwd")
    nb = u0.shape[1] // 4
    dconv = dconv[:, 0].reshape(4, 2, 2, nb).transpose(0, 2, 1, 3).reshape(4, 4 * nb)
    return dx_in, dict(dsh=dsh, dsc=dsc, dgate=dgate, dpre=dpre, dpost=dpost, g_up=g_up, g_down=g_down,
                       d_dw_w=dconv[0:FFN_W], d_dw_b=dconv[3:4])


def _local_step(x, tgt, mods, P, late_weights=None, layer1_grads_ready=None):
    m0, m1 = mods
    h1 = prenorm(x, P["pre_mix_g"][0:1], m0[1], m0[0], name="hgrn_prenorm")
    proj = mm_nn(h1, P["hgrn_w_in"], name="hgrn_in")
    o, states = hgrn_scan(proj, P["hgrn_lb_logits"], name="hgrn_scan")
    og = hgrn_gate(o, proj, P["hgrn_gnorm_g"], name="hgrn_gate")
    y1 = mm_nn(og, P["hgrn_w_out"], name="hgrn_out")
    x1 = post_residual(x, y1, P["post_mix_g"][0:1], m0[2], name="hgrn_post")
    if late_weights is not None:
        P = {**P, **late_weights(x1)}
    x2, ffn0 = _ffn_forward(x1, m0[3:6], P["pre_ffn_g"][0:1], P["post_ffn_g"][0:1], P["ffn_w_up"][0],
                            P["ffn_w_down"][0], P["ffn_dw_w"][0], P["ffn_dw_b"][0:1], "ffn0")
    h3 = prenorm(x2, P["pre_mix_g"][1:2], m1[1], m1[0], name="conv_prenorm")
    u = mm_nn(h3, P["conv_w_in"], name="conv_in", bias=P["conv_b_in"])
    s, cv = conv_act(u, P["conv_dw_w"], P["conv_dw_b"], P["conv_ln_g"], P["conv_ln_b"], name="conv_act")
    y3 = mm_nn(s, P["conv_w_out"], name="conv_out", bias=P["conv_b_out"])
    x3 = post_residual(x2, y3, P["post_mix_g"][1:2], m1[2], name="conv_post")
    x4, ffn1 = _ffn_forward(x3, m1[3:6], P["pre_ffn_g"][1:2], P["post_ffn_g"][1:2], P["ffn_w_up"][1],
                            P["ffn_w_down"][1], P["ffn_dw_w"][1], P["ffn_dw_b"][1:2], "ffn1")
    dx4, lcols = loss_grad(x4, tgt, name="loss")
    dx3, f1 = _ffn_backward(dx4, ffn1, m1[3:6], P["pre_ffn_g"][1:2], P["post_ffn_g"][1:2], P["ffn_w_up"][1],
                            P["ffn_w_down"][1], P["ffn_dw_w"][1], P["ffn_dw_b"][1:2], "ffn1")
    dy3, dg1_1, dpostmix1, d_b_out = post_bwd(dx3, y3, P["post_mix_g"][1:2], m1[2], name="conv_post_bwd")
    ds = mm_nt(dy3, P["conv_w_out"], name="conv_out_dx")
    g_conv_out = mm_tn(s, dy3, name="conv_out_dw", J=1, block="a", chips_per_block=4)
    dcv, d_ln_g, d_ln_b, d_dw_b = conv_norm_bwd(ds, cv, P["conv_ln_g"], P["conv_ln_b"], name="conv_norm_bwd")
    du, d_dw_w, d_b_in = conv_glu_bwd(dcv, u, P["conv_dw_w"], name="conv_glu_bwd")
    dh3 = mm_nt(du, P["conv_w_in"], name="conv_in_dx")
    g_conv_in = mm_tn(h3, du, name="conv_in_dw", J=4, block="b", chips_per_block=1)
    dx2, dsh1_1, dsc1_1, dpremix1 = prenorm_bwd(dh3, x2, dx3, P["pre_mix_g"][1:2], m1[1], name="conv_prenorm_bwd")
    if layer1_grads_ready is not None:
        token = layer1_grads_ready([g_conv_in, g_conv_out, f1["g_up"], f1["g_down"]])
        m0 = tuple(m + token[0:1, 0:1] for m in m0)
    dx1, f0 = _ffn_backward(dx2, ffn0, m0[3:6], P["pre_ffn_g"][0:1], P["post_ffn_g"][0:1], P["ffn_w_up"][0],
                            P["ffn_w_down"][0], P["ffn_dw_w"][0], P["ffn_dw_b"][0:1], "ffn0")
    dy1, dg1_0, dpostmix0, _ = post_bwd(dx1, y1, P["post_mix_g"][0:1], m0[2], name="hgrn_post_bwd")
    dog = mm_nt(dy1, P["hgrn_w_out"], name="hgrn_out_dx")
    g_hgrn_out = mm_tn(og, dy1, name="hgrn_out_dw", J=1, block="a", chips_per_block=4)
    do, dgp, d_gn = hgrn_gate_bwd(dog, o, proj, P["hgrn_gnorm_g"], name="hgrn_gate_bwd")
    dqp, dfz, dv, dlb = hgrn_scan_bwd(proj, P["hgrn_lb_logits"], states, do, name="hgrn_scan_bwd")
    dproj = jnp.concatenate([dqp, dfz, dv, dgp], axis=1)
    dh1 = mm_nt(dproj, P["hgrn_w_in"], name="hgrn_in_dx")
    g_hgrn_in = mm_tn(h1, dproj, name="hgrn_in_dw", J=4, block="b", chips_per_block=1)
    dx0, dsh1_0, dsc1_0, dpremix0 = prenorm_bwd(dh1, x, dx1, P["pre_mix_g"][0:1], m0[1], name="hgrn_prenorm_bwd")

    dmod = jnp.stack([
        jnp.concatenate([dsh1_0, dsc1_0, dg1_0, f0["dsh"], f0["dsc"], f0["dgate"]], axis=1)[0],
        jnp.concatenate([dsh1_1, dsc1_1, dg1_1, f1["dsh"], f1["dsc"], f1["dgate"]], axis=1)[0]])
    small = dict(
        loss=lcols,
        pre_mix_g=jnp.concatenate([dpremix0, dpremix1]), post_mix_g=jnp.concatenate([dpostmix0, dpostmix1]),
        pre_ffn_g=jnp.concatenate([f0["dpre"], f1["dpre"]]), post_ffn_g=jnp.concatenate([f0["dpost"], f1["dpost"]]),
        lb=dlb, hgrn_gnorm_g=d_gn, ffn_dw_b=jnp.concatenate([f0["d_dw_b"], f1["d_dw_b"]]), dmod=dmod,
        conv_b_in=d_b_in, conv_dw_w=d_dw_w[0:CONV_W], conv_dw_b=d_dw_b, conv_ln_g=d_ln_g, conv_ln_b=d_ln_b,
        conv_b_out=d_b_out, ffn_dw_w=jnp.stack([f0["d_dw_w"], f1["d_dw_w"]]))
    big = [g_hgrn_in, g_hgrn_out, g_conv_in, g_conv_out, f0["g_up"], f1["g_up"], f0["g_down"], f1["g_down"]]
    return dx0, small, big


def _pack(parts, rows=8):
    flat = jnp.concatenate([p.reshape(-1).astype(F32) for p in parts])
    per = rows * 128
    pad = (-flat.shape[0]) % per
    return jnp.pad(flat, (0, pad)).reshape(rows, -1)


def _unpack(flat, shapes):
    out, off = [], 0
    for s in shapes:
        n = 1
        for d in s:
            n *= d
        out.append(flat[..., off:off + n].reshape(flat.shape[:-1] + tuple(s)))
        off += n
    return out


def _from_chips(stacked, axis):
    moved = jnp.moveaxis(stacked, 0, axis)
    shape = list(moved.shape)
    return moved.reshape(shape[:axis] + [shape[axis] * shape[axis + 1]] + shape[axis + 2:])


def _my_shard(full, axis, q):
    n = full.shape[axis] // N_CHIPS
    return lax.dynamic_slice_in_dim(full, q * n, n, axis=axis)


def kernel(x, c, ada_w, ada_b, pre_mix_g, post_mix_g, pre_ffn_g, post_ffn_g, hgrn_w_in, hgrn_lb_logits, hgrn_gnorm_g, hgrn_w_out, conv_w_in, conv_b_in, conv_dw_w, conv_dw_b, conv_ln_g, conv_ln_b, conv_w_out, conv_b_out, ffn_w_up, ffn_dw_w, ffn_dw_b, ffn_w_down, loss_target, m_ada_w, m_ada_b, m_pre_mix_g, m_post_mix_g, m_pre_ffn_g, m_post_ffn_g, m_hgrn_w_in, m_hgrn_lb_logits, m_hgrn_gnorm_g, m_hgrn_w_out, m_conv_w_in, m_conv_b_in, m_conv_dw_w, m_conv_dw_b, m_conv_ln_g, m_conv_ln_b, m_conv_w_out, m_conv_b_out, m_ffn_w_up, m_ffn_dw_w, m_ffn_dw_b, m_ffn_w_down, v_ada_w, v_ada_b, v_pre_mix_g, v_post_mix_g, v_pre_ffn_g, v_post_ffn_g, v_hgrn_w_in, v_hgrn_lb_logits, v_hgrn_gnorm_g, v_hgrn_w_out, v_conv_w_in, v_conv_b_in, v_conv_dw_w, v_conv_dw_b, v_conv_ln_g, v_conv_ln_b, v_conv_w_out, v_conv_b_out, v_ffn_w_up, v_ffn_dw_w, v_ffn_dw_b, v_ffn_w_down):
    W = dict(ada_w=ada_w, ada_b=ada_b, pre_mix_g=pre_mix_g, post_mix_g=post_mix_g, pre_ffn_g=pre_ffn_g,
             post_ffn_g=post_ffn_g, hgrn_w_in=hgrn_w_in, hgrn_lb_logits=hgrn_lb_logits, hgrn_gnorm_g=hgrn_gnorm_g,
             hgrn_w_out=hgrn_w_out, conv_w_in=conv_w_in, conv_b_in=conv_b_in, conv_dw_w=conv_dw_w,
             conv_dw_b=conv_dw_b, conv_ln_g=conv_ln_g, conv_ln_b=conv_ln_b, conv_w_out=conv_w_out,
             conv_b_out=conv_b_out, ffn_w_up=ffn_w_up, ffn_dw_w=ffn_dw_w, ffn_dw_b=ffn_dw_b, ffn_w_down=ffn_w_down)
    M = dict(ada_w=m_ada_w, ada_b=m_ada_b, pre_mix_g=m_pre_mix_g, post_mix_g=m_post_mix_g, pre_ffn_g=m_pre_ffn_g,
             post_ffn_g=m_post_ffn_g, hgrn_w_in=m_hgrn_w_in, hgrn_lb_logits=m_hgrn_lb_logits,
             hgrn_gnorm_g=m_hgrn_gnorm_g, hgrn_w_out=m_hgrn_w_out, conv_w_in=m_conv_w_in, conv_b_in=m_conv_b_in,
             conv_dw_w=m_conv_dw_w, conv_dw_b=m_conv_dw_b, conv_ln_g=m_conv_ln_g, conv_ln_b=m_conv_ln_b,
             conv_w_out=m_conv_w_out, conv_b_out=m_conv_b_out, ffn_w_up=m_ffn_w_up, ffn_dw_w=m_ffn_dw_w,
             ffn_dw_b=m_ffn_dw_b, ffn_w_down=m_ffn_w_down)
    V = dict(ada_w=v_ada_w, ada_b=v_ada_b, pre_mix_g=v_pre_mix_g, post_mix_g=v_post_mix_g, pre_ffn_g=v_pre_ffn_g,
             post_ffn_g=v_post_ffn_g, hgrn_w_in=v_hgrn_w_in, hgrn_lb_logits=v_hgrn_lb_logits,
             hgrn_gnorm_g=v_hgrn_gnorm_g, hgrn_w_out=v_hgrn_w_out, conv_w_in=v_conv_w_in, conv_b_in=v_conv_b_in,
             conv_dw_w=v_conv_dw_w, conv_dw_b=v_conv_dw_b, conv_ln_g=v_conv_ln_g, conv_ln_b=v_conv_ln_b,
             conv_w_out=v_conv_w_out, conv_b_out=v_conv_b_out, ffn_w_up=v_ffn_w_up, ffn_dw_w=v_ffn_dw_w,
             ffn_dw_b=v_ffn_dw_b, ffn_w_down=v_ffn_w_down)
    names = list(W)
    xi, yi, ci = lax.axis_index("x"), lax.axis_index("y"), lax.axis_index("c")
    q = 2 * xi + yi
    me = 2 * q + ci
    D = x.shape[-1]
    L = ada_w.shape[0]

    small_w = ["conv_b_in", "conv_dw_w", "conv_dw_b", "conv_ln_g", "conv_ln_b", "conv_b_out", "ffn_dw_w"]
    small_axis = dict(conv_b_in=1, conv_dw_w=2, conv_dw_b=1, conv_ln_g=1, conv_ln_b=1, conv_b_out=1, ffn_dw_w=2)
    packed = _pack([c] + [W[n] for n in small_w])
    gathered = allgather_devices(packed, name="gather_small_params").reshape(N_DEV, -1)
    c_all = gathered[:, 0:D]
    per_chip = gathered.reshape(N_CHIPS, 2, -1)[:, 0, D:]
    parts = _unpack(per_chip, [W[n].shape for n in small_w])
    P = {n: _from_chips(p, small_axis[n]) for n, p in zip(small_w, parts)}
    P["conv_dw_w"] = P["conv_dw_w"][0]
    for n in ("pre_mix_g", "post_mix_g", "pre_ffn_g", "post_ffn_g", "hgrn_lb_logits", "hgrn_gnorm_g", "ffn_dw_b"):
        P[n] = W[n]

    modp = ada_mod(c_all, ada_w, name="ada_mod")
    ncol = modp.shape[-1]
    mod_all = allgather_devices(modp.reshape(L * N_DEV, ncol), name="gather_mod")
    mod_all = mod_all.reshape(N_CHIPS, 2, L, N_DEV, ncol)[:, 0]
    mod_me = lax.dynamic_index_in_dim(mod_all, me, axis=2, keepdims=False)
    mod = mod_me.transpose(1, 0, 2).reshape(L, N_CHIPS * ncol) + ada_b
    mods = [tuple(mod[l:l + 1, k * D:(k + 1) * D] for k in range(6)) for l in range(L)]

    def halves(w):
        shard = w.astype(BF16).reshape(1, 2, w.shape[0] // 2, w.shape[1])
        buf = lax.empty((N_CHIPS,) + shard.shape[1:], BF16)
        return lax.dynamic_update_slice_in_dim(buf, shard, q, axis=0)

    stack = lambda t: t.reshape(N_CHIPS, t.shape[1] * t.shape[2], t.shape[3])
    rowsh = lambda t: t.reshape(1, N_CHIPS * t.shape[1] * t.shape[2], t.shape[3])
    g = allgather_chips([halves(hgrn_w_in[0]), halves(hgrn_w_out[0])], name="gather_hgrn_weights")
    P["hgrn_w_in"], P["hgrn_w_out"] = stack(g[0]), rowsh(g[1])
    late_shards = [conv_w_in[0], conv_w_out[0], ffn_w_up[0], ffn_w_up[1], ffn_w_down[0], ffn_w_down[1]]
    send_sems, recv_sems, bufs, token = allgather_chips_start([halves(w) for w in late_shards], name="gather_weights_start")
    mods[0] = tuple(m + token[0:1, 0:1] for m in mods[0])

    def late_weights(x1):
        landed = allgather_chips_wait(send_sems, recv_sems, bufs, x1, name="gather_weights_wait")
        g = forward_to_sibling(landed, name="gather_weights_forward")
        return dict(conv_w_in=stack(g[0]), conv_w_out=rowsh(g[1]), ffn_w_up=[stack(g[2]), stack(g[3])],
                    ffn_w_down=[rowsh(g[4]), rowsh(g[5])])

    in_flight = {}

    def layer1_grads_ready(grads):
        sums = pair_reduce(grads, ci, "l1_")
        send, recv, sums, landing, tok = chip_exchange_start(sums, name="grad_chip_exchange_start")
        in_flight.update(send=send, recv=recv, sums=sums, landing=landing)
        return tok

    grad_x, small, big = _local_step(x[0], loss_target[0], mods, P, late_weights, layer1_grads_ready)

    small_names = list(small)
    gs = allgather_devices(_pack([small[n] for n in small_names]), name="gather_small_grads")
    dmod_all = _unpack(gs.reshape(N_DEV, -1), [small[n].shape for n in small_names])[small_names.index("dmod")]
    tot = sum_devices(gs, name="sum_small_grads").reshape(1, -1)
    S = dict(zip(small_names, _unpack(tot, [small[n].shape for n in small_names])))
    S = {n: v[0] for n, v in S.items()}
    loss = 0.5 * jnp.sum(S["loss"]) / D

    G = {}
    dmod_q = lax.dynamic_slice_in_dim(dmod_all, q * ncol, ncol, axis=2)
    G["ada_w"] = ada_wgrad(c_all.T, dmod_q.transpose(1, 0, 2), name="ada_wgrad")
    G["ada_b"] = S["dmod"]
    for n in ("pre_mix_g", "post_mix_g", "pre_ffn_g", "post_ffn_g", "hgrn_gnorm_g", "ffn_dw_b"):
        G[n] = S[n]
    G["hgrn_lb_logits"] = lb_logits_grad(hgrn_lb_logits, S["lb"], name="lb_logits_grad")
    G["conv_b_in"] = _my_shard(S["conv_b_in"], 1, q)
    G["conv_dw_w"] = _my_shard(S["conv_dw_w"], 1, q)[None]
    for n in ("conv_dw_b", "conv_ln_g", "conv_ln_b", "conv_b_out"):
        G[n] = _my_shard(S[n], 1, q)
    G["ffn_dw_w"] = _my_shard(S["ffn_dw_w"], 2, q)

    layer0 = [big[0], big[1], big[4], big[6]]
    sums0 = pair_reduce(layer0, ci, "l0_")
    landed0 = chip_exchange(sums0, name="grad_chip_exchange")
    sums1, landed1 = chip_exchange_wait(in_flight["send"], in_flight["recv"], in_flight["sums"], in_flight["landing"],
                                        grad_x, name="grad_chip_exchange_wait")
    halves = finish_reduce(sums0, landed0, q, ci, "l0_") + finish_reduce(sums1, landed1, q, ci, "l1_")
    red = [f.reshape(2 * f.shape[1], f.shape[2]) for f in half_swap(halves, name="grad_half_swap")]
    G["hgrn_w_in"], G["hgrn_w_out"], G["conv_w_in"], G["conv_w_out"] = red[0][None], red[1][None], red[4][None], red[5][None]
    G["ffn_w_up"] = jnp.stack([red[2], red[6]])
    G["ffn_w_down"] = jnp.stack([red[3], red[7]])

    delta, new_m, new_v = {}, {}, {}
    big_names = ["ada_w", "hgrn_w_in", "hgrn_w_out", "conv_w_in", "conv_w_out", "ffn_w_up", "ffn_w_down"]
    for n in big_names:
        shp = W[n].shape
        two = lambda t: t.reshape(-1, shp[-1])
        d_, m_, v_ = adamw(two(W[n]), two(G[n]), two(M[n]), two(V[n]), name=f"adamw_{n}")
        delta[n], new_m[n], new_v[n] = d_.reshape(shp), m_.reshape(shp), v_.reshape(shp)
    rest = [n for n in names if n not in big_names]
    d_, m_, v_ = adamw(_pack([W[n] for n in rest]), _pack([G[n] for n in rest]), _pack([M[n] for n in rest]),
                       _pack([V[n] for n in rest]), name="adamw_small")
    shapes = [W[n].shape for n in rest]
    for n, a, b_, c_ in zip(rest, _unpack(d_.reshape(-1), shapes), _unpack(m_.reshape(-1), shapes),
                            _unpack(v_.reshape(-1), shapes)):
        delta[n], new_m[n], new_v[n] = a, b_, c_

    return (loss, grad_x[None], *[G[n].reshape(W[n].shape) for n in names], *[delta[n] for n in names],
            *[new_m[n] for n in names], *[new_v[n] for n in names])
```

```python
import jax
import jax.numpy as jnp
from jax import lax
from jax.experimental import pallas as pl
from jax.experimental.pallas import tpu as pltpu

F32 = jnp.float32
BF16 = jnp.bfloat16
EPS = 1e-6
HEAD = 128
BLK = 16
NEG = -1e30
CONV_W = 31
FFN_W = 3
N_CHIPS = 4
N_DEV = 8
SUB = 8
LANE = 128
V7X_VMEM_LIMIT = 56 * 1024 * 1024
MESH = pl.DeviceIdType.MESH
HBM = pl.BlockSpec(memory_space=pltpu.HBM)
VMEM_SPEC = pl.BlockSpec(memory_space=pltpu.VMEM)

ADAM_LR = 0.001
ADAM_B1 = 0.9
ADAM_B2 = 0.999
ADAM_EPS = 1e-08
ADAM_WD = 0.01
ADAM_STEP = 10


def _cp(*sem):
    return pltpu.CompilerParams(dimension_semantics=sem, vmem_limit_bytes=V7X_VMEM_LIMIT)


def _sig(x):
    return 0.5 * jnp.tanh(0.5 * x) + 0.5


def _silu(x):
    return x * _sig(x)


def _dsilu(x):
    s = _sig(x)
    return s * (1.0 + x * (1.0 - s))


def _dot(a, b):
    return jnp.dot(a, b, preferred_element_type=F32)


def _dot_nt(a, b):
    return lax.dot_general(a, b, (((1,), (1,)), ((), ())), preferred_element_type=F32)


def _dot_tn(a, b):
    return lax.dot_general(a, b, (((0,), (0,)), ((), ())), preferred_element_type=F32)


def _colsum(x):
    return jnp.sum(x, axis=0, keepdims=True)


def _rowmean(x):
    return jnp.mean(x, axis=-1, keepdims=True)


def _ffn_perm(j):
    return (j % 2) * 2 + j // 2


def _tile(n, pref):
    if n <= pref:
        return n
    t = pref - pref % 8
    while n % t:
        t -= 8
    return t


def mm_nn(a, w, *, name, bias=None, out_dtype=F32, perm=None, tm=512):
    T, K = a.shape
    J, _, nb = w.shape
    tm = min(tm, T)
    col = (lambda j: j) if perm is None else perm

    def body(a_ref, w_ref, *rest):
        acc = _dot(a_ref[...], w_ref[...])
        if bias is not None:
            acc = acc + rest[0][...]
        rest[-1][...] = acc.astype(out_dtype)

    in_specs = [pl.BlockSpec((tm, K), lambda j, i: (i, 0)), pl.BlockSpec((None, K, nb), lambda j, i: (j, 0, 0))]
    args = [a, w]
    if bias is not None:
        in_specs.append(pl.BlockSpec((1, nb), lambda j, i: (0, j)))
        args.append(bias)
    return pl.pallas_call(
        body, grid=(J, T // tm), in_specs=in_specs,
        out_specs=pl.BlockSpec((tm, nb), lambda j, i: (i, col(j))),
        out_shape=jax.ShapeDtypeStruct((T, J * nb), out_dtype), name=name,
        compiler_params=_cp("parallel", "parallel"))(*args)


def mm_nt(a, w, *, name, out_dtype=F32, perm=None, tm=512):
    T = a.shape[0]
    J, K, nb = w.shape
    tm = min(tm, T)
    col = (lambda j: j) if perm is None else perm

    def body(a_ref, w_ref, o_ref, acc_ref):
        j = pl.program_id(1)

        @pl.when(j == 0)
        def _():
            acc_ref[...] = jnp.zeros_like(acc_ref)

        acc_ref[...] += _dot_nt(a_ref[...], w_ref[...])

        @pl.when(j == J - 1)
        def _():
            o_ref[...] = acc_ref[...].astype(out_dtype)

    return pl.pallas_call(
        body, grid=(T // tm, J),
        in_specs=[pl.BlockSpec((tm, nb), lambda i, j: (i, col(j))), pl.BlockSpec((None, K, nb), lambda i, j: (j, 0, 0))],
        out_specs=pl.BlockSpec((tm, K), lambda i, j: (i, 0)),
        out_shape=jax.ShapeDtypeStruct((T, K), out_dtype),
        scratch_shapes=[pltpu.VMEM((tm, K), F32)], name=name,
        compiler_params=_cp("parallel", "arbitrary"))(a, w)


def mm_tn(a, b, *, name, J, block, chips_per_block, perm=None, tk=512):
    T = a.shape[0]
    tk = min(tk, T)
    col = (lambda j: j) if perm is None else perm
    if block == "b":
        rows, nb = a.shape[1], b.shape[1] // J
        a_spec = pl.BlockSpec((tk, rows), lambda j, t: (t, 0))
        b_spec = pl.BlockSpec((tk, nb), lambda j, t: (t, col(j)))
    else:
        rows, nb = a.shape[1] // J, b.shape[1]
        a_spec = pl.BlockSpec((tk, rows), lambda j, t: (t, col(j)))
        b_spec = pl.BlockSpec((tk, nb), lambda j, t: (t, 0))
    cpb = chips_per_block
    rh = rows // (2 * cpb)

    def body(a_ref, b_ref, o_ref):
        @pl.when(pl.program_id(1) == 0)
        def _():
            o_ref[...] = jnp.zeros_like(o_ref)

        acc = _dot_tn(a_ref[...], b_ref[...])
        for ch in range(cpb):
            for hf in range(2):
                r0 = (ch * 2 + hf) * rh
                o_ref[hf, ch] += acc[r0:r0 + rh, :]

    return pl.pallas_call(
        body, grid=(J, T // tk), in_specs=[a_spec, b_spec],
        out_specs=pl.BlockSpec((2, cpb, rh, nb), lambda j, t: (0, j, 0, 0)),
        out_shape=jax.ShapeDtypeStruct((2, J * cpb, rh, nb), F32), name=name,
        compiler_params=_cp("parallel", "arbitrary"))(a, b)


def _row(tm, w):
    return pl.BlockSpec((tm, w), lambda i: (i, 0))


def _full(r, w):
    return pl.BlockSpec((r, w), lambda i: (0, 0))


def _acc_init(i, *refs):
    @pl.when(i == 0)
    def _():
        for r in refs:
            r[...] = jnp.zeros_like(r)


def prenorm(x, g, sc, sh, *, name, tm=512):
    T, D = x.shape
    tm = min(tm, T)

    def body(x_ref, g_ref, sc_ref, sh_ref, h_ref):
        xv = x_ref[...]
        r = lax.rsqrt(_rowmean(xv * xv) + EPS)
        h_ref[...] = ((xv * r) * g_ref[...] * (1.0 + sc_ref[...]) + sh_ref[...]).astype(BF16)

    return pl.pallas_call(
        body, grid=(T // tm,), in_specs=[_row(tm, D), _full(1, D), _full(1, D), _full(1, D)],
        out_specs=_row(tm, D), out_shape=jax.ShapeDtypeStruct((T, D), BF16), name=name,
        compiler_params=_cp("parallel"))(x, g, sc, sh)


def post_residual(x, y, g, gate, *, name, tm=512):
    T, D = x.shape
    tm = min(tm, T)

    def body(x_ref, y_ref, g_ref, gate_ref, o_ref):
        yv = y_ref[...]
        r = lax.rsqrt(_rowmean(yv * yv) + EPS)
        o_ref[...] = x_ref[...] + gate_ref[...] * ((yv * r) * g_ref[...])

    return pl.pallas_call(
        body, grid=(T // tm,), in_specs=[_row(tm, D), _row(tm, D), _full(1, D), _full(1, D)],
        out_specs=_row(tm, D), out_shape=jax.ShapeDtypeStruct((T, D), F32), name=name,
        compiler_params=_cp("parallel"))(x, y, g, gate)


def loss_grad(x, tgt, *, name, tm=512):
    T, D = x.shape
    tm = min(tm, T)

    def body(x_ref, t_ref, dx_ref, l_ref):
        _acc_init(pl.program_id(0), l_ref)
        e = x_ref[...] - t_ref[...]
        dx_ref[...] = e * (1.0 / D)
        l_ref[...] += _colsum(e * e)

    return pl.pallas_call(
        body, grid=(T // tm,), in_specs=[_row(tm, D), _row(tm, D)],
        out_specs=[_row(tm, D), _full(1, D)],
        out_shape=[jax.ShapeDtypeStruct((T, D), F32), jax.ShapeDtypeStruct((1, D), F32)], name=name,
        compiler_params=_cp("arbitrary"))(x, tgt)


def post_bwd(dx, y, g, gate, *, name, tm=512):
    T, D = dx.shape
    tm = min(tm, T)

    def body(dx_ref, y_ref, g_ref, gate_ref, dy_ref, dgate_ref, dg_ref, dbias_ref):
        _acc_init(pl.program_id(0), dgate_ref, dg_ref, dbias_ref)
        yv = y_ref[...]
        dxv = dx_ref[...]
        r = lax.rsqrt(_rowmean(yv * yv) + EPS)
        yn = yv * r
        gv = g_ref[...]
        gt = gate_ref[...]
        dgate_ref[...] += _colsum(dxv * (yn * gv))
        dg_ref[...] += _colsum(dxv * gt * yn)
        dyn = dxv * gt * gv
        dy = r * (dyn - yn * _rowmean(dyn * yn))
        dbias_ref[...] += _colsum(dy)
        dy_ref[...] = dy.astype(BF16)

    return pl.pallas_call(
        body, grid=(T // tm,), in_specs=[_row(tm, D), _row(tm, D), _full(1, D), _full(1, D)],
        out_specs=[_row(tm, D), _full(1, D), _full(1, D), _full(1, D)],
        out_shape=[jax.ShapeDtypeStruct((T, D), BF16)] + [jax.ShapeDtypeStruct((1, D), F32)] * 3, name=name,
        compiler_params=_cp("arbitrary"))(dx, y, g, gate)


def prenorm_bwd(dh, x, dres, g, sc, *, name, tm=512):
    T, D = x.shape
    tm = min(tm, T)

    def body(dh_ref, x_ref, dres_ref, g_ref, sc_ref, dx_ref, dsh_ref, dsc_ref, dg_ref):
        _acc_init(pl.program_id(0), dsh_ref, dsc_ref, dg_ref)
        xv = x_ref[...]
        dhv = dh_ref[...]
        r = lax.rsqrt(_rowmean(xv * xv) + EPS)
        xn = xv * r
        gv = g_ref[...]
        one_sc = 1.0 + sc_ref[...]
        dsh_ref[...] += _colsum(dhv)
        dsc_ref[...] += _colsum(dhv * (xn * gv))
        dg_ref[...] += _colsum(dhv * one_sc * xn)
        dxn = dhv * one_sc * gv
        dx_ref[...] = dres_ref[...] + r * (dxn - xn * _rowmean(dxn * xn))

    return pl.pallas_call(
        body, grid=(T // tm,), in_specs=[_row(tm, D), _row(tm, D), _row(tm, D), _full(1, D), _full(1, D)],
        out_specs=[_row(tm, D), _full(1, D), _full(1, D), _full(1, D)],
        out_shape=[jax.ShapeDtypeStruct((T, D), F32)] + [jax.ShapeDtypeStruct((1, D), F32)] * 3, name=name,
        compiler_params=_cp("arbitrary"))(dh, x, dres, g, sc)


HALO = 16


def _shift_helpers():
    rid = lax.broadcasted_iota(jnp.int32, (SUB, LANE), 0)

    def down(cur, prev, k):
        return pltpu.roll(jnp.where(rid >= SUB - k, prev, cur), k, 0)

    def up(cur, nxt, k):
        return pltpu.roll(jnp.where(rid < k, nxt, cur), SUB - k, 0)

    return down, up


def _ffn_sides(c, nb, wa_ref, wb_ref, ba_ref, bb_ref):
    cols = slice(c * LANE, (c + 1) * LANE)
    return [(cols, [wa_ref[k:k + 1, cols] for k in range(FFN_W)], ba_ref[:, cols]),
            (slice(nb + c * LANE, nb + (c + 1) * LANE), [wb_ref[k:k + 1, cols] for k in range(FFN_W)],
             bb_ref[:, cols])]


def _ffn_specs(tm, nb, hb, idx):
    return [pl.BlockSpec((tm, 2 * nb), lambda jc, i: (idx(i), jc)),
            pl.BlockSpec((HALO, 2 * nb), lambda jc, i: (jnp.maximum(idx(i) * hb - 1, 0), jc)),
            pl.BlockSpec((FFN_W, nb), lambda jc, i: (0, jc)),
            pl.BlockSpec((FFN_W, nb), lambda jc, i: (0, jc + 2)),
            pl.BlockSpec((1, nb), lambda jc, i: (0, jc)),
            pl.BlockSpec((1, nb), lambda jc, i: (0, jc + 2))]


def ffn_act(u0p, dw_w, dw_b, *, name, tm=256):
    T, W = u0p.shape
    nb = W // 4
    tm = min(tm, T)
    unroll = 4
    rows16 = 2 * SUB

    def body(u_ref, halo_ref, wa_ref, wb_ref, ba_ref, bb_ref, z_ref):
        i = pl.program_id(1)
        down, _ = _shift_helpers()
        for c in range(nb // LANE):
            cols = slice(c * LANE, (c + 1) * LANE)
            side = _ffn_sides(c, nb, wa_ref, wb_ref, ba_ref, bb_ref)

            def rows(j, prev):
                prev = list(prev)
                for m in range(unroll):
                    r0 = pl.multiple_of((j * unroll + m) * rows16, rows16)
                    x = [u_ref[pl.ds(r0, rows16), cs].astype(F32) for cs, _, _ in side]
                    zs = []
                    for hf in range(2):
                        conv = []
                        for n, (_, w, b) in enumerate(side):
                            cur = x[n][hf * SUB:(hf + 1) * SUB, :]
                            conv.append(b + w[2] * cur + w[1] * down(cur, prev[n], 1) + w[0] * down(cur, prev[n], 2))
                            prev[n] = cur
                        zs.append(_silu(conv[0]) * conv[1])
                    z_ref[pl.ds(r0, rows16), cols] = jnp.concatenate(zs, axis=0).astype(BF16)
                return tuple(prev)

            first = [jnp.where(i == 0, 0.0, halo_ref[:, cs].astype(F32)[SUB:2 * SUB, :]) for cs, _, _ in side]
            lax.fori_loop(0, tm // (rows16 * unroll), rows, tuple(first))

    return pl.pallas_call(
        body, grid=(2, T // tm), in_specs=_ffn_specs(tm, nb, tm // HALO, lambda i: i),
        out_specs=pl.BlockSpec((tm, nb), lambda jc, i: (i, jc)),
        out_shape=jax.ShapeDtypeStruct((T, 2 * nb), BF16), name=name,
        compiler_params=_cp("parallel", "arbitrary"))(u0p, u0p, dw_w, dw_w, dw_b, dw_b)


def ffn_act_bwd(dz, u0p, dw_w, dw_b, *, name, tm=256):
    T, W = u0p.shape
    nb = W // 4
    tm = min(tm, T)
    nt = T // tm
    unroll = 2
    rows16 = 2 * SUB
    n_it = tm // (rows16 * unroll)

    def body(dz_ref, u_ref, halo_ref, wa_ref, wb_ref, ba_ref, bb_ref, du0_ref, dw_ref, carry):
        i = pl.program_id(1)
        _acc_init(i, dw_ref)
        down, up = _shift_helpers()
        for c in range(nb // LANE):
            cols = slice(c * LANE, (c + 1) * LANE)
            side = _ffn_sides(c, nb, wa_ref, wb_ref, ba_ref, bb_ref)
            halo = [jnp.where(i == nt - 1, 0.0, halo_ref[:, cs].astype(F32)[SUB:2 * SUB, :]) for cs, _, _ in side]

            def rows(j, st):
                nxt, acc, x = list(st[0:2]), list(st[2:10]), list(st[10:12])
                for m in range(unroll):
                    r0 = pl.multiple_of(((n_it - 1 - j) * unroll + unroll - 1 - m) * rows16, rows16)
                    rp = pl.multiple_of(jnp.maximum(r0 - rows16, 0), rows16)
                    dzv = dz_ref[pl.ds(r0, rows16), cols].astype(F32)
                    chunk = []
                    for n, (cs, _, _) in enumerate(side):
                        before = u_ref[pl.ds(rp, rows16), cs].astype(F32)
                        chunk.append([jnp.where(r0 == 0, halo[n], before[SUB:2 * SUB, :]), x[n][0:SUB, :],
                                      x[n][SUB:2 * SUB, :]])
                        x[n] = before
                    out = [[None, None], [None, None]]
                    for hf in (1, 0):
                        cur = [chunk[n][hf + 1] for n in range(2)]
                        s1 = [down(cur[n], chunk[n][hf], 1) for n in range(2)]
                        s2 = [down(cur[n], chunk[n][hf], 2) for n in range(2)]
                        a, b = [side[n][2] + side[n][1][2] * cur[n] + side[n][1][1] * s1[n] + side[n][1][0] * s2[n]
                                for n in range(2)]
                        sa = _sig(a)
                        dzh = dzv[hf * SUB:(hf + 1) * SUB, :]
                        d = [dzh * b * (sa * (1.0 + a * (1.0 - sa))), dzh * (a * sa)]
                        for n in range(2):
                            w = side[n][1]
                            acc[4 * n + 0] = acc[4 * n + 0] + d[n] * s2[n]
                            acc[4 * n + 1] = acc[4 * n + 1] + d[n] * s1[n]
                            acc[4 * n + 2] = acc[4 * n + 2] + d[n] * cur[n]
                            acc[4 * n + 3] = acc[4 * n + 3] + d[n]
                            out[n][hf] = w[2] * d[n] + w[1] * up(d[n], nxt[n], 1) + w[0] * up(d[n], nxt[n], 2)
                            nxt[n] = d[n]
                    for n in range(2):
                        du0_ref[pl.ds(r0, rows16), side[n][0]] = jnp.concatenate(out[n], axis=0).astype(BF16)
                return (*nxt, *acc, *x)

            init = ([jnp.where(i == 0, 0.0, carry[:, cs]) for cs, _, _ in side] + [jnp.zeros((SUB, LANE), F32)] * 8
                    + [u_ref[tm - rows16:tm, cs].astype(F32) for cs, _, _ in side])
            st = lax.fori_loop(0, n_it, rows, tuple(init))
            for n in range(2):
                carry[:, side[n][0]] = st[n]
                for k in range(4):
                    dw_ref[k, :, side[n][0]] += st[2 + 4 * n + k]

        @pl.when(i == nt - 1)
        def _():
            for k in range(4):
                dw_ref[k, 0:1, :] = _colsum(dw_ref[k])

    rev = lambda i: nt - 1 - i
    return pl.pallas_call(
        body, grid=(2, nt),
        in_specs=[pl.BlockSpec((tm, nb), lambda jc, i: (rev(i), jc))] + _ffn_specs(tm, nb, tm // HALO, rev),
        out_specs=[pl.BlockSpec((tm, 2 * nb), lambda jc, i: (rev(i), jc)),
                   pl.BlockSpec((4, SUB, 2 * nb), lambda jc, i: (0, 0, jc))],
        out_shape=[jax.ShapeDtypeStruct((T, W), BF16), jax.ShapeDtypeStruct((4, SUB, W), F32)],
        scratch_shapes=[pltpu.VMEM((SUB, 2 * nb), F32)], name=name,
        compiler_params=_cp("parallel", "arbitrary"))(dz, u0p, u0p, dw_w, dw_w, dw_b, dw_b)


CHALO = 32
CCOL = 256


def _phase_copies(buf, shifted, tm):
    n = tm + CHALO - SUB
    for p in range(1, SUB):
        shifted[p - 1, 0:n, :] = buf[p:p + n, :]


def _shifted(buf, shifted, r, tm, c0):
    m, p = divmod(r, SUB)
    src = buf if p == 0 else shifted.at[p - 1]
    return src[m * SUB:m * SUB + tm, c0:c0 + CCOL]


def conv_act(u, dw_w, dw_b, ln_g, ln_b, *, name, tm=128):
    T, D2 = u.shape
    D = D2 // 2
    tm = min(tm, T)
    hb = tm // CHALO

    def body(u_ref, halo_ref, w_ref, b_ref, g_ref, be_ref, s_ref, cv_ref, gbuf, gsh):
        i = pl.program_id(0)
        hv = halo_ref[...]
        gbuf[0:CHALO, :] = jnp.where(i == 0, 0.0, hv[:, 0:D] * _sig(hv[:, D:D2]))
        uv = u_ref[...]
        gbuf[CHALO:CHALO + tm, :] = uv[:, 0:D] * _sig(uv[:, D:D2])
        _phase_copies(gbuf, gsh, tm)
        for c0 in range(0, D, CCOL):
            acc = jnp.zeros((tm, CCOL), F32) + b_ref[:, c0:c0 + CCOL]
            for k in range(CONV_W):
                acc = acc + w_ref[k:k + 1, c0:c0 + CCOL] * _shifted(gbuf, gsh, CHALO - (CONV_W - 1) + k, tm, c0)
            cv_ref[:, c0:c0 + CCOL] = acc
        cv = cv_ref[...]
        mu = _rowmean(cv)
        xc = cv - mu
        nh = xc * lax.rsqrt(_rowmean(xc * xc) + EPS)
        s_ref[...] = _silu(nh * g_ref[...] + be_ref[...]).astype(BF16)

    return pl.pallas_call(
        body, grid=(T // tm,),
        in_specs=[_row(tm, D2), pl.BlockSpec((CHALO, D2), lambda i: (jnp.maximum(i * hb - 1, 0), 0)),
                  _full(CONV_W, D), _full(1, D), _full(1, D), _full(1, D)],
        out_specs=[_row(tm, D), _row(tm, D)],
        out_shape=[jax.ShapeDtypeStruct((T, D), BF16), jax.ShapeDtypeStruct((T, D), F32)],
        scratch_shapes=[pltpu.VMEM((tm + CHALO, D), F32), pltpu.VMEM((SUB - 1, tm + CHALO, D), F32)], name=name,
        compiler_params=_cp("arbitrary"))(u, u, dw_w, dw_b, ln_g, ln_b)


def conv_norm_bwd(ds, cv, ln_g, ln_b, *, name, tm=512):
    T, D = cv.shape
    tm = min(tm, T)

    def body(ds_ref, cv_ref, g_ref, be_ref, dcv_ref, dg_ref, dbe_ref, dcb_ref):
        _acc_init(pl.program_id(0), dg_ref, dbe_ref, dcb_ref)
        cv_ = cv_ref[...]
        mu = _rowmean(cv_)
        xc = cv_ - mu
        rstd = lax.rsqrt(_rowmean(xc * xc) + EPS)
        nh = xc * rstd
        gv = g_ref[...]
        dln = ds_ref[...] * _dsilu(nh * gv + be_ref[...])
        dg_ref[...] += _colsum(dln * nh)
        dbe_ref[...] += _colsum(dln)
        dnh = dln * gv
        dcv = rstd * (dnh - _rowmean(dnh) - nh * _rowmean(dnh * nh))
        dcb_ref[...] += _colsum(dcv)
        dcv_ref[...] = dcv

    return pl.pallas_call(
        body, grid=(T // tm,), in_specs=[_row(tm, D), _row(tm, D), _full(1, D), _full(1, D)],
        out_specs=[_row(tm, D), _full(1, D), _full(1, D), _full(1, D)],
        out_shape=[jax.ShapeDtypeStruct((T, D), F32)] + [jax.ShapeDtypeStruct((1, D), F32)] * 3, name=name,
        compiler_params=_cp("arbitrary"))(ds, cv, ln_g, ln_b)


def conv_glu_bwd(dcv, u, dw_w, *, name, tm=128):
    T, D2 = u.shape
    D = D2 // 2
    tm = min(tm, T)
    nt = T // tm
    hb = tm // CHALO

    def body(dcv_ref, dnext_ref, u_ref, halo_ref, w_ref, du_ref, dw_ref, dbin_ref, gbuf, dbuf, gsh, dsh):
        i = pl.program_id(0)
        _acc_init(i, dw_ref, dbin_ref)
        hv = halo_ref[...]
        gbuf[0:CHALO, :] = jnp.where(i == 0, 0.0, hv[:, 0:D] * _sig(hv[:, D:D2]))
        uv = u_ref[...]
        av = uv[:, 0:D]
        sg = _sig(uv[:, D:D2])
        gbuf[CHALO:CHALO + tm, :] = av * sg
        dbuf[0:tm, :] = dcv_ref[...]
        dbuf[tm:tm + CHALO, :] = jnp.where(i == nt - 1, 0.0, dnext_ref[...])
        _phase_copies(gbuf, gsh, tm)
        _phase_copies(dbuf, dsh, tm)
        for c0 in range(0, D, CCOL):
            dc = dbuf[0:tm, c0:c0 + CCOL]
            acc = jnp.zeros((tm, CCOL), F32)
            for k in range(CONV_W):
                dw_ref[k:k + 1, c0:c0 + CCOL] += _colsum(dc * _shifted(gbuf, gsh, CHALO - (CONV_W - 1) + k, tm, c0))
                acc = acc + w_ref[k:k + 1, c0:c0 + CCOL] * _shifted(dbuf, dsh, CONV_W - 1 - k, tm, c0)
            a_c = av[:, c0:c0 + CCOL]
            s_c = sg[:, c0:c0 + CCOL]
            da = acc * s_c
            dgt = acc * a_c * s_c * (1.0 - s_c)
            dbin_ref[:, c0:c0 + CCOL] += _colsum(da)
            dbin_ref[:, D + c0:D + c0 + CCOL] += _colsum(dgt)
            du_ref[:, c0:c0 + CCOL] = da.astype(BF16)
            du_ref[:, D + c0:D + c0 + CCOL] = dgt.astype(BF16)

    return pl.pallas_call(
        body, grid=(nt,),
        in_specs=[_row(tm, D), pl.BlockSpec((CHALO, D), lambda i: (jnp.minimum((i + 1) * hb, T // CHALO - 1), 0)),
                  _row(tm, D2), pl.BlockSpec((CHALO, D2), lambda i: (jnp.maximum(i * hb - 1, 0), 0)),
                  _full(CONV_W, D)],
        out_specs=[_row(tm, D2), _full(CHALO, D), _full(1, D2)],
        out_shape=[jax.ShapeDtypeStruct((T, D2), BF16), jax.ShapeDtypeStruct((CHALO, D), F32),
                   jax.ShapeDtypeStruct((1, D2), F32)],
        scratch_shapes=[pltpu.VMEM((tm + CHALO, D), F32)] * 2 + [pltpu.VMEM((SUB - 1, tm + CHALO, D), F32)] * 2,
        name=name, compiler_params=_cp("arbitrary"))(dcv, dcv, u, u, dw_w)


HB = 4


def _lb0(lg_ref):
    l0, l1, l2 = lg_ref[0:1, :], lg_ref[1:2, :], lg_ref[2:3, :]
    m = jnp.maximum(jnp.maximum(l0, l1), l2)
    e0 = jnp.exp(l0 - m)
    return e0 / (e0 + jnp.exp(l1 - m) + jnp.exp(l2 - m))


def _mm_exact(m01, x):
    hi = x.astype(BF16)
    r1 = x - hi.astype(F32)
    mid = r1.astype(BF16)
    lo = (r1 - mid.astype(F32)).astype(BF16)
    return _dot(m01, hi) + _dot(m01, mid) + _dot(m01, lo)


def _block_tri(tm):
    r = jnp.arange(tm)[:, None]
    c = jnp.arange(tm)[None, :]
    same = (r // BLK) == (c // BLK)
    return (same & (c <= r)).astype(BF16), (same & (c >= r)).astype(BF16)


def _const_spec(shape):
    return pl.BlockSpec(shape, lambda h, i: (0, 0))


def _hgrn_specs(H, hb, tm, idx):
    g = H // hb
    return [pl.BlockSpec((tm, hb * HEAD), lambda h, i: (idx(i), h)),
            pl.BlockSpec((tm, hb * HEAD), lambda h, i: (idx(i), g + h)),
            pl.BlockSpec((tm, hb * HEAD), lambda h, i: (idx(i), 2 * g + h)),
            pl.BlockSpec((3, hb * HEAD), lambda h, i: (0, h))]


def hgrn_scan(proj, lb_logits, *, name, tm=128):
    T = proj.shape[0]
    H = proj.shape[1] // (4 * HEAD)
    hb = min(HB, H)
    tm = min(tm, T)
    nt = T // tm
    nblk = tm // BLK
    tril, _ = _block_tri(tm)
    heads = [slice(hh * HEAD, (hh + 1) * HEAD) for hh in range(hb)]

    def body(qp_ref, fz_ref, v_ref, lg_ref, tril_ref, o_ref, st_ref, S_ref, q_s, k_s, b_s):
        @pl.when(pl.program_id(1) == 0)
        def _():
            S_ref[...] = jnp.zeros_like(S_ref)

        st_ref[...] = S_ref[...]
        lb = _lb0(lg_ref)
        f = lb + (1.0 - lb) * _sig(fz_ref[...])
        q_s[...] = _silu(qp_ref[...])
        k_s[...] = 1.0 - f
        b_s[...] = _mm_exact(tril_ref[...], jnp.log(f))
        rows = lax.broadcasted_iota(jnp.int32, (BLK, HEAD), 0)
        S = [S_ref[hh] for hh in range(hb)]
        for nb in range(nblk):
            blk = slice(nb * BLK, (nb + 1) * BLK)
            last = slice(nb * BLK + BLK - 1, nb * BLK + BLK)
            qb = [q_s[blk, c] for c in heads]
            bb = [b_s[blk, c] for c in heads]
            o = [_dot_nt((qb[hh] * jnp.exp(bb[hh])).astype(BF16), S[hh].astype(BF16)) for hh in range(hb)]
            for hh, c in enumerate(heads):
                bc = b_s[last, c]
                kd = k_s[blk, c] * jnp.exp(bc - bb[hh])
                S[hh] = S[hh] * jnp.exp(bc) + _dot_tn(v_ref[blk, c].astype(BF16), kd.astype(BF16))
            for s in range(BLK):
                r = slice(nb * BLK + s, nb * BLK + s + 1)
                for hh, c in enumerate(heads):
                    dec = jnp.exp(jnp.where(rows >= s, bb[hh] - b_s[r, c], NEG))
                    a = jnp.sum(qb[hh] * k_s[r, c] * dec, axis=-1, keepdims=True)
                    o[hh] = o[hh] + a * v_ref[r, c]
            for hh, c in enumerate(heads):
                o_ref[blk, c] = o[hh]
        for hh in range(hb):
            S_ref[hh] = S[hh]

    return pl.pallas_call(
        body, grid=(H // hb, nt),
        in_specs=_hgrn_specs(H, hb, tm, lambda i: i) + [_const_spec((tm, tm))],
        out_specs=[pl.BlockSpec((tm, hb * HEAD), lambda h, i: (i, h)),
                   pl.BlockSpec((None, hb, HEAD, HEAD), lambda h, i: (i, h, 0, 0))],
        out_shape=[jax.ShapeDtypeStruct((T, H * HEAD), F32), jax.ShapeDtypeStruct((nt, H, HEAD, HEAD), F32)],
        scratch_shapes=[pltpu.VMEM((hb, HEAD, HEAD), F32)] + [pltpu.VMEM((tm, hb * HEAD), F32)] * 3, name=name,
        compiler_params=_cp("parallel", "arbitrary"))(proj, proj, proj, lb_logits, tril)


def hgrn_scan_bwd(proj, lb_logits, states, do, *, name, tm=128):
    T = proj.shape[0]
    H = proj.shape[1] // (4 * HEAD)
    hb = min(HB, H)
    tm = min(tm, T)
    nt = T // tm
    nblk = tm // BLK
    tril, triu = _block_tri(tm)
    heads = [slice(hh * HEAD, (hh + 1) * HEAD) for hh in range(hb)]

    def body(qp_ref, fz_ref, v_ref, lg_ref, st_ref, do_ref, tril_ref, triu_ref, dqp_ref, dfz_ref, dv_ref, dlb_ref,
             dS_ref, Sb_ref, q_s, k_s, b_s, dq_s, dk_s, dv_s, db_s):
        i = pl.program_id(1)

        @pl.when(i == 0)
        def _():
            dS_ref[...] = jnp.zeros_like(dS_ref)
            dlb_ref[...] = jnp.zeros_like(dlb_ref)

        lb = _lb0(lg_ref)
        qp = qp_ref[...]
        sg = _sig(fz_ref[...])
        f = lb + (1.0 - lb) * sg
        q_s[...] = _silu(qp)
        k_s[...] = 1.0 - f
        b_s[...] = _mm_exact(tril_ref[...], jnp.log(f))
        rows = lax.broadcasted_iota(jnp.int32, (BLK, HEAD), 0)
        rows1 = lax.broadcasted_iota(jnp.int32, (BLK, 1), 0)

        S = [st_ref[hh] for hh in range(hb)]
        for nb in range(nblk):
            blk = slice(nb * BLK, (nb + 1) * BLK)
            last = slice(nb * BLK + BLK - 1, nb * BLK + BLK)
            for hh, c in enumerate(heads):
                Sb_ref[nb * hb + hh] = S[hh]
                if nb < nblk - 1:
                    bc = b_s[last, c]
                    kd = k_s[blk, c] * jnp.exp(bc - b_s[blk, c])
                    S[hh] = S[hh] * jnp.exp(bc) + _dot_tn(v_ref[blk, c].astype(BF16), kd.astype(BF16))

        dS = [dS_ref[hh] for hh in range(hb)]
        for nb in reversed(range(nblk)):
            blk = slice(nb * BLK, (nb + 1) * BLK)
            last = slice(nb * BLK + BLK - 1, nb * BLK + BLK)
            qb, kb, bb, dob, dq, dbc, ebc = [], [], [], [], [], [], []
            for hh, c in enumerate(heads):
                S0 = Sb_ref[nb * hb + hh]
                qb.append(q_s[blk, c])
                kb.append(k_s[blk, c])
                bb.append(b_s[blk, c])
                dob.append(do_ref[blk, c])
                bc = b_s[last, c]
                eb = jnp.exp(bb[hh])
                ekd = jnp.exp(bc - bb[hh])
                ebc.append(jnp.exp(bc))
                dS16 = dS[hh].astype(BF16)
                dob16 = dob[hh].astype(BF16)
                dq.append(_dot(dob16, S0.astype(BF16)) * eb)
                dki = _dot(v_ref[blk, c].astype(BF16), dS16) * ekd
                dk_s[blk, c] = dki
                dv_s[blk, c] = _dot_nt((kb[hh] * ekd).astype(BF16), dS16)
                dbc.append(_colsum(dS[hh] * S0) * ebc[hh] + _colsum(kb[hh] * dki))
                dS[hh] = dS[hh] * ebc[hh] + _dot_tn(dob16, (qb[hh] * eb).astype(BF16))
            for s in range(BLK):
                r = slice(nb * BLK + s, nb * BLK + s + 1)
                for hh, c in enumerate(heads):
                    ks = k_s[r, c]
                    dec = jnp.exp(jnp.where(rows >= s, bb[hh] - b_s[r, c], NEG))
                    w = qb[hh] * dec
                    a = jnp.sum(w * ks, axis=-1, keepdims=True)
                    da = jnp.where(rows1 >= s, jnp.sum(dob[hh] * v_ref[r, c], axis=-1, keepdims=True), 0.0)
                    dq[hh] = dq[hh] + (da * ks) * dec
                    dk_s[r, c] += _colsum(da * w)
                    dv_s[r, c] += _colsum(a * dob[hh])
            for hh, c in enumerate(heads):
                dq_s[blk, c] = dq[hh]
                db_s[blk, c] = qb[hh] * dq[hh] - kb[hh] * dk_s[blk, c]
                db_s[last, c] += dbc[hh]
        for hh in range(hb):
            dS_ref[hh] = dS[hh]

        dlf = _mm_exact(triu_ref[...], db_s[...])
        df = dlf / f - dk_s[...]
        dfz_ref[...] = (df * (1.0 - lb) * sg * (1.0 - sg)).astype(BF16)
        dlb_ref[...] += _colsum(df * (1.0 - sg))
        dqp_ref[...] = (dq_s[...] * _dsilu(qp)).astype(BF16)
        dv_ref[...] = dv_s[...].astype(BF16)

    rev = lambda i: nt - 1 - i
    out_blk = pl.BlockSpec((tm, hb * HEAD), lambda h, i: (rev(i), h))
    return pl.pallas_call(
        body, grid=(H // hb, nt),
        in_specs=_hgrn_specs(H, hb, tm, rev) + [pl.BlockSpec((None, hb, HEAD, HEAD), lambda h, i: (rev(i), h, 0, 0)),
                                                out_blk, _const_spec((tm, tm)), _const_spec((tm, tm))],
        out_specs=[out_blk, out_blk, out_blk, pl.BlockSpec((1, hb * HEAD), lambda h, i: (0, h))],
        out_shape=[jax.ShapeDtypeStruct((T, H * HEAD), BF16)] * 3 + [jax.ShapeDtypeStruct((1, H * HEAD), F32)],
        scratch_shapes=[pltpu.VMEM((hb, HEAD, HEAD), F32), pltpu.VMEM((nblk * hb, HEAD, HEAD), F32)]
        + [pltpu.VMEM((tm, hb * HEAD), F32)] * 7, name=name,
        compiler_params=_cp("parallel", "arbitrary"))(proj, proj, proj, lb_logits, states, do, tril, triu)


def hgrn_gate(o, proj, gn, *, name, tm=512):
    T, D = o.shape
    H = D // HEAD
    tm = min(tm, T)

    def body(o_ref, gp_ref, gn_ref, og_ref):
        gn_ = gn_ref[...]
        for h in range(H):
            c = slice(h * HEAD, (h + 1) * HEAD)
            oh = o_ref[:, c]
            r = lax.rsqrt(_rowmean(oh * oh) + EPS)
            og_ref[:, c] = ((oh * r) * gn_ * _silu(gp_ref[:, c])).astype(BF16)

    return pl.pallas_call(
        body, grid=(T // tm,),
        in_specs=[_row(tm, D), pl.BlockSpec((tm, D), lambda i: (i, 3)), _full(1, HEAD)],
        out_specs=_row(tm, D), out_shape=jax.ShapeDtypeStruct((T, D), BF16), name=name,
        compiler_params=_cp("parallel"))(o, proj, gn)


def hgrn_gate_bwd(dog, o, proj, gn, *, name, tm=512):
    T, D = o.shape
    H = D // HEAD
    tm = min(tm, T)

    def body(dog_ref, o_ref, gp_ref, gn_ref, do_ref, dgp_ref, dgn_ref):
        _acc_init(pl.program_id(0), dgn_ref)
        gn_ = gn_ref[...]
        for h in range(H):
            c = slice(h * HEAD, (h + 1) * HEAD)
            oh = o_ref[:, c]
            gp = gp_ref[:, c]
            dg = dog_ref[:, c]
            r = lax.rsqrt(_rowmean(oh * oh) + EPS)
            on = oh * r
            dgp_ref[:, c] = (dg * (on * gn_) * _dsilu(gp)).astype(BF16)
            don = dg * _silu(gp)
            dgn_ref[...] += _colsum(don * on)
            dn = don * gn_
            do_ref[:, c] = r * (dn - on * _rowmean(dn * on))

    return pl.pallas_call(
        body, grid=(T // tm,),
        in_specs=[_row(tm, D), _row(tm, D), pl.BlockSpec((tm, D), lambda i: (i, 3)), _full(1, HEAD)],
        out_specs=[_row(tm, D), _row(tm, D), _full(1, HEAD)],
        out_shape=[jax.ShapeDtypeStruct((T, D), F32), jax.ShapeDtypeStruct((T, D), BF16),
                   jax.ShapeDtypeStruct((1, HEAD), F32)], name=name,
        compiler_params=_cp("arbitrary"))(dog, o, proj, gn)


def _split2(x):
    hi = x.astype(BF16)
    return hi, (x - hi.astype(F32)).astype(BF16)


def ada_mod(c_all, ada_w, *, name):
    L, D, N = ada_w.shape
    B = c_all.shape[0]

    def body(c_ref, w_ref, o_ref):
        chi, clo = _split2(_silu(c_ref[...]))
        whi, wlo = _split2(w_ref[...])
        o_ref[...] = _dot(chi, whi) + _dot(chi, wlo) + _dot(clo, whi)

    return pl.pallas_call(
        body, grid=(L,), in_specs=[_full(B, D), pl.BlockSpec((None, D, N), lambda l: (l, 0, 0))],
        out_specs=pl.BlockSpec((None, B, N), lambda l: (l, 0, 0)),
        out_shape=jax.ShapeDtypeStruct((L, B, N), F32), name=name, compiler_params=_cp("parallel"))(c_all, ada_w)


def ada_wgrad(c_all_t, dmod, *, name, tr=256):
    D, B = c_all_t.shape
    L, _, N = dmod.shape
    tr = min(tr, D)

    def body(c_ref, d_ref, o_ref):
        cond = _silu(c_ref[...])
        acc = cond[:, 0:1] * d_ref[0:1, :]
        for b in range(1, B):
            acc = acc + cond[:, b:b + 1] * d_ref[b:b + 1, :]
        o_ref[...] = acc

    return pl.pallas_call(
        body, grid=(L, D // tr),
        in_specs=[pl.BlockSpec((tr, B), lambda l, r: (r, 0)), pl.BlockSpec((None, B, N), lambda l, r: (l, 0, 0))],
        out_specs=pl.BlockSpec((None, tr, N), lambda l, r: (l, r, 0)),
        out_shape=jax.ShapeDtypeStruct((L, D, N), F32), name=name,
        compiler_params=_cp("parallel", "parallel"))(c_all_t, dmod)


def sum_devices(parts, *, name):
    n, R, C = parts.shape

    def body(p_ref, o_ref):
        acc = p_ref[0]
        for d in range(1, n):
            acc = acc + p_ref[d]
        o_ref[...] = acc

    return pl.pallas_call(body, in_specs=[VMEM_SPEC], out_specs=VMEM_SPEC,
                          out_shape=jax.ShapeDtypeStruct((R, C), F32), name=name)(parts)


def lb_logits_grad(lb_logits, dlb, *, name):
    def body(lg_ref, d_ref, o_ref):
        l0, l1, l2 = lg_ref[0:1, :], lg_ref[1:2, :], lg_ref[2:3, :]
        m = jnp.maximum(jnp.maximum(l0, l1), l2)
        e0, e1, e2 = jnp.exp(l0 - m), jnp.exp(l1 - m), jnp.exp(l2 - m)
        z = e0 + e1 + e2
        p0, p1, p2 = e0 / z, e1 / z, e2 / z
        g = d_ref[...] * p0
        o_ref[0:1, :] = g * (1.0 - p0)
        o_ref[1:2, :] = -g * p1
        o_ref[2:3, :] = -g * p2

    return pl.pallas_call(body, in_specs=[VMEM_SPEC, VMEM_SPEC], out_specs=VMEM_SPEC,
                          out_shape=jax.ShapeDtypeStruct(lb_logits.shape, F32), name=name)(lb_logits, dlb)


def adamw(w, g, m, v, *, name, tr=256):
    R, C = w.shape
    tr = _tile(R, tr)

    def body(w_ref, g_ref, m_ref, v_ref, d_ref, nm_ref, nv_ref):
        gv = g_ref[...]
        nm = ADAM_B1 * m_ref[...] + (1.0 - ADAM_B1) * gv
        nv = ADAM_B2 * v_ref[...] + (1.0 - ADAM_B2) * (gv * gv)
        m_hat = nm / (1.0 - ADAM_B1 ** ADAM_STEP)
        v_hat = nv / (1.0 - ADAM_B2 ** ADAM_STEP)
        d_ref[...] = -ADAM_LR * (m_hat / (jnp.sqrt(v_hat) + ADAM_EPS) + ADAM_WD * w_ref[...])
        nm_ref[...] = nm
        nv_ref[...] = nv

    spec = pl.BlockSpec((tr, C), lambda i: (i, 0))
    return pl.pallas_call(
        body, grid=(R // tr,), in_specs=[spec] * 4, out_specs=[spec] * 3,
        out_shape=[jax.ShapeDtypeStruct((R, C), F32)] * 3, name=name, compiler_params=_cp("parallel"))(w, g, m, v)


def _place():
    return lax.axis_index("x"), lax.axis_index("y"), lax.axis_index("c")


def _flip(v, bit):
    return 1 - v if bit else v


def allgather_devices(v, *, name):
    R, C = v.shape

    def body(v_ref, out_ref, send_sems, recv_sems, local_sem):
        x, y, c = _place()
        me = 4 * x + 2 * y + c
        mine = pltpu.make_async_copy(v_ref, out_ref.at[me], local_sem)
        mine.start()
        sends = []
        for k in range(1, N_DEV):
            peer = (_flip(x, k & 4), _flip(y, k & 2), _flip(c, k & 1))
            cp = pltpu.make_async_remote_copy(src_ref=v_ref, dst_ref=out_ref.at[me], send_sem=send_sems.at[k - 1],
                                              recv_sem=recv_sems.at[k - 1], device_id=peer, device_id_type=MESH)
            cp.start()
            sends.append(cp)
        for k in range(1, N_DEV):
            px, py, pc = _flip(x, k & 4), _flip(y, k & 2), _flip(c, k & 1)
            pltpu.make_async_remote_copy(src_ref=v_ref, dst_ref=out_ref.at[4 * px + 2 * py + pc],
                                         send_sem=send_sems.at[k - 1], recv_sem=recv_sems.at[k - 1],
                                         device_id=(px, py, pc), device_id_type=MESH).wait_recv()
        for cp in sends:
            cp.wait_send()
        mine.wait()

    return pl.pallas_call(
        body, in_specs=[VMEM_SPEC], out_specs=VMEM_SPEC, out_shape=jax.ShapeDtypeStruct((N_DEV, R, C), v.dtype),
        scratch_shapes=[pltpu.SemaphoreType.DMA((N_DEV - 1,)), pltpu.SemaphoreType.DMA((N_DEV - 1,)),
                        pltpu.SemaphoreType.DMA], name=name)(v)


def _other_chips(x, y):
    return [(1 - x, y), (x, 1 - y), (1 - x, 1 - y)]


def allgather_chips(bufs, *, name):
    n = len(bufs)

    def body(*refs):
        outs = refs[n:2 * n]
        send_sems, recv_sems = refs[2 * n:]
        x, y, c = _place()
        q = 2 * x + y
        chips = _other_chips(x, y)

        def copy(a, k, block, half, to):
            slab = outs[a].at[block, half]
            return pltpu.make_async_remote_copy(src_ref=slab, dst_ref=slab, send_sem=send_sems.at[a, k],
                                                recv_sem=recv_sems.at[a, k], device_id=to, device_id_type=MESH)

        first = [copy(a, j, q, c, (*chips[j], c)) for a in range(n) for j in range(3)]
        for cp in first:
            cp.start()
        passed = []
        for a in range(n):
            for j, (px, py) in enumerate(chips):
                copy(a, j, 2 * px + py, c, (x, y, c)).wait_recv()
                fw = copy(a, 3 + j, 2 * px + py, c, (x, y, 1 - c))
                fw.start()
                passed.append(fw)
        for a in range(n):
            for j, (px, py) in enumerate(chips):
                copy(a, 3 + j, 2 * px + py, 1 - c, (x, y, c)).wait_recv()
        for cp in first + passed:
            cp.wait_send()

    return pl.pallas_call(
        body, in_specs=[HBM] * n, out_specs=[HBM] * n,
        out_shape=[jax.ShapeDtypeStruct(b.shape, b.dtype) for b in bufs],
        input_output_aliases={a: a for a in range(n)},
        scratch_shapes=[pltpu.SemaphoreType.DMA((n, 6)), pltpu.SemaphoreType.DMA((n, 6))], name=name)(*bufs)


SEM = pl.BlockSpec(memory_space=pltpu.SEMAPHORE)
DATAFLOW = pltpu.SideEffectType.DATAFLOW_SIDE_EFFECTING


def _chip_copy(buf, a, j, q, c, chips, send_sems, recv_sems):
    px, py = chips[j]
    return pltpu.make_async_remote_copy(src_ref=buf.at[q, c], dst_ref=buf.at[q, c], send_sem=send_sems.at[3 * a + j],
                                        recv_sem=recv_sems.at[3 * a + j], device_id=(px, py, c), device_id_type=MESH)


def allgather_chips_start(bufs, *, name):
    n = len(bufs)

    def body(*refs):
        send_sems, recv_sems = refs[n], refs[n + 1]
        outs = refs[n + 2:2 * n + 2]
        token = refs[2 * n + 2]
        x, y, c = _place()
        chips = _other_chips(x, y)
        for a in range(n):
            for j in range(3):
                _chip_copy(outs[a], a, j, 2 * x + y, c, chips, send_sems, recv_sems).start()
        token[...] = jnp.zeros_like(token)

    res = pl.pallas_call(
        body, name=name, in_specs=[HBM] * n,
        out_specs=(SEM, SEM, *([HBM] * n), VMEM_SPEC),
        out_shape=(pltpu.SemaphoreType.DMA((3 * n,)), pltpu.SemaphoreType.DMA((3 * n,)),
                   *[pltpu.HBM(b.shape, b.dtype) for b in bufs], jax.ShapeDtypeStruct((SUB, LANE), F32)),
        input_output_aliases={a: a + 2 for a in range(n)},
        compiler_params=pltpu.CompilerParams(has_side_effects=DATAFLOW),
    )(*[pltpu.with_memory_space_constraint(b, pltpu.HBM) for b in bufs])
    return res[0], res[1], list(res[2:2 + n]), res[2 + n]


def allgather_chips_wait(send_sems, recv_sems, bufs, after, *, name):
    n = len(bufs)

    def body(*refs):
        ins = refs[:n]
        send_sems, recv_sems = refs[n], refs[n + 1]
        x, y, c = _place()
        chips = _other_chips(x, y)
        for a in range(n):
            for j, (px, py) in enumerate(chips):
                _chip_copy(ins[a], a, j, 2 * x + y, c, chips, send_sems, recv_sems).wait_send()
                _chip_copy(ins[a], a, j, 2 * px + py, c, chips, send_sems, recv_sems).wait_recv()

    return list(pl.pallas_call(
        body, name=name, in_specs=[HBM] * n + [SEM, SEM, pl.BlockSpec(memory_space=pl.ANY)],
        out_specs=[HBM] * n, out_shape=[pltpu.HBM(b.shape, b.dtype) for b in bufs],
        input_output_aliases={a: a for a in range(n)},
        compiler_params=pltpu.CompilerParams(has_side_effects=DATAFLOW),
    )(*bufs, send_sems, recv_sems, after))


def forward_to_sibling(bufs, *, name):
    n = len(bufs)

    def body(*refs):
        outs = refs[n:2 * n]
        send_sems, recv_sems = refs[2 * n:]
        x, y, c = _place()
        chips = _other_chips(x, y)

        def copy(a, j, half, to):
            px, py = chips[j]
            slab = outs[a].at[2 * px + py, half]
            return pltpu.make_async_remote_copy(src_ref=slab, dst_ref=slab, send_sem=send_sems.at[a, j],
                                                recv_sem=recv_sems.at[a, j], device_id=to, device_id_type=MESH)

        sends = [copy(a, j, c, (x, y, 1 - c)) for a in range(n) for j in range(3)]
        for cp in sends:
            cp.start()
        for a in range(n):
            for j in range(3):
                copy(a, j, 1 - c, (x, y, c)).wait_recv()
        for cp in sends:
            cp.wait_send()

    return pl.pallas_call(
        body, in_specs=[HBM] * n, out_specs=[HBM] * n,
        out_shape=[jax.ShapeDtypeStruct(b.shape, b.dtype) for b in bufs],
        input_output_aliases={a: a for a in range(n)},
        scratch_shapes=[pltpu.SemaphoreType.DMA((n, 3)), pltpu.SemaphoreType.DMA((n, 3))], name=name)(*bufs)


def pair_exchange(grads, *, name):
    n = len(grads)

    def body(*refs):
        ins, outs = refs[:n], refs[n:2 * n]
        send_sems, recv_sems = refs[2 * n:]
        x, y, c = _place()
        cps = [pltpu.make_async_remote_copy(src_ref=ins[a].at[1 - c], dst_ref=outs[a], send_sem=send_sems.at[a],
                                            recv_sem=recv_sems.at[a], device_id=(x, y, 1 - c), device_id_type=MESH)
               for a in range(n)]
        for cp in cps:
            cp.start()
        for cp in cps:
            cp.wait_recv()
        for cp in cps:
            cp.wait_send()

    return pl.pallas_call(
        body, in_specs=[HBM] * n, out_specs=[HBM] * n,
        out_shape=[jax.ShapeDtypeStruct(g.shape[1:], g.dtype) for g in grads],
        scratch_shapes=[pltpu.SemaphoreType.DMA((n,)), pltpu.SemaphoreType.DMA((n,))], name=name)(*grads)


def pair_add(g, other, c_idx, *, name, tr=256):
    _, Q, R, C = g.shape
    tr = _tile(R, tr)

    def body(c_ref, g_ref, o_ref, out_ref):
        out_ref[...] = (g_ref[...] + o_ref[...]).astype(BF16)

    return pl.pallas_call(
        body,
        grid_spec=pltpu.PrefetchScalarGridSpec(
            num_scalar_prefetch=1, grid=(Q, R // tr),
            in_specs=[pl.BlockSpec((None, None, tr, C), lambda q, r, c_ref: (c_ref[0], q, r, 0)),
                      pl.BlockSpec((None, tr, C), lambda q, r, c_ref: (q, r, 0))],
            out_specs=pl.BlockSpec((None, tr, C), lambda q, r, c_ref: (q, r, 0))),
        out_shape=jax.ShapeDtypeStruct((Q, R, C), BF16), name=name,
        compiler_params=_cp("parallel", "parallel"))(c_idx, g, other)


def chip_exchange(sums, *, name):
    n = len(sums)

    def body(*refs):
        ins, outs = refs[:n], refs[n:2 * n]
        send_sems, recv_sems = refs[2 * n:]
        x, y, c = _place()
        chips = _other_chips(x, y)
        sends = [pltpu.make_async_remote_copy(src_ref=ins[a].at[2 * px + py], dst_ref=outs[a].at[j],
                                              send_sem=send_sems.at[a, j], recv_sem=recv_sems.at[a, j],
                                              device_id=(px, py, c), device_id_type=MESH)
                 for a in range(n) for j, (px, py) in enumerate(chips)]
        for cp in sends:
            cp.start()
        for cp in sends:
            cp.wait_recv()
        for cp in sends:
            cp.wait_send()

    return pl.pallas_call(
        body, in_specs=[HBM] * n, out_specs=[HBM] * n,
        out_shape=[jax.ShapeDtypeStruct((3,) + s.shape[1:], s.dtype) for s in sums],
        scratch_shapes=[pltpu.SemaphoreType.DMA((n, 3)), pltpu.SemaphoreType.DMA((n, 3))], name=name)(*sums)


def chip_sum(sums, landed, qc_idx, *, name, tr=256):
    _, R, C = sums.shape
    tr = _tile(R, tr)

    def body(qc_ref, own_ref, l_ref, o_ref):
        acc = own_ref[...].astype(F32)
        for k in range(3):
            acc = acc + l_ref[k].astype(F32)
        o_ref[...] = acc

    return pl.pallas_call(
        body,
        grid_spec=pltpu.PrefetchScalarGridSpec(
            num_scalar_prefetch=1, grid=(R // tr,),
            in_specs=[pl.BlockSpec((None, tr, C), lambda r, qc: (qc[0], r, 0)),
                      pl.BlockSpec((3, tr, C), lambda r, qc: (0, r, 0))],
            out_specs=pl.BlockSpec((None, tr, C), lambda r, qc: (qc[1], r, 0))),
        out_shape=jax.ShapeDtypeStruct((2, R, C), F32), name=name,
        compiler_params=_cp("parallel"))(qc_idx, sums, landed)


def half_swap(bufs, *, name):
    n = len(bufs)

    def body(*refs):
        outs = refs[n:2 * n]
        send_sems, recv_sems = refs[2 * n:]
        x, y, c = _place()
        cps = [pltpu.make_async_remote_copy(src_ref=outs[a].at[c], dst_ref=outs[a].at[c], send_sem=send_sems.at[a],
                                            recv_sem=recv_sems.at[a], device_id=(x, y, 1 - c), device_id_type=MESH)
               for a in range(n)]
        for cp in cps:
            cp.start()
        for a in range(n):
            pltpu.make_async_remote_copy(src_ref=outs[a].at[c], dst_ref=outs[a].at[1 - c], send_sem=send_sems.at[a],
                                         recv_sem=recv_sems.at[a], device_id=(x, y, 1 - c),
                                         device_id_type=MESH).wait_recv()
        for cp in cps:
            cp.wait_send()

    return pl.pallas_call(
        body, in_specs=[HBM] * n, out_specs=[HBM] * n,
        out_shape=[jax.ShapeDtypeStruct(b.shape, b.dtype) for b in bufs],
        input_output_aliases={a: a for a in range(n)},
        scratch_shapes=[pltpu.SemaphoreType.DMA((n,)), pltpu.SemaphoreType.DMA((n,))], name=name)(*bufs)


def _exchange_copy(sums, landed, a, j, c, chips, send_sems, recv_sems):
    px, py = chips[j]
    return pltpu.make_async_remote_copy(src_ref=sums.at[2 * px + py], dst_ref=landed.at[j],
                                        send_sem=send_sems.at[3 * a + j], recv_sem=recv_sems.at[3 * a + j],
                                        device_id=(px, py, c), device_id_type=MESH)


def chip_exchange_start(sums, *, name):
    n = len(sums)
    landing = [lax.empty((3,) + s.shape[1:], s.dtype) for s in sums]

    def body(*refs):
        send_sems, recv_sems = refs[2 * n], refs[2 * n + 1]
        src, dst = refs[2 * n + 2:3 * n + 2], refs[3 * n + 2:4 * n + 2]
        token = refs[4 * n + 2]
        x, y, c = _place()
        chips = _other_chips(x, y)
        for a in range(n):
            for j in range(3):
                _exchange_copy(src[a], dst[a], a, j, c, chips, send_sems, recv_sems).start()
        token[...] = jnp.zeros_like(token)

    res = pl.pallas_call(
        body, name=name, in_specs=[HBM] * (2 * n),
        out_specs=(SEM, SEM, *([HBM] * (2 * n)), VMEM_SPEC),
        out_shape=(pltpu.SemaphoreType.DMA((3 * n,)), pltpu.SemaphoreType.DMA((3 * n,)),
                   *[pltpu.HBM(b.shape, b.dtype) for b in sums + landing], jax.ShapeDtypeStruct((SUB, LANE), F32)),
        input_output_aliases={a: a + 2 for a in range(2 * n)},
        compiler_params=pltpu.CompilerParams(has_side_effects=DATAFLOW),
    )(*[pltpu.with_memory_space_constraint(b, pltpu.HBM) for b in sums + landing])
    return res[0], res[1], list(res[2:2 + n]), list(res[2 + n:2 + 2 * n]), res[2 + 2 * n]


def chip_exchange_wait(send_sems, recv_sems, sums, landed, after, *, name):
    n = len(sums)

    def body(*refs):
        src, dst = refs[:n], refs[n:2 * n]
        send_sems, recv_sems = refs[2 * n], refs[2 * n + 1]
        x, y, c = _place()
        chips = _other_chips(x, y)
        for a in range(n):
            for j in range(3):
                cp = _exchange_copy(src[a], dst[a], a, j, c, chips, send_sems, recv_sems)
                cp.wait_send()
                cp.wait_recv()

    res = pl.pallas_call(
        body, name=name, in_specs=[HBM] * (2 * n) + [SEM, SEM, pl.BlockSpec(memory_space=pl.ANY)],
        out_specs=[HBM] * (2 * n), out_shape=[pltpu.HBM(b.shape, b.dtype) for b in sums + landed],
        input_output_aliases={a: a for a in range(2 * n)},
        compiler_params=pltpu.CompilerParams(has_side_effects=DATAFLOW),
    )(*sums, *landed, send_sems, recv_sems, after)
    return list(res[:n]), list(res[n:])


def pair_reduce(grads, c, tag):
    c_idx = c.astype(jnp.int32).reshape(1)
    others = pair_exchange(grads, name=f"grad_pair_exchange_{tag}")
    return [pair_add(g, o, c_idx, name=f"grad_pair_add_{tag}{a}") for a, (g, o) in enumerate(zip(grads, others))]


def finish_reduce(sums, landed, q, c, tag):
    qc_idx = jnp.stack([q, c]).astype(jnp.int32)
    return [chip_sum(s, l, qc_idx, name=f"grad_chip_sum_{tag}{a}") for a, (s, l) in enumerate(zip(sums, landed))]


def _ffn_forward(x, mod, pre_g, post_g, w_up, w_down, dw_w, dw_b, tag):
    sh, sc, gate = mod
    h = prenorm(x, pre_g, sc, sh, name=f"{tag}_prenorm")
    u0 = mm_nn(h, w_up, name=f"{tag}_up", out_dtype=BF16, perm=_ffn_perm)
    z = ffn_act(u0, dw_w, dw_b, name=f"{tag}_act")
    y = mm_nn(z, w_down, name=f"{tag}_down")
    x_new = post_residual(x, y, post_g, gate, name=f"{tag}_post")
    return x_new, (x, h, u0, z, y)


def _ffn_backward(dx, saved, mod, pre_g, post_g, w_up, w_down, dw_w, dw_b, tag):
    x, h, u0, z, y = saved
    sh, sc, gate = mod
    dy, dgate, dpost, _ = post_bwd(dx, y, post_g, gate, name=f"{tag}_post_bwd")
    dz = mm_nt(dy, w_down, name=f"{tag}_down_dx", out_dtype=BF16)
    g_down = mm_tn(z, dy, name=f"{tag}_down_dw", J=2, block="a", chips_per_block=2)
    du0, dconv = ffn_act_bwd(dz, u0, dw_w, dw_b, name=f"{tag}_act_bwd")
    dh = mm_nt(du0, w_up, name=f"{tag}_up_dx", perm=_ffn_perm)
    g_up = mm_tn(h, du0, name=f"{tag}_up_dw", J=4, block="b", chips_per_block=1, perm=_ffn_perm)
    dx_in, dsh, dsc, dpre = prenorm_bwd(dh, x, dx, pre_g, sc, name=f"{tag}_prenorm_bwd")
    nb = u0.shape[1] // 4
    dconv = dconv[:, 0].reshape(4, 2, 2, nb).transpose(0, 2, 1, 3).reshape(4, 4 * nb)
    return dx_in, dict(dsh=dsh, dsc=dsc, dgate=dgate, dpre=dpre, dpost=dpost, g_up=g_up, g_down=g_down,
                       d_dw_w=dconv[0:FFN_W], d_dw_b=dconv[3:4])


def _local_step(x, tgt, mods, P, late_weights=None, grads_ready=None):
    m0, m1 = mods
    h1 = prenorm(x, P["pre_mix_g"][0:1], m0[1], m0[0], name="hgrn_prenorm")
    proj = mm_nn(h1, P["hgrn_w_in"], name="hgrn_in")
    o, states = hgrn_scan(proj, P["hgrn_lb_logits"], name="hgrn_scan")
    og = hgrn_gate(o, proj, P["hgrn_gnorm_g"], name="hgrn_gate")
    y1 = mm_nn(og, P["hgrn_w_out"], name="hgrn_out")
    x1 = post_residual(x, y1, P["post_mix_g"][0:1], m0[2], name="hgrn_post")
    if late_weights is not None:
        P = {**P, **late_weights(x1)}
    x2, ffn0 = _ffn_forward(x1, m0[3:6], P["pre_ffn_g"][0:1], P["post_ffn_g"][0:1], P["ffn_w_up"][0],
                            P["ffn_w_down"][0], P["ffn_dw_w"][0], P["ffn_dw_b"][0:1], "ffn0")
    h3 = prenorm(x2, P["pre_mix_g"][1:2], m1[1], m1[0], name="conv_prenorm")
    u = mm_nn(h3, P["conv_w_in"], name="conv_in", bias=P["conv_b_in"])
    s, cv = conv_act(u, P["conv_dw_w"], P["conv_dw_b"], P["conv_ln_g"], P["conv_ln_b"], name="conv_act")
    y3 = mm_nn(s, P["conv_w_out"], name="conv_out", bias=P["conv_b_out"])
    x3 = post_residual(x2, y3, P["post_mix_g"][1:2], m1[2], name="conv_post")
    x4, ffn1 = _ffn_forward(x3, m1[3:6], P["pre_ffn_g"][1:2], P["post_ffn_g"][1:2], P["ffn_w_up"][1],
                            P["ffn_w_down"][1], P["ffn_dw_w"][1], P["ffn_dw_b"][1:2], "ffn1")
    dx4, lcols = loss_grad(x4, tgt, name="loss")
    dx3, f1 = _ffn_backward(dx4, ffn1, m1[3:6], P["pre_ffn_g"][1:2], P["post_ffn_g"][1:2], P["ffn_w_up"][1],
                            P["ffn_w_down"][1], P["ffn_dw_w"][1], P["ffn_dw_b"][1:2], "ffn1")
    dy3, dg1_1, dpostmix1, d_b_out = post_bwd(dx3, y3, P["post_mix_g"][1:2], m1[2], name="conv_post_bwd")
    ds = mm_nt(dy3, P["conv_w_out"], name="conv_out_dx")
    g_conv_out = mm_tn(s, dy3, name="conv_out_dw", J=1, block="a", chips_per_block=4)
    dcv, d_ln_g, d_ln_b, d_dw_b = conv_norm_bwd(ds, cv, P["conv_ln_g"], P["conv_ln_b"], name="conv_norm_bwd")
    du, d_dw_w, d_b_in = conv_glu_bwd(dcv, u, P["conv_dw_w"], name="conv_glu_bwd")
    dh3 = mm_nt(du, P["conv_w_in"], name="conv_in_dx")
    g_conv_in = mm_tn(h3, du, name="conv_in_dw", J=4, block="b", chips_per_block=1)
    dx2, dsh1_1, dsc1_1, dpremix1 = prenorm_bwd(dh3, x2, dx3, P["pre_mix_g"][1:2], m1[1], name="conv_prenorm_bwd")
    if grads_ready is not None:
        token = grads_ready("l1", [g_conv_in, g_conv_out, f1["g_up"], f1["g_down"]])
        m0 = tuple(m + token[0:1, 0:1] for m in m0)
    dx1, f0 = _ffn_backward(dx2, ffn0, m0[3:6], P["pre_ffn_g"][0:1], P["post_ffn_g"][0:1], P["ffn_w_up"][0],
                            P["ffn_w_down"][0], P["ffn_dw_w"][0], P["ffn_dw_b"][0:1], "ffn0")
    if grads_ready is not None:
        token = grads_ready("f0", [f0["g_up"], f0["g_down"]])
        m0 = tuple(m + token[0:1, 0:1] for m in m0)
    dy1, dg1_0, dpostmix0, _ = post_bwd(dx1, y1, P["post_mix_g"][0:1], m0[2], name="hgrn_post_bwd")
    dog = mm_nt(dy1, P["hgrn_w_out"], name="hgrn_out_dx")
    g_hgrn_out = mm_tn(og, dy1, name="hgrn_out_dw", J=1, block="a", chips_per_block=4)
    do, dgp, d_gn = hgrn_gate_bwd(dog, o, proj, P["hgrn_gnorm_g"], name="hgrn_gate_bwd")
    dqp, dfz, dv, dlb = hgrn_scan_bwd(proj, P["hgrn_lb_logits"], states, do, name="hgrn_scan_bwd")
    dproj = jnp.concatenate([dqp, dfz, dv, dgp], axis=1)
    dh1 = mm_nt(dproj, P["hgrn_w_in"], name="hgrn_in_dx")
    g_hgrn_in = mm_tn(h1, dproj, name="hgrn_in_dw", J=4, block="b", chips_per_block=1)
    dx0, dsh1_0, dsc1_0, dpremix0 = prenorm_bwd(dh1, x, dx1, P["pre_mix_g"][0:1], m0[1], name="hgrn_prenorm_bwd")

    dmod = jnp.stack([
        jnp.concatenate([dsh1_0, dsc1_0, dg1_0, f0["dsh"], f0["dsc"], f0["dgate"]], axis=1)[0],
        jnp.concatenate([dsh1_1, dsc1_1, dg1_1, f1["dsh"], f1["dsc"], f1["dgate"]], axis=1)[0]])
    small = dict(
        loss=lcols,
        pre_mix_g=jnp.concatenate([dpremix0, dpremix1]), post_mix_g=jnp.concatenate([dpostmix0, dpostmix1]),
        pre_ffn_g=jnp.concatenate([f0["dpre"], f1["dpre"]]), post_ffn_g=jnp.concatenate([f0["dpost"], f1["dpost"]]),
        lb=dlb, hgrn_gnorm_g=d_gn, ffn_dw_b=jnp.concatenate([f0["d_dw_b"], f1["d_dw_b"]]), dmod=dmod,
        conv_b_in=d_b_in, conv_dw_w=d_dw_w[0:CONV_W], conv_dw_b=d_dw_b, conv_ln_g=d_ln_g, conv_ln_b=d_ln_b,
        conv_b_out=d_b_out, ffn_dw_w=jnp.stack([f0["d_dw_w"], f1["d_dw_w"]]))
    big = [g_hgrn_in, g_hgrn_out, g_conv_in, g_conv_out, f0["g_up"], f1["g_up"], f0["g_down"], f1["g_down"]]
    return dx0, small, big


def _pack(parts, rows=8):
    flat = jnp.concatenate([p.reshape(-1).astype(F32) for p in parts])
    per = rows * 128
    pad = (-flat.shape[0]) % per
    return jnp.pad(flat, (0, pad)).reshape(rows, -1)


def _unpack(flat, shapes):
    out, off = [], 0
    for s in shapes:
        n = 1
        for d in s:
            n *= d
        out.append(flat[..., off:off + n].reshape(flat.shape[:-1] + tuple(s)))
        off += n
    return out


def _from_chips(stacked, axis):
    moved = jnp.moveaxis(stacked, 0, axis)
    shape = list(moved.shape)
    return moved.reshape(shape[:axis] + [shape[axis] * shape[axis + 1]] + shape[axis + 2:])


def _my_shard(full, axis, q):
    n = full.shape[axis] // N_CHIPS
    return lax.dynamic_slice_in_dim(full, q * n, n, axis=axis)


def kernel(x, c, ada_w, ada_b, pre_mix_g, post_mix_g, pre_ffn_g, post_ffn_g, hgrn_w_in, hgrn_lb_logits, hgrn_gnorm_g, hgrn_w_out, conv_w_in, conv_b_in, conv_dw_w, conv_dw_b, conv_ln_g, conv_ln_b, conv_w_out, conv_b_out, ffn_w_up, ffn_dw_w, ffn_dw_b, ffn_w_down, loss_target, m_ada_w, m_ada_b, m_pre_mix_g, m_post_mix_g, m_pre_ffn_g, m_post_ffn_g, m_hgrn_w_in, m_hgrn_lb_logits, m_hgrn_gnorm_g, m_hgrn_w_out, m_conv_w_in, m_conv_b_in, m_conv_dw_w, m_conv_dw_b, m_conv_ln_g, m_conv_ln_b, m_conv_w_out, m_conv_b_out, m_ffn_w_up, m_ffn_dw_w, m_ffn_dw_b, m_ffn_w_down, v_ada_w, v_ada_b, v_pre_mix_g, v_post_mix_g, v_pre_ffn_g, v_post_ffn_g, v_hgrn_w_in, v_hgrn_lb_logits, v_hgrn_gnorm_g, v_hgrn_w_out, v_conv_w_in, v_conv_b_in, v_conv_dw_w, v_conv_dw_b, v_conv_ln_g, v_conv_ln_b, v_conv_w_out, v_conv_b_out, v_ffn_w_up, v_ffn_dw_w, v_ffn_dw_b, v_ffn_w_down):
    W = dict(ada_w=ada_w, ada_b=ada_b, pre_mix_g=pre_mix_g, post_mix_g=post_mix_g, pre_ffn_g=pre_ffn_g,
             post_ffn_g=post_ffn_g, hgrn_w_in=hgrn_w_in, hgrn_lb_logits=hgrn_lb_logits, hgrn_gnorm_g=hgrn_gnorm_g,
             hgrn_w_out=hgrn_w_out, conv_w_in=conv_w_in, conv_b_in=conv_b_in, conv_dw_w=conv_dw_w,
             conv_dw_b=conv_dw_b, conv_ln_g=conv_ln_g, conv_ln_b=conv_ln_b, conv_w_out=conv_w_out,
             conv_b_out=conv_b_out, ffn_w_up=ffn_w_up, ffn_dw_w=ffn_dw_w, ffn_dw_b=ffn_dw_b, ffn_w_down=ffn_w_down)
    M = dict(ada_w=m_ada_w, ada_b=m_ada_b, pre_mix_g=m_pre_mix_g, post_mix_g=m_post_mix_g, pre_ffn_g=m_pre_ffn_g,
             post_ffn_g=m_post_ffn_g, hgrn_w_in=m_hgrn_w_in, hgrn_lb_logits=m_hgrn_lb_logits,
             hgrn_gnorm_g=m_hgrn_gnorm_g, hgrn_w_out=m_hgrn_w_out, conv_w_in=m_conv_w_in, conv_b_in=m_conv_b_in,
             conv_dw_w=m_conv_dw_w, conv_dw_b=m_conv_dw_b, conv_ln_g=m_conv_ln_g, conv_ln_b=m_conv_ln_b,
             conv_w_out=m_conv_w_out, conv_b_out=m_conv_b_out, ffn_w_up=m_ffn_w_up, ffn_dw_w=m_ffn_dw_w,
             ffn_dw_b=m_ffn_dw_b, ffn_w_down=m_ffn_w_down)
    V = dict(ada_w=v_ada_w, ada_b=v_ada_b, pre_mix_g=v_pre_mix_g, post_mix_g=v_post_mix_g, pre_ffn_g=v_pre_ffn_g,
             post_ffn_g=v_post_ffn_g, hgrn_w_in=v_hgrn_w_in, hgrn_lb_logits=v_hgrn_lb_logits,
             hgrn_gnorm_g=v_hgrn_gnorm_g, hgrn_w_out=v_hgrn_w_out, conv_w_in=v_conv_w_in, conv_b_in=v_conv_b_in,
             conv_dw_w=v_conv_dw_w, conv_dw_b=v_conv_dw_b, conv_ln_g=v_conv_ln_g, conv_ln_b=v_conv_ln_b,
             conv_w_out=v_conv_w_out, conv_b_out=v_conv_b_out, ffn_w_up=v_ffn_w_up, ffn_dw_w=v_ffn_dw_w,
             ffn_dw_b=v_ffn_dw_b, ffn_w_down=v_ffn_w_down)
    names = list(W)
    xi, yi, ci = lax.axis_index("x"), lax.axis_index("y"), lax.axis_index("c")
    q = 2 * xi + yi
    me = 2 * q + ci
    D = x.shape[-1]
    L = ada_w.shape[0]

    small_w = ["conv_b_in", "conv_dw_w", "conv_dw_b", "conv_ln_g", "conv_ln_b", "conv_b_out", "ffn_dw_w"]
    small_axis = dict(conv_b_in=1, conv_dw_w=2, conv_dw_b=1, conv_ln_g=1, conv_ln_b=1, conv_b_out=1, ffn_dw_w=2)
    packed = _pack([c] + [W[n] for n in small_w])
    gathered = allgather_devices(packed, name="gather_small_params").reshape(N_DEV, -1)
    c_all = gathered[:, 0:D]
    per_chip = gathered.reshape(N_CHIPS, 2, -1)[:, 0, D:]
    parts = _unpack(per_chip, [W[n].shape for n in small_w])
    P = {n: _from_chips(p, small_axis[n]) for n, p in zip(small_w, parts)}
    P["conv_dw_w"] = P["conv_dw_w"][0]
    for n in ("pre_mix_g", "post_mix_g", "pre_ffn_g", "post_ffn_g", "hgrn_lb_logits", "hgrn_gnorm_g", "ffn_dw_b"):
        P[n] = W[n]

    modp = ada_mod(c_all, ada_w, name="ada_mod")
    ncol = modp.shape[-1]
    mod_all = allgather_devices(modp.reshape(L * N_DEV, ncol), name="gather_mod")
    mod_all = mod_all.reshape(N_CHIPS, 2, L, N_DEV, ncol)[:, 0]
    mod_me = lax.dynamic_index_in_dim(mod_all, me, axis=2, keepdims=False)
    mod = mod_me.transpose(1, 0, 2).reshape(L, N_CHIPS * ncol) + ada_b
    mods = [tuple(mod[l:l + 1, k * D:(k + 1) * D] for k in range(6)) for l in range(L)]

    def halves(w):
        shard = w.astype(BF16).reshape(1, 2, w.shape[0] // 2, w.shape[1])
        buf = lax.empty((N_CHIPS,) + shard.shape[1:], BF16)
        return lax.dynamic_update_slice_in_dim(buf, shard, q, axis=0)

    stack = lambda t: t.reshape(N_CHIPS, t.shape[1] * t.shape[2], t.shape[3])
    rowsh = lambda t: t.reshape(1, N_CHIPS * t.shape[1] * t.shape[2], t.shape[3])
    g = allgather_chips([halves(hgrn_w_in[0]), halves(hgrn_w_out[0])], name="gather_hgrn_weights")
    P["hgrn_w_in"], P["hgrn_w_out"] = stack(g[0]), rowsh(g[1])
    late_shards = [conv_w_in[0], conv_w_out[0], ffn_w_up[0], ffn_w_up[1], ffn_w_down[0], ffn_w_down[1]]
    late_bufs, _, _ = lax.optimization_barrier(([halves(w) for w in late_shards], g, mod))
    send_sems, recv_sems, bufs, token = allgather_chips_start(late_bufs, name="gather_weights_start")
    mods[0] = tuple(m + token[0:1, 0:1] for m in mods[0])

    def late_weights(x1):
        landed = allgather_chips_wait(send_sems, recv_sems, bufs, x1, name="gather_weights_wait")
        g = forward_to_sibling(landed, name="gather_weights_forward")
        return dict(conv_w_in=stack(g[0]), conv_w_out=rowsh(g[1]), ffn_w_up=[stack(g[2]), stack(g[3])],
                    ffn_w_down=[rowsh(g[4]), rowsh(g[5])])

    in_flight = {}

    def grads_ready(tag, grads):
        sums = pair_reduce(grads, ci, f"{tag}_")
        send, recv, sums, landing, tok = chip_exchange_start(sums, name=f"grad_chip_exchange_start_{tag}")
        in_flight[tag] = (send, recv, sums, landing)
        return tok

    grad_x, small, big = _local_step(x[0], loss_target[0], mods, P, late_weights, grads_ready)

    small_names = list(small)
    gs = allgather_devices(_pack([small[n] for n in small_names]), name="gather_small_grads")
    dmod_all = _unpack(gs.reshape(N_DEV, -1), [small[n].shape for n in small_names])[small_names.index("dmod")]
    tot = sum_devices(gs, name="sum_small_grads").reshape(1, -1)
    S = dict(zip(small_names, _unpack(tot, [small[n].shape for n in small_names])))
    S = {n: v[0] for n, v in S.items()}
    loss = 0.5 * jnp.sum(S["loss"]) / D

    G = {}
    dmod_q = lax.dynamic_slice_in_dim(dmod_all, q * ncol, ncol, axis=2)
    G["ada_w"] = ada_wgrad(c_all.T, dmod_q.transpose(1, 0, 2), name="ada_wgrad")
    G["ada_b"] = S["dmod"]
    for n in ("pre_mix_g", "post_mix_g", "pre_ffn_g", "post_ffn_g", "hgrn_gnorm_g", "ffn_dw_b"):
        G[n] = S[n]
    G["hgrn_lb_logits"] = lb_logits_grad(hgrn_lb_logits, S["lb"], name="lb_logits_grad")
    G["conv_b_in"] = _my_shard(S["conv_b_in"], 1, q)
    G["conv_dw_w"] = _my_shard(S["conv_dw_w"], 1, q)[None]
    for n in ("conv_dw_b", "conv_ln_g", "conv_ln_b", "conv_b_out"):
        G[n] = _my_shard(S[n], 1, q)
    G["ffn_dw_w"] = _my_shard(S["ffn_dw_w"], 2, q)

    sums_h = pair_reduce([big[0], big[1]], ci, "hg_")
    landed_h = chip_exchange(sums_h, name="grad_chip_exchange")
    halves = finish_reduce(sums_h, landed_h, q, ci, "hg_")
    for tag in ("f0", "l1"):
        sums_t, landed_t = chip_exchange_wait(*in_flight[tag], grad_x, name=f"grad_chip_exchange_wait_{tag}")
        halves += finish_reduce(sums_t, landed_t, q, ci, f"{tag}_")
    red = [f.reshape(2 * f.shape[1], f.shape[2]) for f in half_swap(halves, name="grad_half_swap")]
    G["hgrn_w_in"], G["hgrn_w_out"], G["conv_w_in"], G["conv_w_out"] = red[0][None], red[1][None], red[4][None], red[5][None]
    G["ffn_w_up"] = jnp.stack([red[2], red[6]])
    G["ffn_w_down"] = jnp.stack([red[3], red[7]])

    delta, new_m, new_v = {}, {}, {}
    big_names = ["ada_w", "hgrn_w_in", "hgrn_w_out", "conv_w_in", "conv_w_out", "ffn_w_up", "ffn_w_down"]
    for n in big_names:
        shp = W[n].shape
        two = lambda t: t.reshape(-1, shp[-1])
        d_, m_, v_ = adamw(two(W[n]), two(G[n]), two(M[n]), two(V[n]), name=f"adamw_{n}")
        delta[n], new_m[n], new_v[n] = d_.reshape(shp), m_.reshape(shp), v_.reshape(shp)
    rest = [n for n in names if n not in big_names]
    d_, m_, v_ = adamw(_pack([W[n] for n in rest]), _pack([G[n] for n in rest]), _pack([M[n] for n in rest]),
                       _pack([V[n] for n in rest]), name="adamw_small")
    shapes = [W[n].shape for n in rest]
    for n, a, b_, c_ in zip(rest, _unpack(d_.reshape(-1), shapes), _unpack(m_.reshape(-1), shapes),
                            _unpack(v_.reshape(-1), shapes)):
        delta[n], new_m[n], new_v[n] = a, b_, c_

    return (loss, grad_x[None], *[G[n].reshape(W[n].shape) for n in names], *[delta[n] for n in names],
            *[new_m[n] for n in names], *[new_v[n] for n in names])
```

```python
import jax
import jax.numpy as jnp
from jax import lax
from jax.experimental import pallas as pl
from jax.experimental.pallas import tpu as pltpu

F32 = jnp.float32
BF16 = jnp.bfloat16
EPS = 1e-6
HEAD = 128
BLK = 16
NEG = -1e30
CONV_W = 31
FFN_W = 3
N_CHIPS = 4
N_DEV = 8
SUB = 8
LANE = 128
V7X_VMEM_LIMIT = 56 * 1024 * 1024
MESH = pl.DeviceIdType.MESH
HBM = pl.BlockSpec(memory_space=pltpu.HBM)
VMEM_SPEC = pl.BlockSpec(memory_space=pltpu.VMEM)

ADAM_LR = 0.001
ADAM_B1 = 0.9
ADAM_B2 = 0.999
ADAM_EPS = 1e-08
ADAM_WD = 0.01
ADAM_STEP = 10


def _cp(*sem):
    return pltpu.CompilerParams(dimension_semantics=sem, vmem_limit_bytes=V7X_VMEM_LIMIT)


def _sig(x):
    return 0.5 * jnp.tanh(0.5 * x) + 0.5


def _silu(x):
    return x * _sig(x)


def _dsilu(x):
    s = _sig(x)
    return s * (1.0 + x * (1.0 - s))


def _dot(a, b):
    return jnp.dot(a, b, preferred_element_type=F32)


def _dot_nt(a, b):
    return lax.dot_general(a, b, (((1,), (1,)), ((), ())), preferred_element_type=F32)


def _dot_tn(a, b):
    return lax.dot_general(a, b, (((0,), (0,)), ((), ())), preferred_element_type=F32)


def _colsum(x):
    return jnp.sum(x, axis=0, keepdims=True)


def _rowmean(x):
    return jnp.mean(x, axis=-1, keepdims=True)


def _ffn_perm(j):
    return (j % 2) * 2 + j // 2


def _tile(n, pref):
    if n <= pref:
        return n
    t = pref - pref % 8
    while n % t:
        t -= 8
    return t


def mm_nn(a, w, *, name, bias=None, out_dtype=F32, perm=None, tm=1024):
    T, K = a.shape
    J, _, nb = w.shape
    tm = min(tm, T)
    col = (lambda j: j) if perm is None else perm

    def body(a_ref, w_ref, *rest):
        acc = _dot(a_ref[...], w_ref[...])
        if bias is not None:
            acc = acc + rest[0][...]
        rest[-1][...] = acc.astype(out_dtype)

    in_specs = [pl.BlockSpec((tm, K), lambda j, i: (i, 0)), pl.BlockSpec((None, K, nb), lambda j, i: (j, 0, 0))]
    args = [a, w]
    if bias is not None:
        in_specs.append(pl.BlockSpec((1, nb), lambda j, i: (0, j)))
        args.append(bias)
    return pl.pallas_call(
        body, grid=(J, T // tm), in_specs=in_specs,
        out_specs=pl.BlockSpec((tm, nb), lambda j, i: (i, col(j))),
        out_shape=jax.ShapeDtypeStruct((T, J * nb), out_dtype), name=name,
        compiler_params=_cp("parallel", "parallel"))(*args)


def mm_nt(a, w, *, name, out_dtype=F32, perm=None, tm=1024):
    T = a.shape[0]
    J, K, nb = w.shape
    tm = min(tm, T)
    col = (lambda j: j) if perm is None else perm

    def body(a_ref, w_ref, o_ref, acc_ref):
        j = pl.program_id(1)

        @pl.when(j == 0)
        def _():
            acc_ref[...] = jnp.zeros_like(acc_ref)

        acc_ref[...] += _dot_nt(a_ref[...], w_ref[...])

        @pl.when(j == J - 1)
        def _():
            o_ref[...] = acc_ref[...].astype(out_dtype)

    return pl.pallas_call(
        body, grid=(T // tm, J),
        in_specs=[pl.BlockSpec((tm, nb), lambda i, j: (i, col(j))), pl.BlockSpec((None, K, nb), lambda i, j: (j, 0, 0))],
        out_specs=pl.BlockSpec((tm, K), lambda i, j: (i, 0)),
        out_shape=jax.ShapeDtypeStruct((T, K), out_dtype),
        scratch_shapes=[pltpu.VMEM((tm, K), F32)], name=name,
        compiler_params=_cp("parallel", "arbitrary"))(a, w)


def mm_tn(a, b, *, name, J, block, row_chips=1, col_chips=1, perm=None, tk=1024):
    T = a.shape[0]
    tk = min(tk, T)
    col = (lambda j: j) if perm is None else perm
    if block == "b":
        rows, nb = a.shape[1], b.shape[1] // J
        a_spec = pl.BlockSpec((tk, rows), lambda j, t: (t, 0))
        b_spec = pl.BlockSpec((tk, nb), lambda j, t: (t, col(j)))
    else:
        rows, nb = a.shape[1] // J, b.shape[1]
        a_spec = pl.BlockSpec((tk, rows), lambda j, t: (t, col(j)))
        b_spec = pl.BlockSpec((tk, nb), lambda j, t: (t, 0))
    rh = rows // (2 * row_chips)
    nc = nb // col_chips
    chips = [(rc, cc) for rc in range(row_chips) for cc in range(col_chips)]

    def body(a_ref, b_ref, o_ref):
        @pl.when(pl.program_id(1) == 0)
        def _():
            o_ref[...] = jnp.zeros_like(o_ref)

        acc = _dot_tn(a_ref[...], b_ref[...])
        for ch, (rc, cc) in enumerate(chips):
            for hf in range(2):
                r0 = (rc * 2 + hf) * rh
                o_ref[hf, ch] += acc[r0:r0 + rh, cc * nc:(cc + 1) * nc]

    return pl.pallas_call(
        body, grid=(J, T // tk), in_specs=[a_spec, b_spec],
        out_specs=pl.BlockSpec((2, len(chips), rh, nc), lambda j, t: (0, j, 0, 0)),
        out_shape=jax.ShapeDtypeStruct((2, J * len(chips), rh, nc), F32), name=name,
        compiler_params=_cp("parallel", "arbitrary"))(a, b)


def _row(tm, w):
    return pl.BlockSpec((tm, w), lambda i: (i, 0))


def _full(r, w):
    return pl.BlockSpec((r, w), lambda i: (0, 0))


def _acc_init(i, *refs):
    @pl.when(i == 0)
    def _():
        for r in refs:
            r[...] = jnp.zeros_like(r)


def prenorm(x, g, sc, sh, *, name, tm=512):
    T, D = x.shape
    tm = min(tm, T)

    def body(x_ref, g_ref, sc_ref, sh_ref, h_ref):
        xv = x_ref[...]
        r = lax.rsqrt(_rowmean(xv * xv) + EPS)
        h_ref[...] = ((xv * r) * g_ref[...] * (1.0 + sc_ref[...]) + sh_ref[...]).astype(BF16)

    return pl.pallas_call(
        body, grid=(T // tm,), in_specs=[_row(tm, D), _full(1, D), _full(1, D), _full(1, D)],
        out_specs=_row(tm, D), out_shape=jax.ShapeDtypeStruct((T, D), BF16), name=name,
        compiler_params=_cp("parallel"))(x, g, sc, sh)


def post_residual(x, y, g, gate, *, name, tm=512):
    T, D = x.shape
    tm = min(tm, T)

    def body(x_ref, y_ref, g_ref, gate_ref, o_ref):
        yv = y_ref[...]
        r = lax.rsqrt(_rowmean(yv * yv) + EPS)
        o_ref[...] = x_ref[...] + gate_ref[...] * ((yv * r) * g_ref[...])

    return pl.pallas_call(
        body, grid=(T // tm,), in_specs=[_row(tm, D), _row(tm, D), _full(1, D), _full(1, D)],
        out_specs=_row(tm, D), out_shape=jax.ShapeDtypeStruct((T, D), F32), name=name,
        compiler_params=_cp("parallel"))(x, y, g, gate)


def loss_grad(x, tgt, *, name, tm=512):
    T, D = x.shape
    tm = min(tm, T)

    def body(x_ref, t_ref, dx_ref, l_ref):
        _acc_init(pl.program_id(0), l_ref)
        e = x_ref[...] - t_ref[...]
        dx_ref[...] = e * (1.0 / D)
        l_ref[...] += _colsum(e * e)

    return pl.pallas_call(
        body, grid=(T // tm,), in_specs=[_row(tm, D), _row(tm, D)],
        out_specs=[_row(tm, D), _full(1, D)],
        out_shape=[jax.ShapeDtypeStruct((T, D), F32), jax.ShapeDtypeStruct((1, D), F32)], name=name,
        compiler_params=_cp("arbitrary"))(x, tgt)


def post_bwd(dx, y, g, gate, *, name, tm=512):
    T, D = dx.shape
    tm = min(tm, T)

    def body(dx_ref, y_ref, g_ref, gate_ref, dy_ref, dgate_ref, dg_ref, dbias_ref):
        _acc_init(pl.program_id(0), dgate_ref, dg_ref, dbias_ref)
        yv = y_ref[...]
        dxv = dx_ref[...]
        r = lax.rsqrt(_rowmean(yv * yv) + EPS)
        yn = yv * r
        gv = g_ref[...]
        gt = gate_ref[...]
        dgate_ref[...] += _colsum(dxv * (yn * gv))
        dg_ref[...] += _colsum(dxv * gt * yn)
        dyn = dxv * gt * gv
        dy = r * (dyn - yn * _rowmean(dyn * yn))
        dbias_ref[...] += _colsum(dy)
        dy_ref[...] = dy.astype(BF16)

    return pl.pallas_call(
        body, grid=(T // tm,), in_specs=[_row(tm, D), _row(tm, D), _full(1, D), _full(1, D)],
        out_specs=[_row(tm, D), _full(1, D), _full(1, D), _full(1, D)],
        out_shape=[jax.ShapeDtypeStruct((T, D), BF16)] + [jax.ShapeDtypeStruct((1, D), F32)] * 3, name=name,
        compiler_params=_cp("arbitrary"))(dx, y, g, gate)


def prenorm_bwd(dh, x, dres, g, sc, *, name, tm=512):
    T, D = x.shape
    tm = min(tm, T)

    def body(dh_ref, x_ref, dres_ref, g_ref, sc_ref, dx_ref, dsh_ref, dsc_ref, dg_ref):
        _acc_init(pl.program_id(0), dsh_ref, dsc_ref, dg_ref)
        xv = x_ref[...]
        dhv = dh_ref[...]
        r = lax.rsqrt(_rowmean(xv * xv) + EPS)
        xn = xv * r
        gv = g_ref[...]
        one_sc = 1.0 + sc_ref[...]
        dsh_ref[...] += _colsum(dhv)
        dsc_ref[...] += _colsum(dhv * (xn * gv))
        dg_ref[...] += _colsum(dhv * one_sc * xn)
        dxn = dhv * one_sc * gv
        dx_ref[...] = dres_ref[...] + r * (dxn - xn * _rowmean(dxn * xn))

    return pl.pallas_call(
        body, grid=(T // tm,), in_specs=[_row(tm, D), _row(tm, D), _row(tm, D), _full(1, D), _full(1, D)],
        out_specs=[_row(tm, D), _full(1, D), _full(1, D), _full(1, D)],
        out_shape=[jax.ShapeDtypeStruct((T, D), F32)] + [jax.ShapeDtypeStruct((1, D), F32)] * 3, name=name,
        compiler_params=_cp("arbitrary"))(dh, x, dres, g, sc)


HALO = 16


def _shift_helpers():
    rid = lax.broadcasted_iota(jnp.int32, (SUB, LANE), 0)

    def down(cur, prev, k):
        return pltpu.roll(jnp.where(rid >= SUB - k, prev, cur), k, 0)

    def up(cur, nxt, k):
        return pltpu.roll(jnp.where(rid < k, nxt, cur), SUB - k, 0)

    return down, up


def _ffn_sides(c, nb, wa_ref, wb_ref, ba_ref, bb_ref):
    cols = slice(c * LANE, (c + 1) * LANE)
    return [(cols, [wa_ref[k:k + 1, cols] for k in range(FFN_W)], ba_ref[:, cols]),
            (slice(nb + c * LANE, nb + (c + 1) * LANE), [wb_ref[k:k + 1, cols] for k in range(FFN_W)],
             bb_ref[:, cols])]


def _ffn_specs(tm, nb, hb, idx):
    return [pl.BlockSpec((tm, 2 * nb), lambda jc, i: (idx(i), jc)),
            pl.BlockSpec((HALO, 2 * nb), lambda jc, i: (jnp.maximum(idx(i) * hb - 1, 0), jc)),
            pl.BlockSpec((FFN_W, nb), lambda jc, i: (0, jc)),
            pl.BlockSpec((FFN_W, nb), lambda jc, i: (0, jc + 2)),
            pl.BlockSpec((1, nb), lambda jc, i: (0, jc)),
            pl.BlockSpec((1, nb), lambda jc, i: (0, jc + 2))]


def ffn_act(u0p, dw_w, dw_b, *, name, tm=256):
    T, W = u0p.shape
    nb = W // 4
    tm = min(tm, T)
    unroll = 4
    rows16 = 2 * SUB

    def body(u_ref, halo_ref, wa_ref, wb_ref, ba_ref, bb_ref, z_ref):
        i = pl.program_id(1)
        down, _ = _shift_helpers()
        for c in range(nb // LANE):
            cols = slice(c * LANE, (c + 1) * LANE)
            side = _ffn_sides(c, nb, wa_ref, wb_ref, ba_ref, bb_ref)

            def rows(j, prev):
                prev = list(prev)
                for m in range(unroll):
                    r0 = pl.multiple_of((j * unroll + m) * rows16, rows16)
                    x = [u_ref[pl.ds(r0, rows16), cs].astype(F32) for cs, _, _ in side]
                    zs = []
                    for hf in range(2):
                        conv = []
                        for n, (_, w, b) in enumerate(side):
                            cur = x[n][hf * SUB:(hf + 1) * SUB, :]
                            conv.append(b + w[2] * cur + w[1] * down(cur, prev[n], 1) + w[0] * down(cur, prev[n], 2))
                            prev[n] = cur
                        zs.append(_silu(conv[0]) * conv[1])
                    z_ref[pl.ds(r0, rows16), cols] = jnp.concatenate(zs, axis=0).astype(BF16)
                return tuple(prev)

            first = [jnp.where(i == 0, 0.0, halo_ref[:, cs].astype(F32)[SUB:2 * SUB, :]) for cs, _, _ in side]
            lax.fori_loop(0, tm // (rows16 * unroll), rows, tuple(first))

    return pl.pallas_call(
        body, grid=(2, T // tm), in_specs=_ffn_specs(tm, nb, tm // HALO, lambda i: i),
        out_specs=pl.BlockSpec((tm, nb), lambda jc, i: (i, jc)),
        out_shape=jax.ShapeDtypeStruct((T, 2 * nb), BF16), name=name,
        compiler_params=_cp("parallel", "arbitrary"))(u0p, u0p, dw_w, dw_w, dw_b, dw_b)


def ffn_act_bwd(dz, u0p, dw_w, dw_b, *, name, tm=256):
    T, W = u0p.shape
    nb = W // 4
    tm = min(tm, T)
    nt = T // tm
    unroll = 2
    rows16 = 2 * SUB
    n_it = tm // (rows16 * unroll)

    def body(dz_ref, u_ref, halo_ref, wa_ref, wb_ref, ba_ref, bb_ref, du0_ref, dw_ref, carry):
        i = pl.program_id(1)
        _acc_init(i, dw_ref)
        down, up = _shift_helpers()
        for c in range(nb // LANE):
            cols = slice(c * LANE, (c + 1) * LANE)
            side = _ffn_sides(c, nb, wa_ref, wb_ref, ba_ref, bb_ref)
            halo = [jnp.where(i == nt - 1, 0.0, halo_ref[:, cs].astype(F32)[SUB:2 * SUB, :]) for cs, _, _ in side]

            def rows(j, st):
                nxt, acc, x = list(st[0:2]), list(st[2:10]), list(st[10:12])
                for m in range(unroll):
                    r0 = pl.multiple_of(((n_it - 1 - j) * unroll + unroll - 1 - m) * rows16, rows16)
                    rp = pl.multiple_of(jnp.maximum(r0 - rows16, 0), rows16)
                    dzv = dz_ref[pl.ds(r0, rows16), cols].astype(F32)
                    chunk = []
                    for n, (cs, _, _) in enumerate(side):
                        before = u_ref[pl.ds(rp, rows16), cs].astype(F32)
                        chunk.append([jnp.where(r0 == 0, halo[n], before[SUB:2 * SUB, :]), x[n][0:SUB, :],
                                      x[n][SUB:2 * SUB, :]])
                        x[n] = before
                    out = [[None, None], [None, None]]
                    for hf in (1, 0):
                        cur = [chunk[n][hf + 1] for n in range(2)]
                        s1 = [down(cur[n], chunk[n][hf], 1) for n in range(2)]
                        s2 = [down(cur[n], chunk[n][hf], 2) for n in range(2)]
                        a, b = [side[n][2] + side[n][1][2] * cur[n] + side[n][1][1] * s1[n] + side[n][1][0] * s2[n]
                                for n in range(2)]
                        sa = _sig(a)
                        dzh = dzv[hf * SUB:(hf + 1) * SUB, :]
                        d = [dzh * b * (sa * (1.0 + a * (1.0 - sa))), dzh * (a * sa)]
                        for n in range(2):
                            w = side[n][1]
                            acc[4 * n + 0] = acc[4 * n + 0] + d[n] * s2[n]
                            acc[4 * n + 1] = acc[4 * n + 1] + d[n] * s1[n]
                            acc[4 * n + 2] = acc[4 * n + 2] + d[n] * cur[n]
                            acc[4 * n + 3] = acc[4 * n + 3] + d[n]
                            out[n][hf] = w[2] * d[n] + w[1] * up(d[n], nxt[n], 1) + w[0] * up(d[n], nxt[n], 2)
                            nxt[n] = d[n]
                    for n in range(2):
                        du0_ref[pl.ds(r0, rows16), side[n][0]] = jnp.concatenate(out[n], axis=0).astype(BF16)
                return (*nxt, *acc, *x)

            init = ([jnp.where(i == 0, 0.0, carry[:, cs]) for cs, _, _ in side] + [jnp.zeros((SUB, LANE), F32)] * 8
                    + [u_ref[tm - rows16:tm, cs].astype(F32) for cs, _, _ in side])
            st = lax.fori_loop(0, n_it, rows, tuple(init))
            for n in range(2):
                carry[:, side[n][0]] = st[n]
                for k in range(4):
                    dw_ref[k, :, side[n][0]] += st[2 + 4 * n + k]

        @pl.when(i == nt - 1)
        def _():
            for k in range(4):
                dw_ref[k, 0:1, :] = _colsum(dw_ref[k])

    rev = lambda i: nt - 1 - i
    return pl.pallas_call(
        body, grid=(2, nt),
        in_specs=[pl.BlockSpec((tm, nb), lambda jc, i: (rev(i), jc))] + _ffn_specs(tm, nb, tm // HALO, rev),
        out_specs=[pl.BlockSpec((tm, 2 * nb), lambda jc, i: (rev(i), jc)),
                   pl.BlockSpec((4, SUB, 2 * nb), lambda jc, i: (0, 0, jc))],
        out_shape=[jax.ShapeDtypeStruct((T, W), BF16), jax.ShapeDtypeStruct((4, SUB, W), F32)],
        scratch_shapes=[pltpu.VMEM((SUB, 2 * nb), F32)], name=name,
        compiler_params=_cp("parallel", "arbitrary"))(dz, u0p, u0p, dw_w, dw_w, dw_b, dw_b)


CHALO = 32
CCOL = 256


def _phase_copies(buf, shifted, tm):
    n = tm + CHALO - SUB
    for p in range(1, SUB):
        shifted[p - 1, 0:n, :] = buf[p:p + n, :]


def _shifted(buf, shifted, r, tm, c0):
    m, p = divmod(r, SUB)
    src = buf if p == 0 else shifted.at[p - 1]
    return src[m * SUB:m * SUB + tm, c0:c0 + CCOL]


def conv_act(u, dw_w, dw_b, ln_g, ln_b, *, name, tm=128):
    T, D2 = u.shape
    D = D2 // 2
    tm = min(tm, T)
    hb = tm // CHALO

    def body(u_ref, halo_ref, w_ref, b_ref, g_ref, be_ref, s_ref, cv_ref, gbuf, gsh):
        i = pl.program_id(0)
        hv = halo_ref[...]
        gbuf[0:CHALO, :] = jnp.where(i == 0, 0.0, hv[:, 0:D] * _sig(hv[:, D:D2]))
        uv = u_ref[...]
        gbuf[CHALO:CHALO + tm, :] = uv[:, 0:D] * _sig(uv[:, D:D2])
        _phase_copies(gbuf, gsh, tm)
        for c0 in range(0, D, CCOL):
            acc = jnp.zeros((tm, CCOL), F32) + b_ref[:, c0:c0 + CCOL]
            for k in range(CONV_W):
                acc = acc + w_ref[k:k + 1, c0:c0 + CCOL] * _shifted(gbuf, gsh, CHALO - (CONV_W - 1) + k, tm, c0)
            cv_ref[:, c0:c0 + CCOL] = acc
        cv = cv_ref[...]
        mu = _rowmean(cv)
        xc = cv - mu
        nh = xc * lax.rsqrt(_rowmean(xc * xc) + EPS)
        s_ref[...] = _silu(nh * g_ref[...] + be_ref[...]).astype(BF16)

    return pl.pallas_call(
        body, grid=(T // tm,),
        in_specs=[_row(tm, D2), pl.BlockSpec((CHALO, D2), lambda i: (jnp.maximum(i * hb - 1, 0), 0)),
                  _full(CONV_W, D), _full(1, D), _full(1, D), _full(1, D)],
        out_specs=[_row(tm, D), _row(tm, D)],
        out_shape=[jax.ShapeDtypeStruct((T, D), BF16), jax.ShapeDtypeStruct((T, D), F32)],
        scratch_shapes=[pltpu.VMEM((tm + CHALO, D), F32), pltpu.VMEM((SUB - 1, tm + CHALO, D), F32)], name=name,
        compiler_params=_cp("arbitrary"))(u, u, dw_w, dw_b, ln_g, ln_b)


def conv_norm_bwd(ds, cv, ln_g, ln_b, *, name, tm=512):
    T, D = cv.shape
    tm = min(tm, T)

    def body(ds_ref, cv_ref, g_ref, be_ref, dcv_ref, dg_ref, dbe_ref, dcb_ref):
        _acc_init(pl.program_id(0), dg_ref, dbe_ref, dcb_ref)
        cv_ = cv_ref[...]
        mu = _rowmean(cv_)
        xc = cv_ - mu
        rstd = lax.rsqrt(_rowmean(xc * xc) + EPS)
        nh = xc * rstd
        gv = g_ref[...]
        dln = ds_ref[...] * _dsilu(nh * gv + be_ref[...])
        dg_ref[...] += _colsum(dln * nh)
        dbe_ref[...] += _colsum(dln)
        dnh = dln * gv
        dcv = rstd * (dnh - _rowmean(dnh) - nh * _rowmean(dnh * nh))
        dcb_ref[...] += _colsum(dcv)
        dcv_ref[...] = dcv

    return pl.pallas_call(
        body, grid=(T // tm,), in_specs=[_row(tm, D), _row(tm, D), _full(1, D), _full(1, D)],
        out_specs=[_row(tm, D), _full(1, D), _full(1, D), _full(1, D)],
        out_shape=[jax.ShapeDtypeStruct((T, D), F32)] + [jax.ShapeDtypeStruct((1, D), F32)] * 3, name=name,
        compiler_params=_cp("arbitrary"))(ds, cv, ln_g, ln_b)


def conv_glu_bwd(dcv, u, dw_w, *, name, tm=128):
    T, D2 = u.shape
    D = D2 // 2
    tm = min(tm, T)
    nt = T // tm
    hb = tm // CHALO

    def body(dcv_ref, dnext_ref, u_ref, halo_ref, w_ref, du_ref, dw_ref, dbin_ref, gbuf, dbuf, gsh, dsh):
        i = pl.program_id(0)
        _acc_init(i, dw_ref, dbin_ref)
        hv = halo_ref[...]
        gbuf[0:CHALO, :] = jnp.where(i == 0, 0.0, hv[:, 0:D] * _sig(hv[:, D:D2]))
        uv = u_ref[...]
        av = uv[:, 0:D]
        sg = _sig(uv[:, D:D2])
        gbuf[CHALO:CHALO + tm, :] = av * sg
        dbuf[0:tm, :] = dcv_ref[...]
        dbuf[tm:tm + CHALO, :] = jnp.where(i == nt - 1, 0.0, dnext_ref[...])
        _phase_copies(gbuf, gsh, tm)
        _phase_copies(dbuf, dsh, tm)
        for c0 in range(0, D, CCOL):
            dc = dbuf[0:tm, c0:c0 + CCOL]
            acc = jnp.zeros((tm, CCOL), F32)
            for k in range(CONV_W):
                dw_ref[k:k + 1, c0:c0 + CCOL] += _colsum(dc * _shifted(gbuf, gsh, CHALO - (CONV_W - 1) + k, tm, c0))
                acc = acc + w_ref[k:k + 1, c0:c0 + CCOL] * _shifted(dbuf, dsh, CONV_W - 1 - k, tm, c0)
            a_c = av[:, c0:c0 + CCOL]
            s_c = sg[:, c0:c0 + CCOL]
            da = acc * s_c
            dgt = acc * a_c * s_c * (1.0 - s_c)
            dbin_ref[:, c0:c0 + CCOL] += _colsum(da)
            dbin_ref[:, D + c0:D + c0 + CCOL] += _colsum(dgt)
            du_ref[:, c0:c0 + CCOL] = da.astype(BF16)
            du_ref[:, D + c0:D + c0 + CCOL] = dgt.astype(BF16)

    return pl.pallas_call(
        body, grid=(nt,),
        in_specs=[_row(tm, D), pl.BlockSpec((CHALO, D), lambda i: (jnp.minimum((i + 1) * hb, T // CHALO - 1), 0)),
                  _row(tm, D2), pl.BlockSpec((CHALO, D2), lambda i: (jnp.maximum(i * hb - 1, 0), 0)),
                  _full(CONV_W, D)],
        out_specs=[_row(tm, D2), _full(CHALO, D), _full(1, D2)],
        out_shape=[jax.ShapeDtypeStruct((T, D2), BF16), jax.ShapeDtypeStruct((CHALO, D), F32),
                   jax.ShapeDtypeStruct((1, D2), F32)],
        scratch_shapes=[pltpu.VMEM((tm + CHALO, D), F32)] * 2 + [pltpu.VMEM((SUB - 1, tm + CHALO, D), F32)] * 2,
        name=name, compiler_params=_cp("arbitrary"))(dcv, dcv, u, u, dw_w)


HB = 4


def _lb0(lg_ref):
    l0, l1, l2 = lg_ref[0:1, :], lg_ref[1:2, :], lg_ref[2:3, :]
    m = jnp.maximum(jnp.maximum(l0, l1), l2)
    e0 = jnp.exp(l0 - m)
    return e0 / (e0 + jnp.exp(l1 - m) + jnp.exp(l2 - m))


def _mm_exact(m01, x):
    hi = x.astype(BF16)
    r1 = x - hi.astype(F32)
    mid = r1.astype(BF16)
    lo = (r1 - mid.astype(F32)).astype(BF16)
    return _dot(m01, hi) + _dot(m01, mid) + _dot(m01, lo)


def _block_tri(tm):
    r = jnp.arange(tm)[:, None]
    c = jnp.arange(tm)[None, :]
    same = (r // BLK) == (c // BLK)
    return (same & (c <= r)).astype(BF16), (same & (c >= r)).astype(BF16)


def _const_spec(shape):
    return pl.BlockSpec(shape, lambda h, i: (0, 0))


def _hgrn_specs(H, hb, tm, idx):
    g = H // hb
    return [pl.BlockSpec((tm, hb * HEAD), lambda h, i: (idx(i), h)),
            pl.BlockSpec((tm, hb * HEAD), lambda h, i: (idx(i), g + h)),
            pl.BlockSpec((tm, hb * HEAD), lambda h, i: (idx(i), 2 * g + h)),
            pl.BlockSpec((3, hb * HEAD), lambda h, i: (0, h))]


def hgrn_scan(proj, lb_logits, *, name, tm=128):
    T = proj.shape[0]
    H = proj.shape[1] // (4 * HEAD)
    hb = min(HB, H)
    tm = min(tm, T)
    nt = T // tm
    nblk = tm // BLK
    tril, _ = _block_tri(tm)
    heads = [slice(hh * HEAD, (hh + 1) * HEAD) for hh in range(hb)]

    def body(qp_ref, fz_ref, v_ref, lg_ref, tril_ref, o_ref, st_ref, S_ref, q_s, k_s, b_s):
        @pl.when(pl.program_id(1) == 0)
        def _():
            S_ref[...] = jnp.zeros_like(S_ref)

        st_ref[...] = S_ref[...]
        lb = _lb0(lg_ref)
        f = lb + (1.0 - lb) * _sig(fz_ref[...])
        q_s[...] = _silu(qp_ref[...])
        k_s[...] = 1.0 - f
        b_s[...] = _mm_exact(tril_ref[...], jnp.log(f))
        rows = lax.broadcasted_iota(jnp.int32, (BLK, HEAD), 0)
        S = [S_ref[hh] for hh in range(hb)]
        for nb in range(nblk):
            blk = slice(nb * BLK, (nb + 1) * BLK)
            last = slice(nb * BLK + BLK - 1, nb * BLK + BLK)
            qb = [q_s[blk, c] for c in heads]
            bb = [b_s[blk, c] for c in heads]
            o = [_dot_nt((qb[hh] * jnp.exp(bb[hh])).astype(BF16), S[hh].astype(BF16)) for hh in range(hb)]
            for hh, c in enumerate(heads):
                bc = b_s[last, c]
                kd = k_s[blk, c] * jnp.exp(bc - bb[hh])
                S[hh] = S[hh] * jnp.exp(bc) + _dot_tn(v_ref[blk, c].astype(BF16), kd.astype(BF16))
            for s in range(BLK):
                r = slice(nb * BLK + s, nb * BLK + s + 1)
                for hh, c in enumerate(heads):
                    dec = jnp.exp(jnp.where(rows >= s, bb[hh] - b_s[r, c], NEG))
                    a = jnp.sum(qb[hh] * k_s[r, c] * dec, axis=-1, keepdims=True)
                    o[hh] = o[hh] + a * v_ref[r, c]
            for hh, c in enumerate(heads):
                o_ref[blk, c] = o[hh]
        for hh in range(hb):
            S_ref[hh] = S[hh]

    return pl.pallas_call(
        body, grid=(H // hb, nt),
        in_specs=_hgrn_specs(H, hb, tm, lambda i: i) + [_const_spec((tm, tm))],
        out_specs=[pl.BlockSpec((tm, hb * HEAD), lambda h, i: (i, h)),
                   pl.BlockSpec((None, hb, HEAD, HEAD), lambda h, i: (i, h, 0, 0))],
        out_shape=[jax.ShapeDtypeStruct((T, H * HEAD), F32), jax.ShapeDtypeStruct((nt, H, HEAD, HEAD), F32)],
        scratch_shapes=[pltpu.VMEM((hb, HEAD, HEAD), F32)] + [pltpu.VMEM((tm, hb * HEAD), F32)] * 3, name=name,
        compiler_params=_cp("parallel", "arbitrary"))(proj, proj, proj, lb_logits, tril)


def hgrn_scan_bwd(proj, lb_logits, states, do, *, name, tm=128):
    T = proj.shape[0]
    H = proj.shape[1] // (4 * HEAD)
    hb = min(HB, H)
    tm = min(tm, T)
    nt = T // tm
    nblk = tm // BLK
    tril, triu = _block_tri(tm)
    heads = [slice(hh * HEAD, (hh + 1) * HEAD) for hh in range(hb)]

    def body(qp_ref, fz_ref, v_ref, lg_ref, st_ref, do_ref, tril_ref, triu_ref, dqp_ref, dfz_ref, dv_ref, dlb_ref,
             dS_ref, Sb_ref, q_s, k_s, b_s, dq_s, dk_s, dv_s, db_s):
        i = pl.program_id(1)

        @pl.when(i == 0)
        def _():
            dS_ref[...] = jnp.zeros_like(dS_ref)
            dlb_ref[...] = jnp.zeros_like(dlb_ref)

        lb = _lb0(lg_ref)
        qp = qp_ref[...]
        sg = _sig(fz_ref[...])
        f = lb + (1.0 - lb) * sg
        q_s[...] = _silu(qp)
        k_s[...] = 1.0 - f
        b_s[...] = _mm_exact(tril_ref[...], jnp.log(f))
        rows = lax.broadcasted_iota(jnp.int32, (BLK, HEAD), 0)
        rows1 = lax.broadcasted_iota(jnp.int32, (BLK, 1), 0)

        S = [st_ref[hh] for hh in range(hb)]
        for nb in range(nblk):
            blk = slice(nb * BLK, (nb + 1) * BLK)
            last = slice(nb * BLK + BLK - 1, nb * BLK + BLK)
            for hh, c in enumerate(heads):
                Sb_ref[nb * hb + hh] = S[hh]
                if nb < nblk - 1:
                    bc = b_s[last, c]
                    kd = k_s[blk, c] * jnp.exp(bc - b_s[blk, c])
                    S[hh] = S[hh] * jnp.exp(bc) + _dot_tn(v_ref[blk, c].astype(BF16), kd.astype(BF16))

        dS = [dS_ref[hh] for hh in range(hb)]
        for nb in reversed(range(nblk)):
            blk = slice(nb * BLK, (nb + 1) * BLK)
            last = slice(nb * BLK + BLK - 1, nb * BLK + BLK)
            qb, kb, bb, dob, dq, dbc, ebc = [], [], [], [], [], [], []
            for hh, c in enumerate(heads):
                S0 = Sb_ref[nb * hb + hh]
                qb.append(q_s[blk, c])
                kb.append(k_s[blk, c])
                bb.append(b_s[blk, c])
                dob.append(do_ref[blk, c])
                bc = b_s[last, c]
                eb = jnp.exp(bb[hh])
                ekd = jnp.exp(bc - bb[hh])
                ebc.append(jnp.exp(bc))
                dS16 = dS[hh].astype(BF16)
                dob16 = dob[hh].astype(BF16)
                dq.append(_dot(dob16, S0.astype(BF16)) * eb)
                dki = _dot(v_ref[blk, c].astype(BF16), dS16) * ekd
                dk_s[blk, c] = dki
                dv_s[blk, c] = _dot_nt((kb[hh] * ekd).astype(BF16), dS16)
                dbc.append(_colsum(dS[hh] * S0) * ebc[hh] + _colsum(kb[hh] * dki))
                dS[hh] = dS[hh] * ebc[hh] + _dot_tn(dob16, (qb[hh] * eb).astype(BF16))
            for s in range(BLK):
                r = slice(nb * BLK + s, nb * BLK + s + 1)
                for hh, c in enumerate(heads):
                    ks = k_s[r, c]
                    dec = jnp.exp(jnp.where(rows >= s, bb[hh] - b_s[r, c], NEG))
                    w = qb[hh] * dec
                    a = jnp.sum(w * ks, axis=-1, keepdims=True)
                    da = jnp.where(rows1 >= s, jnp.sum(dob[hh] * v_ref[r, c], axis=-1, keepdims=True), 0.0)
                    dq[hh] = dq[hh] + (da * ks) * dec
                    dk_s[r, c] += _colsum(da * w)
                    dv_s[r, c] += _colsum(a * dob[hh])
            for hh, c in enumerate(heads):
                dq_s[blk, c] = dq[hh]
                db_s[blk, c] = qb[hh] * dq[hh] - kb[hh] * dk_s[blk, c]
                db_s[last, c] += dbc[hh]
        for hh in range(hb):
            dS_ref[hh] = dS[hh]

        dlf = _mm_exact(triu_ref[...], db_s[...])
        df = dlf / f - dk_s[...]
        dfz_ref[...] = (df * (1.0 - lb) * sg * (1.0 - sg)).astype(BF16)
        dlb_ref[...] += _colsum(df * (1.0 - sg))
        dqp_ref[...] = (dq_s[...] * _dsilu(qp)).astype(BF16)
        dv_ref[...] = dv_s[...].astype(BF16)

    rev = lambda i: nt - 1 - i
    out_blk = pl.BlockSpec((tm, hb * HEAD), lambda h, i: (rev(i), h))
    return pl.pallas_call(
        body, grid=(H // hb, nt),
        in_specs=_hgrn_specs(H, hb, tm, rev) + [pl.BlockSpec((None, hb, HEAD, HEAD), lambda h, i: (rev(i), h, 0, 0)),
                                                out_blk, _const_spec((tm, tm)), _const_spec((tm, tm))],
        out_specs=[out_blk, out_blk, out_blk, pl.BlockSpec((1, hb * HEAD), lambda h, i: (0, h))],
        out_shape=[jax.ShapeDtypeStruct((T, H * HEAD), BF16)] * 3 + [jax.ShapeDtypeStruct((1, H * HEAD), F32)],
        scratch_shapes=[pltpu.VMEM((hb, HEAD, HEAD), F32), pltpu.VMEM((nblk * hb, HEAD, HEAD), F32)]
        + [pltpu.VMEM((tm, hb * HEAD), F32)] * 7, name=name,
        compiler_params=_cp("parallel", "arbitrary"))(proj, proj, proj, lb_logits, states, do, tril, triu)


def hgrn_gate(o, proj, gn, *, name, tm=512):
    T, D = o.shape
    H = D // HEAD
    tm = min(tm, T)

    def body(o_ref, gp_ref, gn_ref, og_ref):
        gn_ = gn_ref[...]
        for h in range(H):
            c = slice(h * HEAD, (h + 1) * HEAD)
            oh = o_ref[:, c]
            r = lax.rsqrt(_rowmean(oh * oh) + EPS)
            og_ref[:, c] = ((oh * r) * gn_ * _silu(gp_ref[:, c])).astype(BF16)

    return pl.pallas_call(
        body, grid=(T // tm,),
        in_specs=[_row(tm, D), pl.BlockSpec((tm, D), lambda i: (i, 3)), _full(1, HEAD)],
        out_specs=_row(tm, D), out_shape=jax.ShapeDtypeStruct((T, D), BF16), name=name,
        compiler_params=_cp("parallel"))(o, proj, gn)


def hgrn_gate_bwd(dog, o, proj, gn, *, name, tm=512):
    T, D = o.shape
    H = D // HEAD
    tm = min(tm, T)

    def body(dog_ref, o_ref, gp_ref, gn_ref, do_ref, dgp_ref, dgn_ref):
        _acc_init(pl.program_id(0), dgn_ref)
        gn_ = gn_ref[...]
        for h in range(H):
            c = slice(h * HEAD, (h + 1) * HEAD)
            oh = o_ref[:, c]
            gp = gp_ref[:, c]
            dg = dog_ref[:, c]
            r = lax.rsqrt(_rowmean(oh * oh) + EPS)
            on = oh * r
            dgp_ref[:, c] = (dg * (on * gn_) * _dsilu(gp)).astype(BF16)
            don = dg * _silu(gp)
            dgn_ref[...] += _colsum(don * on)
            dn = don * gn_
            do_ref[:, c] = r * (dn - on * _rowmean(dn * on))

    return pl.pallas_call(
        body, grid=(T // tm,),
        in_specs=[_row(tm, D), _row(tm, D), pl.BlockSpec((tm, D), lambda i: (i, 3)), _full(1, HEAD)],
        out_specs=[_row(tm, D), _row(tm, D), _full(1, HEAD)],
        out_shape=[jax.ShapeDtypeStruct((T, D), F32), jax.ShapeDtypeStruct((T, D), BF16),
                   jax.ShapeDtypeStruct((1, HEAD), F32)], name=name,
        compiler_params=_cp("arbitrary"))(dog, o, proj, gn)


def _split2(x):
    hi = x.astype(BF16)
    return hi, (x - hi.astype(F32)).astype(BF16)


def ada_mod(c_all, ada_w, *, name):
    L, D, N = ada_w.shape
    B = c_all.shape[0]

    def body(c_ref, w_ref, o_ref):
        chi, clo = _split2(_silu(c_ref[...]))
        whi, wlo = _split2(w_ref[...])
        o_ref[...] = _dot(chi, whi) + _dot(chi, wlo) + _dot(clo, whi)

    return pl.pallas_call(
        body, grid=(L,), in_specs=[_full(B, D), pl.BlockSpec((None, D, N), lambda l: (l, 0, 0))],
        out_specs=pl.BlockSpec((None, B, N), lambda l: (l, 0, 0)),
        out_shape=jax.ShapeDtypeStruct((L, B, N), F32), name=name, compiler_params=_cp("parallel"))(c_all, ada_w)


def ada_wgrad(c_all_t, dmod, *, name, tr=256):
    D, B = c_all_t.shape
    L, _, N = dmod.shape
    tr = min(tr, D)

    def body(c_ref, d_ref, o_ref):
        cond = _silu(c_ref[...])
        acc = cond[:, 0:1] * d_ref[0:1, :]
        for b in range(1, B):
            acc = acc + cond[:, b:b + 1] * d_ref[b:b + 1, :]
        o_ref[...] = acc

    return pl.pallas_call(
        body, grid=(L, D // tr),
        in_specs=[pl.BlockSpec((tr, B), lambda l, r: (r, 0)), pl.BlockSpec((None, B, N), lambda l, r: (l, 0, 0))],
        out_specs=pl.BlockSpec((None, tr, N), lambda l, r: (l, r, 0)),
        out_shape=jax.ShapeDtypeStruct((L, D, N), F32), name=name,
        compiler_params=_cp("parallel", "parallel"))(c_all_t, dmod)


def sum_devices(parts, *, name):
    n, R, C = parts.shape

    def body(p_ref, o_ref):
        acc = p_ref[0]
        for d in range(1, n):
            acc = acc + p_ref[d]
        o_ref[...] = acc

    return pl.pallas_call(body, in_specs=[VMEM_SPEC], out_specs=VMEM_SPEC,
                          out_shape=jax.ShapeDtypeStruct((R, C), F32), name=name)(parts)


def lb_logits_grad(lb_logits, dlb, *, name):
    def body(lg_ref, d_ref, o_ref):
        l0, l1, l2 = lg_ref[0:1, :], lg_ref[1:2, :], lg_ref[2:3, :]
        m = jnp.maximum(jnp.maximum(l0, l1), l2)
        e0, e1, e2 = jnp.exp(l0 - m), jnp.exp(l1 - m), jnp.exp(l2 - m)
        z = e0 + e1 + e2
        p0, p1, p2 = e0 / z, e1 / z, e2 / z
        g = d_ref[...] * p0
        o_ref[0:1, :] = g * (1.0 - p0)
        o_ref[1:2, :] = -g * p1
        o_ref[2:3, :] = -g * p2

    return pl.pallas_call(body, in_specs=[VMEM_SPEC, VMEM_SPEC], out_specs=VMEM_SPEC,
                          out_shape=jax.ShapeDtypeStruct(lb_logits.shape, F32), name=name)(lb_logits, dlb)


def adamw(w, g, m, v, *, name, tr=256):
    R, C = w.shape
    tr = _tile(R, tr)

    def body(w_ref, g_ref, m_ref, v_ref, d_ref, nm_ref, nv_ref):
        gv = g_ref[...]
        nm = ADAM_B1 * m_ref[...] + (1.0 - ADAM_B1) * gv
        nv = ADAM_B2 * v_ref[...] + (1.0 - ADAM_B2) * (gv * gv)
        m_hat = nm / (1.0 - ADAM_B1 ** ADAM_STEP)
        v_hat = nv / (1.0 - ADAM_B2 ** ADAM_STEP)
        d_ref[...] = -ADAM_LR * (m_hat / (jnp.sqrt(v_hat) + ADAM_EPS) + ADAM_WD * w_ref[...])
        nm_ref[...] = nm
        nv_ref[...] = nv

    spec = pl.BlockSpec((tr, C), lambda i: (i, 0))
    return pl.pallas_call(
        body, grid=(R // tr,), in_specs=[spec] * 4, out_specs=[spec] * 3,
        out_shape=[jax.ShapeDtypeStruct((R, C), F32)] * 3, name=name, compiler_params=_cp("parallel"))(w, g, m, v)


def _place():
    return lax.axis_index("x"), lax.axis_index("y"), lax.axis_index("c")


def _flip(v, bit):
    return 1 - v if bit else v


def allgather_devices(v, *, name):
    R, C = v.shape

    def body(v_ref, out_ref, send_sems, recv_sems, local_sem):
        x, y, c = _place()
        me = 4 * x + 2 * y + c
        mine = pltpu.make_async_copy(v_ref, out_ref.at[me], local_sem)
        mine.start()
        sends = []
        for k in range(1, N_DEV):
            peer = (_flip(x, k & 4), _flip(y, k & 2), _flip(c, k & 1))
            cp = pltpu.make_async_remote_copy(src_ref=v_ref, dst_ref=out_ref.at[me], send_sem=send_sems.at[k - 1],
                                              recv_sem=recv_sems.at[k - 1], device_id=peer, device_id_type=MESH)
            cp.start()
            sends.append(cp)
        for k in range(1, N_DEV):
            px, py, pc = _flip(x, k & 4), _flip(y, k & 2), _flip(c, k & 1)
            pltpu.make_async_remote_copy(src_ref=v_ref, dst_ref=out_ref.at[4 * px + 2 * py + pc],
                                         send_sem=send_sems.at[k - 1], recv_sem=recv_sems.at[k - 1],
                                         device_id=(px, py, pc), device_id_type=MESH).wait_recv()
        for cp in sends:
            cp.wait_send()
        mine.wait()

    return pl.pallas_call(
        body, in_specs=[VMEM_SPEC], out_specs=VMEM_SPEC, out_shape=jax.ShapeDtypeStruct((N_DEV, R, C), v.dtype),
        scratch_shapes=[pltpu.SemaphoreType.DMA((N_DEV - 1,)), pltpu.SemaphoreType.DMA((N_DEV - 1,)),
                        pltpu.SemaphoreType.DMA], name=name)(v)


def _other_chips(x, y):
    return [(1 - x, y), (x, 1 - y), (1 - x, 1 - y)]


def allgather_chips(bufs, *, name):
    n = len(bufs)

    def body(*refs):
        outs = refs[n:2 * n]
        send_sems, recv_sems = refs[2 * n:]
        x, y, c = _place()
        q = 2 * x + y
        chips = _other_chips(x, y)

        def copy(a, k, block, half, to):
            slab = outs[a].at[block, half]
            return pltpu.make_async_remote_copy(src_ref=slab, dst_ref=slab, send_sem=send_sems.at[a, k],
                                                recv_sem=recv_sems.at[a, k], device_id=to, device_id_type=MESH)

        first = [copy(a, j, q, c, (*chips[j], c)) for a in range(n) for j in range(3)]
        for cp in first:
            cp.start()
        passed = []
        for a in range(n):
            for j, (px, py) in enumerate(chips):
                copy(a, j, 2 * px + py, c, (x, y, c)).wait_recv()
                fw = copy(a, 3 + j, 2 * px + py, c, (x, y, 1 - c))
                fw.start()
                passed.append(fw)
        for a in range(n):
            for j, (px, py) in enumerate(chips):
                copy(a, 3 + j, 2 * px + py, 1 - c, (x, y, c)).wait_recv()
        for cp in first + passed:
            cp.wait_send()

    return pl.pallas_call(
        body, in_specs=[HBM] * n, out_specs=[HBM] * n,
        out_shape=[jax.ShapeDtypeStruct(b.shape, b.dtype) for b in bufs],
        input_output_aliases={a: a for a in range(n)},
        scratch_shapes=[pltpu.SemaphoreType.DMA((n, 6)), pltpu.SemaphoreType.DMA((n, 6))], name=name)(*bufs)


SEM = pl.BlockSpec(memory_space=pltpu.SEMAPHORE)
DATAFLOW = pltpu.SideEffectType.DATAFLOW_SIDE_EFFECTING


def _chip_copy(buf, a, j, q, c, chips, send_sems, recv_sems):
    px, py = chips[j]
    return pltpu.make_async_remote_copy(src_ref=buf.at[q, c], dst_ref=buf.at[q, c], send_sem=send_sems.at[3 * a + j],
                                        recv_sem=recv_sems.at[3 * a + j], device_id=(px, py, c), device_id_type=MESH)


def allgather_chips_start(bufs, *, name):
    n = len(bufs)

    def body(*refs):
        send_sems, recv_sems = refs[n], refs[n + 1]
        outs = refs[n + 2:2 * n + 2]
        token = refs[2 * n + 2]
        x, y, c = _place()
        chips = _other_chips(x, y)
        for a in range(n):
            for j in range(3):
                _chip_copy(outs[a], a, j, 2 * x + y, c, chips, send_sems, recv_sems).start()
        token[...] = jnp.zeros_like(token)

    res = pl.pallas_call(
        body, name=name, in_specs=[HBM] * n,
        out_specs=(SEM, SEM, *([HBM] * n), VMEM_SPEC),
        out_shape=(pltpu.SemaphoreType.DMA((3 * n,)), pltpu.SemaphoreType.DMA((3 * n,)),
                   *[pltpu.HBM(b.shape, b.dtype) for b in bufs], jax.ShapeDtypeStruct((SUB, LANE), F32)),
        input_output_aliases={a: a + 2 for a in range(n)},
        compiler_params=pltpu.CompilerParams(has_side_effects=DATAFLOW),
    )(*[pltpu.with_memory_space_constraint(b, pltpu.HBM) for b in bufs])
    return res[0], res[1], list(res[2:2 + n]), res[2 + n]


def allgather_chips_wait(send_sems, recv_sems, bufs, after, *, name):
    n = len(bufs)

    def body(*refs):
        ins = refs[:n]
        send_sems, recv_sems = refs[n], refs[n + 1]
        x, y, c = _place()
        chips = _other_chips(x, y)
        for a in range(n):
            for j, (px, py) in enumerate(chips):
                _chip_copy(ins[a], a, j, 2 * x + y, c, chips, send_sems, recv_sems).wait_send()
                _chip_copy(ins[a], a, j, 2 * px + py, c, chips, send_sems, recv_sems).wait_recv()

    return list(pl.pallas_call(
        body, name=name, in_specs=[HBM] * n + [SEM, SEM, pl.BlockSpec(memory_space=pl.ANY)],
        out_specs=[HBM] * n, out_shape=[pltpu.HBM(b.shape, b.dtype) for b in bufs],
        input_output_aliases={a: a for a in range(n)},
        compiler_params=pltpu.CompilerParams(has_side_effects=DATAFLOW),
    )(*bufs, send_sems, recv_sems, after))


def forward_to_sibling(bufs, *, name):
    n = len(bufs)

    def body(*refs):
        outs = refs[n:2 * n]
        send_sems, recv_sems = refs[2 * n:]
        x, y, c = _place()
        chips = _other_chips(x, y)

        def copy(a, j, half, to):
            px, py = chips[j]
            slab = outs[a].at[2 * px + py, half]
            return pltpu.make_async_remote_copy(src_ref=slab, dst_ref=slab, send_sem=send_sems.at[a, j],
                                                recv_sem=recv_sems.at[a, j], device_id=to, device_id_type=MESH)

        sends = [copy(a, j, c, (x, y, 1 - c)) for a in range(n) for j in range(3)]
        for cp in sends:
            cp.start()
        for a in range(n):
            for j in range(3):
                copy(a, j, 1 - c, (x, y, c)).wait_recv()
        for cp in sends:
            cp.wait_send()

    return pl.pallas_call(
        body, in_specs=[HBM] * n, out_specs=[HBM] * n,
        out_shape=[jax.ShapeDtypeStruct(b.shape, b.dtype) for b in bufs],
        input_output_aliases={a: a for a in range(n)},
        scratch_shapes=[pltpu.SemaphoreType.DMA((n, 3)), pltpu.SemaphoreType.DMA((n, 3))], name=name)(*bufs)


def pair_exchange(grads, *, name):
    n = len(grads)

    def body(*refs):
        ins, outs = refs[:n], refs[n:2 * n]
        send_sems, recv_sems = refs[2 * n:]
        x, y, c = _place()
        cps = [pltpu.make_async_remote_copy(src_ref=ins[a].at[1 - c], dst_ref=outs[a], send_sem=send_sems.at[a],
                                            recv_sem=recv_sems.at[a], device_id=(x, y, 1 - c), device_id_type=MESH)
               for a in range(n)]
        for cp in cps:
            cp.start()
        for cp in cps:
            cp.wait_recv()
        for cp in cps:
            cp.wait_send()

    return pl.pallas_call(
        body, in_specs=[HBM] * n, out_specs=[HBM] * n,
        out_shape=[jax.ShapeDtypeStruct(g.shape[1:], g.dtype) for g in grads],
        scratch_shapes=[pltpu.SemaphoreType.DMA((n,)), pltpu.SemaphoreType.DMA((n,))], name=name)(*grads)


def pair_add(g, other, c_idx, *, name, tr=256):
    _, Q, R, C = g.shape
    tr = _tile(R, tr)

    def body(c_ref, g_ref, o_ref, out_ref):
        out_ref[...] = (g_ref[...] + o_ref[...]).astype(BF16)

    return pl.pallas_call(
        body,
        grid_spec=pltpu.PrefetchScalarGridSpec(
            num_scalar_prefetch=1, grid=(Q, R // tr),
            in_specs=[pl.BlockSpec((None, None, tr, C), lambda q, r, c_ref: (c_ref[0], q, r, 0)),
                      pl.BlockSpec((None, tr, C), lambda q, r, c_ref: (q, r, 0))],
            out_specs=pl.BlockSpec((None, tr, C), lambda q, r, c_ref: (q, r, 0))),
        out_shape=jax.ShapeDtypeStruct((Q, R, C), BF16), name=name,
        compiler_params=_cp("parallel", "parallel"))(c_idx, g, other)


def chip_exchange(sums, *, name):
    n = len(sums)

    def body(*refs):
        ins, outs = refs[:n], refs[n:2 * n]
        send_sems, recv_sems = refs[2 * n:]
        x, y, c = _place()
        chips = _other_chips(x, y)
        sends = [pltpu.make_async_remote_copy(src_ref=ins[a].at[2 * px + py], dst_ref=outs[a].at[j],
                                              send_sem=send_sems.at[a, j], recv_sem=recv_sems.at[a, j],
                                              device_id=(px, py, c), device_id_type=MESH)
                 for a in range(n) for j, (px, py) in enumerate(chips)]
        for cp in sends:
            cp.start()
        for cp in sends:
            cp.wait_recv()
        for cp in sends:
            cp.wait_send()

    return pl.pallas_call(
        body, in_specs=[HBM] * n, out_specs=[HBM] * n,
        out_shape=[jax.ShapeDtypeStruct((3,) + s.shape[1:], s.dtype) for s in sums],
        scratch_shapes=[pltpu.SemaphoreType.DMA((n, 3)), pltpu.SemaphoreType.DMA((n, 3))], name=name)(*sums)


def chip_sum(sums, landed, qc_idx, *, name, tr=256):
    _, R, C = sums.shape
    tr = _tile(R, tr)

    def body(qc_ref, own_ref, l_ref, o_ref):
        acc = own_ref[...].astype(F32)
        for k in range(3):
            acc = acc + l_ref[k].astype(F32)
        o_ref[...] = acc

    return pl.pallas_call(
        body,
        grid_spec=pltpu.PrefetchScalarGridSpec(
            num_scalar_prefetch=1, grid=(R // tr,),
            in_specs=[pl.BlockSpec((None, tr, C), lambda r, qc: (qc[0], r, 0)),
                      pl.BlockSpec((3, tr, C), lambda r, qc: (0, r, 0))],
            out_specs=pl.BlockSpec((None, tr, C), lambda r, qc: (qc[1], r, 0))),
        out_shape=jax.ShapeDtypeStruct((2, R, C), F32), name=name,
        compiler_params=_cp("parallel"))(qc_idx, sums, landed)


def half_swap(bufs, *, name):
    n = len(bufs)

    def body(*refs):
        outs = refs[n:2 * n]
        send_sems, recv_sems = refs[2 * n:]
        x, y, c = _place()
        cps = [pltpu.make_async_remote_copy(src_ref=outs[a].at[c], dst_ref=outs[a].at[c], send_sem=send_sems.at[a],
                                            recv_sem=recv_sems.at[a], device_id=(x, y, 1 - c), device_id_type=MESH)
               for a in range(n)]
        for cp in cps:
            cp.start()
        for a in range(n):
            pltpu.make_async_remote_copy(src_ref=outs[a].at[c], dst_ref=outs[a].at[1 - c], send_sem=send_sems.at[a],
                                         recv_sem=recv_sems.at[a], device_id=(x, y, 1 - c),
                                         device_id_type=MESH).wait_recv()
        for cp in cps:
            cp.wait_send()

    return pl.pallas_call(
        body, in_specs=[HBM] * n, out_specs=[HBM] * n,
        out_shape=[jax.ShapeDtypeStruct(b.shape, b.dtype) for b in bufs],
        input_output_aliases={a: a for a in range(n)},
        scratch_shapes=[pltpu.SemaphoreType.DMA((n,)), pltpu.SemaphoreType.DMA((n,))], name=name)(*bufs)


def _exchange_copy(sums, landed, a, j, c, chips, send_sems, recv_sems):
    px, py = chips[j]
    return pltpu.make_async_remote_copy(src_ref=sums.at[2 * px + py], dst_ref=landed.at[j],
                                        send_sem=send_sems.at[3 * a + j], recv_sem=recv_sems.at[3 * a + j],
                                        device_id=(px, py, c), device_id_type=MESH)


def chip_exchange_start(sums, *, name):
    n = len(sums)
    landing = [lax.empty((3,) + s.shape[1:], s.dtype) for s in sums]

    def body(*refs):
        send_sems, recv_sems = refs[2 * n], refs[2 * n + 1]
        src, dst = refs[2 * n + 2:3 * n + 2], refs[3 * n + 2:4 * n + 2]
        token = refs[4 * n + 2]
        x, y, c = _place()
        chips = _other_chips(x, y)
        for a in range(n):
            for j in range(3):
                _exchange_copy(src[a], dst[a], a, j, c, chips, send_sems, recv_sems).start()
        token[...] = jnp.zeros_like(token)

    res = pl.pallas_call(
        body, name=name, in_specs=[HBM] * (2 * n),
        out_specs=(SEM, SEM, *([HBM] * (2 * n)), VMEM_SPEC),
        out_shape=(pltpu.SemaphoreType.DMA((3 * n,)), pltpu.SemaphoreType.DMA((3 * n,)),
                   *[pltpu.HBM(b.shape, b.dtype) for b in sums + landing], jax.ShapeDtypeStruct((SUB, LANE), F32)),
        input_output_aliases={a: a + 2 for a in range(2 * n)},
        compiler_params=pltpu.CompilerParams(has_side_effects=DATAFLOW),
    )(*[pltpu.with_memory_space_constraint(b, pltpu.HBM) for b in sums + landing])
    return res[0], res[1], list(res[2:2 + n]), list(res[2 + n:2 + 2 * n]), res[2 + 2 * n]


def chip_exchange_wait(send_sems, recv_sems, sums, landed, after, *, name):
    n = len(sums)

    def body(*refs):
        src, dst = refs[:n], refs[n:2 * n]
        send_sems, recv_sems = refs[2 * n], refs[2 * n + 1]
        x, y, c = _place()
        chips = _other_chips(x, y)
        for a in range(n):
            for j in range(3):
                cp = _exchange_copy(src[a], dst[a], a, j, c, chips, send_sems, recv_sems)
                cp.wait_send()
                cp.wait_recv()

    res = pl.pallas_call(
        body, name=name, in_specs=[HBM] * (2 * n) + [SEM, SEM, pl.BlockSpec(memory_space=pl.ANY)],
        out_specs=[HBM] * (2 * n), out_shape=[pltpu.HBM(b.shape, b.dtype) for b in sums + landed],
        input_output_aliases={a: a for a in range(2 * n)},
        compiler_params=pltpu.CompilerParams(has_side_effects=DATAFLOW),
    )(*sums, *landed, send_sems, recv_sems, after)
    return list(res[:n]), list(res[n:])


def pair_reduce(grads, c, tag):
    c_idx = c.astype(jnp.int32).reshape(1)
    others = pair_exchange(grads, name=f"grad_pair_exchange_{tag}")
    return [pair_add(g, o, c_idx, name=f"grad_pair_add_{tag}{a}") for a, (g, o) in enumerate(zip(grads, others))]


def finish_reduce(sums, landed, q, c, tag):
    qc_idx = jnp.stack([q, c]).astype(jnp.int32)
    return [chip_sum(s, l, qc_idx, name=f"grad_chip_sum_{tag}{a}") for a, (s, l) in enumerate(zip(sums, landed))]


def _ffn_forward(x, mod, pre_g, post_g, w_up, w_down, dw_w, dw_b, tag):
    sh, sc, gate = mod
    h = prenorm(x, pre_g, sc, sh, name=f"{tag}_prenorm")
    u0 = mm_nn(h, w_up, name=f"{tag}_up", out_dtype=BF16, perm=_ffn_perm)
    z = ffn_act(u0, dw_w, dw_b, name=f"{tag}_act")
    y = mm_nn(z, w_down, name=f"{tag}_down")
    x_new = post_residual(x, y, post_g, gate, name=f"{tag}_post")
    return x_new, (x, h, u0, z, y)


def _ffn_backward(dx, saved, mod, pre_g, post_g, w_up, w_down, dw_w, dw_b, tag):
    x, h, u0, z, y = saved
    sh, sc, gate = mod
    dy, dgate, dpost, _ = post_bwd(dx, y, post_g, gate, name=f"{tag}_post_bwd")
    dz = mm_nt(dy, w_down, name=f"{tag}_down_dx", out_dtype=BF16)
    g_down = mm_tn(z, dy, name=f"{tag}_down_dw", J=2, block="a", row_chips=2)
    du0, dconv = ffn_act_bwd(dz, u0, dw_w, dw_b, name=f"{tag}_act_bwd")
    dh = mm_nt(du0, w_up, name=f"{tag}_up_dx", perm=_ffn_perm)
    g_up = mm_tn(h, du0, name=f"{tag}_up_dw", J=4, block="b", perm=_ffn_perm)
    dx_in, dsh, dsc, dpre = prenorm_bwd(dh, x, dx, pre_g, sc, name=f"{tag}_prenorm_bwd")
    nb = u0.shape[1] // 4
    dconv = dconv[:, 0].reshape(4, 2, 2, nb).transpose(0, 2, 1, 3).reshape(4, 4 * nb)
    return dx_in, dict(dsh=dsh, dsc=dsc, dgate=dgate, dpre=dpre, dpost=dpost, g_up=g_up, g_down=g_down,
                       d_dw_w=dconv[0:FFN_W], d_dw_b=dconv[3:4])


def _local_step(x, tgt, mods, P, late_weights=None, grads_ready=None):
    m0, m1 = mods
    h1 = prenorm(x, P["pre_mix_g"][0:1], m0[1], m0[0], name="hgrn_prenorm")
    proj = mm_nn(h1, P["hgrn_w_in"], name="hgrn_in")
    o, states = hgrn_scan(proj, P["hgrn_lb_logits"], name="hgrn_scan")
    og = hgrn_gate(o, proj, P["hgrn_gnorm_g"], name="hgrn_gate")
    y1 = mm_nn(og, P["hgrn_w_out"], name="hgrn_out")
    x1 = post_residual(x, y1, P["post_mix_g"][0:1], m0[2], name="hgrn_post")
    if late_weights is not None:
        P = {**P, **late_weights(x1)}
    x2, ffn0 = _ffn_forward(x1, m0[3:6], P["pre_ffn_g"][0:1], P["post_ffn_g"][0:1], P["ffn_w_up"][0],
                            P["ffn_w_down"][0], P["ffn_dw_w"][0], P["ffn_dw_b"][0:1], "ffn0")
    h3 = prenorm(x2, P["pre_mix_g"][1:2], m1[1], m1[0], name="conv_prenorm")
    u = mm_nn(h3, P["conv_w_in"], name="conv_in", bias=P["conv_b_in"])
    s, cv = conv_act(u, P["conv_dw_w"], P["conv_dw_b"], P["conv_ln_g"], P["conv_ln_b"], name="conv_act")
    y3 = mm_nn(s, P["conv_w_out"], name="conv_out", bias=P["conv_b_out"])
    x3 = post_residual(x2, y3, P["post_mix_g"][1:2], m1[2], name="conv_post")
    x4, ffn1 = _ffn_forward(x3, m1[3:6], P["pre_ffn_g"][1:2], P["post_ffn_g"][1:2], P["ffn_w_up"][1],
                            P["ffn_w_down"][1], P["ffn_dw_w"][1], P["ffn_dw_b"][1:2], "ffn1")
    dx4, lcols = loss_grad(x4, tgt, name="loss")
    dx3, f1 = _ffn_backward(dx4, ffn1, m1[3:6], P["pre_ffn_g"][1:2], P["post_ffn_g"][1:2], P["ffn_w_up"][1],
                            P["ffn_w_down"][1], P["ffn_dw_w"][1], P["ffn_dw_b"][1:2], "ffn1")
    dy3, dg1_1, dpostmix1, d_b_out = post_bwd(dx3, y3, P["post_mix_g"][1:2], m1[2], name="conv_post_bwd")
    ds = mm_nt(dy3, P["conv_w_out"], name="conv_out_dx")
    g_conv_out = mm_tn(s, dy3, name="conv_out_dw", J=1, block="a", row_chips=4)
    dcv, d_ln_g, d_ln_b, d_dw_b = conv_norm_bwd(ds, cv, P["conv_ln_g"], P["conv_ln_b"], name="conv_norm_bwd")
    du, d_dw_w, d_b_in = conv_glu_bwd(dcv, u, P["conv_dw_w"], name="conv_glu_bwd")
    dh3 = mm_nt(du, P["conv_w_in"], name="conv_in_dx")
    g_conv_in = mm_tn(h3, du, name="conv_in_dw", J=2, block="b", col_chips=2)
    dx2, dsh1_1, dsc1_1, dpremix1 = prenorm_bwd(dh3, x2, dx3, P["pre_mix_g"][1:2], m1[1], name="conv_prenorm_bwd")
    if grads_ready is not None:
        token = grads_ready("l1", [g_conv_in, g_conv_out, f1["g_up"], f1["g_down"]])
        m0 = tuple(m + token[0:1, 0:1] for m in m0)
    dx1, f0 = _ffn_backward(dx2, ffn0, m0[3:6], P["pre_ffn_g"][0:1], P["post_ffn_g"][0:1], P["ffn_w_up"][0],
                            P["ffn_w_down"][0], P["ffn_dw_w"][0], P["ffn_dw_b"][0:1], "ffn0")
    if grads_ready is not None:
        token = grads_ready("f0", [f0["g_up"], f0["g_down"]])
        m0 = tuple(m + token[0:1, 0:1] for m in m0)
    dy1, dg1_0, dpostmix0, _ = post_bwd(dx1, y1, P["post_mix_g"][0:1], m0[2], name="hgrn_post_bwd")
    dog = mm_nt(dy1, P["hgrn_w_out"], name="hgrn_out_dx")
    g_hgrn_out = mm_tn(og, dy1, name="hgrn_out_dw", J=1, block="a", row_chips=4)
    do, dgp, d_gn = hgrn_gate_bwd(dog, o, proj, P["hgrn_gnorm_g"], name="hgrn_gate_bwd")
    dqp, dfz, dv, dlb = hgrn_scan_bwd(proj, P["hgrn_lb_logits"], states, do, name="hgrn_scan_bwd")
    dproj = jnp.concatenate([dqp, dfz, dv, dgp], axis=1)
    dh1 = mm_nt(dproj, P["hgrn_w_in"], name="hgrn_in_dx")
    g_hgrn_in = mm_tn(h1, dproj, name="hgrn_in_dw", J=4, block="b")
    dx0, dsh1_0, dsc1_0, dpremix0 = prenorm_bwd(dh1, x, dx1, P["pre_mix_g"][0:1], m0[1], name="hgrn_prenorm_bwd")

    dmod = jnp.stack([
        jnp.concatenate([dsh1_0, dsc1_0, dg1_0, f0["dsh"], f0["dsc"], f0["dgate"]], axis=1)[0],
        jnp.concatenate([dsh1_1, dsc1_1, dg1_1, f1["dsh"], f1["dsc"], f1["dgate"]], axis=1)[0]])
    small = dict(
        loss=lcols,
        pre_mix_g=jnp.concatenate([dpremix0, dpremix1]), post_mix_g=jnp.concatenate([dpostmix0, dpostmix1]),
        pre_ffn_g=jnp.concatenate([f0["dpre"], f1["dpre"]]), post_ffn_g=jnp.concatenate([f0["dpost"], f1["dpost"]]),
        lb=dlb, hgrn_gnorm_g=d_gn, ffn_dw_b=jnp.concatenate([f0["d_dw_b"], f1["d_dw_b"]]), dmod=dmod,
        conv_b_in=d_b_in, conv_dw_w=d_dw_w[0:CONV_W], conv_dw_b=d_dw_b, conv_ln_g=d_ln_g, conv_ln_b=d_ln_b,
        conv_b_out=d_b_out, ffn_dw_w=jnp.stack([f0["d_dw_w"], f1["d_dw_w"]]))
    big = [g_hgrn_in, g_hgrn_out, g_conv_in, g_conv_out, f0["g_up"], f1["g_up"], f0["g_down"], f1["g_down"]]
    return dx0, small, big


def _pack(parts, rows=8):
    flat = jnp.concatenate([p.reshape(-1).astype(F32) for p in parts])
    per = rows * 128
    pad = (-flat.shape[0]) % per
    return jnp.pad(flat, (0, pad)).reshape(rows, -1)


def _unpack(flat, shapes):
    out, off = [], 0
    for s in shapes:
        n = 1
        for d in s:
            n *= d
        out.append(flat[..., off:off + n].reshape(flat.shape[:-1] + tuple(s)))
        off += n
    return out


def _from_chips(stacked, axis):
    moved = jnp.moveaxis(stacked, 0, axis)
    shape = list(moved.shape)
    return moved.reshape(shape[:axis] + [shape[axis] * shape[axis + 1]] + shape[axis + 2:])


def _my_shard(full, axis, q):
    n = full.shape[axis] // N_CHIPS
    return lax.dynamic_slice_in_dim(full, q * n, n, axis=axis)


def kernel(x, c, ada_w, ada_b, pre_mix_g, post_mix_g, pre_ffn_g, post_ffn_g, hgrn_w_in, hgrn_lb_logits, hgrn_gnorm_g, hgrn_w_out, conv_w_in, conv_b_in, conv_dw_w, conv_dw_b, conv_ln_g, conv_ln_b, conv_w_out, conv_b_out, ffn_w_up, ffn_dw_w, ffn_dw_b, ffn_w_down, loss_target, m_ada_w, m_ada_b, m_pre_mix_g, m_post_mix_g, m_pre_ffn_g, m_post_ffn_g, m_hgrn_w_in, m_hgrn_lb_logits, m_hgrn_gnorm_g, m_hgrn_w_out, m_conv_w_in, m_conv_b_in, m_conv_dw_w, m_conv_dw_b, m_conv_ln_g, m_conv_ln_b, m_conv_w_out, m_conv_b_out, m_ffn_w_up, m_ffn_dw_w, m_ffn_dw_b, m_ffn_w_down, v_ada_w, v_ada_b, v_pre_mix_g, v_post_mix_g, v_pre_ffn_g, v_post_ffn_g, v_hgrn_w_in, v_hgrn_lb_logits, v_hgrn_gnorm_g, v_hgrn_w_out, v_conv_w_in, v_conv_b_in, v_conv_dw_w, v_conv_dw_b, v_conv_ln_g, v_conv_ln_b, v_conv_w_out, v_conv_b_out, v_ffn_w_up, v_ffn_dw_w, v_ffn_dw_b, v_ffn_w_down):
    W = dict(ada_w=ada_w, ada_b=ada_b, pre_mix_g=pre_mix_g, post_mix_g=post_mix_g, pre_ffn_g=pre_ffn_g,
             post_ffn_g=post_ffn_g, hgrn_w_in=hgrn_w_in, hgrn_lb_logits=hgrn_lb_logits, hgrn_gnorm_g=hgrn_gnorm_g,
             hgrn_w_out=hgrn_w_out, conv_w_in=conv_w_in, conv_b_in=conv_b_in, conv_dw_w=conv_dw_w,
             conv_dw_b=conv_dw_b, conv_ln_g=conv_ln_g, conv_ln_b=conv_ln_b, conv_w_out=conv_w_out,
             conv_b_out=conv_b_out, ffn_w_up=ffn_w_up, ffn_dw_w=ffn_dw_w, ffn_dw_b=ffn_dw_b, ffn_w_down=ffn_w_down)
    M = dict(ada_w=m_ada_w, ada_b=m_ada_b, pre_mix_g=m_pre_mix_g, post_mix_g=m_post_mix_g, pre_ffn_g=m_pre_ffn_g,
             post_ffn_g=m_post_ffn_g, hgrn_w_in=m_hgrn_w_in, hgrn_lb_logits=m_hgrn_lb_logits,
             hgrn_gnorm_g=m_hgrn_gnorm_g, hgrn_w_out=m_hgrn_w_out, conv_w_in=m_conv_w_in, conv_b_in=m_conv_b_in,
             conv_dw_w=m_conv_dw_w, conv_dw_b=m_conv_dw_b, conv_ln_g=m_conv_ln_g, conv_ln_b=m_conv_ln_b,
             conv_w_out=m_conv_w_out, conv_b_out=m_conv_b_out, ffn_w_up=m_ffn_w_up, ffn_dw_w=m_ffn_dw_w,
             ffn_dw_b=m_ffn_dw_b, ffn_w_down=m_ffn_w_down)
    V = dict(ada_w=v_ada_w, ada_b=v_ada_b, pre_mix_g=v_pre_mix_g, post_mix_g=v_post_mix_g, pre_ffn_g=v_pre_ffn_g,
             post_ffn_g=v_post_ffn_g, hgrn_w_in=v_hgrn_w_in, hgrn_lb_logits=v_hgrn_lb_logits,
             hgrn_gnorm_g=v_hgrn_gnorm_g, hgrn_w_out=v_hgrn_w_out, conv_w_in=v_conv_w_in, conv_b_in=v_conv_b_in,
             conv_dw_w=v_conv_dw_w, conv_dw_b=v_conv_dw_b, conv_ln_g=v_conv_ln_g, conv_ln_b=v_conv_ln_b,
             conv_w_out=v_conv_w_out, conv_b_out=v_conv_b_out, ffn_w_up=v_ffn_w_up, ffn_dw_w=v_ffn_dw_w,
             ffn_dw_b=v_ffn_dw_b, ffn_w_down=v_ffn_w_down)
    names = list(W)
    xi, yi, ci = lax.axis_index("x"), lax.axis_index("y"), lax.axis_index("c")
    q = 2 * xi + yi
    me = 2 * q + ci
    D = x.shape[-1]
    L = ada_w.shape[0]

    small_w = ["conv_b_in", "conv_dw_w", "conv_dw_b", "conv_ln_g", "conv_ln_b", "conv_b_out", "ffn_dw_w"]
    small_axis = dict(conv_b_in=1, conv_dw_w=2, conv_dw_b=1, conv_ln_g=1, conv_ln_b=1, conv_b_out=1, ffn_dw_w=2)
    packed = _pack([c] + [W[n] for n in small_w])
    gathered = allgather_devices(packed, name="gather_small_params").reshape(N_DEV, -1)
    c_all = gathered[:, 0:D]
    per_chip = gathered.reshape(N_CHIPS, 2, -1)[:, 0, D:]
    parts = _unpack(per_chip, [W[n].shape for n in small_w])
    P = {n: _from_chips(p, small_axis[n]) for n, p in zip(small_w, parts)}
    P["conv_dw_w"] = P["conv_dw_w"][0]
    for n in ("pre_mix_g", "post_mix_g", "pre_ffn_g", "post_ffn_g", "hgrn_lb_logits", "hgrn_gnorm_g", "ffn_dw_b"):
        P[n] = W[n]

    modp = ada_mod(c_all, ada_w, name="ada_mod")
    ncol = modp.shape[-1]
    mod_all = allgather_devices(modp.reshape(L * N_DEV, ncol), name="gather_mod")
    mod_all = mod_all.reshape(N_CHIPS, 2, L, N_DEV, ncol)[:, 0]
    mod_me = lax.dynamic_index_in_dim(mod_all, me, axis=2, keepdims=False)
    mod = mod_me.transpose(1, 0, 2).reshape(L, N_CHIPS * ncol) + ada_b
    mods = [tuple(mod[l:l + 1, k * D:(k + 1) * D] for k in range(6)) for l in range(L)]

    def halves(w):
        shard = w.astype(BF16).reshape(1, 2, w.shape[0] // 2, w.shape[1])
        buf = lax.empty((N_CHIPS,) + shard.shape[1:], BF16)
        return lax.dynamic_update_slice_in_dim(buf, shard, q, axis=0)

    stack = lambda t: t.reshape(N_CHIPS, t.shape[1] * t.shape[2], t.shape[3])
    rowsh = lambda t: t.reshape(1, N_CHIPS * t.shape[1] * t.shape[2], t.shape[3])
    pairs = lambda t: t.reshape(2, 2, t.shape[1], t.shape[2]).transpose(0, 2, 1, 3).reshape(2, t.shape[1], 2 * t.shape[2])
    g = allgather_chips([halves(hgrn_w_in[0]), halves(hgrn_w_out[0])], name="gather_hgrn_weights")
    P["hgrn_w_in"], P["hgrn_w_out"] = stack(g[0]), rowsh(g[1])
    late_shards = [conv_w_in[0], conv_w_out[0], ffn_w_up[0], ffn_w_up[1], ffn_w_down[0], ffn_w_down[1]]
    late_bufs, _, _ = lax.optimization_barrier(([halves(w) for w in late_shards], g, mod))
    send_sems, recv_sems, bufs, token = allgather_chips_start(late_bufs, name="gather_weights_start")
    mods[0] = tuple(m + token[0:1, 0:1] for m in mods[0])

    def late_weights(x1):
        landed = allgather_chips_wait(send_sems, recv_sems, bufs, x1, name="gather_weights_wait")
        g = forward_to_sibling(landed, name="gather_weights_forward")
        return dict(conv_w_in=pairs(stack(g[0])), conv_w_out=rowsh(g[1]), ffn_w_up=[stack(g[2]), stack(g[3])],
                    ffn_w_down=[rowsh(g[4]), rowsh(g[5])])

    in_flight = {}

    def grads_ready(tag, grads):
        sums = pair_reduce(grads, ci, f"{tag}_")
        send, recv, sums, landing, tok = chip_exchange_start(sums, name=f"grad_chip_exchange_start_{tag}")
        in_flight[tag] = (send, recv, sums, landing)
        return tok

    grad_x, small, big = _local_step(x[0], loss_target[0], mods, P, late_weights, grads_ready)

    small_names = list(small)
    gs = allgather_devices(_pack([small[n] for n in small_names]), name="gather_small_grads")
    dmod_all = _unpack(gs.reshape(N_DEV, -1), [small[n].shape for n in small_names])[small_names.index("dmod")]
    tot = sum_devices(gs, name="sum_small_grads").reshape(1, -1)
    S = dict(zip(small_names, _unpack(tot, [small[n].shape for n in small_names])))
    S = {n: v[0] for n, v in S.items()}
    loss = 0.5 * jnp.sum(S["loss"]) / D

    G = {}
    dmod_q = lax.dynamic_slice_in_dim(dmod_all, q * ncol, ncol, axis=2)
    G["ada_w"] = ada_wgrad(c_all.T, dmod_q.transpose(1, 0, 2), name="ada_wgrad")
    G["ada_b"] = S["dmod"]
    for n in ("pre_mix_g", "post_mix_g", "pre_ffn_g", "post_ffn_g", "hgrn_gnorm_g", "ffn_dw_b"):
        G[n] = S[n]
    G["hgrn_lb_logits"] = lb_logits_grad(hgrn_lb_logits, S["lb"], name="lb_logits_grad")
    G["conv_b_in"] = _my_shard(S["conv_b_in"], 1, q)
    G["conv_dw_w"] = _my_shard(S["conv_dw_w"], 1, q)[None]
    for n in ("conv_dw_b", "conv_ln_g", "conv_ln_b", "conv_b_out"):
        G[n] = _my_shard(S[n], 1, q)
    G["ffn_dw_w"] = _my_shard(S["ffn_dw_w"], 2, q)

    sums_h = pair_reduce([big[0], big[1]], ci, "hg_")
    landed_h = chip_exchange(sums_h, name="grad_chip_exchange")
    halves = finish_reduce(sums_h, landed_h, q, ci, "hg_")
    for tag in ("f0", "l1"):
        sums_t, landed_t = chip_exchange_wait(*in_flight[tag], grad_x, name=f"grad_chip_exchange_wait_{tag}")
        halves += finish_reduce(sums_t, landed_t, q, ci, f"{tag}_")
    red = [f.reshape(2 * f.shape[1], f.shape[2]) for f in half_swap(halves, name="grad_half_swap")]
    G["hgrn_w_in"], G["hgrn_w_out"], G["conv_w_in"], G["conv_w_out"] = red[0][None], red[1][None], red[4][None], red[5][None]
    G["ffn_w_up"] = jnp.stack([red[2], red[6]])
    G["ffn_w_down"] = jnp.stack([red[3], red[7]])

    delta, new_m, new_v = {}, {}, {}
    big_names = ["ada_w", "hgrn_w_in", "hgrn_w_out", "conv_w_in", "conv_w_out", "ffn_w_up", "ffn_w_down"]
    for n in big_names:
        shp = W[n].shape
        two = lambda t: t.reshape(-1, shp[-1])
        d_, m_, v_ = adamw(two(W[n]), two(G[n]), two(M[n]), two(V[n]), name=f"adamw_{n}")
        delta[n], new_m[n], new_v[n] = d_.reshape(shp), m_.reshape(shp), v_.reshape(shp)
    rest = [n for n in names if n not in big_names]
    d_, m_, v_ = adamw(_pack([W[n] for n in rest]), _pack([G[n] for n in rest]), _pack([M[n] for n in rest]),
                       _pack([V[n] for n in rest]), name="adamw_small")
    shapes = [W[n].shape for n in rest]
    for n, a, b_, c_ in zip(rest, _unpack(d_.reshape(-1), shapes), _unpack(m_.reshape(-1), shapes),
                            _unpack(v_.reshape(-1), shapes)):
        delta[n], new_m[n], new_v[n] = a, b_, c_

    return (loss, grad_x[None], *[G[n].reshape(W[n].shape) for n in names], *[delta[n] for n in names],
            *[new_m[n] for n in names], *[new_v[n] for n in names])
```

```python
import jax
import jax.numpy as jnp
from jax import lax
from jax.experimental import pallas as pl
from jax.experimental.pallas import tpu as pltpu

F32 = jnp.float32
BF16 = jnp.bfloat16
EPS = 1e-6
HEAD = 128
BLK = 16
NEG = -1e30
CONV_W = 31
FFN_W = 3
N_CHIPS = 4
N_DEV = 8
SUB = 8
LANE = 128
V7X_VMEM_LIMIT = 56 * 1024 * 1024
MESH = pl.DeviceIdType.MESH
HBM = pl.BlockSpec(memory_space=pltpu.HBM)
VMEM_SPEC = pl.BlockSpec(memory_space=pltpu.VMEM)

ADAM_LR = 0.001
ADAM_B1 = 0.9
ADAM_B2 = 0.999
ADAM_EPS = 1e-08
ADAM_WD = 0.01
ADAM_STEP = 10


def _cp(*sem):
    return pltpu.CompilerParams(dimension_semantics=sem, vmem_limit_bytes=V7X_VMEM_LIMIT)


def _sig(x):
    return 0.5 * jnp.tanh(0.5 * x) + 0.5


def _silu(x):
    return x * _sig(x)


def _dsilu(x):
    s = _sig(x)
    return s * (1.0 + x * (1.0 - s))


def _dot(a, b):
    return jnp.dot(a, b, preferred_element_type=F32)


def _dot_nt(a, b):
    return lax.dot_general(a, b, (((1,), (1,)), ((), ())), preferred_element_type=F32)


def _dot_tn(a, b):
    return lax.dot_general(a, b, (((0,), (0,)), ((), ())), preferred_element_type=F32)


def _colsum(x):
    return jnp.sum(x, axis=0, keepdims=True)


def _rowmean(x):
    return jnp.mean(x, axis=-1, keepdims=True)


def _ffn_perm(j):
    return (j % 2) * 2 + j // 2


def _tile(n, pref):
    if n <= pref:
        return n
    t = pref - pref % 8
    while n % t:
        t -= 8
    return t


def mm_nn(a, w, *, name, bias=None, out_dtype=F32, perm=None, tm=1024):
    T, K = a.shape
    J, _, nb = w.shape
    tm = min(tm, T)
    col = (lambda j: j) if perm is None else perm

    def body(a_ref, w_ref, *rest):
        acc = _dot(a_ref[...], w_ref[...])
        if bias is not None:
            acc = acc + rest[0][...]
        rest[-1][...] = acc.astype(out_dtype)

    in_specs = [pl.BlockSpec((tm, K), lambda j, i: (i, 0)), pl.BlockSpec((None, K, nb), lambda j, i: (j, 0, 0))]
    args = [a, w]
    if bias is not None:
        in_specs.append(pl.BlockSpec((1, nb), lambda j, i: (0, j)))
        args.append(bias)
    return pl.pallas_call(
        body, grid=(J, T // tm), in_specs=in_specs,
        out_specs=pl.BlockSpec((tm, nb), lambda j, i: (i, col(j))),
        out_shape=jax.ShapeDtypeStruct((T, J * nb), out_dtype), name=name,
        compiler_params=_cp("parallel", "parallel"))(*args)


def mm_nt(a, w, *, name, out_dtype=F32, perm=None, tm=1024):
    T = a.shape[0]
    J, K, nb = w.shape
    tm = min(tm, T)
    col = (lambda j: j) if perm is None else perm

    def body(a_ref, w_ref, o_ref, acc_ref):
        j = pl.program_id(1)

        @pl.when(j == 0)
        def _():
            acc_ref[...] = jnp.zeros_like(acc_ref)

        acc_ref[...] += _dot_nt(a_ref[...], w_ref[...])

        @pl.when(j == J - 1)
        def _():
            o_ref[...] = acc_ref[...].astype(out_dtype)

    return pl.pallas_call(
        body, grid=(T // tm, J),
        in_specs=[pl.BlockSpec((tm, nb), lambda i, j: (i, col(j))), pl.BlockSpec((None, K, nb), lambda i, j: (j, 0, 0))],
        out_specs=pl.BlockSpec((tm, K), lambda i, j: (i, 0)),
        out_shape=jax.ShapeDtypeStruct((T, K), out_dtype),
        scratch_shapes=[pltpu.VMEM((tm, K), F32)], name=name,
        compiler_params=_cp("parallel", "arbitrary"))(a, w)


def mm_tn(a, b, *, name, J, block, row_chips=1, col_chips=1, perm=None, tk=1024):
    T = a.shape[0]
    tk = min(tk, T)
    col = (lambda j: j) if perm is None else perm
    if block == "b":
        rows, nb = a.shape[1], b.shape[1] // J
        a_spec = pl.BlockSpec((tk, rows), lambda j, t: (t, 0))
        b_spec = pl.BlockSpec((tk, nb), lambda j, t: (t, col(j)))
    else:
        rows, nb = a.shape[1] // J, b.shape[1]
        a_spec = pl.BlockSpec((tk, rows), lambda j, t: (t, col(j)))
        b_spec = pl.BlockSpec((tk, nb), lambda j, t: (t, 0))
    rh = rows // (2 * row_chips)
    nc = nb // col_chips
    chips = [(rc, cc) for rc in range(row_chips) for cc in range(col_chips)]

    def body(a_ref, b_ref, o_ref):
        @pl.when(pl.program_id(1) == 0)
        def _():
            o_ref[...] = jnp.zeros_like(o_ref)

        acc = _dot_tn(a_ref[...], b_ref[...])
        for ch, (rc, cc) in enumerate(chips):
            for hf in range(2):
                r0 = (rc * 2 + hf) * rh
                o_ref[hf, ch] += acc[r0:r0 + rh, cc * nc:(cc + 1) * nc]

    return pl.pallas_call(
        body, grid=(J, T // tk), in_specs=[a_spec, b_spec],
        out_specs=pl.BlockSpec((2, len(chips), rh, nc), lambda j, t: (0, j, 0, 0)),
        out_shape=jax.ShapeDtypeStruct((2, J * len(chips), rh, nc), F32), name=name,
        compiler_params=_cp("parallel", "arbitrary"))(a, b)


def _row(tm, w):
    return pl.BlockSpec((tm, w), lambda i: (i, 0))


def _full(r, w):
    return pl.BlockSpec((r, w), lambda i: (0, 0))


def _acc_init(i, *refs):
    @pl.when(i == 0)
    def _():
        for r in refs:
            r[...] = jnp.zeros_like(r)


def prenorm(x, g, sc, sh, *, name, tm=512):
    T, D = x.shape
    tm = min(tm, T)

    def body(x_ref, g_ref, sc_ref, sh_ref, h_ref):
        xv = x_ref[...]
        r = lax.rsqrt(_rowmean(xv * xv) + EPS)
        h_ref[...] = ((xv * r) * g_ref[...] * (1.0 + sc_ref[...]) + sh_ref[...]).astype(BF16)

    return pl.pallas_call(
        body, grid=(T // tm,), in_specs=[_row(tm, D), _full(1, D), _full(1, D), _full(1, D)],
        out_specs=_row(tm, D), out_shape=jax.ShapeDtypeStruct((T, D), BF16), name=name,
        compiler_params=_cp("parallel"))(x, g, sc, sh)


def post_residual(x, y, g, gate, *, name, tm=512):
    T, D = x.shape
    tm = min(tm, T)

    def body(x_ref, y_ref, g_ref, gate_ref, o_ref):
        yv = y_ref[...]
        r = lax.rsqrt(_rowmean(yv * yv) + EPS)
        o_ref[...] = x_ref[...] + gate_ref[...] * ((yv * r) * g_ref[...])

    return pl.pallas_call(
        body, grid=(T // tm,), in_specs=[_row(tm, D), _row(tm, D), _full(1, D), _full(1, D)],
        out_specs=_row(tm, D), out_shape=jax.ShapeDtypeStruct((T, D), F32), name=name,
        compiler_params=_cp("parallel"))(x, y, g, gate)


def loss_grad(x, tgt, *, name, tm=512):
    T, D = x.shape
    tm = min(tm, T)

    def body(x_ref, t_ref, dx_ref, l_ref):
        _acc_init(pl.program_id(0), l_ref)
        e = x_ref[...] - t_ref[...]
        dx_ref[...] = e * (1.0 / D)
        l_ref[...] += _colsum(e * e)

    return pl.pallas_call(
        body, grid=(T // tm,), in_specs=[_row(tm, D), _row(tm, D)],
        out_specs=[_row(tm, D), _full(1, D)],
        out_shape=[jax.ShapeDtypeStruct((T, D), F32), jax.ShapeDtypeStruct((1, D), F32)], name=name,
        compiler_params=_cp("arbitrary"))(x, tgt)


def post_bwd(dx, y, g, gate, *, name, tm=512):
    T, D = dx.shape
    tm = min(tm, T)

    def body(dx_ref, y_ref, g_ref, gate_ref, dy_ref, dgate_ref, dg_ref, dbias_ref):
        _acc_init(pl.program_id(0), dgate_ref, dg_ref, dbias_ref)
        yv = y_ref[...]
        dxv = dx_ref[...]
        r = lax.rsqrt(_rowmean(yv * yv) + EPS)
        yn = yv * r
        gv = g_ref[...]
        gt = gate_ref[...]
        dgate_ref[...] += _colsum(dxv * (yn * gv))
        dg_ref[...] += _colsum(dxv * gt * yn)
        dyn = dxv * gt * gv
        dy = r * (dyn - yn * _rowmean(dyn * yn))
        dbias_ref[...] += _colsum(dy)
        dy_ref[...] = dy.astype(BF16)

    return pl.pallas_call(
        body, grid=(T // tm,), in_specs=[_row(tm, D), _row(tm, D), _full(1, D), _full(1, D)],
        out_specs=[_row(tm, D), _full(1, D), _full(1, D), _full(1, D)],
        out_shape=[jax.ShapeDtypeStruct((T, D), BF16)] + [jax.ShapeDtypeStruct((1, D), F32)] * 3, name=name,
        compiler_params=_cp("arbitrary"))(dx, y, g, gate)


def prenorm_bwd(dh, x, dres, g, sc, *, name, tm=512):
    T, D = x.shape
    tm = min(tm, T)

    def body(dh_ref, x_ref, dres_ref, g_ref, sc_ref, dx_ref, dsh_ref, dsc_ref, dg_ref):
        _acc_init(pl.program_id(0), dsh_ref, dsc_ref, dg_ref)
        xv = x_ref[...]
        dhv = dh_ref[...]
        r = lax.rsqrt(_rowmean(xv * xv) + EPS)
        xn = xv * r
        gv = g_ref[...]
        one_sc = 1.0 + sc_ref[...]
        dsh_ref[...] += _colsum(dhv)
        dsc_ref[...] += _colsum(dhv * (xn * gv))
        dg_ref[...] += _colsum(dhv * one_sc * xn)
        dxn = dhv * one_sc * gv
        dx_ref[...] = dres_ref[...] + r * (dxn - xn * _rowmean(dxn * xn))

    return pl.pallas_call(
        body, grid=(T // tm,), in_specs=[_row(tm, D), _row(tm, D), _row(tm, D), _full(1, D), _full(1, D)],
        out_specs=[_row(tm, D), _full(1, D), _full(1, D), _full(1, D)],
        out_shape=[jax.ShapeDtypeStruct((T, D), F32)] + [jax.ShapeDtypeStruct((1, D), F32)] * 3, name=name,
        compiler_params=_cp("arbitrary"))(dh, x, dres, g, sc)


HALO = 16


def _shift_helpers():
    rid = lax.broadcasted_iota(jnp.int32, (SUB, LANE), 0)

    def down(cur, prev, k):
        return pltpu.roll(jnp.where(rid >= SUB - k, prev, cur), k, 0)

    def up(cur, nxt, k):
        return pltpu.roll(jnp.where(rid < k, nxt, cur), SUB - k, 0)

    return down, up


def _ffn_sides(c, nb, wa_ref, wb_ref, ba_ref, bb_ref):
    cols = slice(c * LANE, (c + 1) * LANE)
    return [(cols, [wa_ref[k:k + 1, cols] for k in range(FFN_W)], ba_ref[:, cols]),
            (slice(nb + c * LANE, nb + (c + 1) * LANE), [wb_ref[k:k + 1, cols] for k in range(FFN_W)],
             bb_ref[:, cols])]


def _ffn_specs(tm, nb, hb, idx):
    return [pl.BlockSpec((tm, 2 * nb), lambda jc, i: (idx(i), jc)),
            pl.BlockSpec((HALO, 2 * nb), lambda jc, i: (jnp.maximum(idx(i) * hb - 1, 0), jc)),
            pl.BlockSpec((FFN_W, nb), lambda jc, i: (0, jc)),
            pl.BlockSpec((FFN_W, nb), lambda jc, i: (0, jc + 2)),
            pl.BlockSpec((1, nb), lambda jc, i: (0, jc)),
            pl.BlockSpec((1, nb), lambda jc, i: (0, jc + 2))]


def ffn_act(u0p, dw_w, dw_b, *, name, tm=256):
    T, W = u0p.shape
    nb = W // 4
    tm = min(tm, T)
    unroll = 4
    rows16 = 2 * SUB

    def body(u_ref, halo_ref, wa_ref, wb_ref, ba_ref, bb_ref, z_ref, ab_ref):
        i = pl.program_id(1)
        down, _ = _shift_helpers()
        for c in range(nb // LANE):
            cols = slice(c * LANE, (c + 1) * LANE)
            side = _ffn_sides(c, nb, wa_ref, wb_ref, ba_ref, bb_ref)

            def rows(j, prev):
                prev = list(prev)
                for m in range(unroll):
                    r0 = pl.multiple_of((j * unroll + m) * rows16, rows16)
                    x = [u_ref[pl.ds(r0, rows16), cs].astype(F32) for cs, _, _ in side]
                    conv = [[None, None], [None, None]]
                    for hf in range(2):
                        for n, (_, w, b) in enumerate(side):
                            cur = x[n][hf * SUB:(hf + 1) * SUB, :]
                            conv[n][hf] = b + w[2] * cur + w[1] * down(cur, prev[n], 1) + w[0] * down(cur, prev[n], 2)
                            prev[n] = cur
                    a, b = [jnp.concatenate(conv[n], axis=0) for n in range(2)]
                    z_ref[pl.ds(r0, rows16), cols] = (_silu(a) * b).astype(BF16)
                    ab_ref[pl.ds(r0, rows16), side[0][0]] = a.astype(BF16)
                    ab_ref[pl.ds(r0, rows16), side[1][0]] = b.astype(BF16)
                return tuple(prev)

            first = [jnp.where(i == 0, 0.0, halo_ref[:, cs].astype(F32)[SUB:2 * SUB, :]) for cs, _, _ in side]
            lax.fori_loop(0, tm // (rows16 * unroll), rows, tuple(first))

    return pl.pallas_call(
        body, grid=(2, T // tm), in_specs=_ffn_specs(tm, nb, tm // HALO, lambda i: i),
        out_specs=[pl.BlockSpec((tm, nb), lambda jc, i: (i, jc)), pl.BlockSpec((tm, 2 * nb), lambda jc, i: (i, jc))],
        out_shape=[jax.ShapeDtypeStruct((T, 2 * nb), BF16), jax.ShapeDtypeStruct((T, W), BF16)], name=name,
        compiler_params=_cp("parallel", "arbitrary"))(u0p, u0p, dw_w, dw_w, dw_b, dw_b)


def ffn_act_bwd(dz, u0p, ab, dw_w, *, name, tm=256):
    T, W = u0p.shape
    nb = W // 4
    tm = min(tm, T)
    nt = T // tm
    unroll = 4
    rows16 = 2 * SUB
    n_it = tm // (rows16 * unroll)

    def body(dz_ref, u_ref, ab_ref, wa_ref, wb_ref, du0_ref, dw_ref, carry):
        i = pl.program_id(1)
        _acc_init(i, dw_ref)
        _, up = _shift_helpers()
        for c in range(nb // LANE):
            cols = slice(c * LANE, (c + 1) * LANE)
            side = [(cols, [wa_ref[k:k + 1, cols] for k in range(FFN_W)]),
                    (slice(nb + c * LANE, nb + (c + 1) * LANE), [wb_ref[k:k + 1, cols] for k in range(FFN_W)])]

            def rows(j, st):
                nxt, acc = list(st[0:2]), list(st[2:10])
                for m in range(unroll):
                    r0 = pl.multiple_of(((n_it - 1 - j) * unroll + unroll - 1 - m) * rows16, rows16)
                    dzv = dz_ref[pl.ds(r0, rows16), cols].astype(F32)
                    a, b = [ab_ref[pl.ds(r0, rows16), cs].astype(F32) for cs, _ in side]
                    x = [u_ref[pl.ds(r0, rows16), cs].astype(F32) for cs, _ in side]
                    sa = _sig(a)
                    d16 = [dzv * b * (sa * (1.0 + a * (1.0 - sa))), dzv * (a * sa)]
                    out = [[None, None], [None, None]]
                    for hf in (1, 0):
                        half = slice(hf * SUB, (hf + 1) * SUB)
                        for n in range(2):
                            w = side[n][1]
                            d = d16[n][half, :]
                            u = x[n][half, :]
                            up1, up2 = up(d, nxt[n], 1), up(d, nxt[n], 2)
                            acc[4 * n + 0] = acc[4 * n + 0] + up2 * u
                            acc[4 * n + 1] = acc[4 * n + 1] + up1 * u
                            acc[4 * n + 2] = acc[4 * n + 2] + d * u
                            acc[4 * n + 3] = acc[4 * n + 3] + d
                            out[n][hf] = w[2] * d + w[1] * up1 + w[0] * up2
                            nxt[n] = d
                    for n in range(2):
                        du0_ref[pl.ds(r0, rows16), side[n][0]] = jnp.concatenate(out[n], axis=0).astype(BF16)
                return (*nxt, *acc)

            init = [jnp.where(i == 0, 0.0, carry[:, cs]) for cs, _ in side] + [jnp.zeros((SUB, LANE), F32)] * 8
            st = lax.fori_loop(0, n_it, rows, tuple(init))
            for n in range(2):
                carry[:, side[n][0]] = st[n]
                for k in range(4):
                    dw_ref[k, :, side[n][0]] += st[2 + 4 * n + k]

        @pl.when(i == nt - 1)
        def _():
            for k in range(4):
                dw_ref[k, 0:1, :] = _colsum(dw_ref[k])

    rev = lambda i: nt - 1 - i
    wide = pl.BlockSpec((tm, 2 * nb), lambda jc, i: (rev(i), jc))
    return pl.pallas_call(
        body, grid=(2, nt),
        in_specs=[pl.BlockSpec((tm, nb), lambda jc, i: (rev(i), jc)), wide, wide,
                  pl.BlockSpec((FFN_W, nb), lambda jc, i: (0, jc)), pl.BlockSpec((FFN_W, nb), lambda jc, i: (0, jc + 2))],
        out_specs=[wide, pl.BlockSpec((4, SUB, 2 * nb), lambda jc, i: (0, 0, jc))],
        out_shape=[jax.ShapeDtypeStruct((T, W), BF16), jax.ShapeDtypeStruct((4, SUB, W), F32)],
        scratch_shapes=[pltpu.VMEM((SUB, 2 * nb), F32)], name=name,
        compiler_params=_cp("parallel", "arbitrary"))(dz, u0p, ab, dw_w, dw_w)


CHALO = 32
CCOL = 256


def _phase_copies(buf, shifted, tm):
    n = tm + CHALO - SUB
    for p in range(1, SUB):
        shifted[p - 1, 0:n, :] = buf[p:p + n, :]


def _shifted(buf, shifted, r, tm, c0):
    m, p = divmod(r, SUB)
    src = buf if p == 0 else shifted.at[p - 1]
    return src[m * SUB:m * SUB + tm, c0:c0 + CCOL]


def conv_act(u, dw_w, dw_b, ln_g, ln_b, *, name, tm=128):
    T, D2 = u.shape
    D = D2 // 2
    tm = min(tm, T)
    hb = tm // CHALO

    def body(u_ref, halo_ref, w_ref, b_ref, g_ref, be_ref, s_ref, cv_ref, gbuf, gsh):
        i = pl.program_id(0)
        hv = halo_ref[...]
        gbuf[0:CHALO, :] = jnp.where(i == 0, 0.0, hv[:, 0:D] * _sig(hv[:, D:D2]))
        uv = u_ref[...]
        gbuf[CHALO:CHALO + tm, :] = uv[:, 0:D] * _sig(uv[:, D:D2])
        _phase_copies(gbuf, gsh, tm)
        for c0 in range(0, D, CCOL):
            acc = jnp.zeros((tm, CCOL), F32) + b_ref[:, c0:c0 + CCOL]
            for k in range(CONV_W):
                acc = acc + w_ref[k:k + 1, c0:c0 + CCOL] * _shifted(gbuf, gsh, CHALO - (CONV_W - 1) + k, tm, c0)
            cv_ref[:, c0:c0 + CCOL] = acc
        cv = cv_ref[...]
        mu = _rowmean(cv)
        xc = cv - mu
        nh = xc * lax.rsqrt(_rowmean(xc * xc) + EPS)
        s_ref[...] = _silu(nh * g_ref[...] + be_ref[...]).astype(BF16)

    return pl.pallas_call(
        body, grid=(T // tm,),
        in_specs=[_row(tm, D2), pl.BlockSpec((CHALO, D2), lambda i: (jnp.maximum(i * hb - 1, 0), 0)),
                  _full(CONV_W, D), _full(1, D), _full(1, D), _full(1, D)],
        out_specs=[_row(tm, D), _row(tm, D)],
        out_shape=[jax.ShapeDtypeStruct((T, D), BF16), jax.ShapeDtypeStruct((T, D), F32)],
        scratch_shapes=[pltpu.VMEM((tm + CHALO, D), F32), pltpu.VMEM((SUB - 1, tm + CHALO, D), F32)], name=name,
        compiler_params=_cp("arbitrary"))(u, u, dw_w, dw_b, ln_g, ln_b)


def conv_norm_bwd(ds, cv, ln_g, ln_b, *, name, tm=512):
    T, D = cv.shape
    tm = min(tm, T)

    def body(ds_ref, cv_ref, g_ref, be_ref, dcv_ref, dg_ref, dbe_ref, dcb_ref):
        _acc_init(pl.program_id(0), dg_ref, dbe_ref, dcb_ref)
        cv_ = cv_ref[...]
        mu = _rowmean(cv_)
        xc = cv_ - mu
        rstd = lax.rsqrt(_rowmean(xc * xc) + EPS)
        nh = xc * rstd
        gv = g_ref[...]
        dln = ds_ref[...] * _dsilu(nh * gv + be_ref[...])
        dg_ref[...] += _colsum(dln * nh)
        dbe_ref[...] += _colsum(dln)
        dnh = dln * gv
        dcv = rstd * (dnh - _rowmean(dnh) - nh * _rowmean(dnh * nh))
        dcb_ref[...] += _colsum(dcv)
        dcv_ref[...] = dcv

    return pl.pallas_call(
        body, grid=(T // tm,), in_specs=[_row(tm, D), _row(tm, D), _full(1, D), _full(1, D)],
        out_specs=[_row(tm, D), _full(1, D), _full(1, D), _full(1, D)],
        out_shape=[jax.ShapeDtypeStruct((T, D), F32)] + [jax.ShapeDtypeStruct((1, D), F32)] * 3, name=name,
        compiler_params=_cp("arbitrary"))(ds, cv, ln_g, ln_b)


def conv_glu_bwd(dcv, u, dw_w, *, name, tm=128):
    T, D2 = u.shape
    D = D2 // 2
    tm = min(tm, T)
    nt = T // tm
    hb = tm // CHALO

    def body(dcv_ref, dnext_ref, u_ref, halo_ref, w_ref, du_ref, dw_ref, dbin_ref, gbuf, dbuf, gsh, dsh):
        i = pl.program_id(0)
        _acc_init(i, dw_ref, dbin_ref)
        hv = halo_ref[...]
        gbuf[0:CHALO, :] = jnp.where(i == 0, 0.0, hv[:, 0:D] * _sig(hv[:, D:D2]))
        uv = u_ref[...]
        av = uv[:, 0:D]
        sg = _sig(uv[:, D:D2])
        gbuf[CHALO:CHALO + tm, :] = av * sg
        dbuf[0:tm, :] = dcv_ref[...]
        dbuf[tm:tm + CHALO, :] = jnp.where(i == nt - 1, 0.0, dnext_ref[...])
        _phase_copies(gbuf, gsh, tm)
        _phase_copies(dbuf, dsh, tm)
        for c0 in range(0, D, CCOL):
            dc = dbuf[0:tm, c0:c0 + CCOL]
            acc = jnp.zeros((tm, CCOL), F32)
            for k in range(CONV_W):
                dw_ref[k:k + 1, c0:c0 + CCOL] += _colsum(dc * _shifted(gbuf, gsh, CHALO - (CONV_W - 1) + k, tm, c0))
                acc = acc + w_ref[k:k + 1, c0:c0 + CCOL] * _shifted(dbuf, dsh, CONV_W - 1 - k, tm, c0)
            a_c = av[:, c0:c0 + CCOL]
            s_c = sg[:, c0:c0 + CCOL]
            da = acc * s_c
            dgt = acc * a_c * s_c * (1.0 - s_c)
            dbin_ref[:, c0:c0 + CCOL] += _colsum(da)
            dbin_ref[:, D + c0:D + c0 + CCOL] += _colsum(dgt)
            du_ref[:, c0:c0 + CCOL] = da.astype(BF16)
            du_ref[:, D + c0:D + c0 + CCOL] = dgt.astype(BF16)

    return pl.pallas_call(
        body, grid=(nt,),
        in_specs=[_row(tm, D), pl.BlockSpec((CHALO, D), lambda i: (jnp.minimum((i + 1) * hb, T // CHALO - 1), 0)),
                  _row(tm, D2), pl.BlockSpec((CHALO, D2), lambda i: (jnp.maximum(i * hb - 1, 0), 0)),
                  _full(CONV_W, D)],
        out_specs=[_row(tm, D2), _full(CHALO, D), _full(1, D2)],
        out_shape=[jax.ShapeDtypeStruct((T, D2), BF16), jax.ShapeDtypeStruct((CHALO, D), F32),
                   jax.ShapeDtypeStruct((1, D2), F32)],
        scratch_shapes=[pltpu.VMEM((tm + CHALO, D), F32)] * 2 + [pltpu.VMEM((SUB - 1, tm + CHALO, D), F32)] * 2,
        name=name, compiler_params=_cp("arbitrary"))(dcv, dcv, u, u, dw_w)


HB = 4


def _lb0(lg_ref):
    l0, l1, l2 = lg_ref[0:1, :], lg_ref[1:2, :], lg_ref[2:3, :]
    m = jnp.maximum(jnp.maximum(l0, l1), l2)
    e0 = jnp.exp(l0 - m)
    return e0 / (e0 + jnp.exp(l1 - m) + jnp.exp(l2 - m))


def _mm_exact(m01, x):
    hi = x.astype(BF16)
    r1 = x - hi.astype(F32)
    mid = r1.astype(BF16)
    lo = (r1 - mid.astype(F32)).astype(BF16)
    return _dot(m01, hi) + _dot(m01, mid) + _dot(m01, lo)


def _block_tri(tm):
    r = jnp.arange(tm)[:, None]
    c = jnp.arange(tm)[None, :]
    same = (r // BLK) == (c // BLK)
    return (same & (c <= r)).astype(BF16), (same & (c >= r)).astype(BF16)


def _const_spec(shape):
    return pl.BlockSpec(shape, lambda h, i: (0, 0))


def _hgrn_specs(H, hb, tm, idx):
    g = H // hb
    return [pl.BlockSpec((tm, hb * HEAD), lambda h, i: (idx(i), h)),
            pl.BlockSpec((tm, hb * HEAD), lambda h, i: (idx(i), g + h)),
            pl.BlockSpec((tm, hb * HEAD), lambda h, i: (idx(i), 2 * g + h)),
            pl.BlockSpec((3, hb * HEAD), lambda h, i: (0, h))]


def hgrn_scan(proj, lb_logits, *, name, tm=128):
    T = proj.shape[0]
    H = proj.shape[1] // (4 * HEAD)
    hb = min(HB, H)
    tm = min(tm, T)
    nt = T // tm
    nblk = tm // BLK
    tril, _ = _block_tri(tm)
    heads = [slice(hh * HEAD, (hh + 1) * HEAD) for hh in range(hb)]

    def body(qp_ref, fz_ref, v_ref, lg_ref, tril_ref, o_ref, st_ref, S_ref, q_s, k_s, b_s):
        @pl.when(pl.program_id(1) == 0)
        def _():
            S_ref[...] = jnp.zeros_like(S_ref)

        st_ref[...] = S_ref[...]
        lb = _lb0(lg_ref)
        f = lb + (1.0 - lb) * _sig(fz_ref[...])
        q_s[...] = _silu(qp_ref[...])
        k_s[...] = 1.0 - f
        b_s[...] = _mm_exact(tril_ref[...], jnp.log(f))
        rows = lax.broadcasted_iota(jnp.int32, (BLK, HEAD), 0)
        S = [S_ref[hh] for hh in range(hb)]
        for nb in range(nblk):
            blk = slice(nb * BLK, (nb + 1) * BLK)
            last = slice(nb * BLK + BLK - 1, nb * BLK + BLK)
            qb = [q_s[blk, c] for c in heads]
            bb = [b_s[blk, c] for c in heads]
            o = [_dot_nt((qb[hh] * jnp.exp(bb[hh])).astype(BF16), S[hh].astype(BF16)) for hh in range(hb)]
            for hh, c in enumerate(heads):
                bc = b_s[last, c]
                kd = k_s[blk, c] * jnp.exp(bc - bb[hh])
                S[hh] = S[hh] * jnp.exp(bc) + _dot_tn(v_ref[blk, c].astype(BF16), kd.astype(BF16))
            for s in range(BLK):
                r = slice(nb * BLK + s, nb * BLK + s + 1)
                for hh, c in enumerate(heads):
                    dec = jnp.exp(jnp.where(rows >= s, bb[hh] - b_s[r, c], NEG))
                    a = jnp.sum(qb[hh] * k_s[r, c] * dec, axis=-1, keepdims=True)
                    o[hh] = o[hh] + a * v_ref[r, c]
            for hh, c in enumerate(heads):
                o_ref[blk, c] = o[hh]
        for hh in range(hb):
            S_ref[hh] = S[hh]

    return pl.pallas_call(
        body, grid=(H // hb, nt),
        in_specs=_hgrn_specs(H, hb, tm, lambda i: i) + [_const_spec((tm, tm))],
        out_specs=[pl.BlockSpec((tm, hb * HEAD), lambda h, i: (i, h)),
                   pl.BlockSpec((None, hb, HEAD, HEAD), lambda h, i: (i, h, 0, 0))],
        out_shape=[jax.ShapeDtypeStruct((T, H * HEAD), F32), jax.ShapeDtypeStruct((nt, H, HEAD, HEAD), F32)],
        scratch_shapes=[pltpu.VMEM((hb, HEAD, HEAD), F32)] + [pltpu.VMEM((tm, hb * HEAD), F32)] * 3, name=name,
        compiler_params=_cp("parallel", "arbitrary"))(proj, proj, proj, lb_logits, tril)


def hgrn_scan_bwd(proj, lb_logits, states, do, *, name, tm=128):
    T = proj.shape[0]
    H = proj.shape[1] // (4 * HEAD)
    hb = min(HB, H)
    tm = min(tm, T)
    nt = T // tm
    nblk = tm // BLK
    tril, triu = _block_tri(tm)
    heads = [slice(hh * HEAD, (hh + 1) * HEAD) for hh in range(hb)]

    def body(qp_ref, fz_ref, v_ref, lg_ref, st_ref, do_ref, tril_ref, triu_ref, dqp_ref, dfz_ref, dv_ref, dlb_ref,
             dS_ref, Sb_ref, q_s, k_s, b_s, dq_s, dk_s, dv_s, db_s):
        i = pl.program_id(1)

        @pl.when(i == 0)
        def _():
            dS_ref[...] = jnp.zeros_like(dS_ref)
            dlb_ref[...] = jnp.zeros_like(dlb_ref)

        lb = _lb0(lg_ref)
        qp = qp_ref[...]
        sg = _sig(fz_ref[...])
        f = lb + (1.0 - lb) * sg
        q_s[...] = _silu(qp)
        k_s[...] = 1.0 - f
        b_s[...] = _mm_exact(tril_ref[...], jnp.log(f))
        rows = lax.broadcasted_iota(jnp.int32, (BLK, HEAD), 0)
        rows1 = lax.broadcasted_iota(jnp.int32, (BLK, 1), 0)

        S = [st_ref[hh] for hh in range(hb)]
        for nb in range(nblk):
            blk = slice(nb * BLK, (nb + 1) * BLK)
            last = slice(nb * BLK + BLK - 1, nb * BLK + BLK)
            for hh, c in enumerate(heads):
                Sb_ref[nb * hb + hh] = S[hh]
                if nb < nblk - 1:
                    bc = b_s[last, c]
                    kd = k_s[blk, c] * jnp.exp(bc - b_s[blk, c])
                    S[hh] = S[hh] * jnp.exp(bc) + _dot_tn(v_ref[blk, c].astype(BF16), kd.astype(BF16))

        dS = [dS_ref[hh] for hh in range(hb)]
        for nb in reversed(range(nblk)):
            blk = slice(nb * BLK, (nb + 1) * BLK)
            last = slice(nb * BLK + BLK - 1, nb * BLK + BLK)
            qb, kb, bb, dob, dq, dbc, ebc = [], [], [], [], [], [], []
            for hh, c in enumerate(heads):
                S0 = Sb_ref[nb * hb + hh]
                qb.append(q_s[blk, c])
                kb.append(k_s[blk, c])
                bb.append(b_s[blk, c])
                dob.append(do_ref[blk, c])
                bc = b_s[last, c]
                eb = jnp.exp(bb[hh])
                ekd = jnp.exp(bc - bb[hh])
                ebc.append(jnp.exp(bc))
                dS16 = dS[hh].astype(BF16)
                dob16 = dob[hh].astype(BF16)
                dq.append(_dot(dob16, S0.astype(BF16)) * eb)
                dki = _dot(v_ref[blk, c].astype(BF16), dS16) * ekd
                dk_s[blk, c] = dki
                dv_s[blk, c] = _dot_nt((kb[hh] * ekd).astype(BF16), dS16)
                dbc.append(_colsum(dS[hh] * S0) * ebc[hh] + _colsum(kb[hh] * dki))
                dS[hh] = dS[hh] * ebc[hh] + _dot_tn(dob16, (qb[hh] * eb).astype(BF16))
            for s in range(BLK):
                r = slice(nb * BLK + s, nb * BLK + s + 1)
                for hh, c in enumerate(heads):
                    ks = k_s[r, c]
                    dec = jnp.exp(jnp.where(rows >= s, bb[hh] - b_s[r, c], NEG))
                    w = qb[hh] * dec
                    a = jnp.sum(w * ks, axis=-1, keepdims=True)
                    da = jnp.where(rows1 >= s, jnp.sum(dob[hh] * v_ref[r, c], axis=-1, keepdims=True), 0.0)
                    dq[hh] = dq[hh] + (da * ks) * dec
                    dk_s[r, c] += _colsum(da * w)
                    dv_s[r, c] += _colsum(a * dob[hh])
            for hh, c in enumerate(heads):
                dq_s[blk, c] = dq[hh]
                db_s[blk, c] = qb[hh] * dq[hh] - kb[hh] * dk_s[blk, c]
                db_s[last, c] += dbc[hh]
        for hh in range(hb):
            dS_ref[hh] = dS[hh]

        dlf = _mm_exact(triu_ref[...], db_s[...])
        df = dlf / f - dk_s[...]
        dfz_ref[...] = (df * (1.0 - lb) * sg * (1.0 - sg)).astype(BF16)
        dlb_ref[...] += _colsum(df * (1.0 - sg))
        dqp_ref[...] = (dq_s[...] * _dsilu(qp)).astype(BF16)
        dv_ref[...] = dv_s[...].astype(BF16)

    rev = lambda i: nt - 1 - i
    out_blk = pl.BlockSpec((tm, hb * HEAD), lambda h, i: (rev(i), h))
    return pl.pallas_call(
        body, grid=(H // hb, nt),
        in_specs=_hgrn_specs(H, hb, tm, rev) + [pl.BlockSpec((None, hb, HEAD, HEAD), lambda h, i: (rev(i), h, 0, 0)),
                                                out_blk, _const_spec((tm, tm)), _const_spec((tm, tm))],
        out_specs=[out_blk, out_blk, out_blk, pl.BlockSpec((1, hb * HEAD), lambda h, i: (0, h))],
        out_shape=[jax.ShapeDtypeStruct((T, H * HEAD), BF16)] * 3 + [jax.ShapeDtypeStruct((1, H * HEAD), F32)],
        scratch_shapes=[pltpu.VMEM((hb, HEAD, HEAD), F32), pltpu.VMEM((nblk * hb, HEAD, HEAD), F32)]
        + [pltpu.VMEM((tm, hb * HEAD), F32)] * 7, name=name,
        compiler_params=_cp("parallel", "arbitrary"))(proj, proj, proj, lb_logits, states, do, tril, triu)


def hgrn_gate(o, proj, gn, *, name, tm=512):
    T, D = o.shape
    H = D // HEAD
    tm = min(tm, T)

    def body(o_ref, gp_ref, gn_ref, og_ref):
        gn_ = gn_ref[...]
        for h in range(H):
            c = slice(h * HEAD, (h + 1) * HEAD)
            oh = o_ref[:, c]
            r = lax.rsqrt(_rowmean(oh * oh) + EPS)
            og_ref[:, c] = ((oh * r) * gn_ * _silu(gp_ref[:, c])).astype(BF16)

    return pl.pallas_call(
        body, grid=(T // tm,),
        in_specs=[_row(tm, D), pl.BlockSpec((tm, D), lambda i: (i, 3)), _full(1, HEAD)],
        out_specs=_row(tm, D), out_shape=jax.ShapeDtypeStruct((T, D), BF16), name=name,
        compiler_params=_cp("parallel"))(o, proj, gn)


def hgrn_gate_bwd(dog, o, proj, gn, *, name, tm=512):
    T, D = o.shape
    H = D // HEAD
    tm = min(tm, T)

    def body(dog_ref, o_ref, gp_ref, gn_ref, do_ref, dgp_ref, dgn_ref):
        _acc_init(pl.program_id(0), dgn_ref)
        gn_ = gn_ref[...]
        for h in range(H):
            c = slice(h * HEAD, (h + 1) * HEAD)
            oh = o_ref[:, c]
            gp = gp_ref[:, c]
            dg = dog_ref[:, c]
            r = lax.rsqrt(_rowmean(oh * oh) + EPS)
            on = oh * r
            dgp_ref[:, c] = (dg * (on * gn_) * _dsilu(gp)).astype(BF16)
            don = dg * _silu(gp)
            dgn_ref[...] += _colsum(don * on)
            dn = don * gn_
            do_ref[:, c] = r * (dn - on * _rowmean(dn * on))

    return pl.pallas_call(
        body, grid=(T // tm,),
        in_specs=[_row(tm, D), _row(tm, D), pl.BlockSpec((tm, D), lambda i: (i, 3)), _full(1, HEAD)],
        out_specs=[_row(tm, D), _row(tm, D), _full(1, HEAD)],
        out_shape=[jax.ShapeDtypeStruct((T, D), F32), jax.ShapeDtypeStruct((T, D), BF16),
                   jax.ShapeDtypeStruct((1, HEAD), F32)], name=name,
        compiler_params=_cp("arbitrary"))(dog, o, proj, gn)


def _split2(x):
    hi = x.astype(BF16)
    return hi, (x - hi.astype(F32)).astype(BF16)


def ada_mod(c_all, ada_w, *, name):
    L, D, N = ada_w.shape
    B = c_all.shape[0]

    def body(c_ref, w_ref, o_ref):
        chi, clo = _split2(_silu(c_ref[...]))
        whi, wlo = _split2(w_ref[...])
        o_ref[...] = _dot(chi, whi) + _dot(chi, wlo) + _dot(clo, whi)

    return pl.pallas_call(
        body, grid=(L,), in_specs=[_full(B, D), pl.BlockSpec((None, D, N), lambda l: (l, 0, 0))],
        out_specs=pl.BlockSpec((None, B, N), lambda l: (l, 0, 0)),
        out_shape=jax.ShapeDtypeStruct((L, B, N), F32), name=name, compiler_params=_cp("parallel"))(c_all, ada_w)


def ada_wgrad(c_all_t, dmod, *, name, tr=256):
    D, B = c_all_t.shape
    L, _, N = dmod.shape
    tr = min(tr, D)

    def body(c_ref, d_ref, o_ref):
        cond = _silu(c_ref[...])
        acc = cond[:, 0:1] * d_ref[0:1, :]
        for b in range(1, B):
            acc = acc + cond[:, b:b + 1] * d_ref[b:b + 1, :]
        o_ref[...] = acc

    return pl.pallas_call(
        body, grid=(L, D // tr),
        in_specs=[pl.BlockSpec((tr, B), lambda l, r: (r, 0)), pl.BlockSpec((None, B, N), lambda l, r: (l, 0, 0))],
        out_specs=pl.BlockSpec((None, tr, N), lambda l, r: (l, r, 0)),
        out_shape=jax.ShapeDtypeStruct((L, D, N), F32), name=name,
        compiler_params=_cp("parallel", "parallel"))(c_all_t, dmod)


def sum_devices(parts, *, name):
    n, R, C = parts.shape

    def body(p_ref, o_ref):
        acc = p_ref[0]
        for d in range(1, n):
            acc = acc + p_ref[d]
        o_ref[...] = acc

    return pl.pallas_call(body, in_specs=[VMEM_SPEC], out_specs=VMEM_SPEC,
                          out_shape=jax.ShapeDtypeStruct((R, C), F32), name=name)(parts)


def lb_logits_grad(lb_logits, dlb, *, name):
    def body(lg_ref, d_ref, o_ref):
        l0, l1, l2 = lg_ref[0:1, :], lg_ref[1:2, :], lg_ref[2:3, :]
        m = jnp.maximum(jnp.maximum(l0, l1), l2)
        e0, e1, e2 = jnp.exp(l0 - m), jnp.exp(l1 - m), jnp.exp(l2 - m)
        z = e0 + e1 + e2
        p0, p1, p2 = e0 / z, e1 / z, e2 / z
        g = d_ref[...] * p0
        o_ref[0:1, :] = g * (1.0 - p0)
        o_ref[1:2, :] = -g * p1
        o_ref[2:3, :] = -g * p2

    return pl.pallas_call(body, in_specs=[VMEM_SPEC, VMEM_SPEC], out_specs=VMEM_SPEC,
                          out_shape=jax.ShapeDtypeStruct(lb_logits.shape, F32), name=name)(lb_logits, dlb)


def adamw(w, g, m, v, *, name, tr=256):
    R, C = w.shape
    tr = _tile(R, tr)

    def body(w_ref, g_ref, m_ref, v_ref, d_ref, nm_ref, nv_ref):
        gv = g_ref[...]
        nm = ADAM_B1 * m_ref[...] + (1.0 - ADAM_B1) * gv
        nv = ADAM_B2 * v_ref[...] + (1.0 - ADAM_B2) * (gv * gv)
        m_hat = nm / (1.0 - ADAM_B1 ** ADAM_STEP)
        v_hat = nv / (1.0 - ADAM_B2 ** ADAM_STEP)
        d_ref[...] = -ADAM_LR * (m_hat / (jnp.sqrt(v_hat) + ADAM_EPS) + ADAM_WD * w_ref[...])
        nm_ref[...] = nm
        nv_ref[...] = nv

    spec = pl.BlockSpec((tr, C), lambda i: (i, 0))
    return pl.pallas_call(
        body, grid=(R // tr,), in_specs=[spec] * 4, out_specs=[spec] * 3,
        out_shape=[jax.ShapeDtypeStruct((R, C), F32)] * 3, name=name, compiler_params=_cp("parallel"))(w, g, m, v)


def _place():
    return lax.axis_index("x"), lax.axis_index("y"), lax.axis_index("c")


def _flip(v, bit):
    return 1 - v if bit else v


def allgather_devices(v, *, name):
    R, C = v.shape

    def body(v_ref, out_ref, send_sems, recv_sems, local_sem):
        x, y, c = _place()
        me = 4 * x + 2 * y + c
        mine = pltpu.make_async_copy(v_ref, out_ref.at[me], local_sem)
        mine.start()
        sends = []
        for k in range(1, N_DEV):
            peer = (_flip(x, k & 4), _flip(y, k & 2), _flip(c, k & 1))
            cp = pltpu.make_async_remote_copy(src_ref=v_ref, dst_ref=out_ref.at[me], send_sem=send_sems.at[k - 1],
                                              recv_sem=recv_sems.at[k - 1], device_id=peer, device_id_type=MESH)
            cp.start()
            sends.append(cp)
        for k in range(1, N_DEV):
            px, py, pc = _flip(x, k & 4), _flip(y, k & 2), _flip(c, k & 1)
            pltpu.make_async_remote_copy(src_ref=v_ref, dst_ref=out_ref.at[4 * px + 2 * py + pc],
                                         send_sem=send_sems.at[k - 1], recv_sem=recv_sems.at[k - 1],
                                         device_id=(px, py, pc), device_id_type=MESH).wait_recv()
        for cp in sends:
            cp.wait_send()
        mine.wait()

    return pl.pallas_call(
        body, in_specs=[VMEM_SPEC], out_specs=VMEM_SPEC, out_shape=jax.ShapeDtypeStruct((N_DEV, R, C), v.dtype),
        scratch_shapes=[pltpu.SemaphoreType.DMA((N_DEV - 1,)), pltpu.SemaphoreType.DMA((N_DEV - 1,)),
                        pltpu.SemaphoreType.DMA], name=name)(v)


def _other_chips(x, y):
    return [(1 - x, y), (x, 1 - y), (1 - x, 1 - y)]


def allgather_chips(bufs, *, name):
    n = len(bufs)

    def body(*refs):
        outs = refs[n:2 * n]
        send_sems, recv_sems = refs[2 * n:]
        x, y, c = _place()
        q = 2 * x + y
        chips = _other_chips(x, y)

        def copy(a, k, block, half, to):
            slab = outs[a].at[block, half]
            return pltpu.make_async_remote_copy(src_ref=slab, dst_ref=slab, send_sem=send_sems.at[a, k],
                                                recv_sem=recv_sems.at[a, k], device_id=to, device_id_type=MESH)

        first = [copy(a, j, q, c, (*chips[j], c)) for a in range(n) for j in range(3)]
        for cp in first:
            cp.start()
        passed = []
        for a in range(n):
            for j, (px, py) in enumerate(chips):
                copy(a, j, 2 * px + py, c, (x, y, c)).wait_recv()
                fw = copy(a, 3 + j, 2 * px + py, c, (x, y, 1 - c))
                fw.start()
                passed.append(fw)
        for a in range(n):
            for j, (px, py) in enumerate(chips):
                copy(a, 3 + j, 2 * px + py, 1 - c, (x, y, c)).wait_recv()
        for cp in first + passed:
            cp.wait_send()

    return pl.pallas_call(
        body, in_specs=[HBM] * n, out_specs=[HBM] * n,
        out_shape=[jax.ShapeDtypeStruct(b.shape, b.dtype) for b in bufs],
        input_output_aliases={a: a for a in range(n)},
        scratch_shapes=[pltpu.SemaphoreType.DMA((n, 6)), pltpu.SemaphoreType.DMA((n, 6))], name=name)(*bufs)


SEM = pl.BlockSpec(memory_space=pltpu.SEMAPHORE)
DATAFLOW = pltpu.SideEffectType.DATAFLOW_SIDE_EFFECTING


def _chip_copy(buf, a, j, q, c, chips, send_sems, recv_sems):
    px, py = chips[j]
    return pltpu.make_async_remote_copy(src_ref=buf.at[q, c], dst_ref=buf.at[q, c], send_sem=send_sems.at[3 * a + j],
                                        recv_sem=recv_sems.at[3 * a + j], device_id=(px, py, c), device_id_type=MESH)


def allgather_chips_start(bufs, *, name):
    n = len(bufs)

    def body(*refs):
        send_sems, recv_sems = refs[n], refs[n + 1]
        outs = refs[n + 2:2 * n + 2]
        token = refs[2 * n + 2]
        x, y, c = _place()
        chips = _other_chips(x, y)
        for a in range(n):
            for j in range(3):
                _chip_copy(outs[a], a, j, 2 * x + y, c, chips, send_sems, recv_sems).start()
        token[...] = jnp.zeros_like(token)

    res = pl.pallas_call(
        body, name=name, in_specs=[HBM] * n,
        out_specs=(SEM, SEM, *([HBM] * n), VMEM_SPEC),
        out_shape=(pltpu.SemaphoreType.DMA((3 * n,)), pltpu.SemaphoreType.DMA((3 * n,)),
                   *[pltpu.HBM(b.shape, b.dtype) for b in bufs], jax.ShapeDtypeStruct((SUB, LANE), F32)),
        input_output_aliases={a: a + 2 for a in range(n)},
        compiler_params=pltpu.CompilerParams(has_side_effects=DATAFLOW),
    )(*[pltpu.with_memory_space_constraint(b, pltpu.HBM) for b in bufs])
    return res[0], res[1], list(res[2:2 + n]), res[2 + n]


def allgather_chips_wait(send_sems, recv_sems, bufs, after, *, name):
    n = len(bufs)

    def body(*refs):
        ins = refs[:n]
        send_sems, recv_sems = refs[n], refs[n + 1]
        x, y, c = _place()
        chips = _other_chips(x, y)
        for a in range(n):
            for j, (px, py) in enumerate(chips):
                _chip_copy(ins[a], a, j, 2 * x + y, c, chips, send_sems, recv_sems).wait_send()
                _chip_copy(ins[a], a, j, 2 * px + py, c, chips, send_sems, recv_sems).wait_recv()

    return list(pl.pallas_call(
        body, name=name, in_specs=[HBM] * n + [SEM, SEM, pl.BlockSpec(memory_space=pl.ANY)],
        out_specs=[HBM] * n, out_shape=[pltpu.HBM(b.shape, b.dtype) for b in bufs],
        input_output_aliases={a: a for a in range(n)},
        compiler_params=pltpu.CompilerParams(has_side_effects=DATAFLOW),
    )(*bufs, send_sems, recv_sems, after))


def forward_to_sibling(bufs, *, name):
    n = len(bufs)

    def body(*refs):
        outs = refs[n:2 * n]
        send_sems, recv_sems = refs[2 * n:]
        x, y, c = _place()
        chips = _other_chips(x, y)

        def copy(a, j, half, to):
            px, py = chips[j]
            slab = outs[a].at[2 * px + py, half]
            return pltpu.make_async_remote_copy(src_ref=slab, dst_ref=slab, send_sem=send_sems.at[a, j],
                                                recv_sem=recv_sems.at[a, j], device_id=to, device_id_type=MESH)

        sends = [copy(a, j, c, (x, y, 1 - c)) for a in range(n) for j in range(3)]
        for cp in sends:
            cp.start()
        for a in range(n):
            for j in range(3):
                copy(a, j, 1 - c, (x, y, c)).wait_recv()
        for cp in sends:
            cp.wait_send()

    return pl.pallas_call(
        body, in_specs=[HBM] * n, out_specs=[HBM] * n,
        out_shape=[jax.ShapeDtypeStruct(b.shape, b.dtype) for b in bufs],
        input_output_aliases={a: a for a in range(n)},
        scratch_shapes=[pltpu.SemaphoreType.DMA((n, 3)), pltpu.SemaphoreType.DMA((n, 3))], name=name)(*bufs)


def pair_exchange(grads, *, name):
    n = len(grads)

    def body(*refs):
        ins, outs = refs[:n], refs[n:2 * n]
        send_sems, recv_sems = refs[2 * n:]
        x, y, c = _place()
        cps = [pltpu.make_async_remote_copy(src_ref=ins[a].at[1 - c], dst_ref=outs[a], send_sem=send_sems.at[a],
                                            recv_sem=recv_sems.at[a], device_id=(x, y, 1 - c), device_id_type=MESH)
               for a in range(n)]
        for cp in cps:
            cp.start()
        for cp in cps:
            cp.wait_recv()
        for cp in cps:
            cp.wait_send()

    return pl.pallas_call(
        body, in_specs=[HBM] * n, out_specs=[HBM] * n,
        out_shape=[jax.ShapeDtypeStruct(g.shape[1:], g.dtype) for g in grads],
        scratch_shapes=[pltpu.SemaphoreType.DMA((n,)), pltpu.SemaphoreType.DMA((n,))], name=name)(*grads)


def pair_add(g, other, c_idx, *, name, tr=256):
    _, Q, R, C = g.shape
    tr = _tile(R, tr)

    def body(c_ref, g_ref, o_ref, out_ref):
        out_ref[...] = (g_ref[...] + o_ref[...]).astype(BF16)

    return pl.pallas_call(
        body,
        grid_spec=pltpu.PrefetchScalarGridSpec(
            num_scalar_prefetch=1, grid=(Q, R // tr),
            in_specs=[pl.BlockSpec((None, None, tr, C), lambda q, r, c_ref: (c_ref[0], q, r, 0)),
                      pl.BlockSpec((None, tr, C), lambda q, r, c_ref: (q, r, 0))],
            out_specs=pl.BlockSpec((None, tr, C), lambda q, r, c_ref: (q, r, 0))),
        out_shape=jax.ShapeDtypeStruct((Q, R, C), BF16), name=name,
        compiler_params=_cp("parallel", "parallel"))(c_idx, g, other)


def chip_sum(sums, landed, qc_idx, *, name, tr=256):
    _, R, C = sums.shape
    tr = _tile(R, tr)

    def body(qc_ref, own_ref, l_ref, o_ref):
        acc = own_ref[...].astype(F32)
        for k in range(3):
            acc = acc + l_ref[k].astype(F32)
        o_ref[...] = acc

    return pl.pallas_call(
        body,
        grid_spec=pltpu.PrefetchScalarGridSpec(
            num_scalar_prefetch=1, grid=(R // tr,),
            in_specs=[pl.BlockSpec((None, tr, C), lambda r, qc: (qc[0], r, 0)),
                      pl.BlockSpec((3, tr, C), lambda r, qc: (0, r, 0))],
            out_specs=pl.BlockSpec((None, tr, C), lambda r, qc: (qc[1], r, 0))),
        out_shape=jax.ShapeDtypeStruct((2, R, C), F32), name=name,
        compiler_params=_cp("parallel"))(qc_idx, sums, landed)


def half_swap(bufs, *, name):
    n = len(bufs)

    def body(*refs):
        outs = refs[n:2 * n]
        send_sems, recv_sems = refs[2 * n:]
        x, y, c = _place()
        cps = [pltpu.make_async_remote_copy(src_ref=outs[a].at[c], dst_ref=outs[a].at[c], send_sem=send_sems.at[a],
                                            recv_sem=recv_sems.at[a], device_id=(x, y, 1 - c), device_id_type=MESH)
               for a in range(n)]
        for cp in cps:
            cp.start()
        for a in range(n):
            pltpu.make_async_remote_copy(src_ref=outs[a].at[c], dst_ref=outs[a].at[1 - c], send_sem=send_sems.at[a],
                                         recv_sem=recv_sems.at[a], device_id=(x, y, 1 - c),
                                         device_id_type=MESH).wait_recv()
        for cp in cps:
            cp.wait_send()

    return pl.pallas_call(
        body, in_specs=[HBM] * n, out_specs=[HBM] * n,
        out_shape=[jax.ShapeDtypeStruct(b.shape, b.dtype) for b in bufs],
        input_output_aliases={a: a for a in range(n)},
        scratch_shapes=[pltpu.SemaphoreType.DMA((n,)), pltpu.SemaphoreType.DMA((n,))], name=name)(*bufs)


def _exchange_copy(sums, landed, a, j, c, chips, send_sems, recv_sems):
    px, py = chips[j]
    return pltpu.make_async_remote_copy(src_ref=sums.at[2 * px + py], dst_ref=landed.at[j],
                                        send_sem=send_sems.at[3 * a + j], recv_sem=recv_sems.at[3 * a + j],
                                        device_id=(px, py, c), device_id_type=MESH)


def chip_exchange_start(sums, *, name):
    n = len(sums)
    landing = [lax.empty((3,) + s.shape[1:], s.dtype) for s in sums]

    def body(*refs):
        send_sems, recv_sems = refs[2 * n], refs[2 * n + 1]
        src, dst = refs[2 * n + 2:3 * n + 2], refs[3 * n + 2:4 * n + 2]
        token = refs[4 * n + 2]
        x, y, c = _place()
        chips = _other_chips(x, y)
        for a in range(n):
            for j in range(3):
                _exchange_copy(src[a], dst[a], a, j, c, chips, send_sems, recv_sems).start()
        token[...] = jnp.zeros_like(token)

    res = pl.pallas_call(
        body, name=name, in_specs=[HBM] * (2 * n),
        out_specs=(SEM, SEM, *([HBM] * (2 * n)), VMEM_SPEC),
        out_shape=(pltpu.SemaphoreType.DMA((3 * n,)), pltpu.SemaphoreType.DMA((3 * n,)),
                   *[pltpu.HBM(b.shape, b.dtype) for b in sums + landing], jax.ShapeDtypeStruct((SUB, LANE), F32)),
        input_output_aliases={a: a + 2 for a in range(2 * n)},
        compiler_params=pltpu.CompilerParams(has_side_effects=DATAFLOW),
    )(*[pltpu.with_memory_space_constraint(b, pltpu.HBM) for b in sums + landing])
    return res[0], res[1], list(res[2:2 + n]), list(res[2 + n:2 + 2 * n]), res[2 + 2 * n]


def chip_exchange_wait(send_sems, recv_sems, sums, landed, after, *, name):
    n = len(sums)

    def body(*refs):
        src, dst = refs[:n], refs[n:2 * n]
        send_sems, recv_sems = refs[2 * n], refs[2 * n + 1]
        x, y, c = _place()
        chips = _other_chips(x, y)
        for a in range(n):
            for j in range(3):
                cp = _exchange_copy(src[a], dst[a], a, j, c, chips, send_sems, recv_sems)
                cp.wait_send()
                cp.wait_recv()

    res = pl.pallas_call(
        body, name=name, in_specs=[HBM] * (2 * n) + [SEM, SEM, pl.BlockSpec(memory_space=pl.ANY)],
        out_specs=[HBM] * (2 * n), out_shape=[pltpu.HBM(b.shape, b.dtype) for b in sums + landed],
        input_output_aliases={a: a for a in range(2 * n)},
        compiler_params=pltpu.CompilerParams(has_side_effects=DATAFLOW),
    )(*sums, *landed, send_sems, recv_sems, after)
    return list(res[:n]), list(res[n:])


def pair_reduce(grads, c, tag):
    c_idx = c.astype(jnp.int32).reshape(1)
    others = pair_exchange(grads, name=f"grad_pair_exchange_{tag}")
    return [pair_add(g, o, c_idx, name=f"grad_pair_add_{tag}{a}") for a, (g, o) in enumerate(zip(grads, others))]


def finish_reduce(sums, landed, q, c, tag):
    qc_idx = jnp.stack([q, c]).astype(jnp.int32)
    return [chip_sum(s, l, qc_idx, name=f"grad_chip_sum_{tag}{a}") for a, (s, l) in enumerate(zip(sums, landed))]


def _ffn_forward(x, mod, pre_g, post_g, w_up, w_down, dw_w, dw_b, tag):
    sh, sc, gate = mod
    h = prenorm(x, pre_g, sc, sh, name=f"{tag}_prenorm")
    u0 = mm_nn(h, w_up, name=f"{tag}_up", out_dtype=BF16, perm=_ffn_perm)
    z, ab = ffn_act(u0, dw_w, dw_b, name=f"{tag}_act")
    y = mm_nn(z, w_down, name=f"{tag}_down")
    x_new = post_residual(x, y, post_g, gate, name=f"{tag}_post")
    return x_new, (x, h, u0, ab, z, y)


def _ffn_backward(dx, saved, mod, pre_g, post_g, w_up, w_down, dw_w, dw_b, tag):
    x, h, u0, ab, z, y = saved
    sh, sc, gate = mod
    dy, dgate, dpost, _ = post_bwd(dx, y, post_g, gate, name=f"{tag}_post_bwd")
    dz = mm_nt(dy, w_down, name=f"{tag}_down_dx", out_dtype=BF16)
    g_down = mm_tn(z, dy, name=f"{tag}_down_dw", J=2, block="a", row_chips=2)
    du0, dconv = ffn_act_bwd(dz, u0, ab, dw_w, name=f"{tag}_act_bwd")
    dh = mm_nt(du0, w_up, name=f"{tag}_up_dx", perm=_ffn_perm)
    g_up = mm_tn(h, du0, name=f"{tag}_up_dw", J=4, block="b", perm=_ffn_perm)
    dx_in, dsh, dsc, dpre = prenorm_bwd(dh, x, dx, pre_g, sc, name=f"{tag}_prenorm_bwd")
    nb = u0.shape[1] // 4
    dconv = dconv[:, 0].reshape(4, 2, 2, nb).transpose(0, 2, 1, 3).reshape(4, 4 * nb)
    return dx_in, dict(dsh=dsh, dsc=dsc, dgate=dgate, dpre=dpre, dpost=dpost, g_up=g_up, g_down=g_down,
                       d_dw_w=dconv[0:FFN_W], d_dw_b=dconv[3:4])


def _local_step(x, tgt, mods, P, late_weights=None, grads_ready=None):
    m0, m1 = mods
    h1 = prenorm(x, P["pre_mix_g"][0:1], m0[1], m0[0], name="hgrn_prenorm")
    proj = mm_nn(h1, P["hgrn_w_in"], name="hgrn_in")
    o, states = hgrn_scan(proj, P["hgrn_lb_logits"], name="hgrn_scan")
    og = hgrn_gate(o, proj, P["hgrn_gnorm_g"], name="hgrn_gate")
    y1 = mm_nn(og, P["hgrn_w_out"], name="hgrn_out")
    x1 = post_residual(x, y1, P["post_mix_g"][0:1], m0[2], name="hgrn_post")
    if late_weights is not None:
        P = {**P, **late_weights(x1)}
    x2, ffn0 = _ffn_forward(x1, m0[3:6], P["pre_ffn_g"][0:1], P["post_ffn_g"][0:1], P["ffn_w_up"][0],
                            P["ffn_w_down"][0], P["ffn_dw_w"][0], P["ffn_dw_b"][0:1], "ffn0")
    h3 = prenorm(x2, P["pre_mix_g"][1:2], m1[1], m1[0], name="conv_prenorm")
    u = mm_nn(h3, P["conv_w_in"], name="conv_in", bias=P["conv_b_in"])
    s, cv = conv_act(u, P["conv_dw_w"], P["conv_dw_b"], P["conv_ln_g"], P["conv_ln_b"], name="conv_act")
    y3 = mm_nn(s, P["conv_w_out"], name="conv_out", bias=P["conv_b_out"])
    x3 = post_residual(x2, y3, P["post_mix_g"][1:2], m1[2], name="conv_post")
    x4, ffn1 = _ffn_forward(x3, m1[3:6], P["pre_ffn_g"][1:2], P["post_ffn_g"][1:2], P["ffn_w_up"][1],
                            P["ffn_w_down"][1], P["ffn_dw_w"][1], P["ffn_dw_b"][1:2], "ffn1")
    dx4, lcols = loss_grad(x4, tgt, name="loss")
    dx3, f1 = _ffn_backward(dx4, ffn1, m1[3:6], P["pre_ffn_g"][1:2], P["post_ffn_g"][1:2], P["ffn_w_up"][1],
                            P["ffn_w_down"][1], P["ffn_dw_w"][1], P["ffn_dw_b"][1:2], "ffn1")
    dy3, dg1_1, dpostmix1, d_b_out = post_bwd(dx3, y3, P["post_mix_g"][1:2], m1[2], name="conv_post_bwd")
    ds = mm_nt(dy3, P["conv_w_out"], name="conv_out_dx")
    g_conv_out = mm_tn(s, dy3, name="conv_out_dw", J=1, block="a", row_chips=4)
    dcv, d_ln_g, d_ln_b, d_dw_b = conv_norm_bwd(ds, cv, P["conv_ln_g"], P["conv_ln_b"], name="conv_norm_bwd")
    du, d_dw_w, d_b_in = conv_glu_bwd(dcv, u, P["conv_dw_w"], name="conv_glu_bwd")
    dh3 = mm_nt(du, P["conv_w_in"], name="conv_in_dx")
    g_conv_in = mm_tn(h3, du, name="conv_in_dw", J=2, block="b", col_chips=2)
    dx2, dsh1_1, dsc1_1, dpremix1 = prenorm_bwd(dh3, x2, dx3, P["pre_mix_g"][1:2], m1[1], name="conv_prenorm_bwd")
    if grads_ready is not None:
        token = grads_ready("l1", [g_conv_in, g_conv_out, f1["g_up"], f1["g_down"]])
        m0 = tuple(m + token[0:1, 0:1] for m in m0)
    dx1, f0 = _ffn_backward(dx2, ffn0, m0[3:6], P["pre_ffn_g"][0:1], P["post_ffn_g"][0:1], P["ffn_w_up"][0],
                            P["ffn_w_down"][0], P["ffn_dw_w"][0], P["ffn_dw_b"][0:1], "ffn0")
    if grads_ready is not None:
        token = grads_ready("f0", [f0["g_up"], f0["g_down"]])
        m0 = tuple(m + token[0:1, 0:1] for m in m0)
    dy1, dg1_0, dpostmix0, _ = post_bwd(dx1, y1, P["post_mix_g"][0:1], m0[2], name="hgrn_post_bwd")
    dog = mm_nt(dy1, P["hgrn_w_out"], name="hgrn_out_dx")
    g_hgrn_out = mm_tn(og, dy1, name="hgrn_out_dw", J=1, block="a", row_chips=4)
    do, dgp, d_gn = hgrn_gate_bwd(dog, o, proj, P["hgrn_gnorm_g"], name="hgrn_gate_bwd")
    dqp, dfz, dv, dlb = hgrn_scan_bwd(proj, P["hgrn_lb_logits"], states, do, name="hgrn_scan_bwd")
    dproj = jnp.concatenate([dqp, dfz, dv, dgp], axis=1)
    dh1 = mm_nt(dproj, P["hgrn_w_in"], name="hgrn_in_dx")
    g_hgrn_in = mm_tn(h1, dproj, name="hgrn_in_dw", J=4, block="b")
    dx0, dsh1_0, dsc1_0, dpremix0 = prenorm_bwd(dh1, x, dx1, P["pre_mix_g"][0:1], m0[1], name="hgrn_prenorm_bwd")

    dmod = jnp.stack([
        jnp.concatenate([dsh1_0, dsc1_0, dg1_0, f0["dsh"], f0["dsc"], f0["dgate"]], axis=1)[0],
        jnp.concatenate([dsh1_1, dsc1_1, dg1_1, f1["dsh"], f1["dsc"], f1["dgate"]], axis=1)[0]])
    small = dict(
        loss=lcols,
        pre_mix_g=jnp.concatenate([dpremix0, dpremix1]), post_mix_g=jnp.concatenate([dpostmix0, dpostmix1]),
        pre_ffn_g=jnp.concatenate([f0["dpre"], f1["dpre"]]), post_ffn_g=jnp.concatenate([f0["dpost"], f1["dpost"]]),
        lb=dlb, hgrn_gnorm_g=d_gn, ffn_dw_b=jnp.concatenate([f0["d_dw_b"], f1["d_dw_b"]]), dmod=dmod,
        conv_b_in=d_b_in, conv_dw_w=d_dw_w[0:CONV_W], conv_dw_b=d_dw_b, conv_ln_g=d_ln_g, conv_ln_b=d_ln_b,
        conv_b_out=d_b_out, ffn_dw_w=jnp.stack([f0["d_dw_w"], f1["d_dw_w"]]))
    big = [g_hgrn_in, g_hgrn_out, g_conv_in, g_conv_out, f0["g_up"], f1["g_up"], f0["g_down"], f1["g_down"]]
    return dx0, small, big


def _pack(parts, rows=8):
    flat = jnp.concatenate([p.reshape(-1).astype(F32) for p in parts])
    per = rows * 128
    pad = (-flat.shape[0]) % per
    return jnp.pad(flat, (0, pad)).reshape(rows, -1)


def _unpack(flat, shapes):
    out, off = [], 0
    for s in shapes:
        n = 1
        for d in s:
            n *= d
        out.append(flat[..., off:off + n].reshape(flat.shape[:-1] + tuple(s)))
        off += n
    return out


def _from_chips(stacked, axis):
    moved = jnp.moveaxis(stacked, 0, axis)
    shape = list(moved.shape)
    return moved.reshape(shape[:axis] + [shape[axis] * shape[axis + 1]] + shape[axis + 2:])


def _my_shard(full, axis, q):
    n = full.shape[axis] // N_CHIPS
    return lax.dynamic_slice_in_dim(full, q * n, n, axis=axis)


def kernel(x, c, ada_w, ada_b, pre_mix_g, post_mix_g, pre_ffn_g, post_ffn_g, hgrn_w_in, hgrn_lb_logits, hgrn_gnorm_g, hgrn_w_out, conv_w_in, conv_b_in, conv_dw_w, conv_dw_b, conv_ln_g, conv_ln_b, conv_w_out, conv_b_out, ffn_w_up, ffn_dw_w, ffn_dw_b, ffn_w_down, loss_target, m_ada_w, m_ada_b, m_pre_mix_g, m_post_mix_g, m_pre_ffn_g, m_post_ffn_g, m_hgrn_w_in, m_hgrn_lb_logits, m_hgrn_gnorm_g, m_hgrn_w_out, m_conv_w_in, m_conv_b_in, m_conv_dw_w, m_conv_dw_b, m_conv_ln_g, m_conv_ln_b, m_conv_w_out, m_conv_b_out, m_ffn_w_up, m_ffn_dw_w, m_ffn_dw_b, m_ffn_w_down, v_ada_w, v_ada_b, v_pre_mix_g, v_post_mix_g, v_pre_ffn_g, v_post_ffn_g, v_hgrn_w_in, v_hgrn_lb_logits, v_hgrn_gnorm_g, v_hgrn_w_out, v_conv_w_in, v_conv_b_in, v_conv_dw_w, v_conv_dw_b, v_conv_ln_g, v_conv_ln_b, v_conv_w_out, v_conv_b_out, v_ffn_w_up, v_ffn_dw_w, v_ffn_dw_b, v_ffn_w_down):
    W = dict(ada_w=ada_w, ada_b=ada_b, pre_mix_g=pre_mix_g, post_mix_g=post_mix_g, pre_ffn_g=pre_ffn_g,
             post_ffn_g=post_ffn_g, hgrn_w_in=hgrn_w_in, hgrn_lb_logits=hgrn_lb_logits, hgrn_gnorm_g=hgrn_gnorm_g,
             hgrn_w_out=hgrn_w_out, conv_w_in=conv_w_in, conv_b_in=conv_b_in, conv_dw_w=conv_dw_w,
             conv_dw_b=conv_dw_b, conv_ln_g=conv_ln_g, conv_ln_b=conv_ln_b, conv_w_out=conv_w_out,
             conv_b_out=conv_b_out, ffn_w_up=ffn_w_up, ffn_dw_w=ffn_dw_w, ffn_dw_b=ffn_dw_b, ffn_w_down=ffn_w_down)
    M = dict(ada_w=m_ada_w, ada_b=m_ada_b, pre_mix_g=m_pre_mix_g, post_mix_g=m_post_mix_g, pre_ffn_g=m_pre_ffn_g,
             post_ffn_g=m_post_ffn_g, hgrn_w_in=m_hgrn_w_in, hgrn_lb_logits=m_hgrn_lb_logits,
             hgrn_gnorm_g=m_hgrn_gnorm_g, hgrn_w_out=m_hgrn_w_out, conv_w_in=m_conv_w_in, conv_b_in=m_conv_b_in,
             conv_dw_w=m_conv_dw_w, conv_dw_b=m_conv_dw_b, conv_ln_g=m_conv_ln_g, conv_ln_b=m_conv_ln_b,
             conv_w_out=m_conv_w_out, conv_b_out=m_conv_b_out, ffn_w_up=m_ffn_w_up, ffn_dw_w=m_ffn_dw_w,
             ffn_dw_b=m_ffn_dw_b, ffn_w_down=m_ffn_w_down)
    V = dict(ada_w=v_ada_w, ada_b=v_ada_b, pre_mix_g=v_pre_mix_g, post_mix_g=v_post_mix_g, pre_ffn_g=v_pre_ffn_g,
             post_ffn_g=v_post_ffn_g, hgrn_w_in=v_hgrn_w_in, hgrn_lb_logits=v_hgrn_lb_logits,
             hgrn_gnorm_g=v_hgrn_gnorm_g, hgrn_w_out=v_hgrn_w_out, conv_w_in=v_conv_w_in, conv_b_in=v_conv_b_in,
             conv_dw_w=v_conv_dw_w, conv_dw_b=v_conv_dw_b, conv_ln_g=v_conv_ln_g, conv_ln_b=v_conv_ln_b,
             conv_w_out=v_conv_w_out, conv_b_out=v_conv_b_out, ffn_w_up=v_ffn_w_up, ffn_dw_w=v_ffn_dw_w,
             ffn_dw_b=v_ffn_dw_b, ffn_w_down=v_ffn_w_down)
    names = list(W)
    xi, yi, ci = lax.axis_index("x"), lax.axis_index("y"), lax.axis_index("c")
    q = 2 * xi + yi
    me = 2 * q + ci
    D = x.shape[-1]
    L = ada_w.shape[0]

    small_w = ["conv_b_in", "conv_dw_w", "conv_dw_b", "conv_ln_g", "conv_ln_b", "conv_b_out", "ffn_dw_w"]
    small_axis = dict(conv_b_in=1, conv_dw_w=2, conv_dw_b=1, conv_ln_g=1, conv_ln_b=1, conv_b_out=1, ffn_dw_w=2)
    packed = _pack([c] + [W[n] for n in small_w])
    gathered = allgather_devices(packed, name="gather_small_params").reshape(N_DEV, -1)
    c_all = gathered[:, 0:D]
    per_chip = gathered.reshape(N_CHIPS, 2, -1)[:, 0, D:]
    parts = _unpack(per_chip, [W[n].shape for n in small_w])
    P = {n: _from_chips(p, small_axis[n]) for n, p in zip(small_w, parts)}
    P["conv_dw_w"] = P["conv_dw_w"][0]
    for n in ("pre_mix_g", "post_mix_g", "pre_ffn_g", "post_ffn_g", "hgrn_lb_logits", "hgrn_gnorm_g", "ffn_dw_b"):
        P[n] = W[n]

    modp = ada_mod(c_all, ada_w, name="ada_mod")
    ncol = modp.shape[-1]
    mod_all = allgather_devices(modp.reshape(L * N_DEV, ncol), name="gather_mod")
    mod_all = mod_all.reshape(N_CHIPS, 2, L, N_DEV, ncol)[:, 0]
    mod_me = lax.dynamic_index_in_dim(mod_all, me, axis=2, keepdims=False)
    mod = mod_me.transpose(1, 0, 2).reshape(L, N_CHIPS * ncol) + ada_b
    mods = [tuple(mod[l:l + 1, k * D:(k + 1) * D] for k in range(6)) for l in range(L)]

    def halves(w):
        shard = w.astype(BF16).reshape(1, 2, w.shape[0] // 2, w.shape[1])
        buf = lax.empty((N_CHIPS,) + shard.shape[1:], BF16)
        return lax.dynamic_update_slice_in_dim(buf, shard, q, axis=0)

    stack = lambda t: t.reshape(N_CHIPS, t.shape[1] * t.shape[2], t.shape[3])
    rowsh = lambda t: t.reshape(1, N_CHIPS * t.shape[1] * t.shape[2], t.shape[3])
    pairs = lambda t: t.reshape(2, 2, t.shape[1], t.shape[2]).transpose(0, 2, 1, 3).reshape(2, t.shape[1], 2 * t.shape[2])
    g = allgather_chips([halves(hgrn_w_in[0]), halves(hgrn_w_out[0])], name="gather_hgrn_weights")
    P["hgrn_w_in"], P["hgrn_w_out"] = stack(g[0]), rowsh(g[1])
    late_shards = [conv_w_in[0], conv_w_out[0], ffn_w_up[0], ffn_w_up[1], ffn_w_down[0], ffn_w_down[1]]
    late_bufs, _, _ = lax.optimization_barrier(([halves(w) for w in late_shards], g, mod))
    send_sems, recv_sems, bufs, token = allgather_chips_start(late_bufs, name="gather_weights_start")
    mods[0] = tuple(m + token[0:1, 0:1] for m in mods[0])

    def late_weights(x1):
        landed = allgather_chips_wait(send_sems, recv_sems, bufs, x1, name="gather_weights_wait")
        g = forward_to_sibling(landed, name="gather_weights_forward")
        return dict(conv_w_in=pairs(stack(g[0])), conv_w_out=rowsh(g[1]), ffn_w_up=[stack(g[2]), stack(g[3])],
                    ffn_w_down=[rowsh(g[4]), rowsh(g[5])])

    in_flight = {}

    def grads_ready(tag, grads):
        sums = pair_reduce(grads, ci, f"{tag}_")
        send, recv, sums, landing, tok = chip_exchange_start(sums, name=f"grad_chip_exchange_start_{tag}")
        in_flight[tag] = (send, recv, sums, landing)
        return tok

    grad_x, small, big = _local_step(x[0], loss_target[0], mods, P, late_weights, grads_ready)

    small_names = list(small)
    gs = allgather_devices(_pack([small[n] for n in small_names]), name="gather_small_grads")
    dmod_all = _unpack(gs.reshape(N_DEV, -1), [small[n].shape for n in small_names])[small_names.index("dmod")]
    tot = sum_devices(gs, name="sum_small_grads").reshape(1, -1)
    S = dict(zip(small_names, _unpack(tot, [small[n].shape for n in small_names])))
    S = {n: v[0] for n, v in S.items()}
    loss = 0.5 * jnp.sum(S["loss"]) / D

    G = {}
    dmod_q = lax.dynamic_slice_in_dim(dmod_all, q * ncol, ncol, axis=2)
    G["ada_w"] = ada_wgrad(c_all.T, dmod_q.transpose(1, 0, 2), name="ada_wgrad")
    G["ada_b"] = S["dmod"]
    for n in ("pre_mix_g", "post_mix_g", "pre_ffn_g", "post_ffn_g", "hgrn_gnorm_g", "ffn_dw_b"):
        G[n] = S[n]
    G["hgrn_lb_logits"] = lb_logits_grad(hgrn_lb_logits, S["lb"], name="lb_logits_grad")
    G["conv_b_in"] = _my_shard(S["conv_b_in"], 1, q)
    G["conv_dw_w"] = _my_shard(S["conv_dw_w"], 1, q)[None]
    for n in ("conv_dw_b", "conv_ln_g", "conv_ln_b", "conv_b_out"):
        G[n] = _my_shard(S[n], 1, q)
    G["ffn_dw_w"] = _my_shard(S["ffn_dw_w"], 2, q)

    sums_h = pair_reduce([big[0], big[1]], ci, "hg_")
    send_h, recv_h, sums_h, landing_h, token = chip_exchange_start(sums_h, name="grad_chip_exchange_start_hg")
    halves = []
    for tag in ("f0", "l1"):
        sums_t, landed_t = chip_exchange_wait(*in_flight[tag], grad_x, name=f"grad_chip_exchange_wait_{tag}")
        halves += finish_reduce(sums_t, landed_t, q, ci, f"{tag}_")
    halves, _ = lax.optimization_barrier((halves, token))
    red = [f.reshape(2 * f.shape[1], f.shape[2]) for f in half_swap(halves, name="grad_half_swap")]
    G["conv_w_in"], G["conv_w_out"] = red[2][None], red[3][None]
    G["ffn_w_up"] = jnp.stack([red[0], red[4]])
    G["ffn_w_down"] = jnp.stack([red[1], red[5]])

    delta, new_m, new_v = {}, {}, {}

    def adamw_matrix(n):
        shp = W[n].shape
        two = lambda t: t.reshape(-1, shp[-1])
        d_, m_, v_ = adamw(two(W[n]), two(G[n]), two(M[n]), two(V[n]), name=f"adamw_{n}")
        delta[n], new_m[n], new_v[n] = d_.reshape(shp), m_.reshape(shp), v_.reshape(shp)

    big_names = ["ada_w", "hgrn_w_in", "hgrn_w_out", "conv_w_in", "conv_w_out", "ffn_w_up", "ffn_w_down"]
    for n in ("ada_w", "conv_w_in", "conv_w_out", "ffn_w_up", "ffn_w_down"):
        adamw_matrix(n)
    sums_h, landed_h = chip_exchange_wait(send_h, recv_h, sums_h, landing_h, delta["ffn_w_down"],
                                          name="grad_chip_exchange_wait_hg")
    red_h = half_swap(finish_reduce(sums_h, landed_h, q, ci, "hg_"), name="grad_half_swap_hg")
    G["hgrn_w_in"], G["hgrn_w_out"] = [f.reshape(1, 2 * f.shape[1], f.shape[2]) for f in red_h]
    for n in ("hgrn_w_in", "hgrn_w_out"):
        adamw_matrix(n)
    rest = [n for n in names if n not in big_names]
    d_, m_, v_ = adamw(_pack([W[n] for n in rest]), _pack([G[n] for n in rest]), _pack([M[n] for n in rest]),
                       _pack([V[n] for n in rest]), name="adamw_small")
    shapes = [W[n].shape for n in rest]
    for n, a, b_, c_ in zip(rest, _unpack(d_.reshape(-1), shapes), _unpack(m_.reshape(-1), shapes),
                            _unpack(v_.reshape(-1), shapes)):
        delta[n], new_m[n], new_v[n] = a, b_, c_

    return (loss, grad_x[None], *[G[n].reshape(W[n].shape) for n in names], *[delta[n] for n in names],
            *[new_m[n] for n in names], *[new_v[n] for n in names])
```

```python
import jax
import jax.numpy as jnp
from jax import lax
from jax.experimental import pallas as pl
from jax.experimental.pallas import tpu as pltpu

F32 = jnp.float32
BF16 = jnp.bfloat16
EPS = 1e-6
HEAD = 128
BLK = 16
NEG = -1e30
CONV_W = 31
FFN_W = 3
N_CHIPS = 4
N_DEV = 8
SUB = 8
LANE = 128
V7X_VMEM_LIMIT = 56 * 1024 * 1024
MESH = pl.DeviceIdType.MESH
HBM = pl.BlockSpec(memory_space=pltpu.HBM)
VMEM_SPEC = pl.BlockSpec(memory_space=pltpu.VMEM)

ADAM_LR = 0.001
ADAM_B1 = 0.9
ADAM_B2 = 0.999
ADAM_EPS = 1e-08
ADAM_WD = 0.01
ADAM_STEP = 10


def _cp(*sem):
    return pltpu.CompilerParams(dimension_semantics=sem, vmem_limit_bytes=V7X_VMEM_LIMIT)


def _sig(x):
    return 0.5 * jnp.tanh(0.5 * x) + 0.5


def _silu(x):
    return x * _sig(x)


def _dsilu(x):
    s = _sig(x)
    return s * (1.0 + x * (1.0 - s))


def _dot(a, b):
    return jnp.dot(a, b, preferred_element_type=F32)


def _dot_nt(a, b):
    return lax.dot_general(a, b, (((1,), (1,)), ((), ())), preferred_element_type=F32)


def _dot_tn(a, b):
    return lax.dot_general(a, b, (((0,), (0,)), ((), ())), preferred_element_type=F32)


def _colsum(x):
    return jnp.sum(x, axis=0, keepdims=True)


def _rowmean(x):
    return jnp.mean(x, axis=-1, keepdims=True)


def _ffn_perm(j):
    return (j % 2) * 2 + j // 2


def _tile(n, pref):
    if n <= pref:
        return n
    t = pref - pref % 8
    while n % t:
        t -= 8
    return t


def mm_nn(a, w, *, name, bias=None, out_dtype=F32, perm=None, tm=1024):
    T, K = a.shape
    J, _, nb = w.shape
    tm = min(tm, T)
    col = (lambda j: j) if perm is None else perm

    def body(a_ref, w_ref, *rest):
        acc = _dot(a_ref[...], w_ref[...])
        if bias is not None:
            acc = acc + rest[0][...]
        rest[-1][...] = acc.astype(out_dtype)

    in_specs = [pl.BlockSpec((tm, K), lambda j, i: (i, 0)), pl.BlockSpec((None, K, nb), lambda j, i: (j, 0, 0))]
    args = [a, w]
    if bias is not None:
        in_specs.append(pl.BlockSpec((1, nb), lambda j, i: (0, j)))
        args.append(bias)
    return pl.pallas_call(
        body, grid=(J, T // tm), in_specs=in_specs,
        out_specs=pl.BlockSpec((tm, nb), lambda j, i: (i, col(j))),
        out_shape=jax.ShapeDtypeStruct((T, J * nb), out_dtype), name=name,
        compiler_params=_cp("parallel", "parallel"))(*args)


def mm_nt(a, w, *, name, out_dtype=F32, perm=None, tm=1024):
    T = a.shape[0]
    J, K, nb = w.shape
    tm = min(tm, T)
    col = (lambda j: j) if perm is None else perm

    def body(a_ref, w_ref, o_ref, acc_ref):
        j = pl.program_id(1)

        @pl.when(j == 0)
        def _():
            acc_ref[...] = jnp.zeros_like(acc_ref)

        acc_ref[...] += _dot_nt(a_ref[...], w_ref[...])

        @pl.when(j == J - 1)
        def _():
            o_ref[...] = acc_ref[...].astype(out_dtype)

    return pl.pallas_call(
        body, grid=(T // tm, J),
        in_specs=[pl.BlockSpec((tm, nb), lambda i, j: (i, col(j))), pl.BlockSpec((None, K, nb), lambda i, j: (j, 0, 0))],
        out_specs=pl.BlockSpec((tm, K), lambda i, j: (i, 0)),
        out_shape=jax.ShapeDtypeStruct((T, K), out_dtype),
        scratch_shapes=[pltpu.VMEM((tm, K), F32)], name=name,
        compiler_params=_cp("parallel", "arbitrary"))(a, w)


def mm_tn(a, b, *, name, J, block, row_chips=1, col_chips=1, perm=None, tk=1024):
    T = a.shape[0]
    tk = min(tk, T)
    col = (lambda j: j) if perm is None else perm
    if block == "b":
        rows, nb = a.shape[1], b.shape[1] // J
        a_spec = pl.BlockSpec((tk, rows), lambda j, t: (t, 0))
        b_spec = pl.BlockSpec((tk, nb), lambda j, t: (t, col(j)))
    else:
        rows, nb = a.shape[1] // J, b.shape[1]
        a_spec = pl.BlockSpec((tk, rows), lambda j, t: (t, col(j)))
        b_spec = pl.BlockSpec((tk, nb), lambda j, t: (t, 0))
    rh = rows // (2 * row_chips)
    nc = nb // col_chips
    chips = [(rc, cc) for rc in range(row_chips) for cc in range(col_chips)]

    def body(a_ref, b_ref, o_ref):
        @pl.when(pl.program_id(1) == 0)
        def _():
            o_ref[...] = jnp.zeros_like(o_ref)

        acc = _dot_tn(a_ref[...], b_ref[...])
        for ch, (rc, cc) in enumerate(chips):
            for hf in range(2):
                r0 = (rc * 2 + hf) * rh
                o_ref[hf, ch] += acc[r0:r0 + rh, cc * nc:(cc + 1) * nc]

    return pl.pallas_call(
        body, grid=(J, T // tk), in_specs=[a_spec, b_spec],
        out_specs=pl.BlockSpec((2, len(chips), rh, nc), lambda j, t: (0, j, 0, 0)),
        out_shape=jax.ShapeDtypeStruct((2, J * len(chips), rh, nc), F32), name=name,
        compiler_params=_cp("parallel", "arbitrary"))(a, b)


def _row(tm, w):
    return pl.BlockSpec((tm, w), lambda i: (i, 0))


def _full(r, w):
    return pl.BlockSpec((r, w), lambda i: (0, 0))


def _acc_init(i, *refs):
    @pl.when(i == 0)
    def _():
        for r in refs:
            r[...] = jnp.zeros_like(r)


def prenorm(x, g, sc, sh, *, name, tm=512):
    T, D = x.shape
    tm = min(tm, T)

    def body(x_ref, g_ref, sc_ref, sh_ref, h_ref):
        xv = x_ref[...]
        r = lax.rsqrt(_rowmean(xv * xv) + EPS)
        h_ref[...] = ((xv * r) * g_ref[...] * (1.0 + sc_ref[...]) + sh_ref[...]).astype(BF16)

    return pl.pallas_call(
        body, grid=(T // tm,), in_specs=[_row(tm, D), _full(1, D), _full(1, D), _full(1, D)],
        out_specs=_row(tm, D), out_shape=jax.ShapeDtypeStruct((T, D), BF16), name=name,
        compiler_params=_cp("parallel"))(x, g, sc, sh)


def post_residual(x, y, g, gate, *, name, tm=512):
    T, D = x.shape
    tm = min(tm, T)

    def body(x_ref, y_ref, g_ref, gate_ref, o_ref):
        yv = y_ref[...]
        r = lax.rsqrt(_rowmean(yv * yv) + EPS)
        o_ref[...] = x_ref[...] + gate_ref[...] * ((yv * r) * g_ref[...])

    return pl.pallas_call(
        body, grid=(T // tm,), in_specs=[_row(tm, D), _row(tm, D), _full(1, D), _full(1, D)],
        out_specs=_row(tm, D), out_shape=jax.ShapeDtypeStruct((T, D), F32), name=name,
        compiler_params=_cp("parallel"))(x, y, g, gate)


def loss_grad(x, tgt, *, name, tm=512):
    T, D = x.shape
    tm = min(tm, T)

    def body(x_ref, t_ref, dx_ref, l_ref):
        _acc_init(pl.program_id(0), l_ref)
        e = x_ref[...] - t_ref[...]
        dx_ref[...] = e * (1.0 / D)
        l_ref[...] += _colsum(e * e)

    return pl.pallas_call(
        body, grid=(T // tm,), in_specs=[_row(tm, D), _row(tm, D)],
        out_specs=[_row(tm, D), _full(1, D)],
        out_shape=[jax.ShapeDtypeStruct((T, D), F32), jax.ShapeDtypeStruct((1, D), F32)], name=name,
        compiler_params=_cp("arbitrary"))(x, tgt)


def post_bwd(dx, y, g, gate, *, name, tm=512):
    T, D = dx.shape
    tm = min(tm, T)

    def body(dx_ref, y_ref, g_ref, gate_ref, dy_ref, dgate_ref, dg_ref, dbias_ref):
        _acc_init(pl.program_id(0), dgate_ref, dg_ref, dbias_ref)
        yv = y_ref[...]
        dxv = dx_ref[...]
        r = lax.rsqrt(_rowmean(yv * yv) + EPS)
        yn = yv * r
        gv = g_ref[...]
        gt = gate_ref[...]
        dgate_ref[...] += _colsum(dxv * (yn * gv))
        dg_ref[...] += _colsum(dxv * gt * yn)
        dyn = dxv * gt * gv
        dy = r * (dyn - yn * _rowmean(dyn * yn))
        dbias_ref[...] += _colsum(dy)
        dy_ref[...] = dy.astype(BF16)

    return pl.pallas_call(
        body, grid=(T // tm,), in_specs=[_row(tm, D), _row(tm, D), _full(1, D), _full(1, D)],
        out_specs=[_row(tm, D), _full(1, D), _full(1, D), _full(1, D)],
        out_shape=[jax.ShapeDtypeStruct((T, D), BF16)] + [jax.ShapeDtypeStruct((1, D), F32)] * 3, name=name,
        compiler_params=_cp("arbitrary"))(dx, y, g, gate)


def prenorm_bwd(dh, x, dres, g, sc, *, name, tm=512):
    T, D = x.shape
    tm = min(tm, T)

    def body(dh_ref, x_ref, dres_ref, g_ref, sc_ref, dx_ref, dsh_ref, dsc_ref, dg_ref):
        _acc_init(pl.program_id(0), dsh_ref, dsc_ref, dg_ref)
        xv = x_ref[...]
        dhv = dh_ref[...]
        r = lax.rsqrt(_rowmean(xv * xv) + EPS)
        xn = xv * r
        gv = g_ref[...]
        one_sc = 1.0 + sc_ref[...]
        dsh_ref[...] += _colsum(dhv)
        dsc_ref[...] += _colsum(dhv * (xn * gv))
        dg_ref[...] += _colsum(dhv * one_sc * xn)
        dxn = dhv * one_sc * gv
        dx_ref[...] = dres_ref[...] + r * (dxn - xn * _rowmean(dxn * xn))

    return pl.pallas_call(
        body, grid=(T // tm,), in_specs=[_row(tm, D), _row(tm, D), _row(tm, D), _full(1, D), _full(1, D)],
        out_specs=[_row(tm, D), _full(1, D), _full(1, D), _full(1, D)],
        out_shape=[jax.ShapeDtypeStruct((T, D), F32)] + [jax.ShapeDtypeStruct((1, D), F32)] * 3, name=name,
        compiler_params=_cp("arbitrary"))(dh, x, dres, g, sc)


HALO = 16


def _shift_helpers():
    rid = lax.broadcasted_iota(jnp.int32, (SUB, LANE), 0)

    def down(cur, prev, k):
        return pltpu.roll(jnp.where(rid >= SUB - k, prev, cur), k, 0)

    def up(cur, nxt, k):
        return pltpu.roll(jnp.where(rid < k, nxt, cur), SUB - k, 0)

    return down, up


def _ffn_sides(c, nb, wa_ref, wb_ref, ba_ref, bb_ref):
    cols = slice(c * LANE, (c + 1) * LANE)
    return [(cols, [wa_ref[k:k + 1, cols] for k in range(FFN_W)], ba_ref[:, cols]),
            (slice(nb + c * LANE, nb + (c + 1) * LANE), [wb_ref[k:k + 1, cols] for k in range(FFN_W)],
             bb_ref[:, cols])]


def _ffn_specs(tm, nb, hb, idx):
    return [pl.BlockSpec((tm, 2 * nb), lambda jc, i: (idx(i), jc)),
            pl.BlockSpec((HALO, 2 * nb), lambda jc, i: (jnp.maximum(idx(i) * hb - 1, 0), jc)),
            pl.BlockSpec((FFN_W, nb), lambda jc, i: (0, jc)),
            pl.BlockSpec((FFN_W, nb), lambda jc, i: (0, jc + 2)),
            pl.BlockSpec((1, nb), lambda jc, i: (0, jc)),
            pl.BlockSpec((1, nb), lambda jc, i: (0, jc + 2))]


def ffn_act(u0p, dw_w, dw_b, *, name, tm=256):
    T, W = u0p.shape
    nb = W // 4
    tm = min(tm, T)
    unroll = 4
    rows16 = 2 * SUB

    def body(u_ref, halo_ref, wa_ref, wb_ref, ba_ref, bb_ref, z_ref, ab_ref):
        i = pl.program_id(1)
        down, _ = _shift_helpers()
        for c in range(nb // LANE):
            cols = slice(c * LANE, (c + 1) * LANE)
            side = _ffn_sides(c, nb, wa_ref, wb_ref, ba_ref, bb_ref)

            def rows(j, prev):
                prev = list(prev)
                for m in range(unroll):
                    r0 = pl.multiple_of((j * unroll + m) * rows16, rows16)
                    x = [u_ref[pl.ds(r0, rows16), cs].astype(F32) for cs, _, _ in side]
                    conv = [[None, None], [None, None]]
                    for hf in range(2):
                        for n, (_, w, b) in enumerate(side):
                            cur = x[n][hf * SUB:(hf + 1) * SUB, :]
                            conv[n][hf] = b + w[2] * cur + w[1] * down(cur, prev[n], 1) + w[0] * down(cur, prev[n], 2)
                            prev[n] = cur
                    a, b = [jnp.concatenate(conv[n], axis=0) for n in range(2)]
                    z_ref[pl.ds(r0, rows16), cols] = (_silu(a) * b).astype(BF16)
                    ab_ref[pl.ds(r0, rows16), side[0][0]] = a.astype(BF16)
                    ab_ref[pl.ds(r0, rows16), side[1][0]] = b.astype(BF16)
                return tuple(prev)

            first = [jnp.where(i == 0, 0.0, halo_ref[:, cs].astype(F32)[SUB:2 * SUB, :]) for cs, _, _ in side]
            lax.fori_loop(0, tm // (rows16 * unroll), rows, tuple(first))

    return pl.pallas_call(
        body, grid=(2, T // tm), in_specs=_ffn_specs(tm, nb, tm // HALO, lambda i: i),
        out_specs=[pl.BlockSpec((tm, nb), lambda jc, i: (i, jc)), pl.BlockSpec((tm, 2 * nb), lambda jc, i: (i, jc))],
        out_shape=[jax.ShapeDtypeStruct((T, 2 * nb), BF16), jax.ShapeDtypeStruct((T, W), BF16)], name=name,
        compiler_params=_cp("parallel", "arbitrary"))(u0p, u0p, dw_w, dw_w, dw_b, dw_b)


def ffn_act_bwd(dz, u0p, ab, dw_w, *, name, tm=256):
    T, W = u0p.shape
    nb = W // 4
    tm = min(tm, T)
    nt = T // tm
    unroll = 4
    rows16 = 2 * SUB
    n_it = tm // (rows16 * unroll)

    def body(dz_ref, u_ref, ab_ref, wa_ref, wb_ref, du0_ref, dw_ref, carry):
        i = pl.program_id(1)
        _acc_init(i, dw_ref)
        _, up = _shift_helpers()
        for c in range(nb // LANE):
            cols = slice(c * LANE, (c + 1) * LANE)
            side = [(cols, [wa_ref[k:k + 1, cols] for k in range(FFN_W)]),
                    (slice(nb + c * LANE, nb + (c + 1) * LANE), [wb_ref[k:k + 1, cols] for k in range(FFN_W)])]

            def rows(j, st):
                nxt, acc = list(st[0:2]), list(st[2:10])
                for m in range(unroll):
                    r0 = pl.multiple_of(((n_it - 1 - j) * unroll + unroll - 1 - m) * rows16, rows16)
                    dzv = dz_ref[pl.ds(r0, rows16), cols].astype(F32)
                    a, b = [ab_ref[pl.ds(r0, rows16), cs].astype(F32) for cs, _ in side]
                    x = [u_ref[pl.ds(r0, rows16), cs].astype(F32) for cs, _ in side]
                    sa = _sig(a)
                    d16 = [dzv * b * (sa * (1.0 + a * (1.0 - sa))), dzv * (a * sa)]
                    out = [[None, None], [None, None]]
                    for hf in (1, 0):
                        half = slice(hf * SUB, (hf + 1) * SUB)
                        for n in range(2):
                            w = side[n][1]
                            d = d16[n][half, :]
                            u = x[n][half, :]
                            up1, up2 = up(d, nxt[n], 1), up(d, nxt[n], 2)
                            acc[4 * n + 0] = acc[4 * n + 0] + up2 * u
                            acc[4 * n + 1] = acc[4 * n + 1] + up1 * u
                            acc[4 * n + 2] = acc[4 * n + 2] + d * u
                            acc[4 * n + 3] = acc[4 * n + 3] + d
                            out[n][hf] = w[2] * d + w[1] * up1 + w[0] * up2
                            nxt[n] = d
                    for n in range(2):
                        du0_ref[pl.ds(r0, rows16), side[n][0]] = jnp.concatenate(out[n], axis=0).astype(BF16)
                return (*nxt, *acc)

            init = [jnp.where(i == 0, 0.0, carry[:, cs]) for cs, _ in side] + [jnp.zeros((SUB, LANE), F32)] * 8
            st = lax.fori_loop(0, n_it, rows, tuple(init))
            for n in range(2):
                carry[:, side[n][0]] = st[n]
                for k in range(4):
                    dw_ref[k, :, side[n][0]] += st[2 + 4 * n + k]

        @pl.when(i == nt - 1)
        def _():
            for k in range(4):
                dw_ref[k, 0:1, :] = _colsum(dw_ref[k])

    rev = lambda i: nt - 1 - i
    wide = pl.BlockSpec((tm, 2 * nb), lambda jc, i: (rev(i), jc))
    return pl.pallas_call(
        body, grid=(2, nt),
        in_specs=[pl.BlockSpec((tm, nb), lambda jc, i: (rev(i), jc)), wide, wide,
                  pl.BlockSpec((FFN_W, nb), lambda jc, i: (0, jc)), pl.BlockSpec((FFN_W, nb), lambda jc, i: (0, jc + 2))],
        out_specs=[wide, pl.BlockSpec((4, SUB, 2 * nb), lambda jc, i: (0, 0, jc))],
        out_shape=[jax.ShapeDtypeStruct((T, W), BF16), jax.ShapeDtypeStruct((4, SUB, W), F32)],
        scratch_shapes=[pltpu.VMEM((SUB, 2 * nb), F32)], name=name,
        compiler_params=_cp("parallel", "arbitrary"))(dz, u0p, ab, dw_w, dw_w)


CHALO = 32
CCOL = 256


def _phase_copies(buf, shifted, tm):
    n = tm + CHALO - SUB
    for p in range(1, SUB):
        shifted[p - 1, 0:n, :] = buf[p:p + n, :]


def _shifted(buf, shifted, r, tm, c0):
    m, p = divmod(r, SUB)
    src = buf if p == 0 else shifted.at[p - 1]
    return src[m * SUB:m * SUB + tm, c0:c0 + CCOL]


def conv_act(u, dw_w, dw_b, ln_g, ln_b, *, name, tm=128):
    T, D2 = u.shape
    D = D2 // 2
    tm = min(tm, T)
    hb = tm // CHALO

    def body(u_ref, halo_ref, w_ref, b_ref, g_ref, be_ref, s_ref, cv_ref, gbuf, gsh):
        i = pl.program_id(0)
        hv = halo_ref[...]
        gbuf[0:CHALO, :] = jnp.where(i == 0, 0.0, hv[:, 0:D] * _sig(hv[:, D:D2]))
        uv = u_ref[...]
        gbuf[CHALO:CHALO + tm, :] = uv[:, 0:D] * _sig(uv[:, D:D2])
        _phase_copies(gbuf, gsh, tm)
        for c0 in range(0, D, CCOL):
            acc = jnp.zeros((tm, CCOL), F32) + b_ref[:, c0:c0 + CCOL]
            for k in range(CONV_W):
                acc = acc + w_ref[k:k + 1, c0:c0 + CCOL] * _shifted(gbuf, gsh, CHALO - (CONV_W - 1) + k, tm, c0)
            cv_ref[:, c0:c0 + CCOL] = acc
        cv = cv_ref[...]
        mu = _rowmean(cv)
        xc = cv - mu
        nh = xc * lax.rsqrt(_rowmean(xc * xc) + EPS)
        s_ref[...] = _silu(nh * g_ref[...] + be_ref[...]).astype(BF16)

    return pl.pallas_call(
        body, grid=(T // tm,),
        in_specs=[_row(tm, D2), pl.BlockSpec((CHALO, D2), lambda i: (jnp.maximum(i * hb - 1, 0), 0)),
                  _full(CONV_W, D), _full(1, D), _full(1, D), _full(1, D)],
        out_specs=[_row(tm, D), _row(tm, D)],
        out_shape=[jax.ShapeDtypeStruct((T, D), BF16), jax.ShapeDtypeStruct((T, D), F32)],
        scratch_shapes=[pltpu.VMEM((tm + CHALO, D), F32), pltpu.VMEM((SUB - 1, tm + CHALO, D), F32)], name=name,
        compiler_params=_cp("arbitrary"))(u, u, dw_w, dw_b, ln_g, ln_b)


def conv_norm_bwd(ds, cv, ln_g, ln_b, *, name, tm=512):
    T, D = cv.shape
    tm = min(tm, T)

    def body(ds_ref, cv_ref, g_ref, be_ref, dcv_ref, dg_ref, dbe_ref, dcb_ref):
        _acc_init(pl.program_id(0), dg_ref, dbe_ref, dcb_ref)
        cv_ = cv_ref[...]
        mu = _rowmean(cv_)
        xc = cv_ - mu
        rstd = lax.rsqrt(_rowmean(xc * xc) + EPS)
        nh = xc * rstd
        gv = g_ref[...]
        dln = ds_ref[...] * _dsilu(nh * gv + be_ref[...])
        dg_ref[...] += _colsum(dln * nh)
        dbe_ref[...] += _colsum(dln)
        dnh = dln * gv
        dcv = rstd * (dnh - _rowmean(dnh) - nh * _rowmean(dnh * nh))
        dcb_ref[...] += _colsum(dcv)
        dcv_ref[...] = dcv

    return pl.pallas_call(
        body, grid=(T // tm,), in_specs=[_row(tm, D), _row(tm, D), _full(1, D), _full(1, D)],
        out_specs=[_row(tm, D), _full(1, D), _full(1, D), _full(1, D)],
        out_shape=[jax.ShapeDtypeStruct((T, D), F32)] + [jax.ShapeDtypeStruct((1, D), F32)] * 3, name=name,
        compiler_params=_cp("arbitrary"))(ds, cv, ln_g, ln_b)


def conv_glu_bwd(dcv, u, dw_w, *, name, tm=128):
    T, D2 = u.shape
    D = D2 // 2
    tm = min(tm, T)
    nt = T // tm
    hb = tm // CHALO

    def body(dcv_ref, dnext_ref, u_ref, w_ref, du_ref, dw_ref, dbin_ref, dbuf, dsh):
        i = pl.program_id(0)
        _acc_init(i, dw_ref, dbin_ref)
        uv = u_ref[...]
        av = uv[:, 0:D]
        sg = _sig(uv[:, D:D2])
        glu = av * sg
        dbuf[0:tm, :] = dcv_ref[...]
        dbuf[tm:tm + CHALO, :] = jnp.where(i == nt - 1, 0.0, dnext_ref[...])
        _phase_copies(dbuf, dsh, tm)
        for c0 in range(0, D, CCOL):
            glu_c = glu[:, c0:c0 + CCOL]
            acc = jnp.zeros((tm, CCOL), F32)
            for k in range(CONV_W):
                moved = _shifted(dbuf, dsh, CONV_W - 1 - k, tm, c0)
                dw_ref[k:k + 1, c0:c0 + CCOL] += _colsum(moved * glu_c)
                acc = acc + w_ref[k:k + 1, c0:c0 + CCOL] * moved
            a_c = av[:, c0:c0 + CCOL]
            s_c = sg[:, c0:c0 + CCOL]
            da = acc * s_c
            dgt = acc * a_c * s_c * (1.0 - s_c)
            dbin_ref[:, c0:c0 + CCOL] += _colsum(da)
            dbin_ref[:, D + c0:D + c0 + CCOL] += _colsum(dgt)
            du_ref[:, c0:c0 + CCOL] = da.astype(BF16)
            du_ref[:, D + c0:D + c0 + CCOL] = dgt.astype(BF16)

    return pl.pallas_call(
        body, grid=(nt,),
        in_specs=[_row(tm, D), pl.BlockSpec((CHALO, D), lambda i: (jnp.minimum((i + 1) * hb, T // CHALO - 1), 0)),
                  _row(tm, D2), _full(CONV_W, D)],
        out_specs=[_row(tm, D2), _full(CHALO, D), _full(1, D2)],
        out_shape=[jax.ShapeDtypeStruct((T, D2), BF16), jax.ShapeDtypeStruct((CHALO, D), F32),
                   jax.ShapeDtypeStruct((1, D2), F32)],
        scratch_shapes=[pltpu.VMEM((tm + CHALO, D), F32), pltpu.VMEM((SUB - 1, tm + CHALO, D), F32)],
        name=name, compiler_params=_cp("arbitrary"))(dcv, dcv, u, dw_w)


HB = 4


def _lb0(lg_ref):
    l0, l1, l2 = lg_ref[0:1, :], lg_ref[1:2, :], lg_ref[2:3, :]
    m = jnp.maximum(jnp.maximum(l0, l1), l2)
    e0 = jnp.exp(l0 - m)
    return e0 / (e0 + jnp.exp(l1 - m) + jnp.exp(l2 - m))


def _mm_exact(m01, x):
    hi = x.astype(BF16)
    r1 = x - hi.astype(F32)
    mid = r1.astype(BF16)
    lo = (r1 - mid.astype(F32)).astype(BF16)
    return _dot(m01, hi) + _dot(m01, mid) + _dot(m01, lo)


def _block_tri(tm):
    r = jnp.arange(tm)[:, None]
    c = jnp.arange(tm)[None, :]
    same = (r // BLK) == (c // BLK)
    return (same & (c <= r)).astype(BF16), (same & (c >= r)).astype(BF16)


def _halves(x):
    return [x[0:SUB, :], x[SUB:BLK, :]]


def _live_halves(s):
    return ([(0, s)] if s < SUB else []) + [(1, max(s - SUB, 0))]


def _const_spec(shape):
    return pl.BlockSpec(shape, lambda h, i: (0, 0))


def _hgrn_specs(H, hb, tm, idx):
    g = H // hb
    return [pl.BlockSpec((tm, hb * HEAD), lambda h, i: (idx(i), h)),
            pl.BlockSpec((tm, hb * HEAD), lambda h, i: (idx(i), g + h)),
            pl.BlockSpec((tm, hb * HEAD), lambda h, i: (idx(i), 2 * g + h)),
            pl.BlockSpec((3, hb * HEAD), lambda h, i: (0, h))]


def hgrn_scan(proj, lb_logits, *, name, tm=128):
    T = proj.shape[0]
    H = proj.shape[1] // (4 * HEAD)
    hb = min(HB, H)
    tm = min(tm, T)
    nt = T // tm
    nblk = tm // BLK
    tril, _ = _block_tri(tm)
    heads = [slice(hh * HEAD, (hh + 1) * HEAD) for hh in range(hb)]

    def body(qp_ref, fz_ref, v_ref, lg_ref, tril_ref, o_ref, st_ref, S_ref, q_s, k_s, b_s):
        @pl.when(pl.program_id(1) == 0)
        def _():
            S_ref[...] = jnp.zeros_like(S_ref)

        st_ref[...] = S_ref[...]
        lb = _lb0(lg_ref)
        f = lb + (1.0 - lb) * _sig(fz_ref[...])
        q_s[...] = _silu(qp_ref[...])
        k_s[...] = 1.0 - f
        b_s[...] = _mm_exact(tril_ref[...], jnp.log(f))
        rows = lax.broadcasted_iota(jnp.int32, (BLK, HEAD), 0)
        S = [S_ref[hh] for hh in range(hb)]
        for nb in range(nblk):
            blk = slice(nb * BLK, (nb + 1) * BLK)
            last = slice(nb * BLK + BLK - 1, nb * BLK + BLK)
            qb = [q_s[blk, c] for c in heads]
            bb = [b_s[blk, c] for c in heads]
            o = [_dot_nt((qb[hh] * jnp.exp(bb[hh])).astype(BF16), S[hh].astype(BF16)) for hh in range(hb)]
            for hh, c in enumerate(heads):
                bc = b_s[last, c]
                kd = k_s[blk, c] * jnp.exp(bc - bb[hh])
                S[hh] = S[hh] * jnp.exp(bc) + _dot_tn(v_ref[blk, c].astype(BF16), kd.astype(BF16))
            for s in range(BLK):
                r = slice(nb * BLK + s, nb * BLK + s + 1)
                for hh, c in enumerate(heads):
                    dec = jnp.exp(jnp.where(rows >= s, bb[hh] - b_s[r, c], NEG))
                    a = jnp.sum(qb[hh] * k_s[r, c] * dec, axis=-1, keepdims=True)
                    o[hh] = o[hh] + a * v_ref[r, c]
            for hh, c in enumerate(heads):
                o_ref[blk, c] = o[hh]
        for hh in range(hb):
            S_ref[hh] = S[hh]

    return pl.pallas_call(
        body, grid=(H // hb, nt),
        in_specs=_hgrn_specs(H, hb, tm, lambda i: i) + [_const_spec((tm, tm))],
        out_specs=[pl.BlockSpec((tm, hb * HEAD), lambda h, i: (i, h)),
                   pl.BlockSpec((None, hb, HEAD, HEAD), lambda h, i: (i, h, 0, 0))],
        out_shape=[jax.ShapeDtypeStruct((T, H * HEAD), F32), jax.ShapeDtypeStruct((nt, H, HEAD, HEAD), F32)],
        scratch_shapes=[pltpu.VMEM((hb, HEAD, HEAD), F32)] + [pltpu.VMEM((tm, hb * HEAD), F32)] * 3, name=name,
        compiler_params=_cp("parallel", "arbitrary"))(proj, proj, proj, lb_logits, tril)


def hgrn_scan_bwd(proj, lb_logits, states, do, *, name, tm=128):
    T = proj.shape[0]
    H = proj.shape[1] // (4 * HEAD)
    hb = min(HB, H)
    tm = min(tm, T)
    nt = T // tm
    nblk = tm // BLK
    tril, triu = _block_tri(tm)
    sel = (jnp.arange(BLK * SUB)[None, :] // SUB == jnp.arange(BLK)[:, None]).astype(BF16)
    heads = [slice(hh * HEAD, (hh + 1) * HEAD) for hh in range(hb)]

    def body(qp_ref, fz_ref, v_ref, lg_ref, st_ref, do_ref, tril_ref, triu_ref, sel_ref, dqp_ref, dfz_ref, dv_ref, dlb_ref,
             dS_ref, Sb_ref, q_s, k_s, b_s, dq_s, dk_s, dv_s, db_s, pk_s, pv_s):
        i = pl.program_id(1)

        @pl.when(i == 0)
        def _():
            dS_ref[...] = jnp.zeros_like(dS_ref)
            dlb_ref[...] = jnp.zeros_like(dlb_ref)

        lb = _lb0(lg_ref)
        qp = qp_ref[...]
        sg = _sig(fz_ref[...])
        f = lb + (1.0 - lb) * sg
        q_s[...] = _silu(qp)
        k_s[...] = 1.0 - f
        b_s[...] = _mm_exact(tril_ref[...], jnp.log(f))
        rows = lax.broadcasted_iota(jnp.int32, (SUB, HEAD), 0)
        rows1 = lax.broadcasted_iota(jnp.int32, (SUB, 1), 0)

        S = [st_ref[hh] for hh in range(hb)]
        for nb in range(nblk):
            blk = slice(nb * BLK, (nb + 1) * BLK)
            last = slice(nb * BLK + BLK - 1, nb * BLK + BLK)
            for hh, c in enumerate(heads):
                Sb_ref[nb * hb + hh] = S[hh]
                if nb < nblk - 1:
                    bc = b_s[last, c]
                    kd = k_s[blk, c] * jnp.exp(bc - b_s[blk, c])
                    S[hh] = S[hh] * jnp.exp(bc) + _dot_tn(v_ref[blk, c].astype(BF16), kd.astype(BF16))

        dS = [dS_ref[hh] for hh in range(hb)]
        for nb in reversed(range(nblk)):
            blk = slice(nb * BLK, (nb + 1) * BLK)
            last = slice(nb * BLK + BLK - 1, nb * BLK + BLK)
            qb, kb, bb, dob, dq, dbc, ebc = [], [], [], [], [], [], []
            for hh, c in enumerate(heads):
                S0 = Sb_ref[nb * hb + hh]
                qb.append(q_s[blk, c])
                kb.append(k_s[blk, c])
                bb.append(b_s[blk, c])
                dob.append(do_ref[blk, c])
                bc = b_s[last, c]
                eb = jnp.exp(bb[hh])
                ekd = jnp.exp(bc - bb[hh])
                ebc.append(jnp.exp(bc))
                dS16 = dS[hh].astype(BF16)
                dob16 = dob[hh].astype(BF16)
                dq.append(_dot(dob16, S0.astype(BF16)) * eb)
                dki = _dot(v_ref[blk, c].astype(BF16), dS16) * ekd
                dk_s[blk, c] = dki
                dv_s[blk, c] = _dot_nt((kb[hh] * ekd).astype(BF16), dS16)
                dbc.append(_colsum(dS[hh] * S0) * ebc[hh] + _colsum(kb[hh] * dki))
                dS[hh] = dS[hh] * ebc[hh] + _dot_tn(dob16, (qb[hh] * eb).astype(BF16))
            qh, bh, doh, dqh = [[_halves(t[hh]) for hh in range(hb)] for t in (qb, bb, dob, dq)]
            for s in range(BLK):
                r = slice(nb * BLK + s, nb * BLK + s + 1)
                for hh, c in enumerate(heads):
                    ks = k_s[r, c]
                    pk, pv = None, None
                    for hf, lo in _live_halves(s):
                        diff = bh[hh][hf] - b_s[r, c]
                        dec = jnp.exp(diff if lo == 0 else jnp.where(rows >= lo, diff, NEG))
                        w = qh[hh][hf] * dec
                        a = jnp.sum(w * ks, axis=-1, keepdims=True)
                        da = jnp.sum(doh[hh][hf] * v_ref[r, c], axis=-1, keepdims=True)
                        if lo:
                            da = jnp.where(rows1 >= lo, da, 0.0)
                        dqh[hh][hf] = dqh[hh][hf] + (da * ks) * dec
                        pk = da * w if pk is None else pk + da * w
                        pv = a * doh[hh][hf] if pv is None else pv + a * doh[hh][hf]
                    pk_s[hh, s * SUB:(s + 1) * SUB, :] = pk
                    pv_s[hh, s * SUB:(s + 1) * SUB, :] = pv
            for hh, c in enumerate(heads):
                khi, klo = _split2(pk_s[hh])
                dk_s[blk, c] += _dot(sel_ref[...], khi) + _dot(sel_ref[...], klo)
                dv_s[blk, c] += _dot(sel_ref[...], pv_s[hh].astype(BF16))
                dq[hh] = jnp.concatenate(dqh[hh], axis=0)
                dq_s[blk, c] = dq[hh]
                db_s[blk, c] = qb[hh] * dq[hh] - kb[hh] * dk_s[blk, c]
                db_s[last, c] += dbc[hh]
        for hh in range(hb):
            dS_ref[hh] = dS[hh]

        dlf = _mm_exact(triu_ref[...], db_s[...])
        df = dlf / f - dk_s[...]
        dfz_ref[...] = (df * (1.0 - lb) * sg * (1.0 - sg)).astype(BF16)
        dlb_ref[...] += _colsum(df * (1.0 - sg))
        dqp_ref[...] = (dq_s[...] * _dsilu(qp)).astype(BF16)
        dv_ref[...] = dv_s[...].astype(BF16)

    rev = lambda i: nt - 1 - i
    out_blk = pl.BlockSpec((tm, hb * HEAD), lambda h, i: (rev(i), h))
    return pl.pallas_call(
        body, grid=(H // hb, nt),
        in_specs=_hgrn_specs(H, hb, tm, rev) + [pl.BlockSpec((None, hb, HEAD, HEAD), lambda h, i: (rev(i), h, 0, 0)),
                                                out_blk, _const_spec((tm, tm)), _const_spec((tm, tm)),
                                                _const_spec((BLK, BLK * SUB))],
        out_specs=[out_blk, out_blk, out_blk, pl.BlockSpec((1, hb * HEAD), lambda h, i: (0, h))],
        out_shape=[jax.ShapeDtypeStruct((T, H * HEAD), BF16)] * 3 + [jax.ShapeDtypeStruct((1, H * HEAD), F32)],
        scratch_shapes=[pltpu.VMEM((hb, HEAD, HEAD), F32), pltpu.VMEM((nblk * hb, HEAD, HEAD), F32)]
        + [pltpu.VMEM((tm, hb * HEAD), F32)] * 7 + [pltpu.VMEM((hb, BLK * SUB, HEAD), F32)] * 2, name=name,
        compiler_params=_cp("parallel", "arbitrary"))(proj, proj, proj, lb_logits, states, do, tril, triu, sel)


def hgrn_gate(o, proj, gn, *, name, tm=512):
    T, D = o.shape
    H = D // HEAD
    tm = min(tm, T)

    def body(o_ref, gp_ref, gn_ref, og_ref):
        gn_ = gn_ref[...]
        for h in range(H):
            c = slice(h * HEAD, (h + 1) * HEAD)
            oh = o_ref[:, c]
            r = lax.rsqrt(_rowmean(oh * oh) + EPS)
            og_ref[:, c] = ((oh * r) * gn_ * _silu(gp_ref[:, c])).astype(BF16)

    return pl.pallas_call(
        body, grid=(T // tm,),
        in_specs=[_row(tm, D), pl.BlockSpec((tm, D), lambda i: (i, 3)), _full(1, HEAD)],
        out_specs=_row(tm, D), out_shape=jax.ShapeDtypeStruct((T, D), BF16), name=name,
        compiler_params=_cp("parallel"))(o, proj, gn)


def hgrn_gate_bwd(dog, o, proj, gn, *, name, tm=512):
    T, D = o.shape
    H = D // HEAD
    tm = min(tm, T)

    def body(dog_ref, o_ref, gp_ref, gn_ref, do_ref, dgp_ref, dgn_ref):
        _acc_init(pl.program_id(0), dgn_ref)
        gn_ = gn_ref[...]
        for h in range(H):
            c = slice(h * HEAD, (h + 1) * HEAD)
            oh = o_ref[:, c]
            gp = gp_ref[:, c]
            dg = dog_ref[:, c]
            r = lax.rsqrt(_rowmean(oh * oh) + EPS)
            on = oh * r
            dgp_ref[:, c] = (dg * (on * gn_) * _dsilu(gp)).astype(BF16)
            don = dg * _silu(gp)
            dgn_ref[...] += _colsum(don * on)
            dn = don * gn_
            do_ref[:, c] = r * (dn - on * _rowmean(dn * on))

    return pl.pallas_call(
        body, grid=(T // tm,),
        in_specs=[_row(tm, D), _row(tm, D), pl.BlockSpec((tm, D), lambda i: (i, 3)), _full(1, HEAD)],
        out_specs=[_row(tm, D), _row(tm, D), _full(1, HEAD)],
        out_shape=[jax.ShapeDtypeStruct((T, D), F32), jax.ShapeDtypeStruct((T, D), BF16),
                   jax.ShapeDtypeStruct((1, HEAD), F32)], name=name,
        compiler_params=_cp("arbitrary"))(dog, o, proj, gn)


def _split2(x):
    hi = x.astype(BF16)
    return hi, (x - hi.astype(F32)).astype(BF16)


def ada_mod(c_all, ada_w, *, name):
    L, D, N = ada_w.shape
    B = c_all.shape[0]

    def body(c_ref, w_ref, o_ref):
        chi, clo = _split2(_silu(c_ref[...]))
        whi, wlo = _split2(w_ref[...])
        o_ref[...] = _dot(chi, whi) + _dot(chi, wlo) + _dot(clo, whi)

    return pl.pallas_call(
        body, grid=(L,), in_specs=[_full(B, D), pl.BlockSpec((None, D, N), lambda l: (l, 0, 0))],
        out_specs=pl.BlockSpec((None, B, N), lambda l: (l, 0, 0)),
        out_shape=jax.ShapeDtypeStruct((L, B, N), F32), name=name, compiler_params=_cp("parallel"))(c_all, ada_w)


def ada_wgrad(c_all_t, dmod, *, name, tr=256):
    D, B = c_all_t.shape
    L, _, N = dmod.shape
    tr = min(tr, D)

    def body(c_ref, d_ref, o_ref):
        cond = _silu(c_ref[...])
        acc = cond[:, 0:1] * d_ref[0:1, :]
        for b in range(1, B):
            acc = acc + cond[:, b:b + 1] * d_ref[b:b + 1, :]
        o_ref[...] = acc

    return pl.pallas_call(
        body, grid=(L, D // tr),
        in_specs=[pl.BlockSpec((tr, B), lambda l, r: (r, 0)), pl.BlockSpec((None, B, N), lambda l, r: (l, 0, 0))],
        out_specs=pl.BlockSpec((None, tr, N), lambda l, r: (l, r, 0)),
        out_shape=jax.ShapeDtypeStruct((L, D, N), F32), name=name,
        compiler_params=_cp("parallel", "parallel"))(c_all_t, dmod)


def sum_devices(parts, *, name):
    n, R, C = parts.shape

    def body(p_ref, o_ref):
        acc = p_ref[0]
        for d in range(1, n):
            acc = acc + p_ref[d]
        o_ref[...] = acc

    return pl.pallas_call(body, in_specs=[VMEM_SPEC], out_specs=VMEM_SPEC,
                          out_shape=jax.ShapeDtypeStruct((R, C), F32), name=name)(parts)


def lb_logits_grad(lb_logits, dlb, *, name):
    def body(lg_ref, d_ref, o_ref):
        l0, l1, l2 = lg_ref[0:1, :], lg_ref[1:2, :], lg_ref[2:3, :]
        m = jnp.maximum(jnp.maximum(l0, l1), l2)
        e0, e1, e2 = jnp.exp(l0 - m), jnp.exp(l1 - m), jnp.exp(l2 - m)
        z = e0 + e1 + e2
        p0, p1, p2 = e0 / z, e1 / z, e2 / z
        g = d_ref[...] * p0
        o_ref[0:1, :] = g * (1.0 - p0)
        o_ref[1:2, :] = -g * p1
        o_ref[2:3, :] = -g * p2

    return pl.pallas_call(body, in_specs=[VMEM_SPEC, VMEM_SPEC], out_specs=VMEM_SPEC,
                          out_shape=jax.ShapeDtypeStruct(lb_logits.shape, F32), name=name)(lb_logits, dlb)


def adamw(w, g, m, v, *, name, tr=256):
    R, C = w.shape
    tr = _tile(R, tr)

    def body(w_ref, g_ref, m_ref, v_ref, d_ref, nm_ref, nv_ref):
        gv = g_ref[...]
        nm = ADAM_B1 * m_ref[...] + (1.0 - ADAM_B1) * gv
        nv = ADAM_B2 * v_ref[...] + (1.0 - ADAM_B2) * (gv * gv)
        m_hat = nm / (1.0 - ADAM_B1 ** ADAM_STEP)
        v_hat = nv / (1.0 - ADAM_B2 ** ADAM_STEP)
        d_ref[...] = -ADAM_LR * (m_hat / (jnp.sqrt(v_hat) + ADAM_EPS) + ADAM_WD * w_ref[...])
        nm_ref[...] = nm
        nv_ref[...] = nv

    spec = pl.BlockSpec((tr, C), lambda i: (i, 0))
    return pl.pallas_call(
        body, grid=(R // tr,), in_specs=[spec] * 4, out_specs=[spec] * 3,
        out_shape=[jax.ShapeDtypeStruct((R, C), F32)] * 3, name=name, compiler_params=_cp("parallel"))(w, g, m, v)


def _place():
    return lax.axis_index("x"), lax.axis_index("y"), lax.axis_index("c")


def _flip(v, bit):
    return 1 - v if bit else v


def allgather_devices(v, *, name):
    R, C = v.shape

    def body(v_ref, out_ref, send_sems, recv_sems, local_sem):
        x, y, c = _place()
        me = 4 * x + 2 * y + c
        mine = pltpu.make_async_copy(v_ref, out_ref.at[me], local_sem)
        mine.start()
        sends = []
        for k in range(1, N_DEV):
            peer = (_flip(x, k & 4), _flip(y, k & 2), _flip(c, k & 1))
            cp = pltpu.make_async_remote_copy(src_ref=v_ref, dst_ref=out_ref.at[me], send_sem=send_sems.at[k - 1],
                                              recv_sem=recv_sems.at[k - 1], device_id=peer, device_id_type=MESH)
            cp.start()
            sends.append(cp)
        for k in range(1, N_DEV):
            px, py, pc = _flip(x, k & 4), _flip(y, k & 2), _flip(c, k & 1)
            pltpu.make_async_remote_copy(src_ref=v_ref, dst_ref=out_ref.at[4 * px + 2 * py + pc],
                                         send_sem=send_sems.at[k - 1], recv_sem=recv_sems.at[k - 1],
                                         device_id=(px, py, pc), device_id_type=MESH).wait_recv()
        for cp in sends:
            cp.wait_send()
        mine.wait()

    return pl.pallas_call(
        body, in_specs=[VMEM_SPEC], out_specs=VMEM_SPEC, out_shape=jax.ShapeDtypeStruct((N_DEV, R, C), v.dtype),
        scratch_shapes=[pltpu.SemaphoreType.DMA((N_DEV - 1,)), pltpu.SemaphoreType.DMA((N_DEV - 1,)),
                        pltpu.SemaphoreType.DMA], name=name)(v)


def _other_chips(x, y):
    return [(1 - x, y), (x, 1 - y), (1 - x, 1 - y)]


SEM = pl.BlockSpec(memory_space=pltpu.SEMAPHORE)
DATAFLOW = pltpu.SideEffectType.DATAFLOW_SIDE_EFFECTING


def _chip_copy(buf, a, j, q, c, chips, send_sems, recv_sems):
    px, py = chips[j]
    return pltpu.make_async_remote_copy(src_ref=buf.at[q, c], dst_ref=buf.at[q, c], send_sem=send_sems.at[3 * a + j],
                                        recv_sem=recv_sems.at[3 * a + j], device_id=(px, py, c), device_id_type=MESH)


def allgather_chips_start(bufs, *, name):
    n = len(bufs)

    def body(*refs):
        send_sems, recv_sems = refs[n], refs[n + 1]
        outs = refs[n + 2:2 * n + 2]
        token = refs[2 * n + 2]
        x, y, c = _place()
        chips = _other_chips(x, y)
        for a in range(n):
            for j in range(3):
                _chip_copy(outs[a], a, j, 2 * x + y, c, chips, send_sems, recv_sems).start()
        token[...] = jnp.zeros_like(token)

    res = pl.pallas_call(
        body, name=name, in_specs=[HBM] * n,
        out_specs=(SEM, SEM, *([HBM] * n), VMEM_SPEC),
        out_shape=(pltpu.SemaphoreType.DMA((3 * n,)), pltpu.SemaphoreType.DMA((3 * n,)),
                   *[pltpu.HBM(b.shape, b.dtype) for b in bufs], jax.ShapeDtypeStruct((SUB, LANE), F32)),
        input_output_aliases={a: a + 2 for a in range(n)},
        compiler_params=pltpu.CompilerParams(has_side_effects=DATAFLOW),
    )(*[pltpu.with_memory_space_constraint(b, pltpu.HBM) for b in bufs])
    return res[0], res[1], list(res[2:2 + n]), res[2 + n]


def allgather_chips_wait(send_sems, recv_sems, bufs, after, *, name):
    n = len(bufs)

    def body(*refs):
        ins = refs[:n]
        send_sems, recv_sems = refs[n], refs[n + 1]
        x, y, c = _place()
        chips = _other_chips(x, y)
        for a in range(n):
            for j, (px, py) in enumerate(chips):
                _chip_copy(ins[a], a, j, 2 * x + y, c, chips, send_sems, recv_sems).wait_send()
                _chip_copy(ins[a], a, j, 2 * px + py, c, chips, send_sems, recv_sems).wait_recv()

    return list(pl.pallas_call(
        body, name=name, in_specs=[HBM] * n + [SEM, SEM, pl.BlockSpec(memory_space=pl.ANY)],
        out_specs=[HBM] * n, out_shape=[pltpu.HBM(b.shape, b.dtype) for b in bufs],
        input_output_aliases={a: a for a in range(n)},
        compiler_params=pltpu.CompilerParams(has_side_effects=DATAFLOW),
    )(*bufs, send_sems, recv_sems, after))


def forward_to_sibling(bufs, *, name):
    n = len(bufs)

    def body(*refs):
        outs = refs[n:2 * n]
        send_sems, recv_sems = refs[2 * n:]
        x, y, c = _place()
        chips = _other_chips(x, y)

        def copy(a, j, half, to):
            px, py = chips[j]
            slab = outs[a].at[2 * px + py, half]
            return pltpu.make_async_remote_copy(src_ref=slab, dst_ref=slab, send_sem=send_sems.at[a, j],
                                                recv_sem=recv_sems.at[a, j], device_id=to, device_id_type=MESH)

        sends = [copy(a, j, c, (x, y, 1 - c)) for a in range(n) for j in range(3)]
        for cp in sends:
            cp.start()
        for a in range(n):
            for j in range(3):
                copy(a, j, 1 - c, (x, y, c)).wait_recv()
        for cp in sends:
            cp.wait_send()

    return pl.pallas_call(
        body, in_specs=[HBM] * n, out_specs=[HBM] * n,
        out_shape=[jax.ShapeDtypeStruct(b.shape, b.dtype) for b in bufs],
        input_output_aliases={a: a for a in range(n)},
        scratch_shapes=[pltpu.SemaphoreType.DMA((n, 3)), pltpu.SemaphoreType.DMA((n, 3))], name=name)(*bufs)


def pair_exchange(grads, *, name):
    n = len(grads)

    def body(*refs):
        ins, outs = refs[:n], refs[n:2 * n]
        send_sems, recv_sems = refs[2 * n:]
        x, y, c = _place()
        cps = [pltpu.make_async_remote_copy(src_ref=ins[a].at[1 - c], dst_ref=outs[a], send_sem=send_sems.at[a],
                                            recv_sem=recv_sems.at[a], device_id=(x, y, 1 - c), device_id_type=MESH)
               for a in range(n)]
        for cp in cps:
            cp.start()
        for cp in cps:
            cp.wait_recv()
        for cp in cps:
            cp.wait_send()

    return pl.pallas_call(
        body, in_specs=[HBM] * n, out_specs=[HBM] * n,
        out_shape=[jax.ShapeDtypeStruct(g.shape[1:], g.dtype) for g in grads],
        scratch_shapes=[pltpu.SemaphoreType.DMA((n,)), pltpu.SemaphoreType.DMA((n,))], name=name)(*grads)


def pair_add(g, other, c_idx, *, name, tr=256):
    _, Q, R, C = g.shape
    tr = _tile(R, tr)

    def body(c_ref, g_ref, o_ref, out_ref):
        out_ref[...] = (g_ref[...] + o_ref[...]).astype(BF16)

    return pl.pallas_call(
        body,
        grid_spec=pltpu.PrefetchScalarGridSpec(
            num_scalar_prefetch=1, grid=(Q, R // tr),
            in_specs=[pl.BlockSpec((None, None, tr, C), lambda q, r, c_ref: (c_ref[0], q, r, 0)),
                      pl.BlockSpec((None, tr, C), lambda q, r, c_ref: (q, r, 0))],
            out_specs=pl.BlockSpec((None, tr, C), lambda q, r, c_ref: (q, r, 0))),
        out_shape=jax.ShapeDtypeStruct((Q, R, C), BF16), name=name,
        compiler_params=_cp("parallel", "parallel"))(c_idx, g, other)


def chip_sum(sums, landed, qc_idx, *, name, tr=256):
    _, R, C = sums.shape
    tr = _tile(R, tr)

    def body(qc_ref, own_ref, l_ref, o_ref):
        acc = own_ref[...].astype(F32)
        for k in range(3):
            acc = acc + l_ref[k].astype(F32)
        o_ref[...] = acc

    return pl.pallas_call(
        body,
        grid_spec=pltpu.PrefetchScalarGridSpec(
            num_scalar_prefetch=1, grid=(R // tr,),
            in_specs=[pl.BlockSpec((None, tr, C), lambda r, qc: (qc[0], r, 0)),
                      pl.BlockSpec((3, tr, C), lambda r, qc: (0, r, 0))],
            out_specs=pl.BlockSpec((None, tr, C), lambda r, qc: (qc[1], r, 0))),
        out_shape=jax.ShapeDtypeStruct((2, R, C), F32), name=name,
        compiler_params=_cp("parallel"))(qc_idx, sums, landed)


def half_swap(bufs, *, name):
    n = len(bufs)

    def body(*refs):
        outs = refs[n:2 * n]
        send_sems, recv_sems = refs[2 * n:]
        x, y, c = _place()
        cps = [pltpu.make_async_remote_copy(src_ref=outs[a].at[c], dst_ref=outs[a].at[c], send_sem=send_sems.at[a],
                                            recv_sem=recv_sems.at[a], device_id=(x, y, 1 - c), device_id_type=MESH)
               for a in range(n)]
        for cp in cps:
            cp.start()
        for a in range(n):
            pltpu.make_async_remote_copy(src_ref=outs[a].at[c], dst_ref=outs[a].at[1 - c], send_sem=send_sems.at[a],
                                         recv_sem=recv_sems.at[a], device_id=(x, y, 1 - c),
                                         device_id_type=MESH).wait_recv()
        for cp in cps:
            cp.wait_send()

    return pl.pallas_call(
        body, in_specs=[HBM] * n, out_specs=[HBM] * n,
        out_shape=[jax.ShapeDtypeStruct(b.shape, b.dtype) for b in bufs],
        input_output_aliases={a: a for a in range(n)},
        scratch_shapes=[pltpu.SemaphoreType.DMA((n,)), pltpu.SemaphoreType.DMA((n,))], name=name)(*bufs)


def _exchange_copy(sums, landed, a, j, c, chips, send_sems, recv_sems):
    px, py = chips[j]
    return pltpu.make_async_remote_copy(src_ref=sums.at[2 * px + py], dst_ref=landed.at[j],
                                        send_sem=send_sems.at[3 * a + j], recv_sem=recv_sems.at[3 * a + j],
                                        device_id=(px, py, c), device_id_type=MESH)


def chip_exchange_start(sums, *, name):
    n = len(sums)
    landing = [lax.empty((3,) + s.shape[1:], s.dtype) for s in sums]

    def body(*refs):
        send_sems, recv_sems = refs[2 * n], refs[2 * n + 1]
        src, dst = refs[2 * n + 2:3 * n + 2], refs[3 * n + 2:4 * n + 2]
        token = refs[4 * n + 2]
        x, y, c = _place()
        chips = _other_chips(x, y)
        for a in range(n):
            for j in range(3):
                _exchange_copy(src[a], dst[a], a, j, c, chips, send_sems, recv_sems).start()
        token[...] = jnp.zeros_like(token)

    res = pl.pallas_call(
        body, name=name, in_specs=[HBM] * (2 * n),
        out_specs=(SEM, SEM, *([HBM] * (2 * n)), VMEM_SPEC),
        out_shape=(pltpu.SemaphoreType.DMA((3 * n,)), pltpu.SemaphoreType.DMA((3 * n,)),
                   *[pltpu.HBM(b.shape, b.dtype) for b in sums + landing], jax.ShapeDtypeStruct((SUB, LANE), F32)),
        input_output_aliases={a: a + 2 for a in range(2 * n)},
        compiler_params=pltpu.CompilerParams(has_side_effects=DATAFLOW),
    )(*[pltpu.with_memory_space_constraint(b, pltpu.HBM) for b in sums + landing])
    return res[0], res[1], list(res[2:2 + n]), list(res[2 + n:2 + 2 * n]), res[2 + 2 * n]


def chip_exchange_wait(send_sems, recv_sems, sums, landed, after, *, name):
    n = len(sums)

    def body(*refs):
        src, dst = refs[:n], refs[n:2 * n]
        send_sems, recv_sems = refs[2 * n], refs[2 * n + 1]
        x, y, c = _place()
        chips = _other_chips(x, y)
        for a in range(n):
            for j in range(3):
                cp = _exchange_copy(src[a], dst[a], a, j, c, chips, send_sems, recv_sems)
                cp.wait_send()
                cp.wait_recv()

    res = pl.pallas_call(
        body, name=name, in_specs=[HBM] * (2 * n) + [SEM, SEM, pl.BlockSpec(memory_space=pl.ANY)],
        out_specs=[HBM] * (2 * n), out_shape=[pltpu.HBM(b.shape, b.dtype) for b in sums + landed],
        input_output_aliases={a: a for a in range(2 * n)},
        compiler_params=pltpu.CompilerParams(has_side_effects=DATAFLOW),
    )(*sums, *landed, send_sems, recv_sems, after)
    return list(res[:n]), list(res[n:])


def pair_reduce(grads, c, tag):
    c_idx = c.astype(jnp.int32).reshape(1)
    others = pair_exchange(grads, name=f"grad_pair_exchange_{tag}")
    return [pair_add(g, o, c_idx, name=f"grad_pair_add_{tag}{a}") for a, (g, o) in enumerate(zip(grads, others))]


def finish_reduce(sums, landed, q, c, tag):
    qc_idx = jnp.stack([q, c]).astype(jnp.int32)
    return [chip_sum(s, l, qc_idx, name=f"grad_chip_sum_{tag}{a}") for a, (s, l) in enumerate(zip(sums, landed))]


def _ffn_forward(x, mod, pre_g, post_g, w_up, w_down, dw_w, dw_b, tag):
    sh, sc, gate = mod
    h = prenorm(x, pre_g, sc, sh, name=f"{tag}_prenorm")
    u0 = mm_nn(h, w_up, name=f"{tag}_up", out_dtype=BF16, perm=_ffn_perm)
    z, ab = ffn_act(u0, dw_w, dw_b, name=f"{tag}_act")
    y = mm_nn(z, w_down, name=f"{tag}_down")
    x_new = post_residual(x, y, post_g, gate, name=f"{tag}_post")
    return x_new, (x, h, u0, ab, z, y)


def _ffn_backward(dx, saved, mod, pre_g, post_g, w_up, w_down, dw_w, dw_b, tag):
    x, h, u0, ab, z, y = saved
    sh, sc, gate = mod
    dy, dgate, dpost, _ = post_bwd(dx, y, post_g, gate, name=f"{tag}_post_bwd")
    dz = mm_nt(dy, w_down, name=f"{tag}_down_dx", out_dtype=BF16)
    g_down = mm_tn(z, dy, name=f"{tag}_down_dw", J=2, block="a", row_chips=2)
    du0, dconv = ffn_act_bwd(dz, u0, ab, dw_w, name=f"{tag}_act_bwd")
    dh = mm_nt(du0, w_up, name=f"{tag}_up_dx", perm=_ffn_perm)
    g_up = mm_tn(h, du0, name=f"{tag}_up_dw", J=4, block="b", perm=_ffn_perm)
    dx_in, dsh, dsc, dpre = prenorm_bwd(dh, x, dx, pre_g, sc, name=f"{tag}_prenorm_bwd")
    nb = u0.shape[1] // 4
    dconv = dconv[:, 0].reshape(4, 2, 2, nb).transpose(0, 2, 1, 3).reshape(4, 4 * nb)
    return dx_in, dict(dsh=dsh, dsc=dsc, dgate=dgate, dpre=dpre, dpost=dpost, g_up=g_up, g_down=g_down,
                       d_dw_w=dconv[0:FFN_W], d_dw_b=dconv[3:4])


def _local_step(x, tgt, mods, P, late_weights=None, grads_ready=None):
    m0, m1 = mods
    h1 = prenorm(x, P["pre_mix_g"][0:1], m0[1], m0[0], name="hgrn_prenorm")
    proj = mm_nn(h1, P["hgrn_w_in"], name="hgrn_in")
    o, states = hgrn_scan(proj, P["hgrn_lb_logits"], name="hgrn_scan")
    og = hgrn_gate(o, proj, P["hgrn_gnorm_g"], name="hgrn_gate")
    y1 = mm_nn(og, P["hgrn_w_out"], name="hgrn_out")
    x1 = post_residual(x, y1, P["post_mix_g"][0:1], m0[2], name="hgrn_post")
    if late_weights is not None:
        P = {**P, **late_weights(x1)}
    x2, ffn0 = _ffn_forward(x1, m0[3:6], P["pre_ffn_g"][0:1], P["post_ffn_g"][0:1], P["ffn_w_up"][0],
                            P["ffn_w_down"][0], P["ffn_dw_w"][0], P["ffn_dw_b"][0:1], "ffn0")
    h3 = prenorm(x2, P["pre_mix_g"][1:2], m1[1], m1[0], name="conv_prenorm")
    u = mm_nn(h3, P["conv_w_in"], name="conv_in", bias=P["conv_b_in"])
    s, cv = conv_act(u, P["conv_dw_w"], P["conv_dw_b"], P["conv_ln_g"], P["conv_ln_b"], name="conv_act")
    y3 = mm_nn(s, P["conv_w_out"], name="conv_out", bias=P["conv_b_out"])
    x3 = post_residual(x2, y3, P["post_mix_g"][1:2], m1[2], name="conv_post")
    x4, ffn1 = _ffn_forward(x3, m1[3:6], P["pre_ffn_g"][1:2], P["post_ffn_g"][1:2], P["ffn_w_up"][1],
                            P["ffn_w_down"][1], P["ffn_dw_w"][1], P["ffn_dw_b"][1:2], "ffn1")
    dx4, lcols = loss_grad(x4, tgt, name="loss")
    dx3, f1 = _ffn_backward(dx4, ffn1, m1[3:6], P["pre_ffn_g"][1:2], P["post_ffn_g"][1:2], P["ffn_w_up"][1],
                            P["ffn_w_down"][1], P["ffn_dw_w"][1], P["ffn_dw_b"][1:2], "ffn1")
    dy3, dg1_1, dpostmix1, d_b_out = post_bwd(dx3, y3, P["post_mix_g"][1:2], m1[2], name="conv_post_bwd")
    ds = mm_nt(dy3, P["conv_w_out"], name="conv_out_dx")
    g_conv_out = mm_tn(s, dy3, name="conv_out_dw", J=1, block="a", row_chips=4)
    dcv, d_ln_g, d_ln_b, d_dw_b = conv_norm_bwd(ds, cv, P["conv_ln_g"], P["conv_ln_b"], name="conv_norm_bwd")
    du, d_dw_w, d_b_in = conv_glu_bwd(dcv, u, P["conv_dw_w"], name="conv_glu_bwd")
    dh3 = mm_nt(du, P["conv_w_in"], name="conv_in_dx")
    g_conv_in = mm_tn(h3, du, name="conv_in_dw", J=2, block="b", col_chips=2)
    dx2, dsh1_1, dsc1_1, dpremix1 = prenorm_bwd(dh3, x2, dx3, P["pre_mix_g"][1:2], m1[1], name="conv_prenorm_bwd")
    if grads_ready is not None:
        token = grads_ready("l1", [g_conv_in, g_conv_out, f1["g_up"], f1["g_down"]])
        m0 = tuple(m + token[0:1, 0:1] for m in m0)
    dx1, f0 = _ffn_backward(dx2, ffn0, m0[3:6], P["pre_ffn_g"][0:1], P["post_ffn_g"][0:1], P["ffn_w_up"][0],
                            P["ffn_w_down"][0], P["ffn_dw_w"][0], P["ffn_dw_b"][0:1], "ffn0")
    if grads_ready is not None:
        token = grads_ready("f0", [f0["g_up"], f0["g_down"]])
        m0 = tuple(m + token[0:1, 0:1] for m in m0)
    dy1, dg1_0, dpostmix0, _ = post_bwd(dx1, y1, P["post_mix_g"][0:1], m0[2], name="hgrn_post_bwd")
    dog = mm_nt(dy1, P["hgrn_w_out"], name="hgrn_out_dx")
    g_hgrn_out = mm_tn(og, dy1, name="hgrn_out_dw", J=1, block="a", row_chips=4)
    do, dgp, d_gn = hgrn_gate_bwd(dog, o, proj, P["hgrn_gnorm_g"], name="hgrn_gate_bwd")
    dqp, dfz, dv, dlb = hgrn_scan_bwd(proj, P["hgrn_lb_logits"], states, do, name="hgrn_scan_bwd")
    dproj = jnp.concatenate([dqp, dfz, dv, dgp], axis=1)
    dh1 = mm_nt(dproj, P["hgrn_w_in"], name="hgrn_in_dx")
    g_hgrn_in = mm_tn(h1, dproj, name="hgrn_in_dw", J=4, block="b")
    dx0, dsh1_0, dsc1_0, dpremix0 = prenorm_bwd(dh1, x, dx1, P["pre_mix_g"][0:1], m0[1], name="hgrn_prenorm_bwd")

    dmod = jnp.stack([
        jnp.concatenate([dsh1_0, dsc1_0, dg1_0, f0["dsh"], f0["dsc"], f0["dgate"]], axis=1)[0],
        jnp.concatenate([dsh1_1, dsc1_1, dg1_1, f1["dsh"], f1["dsc"], f1["dgate"]], axis=1)[0]])
    small = dict(
        loss=lcols,
        pre_mix_g=jnp.concatenate([dpremix0, dpremix1]), post_mix_g=jnp.concatenate([dpostmix0, dpostmix1]),
        pre_ffn_g=jnp.concatenate([f0["dpre"], f1["dpre"]]), post_ffn_g=jnp.concatenate([f0["dpost"], f1["dpost"]]),
        lb=dlb, hgrn_gnorm_g=d_gn, ffn_dw_b=jnp.concatenate([f0["d_dw_b"], f1["d_dw_b"]]), dmod=dmod,
        conv_b_in=d_b_in, conv_dw_w=d_dw_w[0:CONV_W], conv_dw_b=d_dw_b, conv_ln_g=d_ln_g, conv_ln_b=d_ln_b,
        conv_b_out=d_b_out, ffn_dw_w=jnp.stack([f0["d_dw_w"], f1["d_dw_w"]]))
    big = [g_hgrn_in, g_hgrn_out, g_conv_in, g_conv_out, f0["g_up"], f1["g_up"], f0["g_down"], f1["g_down"]]
    return dx0, small, big


def _pack(parts, rows=8):
    flat = jnp.concatenate([p.reshape(-1).astype(F32) for p in parts])
    per = rows * 128
    pad = (-flat.shape[0]) % per
    return jnp.pad(flat, (0, pad)).reshape(rows, -1)


def _unpack(flat, shapes):
    out, off = [], 0
    for s in shapes:
        n = 1
        for d in s:
            n *= d
        out.append(flat[..., off:off + n].reshape(flat.shape[:-1] + tuple(s)))
        off += n
    return out


def _from_chips(stacked, axis):
    moved = jnp.moveaxis(stacked, 0, axis)
    shape = list(moved.shape)
    return moved.reshape(shape[:axis] + [shape[axis] * shape[axis + 1]] + shape[axis + 2:])


def _my_shard(full, axis, q):
    n = full.shape[axis] // N_CHIPS
    return lax.dynamic_slice_in_dim(full, q * n, n, axis=axis)


def kernel(x, c, ada_w, ada_b, pre_mix_g, post_mix_g, pre_ffn_g, post_ffn_g, hgrn_w_in, hgrn_lb_logits, hgrn_gnorm_g, hgrn_w_out, conv_w_in, conv_b_in, conv_dw_w, conv_dw_b, conv_ln_g, conv_ln_b, conv_w_out, conv_b_out, ffn_w_up, ffn_dw_w, ffn_dw_b, ffn_w_down, loss_target, m_ada_w, m_ada_b, m_pre_mix_g, m_post_mix_g, m_pre_ffn_g, m_post_ffn_g, m_hgrn_w_in, m_hgrn_lb_logits, m_hgrn_gnorm_g, m_hgrn_w_out, m_conv_w_in, m_conv_b_in, m_conv_dw_w, m_conv_dw_b, m_conv_ln_g, m_conv_ln_b, m_conv_w_out, m_conv_b_out, m_ffn_w_up, m_ffn_dw_w, m_ffn_dw_b, m_ffn_w_down, v_ada_w, v_ada_b, v_pre_mix_g, v_post_mix_g, v_pre_ffn_g, v_post_ffn_g, v_hgrn_w_in, v_hgrn_lb_logits, v_hgrn_gnorm_g, v_hgrn_w_out, v_conv_w_in, v_conv_b_in, v_conv_dw_w, v_conv_dw_b, v_conv_ln_g, v_conv_ln_b, v_conv_w_out, v_conv_b_out, v_ffn_w_up, v_ffn_dw_w, v_ffn_dw_b, v_ffn_w_down):
    W = dict(ada_w=ada_w, ada_b=ada_b, pre_mix_g=pre_mix_g, post_mix_g=post_mix_g, pre_ffn_g=pre_ffn_g,
             post_ffn_g=post_ffn_g, hgrn_w_in=hgrn_w_in, hgrn_lb_logits=hgrn_lb_logits, hgrn_gnorm_g=hgrn_gnorm_g,
             hgrn_w_out=hgrn_w_out, conv_w_in=conv_w_in, conv_b_in=conv_b_in, conv_dw_w=conv_dw_w,
             conv_dw_b=conv_dw_b, conv_ln_g=conv_ln_g, conv_ln_b=conv_ln_b, conv_w_out=conv_w_out,
             conv_b_out=conv_b_out, ffn_w_up=ffn_w_up, ffn_dw_w=ffn_dw_w, ffn_dw_b=ffn_dw_b, ffn_w_down=ffn_w_down)
    M = dict(ada_w=m_ada_w, ada_b=m_ada_b, pre_mix_g=m_pre_mix_g, post_mix_g=m_post_mix_g, pre_ffn_g=m_pre_ffn_g,
             post_ffn_g=m_post_ffn_g, hgrn_w_in=m_hgrn_w_in, hgrn_lb_logits=m_hgrn_lb_logits,
             hgrn_gnorm_g=m_hgrn_gnorm_g, hgrn_w_out=m_hgrn_w_out, conv_w_in=m_conv_w_in, conv_b_in=m_conv_b_in,
             conv_dw_w=m_conv_dw_w, conv_dw_b=m_conv_dw_b, conv_ln_g=m_conv_ln_g, conv_ln_b=m_conv_ln_b,
             conv_w_out=m_conv_w_out, conv_b_out=m_conv_b_out, ffn_w_up=m_ffn_w_up, ffn_dw_w=m_ffn_dw_w,
             ffn_dw_b=m_ffn_dw_b, ffn_w_down=m_ffn_w_down)
    V = dict(ada_w=v_ada_w, ada_b=v_ada_b, pre_mix_g=v_pre_mix_g, post_mix_g=v_post_mix_g, pre_ffn_g=v_pre_ffn_g,
             post_ffn_g=v_post_ffn_g, hgrn_w_in=v_hgrn_w_in, hgrn_lb_logits=v_hgrn_lb_logits,
             hgrn_gnorm_g=v_hgrn_gnorm_g, hgrn_w_out=v_hgrn_w_out, conv_w_in=v_conv_w_in, conv_b_in=v_conv_b_in,
             conv_dw_w=v_conv_dw_w, conv_dw_b=v_conv_dw_b, conv_ln_g=v_conv_ln_g, conv_ln_b=v_conv_ln_b,
             conv_w_out=v_conv_w_out, conv_b_out=v_conv_b_out, ffn_w_up=v_ffn_w_up, ffn_dw_w=v_ffn_dw_w,
             ffn_dw_b=v_ffn_dw_b, ffn_w_down=v_ffn_w_down)
    names = list(W)
    xi, yi, ci = lax.axis_index("x"), lax.axis_index("y"), lax.axis_index("c")
    q = 2 * xi + yi
    me = 2 * q + ci
    D = x.shape[-1]
    L = ada_w.shape[0]

    small_w = ["conv_b_in", "conv_dw_w", "conv_dw_b", "conv_ln_g", "conv_ln_b", "conv_b_out", "ffn_dw_w"]
    small_axis = dict(conv_b_in=1, conv_dw_w=2, conv_dw_b=1, conv_ln_g=1, conv_ln_b=1, conv_b_out=1, ffn_dw_w=2)
    packed = _pack([c] + [W[n] for n in small_w])

    def halves(w):
        shard = w.astype(BF16).reshape(1, 2, w.shape[0] // 2, w.shape[1])
        buf = lax.empty((N_CHIPS,) + shard.shape[1:], BF16)
        return lax.dynamic_update_slice_in_dim(buf, shard, q, axis=0)

    hg_send, hg_recv, hg_bufs, hg_token = allgather_chips_start([halves(hgrn_w_in[0]), halves(hgrn_w_out[0])],
                                                                name="gather_hgrn_weights_start")
    packed, _ = lax.optimization_barrier((packed, hg_token))
    gathered = allgather_devices(packed, name="gather_small_params").reshape(N_DEV, -1)
    c_all = gathered[:, 0:D]
    per_chip = gathered.reshape(N_CHIPS, 2, -1)[:, 0, D:]
    parts = _unpack(per_chip, [W[n].shape for n in small_w])
    P = {n: _from_chips(p, small_axis[n]) for n, p in zip(small_w, parts)}
    P["conv_dw_w"] = P["conv_dw_w"][0]
    for n in ("pre_mix_g", "post_mix_g", "pre_ffn_g", "post_ffn_g", "hgrn_lb_logits", "hgrn_gnorm_g", "ffn_dw_b"):
        P[n] = W[n]

    modp = ada_mod(c_all, ada_w, name="ada_mod")
    ncol = modp.shape[-1]
    mod_all = allgather_devices(modp.reshape(L * N_DEV, ncol), name="gather_mod")
    mod_all = mod_all.reshape(N_CHIPS, 2, L, N_DEV, ncol)[:, 0]
    mod_me = lax.dynamic_index_in_dim(mod_all, me, axis=2, keepdims=False)
    mod = mod_me.transpose(1, 0, 2).reshape(L, N_CHIPS * ncol) + ada_b
    mods = [tuple(mod[l:l + 1, k * D:(k + 1) * D] for k in range(6)) for l in range(L)]

    stack = lambda t: t.reshape(N_CHIPS, t.shape[1] * t.shape[2], t.shape[3])
    rowsh = lambda t: t.reshape(1, N_CHIPS * t.shape[1] * t.shape[2], t.shape[3])
    pairs = lambda t: t.reshape(2, 2, t.shape[1], t.shape[2]).transpose(0, 2, 1, 3).reshape(2, t.shape[1], 2 * t.shape[2])
    g = forward_to_sibling(allgather_chips_wait(hg_send, hg_recv, hg_bufs, mod, name="gather_hgrn_weights_wait"),
                           name="gather_hgrn_weights_forward")
    P["hgrn_w_in"], P["hgrn_w_out"] = stack(g[0]), rowsh(g[1])
    late_shards = [conv_w_in[0], conv_w_out[0], ffn_w_up[0], ffn_w_up[1], ffn_w_down[0], ffn_w_down[1]]
    late_bufs, _, _ = lax.optimization_barrier(([halves(w) for w in late_shards], g, mod))
    send_sems, recv_sems, bufs, token = allgather_chips_start(late_bufs, name="gather_weights_start")
    mods[0] = tuple(m + token[0:1, 0:1] for m in mods[0])

    def late_weights(x1):
        landed = allgather_chips_wait(send_sems, recv_sems, bufs, x1, name="gather_weights_wait")
        g = forward_to_sibling(landed, name="gather_weights_forward")
        return dict(conv_w_in=pairs(stack(g[0])), conv_w_out=rowsh(g[1]), ffn_w_up=[stack(g[2]), stack(g[3])],
                    ffn_w_down=[rowsh(g[4]), rowsh(g[5])])

    in_flight = {}

    def grads_ready(tag, grads):
        sums = pair_reduce(grads, ci, f"{tag}_")
        send, recv, sums, landing, tok = chip_exchange_start(sums, name=f"grad_chip_exchange_start_{tag}")
        in_flight[tag] = (send, recv, sums, landing)
        return tok

    grad_x, small, big = _local_step(x[0], loss_target[0], mods, P, late_weights, grads_ready)

    small_names = list(small)
    gs = allgather_devices(_pack([small[n] for n in small_names]), name="gather_small_grads")
    dmod_all = _unpack(gs.reshape(N_DEV, -1), [small[n].shape for n in small_names])[small_names.index("dmod")]
    tot = sum_devices(gs, name="sum_small_grads").reshape(1, -1)
    S = dict(zip(small_names, _unpack(tot, [small[n].shape for n in small_names])))
    S = {n: v[0] for n, v in S.items()}
    loss = 0.5 * jnp.sum(S["loss"]) / D

    G = {}
    dmod_q = lax.dynamic_slice_in_dim(dmod_all, q * ncol, ncol, axis=2)
    G["ada_w"] = ada_wgrad(c_all.T, dmod_q.transpose(1, 0, 2), name="ada_wgrad")
    G["ada_b"] = S["dmod"]
    for n in ("pre_mix_g", "post_mix_g", "pre_ffn_g", "post_ffn_g", "hgrn_gnorm_g", "ffn_dw_b"):
        G[n] = S[n]
    G["hgrn_lb_logits"] = lb_logits_grad(hgrn_lb_logits, S["lb"], name="lb_logits_grad")
    G["conv_b_in"] = _my_shard(S["conv_b_in"], 1, q)
    G["conv_dw_w"] = _my_shard(S["conv_dw_w"], 1, q)[None]
    for n in ("conv_dw_b", "conv_ln_g", "conv_ln_b", "conv_b_out"):
        G[n] = _my_shard(S[n], 1, q)
    G["ffn_dw_w"] = _my_shard(S["ffn_dw_w"], 2, q)

    sums_h = pair_reduce([big[0], big[1]], ci, "hg_")
    send_h, recv_h, sums_h, landing_h, token = chip_exchange_start(sums_h, name="grad_chip_exchange_start_hg")
    halves = []
    for tag in ("f0", "l1"):
        sums_t, landed_t = chip_exchange_wait(*in_flight[tag], grad_x, name=f"grad_chip_exchange_wait_{tag}")
        halves += finish_reduce(sums_t, landed_t, q, ci, f"{tag}_")
    halves, _ = lax.optimization_barrier((halves, token))
    red = [f.reshape(2 * f.shape[1], f.shape[2]) for f in half_swap(halves, name="grad_half_swap")]
    G["conv_w_in"], G["conv_w_out"] = red[2][None], red[3][None]
    G["ffn_w_up"] = jnp.stack([red[0], red[4]])
    G["ffn_w_down"] = jnp.stack([red[1], red[5]])

    delta, new_m, new_v = {}, {}, {}

    def adamw_matrix(n):
        shp = W[n].shape
        two = lambda t: t.reshape(-1, shp[-1])
        d_, m_, v_ = adamw(two(W[n]), two(G[n]), two(M[n]), two(V[n]), name=f"adamw_{n}")
        delta[n], new_m[n], new_v[n] = d_.reshape(shp), m_.reshape(shp), v_.reshape(shp)

    big_names = ["ada_w", "hgrn_w_in", "hgrn_w_out", "conv_w_in", "conv_w_out", "ffn_w_up", "ffn_w_down"]
    for n in ("ada_w", "conv_w_in", "conv_w_out", "ffn_w_up", "ffn_w_down"):
        adamw_matrix(n)
    sums_h, landed_h = chip_exchange_wait(send_h, recv_h, sums_h, landing_h, delta["ffn_w_down"],
                                          name="grad_chip_exchange_wait_hg")
    red_h = half_swap(finish_reduce(sums_h, landed_h, q, ci, "hg_"), name="grad_half_swap_hg")
    G["hgrn_w_in"], G["hgrn_w_out"] = [f.reshape(1, 2 * f.shape[1], f.shape[2]) for f in red_h]
    for n in ("hgrn_w_in", "hgrn_w_out"):
        adamw_matrix(n)
    rest = [n for n in names if n not in big_names]
    d_, m_, v_ = adamw(_pack([W[n] for n in rest]), _pack([G[n] for n in rest]), _pack([M[n] for n in rest]),
                       _pack([V[n] for n in rest]), name="adamw_small")
    shapes = [W[n].shape for n in rest]
    for n, a, b_, c_ in zip(rest, _unpack(d_.reshape(-1), shapes), _unpack(m_.reshape(-1), shapes),
                            _unpack(v_.reshape(-1), shapes)):
        delta[n], new_m[n], new_v[n] = a, b_, c_

    return (loss, grad_x[None], *[G[n].reshape(W[n].shape) for n in names], *[delta[n] for n in names],
            *[new_m[n] for n in names], *[new_v[n] for n in names])
```

```python
import jax
import jax.numpy as jnp
from jax import lax
from jax.experimental import pallas as pl
from jax.experimental.pallas import tpu as pltpu

F32 = jnp.float32
BF16 = jnp.bfloat16
EPS = 1e-6
HEAD = 128
BLK = 16
NEG = -1e30
CONV_W = 31
FFN_W = 3
N_CHIPS = 4
N_DEV = 8
SUB = 8
LANE = 128
V7X_VMEM_LIMIT = 56 * 1024 * 1024
MESH = pl.DeviceIdType.MESH
HBM = pl.BlockSpec(memory_space=pltpu.HBM)
VMEM_SPEC = pl.BlockSpec(memory_space=pltpu.VMEM)

ADAM_LR = 0.001
ADAM_B1 = 0.9
ADAM_B2 = 0.999
ADAM_EPS = 1e-08
ADAM_WD = 0.01
ADAM_STEP = 10


def _cp(*sem):
    return pltpu.CompilerParams(dimension_semantics=sem, vmem_limit_bytes=V7X_VMEM_LIMIT)


def _sig(x):
    return 0.5 * jnp.tanh(0.5 * x) + 0.5


def _silu(x):
    return x * _sig(x)


def _dsilu(x):
    s = _sig(x)
    return s * (1.0 + x * (1.0 - s))


def _dot(a, b):
    return jnp.dot(a, b, preferred_element_type=F32)


def _dot_nt(a, b):
    return lax.dot_general(a, b, (((1,), (1,)), ((), ())), preferred_element_type=F32)


def _dot_tn(a, b):
    return lax.dot_general(a, b, (((0,), (0,)), ((), ())), preferred_element_type=F32)


def _colsum(x):
    return jnp.sum(x, axis=0, keepdims=True)


def _rowmean(x):
    return jnp.mean(x, axis=-1, keepdims=True)


def _ffn_perm(j):
    return (j % 2) * 2 + j // 2


def _tile(n, pref):
    if n <= pref:
        return n
    t = pref - pref % 8
    while n % t:
        t -= 8
    return t


def mm_nn(a, w, *, name, bias=None, out_dtype=F32, perm=None, tm=1024):
    T, K = a.shape
    J, _, nb = w.shape
    tm = min(tm, T)
    col = (lambda j: j) if perm is None else perm

    def body(a_ref, w_ref, *rest):
        acc = _dot(a_ref[...], w_ref[...])
        if bias is not None:
            acc = acc + rest[0][...]
        rest[-1][...] = acc.astype(out_dtype)

    in_specs = [pl.BlockSpec((tm, K), lambda j, i: (i, 0)), pl.BlockSpec((None, K, nb), lambda j, i: (j, 0, 0))]
    args = [a, w]
    if bias is not None:
        in_specs.append(pl.BlockSpec((1, nb), lambda j, i: (0, j)))
        args.append(bias)
    return pl.pallas_call(
        body, grid=(J, T // tm), in_specs=in_specs,
        out_specs=pl.BlockSpec((tm, nb), lambda j, i: (i, col(j))),
        out_shape=jax.ShapeDtypeStruct((T, J * nb), out_dtype), name=name,
        compiler_params=_cp("parallel", "parallel"))(*args)


def mm_nt(a, w, *, name, out_dtype=F32, perm=None, tm=1024, after=None):
    T = a.shape[0]
    J, K, nb = w.shape
    tm = min(tm, T)
    col = (lambda j: j) if perm is None else perm
    deps = [] if after is None else [after]

    def body(a_ref, w_ref, *rest):
        o_ref, acc_ref = rest[len(deps):]
        j = pl.program_id(1)

        @pl.when(j == 0)
        def _():
            acc_ref[...] = jnp.zeros_like(acc_ref)

        acc_ref[...] += _dot_nt(a_ref[...], w_ref[...])

        @pl.when(j == J - 1)
        def _():
            o_ref[...] = acc_ref[...].astype(out_dtype)

    return pl.pallas_call(
        body, grid=(T // tm, J),
        in_specs=[pl.BlockSpec((tm, nb), lambda i, j: (i, col(j))), pl.BlockSpec((None, K, nb), lambda i, j: (j, 0, 0))]
        + [pl.BlockSpec(memory_space=pl.ANY)] * len(deps),
        out_specs=pl.BlockSpec((tm, K), lambda i, j: (i, 0)),
        out_shape=jax.ShapeDtypeStruct((T, K), out_dtype),
        scratch_shapes=[pltpu.VMEM((tm, K), F32)], name=name,
        compiler_params=_cp("parallel", "arbitrary"))(a, w, *deps)


def mm_tn(a, b, *, name, J, block, row_chips=1, col_chips=1, perm=None, tk=1024):
    T = a.shape[0]
    tk = min(tk, T)
    col = (lambda j: j) if perm is None else perm
    if block == "b":
        rows, nb = a.shape[1], b.shape[1] // J
        a_spec = pl.BlockSpec((tk, rows), lambda j, t: (t, 0))
        b_spec = pl.BlockSpec((tk, nb), lambda j, t: (t, col(j)))
    else:
        rows, nb = a.shape[1] // J, b.shape[1]
        a_spec = pl.BlockSpec((tk, rows), lambda j, t: (t, col(j)))
        b_spec = pl.BlockSpec((tk, nb), lambda j, t: (t, 0))
    rh = rows // (2 * row_chips)
    nc = nb // col_chips
    chips = [(rc, cc) for rc in range(row_chips) for cc in range(col_chips)]

    def body(a_ref, b_ref, o_ref):
        @pl.when(pl.program_id(1) == 0)
        def _():
            o_ref[...] = jnp.zeros_like(o_ref)

        acc = _dot_tn(a_ref[...], b_ref[...])
        for ch, (rc, cc) in enumerate(chips):
            for hf in range(2):
                r0 = (rc * 2 + hf) * rh
                o_ref[hf, ch] += acc[r0:r0 + rh, cc * nc:(cc + 1) * nc]

    return pl.pallas_call(
        body, grid=(J, T // tk), in_specs=[a_spec, b_spec],
        out_specs=pl.BlockSpec((2, len(chips), rh, nc), lambda j, t: (0, j, 0, 0)),
        out_shape=jax.ShapeDtypeStruct((2, J * len(chips), rh, nc), F32), name=name,
        compiler_params=_cp("parallel", "arbitrary"))(a, b)


def _row(tm, w):
    return pl.BlockSpec((tm, w), lambda i: (i, 0))


def _full(r, w):
    return pl.BlockSpec((r, w), lambda i: (0, 0))


def _acc_init(i, *refs):
    @pl.when(i == 0)
    def _():
        for r in refs:
            r[...] = jnp.zeros_like(r)


def prenorm(x, g, sc, sh, *, name, tm=512):
    T, D = x.shape
    tm = min(tm, T)

    def body(x_ref, g_ref, sc_ref, sh_ref, h_ref):
        xv = x_ref[...]
        r = lax.rsqrt(_rowmean(xv * xv) + EPS)
        h_ref[...] = ((xv * r) * g_ref[...] * (1.0 + sc_ref[...]) + sh_ref[...]).astype(BF16)

    return pl.pallas_call(
        body, grid=(T // tm,), in_specs=[_row(tm, D), _full(1, D), _full(1, D), _full(1, D)],
        out_specs=_row(tm, D), out_shape=jax.ShapeDtypeStruct((T, D), BF16), name=name,
        compiler_params=_cp("parallel"))(x, g, sc, sh)


def post_residual(x, y, g, gate, *, name, tm=512):
    T, D = x.shape
    tm = min(tm, T)

    def body(x_ref, y_ref, g_ref, gate_ref, o_ref):
        yv = y_ref[...]
        r = lax.rsqrt(_rowmean(yv * yv) + EPS)
        o_ref[...] = x_ref[...] + gate_ref[...] * ((yv * r) * g_ref[...])

    return pl.pallas_call(
        body, grid=(T // tm,), in_specs=[_row(tm, D), _row(tm, D), _full(1, D), _full(1, D)],
        out_specs=_row(tm, D), out_shape=jax.ShapeDtypeStruct((T, D), F32), name=name,
        compiler_params=_cp("parallel"))(x, y, g, gate)


def loss_grad(x, tgt, *, name, tm=512):
    T, D = x.shape
    tm = min(tm, T)

    def body(x_ref, t_ref, dx_ref, l_ref):
        _acc_init(pl.program_id(0), l_ref)
        e = x_ref[...] - t_ref[...]
        dx_ref[...] = e * (1.0 / D)
        l_ref[...] += _colsum(e * e)

    return pl.pallas_call(
        body, grid=(T // tm,), in_specs=[_row(tm, D), _row(tm, D)],
        out_specs=[_row(tm, D), _full(1, D)],
        out_shape=[jax.ShapeDtypeStruct((T, D), F32), jax.ShapeDtypeStruct((1, D), F32)], name=name,
        compiler_params=_cp("arbitrary"))(x, tgt)


def post_bwd(dx, y, g, gate, *, name, tm=512):
    T, D = dx.shape
    tm = min(tm, T)

    def body(dx_ref, y_ref, g_ref, gate_ref, dy_ref, dgate_ref, dg_ref, dbias_ref):
        _acc_init(pl.program_id(0), dgate_ref, dg_ref, dbias_ref)
        yv = y_ref[...]
        dxv = dx_ref[...]
        r = lax.rsqrt(_rowmean(yv * yv) + EPS)
        yn = yv * r
        gv = g_ref[...]
        gt = gate_ref[...]
        dgate_ref[...] += _colsum(dxv * (yn * gv))
        dg_ref[...] += _colsum(dxv * gt * yn)
        dyn = dxv * gt * gv
        dy = r * (dyn - yn * _rowmean(dyn * yn))
        dbias_ref[...] += _colsum(dy)
        dy_ref[...] = dy.astype(BF16)

    return pl.pallas_call(
        body, grid=(T // tm,), in_specs=[_row(tm, D), _row(tm, D), _full(1, D), _full(1, D)],
        out_specs=[_row(tm, D), _full(1, D), _full(1, D), _full(1, D)],
        out_shape=[jax.ShapeDtypeStruct((T, D), BF16)] + [jax.ShapeDtypeStruct((1, D), F32)] * 3, name=name,
        compiler_params=_cp("arbitrary"))(dx, y, g, gate)


def prenorm_bwd(dh, x, dres, g, sc, *, name, tm=512):
    T, D = x.shape
    tm = min(tm, T)

    def body(dh_ref, x_ref, dres_ref, g_ref, sc_ref, dx_ref, dsh_ref, dsc_ref, dg_ref):
        _acc_init(pl.program_id(0), dsh_ref, dsc_ref, dg_ref)
        xv = x_ref[...]
        dhv = dh_ref[...]
        r = lax.rsqrt(_rowmean(xv * xv) + EPS)
        xn = xv * r
        gv = g_ref[...]
        one_sc = 1.0 + sc_ref[...]
        dsh_ref[...] += _colsum(dhv)
        dsc_ref[...] += _colsum(dhv * (xn * gv))
        dg_ref[...] += _colsum(dhv * one_sc * xn)
        dxn = dhv * one_sc * gv
        dx_ref[...] = dres_ref[...] + r * (dxn - xn * _rowmean(dxn * xn))

    return pl.pallas_call(
        body, grid=(T // tm,), in_specs=[_row(tm, D), _row(tm, D), _row(tm, D), _full(1, D), _full(1, D)],
        out_specs=[_row(tm, D), _full(1, D), _full(1, D), _full(1, D)],
        out_shape=[jax.ShapeDtypeStruct((T, D), F32)] + [jax.ShapeDtypeStruct((1, D), F32)] * 3, name=name,
        compiler_params=_cp("arbitrary"))(dh, x, dres, g, sc)


HALO = 16


def _shift_helpers():
    rid = lax.broadcasted_iota(jnp.int32, (SUB, LANE), 0)

    def down(cur, prev, k):
        return pltpu.roll(jnp.where(rid >= SUB - k, prev, cur), k, 0)

    def up(cur, nxt, k):
        return pltpu.roll(jnp.where(rid < k, nxt, cur), SUB - k, 0)

    return down, up


def _ffn_sides(c, nb, wa_ref, wb_ref, ba_ref, bb_ref):
    cols = slice(c * LANE, (c + 1) * LANE)
    return [(cols, [wa_ref[k:k + 1, cols] for k in range(FFN_W)], ba_ref[:, cols]),
            (slice(nb + c * LANE, nb + (c + 1) * LANE), [wb_ref[k:k + 1, cols] for k in range(FFN_W)],
             bb_ref[:, cols])]


def _ffn_specs(tm, nb, hb, idx):
    return [pl.BlockSpec((tm, 2 * nb), lambda jc, i: (idx(i), jc)),
            pl.BlockSpec((HALO, 2 * nb), lambda jc, i: (jnp.maximum(idx(i) * hb - 1, 0), jc)),
            pl.BlockSpec((FFN_W, nb), lambda jc, i: (0, jc)),
            pl.BlockSpec((FFN_W, nb), lambda jc, i: (0, jc + 2)),
            pl.BlockSpec((1, nb), lambda jc, i: (0, jc)),
            pl.BlockSpec((1, nb), lambda jc, i: (0, jc + 2))]


def ffn_act(u0p, dw_w, dw_b, *, name, tm=256):
    T, W = u0p.shape
    nb = W // 4
    tm = min(tm, T)
    unroll = 4
    rows16 = 2 * SUB

    def body(u_ref, halo_ref, wa_ref, wb_ref, ba_ref, bb_ref, z_ref, ab_ref):
        i = pl.program_id(1)
        down, _ = _shift_helpers()
        for c in range(nb // LANE):
            cols = slice(c * LANE, (c + 1) * LANE)
            side = _ffn_sides(c, nb, wa_ref, wb_ref, ba_ref, bb_ref)

            def rows(j, prev):
                prev = list(prev)
                for m in range(unroll):
                    r0 = pl.multiple_of((j * unroll + m) * rows16, rows16)
                    x = [u_ref[pl.ds(r0, rows16), cs].astype(F32) for cs, _, _ in side]
                    conv = [[None, None], [None, None]]
                    for hf in range(2):
                        for n, (_, w, b) in enumerate(side):
                            cur = x[n][hf * SUB:(hf + 1) * SUB, :]
                            conv[n][hf] = b + w[2] * cur + w[1] * down(cur, prev[n], 1) + w[0] * down(cur, prev[n], 2)
                            prev[n] = cur
                    a, b = [jnp.concatenate(conv[n], axis=0) for n in range(2)]
                    z_ref[pl.ds(r0, rows16), cols] = (_silu(a) * b).astype(BF16)
                    ab_ref[pl.ds(r0, rows16), side[0][0]] = a.astype(BF16)
                    ab_ref[pl.ds(r0, rows16), side[1][0]] = b.astype(BF16)
                return tuple(prev)

            first = [jnp.where(i == 0, 0.0, halo_ref[:, cs].astype(F32)[SUB:2 * SUB, :]) for cs, _, _ in side]
            lax.fori_loop(0, tm // (rows16 * unroll), rows, tuple(first))

    return pl.pallas_call(
        body, grid=(2, T // tm), in_specs=_ffn_specs(tm, nb, tm // HALO, lambda i: i),
        out_specs=[pl.BlockSpec((tm, nb), lambda jc, i: (i, jc)), pl.BlockSpec((tm, 2 * nb), lambda jc, i: (i, jc))],
        out_shape=[jax.ShapeDtypeStruct((T, 2 * nb), BF16), jax.ShapeDtypeStruct((T, W), BF16)], name=name,
        compiler_params=_cp("parallel", "arbitrary"))(u0p, u0p, dw_w, dw_w, dw_b, dw_b)


def ffn_act_bwd(dz, u0p, ab, dw_w, *, name, tm=256):
    T, W = u0p.shape
    nb = W // 4
    tm = min(tm, T)
    nt = T // tm
    unroll = 4
    rows16 = 2 * SUB
    n_it = tm // (rows16 * unroll)

    def body(dz_ref, u_ref, ab_ref, wa_ref, wb_ref, du0_ref, dw_ref, carry):
        i = pl.program_id(1)
        _acc_init(i, dw_ref)
        _, up = _shift_helpers()
        for c in range(nb // LANE):
            cols = slice(c * LANE, (c + 1) * LANE)
            side = [(cols, [wa_ref[k:k + 1, cols] for k in range(FFN_W)]),
                    (slice(nb + c * LANE, nb + (c + 1) * LANE), [wb_ref[k:k + 1, cols] for k in range(FFN_W)])]

            def rows(j, st):
                nxt, acc = list(st[0:2]), list(st[2:10])
                for m in range(unroll):
                    r0 = pl.multiple_of(((n_it - 1 - j) * unroll + unroll - 1 - m) * rows16, rows16)
                    dzv = dz_ref[pl.ds(r0, rows16), cols].astype(F32)
                    a, b = [ab_ref[pl.ds(r0, rows16), cs].astype(F32) for cs, _ in side]
                    x = [u_ref[pl.ds(r0, rows16), cs].astype(F32) for cs, _ in side]
                    sa = _sig(a)
                    d16 = [dzv * b * (sa * (1.0 + a * (1.0 - sa))), dzv * (a * sa)]
                    out = [[None, None], [None, None]]
                    for hf in (1, 0):
                        half = slice(hf * SUB, (hf + 1) * SUB)
                        for n in range(2):
                            w = side[n][1]
                            d = d16[n][half, :]
                            u = x[n][half, :]
                            up1, up2 = up(d, nxt[n], 1), up(d, nxt[n], 2)
                            acc[4 * n + 0] = acc[4 * n + 0] + up2 * u
                            acc[4 * n + 1] = acc[4 * n + 1] + up1 * u
                            acc[4 * n + 2] = acc[4 * n + 2] + d * u
                            acc[4 * n + 3] = acc[4 * n + 3] + d
                            out[n][hf] = w[2] * d + w[1] * up1 + w[0] * up2
                            nxt[n] = d
                    for n in range(2):
                        du0_ref[pl.ds(r0, rows16), side[n][0]] = jnp.concatenate(out[n], axis=0).astype(BF16)
                return (*nxt, *acc)

            init = [jnp.where(i == 0, 0.0, carry[:, cs]) for cs, _ in side] + [jnp.zeros((SUB, LANE), F32)] * 8
            st = lax.fori_loop(0, n_it, rows, tuple(init))
            for n in range(2):
                carry[:, side[n][0]] = st[n]
                for k in range(4):
                    dw_ref[k, :, side[n][0]] += st[2 + 4 * n + k]

        @pl.when(i == nt - 1)
        def _():
            for k in range(4):
                dw_ref[k, 0:1, :] = _colsum(dw_ref[k])

    rev = lambda i: nt - 1 - i
    wide = pl.BlockSpec((tm, 2 * nb), lambda jc, i: (rev(i), jc))
    return pl.pallas_call(
        body, grid=(2, nt),
        in_specs=[pl.BlockSpec((tm, nb), lambda jc, i: (rev(i), jc)), wide, wide,
                  pl.BlockSpec((FFN_W, nb), lambda jc, i: (0, jc)), pl.BlockSpec((FFN_W, nb), lambda jc, i: (0, jc + 2))],
        out_specs=[wide, pl.BlockSpec((4, SUB, 2 * nb), lambda jc, i: (0, 0, jc))],
        out_shape=[jax.ShapeDtypeStruct((T, W), BF16), jax.ShapeDtypeStruct((4, SUB, W), F32)],
        scratch_shapes=[pltpu.VMEM((SUB, 2 * nb), F32)], name=name,
        compiler_params=_cp("parallel", "arbitrary"))(dz, u0p, ab, dw_w, dw_w)


CHALO = 32
CCOL = 256


def _phase_copies(buf, shifted, tm):
    n = tm + CHALO - SUB
    for p in range(1, SUB):
        shifted[p - 1, 0:n, :] = buf[p:p + n, :]


def _shifted(buf, shifted, r, tm, c0):
    m, p = divmod(r, SUB)
    src = buf if p == 0 else shifted.at[p - 1]
    return src[m * SUB:m * SUB + tm, c0:c0 + CCOL]


def conv_act(u, dw_w, dw_b, ln_g, ln_b, *, name, tm=128):
    T, D2 = u.shape
    D = D2 // 2
    tm = min(tm, T)
    hb = tm // CHALO

    def body(u_ref, halo_ref, w_ref, b_ref, g_ref, be_ref, s_ref, cv_ref, gbuf, gsh):
        i = pl.program_id(0)
        hv = halo_ref[...]
        gbuf[0:CHALO, :] = jnp.where(i == 0, 0.0, hv[:, 0:D] * _sig(hv[:, D:D2]))
        uv = u_ref[...]
        gbuf[CHALO:CHALO + tm, :] = uv[:, 0:D] * _sig(uv[:, D:D2])
        _phase_copies(gbuf, gsh, tm)
        for c0 in range(0, D, CCOL):
            acc = jnp.zeros((tm, CCOL), F32) + b_ref[:, c0:c0 + CCOL]
            for k in range(CONV_W):
                acc = acc + w_ref[k:k + 1, c0:c0 + CCOL] * _shifted(gbuf, gsh, CHALO - (CONV_W - 1) + k, tm, c0)
            cv_ref[:, c0:c0 + CCOL] = acc
        cv = cv_ref[...]
        mu = _rowmean(cv)
        xc = cv - mu
        nh = xc * lax.rsqrt(_rowmean(xc * xc) + EPS)
        s_ref[...] = _silu(nh * g_ref[...] + be_ref[...]).astype(BF16)

    return pl.pallas_call(
        body, grid=(T // tm,),
        in_specs=[_row(tm, D2), pl.BlockSpec((CHALO, D2), lambda i: (jnp.maximum(i * hb - 1, 0), 0)),
                  _full(CONV_W, D), _full(1, D), _full(1, D), _full(1, D)],
        out_specs=[_row(tm, D), _row(tm, D)],
        out_shape=[jax.ShapeDtypeStruct((T, D), BF16), jax.ShapeDtypeStruct((T, D), F32)],
        scratch_shapes=[pltpu.VMEM((tm + CHALO, D), F32), pltpu.VMEM((SUB - 1, tm + CHALO, D), F32)], name=name,
        compiler_params=_cp("arbitrary"))(u, u, dw_w, dw_b, ln_g, ln_b)


def conv_norm_bwd(ds, cv, ln_g, ln_b, *, name, tm=512):
    T, D = cv.shape
    tm = min(tm, T)

    def body(ds_ref, cv_ref, g_ref, be_ref, dcv_ref, dg_ref, dbe_ref, dcb_ref):
        _acc_init(pl.program_id(0), dg_ref, dbe_ref, dcb_ref)
        cv_ = cv_ref[...]
        mu = _rowmean(cv_)
        xc = cv_ - mu
        rstd = lax.rsqrt(_rowmean(xc * xc) + EPS)
        nh = xc * rstd
        gv = g_ref[...]
        dln = ds_ref[...] * _dsilu(nh * gv + be_ref[...])
        dg_ref[...] += _colsum(dln * nh)
        dbe_ref[...] += _colsum(dln)
        dnh = dln * gv
        dcv = rstd * (dnh - _rowmean(dnh) - nh * _rowmean(dnh * nh))
        dcb_ref[...] += _colsum(dcv)
        dcv_ref[...] = dcv

    return pl.pallas_call(
        body, grid=(T // tm,), in_specs=[_row(tm, D), _row(tm, D), _full(1, D), _full(1, D)],
        out_specs=[_row(tm, D), _full(1, D), _full(1, D), _full(1, D)],
        out_shape=[jax.ShapeDtypeStruct((T, D), F32)] + [jax.ShapeDtypeStruct((1, D), F32)] * 3, name=name,
        compiler_params=_cp("arbitrary"))(ds, cv, ln_g, ln_b)


def conv_glu_bwd(dcv, u, dw_w, *, name, tm=128):
    T, D2 = u.shape
    D = D2 // 2
    tm = min(tm, T)
    nt = T // tm
    hb = tm // CHALO

    def body(dcv_ref, dnext_ref, u_ref, w_ref, du_ref, dw_ref, dbin_ref, dbuf, dsh):
        i = pl.program_id(0)
        _acc_init(i, dw_ref, dbin_ref)
        uv = u_ref[...]
        av = uv[:, 0:D]
        sg = _sig(uv[:, D:D2])
        glu = av * sg
        dbuf[0:tm, :] = dcv_ref[...]
        dbuf[tm:tm + CHALO, :] = jnp.where(i == nt - 1, 0.0, dnext_ref[...])
        _phase_copies(dbuf, dsh, tm)
        for c0 in range(0, D, CCOL):
            glu_c = glu[:, c0:c0 + CCOL]
            acc = jnp.zeros((tm, CCOL), F32)
            for k in range(CONV_W):
                moved = _shifted(dbuf, dsh, CONV_W - 1 - k, tm, c0)
                dw_ref[k:k + 1, c0:c0 + CCOL] += _colsum(moved * glu_c)
                acc = acc + w_ref[k:k + 1, c0:c0 + CCOL] * moved
            a_c = av[:, c0:c0 + CCOL]
            s_c = sg[:, c0:c0 + CCOL]
            da = acc * s_c
            dgt = acc * a_c * s_c * (1.0 - s_c)
            dbin_ref[:, c0:c0 + CCOL] += _colsum(da)
            dbin_ref[:, D + c0:D + c0 + CCOL] += _colsum(dgt)
            du_ref[:, c0:c0 + CCOL] = da.astype(BF16)
            du_ref[:, D + c0:D + c0 + CCOL] = dgt.astype(BF16)

    return pl.pallas_call(
        body, grid=(nt,),
        in_specs=[_row(tm, D), pl.BlockSpec((CHALO, D), lambda i: (jnp.minimum((i + 1) * hb, T // CHALO - 1), 0)),
                  _row(tm, D2), _full(CONV_W, D)],
        out_specs=[_row(tm, D2), _full(CHALO, D), _full(1, D2)],
        out_shape=[jax.ShapeDtypeStruct((T, D2), BF16), jax.ShapeDtypeStruct((CHALO, D), F32),
                   jax.ShapeDtypeStruct((1, D2), F32)],
        scratch_shapes=[pltpu.VMEM((tm + CHALO, D), F32), pltpu.VMEM((SUB - 1, tm + CHALO, D), F32)],
        name=name, compiler_params=_cp("arbitrary"))(dcv, dcv, u, dw_w)


HB = 4


def _lb0(lg_ref):
    l0, l1, l2 = lg_ref[0:1, :], lg_ref[1:2, :], lg_ref[2:3, :]
    m = jnp.maximum(jnp.maximum(l0, l1), l2)
    e0 = jnp.exp(l0 - m)
    return e0 / (e0 + jnp.exp(l1 - m) + jnp.exp(l2 - m))


def _mm_exact(m01, x):
    hi = x.astype(BF16)
    r1 = x - hi.astype(F32)
    mid = r1.astype(BF16)
    lo = (r1 - mid.astype(F32)).astype(BF16)
    return _dot(m01, hi) + _dot(m01, mid) + _dot(m01, lo)


def _block_tri(tm):
    r = jnp.arange(tm)[:, None]
    c = jnp.arange(tm)[None, :]
    same = (r // BLK) == (c // BLK)
    return (same & (c <= r)).astype(BF16), (same & (c >= r)).astype(BF16)


def _halves(x):
    return [x[0:SUB, :], x[SUB:BLK, :]]


def _live_halves(s):
    return ([(0, s)] if s < SUB else []) + [(1, max(s - SUB, 0))]


def _const_spec(shape):
    return pl.BlockSpec(shape, lambda h, i: (0, 0))


def _hgrn_specs(H, hb, tm, idx):
    g = H // hb
    return [pl.BlockSpec((tm, hb * HEAD), lambda h, i: (idx(i), h)),
            pl.BlockSpec((tm, hb * HEAD), lambda h, i: (idx(i), g + h)),
            pl.BlockSpec((tm, hb * HEAD), lambda h, i: (idx(i), 2 * g + h)),
            pl.BlockSpec((3, hb * HEAD), lambda h, i: (0, h))]


def hgrn_scan(proj, lb_logits, *, name, tm=128):
    T = proj.shape[0]
    H = proj.shape[1] // (4 * HEAD)
    hb = min(HB, H)
    tm = min(tm, T)
    nt = T // tm
    nblk = tm // BLK
    tril, _ = _block_tri(tm)
    heads = [slice(hh * HEAD, (hh + 1) * HEAD) for hh in range(hb)]

    def body(qp_ref, fz_ref, v_ref, lg_ref, tril_ref, o_ref, st_ref, S_ref, q_s, k_s, b_s):
        @pl.when(pl.program_id(1) == 0)
        def _():
            S_ref[...] = jnp.zeros_like(S_ref)

        st_ref[...] = S_ref[...]
        lb = _lb0(lg_ref)
        f = lb + (1.0 - lb) * _sig(fz_ref[...])
        q_s[...] = _silu(qp_ref[...])
        k_s[...] = 1.0 - f
        b_s[...] = _mm_exact(tril_ref[...], jnp.log(f))
        rows = lax.broadcasted_iota(jnp.int32, (BLK, HEAD), 0)
        S = [S_ref[hh] for hh in range(hb)]
        for nb in range(nblk):
            blk = slice(nb * BLK, (nb + 1) * BLK)
            last = slice(nb * BLK + BLK - 1, nb * BLK + BLK)
            qb = [q_s[blk, c] for c in heads]
            bb = [b_s[blk, c] for c in heads]
            o = [_dot_nt((qb[hh] * jnp.exp(bb[hh])).astype(BF16), S[hh].astype(BF16)) for hh in range(hb)]
            for hh, c in enumerate(heads):
                bc = b_s[last, c]
                kd = k_s[blk, c] * jnp.exp(bc - bb[hh])
                S[hh] = S[hh] * jnp.exp(bc) + _dot_tn(v_ref[blk, c].astype(BF16), kd.astype(BF16))
            for s in range(BLK):
                r = slice(nb * BLK + s, nb * BLK + s + 1)
                for hh, c in enumerate(heads):
                    dec = jnp.exp(jnp.where(rows >= s, bb[hh] - b_s[r, c], NEG))
                    a = jnp.sum(qb[hh] * k_s[r, c] * dec, axis=-1, keepdims=True)
                    o[hh] = o[hh] + a * v_ref[r, c]
            for hh, c in enumerate(heads):
                o_ref[blk, c] = o[hh]
        for hh in range(hb):
            S_ref[hh] = S[hh]

    return pl.pallas_call(
        body, grid=(H // hb, nt),
        in_specs=_hgrn_specs(H, hb, tm, lambda i: i) + [_const_spec((tm, tm))],
        out_specs=[pl.BlockSpec((tm, hb * HEAD), lambda h, i: (i, h)),
                   pl.BlockSpec((None, hb, HEAD, HEAD), lambda h, i: (i, h, 0, 0))],
        out_shape=[jax.ShapeDtypeStruct((T, H * HEAD), F32), jax.ShapeDtypeStruct((nt, H, HEAD, HEAD), F32)],
        scratch_shapes=[pltpu.VMEM((hb, HEAD, HEAD), F32)] + [pltpu.VMEM((tm, hb * HEAD), F32)] * 3, name=name,
        compiler_params=_cp("parallel", "arbitrary"))(proj, proj, proj, lb_logits, tril)


def hgrn_scan_bwd(proj, lb_logits, states, do, *, name, tm=128):
    T = proj.shape[0]
    H = proj.shape[1] // (4 * HEAD)
    hb = min(HB, H)
    tm = min(tm, T)
    nt = T // tm
    nblk = tm // BLK
    tril, triu = _block_tri(tm)
    sel = (jnp.arange(BLK * SUB)[None, :] // SUB == jnp.arange(BLK)[:, None]).astype(BF16)
    heads = [slice(hh * HEAD, (hh + 1) * HEAD) for hh in range(hb)]

    def body(qp_ref, fz_ref, v_ref, lg_ref, st_ref, do_ref, tril_ref, triu_ref, sel_ref, dqp_ref, dfz_ref, dv_ref, dlb_ref,
             dS_ref, Sb_ref, q_s, k_s, b_s, dq_s, dk_s, dv_s, db_s, pk_s, pv_s):
        i = pl.program_id(1)

        @pl.when(i == 0)
        def _():
            dS_ref[...] = jnp.zeros_like(dS_ref)
            dlb_ref[...] = jnp.zeros_like(dlb_ref)

        lb = _lb0(lg_ref)
        qp = qp_ref[...]
        sg = _sig(fz_ref[...])
        f = lb + (1.0 - lb) * sg
        q_s[...] = _silu(qp)
        k_s[...] = 1.0 - f
        b_s[...] = _mm_exact(tril_ref[...], jnp.log(f))
        rows = lax.broadcasted_iota(jnp.int32, (SUB, HEAD), 0)
        rows1 = lax.broadcasted_iota(jnp.int32, (SUB, 1), 0)

        S = [st_ref[hh] for hh in range(hb)]
        for nb in range(nblk):
            blk = slice(nb * BLK, (nb + 1) * BLK)
            last = slice(nb * BLK + BLK - 1, nb * BLK + BLK)
            for hh, c in enumerate(heads):
                Sb_ref[nb * hb + hh] = S[hh]
                if nb < nblk - 1:
                    bc = b_s[last, c]
                    kd = k_s[blk, c] * jnp.exp(bc - b_s[blk, c])
                    S[hh] = S[hh] * jnp.exp(bc) + _dot_tn(v_ref[blk, c].astype(BF16), kd.astype(BF16))

        dS = [dS_ref[hh] for hh in range(hb)]
        for nb in reversed(range(nblk)):
            blk = slice(nb * BLK, (nb + 1) * BLK)
            last = slice(nb * BLK + BLK - 1, nb * BLK + BLK)
            qb, kb, bb, dob, dq, dbc, ebc = [], [], [], [], [], [], []
            for hh, c in enumerate(heads):
                S0 = Sb_ref[nb * hb + hh]
                qb.append(q_s[blk, c])
                kb.append(k_s[blk, c])
                bb.append(b_s[blk, c])
                dob.append(do_ref[blk, c])
                bc = b_s[last, c]
                eb = jnp.exp(bb[hh])
                ekd = jnp.exp(bc - bb[hh])
                ebc.append(jnp.exp(bc))
                dS16 = dS[hh].astype(BF16)
                dob16 = dob[hh].astype(BF16)
                dq.append(_dot(dob16, S0.astype(BF16)) * eb)
                dki = _dot(v_ref[blk, c].astype(BF16), dS16) * ekd
                dk_s[blk, c] = dki
                dv_s[blk, c] = _dot_nt((kb[hh] * ekd).astype(BF16), dS16)
                dbc.append(_colsum(dS[hh] * S0) * ebc[hh] + _colsum(kb[hh] * dki))
                dS[hh] = dS[hh] * ebc[hh] + _dot_tn(dob16, (qb[hh] * eb).astype(BF16))
            qh, bh, doh, dqh = [[_halves(t[hh]) for hh in range(hb)] for t in (qb, bb, dob, dq)]
            for s in range(BLK):
                r = slice(nb * BLK + s, nb * BLK + s + 1)
                for hh, c in enumerate(heads):
                    ks = k_s[r, c]
                    pk, pv = None, None
                    for hf, lo in _live_halves(s):
                        diff = bh[hh][hf] - b_s[r, c]
                        dec = jnp.exp(diff if lo == 0 else jnp.where(rows >= lo, diff, NEG))
                        w = qh[hh][hf] * dec
                        a = jnp.sum(w * ks, axis=-1, keepdims=True)
                        da = jnp.sum(doh[hh][hf] * v_ref[r, c], axis=-1, keepdims=True)
                        if lo:
                            da = jnp.where(rows1 >= lo, da, 0.0)
                        dqh[hh][hf] = dqh[hh][hf] + (da * ks) * dec
                        pk = da * w if pk is None else pk + da * w
                        pv = a * doh[hh][hf] if pv is None else pv + a * doh[hh][hf]
                    pk_s[hh, s * SUB:(s + 1) * SUB, :] = pk
                    pv_s[hh, s * SUB:(s + 1) * SUB, :] = pv
            for hh, c in enumerate(heads):
                khi, klo = _split2(pk_s[hh])
                dk_s[blk, c] += _dot(sel_ref[...], khi) + _dot(sel_ref[...], klo)
                dv_s[blk, c] += _dot(sel_ref[...], pv_s[hh].astype(BF16))
                dq[hh] = jnp.concatenate(dqh[hh], axis=0)
                dq_s[blk, c] = dq[hh]
                db_s[blk, c] = qb[hh] * dq[hh] - kb[hh] * dk_s[blk, c]
                db_s[last, c] += dbc[hh]
        for hh in range(hb):
            dS_ref[hh] = dS[hh]

        dlf = _mm_exact(triu_ref[...], db_s[...])
        df = dlf / f - dk_s[...]
        dfz_ref[...] = (df * (1.0 - lb) * sg * (1.0 - sg)).astype(BF16)
        dlb_ref[...] += _colsum(df * (1.0 - sg))
        dqp_ref[...] = (dq_s[...] * _dsilu(qp)).astype(BF16)
        dv_ref[...] = dv_s[...].astype(BF16)

    rev = lambda i: nt - 1 - i
    out_blk = pl.BlockSpec((tm, hb * HEAD), lambda h, i: (rev(i), h))
    return pl.pallas_call(
        body, grid=(H // hb, nt),
        in_specs=_hgrn_specs(H, hb, tm, rev) + [pl.BlockSpec((None, hb, HEAD, HEAD), lambda h, i: (rev(i), h, 0, 0)),
                                                out_blk, _const_spec((tm, tm)), _const_spec((tm, tm)),
                                                _const_spec((BLK, BLK * SUB))],
        out_specs=[out_blk, out_blk, out_blk, pl.BlockSpec((1, hb * HEAD), lambda h, i: (0, h))],
        out_shape=[jax.ShapeDtypeStruct((T, H * HEAD), BF16)] * 3 + [jax.ShapeDtypeStruct((1, H * HEAD), F32)],
        scratch_shapes=[pltpu.VMEM((hb, HEAD, HEAD), F32), pltpu.VMEM((nblk * hb, HEAD, HEAD), F32)]
        + [pltpu.VMEM((tm, hb * HEAD), F32)] * 7 + [pltpu.VMEM((hb, BLK * SUB, HEAD), F32)] * 2, name=name,
        compiler_params=_cp("parallel", "arbitrary"))(proj, proj, proj, lb_logits, states, do, tril, triu, sel)


def hgrn_gate(o, proj, gn, *, name, tm=512):
    T, D = o.shape
    H = D // HEAD
    tm = min(tm, T)

    def body(o_ref, gp_ref, gn_ref, og_ref):
        gn_ = gn_ref[...]
        for h in range(H):
            c = slice(h * HEAD, (h + 1) * HEAD)
            oh = o_ref[:, c]
            r = lax.rsqrt(_rowmean(oh * oh) + EPS)
            og_ref[:, c] = ((oh * r) * gn_ * _silu(gp_ref[:, c])).astype(BF16)

    return pl.pallas_call(
        body, grid=(T // tm,),
        in_specs=[_row(tm, D), pl.BlockSpec((tm, D), lambda i: (i, 3)), _full(1, HEAD)],
        out_specs=_row(tm, D), out_shape=jax.ShapeDtypeStruct((T, D), BF16), name=name,
        compiler_params=_cp("parallel"))(o, proj, gn)


def hgrn_gate_bwd(dog, o, proj, gn, *, name, tm=512):
    T, D = o.shape
    H = D // HEAD
    tm = min(tm, T)

    def body(dog_ref, o_ref, gp_ref, gn_ref, do_ref, dgp_ref, dgn_ref):
        _acc_init(pl.program_id(0), dgn_ref)
        gn_ = gn_ref[...]
        for h in range(H):
            c = slice(h * HEAD, (h + 1) * HEAD)
            oh = o_ref[:, c]
            gp = gp_ref[:, c]
            dg = dog_ref[:, c]
            r = lax.rsqrt(_rowmean(oh * oh) + EPS)
            on = oh * r
            dgp_ref[:, c] = (dg * (on * gn_) * _dsilu(gp)).astype(BF16)
            don = dg * _silu(gp)
            dgn_ref[...] += _colsum(don * on)
            dn = don * gn_
            do_ref[:, c] = r * (dn - on * _rowmean(dn * on))

    return pl.pallas_call(
        body, grid=(T // tm,),
        in_specs=[_row(tm, D), _row(tm, D), pl.BlockSpec((tm, D), lambda i: (i, 3)), _full(1, HEAD)],
        out_specs=[_row(tm, D), _row(tm, D), _full(1, HEAD)],
        out_shape=[jax.ShapeDtypeStruct((T, D), F32), jax.ShapeDtypeStruct((T, D), BF16),
                   jax.ShapeDtypeStruct((1, HEAD), F32)], name=name,
        compiler_params=_cp("arbitrary"))(dog, o, proj, gn)


def _split2(x):
    hi = x.astype(BF16)
    return hi, (x - hi.astype(F32)).astype(BF16)


def ada_mod(c_all, ada_w, *, name):
    L, D, N = ada_w.shape
    B = c_all.shape[0]

    def body(c_ref, w_ref, o_ref):
        chi, clo = _split2(_silu(c_ref[...]))
        whi, wlo = _split2(w_ref[...])
        o_ref[...] = _dot(chi, whi) + _dot(chi, wlo) + _dot(clo, whi)

    return pl.pallas_call(
        body, grid=(L,), in_specs=[_full(B, D), pl.BlockSpec((None, D, N), lambda l: (l, 0, 0))],
        out_specs=pl.BlockSpec((None, B, N), lambda l: (l, 0, 0)),
        out_shape=jax.ShapeDtypeStruct((L, B, N), F32), name=name, compiler_params=_cp("parallel"))(c_all, ada_w)


def ada_wgrad(c_all_t, dmod, *, name, tr=256):
    D, B = c_all_t.shape
    L, _, N = dmod.shape
    tr = min(tr, D)

    def body(c_ref, d_ref, o_ref):
        cond = _silu(c_ref[...])
        acc = cond[:, 0:1] * d_ref[0:1, :]
        for b in range(1, B):
            acc = acc + cond[:, b:b + 1] * d_ref[b:b + 1, :]
        o_ref[...] = acc

    return pl.pallas_call(
        body, grid=(L, D // tr),
        in_specs=[pl.BlockSpec((tr, B), lambda l, r: (r, 0)), pl.BlockSpec((None, B, N), lambda l, r: (l, 0, 0))],
        out_specs=pl.BlockSpec((None, tr, N), lambda l, r: (l, r, 0)),
        out_shape=jax.ShapeDtypeStruct((L, D, N), F32), name=name,
        compiler_params=_cp("parallel", "parallel"))(c_all_t, dmod)


def sum_devices(parts, *, name):
    n, R, C = parts.shape

    def body(p_ref, o_ref):
        acc = p_ref[0]
        for d in range(1, n):
            acc = acc + p_ref[d]
        o_ref[...] = acc

    return pl.pallas_call(body, in_specs=[VMEM_SPEC], out_specs=VMEM_SPEC,
                          out_shape=jax.ShapeDtypeStruct((R, C), F32), name=name)(parts)


def lb_logits_grad(lb_logits, dlb, *, name):
    def body(lg_ref, d_ref, o_ref):
        l0, l1, l2 = lg_ref[0:1, :], lg_ref[1:2, :], lg_ref[2:3, :]
        m = jnp.maximum(jnp.maximum(l0, l1), l2)
        e0, e1, e2 = jnp.exp(l0 - m), jnp.exp(l1 - m), jnp.exp(l2 - m)
        z = e0 + e1 + e2
        p0, p1, p2 = e0 / z, e1 / z, e2 / z
        g = d_ref[...] * p0
        o_ref[0:1, :] = g * (1.0 - p0)
        o_ref[1:2, :] = -g * p1
        o_ref[2:3, :] = -g * p2

    return pl.pallas_call(body, in_specs=[VMEM_SPEC, VMEM_SPEC], out_specs=VMEM_SPEC,
                          out_shape=jax.ShapeDtypeStruct(lb_logits.shape, F32), name=name)(lb_logits, dlb)


def adamw(w, g, m, v, *, name, tr=256):
    R, C = w.shape
    tr = _tile(R, tr)

    def body(w_ref, g_ref, m_ref, v_ref, d_ref, nm_ref, nv_ref):
        gv = g_ref[...]
        nm = ADAM_B1 * m_ref[...] + (1.0 - ADAM_B1) * gv
        nv = ADAM_B2 * v_ref[...] + (1.0 - ADAM_B2) * (gv * gv)
        m_hat = nm / (1.0 - ADAM_B1 ** ADAM_STEP)
        v_hat = nv / (1.0 - ADAM_B2 ** ADAM_STEP)
        d_ref[...] = -ADAM_LR * (m_hat / (jnp.sqrt(v_hat) + ADAM_EPS) + ADAM_WD * w_ref[...])
        nm_ref[...] = nm
        nv_ref[...] = nv

    spec = pl.BlockSpec((tr, C), lambda i: (i, 0))
    return pl.pallas_call(
        body, grid=(R // tr,), in_specs=[spec] * 4, out_specs=[spec] * 3,
        out_shape=[jax.ShapeDtypeStruct((R, C), F32)] * 3, name=name, compiler_params=_cp("parallel"))(w, g, m, v)


def _place():
    return lax.axis_index("x"), lax.axis_index("y"), lax.axis_index("c")


def _flip(v, bit):
    return 1 - v if bit else v


def allgather_devices(v, *, name):
    R, C = v.shape

    def body(v_ref, out_ref, send_sems, recv_sems, local_sem):
        x, y, c = _place()
        me = 4 * x + 2 * y + c
        mine = pltpu.make_async_copy(v_ref, out_ref.at[me], local_sem)
        mine.start()
        sends = []
        for k in range(1, N_DEV):
            peer = (_flip(x, k & 4), _flip(y, k & 2), _flip(c, k & 1))
            cp = pltpu.make_async_remote_copy(src_ref=v_ref, dst_ref=out_ref.at[me], send_sem=send_sems.at[k - 1],
                                              recv_sem=recv_sems.at[k - 1], device_id=peer, device_id_type=MESH)
            cp.start()
            sends.append(cp)
        for k in range(1, N_DEV):
            px, py, pc = _flip(x, k & 4), _flip(y, k & 2), _flip(c, k & 1)
            pltpu.make_async_remote_copy(src_ref=v_ref, dst_ref=out_ref.at[4 * px + 2 * py + pc],
                                         send_sem=send_sems.at[k - 1], recv_sem=recv_sems.at[k - 1],
                                         device_id=(px, py, pc), device_id_type=MESH).wait_recv()
        for cp in sends:
            cp.wait_send()
        mine.wait()

    return pl.pallas_call(
        body, in_specs=[VMEM_SPEC], out_specs=VMEM_SPEC, out_shape=jax.ShapeDtypeStruct((N_DEV, R, C), v.dtype),
        scratch_shapes=[pltpu.SemaphoreType.DMA((N_DEV - 1,)), pltpu.SemaphoreType.DMA((N_DEV - 1,)),
                        pltpu.SemaphoreType.DMA], name=name)(v)


def _other_chips(x, y):
    return [(1 - x, y), (x, 1 - y), (1 - x, 1 - y)]


SEM = pl.BlockSpec(memory_space=pltpu.SEMAPHORE)
DATAFLOW = pltpu.SideEffectType.DATAFLOW_SIDE_EFFECTING


def _chip_copy(buf, a, j, q, c, chips, send_sems, recv_sems):
    px, py = chips[j]
    return pltpu.make_async_remote_copy(src_ref=buf.at[q, c], dst_ref=buf.at[q, c], send_sem=send_sems.at[3 * a + j],
                                        recv_sem=recv_sems.at[3 * a + j], device_id=(px, py, c), device_id_type=MESH)


def allgather_chips_start(bufs, *, name):
    n = len(bufs)

    def body(*refs):
        send_sems, recv_sems = refs[n], refs[n + 1]
        outs = refs[n + 2:2 * n + 2]
        token = refs[2 * n + 2]
        x, y, c = _place()
        chips = _other_chips(x, y)
        for a in range(n):
            for j in range(3):
                _chip_copy(outs[a], a, j, 2 * x + y, c, chips, send_sems, recv_sems).start()
        token[...] = jnp.zeros_like(token)

    res = pl.pallas_call(
        body, name=name, in_specs=[HBM] * n,
        out_specs=(SEM, SEM, *([HBM] * n), VMEM_SPEC),
        out_shape=(pltpu.SemaphoreType.DMA((3 * n,)), pltpu.SemaphoreType.DMA((3 * n,)),
                   *[pltpu.HBM(b.shape, b.dtype) for b in bufs], jax.ShapeDtypeStruct((SUB, LANE), F32)),
        input_output_aliases={a: a + 2 for a in range(n)},
        compiler_params=pltpu.CompilerParams(has_side_effects=DATAFLOW),
    )(*[pltpu.with_memory_space_constraint(b, pltpu.HBM) for b in bufs])
    return res[0], res[1], list(res[2:2 + n]), res[2 + n]


def allgather_chips_wait(send_sems, recv_sems, bufs, after, *, name):
    n = len(bufs)

    def body(*refs):
        ins = refs[:n]
        send_sems, recv_sems = refs[n], refs[n + 1]
        x, y, c = _place()
        chips = _other_chips(x, y)
        for a in range(n):
            for j, (px, py) in enumerate(chips):
                _chip_copy(ins[a], a, j, 2 * x + y, c, chips, send_sems, recv_sems).wait_send()
                _chip_copy(ins[a], a, j, 2 * px + py, c, chips, send_sems, recv_sems).wait_recv()

    return list(pl.pallas_call(
        body, name=name, in_specs=[HBM] * n + [SEM, SEM, pl.BlockSpec(memory_space=pl.ANY)],
        out_specs=[HBM] * n, out_shape=[pltpu.HBM(b.shape, b.dtype) for b in bufs],
        input_output_aliases={a: a for a in range(n)},
        compiler_params=pltpu.CompilerParams(has_side_effects=DATAFLOW),
    )(*bufs, send_sems, recv_sems, after))


def forward_to_sibling(bufs, *, name):
    n = len(bufs)

    def body(*refs):
        outs = refs[n:2 * n]
        send_sems, recv_sems = refs[2 * n:]
        x, y, c = _place()
        chips = _other_chips(x, y)

        def copy(a, j, half, to):
            px, py = chips[j]
            slab = outs[a].at[2 * px + py, half]
            return pltpu.make_async_remote_copy(src_ref=slab, dst_ref=slab, send_sem=send_sems.at[a, j],
                                                recv_sem=recv_sems.at[a, j], device_id=to, device_id_type=MESH)

        sends = [copy(a, j, c, (x, y, 1 - c)) for a in range(n) for j in range(3)]
        for cp in sends:
            cp.start()
        for a in range(n):
            for j in range(3):
                copy(a, j, 1 - c, (x, y, c)).wait_recv()
        for cp in sends:
            cp.wait_send()

    return pl.pallas_call(
        body, in_specs=[HBM] * n, out_specs=[HBM] * n,
        out_shape=[jax.ShapeDtypeStruct(b.shape, b.dtype) for b in bufs],
        input_output_aliases={a: a for a in range(n)},
        scratch_shapes=[pltpu.SemaphoreType.DMA((n, 3)), pltpu.SemaphoreType.DMA((n, 3))], name=name)(*bufs)


def pair_exchange(grads, *, name):
    n = len(grads)

    def body(*refs):
        ins, outs = refs[:n], refs[n:2 * n]
        send_sems, recv_sems = refs[2 * n:]
        x, y, c = _place()
        cps = [pltpu.make_async_remote_copy(src_ref=ins[a].at[1 - c], dst_ref=outs[a], send_sem=send_sems.at[a],
                                            recv_sem=recv_sems.at[a], device_id=(x, y, 1 - c), device_id_type=MESH)
               for a in range(n)]
        for cp in cps:
            cp.start()
        for cp in cps:
            cp.wait_recv()
        for cp in cps:
            cp.wait_send()

    return pl.pallas_call(
        body, in_specs=[HBM] * n, out_specs=[HBM] * n,
        out_shape=[jax.ShapeDtypeStruct(g.shape[1:], g.dtype) for g in grads],
        scratch_shapes=[pltpu.SemaphoreType.DMA((n,)), pltpu.SemaphoreType.DMA((n,))], name=name)(*grads)


def pair_add(g, other, c_idx, *, name, tr=256):
    _, Q, R, C = g.shape
    tr = _tile(R, tr)

    def body(c_ref, g_ref, o_ref, out_ref):
        out_ref[...] = (g_ref[...] + o_ref[...]).astype(BF16)

    return pl.pallas_call(
        body,
        grid_spec=pltpu.PrefetchScalarGridSpec(
            num_scalar_prefetch=1, grid=(Q, R // tr),
            in_specs=[pl.BlockSpec((None, None, tr, C), lambda q, r, c_ref: (c_ref[0], q, r, 0)),
                      pl.BlockSpec((None, tr, C), lambda q, r, c_ref: (q, r, 0))],
            out_specs=pl.BlockSpec((None, tr, C), lambda q, r, c_ref: (q, r, 0))),
        out_shape=jax.ShapeDtypeStruct((Q, R, C), BF16), name=name,
        compiler_params=_cp("parallel", "parallel"))(c_idx, g, other)


def chip_sum(sums, landed, qc_idx, *, name, tr=256):
    _, R, C = sums.shape
    tr = _tile(R, tr)

    def body(qc_ref, own_ref, l_ref, o_ref):
        acc = own_ref[...].astype(F32)
        for k in range(3):
            acc = acc + l_ref[k].astype(F32)
        o_ref[...] = acc

    return pl.pallas_call(
        body,
        grid_spec=pltpu.PrefetchScalarGridSpec(
            num_scalar_prefetch=1, grid=(R // tr,),
            in_specs=[pl.BlockSpec((None, tr, C), lambda r, qc: (qc[0], r, 0)),
                      pl.BlockSpec((3, tr, C), lambda r, qc: (0, r, 0))],
            out_specs=pl.BlockSpec((None, tr, C), lambda r, qc: (qc[1], r, 0))),
        out_shape=jax.ShapeDtypeStruct((2, R, C), F32), name=name,
        compiler_params=_cp("parallel"))(qc_idx, sums, landed)


def half_swap(bufs, *, name):
    n = len(bufs)

    def body(*refs):
        outs = refs[n:2 * n]
        send_sems, recv_sems = refs[2 * n:]
        x, y, c = _place()
        cps = [pltpu.make_async_remote_copy(src_ref=outs[a].at[c], dst_ref=outs[a].at[c], send_sem=send_sems.at[a],
                                            recv_sem=recv_sems.at[a], device_id=(x, y, 1 - c), device_id_type=MESH)
               for a in range(n)]
        for cp in cps:
            cp.start()
        for a in range(n):
            pltpu.make_async_remote_copy(src_ref=outs[a].at[c], dst_ref=outs[a].at[1 - c], send_sem=send_sems.at[a],
                                         recv_sem=recv_sems.at[a], device_id=(x, y, 1 - c),
                                         device_id_type=MESH).wait_recv()
        for cp in cps:
            cp.wait_send()

    return pl.pallas_call(
        body, in_specs=[HBM] * n, out_specs=[HBM] * n,
        out_shape=[jax.ShapeDtypeStruct(b.shape, b.dtype) for b in bufs],
        input_output_aliases={a: a for a in range(n)},
        scratch_shapes=[pltpu.SemaphoreType.DMA((n,)), pltpu.SemaphoreType.DMA((n,))], name=name)(*bufs)


def _exchange_copy(sums, landed, a, j, c, chips, send_sems, recv_sems):
    px, py = chips[j]
    return pltpu.make_async_remote_copy(src_ref=sums.at[2 * px + py], dst_ref=landed.at[j],
                                        send_sem=send_sems.at[3 * a + j], recv_sem=recv_sems.at[3 * a + j],
                                        device_id=(px, py, c), device_id_type=MESH)


def chip_exchange_start(sums, *, name):
    n = len(sums)
    landing = [lax.empty((3,) + s.shape[1:], s.dtype) for s in sums]

    def body(*refs):
        send_sems, recv_sems = refs[2 * n], refs[2 * n + 1]
        src, dst = refs[2 * n + 2:3 * n + 2], refs[3 * n + 2:4 * n + 2]
        token = refs[4 * n + 2]
        x, y, c = _place()
        chips = _other_chips(x, y)
        for a in range(n):
            for j in range(3):
                _exchange_copy(src[a], dst[a], a, j, c, chips, send_sems, recv_sems).start()
        token[...] = jnp.zeros_like(token)

    res = pl.pallas_call(
        body, name=name, in_specs=[HBM] * (2 * n),
        out_specs=(SEM, SEM, *([HBM] * (2 * n)), VMEM_SPEC),
        out_shape=(pltpu.SemaphoreType.DMA((3 * n,)), pltpu.SemaphoreType.DMA((3 * n,)),
                   *[pltpu.HBM(b.shape, b.dtype) for b in sums + landing], jax.ShapeDtypeStruct((SUB, LANE), F32)),
        input_output_aliases={a: a + 2 for a in range(2 * n)},
        compiler_params=pltpu.CompilerParams(has_side_effects=DATAFLOW),
    )(*[pltpu.with_memory_space_constraint(b, pltpu.HBM) for b in sums + landing])
    return res[0], res[1], list(res[2:2 + n]), list(res[2 + n:2 + 2 * n]), res[2 + 2 * n]


def chip_exchange_wait(send_sems, recv_sems, sums, landed, after, *, name):
    n = len(sums)

    def body(*refs):
        src, dst = refs[:n], refs[n:2 * n]
        send_sems, recv_sems = refs[2 * n], refs[2 * n + 1]
        x, y, c = _place()
        chips = _other_chips(x, y)
        for a in range(n):
            for j in range(3):
                cp = _exchange_copy(src[a], dst[a], a, j, c, chips, send_sems, recv_sems)
                cp.wait_send()
                cp.wait_recv()

    res = pl.pallas_call(
        body, name=name, in_specs=[HBM] * (2 * n) + [SEM, SEM, pl.BlockSpec(memory_space=pl.ANY)],
        out_specs=[HBM] * (2 * n), out_shape=[pltpu.HBM(b.shape, b.dtype) for b in sums + landed],
        input_output_aliases={a: a for a in range(2 * n)},
        compiler_params=pltpu.CompilerParams(has_side_effects=DATAFLOW),
    )(*sums, *landed, send_sems, recv_sems, after)
    return list(res[:n]), list(res[n:])


def pair_reduce(grads, c, tag):
    c_idx = c.astype(jnp.int32).reshape(1)
    others = pair_exchange(grads, name=f"grad_pair_exchange_{tag}")
    return [pair_add(g, o, c_idx, name=f"grad_pair_add_{tag}{a}") for a, (g, o) in enumerate(zip(grads, others))]


def finish_reduce(sums, landed, q, c, tag):
    qc_idx = jnp.stack([q, c]).astype(jnp.int32)
    return [chip_sum(s, l, qc_idx, name=f"grad_chip_sum_{tag}{a}") for a, (s, l) in enumerate(zip(sums, landed))]


def _ffn_forward(x, mod, pre_g, post_g, w_up, w_down, dw_w, dw_b, tag):
    sh, sc, gate = mod
    h = prenorm(x, pre_g, sc, sh, name=f"{tag}_prenorm")
    u0 = mm_nn(h, w_up, name=f"{tag}_up", out_dtype=BF16, perm=_ffn_perm)
    z, ab = ffn_act(u0, dw_w, dw_b, name=f"{tag}_act")
    y = mm_nn(z, w_down, name=f"{tag}_down")
    x_new = post_residual(x, y, post_g, gate, name=f"{tag}_post")
    return x_new, (x, h, u0, ab, z, y)


def _ffn_backward(dx, saved, mod, pre_g, post_g, w_up, w_down, dw_w, dw_b, tag):
    x, h, u0, ab, z, y = saved
    sh, sc, gate = mod
    dy, dgate, dpost, _ = post_bwd(dx, y, post_g, gate, name=f"{tag}_post_bwd")
    dz = mm_nt(dy, w_down, name=f"{tag}_down_dx", out_dtype=BF16)
    g_down = mm_tn(z, dy, name=f"{tag}_down_dw", J=2, block="a", row_chips=2)
    du0, dconv = ffn_act_bwd(dz, u0, ab, dw_w, name=f"{tag}_act_bwd")
    dh = mm_nt(du0, w_up, name=f"{tag}_up_dx", perm=_ffn_perm)
    g_up = mm_tn(h, du0, name=f"{tag}_up_dw", J=4, block="b", perm=_ffn_perm)
    dx_in, dsh, dsc, dpre = prenorm_bwd(dh, x, dx, pre_g, sc, name=f"{tag}_prenorm_bwd")
    nb = u0.shape[1] // 4
    dconv = dconv[:, 0].reshape(4, 2, 2, nb).transpose(0, 2, 1, 3).reshape(4, 4 * nb)
    return dx_in, dict(dsh=dsh, dsc=dsc, dgate=dgate, dpre=dpre, dpost=dpost, g_up=g_up, g_down=g_down,
                       d_dw_w=dconv[0:FFN_W], d_dw_b=dconv[3:4])


def _local_step(x, tgt, mods, P, late_weights=None, grads_ready=None):
    m0, m1 = mods
    h1 = prenorm(x, P["pre_mix_g"][0:1], m0[1], m0[0], name="hgrn_prenorm")
    proj = mm_nn(h1, P["hgrn_w_in"], name="hgrn_in")
    o, states = hgrn_scan(proj, P["hgrn_lb_logits"], name="hgrn_scan")
    og = hgrn_gate(o, proj, P["hgrn_gnorm_g"], name="hgrn_gate")
    y1 = mm_nn(og, P["hgrn_w_out"], name="hgrn_out")
    x1 = post_residual(x, y1, P["post_mix_g"][0:1], m0[2], name="hgrn_post")
    if late_weights is not None:
        P = {**P, **late_weights(x1)}
    x2, ffn0 = _ffn_forward(x1, m0[3:6], P["pre_ffn_g"][0:1], P["post_ffn_g"][0:1], P["ffn_w_up"][0],
                            P["ffn_w_down"][0], P["ffn_dw_w"][0], P["ffn_dw_b"][0:1], "ffn0")
    h3 = prenorm(x2, P["pre_mix_g"][1:2], m1[1], m1[0], name="conv_prenorm")
    u = mm_nn(h3, P["conv_w_in"], name="conv_in", bias=P["conv_b_in"])
    s, cv = conv_act(u, P["conv_dw_w"], P["conv_dw_b"], P["conv_ln_g"], P["conv_ln_b"], name="conv_act")
    y3 = mm_nn(s, P["conv_w_out"], name="conv_out", bias=P["conv_b_out"])
    x3 = post_residual(x2, y3, P["post_mix_g"][1:2], m1[2], name="conv_post")
    x4, ffn1 = _ffn_forward(x3, m1[3:6], P["pre_ffn_g"][1:2], P["post_ffn_g"][1:2], P["ffn_w_up"][1],
                            P["ffn_w_down"][1], P["ffn_dw_w"][1], P["ffn_dw_b"][1:2], "ffn1")
    dx4, lcols = loss_grad(x4, tgt, name="loss")
    dx3, f1 = _ffn_backward(dx4, ffn1, m1[3:6], P["pre_ffn_g"][1:2], P["post_ffn_g"][1:2], P["ffn_w_up"][1],
                            P["ffn_w_down"][1], P["ffn_dw_w"][1], P["ffn_dw_b"][1:2], "ffn1")
    dy3, dg1_1, dpostmix1, d_b_out = post_bwd(dx3, y3, P["post_mix_g"][1:2], m1[2], name="conv_post_bwd")
    ds = mm_nt(dy3, P["conv_w_out"], name="conv_out_dx")
    g_conv_out = mm_tn(s, dy3, name="conv_out_dw", J=1, block="a", row_chips=4)
    dcv, d_ln_g, d_ln_b, d_dw_b = conv_norm_bwd(ds, cv, P["conv_ln_g"], P["conv_ln_b"], name="conv_norm_bwd")
    du, d_dw_w, d_b_in = conv_glu_bwd(dcv, u, P["conv_dw_w"], name="conv_glu_bwd")
    dh3 = mm_nt(du, P["conv_w_in"], name="conv_in_dx")
    g_conv_in = mm_tn(h3, du, name="conv_in_dw", J=2, block="b", col_chips=2)
    dx2, dsh1_1, dsc1_1, dpremix1 = prenorm_bwd(dh3, x2, dx3, P["pre_mix_g"][1:2], m1[1], name="conv_prenorm_bwd")
    if grads_ready is not None:
        token = grads_ready("l1", [g_conv_in, g_conv_out, f1["g_up"], f1["g_down"]])
        m0 = tuple(m + token[0:1, 0:1] for m in m0)
    dx1, f0 = _ffn_backward(dx2, ffn0, m0[3:6], P["pre_ffn_g"][0:1], P["post_ffn_g"][0:1], P["ffn_w_up"][0],
                            P["ffn_w_down"][0], P["ffn_dw_w"][0], P["ffn_dw_b"][0:1], "ffn0")
    if grads_ready is not None:
        token = grads_ready("f0", [f0["g_up"], f0["g_down"]])
        m0 = tuple(m + token[0:1, 0:1] for m in m0)
    dy1, dg1_0, dpostmix0, _ = post_bwd(dx1, y1, P["post_mix_g"][0:1], m0[2], name="hgrn_post_bwd")
    dog = mm_nt(dy1, P["hgrn_w_out"], name="hgrn_out_dx")
    g_hgrn_out = mm_tn(og, dy1, name="hgrn_out_dw", J=1, block="a", row_chips=4)
    do, dgp, d_gn = hgrn_gate_bwd(dog, o, proj, P["hgrn_gnorm_g"], name="hgrn_gate_bwd")
    dqp, dfz, dv, dlb = hgrn_scan_bwd(proj, P["hgrn_lb_logits"], states, do, name="hgrn_scan_bwd")
    dproj = jnp.concatenate([dqp, dfz, dv, dgp], axis=1)
    g_hgrn_in = mm_tn(h1, dproj, name="hgrn_in_dw", J=4, block="b")
    token = grads_ready("hg", [g_hgrn_in, g_hgrn_out]) if grads_ready is not None else None
    dh1 = mm_nt(dproj, P["hgrn_w_in"], name="hgrn_in_dx", after=token)
    dx0, dsh1_0, dsc1_0, dpremix0 = prenorm_bwd(dh1, x, dx1, P["pre_mix_g"][0:1], m0[1], name="hgrn_prenorm_bwd")

    dmod = jnp.stack([
        jnp.concatenate([dsh1_0, dsc1_0, dg1_0, f0["dsh"], f0["dsc"], f0["dgate"]], axis=1)[0],
        jnp.concatenate([dsh1_1, dsc1_1, dg1_1, f1["dsh"], f1["dsc"], f1["dgate"]], axis=1)[0]])
    small = dict(
        loss=lcols,
        pre_mix_g=jnp.concatenate([dpremix0, dpremix1]), post_mix_g=jnp.concatenate([dpostmix0, dpostmix1]),
        pre_ffn_g=jnp.concatenate([f0["dpre"], f1["dpre"]]), post_ffn_g=jnp.concatenate([f0["dpost"], f1["dpost"]]),
        lb=dlb, hgrn_gnorm_g=d_gn, ffn_dw_b=jnp.concatenate([f0["d_dw_b"], f1["d_dw_b"]]), dmod=dmod,
        conv_b_in=d_b_in, conv_dw_w=d_dw_w[0:CONV_W], conv_dw_b=d_dw_b, conv_ln_g=d_ln_g, conv_ln_b=d_ln_b,
        conv_b_out=d_b_out, ffn_dw_w=jnp.stack([f0["d_dw_w"], f1["d_dw_w"]]))
    big = [g_hgrn_in, g_hgrn_out, g_conv_in, g_conv_out, f0["g_up"], f1["g_up"], f0["g_down"], f1["g_down"]]
    return dx0, small, big


def _pack(parts, rows=8):
    flat = jnp.concatenate([p.reshape(-1).astype(F32) for p in parts])
    per = rows * 128
    pad = (-flat.shape[0]) % per
    return jnp.pad(flat, (0, pad)).reshape(rows, -1)


def _unpack(flat, shapes):
    out, off = [], 0
    for s in shapes:
        n = 1
        for d in s:
            n *= d
        out.append(flat[..., off:off + n].reshape(flat.shape[:-1] + tuple(s)))
        off += n
    return out


def _from_chips(stacked, axis):
    moved = jnp.moveaxis(stacked, 0, axis)
    shape = list(moved.shape)
    return moved.reshape(shape[:axis] + [shape[axis] * shape[axis + 1]] + shape[axis + 2:])


def _my_shard(full, axis, q):
    n = full.shape[axis] // N_CHIPS
    return lax.dynamic_slice_in_dim(full, q * n, n, axis=axis)


def kernel(x, c, ada_w, ada_b, pre_mix_g, post_mix_g, pre_ffn_g, post_ffn_g, hgrn_w_in, hgrn_lb_logits, hgrn_gnorm_g, hgrn_w_out, conv_w_in, conv_b_in, conv_dw_w, conv_dw_b, conv_ln_g, conv_ln_b, conv_w_out, conv_b_out, ffn_w_up, ffn_dw_w, ffn_dw_b, ffn_w_down, loss_target, m_ada_w, m_ada_b, m_pre_mix_g, m_post_mix_g, m_pre_ffn_g, m_post_ffn_g, m_hgrn_w_in, m_hgrn_lb_logits, m_hgrn_gnorm_g, m_hgrn_w_out, m_conv_w_in, m_conv_b_in, m_conv_dw_w, m_conv_dw_b, m_conv_ln_g, m_conv_ln_b, m_conv_w_out, m_conv_b_out, m_ffn_w_up, m_ffn_dw_w, m_ffn_dw_b, m_ffn_w_down, v_ada_w, v_ada_b, v_pre_mix_g, v_post_mix_g, v_pre_ffn_g, v_post_ffn_g, v_hgrn_w_in, v_hgrn_lb_logits, v_hgrn_gnorm_g, v_hgrn_w_out, v_conv_w_in, v_conv_b_in, v_conv_dw_w, v_conv_dw_b, v_conv_ln_g, v_conv_ln_b, v_conv_w_out, v_conv_b_out, v_ffn_w_up, v_ffn_dw_w, v_ffn_dw_b, v_ffn_w_down):
    W = dict(ada_w=ada_w, ada_b=ada_b, pre_mix_g=pre_mix_g, post_mix_g=post_mix_g, pre_ffn_g=pre_ffn_g,
             post_ffn_g=post_ffn_g, hgrn_w_in=hgrn_w_in, hgrn_lb_logits=hgrn_lb_logits, hgrn_gnorm_g=hgrn_gnorm_g,
             hgrn_w_out=hgrn_w_out, conv_w_in=conv_w_in, conv_b_in=conv_b_in, conv_dw_w=conv_dw_w,
             conv_dw_b=conv_dw_b, conv_ln_g=conv_ln_g, conv_ln_b=conv_ln_b, conv_w_out=conv_w_out,
             conv_b_out=conv_b_out, ffn_w_up=ffn_w_up, ffn_dw_w=ffn_dw_w, ffn_dw_b=ffn_dw_b, ffn_w_down=ffn_w_down)
    M = dict(ada_w=m_ada_w, ada_b=m_ada_b, pre_mix_g=m_pre_mix_g, post_mix_g=m_post_mix_g, pre_ffn_g=m_pre_ffn_g,
             post_ffn_g=m_post_ffn_g, hgrn_w_in=m_hgrn_w_in, hgrn_lb_logits=m_hgrn_lb_logits,
             hgrn_gnorm_g=m_hgrn_gnorm_g, hgrn_w_out=m_hgrn_w_out, conv_w_in=m_conv_w_in, conv_b_in=m_conv_b_in,
             conv_dw_w=m_conv_dw_w, conv_dw_b=m_conv_dw_b, conv_ln_g=m_conv_ln_g, conv_ln_b=m_conv_ln_b,
             conv_w_out=m_conv_w_out, conv_b_out=m_conv_b_out, ffn_w_up=m_ffn_w_up, ffn_dw_w=m_ffn_dw_w,
             ffn_dw_b=m_ffn_dw_b, ffn_w_down=m_ffn_w_down)
    V = dict(ada_w=v_ada_w, ada_b=v_ada_b, pre_mix_g=v_pre_mix_g, post_mix_g=v_post_mix_g, pre_ffn_g=v_pre_ffn_g,
             post_ffn_g=v_post_ffn_g, hgrn_w_in=v_hgrn_w_in, hgrn_lb_logits=v_hgrn_lb_logits,
             hgrn_gnorm_g=v_hgrn_gnorm_g, hgrn_w_out=v_hgrn_w_out, conv_w_in=v_conv_w_in, conv_b_in=v_conv_b_in,
             conv_dw_w=v_conv_dw_w, conv_dw_b=v_conv_dw_b, conv_ln_g=v_conv_ln_g, conv_ln_b=v_conv_ln_b,
             conv_w_out=v_conv_w_out, conv_b_out=v_conv_b_out, ffn_w_up=v_ffn_w_up, ffn_dw_w=v_ffn_dw_w,
             ffn_dw_b=v_ffn_dw_b, ffn_w_down=v_ffn_w_down)
    names = list(W)
    xi, yi, ci = lax.axis_index("x"), lax.axis_index("y"), lax.axis_index("c")
    q = 2 * xi + yi
    me = 2 * q + ci
    D = x.shape[-1]
    L = ada_w.shape[0]

    small_w = ["conv_b_in", "conv_dw_w", "conv_dw_b", "conv_ln_g", "conv_ln_b", "conv_b_out", "ffn_dw_w"]
    small_axis = dict(conv_b_in=1, conv_dw_w=2, conv_dw_b=1, conv_ln_g=1, conv_ln_b=1, conv_b_out=1, ffn_dw_w=2)
    packed = _pack([c] + [W[n] for n in small_w])

    def halves(w):
        shard = w.astype(BF16).reshape(1, 2, w.shape[0] // 2, w.shape[1])
        buf = lax.empty((N_CHIPS,) + shard.shape[1:], BF16)
        return lax.dynamic_update_slice_in_dim(buf, shard, q, axis=0)

    hg_send, hg_recv, hg_bufs, hg_token = allgather_chips_start([halves(hgrn_w_in[0]), halves(hgrn_w_out[0])],
                                                                name="gather_hgrn_weights_start")
    packed, _ = lax.optimization_barrier((packed, hg_token))
    gathered = allgather_devices(packed, name="gather_small_params").reshape(N_DEV, -1)
    c_all = gathered[:, 0:D]
    per_chip = gathered.reshape(N_CHIPS, 2, -1)[:, 0, D:]
    parts = _unpack(per_chip, [W[n].shape for n in small_w])
    P = {n: _from_chips(p, small_axis[n]) for n, p in zip(small_w, parts)}
    P["conv_dw_w"] = P["conv_dw_w"][0]
    for n in ("pre_mix_g", "post_mix_g", "pre_ffn_g", "post_ffn_g", "hgrn_lb_logits", "hgrn_gnorm_g", "ffn_dw_b"):
        P[n] = W[n]

    modp = ada_mod(c_all, ada_w, name="ada_mod")
    ncol = modp.shape[-1]
    mod_all = allgather_devices(modp.reshape(L * N_DEV, ncol), name="gather_mod")
    mod_all = mod_all.reshape(N_CHIPS, 2, L, N_DEV, ncol)[:, 0]
    mod_me = lax.dynamic_index_in_dim(mod_all, me, axis=2, keepdims=False)
    mod = mod_me.transpose(1, 0, 2).reshape(L, N_CHIPS * ncol) + ada_b
    mods = [tuple(mod[l:l + 1, k * D:(k + 1) * D] for k in range(6)) for l in range(L)]

    stack = lambda t: t.reshape(N_CHIPS, t.shape[1] * t.shape[2], t.shape[3])
    rowsh = lambda t: t.reshape(1, N_CHIPS * t.shape[1] * t.shape[2], t.shape[3])
    pairs = lambda t: t.reshape(2, 2, t.shape[1], t.shape[2]).transpose(0, 2, 1, 3).reshape(2, t.shape[1], 2 * t.shape[2])
    g = forward_to_sibling(allgather_chips_wait(hg_send, hg_recv, hg_bufs, mod, name="gather_hgrn_weights_wait"),
                           name="gather_hgrn_weights_forward")
    P["hgrn_w_in"], P["hgrn_w_out"] = stack(g[0]), rowsh(g[1])
    late_shards = [conv_w_in[0], conv_w_out[0], ffn_w_up[0], ffn_w_up[1], ffn_w_down[0], ffn_w_down[1]]
    late_bufs, _, _ = lax.optimization_barrier(([halves(w) for w in late_shards], g, mod))
    send_sems, recv_sems, bufs, token = allgather_chips_start(late_bufs, name="gather_weights_start")
    mods[0] = tuple(m + token[0:1, 0:1] for m in mods[0])

    def late_weights(x1):
        landed = allgather_chips_wait(send_sems, recv_sems, bufs, x1, name="gather_weights_wait")
        g = forward_to_sibling(landed, name="gather_weights_forward")
        return dict(conv_w_in=pairs(stack(g[0])), conv_w_out=rowsh(g[1]), ffn_w_up=[stack(g[2]), stack(g[3])],
                    ffn_w_down=[rowsh(g[4]), rowsh(g[5])])

    in_flight = {}

    def grads_ready(tag, grads):
        sums = pair_reduce(grads, ci, f"{tag}_")
        send, recv, sums, landing, tok = chip_exchange_start(sums, name=f"grad_chip_exchange_start_{tag}")
        in_flight[tag] = (send, recv, sums, landing)
        return tok

    grad_x, small, big = _local_step(x[0], loss_target[0], mods, P, late_weights, grads_ready)

    small_names = list(small)
    gs = allgather_devices(_pack([small[n] for n in small_names]), name="gather_small_grads")
    dmod_all = _unpack(gs.reshape(N_DEV, -1), [small[n].shape for n in small_names])[small_names.index("dmod")]
    tot = sum_devices(gs, name="sum_small_grads").reshape(1, -1)
    S = dict(zip(small_names, _unpack(tot, [small[n].shape for n in small_names])))
    S = {n: v[0] for n, v in S.items()}
    loss = 0.5 * jnp.sum(S["loss"]) / D

    G = {}
    dmod_q = lax.dynamic_slice_in_dim(dmod_all, q * ncol, ncol, axis=2)
    G["ada_w"] = ada_wgrad(c_all.T, dmod_q.transpose(1, 0, 2), name="ada_wgrad")
    G["ada_b"] = S["dmod"]
    for n in ("pre_mix_g", "post_mix_g", "pre_ffn_g", "post_ffn_g", "hgrn_gnorm_g", "ffn_dw_b"):
        G[n] = S[n]
    G["hgrn_lb_logits"] = lb_logits_grad(hgrn_lb_logits, S["lb"], name="lb_logits_grad")
    G["conv_b_in"] = _my_shard(S["conv_b_in"], 1, q)
    G["conv_dw_w"] = _my_shard(S["conv_dw_w"], 1, q)[None]
    for n in ("conv_dw_b", "conv_ln_g", "conv_ln_b", "conv_b_out"):
        G[n] = _my_shard(S[n], 1, q)
    G["ffn_dw_w"] = _my_shard(S["ffn_dw_w"], 2, q)

    halves = []
    for tag in ("f0", "l1"):
        sums_t, landed_t = chip_exchange_wait(*in_flight[tag], grad_x, name=f"grad_chip_exchange_wait_{tag}")
        halves += finish_reduce(sums_t, landed_t, q, ci, f"{tag}_")
    red = [f.reshape(2 * f.shape[1], f.shape[2]) for f in half_swap(halves, name="grad_half_swap")]
    G["conv_w_in"], G["conv_w_out"] = red[2][None], red[3][None]
    G["ffn_w_up"] = jnp.stack([red[0], red[4]])
    G["ffn_w_down"] = jnp.stack([red[1], red[5]])

    delta, new_m, new_v = {}, {}, {}

    def adamw_matrix(n):
        shp = W[n].shape
        two = lambda t: t.reshape(-1, shp[-1])
        d_, m_, v_ = adamw(two(W[n]), two(G[n]), two(M[n]), two(V[n]), name=f"adamw_{n}")
        delta[n], new_m[n], new_v[n] = d_.reshape(shp), m_.reshape(shp), v_.reshape(shp)

    big_names = ["ada_w", "hgrn_w_in", "hgrn_w_out", "conv_w_in", "conv_w_out", "ffn_w_up", "ffn_w_down"]
    for n in ("ada_w", "conv_w_in", "conv_w_out", "ffn_w_up", "ffn_w_down"):
        adamw_matrix(n)
    sums_h, landed_h = chip_exchange_wait(*in_flight["hg"], delta["ffn_w_down"], name="grad_chip_exchange_wait_hg")
    red_h = half_swap(finish_reduce(sums_h, landed_h, q, ci, "hg_"), name="grad_half_swap_hg")
    G["hgrn_w_in"], G["hgrn_w_out"] = [f.reshape(1, 2 * f.shape[1], f.shape[2]) for f in red_h]
    for n in ("hgrn_w_in", "hgrn_w_out"):
        adamw_matrix(n)
    rest = [n for n in names if n not in big_names]
    d_, m_, v_ = adamw(_pack([W[n] for n in rest]), _pack([G[n] for n in rest]), _pack([M[n] for n in rest]),
                       _pack([V[n] for n in rest]), name="adamw_small")
    shapes = [W[n].shape for n in rest]
    for n, a, b_, c_ in zip(rest, _unpack(d_.reshape(-1), shapes), _unpack(m_.reshape(-1), shapes),
                            _unpack(v_.reshape(-1), shapes)):
        delta[n], new_m[n], new_v[n] = a, b_, c_

    return (loss, grad_x[None], *[G[n].reshape(W[n].shape) for n in names], *[delta[n] for n in names],
            *[new_m[n] for n in names], *[new_v[n] for n in names])
```

```python
import jax
import jax.numpy as jnp
from jax import lax
from jax.experimental import pallas as pl
from jax.experimental.pallas import tpu as pltpu

F32 = jnp.float32
BF16 = jnp.bfloat16
EPS = 1e-6
HEAD = 128
BLK = 16
NEG = -1e30
CONV_W = 31
FFN_W = 3
N_CHIPS = 4
N_DEV = 8
SUB = 8
LANE = 128
V7X_VMEM_LIMIT = 56 * 1024 * 1024
MESH = pl.DeviceIdType.MESH
HBM = pl.BlockSpec(memory_space=pltpu.HBM)
VMEM_SPEC = pl.BlockSpec(memory_space=pltpu.VMEM)

ADAM_LR = 0.001
ADAM_B1 = 0.9
ADAM_B2 = 0.999
ADAM_EPS = 1e-08
ADAM_WD = 0.01
ADAM_STEP = 10


def _cp(*sem):
    return pltpu.CompilerParams(dimension_semantics=sem, vmem_limit_bytes=V7X_VMEM_LIMIT)


def _sig(x):
    return 0.5 * jnp.tanh(0.5 * x) + 0.5


def _silu(x):
    return x * _sig(x)


def _dsilu(x):
    s = _sig(x)
    return s * (1.0 + x * (1.0 - s))


def _dot(a, b):
    return jnp.dot(a, b, preferred_element_type=F32)


def _dot_nt(a, b):
    return lax.dot_general(a, b, (((1,), (1,)), ((), ())), preferred_element_type=F32)


def _dot_tn(a, b):
    return lax.dot_general(a, b, (((0,), (0,)), ((), ())), preferred_element_type=F32)


def _colsum(x):
    return jnp.sum(x, axis=0, keepdims=True)


def _rowmean(x):
    return jnp.mean(x, axis=-1, keepdims=True)


def _ffn_perm(j):
    return (j % 2) * 2 + j // 2


def _tile(n, pref):
    if n <= pref:
        return n
    t = pref - pref % 8
    while n % t:
        t -= 8
    return t


def mm_nn(a, w, *, name, bias=None, out_dtype=F32, perm=None, tm=1024):
    T, K = a.shape
    J, _, nb = w.shape
    tm = min(tm, T)
    col = (lambda j: j) if perm is None else perm

    def body(a_ref, w_ref, *rest):
        acc = _dot(a_ref[...], w_ref[...])
        if bias is not None:
            acc = acc + rest[0][...]
        rest[-1][...] = acc.astype(out_dtype)

    in_specs = [pl.BlockSpec((tm, K), lambda j, i: (i, 0)), pl.BlockSpec((None, K, nb), lambda j, i: (j, 0, 0))]
    args = [a, w]
    if bias is not None:
        in_specs.append(pl.BlockSpec((1, nb), lambda j, i: (0, j)))
        args.append(bias)
    return pl.pallas_call(
        body, grid=(J, T // tm), in_specs=in_specs,
        out_specs=pl.BlockSpec((tm, nb), lambda j, i: (i, col(j))),
        out_shape=jax.ShapeDtypeStruct((T, J * nb), out_dtype), name=name,
        compiler_params=_cp("parallel", "parallel"))(*args)


def mm_nt(a, w, *, name, out_dtype=F32, perm=None, tm=1024, after=None):
    T = a.shape[0]
    J, K, nb = w.shape
    tm = min(tm, T)
    col = (lambda j: j) if perm is None else perm
    deps = [] if after is None else [after]

    def body(a_ref, w_ref, *rest):
        o_ref, acc_ref = rest[len(deps):]
        j = pl.program_id(1)

        @pl.when(j == 0)
        def _():
            acc_ref[...] = jnp.zeros_like(acc_ref)

        acc_ref[...] += _dot_nt(a_ref[...], w_ref[...])

        @pl.when(j == J - 1)
        def _():
            o_ref[...] = acc_ref[...].astype(out_dtype)

    return pl.pallas_call(
        body, grid=(T // tm, J),
        in_specs=[pl.BlockSpec((tm, nb), lambda i, j: (i, col(j))), pl.BlockSpec((None, K, nb), lambda i, j: (j, 0, 0))]
        + [pl.BlockSpec(memory_space=pl.ANY)] * len(deps),
        out_specs=pl.BlockSpec((tm, K), lambda i, j: (i, 0)),
        out_shape=jax.ShapeDtypeStruct((T, K), out_dtype),
        scratch_shapes=[pltpu.VMEM((tm, K), F32)], name=name,
        compiler_params=_cp("parallel", "arbitrary"))(a, w, *deps)


def mm_tn(a, b, *, name, J, block, row_chips=1, col_chips=1, perm=None, tk=1024):
    T = a.shape[0]
    tk = min(tk, T)
    col = (lambda j: j) if perm is None else perm
    if block == "b":
        rows, nb = a.shape[1], b.shape[1] // J
        a_spec = pl.BlockSpec((tk, rows), lambda j, t: (t, 0))
        b_spec = pl.BlockSpec((tk, nb), lambda j, t: (t, col(j)))
    else:
        rows, nb = a.shape[1] // J, b.shape[1]
        a_spec = pl.BlockSpec((tk, rows), lambda j, t: (t, col(j)))
        b_spec = pl.BlockSpec((tk, nb), lambda j, t: (t, 0))
    rh = rows // (2 * row_chips)
    nc = nb // col_chips
    chips = [(rc, cc) for rc in range(row_chips) for cc in range(col_chips)]

    def body(a_ref, b_ref, o_ref):
        @pl.when(pl.program_id(1) == 0)
        def _():
            o_ref[...] = jnp.zeros_like(o_ref)

        acc = _dot_tn(a_ref[...], b_ref[...])
        for ch, (rc, cc) in enumerate(chips):
            for hf in range(2):
                r0 = (rc * 2 + hf) * rh
                o_ref[hf, ch] += acc[r0:r0 + rh, cc * nc:(cc + 1) * nc]

    return pl.pallas_call(
        body, grid=(J, T // tk), in_specs=[a_spec, b_spec],
        out_specs=pl.BlockSpec((2, len(chips), rh, nc), lambda j, t: (0, j, 0, 0)),
        out_shape=jax.ShapeDtypeStruct((2, J * len(chips), rh, nc), F32), name=name,
        compiler_params=_cp("parallel", "arbitrary"))(a, b)


def mm_nt_parts(s3, g, w, *, name, tm=1024, after=None):
    n3, T, nb = s3.shape
    J, K, _ = w.shape
    tm = min(tm, T)
    deps = [] if after is None else [after]

    def body(s_ref, g_ref, w_ref, *rest):
        o_ref, acc_ref = rest[len(deps):]
        j = pl.program_id(1)

        @pl.when(j == 0)
        def _():
            acc_ref[...] = jnp.zeros_like(acc_ref)

        @pl.when(j < n3)
        def _():
            acc_ref[...] += _dot_nt(s_ref[...], w_ref[...])

        @pl.when(j == n3)
        def _():
            acc_ref[...] += _dot_nt(g_ref[...], w_ref[...])

        @pl.when(j == J - 1)
        def _():
            o_ref[...] = acc_ref[...]

    return pl.pallas_call(
        body, grid=(T // tm, J),
        in_specs=[pl.BlockSpec((None, tm, nb), lambda i, j: (jnp.minimum(j, n3 - 1), i, 0)),
                  pl.BlockSpec((tm, nb), lambda i, j: (i, 0)), pl.BlockSpec((None, K, nb), lambda i, j: (j, 0, 0))]
        + [pl.BlockSpec(memory_space=pl.ANY)] * len(deps),
        out_specs=pl.BlockSpec((tm, K), lambda i, j: (i, 0)), out_shape=jax.ShapeDtypeStruct((T, K), F32),
        scratch_shapes=[pltpu.VMEM((tm, K), F32)], name=name,
        compiler_params=_cp("parallel", "arbitrary"))(s3, g, w, *deps)


def mm_tn_parts(a, s3, g, *, name, tk=1024):
    n3, T, nb = s3.shape
    J = n3 + 1
    tk = min(tk, T)
    rows = a.shape[1]
    rh = rows // 2

    def body(a_ref, s_ref, g_ref, o_ref):
        j = pl.program_id(0)

        @pl.when(pl.program_id(1) == 0)
        def _():
            o_ref[...] = jnp.zeros_like(o_ref)

        def add(b_ref):
            acc = _dot_tn(a_ref[...], b_ref[...])
            for hf in range(2):
                o_ref[hf, 0] += acc[hf * rh:(hf + 1) * rh, :]

        pl.when(j < n3)(lambda: add(s_ref))
        pl.when(j == n3)(lambda: add(g_ref))

    return pl.pallas_call(
        body, grid=(J, T // tk),
        in_specs=[pl.BlockSpec((tk, rows), lambda j, t: (t, 0)),
                  pl.BlockSpec((None, tk, nb), lambda j, t: (jnp.minimum(j, n3 - 1), t, 0)),
                  pl.BlockSpec((tk, nb), lambda j, t: (t, 0))],
        out_specs=pl.BlockSpec((2, 1, rh, nb), lambda j, t: (0, j, 0, 0)),
        out_shape=jax.ShapeDtypeStruct((2, J, rh, nb), F32), name=name,
        compiler_params=_cp("parallel", "arbitrary"))(a, s3, g)


def _row(tm, w):
    return pl.BlockSpec((tm, w), lambda i: (i, 0))


def _full(r, w):
    return pl.BlockSpec((r, w), lambda i: (0, 0))


def _acc_init(i, *refs):
    @pl.when(i == 0)
    def _():
        for r in refs:
            r[...] = jnp.zeros_like(r)


def prenorm(x, g, sc, sh, *, name, tm=512):
    T, D = x.shape
    tm = min(tm, T)

    def body(x_ref, g_ref, sc_ref, sh_ref, h_ref):
        xv = x_ref[...]
        r = lax.rsqrt(_rowmean(xv * xv) + EPS)
        h_ref[...] = ((xv * r) * g_ref[...] * (1.0 + sc_ref[...]) + sh_ref[...]).astype(BF16)

    return pl.pallas_call(
        body, grid=(T // tm,), in_specs=[_row(tm, D), _full(1, D), _full(1, D), _full(1, D)],
        out_specs=_row(tm, D), out_shape=jax.ShapeDtypeStruct((T, D), BF16), name=name,
        compiler_params=_cp("parallel"))(x, g, sc, sh)


def post_residual(x, y, g, gate, *, name, tm=512):
    T, D = x.shape
    tm = min(tm, T)

    def body(x_ref, y_ref, g_ref, gate_ref, o_ref):
        yv = y_ref[...]
        r = lax.rsqrt(_rowmean(yv * yv) + EPS)
        o_ref[...] = x_ref[...] + gate_ref[...] * ((yv * r) * g_ref[...])

    return pl.pallas_call(
        body, grid=(T // tm,), in_specs=[_row(tm, D), _row(tm, D), _full(1, D), _full(1, D)],
        out_specs=_row(tm, D), out_shape=jax.ShapeDtypeStruct((T, D), F32), name=name,
        compiler_params=_cp("parallel"))(x, y, g, gate)


def post_residual_loss(x, y, g, gate, tgt, *, name, tm=512):
    T, D = x.shape
    tm = min(tm, T)

    def body(x_ref, y_ref, g_ref, gate_ref, t_ref, dx_ref, l_ref):
        _acc_init(pl.program_id(0), l_ref)
        yv = y_ref[...]
        r = lax.rsqrt(_rowmean(yv * yv) + EPS)
        e = x_ref[...] + gate_ref[...] * ((yv * r) * g_ref[...]) - t_ref[...]
        dx_ref[...] = e * (1.0 / D)
        l_ref[...] += _colsum(e * e)

    return pl.pallas_call(
        body, grid=(T // tm,), in_specs=[_row(tm, D), _row(tm, D), _full(1, D), _full(1, D), _row(tm, D)],
        out_specs=[_row(tm, D), _full(1, D)],
        out_shape=[jax.ShapeDtypeStruct((T, D), F32), jax.ShapeDtypeStruct((1, D), F32)], name=name,
        compiler_params=_cp("arbitrary"))(x, y, g, gate, tgt)


def post_bwd(dx, y, g, gate, *, name, tm=512):
    T, D = dx.shape
    tm = min(tm, T)

    def body(dx_ref, y_ref, g_ref, gate_ref, dy_ref, dgate_ref, dg_ref, dbias_ref):
        _acc_init(pl.program_id(0), dgate_ref, dg_ref, dbias_ref)
        yv = y_ref[...]
        dxv = dx_ref[...]
        r = lax.rsqrt(_rowmean(yv * yv) + EPS)
        yn = yv * r
        gv = g_ref[...]
        gt = gate_ref[...]
        dgate_ref[...] += _colsum(dxv * (yn * gv))
        dg_ref[...] += _colsum(dxv * gt * yn)
        dyn = dxv * gt * gv
        dy = r * (dyn - yn * _rowmean(dyn * yn))
        dbias_ref[...] += _colsum(dy)
        dy_ref[...] = dy.astype(BF16)

    return pl.pallas_call(
        body, grid=(T // tm,), in_specs=[_row(tm, D), _row(tm, D), _full(1, D), _full(1, D)],
        out_specs=[_row(tm, D), _full(1, D), _full(1, D), _full(1, D)],
        out_shape=[jax.ShapeDtypeStruct((T, D), BF16)] + [jax.ShapeDtypeStruct((1, D), F32)] * 3, name=name,
        compiler_params=_cp("arbitrary"))(dx, y, g, gate)


def prenorm_bwd(dh, x, dres, g, sc, *, name, tm=512):
    T, D = x.shape
    tm = min(tm, T)

    def body(dh_ref, x_ref, dres_ref, g_ref, sc_ref, dx_ref, dsh_ref, dsc_ref, dg_ref):
        _acc_init(pl.program_id(0), dsh_ref, dsc_ref, dg_ref)
        xv = x_ref[...]
        dhv = dh_ref[...]
        r = lax.rsqrt(_rowmean(xv * xv) + EPS)
        xn = xv * r
        gv = g_ref[...]
        one_sc = 1.0 + sc_ref[...]
        dsh_ref[...] += _colsum(dhv)
        dsc_ref[...] += _colsum(dhv * (xn * gv))
        dg_ref[...] += _colsum(dhv * one_sc * xn)
        dxn = dhv * one_sc * gv
        dx_ref[...] = dres_ref[...] + r * (dxn - xn * _rowmean(dxn * xn))

    return pl.pallas_call(
        body, grid=(T // tm,), in_specs=[_row(tm, D), _row(tm, D), _row(tm, D), _full(1, D), _full(1, D)],
        out_specs=[_row(tm, D), _full(1, D), _full(1, D), _full(1, D)],
        out_shape=[jax.ShapeDtypeStruct((T, D), F32)] + [jax.ShapeDtypeStruct((1, D), F32)] * 3, name=name,
        compiler_params=_cp("arbitrary"))(dh, x, dres, g, sc)


HALO = 16


def _shift_helpers():
    rid = lax.broadcasted_iota(jnp.int32, (SUB, LANE), 0)

    def down(cur, prev, k):
        return pltpu.roll(jnp.where(rid >= SUB - k, prev, cur), k, 0)

    def up(cur, nxt, k):
        return pltpu.roll(jnp.where(rid < k, nxt, cur), SUB - k, 0)

    return down, up


def _ffn_sides(c, nb, wa_ref, wb_ref, ba_ref, bb_ref):
    cols = slice(c * LANE, (c + 1) * LANE)
    return [(cols, [wa_ref[k:k + 1, cols] for k in range(FFN_W)], ba_ref[:, cols]),
            (slice(nb + c * LANE, nb + (c + 1) * LANE), [wb_ref[k:k + 1, cols] for k in range(FFN_W)],
             bb_ref[:, cols])]


def _ffn_specs(tm, nb, hb, idx):
    return [pl.BlockSpec((tm, 2 * nb), lambda jc, i: (idx(i), jc)),
            pl.BlockSpec((HALO, 2 * nb), lambda jc, i: (jnp.maximum(idx(i) * hb - 1, 0), jc)),
            pl.BlockSpec((FFN_W, nb), lambda jc, i: (0, jc)),
            pl.BlockSpec((FFN_W, nb), lambda jc, i: (0, jc + 2)),
            pl.BlockSpec((1, nb), lambda jc, i: (0, jc)),
            pl.BlockSpec((1, nb), lambda jc, i: (0, jc + 2))]


def ffn_act(u0p, dw_w, dw_b, *, name, tm=256):
    T, W = u0p.shape
    nb = W // 4
    tm = min(tm, T)
    unroll = 4
    rows16 = 2 * SUB

    def body(u_ref, halo_ref, wa_ref, wb_ref, ba_ref, bb_ref, z_ref, ab_ref):
        i = pl.program_id(1)
        down, _ = _shift_helpers()
        for c in range(nb // LANE):
            cols = slice(c * LANE, (c + 1) * LANE)
            side = _ffn_sides(c, nb, wa_ref, wb_ref, ba_ref, bb_ref)

            def rows(j, prev):
                prev = list(prev)
                for m in range(unroll):
                    r0 = pl.multiple_of((j * unroll + m) * rows16, rows16)
                    x = [u_ref[pl.ds(r0, rows16), cs].astype(F32) for cs, _, _ in side]
                    conv = [[None, None], [None, None]]
                    for hf in range(2):
                        for n, (_, w, b) in enumerate(side):
                            cur = x[n][hf * SUB:(hf + 1) * SUB, :]
                            conv[n][hf] = b + w[2] * cur + w[1] * down(cur, prev[n], 1) + w[0] * down(cur, prev[n], 2)
                            prev[n] = cur
                    a, b = [jnp.concatenate(conv[n], axis=0) for n in range(2)]
                    z_ref[pl.ds(r0, rows16), cols] = (_silu(a) * b).astype(BF16)
                    ab_ref[pl.ds(r0, rows16), side[0][0]] = a.astype(BF16)
                    ab_ref[pl.ds(r0, rows16), side[1][0]] = b.astype(BF16)
                return tuple(prev)

            first = [jnp.where(i == 0, 0.0, halo_ref[:, cs].astype(F32)[SUB:2 * SUB, :]) for cs, _, _ in side]
            lax.fori_loop(0, tm // (rows16 * unroll), rows, tuple(first))

    return pl.pallas_call(
        body, grid=(2, T // tm), in_specs=_ffn_specs(tm, nb, tm // HALO, lambda i: i),
        out_specs=[pl.BlockSpec((tm, nb), lambda jc, i: (i, jc)), pl.BlockSpec((tm, 2 * nb), lambda jc, i: (i, jc))],
        out_shape=[jax.ShapeDtypeStruct((T, 2 * nb), BF16), jax.ShapeDtypeStruct((T, W), BF16)], name=name,
        compiler_params=_cp("parallel", "arbitrary"))(u0p, u0p, dw_w, dw_w, dw_b, dw_b)


def ffn_act_bwd(dz, u0p, ab, dw_w, *, name, tm=256):
    T, W = u0p.shape
    nb = W // 4
    tm = min(tm, T)
    nt = T // tm
    unroll = 4
    rows16 = 2 * SUB
    n_it = tm // (rows16 * unroll)

    def body(dz_ref, u_ref, ab_ref, wa_ref, wb_ref, du0_ref, dw_ref, carry):
        i = pl.program_id(1)
        _acc_init(i, dw_ref)
        _, up = _shift_helpers()
        for c in range(nb // LANE):
            cols = slice(c * LANE, (c + 1) * LANE)
            side = [(cols, [wa_ref[k:k + 1, cols] for k in range(FFN_W)]),
                    (slice(nb + c * LANE, nb + (c + 1) * LANE), [wb_ref[k:k + 1, cols] for k in range(FFN_W)])]

            def rows(j, st):
                nxt, acc = list(st[0:2]), list(st[2:10])
                for m in range(unroll):
                    r0 = pl.multiple_of(((n_it - 1 - j) * unroll + unroll - 1 - m) * rows16, rows16)
                    dzv = dz_ref[pl.ds(r0, rows16), cols].astype(F32)
                    a, b = [ab_ref[pl.ds(r0, rows16), cs].astype(F32) for cs, _ in side]
                    x = [u_ref[pl.ds(r0, rows16), cs].astype(F32) for cs, _ in side]
                    sa = _sig(a)
                    d16 = [dzv * b * (sa * (1.0 + a * (1.0 - sa))), dzv * (a * sa)]
                    out = [[None, None], [None, None]]
                    for hf in (1, 0):
                        half = slice(hf * SUB, (hf + 1) * SUB)
                        for n in range(2):
                            w = side[n][1]
                            d = d16[n][half, :]
                            u = x[n][half, :]
                            up1, up2 = up(d, nxt[n], 1), up(d, nxt[n], 2)
                            acc[4 * n + 0] = acc[4 * n + 0] + up2 * u
                            acc[4 * n + 1] = acc[4 * n + 1] + up1 * u
                            acc[4 * n + 2] = acc[4 * n + 2] + d * u
                            acc[4 * n + 3] = acc[4 * n + 3] + d
                            out[n][hf] = w[2] * d + w[1] * up1 + w[0] * up2
                            nxt[n] = d
                    for n in range(2):
                        du0_ref[pl.ds(r0, rows16), side[n][0]] = jnp.concatenate(out[n], axis=0).astype(BF16)
                return (*nxt, *acc)

            init = [jnp.where(i == 0, 0.0, carry[:, cs]) for cs, _ in side] + [jnp.zeros((SUB, LANE), F32)] * 8
            st = lax.fori_loop(0, n_it, rows, tuple(init))
            for n in range(2):
                carry[:, side[n][0]] = st[n]
                for k in range(4):
                    dw_ref[k, :, side[n][0]] += st[2 + 4 * n + k]

        @pl.when(i == nt - 1)
        def _():
            for k in range(4):
                dw_ref[k, 0:1, :] = _colsum(dw_ref[k])

    rev = lambda i: nt - 1 - i
    wide = pl.BlockSpec((tm, 2 * nb), lambda jc, i: (rev(i), jc))
    return pl.pallas_call(
        body, grid=(2, nt),
        in_specs=[pl.BlockSpec((tm, nb), lambda jc, i: (rev(i), jc)), wide, wide,
                  pl.BlockSpec((FFN_W, nb), lambda jc, i: (0, jc)), pl.BlockSpec((FFN_W, nb), lambda jc, i: (0, jc + 2))],
        out_specs=[wide, pl.BlockSpec((4, SUB, 2 * nb), lambda jc, i: (0, 0, jc))],
        out_shape=[jax.ShapeDtypeStruct((T, W), BF16), jax.ShapeDtypeStruct((4, SUB, W), F32)],
        scratch_shapes=[pltpu.VMEM((SUB, 2 * nb), F32)], name=name,
        compiler_params=_cp("parallel", "arbitrary"))(dz, u0p, ab, dw_w, dw_w)


CHALO = 32
CCOL = 256


def _phase_copies(buf, shifted, tm):
    n = tm + CHALO - SUB
    for p in range(1, SUB):
        shifted[p - 1, 0:n, :] = buf[p:p + n, :]


def _shifted(buf, shifted, r, tm, c0):
    m, p = divmod(r, SUB)
    src = buf if p == 0 else shifted.at[p - 1]
    return src[m * SUB:m * SUB + tm, c0:c0 + CCOL]


def conv_act(u, dw_w, dw_b, ln_g, ln_b, *, name, tm=128):
    T, D2 = u.shape
    D = D2 // 2
    tm = min(tm, T)
    hb = tm // CHALO

    def body(u_ref, halo_ref, w_ref, b_ref, g_ref, be_ref, s_ref, cv_ref, gbuf, gsh):
        i = pl.program_id(0)
        hv = halo_ref[...]
        gbuf[0:CHALO, :] = jnp.where(i == 0, 0.0, hv[:, 0:D] * _sig(hv[:, D:D2]))
        uv = u_ref[...]
        gbuf[CHALO:CHALO + tm, :] = uv[:, 0:D] * _sig(uv[:, D:D2])
        _phase_copies(gbuf, gsh, tm)
        for c0 in range(0, D, CCOL):
            acc = jnp.zeros((tm, CCOL), F32) + b_ref[:, c0:c0 + CCOL]
            for k in range(CONV_W):
                acc = acc + w_ref[k:k + 1, c0:c0 + CCOL] * _shifted(gbuf, gsh, CHALO - (CONV_W - 1) + k, tm, c0)
            cv_ref[:, c0:c0 + CCOL] = acc
        cv = cv_ref[...]
        mu = _rowmean(cv)
        xc = cv - mu
        nh = xc * lax.rsqrt(_rowmean(xc * xc) + EPS)
        s_ref[...] = _silu(nh * g_ref[...] + be_ref[...]).astype(BF16)

    return pl.pallas_call(
        body, grid=(T // tm,),
        in_specs=[_row(tm, D2), pl.BlockSpec((CHALO, D2), lambda i: (jnp.maximum(i * hb - 1, 0), 0)),
                  _full(CONV_W, D), _full(1, D), _full(1, D), _full(1, D)],
        out_specs=[_row(tm, D), _row(tm, D)],
        out_shape=[jax.ShapeDtypeStruct((T, D), BF16), jax.ShapeDtypeStruct((T, D), F32)],
        scratch_shapes=[pltpu.VMEM((tm + CHALO, D), F32), pltpu.VMEM((SUB - 1, tm + CHALO, D), F32)], name=name,
        compiler_params=_cp("arbitrary"))(u, u, dw_w, dw_b, ln_g, ln_b)


def conv_norm_bwd(ds, cv, ln_g, ln_b, *, name, tm=512):
    T, D = cv.shape
    tm = min(tm, T)

    def body(ds_ref, cv_ref, g_ref, be_ref, dcv_ref, dg_ref, dbe_ref, dcb_ref):
        _acc_init(pl.program_id(0), dg_ref, dbe_ref, dcb_ref)
        cv_ = cv_ref[...]
        mu = _rowmean(cv_)
        xc = cv_ - mu
        rstd = lax.rsqrt(_rowmean(xc * xc) + EPS)
        nh = xc * rstd
        gv = g_ref[...]
        dln = ds_ref[...] * _dsilu(nh * gv + be_ref[...])
        dg_ref[...] += _colsum(dln * nh)
        dbe_ref[...] += _colsum(dln)
        dnh = dln * gv
        dcv = rstd * (dnh - _rowmean(dnh) - nh * _rowmean(dnh * nh))
        dcb_ref[...] += _colsum(dcv)
        dcv_ref[...] = dcv

    return pl.pallas_call(
        body, grid=(T // tm,), in_specs=[_row(tm, D), _row(tm, D), _full(1, D), _full(1, D)],
        out_specs=[_row(tm, D), _full(1, D), _full(1, D), _full(1, D)],
        out_shape=[jax.ShapeDtypeStruct((T, D), F32)] + [jax.ShapeDtypeStruct((1, D), F32)] * 3, name=name,
        compiler_params=_cp("arbitrary"))(ds, cv, ln_g, ln_b)


def conv_glu_bwd(dcv, u, dw_w, *, name, tm=128):
    T, D2 = u.shape
    D = D2 // 2
    tm = min(tm, T)
    nt = T // tm
    hb = tm // CHALO

    def body(dcv_ref, dnext_ref, u_ref, w_ref, du_ref, dw_ref, dbin_ref, dbuf, dsh):
        i = pl.program_id(0)
        _acc_init(i, dw_ref, dbin_ref)
        uv = u_ref[...]
        av = uv[:, 0:D]
        sg = _sig(uv[:, D:D2])
        glu = av * sg
        dbuf[0:tm, :] = dcv_ref[...]
        dbuf[tm:tm + CHALO, :] = jnp.where(i == nt - 1, 0.0, dnext_ref[...])
        _phase_copies(dbuf, dsh, tm)
        for c0 in range(0, D, CCOL):
            glu_c = glu[:, c0:c0 + CCOL]
            acc = jnp.zeros((tm, CCOL), F32)
            for k in range(CONV_W):
                moved = _shifted(dbuf, dsh, CONV_W - 1 - k, tm, c0)
                dw_ref[k:k + 1, c0:c0 + CCOL] += _colsum(moved * glu_c)
                acc = acc + w_ref[k:k + 1, c0:c0 + CCOL] * moved
            a_c = av[:, c0:c0 + CCOL]
            s_c = sg[:, c0:c0 + CCOL]
            da = acc * s_c
            dgt = acc * a_c * s_c * (1.0 - s_c)
            dbin_ref[:, c0:c0 + CCOL] += _colsum(da)
            dbin_ref[:, D + c0:D + c0 + CCOL] += _colsum(dgt)
            du_ref[:, c0:c0 + CCOL] = da.astype(BF16)
            du_ref[:, D + c0:D + c0 + CCOL] = dgt.astype(BF16)

    return pl.pallas_call(
        body, grid=(nt,),
        in_specs=[_row(tm, D), pl.BlockSpec((CHALO, D), lambda i: (jnp.minimum((i + 1) * hb, T // CHALO - 1), 0)),
                  _row(tm, D2), _full(CONV_W, D)],
        out_specs=[_row(tm, D2), _full(CHALO, D), _full(1, D2)],
        out_shape=[jax.ShapeDtypeStruct((T, D2), BF16), jax.ShapeDtypeStruct((CHALO, D), F32),
                   jax.ShapeDtypeStruct((1, D2), F32)],
        scratch_shapes=[pltpu.VMEM((tm + CHALO, D), F32), pltpu.VMEM((SUB - 1, tm + CHALO, D), F32)],
        name=name, compiler_params=_cp("arbitrary"))(dcv, dcv, u, dw_w)


HB = 4


def _lb0(lg_ref):
    l0, l1, l2 = lg_ref[0:1, :], lg_ref[1:2, :], lg_ref[2:3, :]
    m = jnp.maximum(jnp.maximum(l0, l1), l2)
    e0 = jnp.exp(l0 - m)
    return e0 / (e0 + jnp.exp(l1 - m) + jnp.exp(l2 - m))


def _mm_exact(m01, x):
    hi = x.astype(BF16)
    r1 = x - hi.astype(F32)
    mid = r1.astype(BF16)
    lo = (r1 - mid.astype(F32)).astype(BF16)
    return _dot(m01, hi) + _dot(m01, mid) + _dot(m01, lo)


def _block_tri(tm):
    r = jnp.arange(tm)[:, None]
    c = jnp.arange(tm)[None, :]
    same = (r // BLK) == (c // BLK)
    return (same & (c <= r)).astype(BF16), (same & (c >= r)).astype(BF16)


def _halves(x):
    return [x[0:SUB, :], x[SUB:BLK, :]]


def _live_halves(s):
    return ([(0, s)] if s < SUB else []) + [(1, max(s - SUB, 0))]


def _const_spec(shape):
    return pl.BlockSpec(shape, lambda h, i: (0, 0))


def _hgrn_specs(H, hb, tm, idx):
    g = H // hb
    return [pl.BlockSpec((tm, hb * HEAD), lambda h, i: (idx(i), h)),
            pl.BlockSpec((tm, hb * HEAD), lambda h, i: (idx(i), g + h)),
            pl.BlockSpec((tm, hb * HEAD), lambda h, i: (idx(i), 2 * g + h)),
            pl.BlockSpec((3, hb * HEAD), lambda h, i: (0, h))]


def hgrn_scan(proj, lb_logits, *, name, tm=128):
    T = proj.shape[0]
    H = proj.shape[1] // (4 * HEAD)
    hb = min(HB, H)
    tm = min(tm, T)
    nt = T // tm
    nblk = tm // BLK
    tril, _ = _block_tri(tm)
    heads = [slice(hh * HEAD, (hh + 1) * HEAD) for hh in range(hb)]

    def body(qp_ref, fz_ref, v_ref, lg_ref, tril_ref, o_ref, st_ref, S_ref, q_s, k_s, b_s):
        @pl.when(pl.program_id(1) == 0)
        def _():
            S_ref[...] = jnp.zeros_like(S_ref)

        st_ref[...] = S_ref[...]
        lb = _lb0(lg_ref)
        f = lb + (1.0 - lb) * _sig(fz_ref[...])
        q_s[...] = _silu(qp_ref[...])
        k_s[...] = 1.0 - f
        b_s[...] = _mm_exact(tril_ref[...], jnp.log(f))
        rows = lax.broadcasted_iota(jnp.int32, (BLK, HEAD), 0)
        S = [S_ref[hh] for hh in range(hb)]
        for nb in range(nblk):
            blk = slice(nb * BLK, (nb + 1) * BLK)
            last = slice(nb * BLK + BLK - 1, nb * BLK + BLK)
            qb = [q_s[blk, c] for c in heads]
            bb = [b_s[blk, c] for c in heads]
            o = [_dot_nt((qb[hh] * jnp.exp(bb[hh])).astype(BF16), S[hh].astype(BF16)) for hh in range(hb)]
            for hh, c in enumerate(heads):
                bc = b_s[last, c]
                kd = k_s[blk, c] * jnp.exp(bc - bb[hh])
                S[hh] = S[hh] * jnp.exp(bc) + _dot_tn(v_ref[blk, c].astype(BF16), kd.astype(BF16))
            for s in range(BLK):
                r = slice(nb * BLK + s, nb * BLK + s + 1)
                for hh, c in enumerate(heads):
                    dec = jnp.exp(jnp.where(rows >= s, bb[hh] - b_s[r, c], NEG))
                    a = jnp.sum(qb[hh] * k_s[r, c] * dec, axis=-1, keepdims=True)
                    o[hh] = o[hh] + a * v_ref[r, c]
            for hh, c in enumerate(heads):
                o_ref[blk, c] = o[hh]
        for hh in range(hb):
            S_ref[hh] = S[hh]

    return pl.pallas_call(
        body, grid=(H // hb, nt),
        in_specs=_hgrn_specs(H, hb, tm, lambda i: i) + [_const_spec((tm, tm))],
        out_specs=[pl.BlockSpec((tm, hb * HEAD), lambda h, i: (i, h)),
                   pl.BlockSpec((None, hb, HEAD, HEAD), lambda h, i: (i, h, 0, 0))],
        out_shape=[jax.ShapeDtypeStruct((T, H * HEAD), F32), jax.ShapeDtypeStruct((nt, H, HEAD, HEAD), F32)],
        scratch_shapes=[pltpu.VMEM((hb, HEAD, HEAD), F32)] + [pltpu.VMEM((tm, hb * HEAD), F32)] * 3, name=name,
        compiler_params=_cp("parallel", "arbitrary"))(proj, proj, proj, lb_logits, tril)


def hgrn_scan_bwd(proj, lb_logits, states, do, *, name, tm=128):
    T = proj.shape[0]
    H = proj.shape[1] // (4 * HEAD)
    hb = min(HB, H)
    tm = min(tm, T)
    nt = T // tm
    nblk = tm // BLK
    tril, triu = _block_tri(tm)
    sel = (jnp.arange(BLK * SUB)[None, :] // SUB == jnp.arange(BLK)[:, None]).astype(BF16)
    heads = [slice(hh * HEAD, (hh + 1) * HEAD) for hh in range(hb)]

    def body(qp_ref, fz_ref, v_ref, lg_ref, st_ref, do_ref, tril_ref, triu_ref, sel_ref, d3_ref, dlb_ref,
             dS_ref, Sb_ref, q_s, k_s, b_s, dq_s, dk_s, dv_s, db_s, pk_s, pv_s):
        i = pl.program_id(1)

        @pl.when(i == 0)
        def _():
            dS_ref[...] = jnp.zeros_like(dS_ref)
            dlb_ref[...] = jnp.zeros_like(dlb_ref)

        lb = _lb0(lg_ref)
        qp = qp_ref[...]
        sg = _sig(fz_ref[...])
        f = lb + (1.0 - lb) * sg
        q_s[...] = _silu(qp)
        k_s[...] = 1.0 - f
        b_s[...] = _mm_exact(tril_ref[...], jnp.log(f))
        rows = lax.broadcasted_iota(jnp.int32, (SUB, HEAD), 0)
        rows1 = lax.broadcasted_iota(jnp.int32, (SUB, 1), 0)

        S = [st_ref[hh] for hh in range(hb)]
        for nb in range(nblk):
            blk = slice(nb * BLK, (nb + 1) * BLK)
            last = slice(nb * BLK + BLK - 1, nb * BLK + BLK)
            for hh, c in enumerate(heads):
                Sb_ref[nb * hb + hh] = S[hh]
                if nb < nblk - 1:
                    bc = b_s[last, c]
                    kd = k_s[blk, c] * jnp.exp(bc - b_s[blk, c])
                    S[hh] = S[hh] * jnp.exp(bc) + _dot_tn(v_ref[blk, c].astype(BF16), kd.astype(BF16))

        dS = [dS_ref[hh] for hh in range(hb)]
        for nb in reversed(range(nblk)):
            blk = slice(nb * BLK, (nb + 1) * BLK)
            last = slice(nb * BLK + BLK - 1, nb * BLK + BLK)
            qb, kb, bb, dob, dq, dbc, ebc = [], [], [], [], [], [], []
            for hh, c in enumerate(heads):
                S0 = Sb_ref[nb * hb + hh]
                qb.append(q_s[blk, c])
                kb.append(k_s[blk, c])
                bb.append(b_s[blk, c])
                dob.append(do_ref[blk, c])
                bc = b_s[last, c]
                eb = jnp.exp(bb[hh])
                ekd = jnp.exp(bc - bb[hh])
                ebc.append(jnp.exp(bc))
                dS16 = dS[hh].astype(BF16)
                dob16 = dob[hh].astype(BF16)
                dq.append(_dot(dob16, S0.astype(BF16)) * eb)
                dki = _dot(v_ref[blk, c].astype(BF16), dS16) * ekd
                dk_s[blk, c] = dki
                dv_s[blk, c] = _dot_nt((kb[hh] * ekd).astype(BF16), dS16)
                dbc.append(_colsum(dS[hh] * S0) * ebc[hh] + _colsum(kb[hh] * dki))
                dS[hh] = dS[hh] * ebc[hh] + _dot_tn(dob16, (qb[hh] * eb).astype(BF16))
            qh, bh, doh, dqh = [[_halves(t[hh]) for hh in range(hb)] for t in (qb, bb, dob, dq)]
            for s in range(BLK):
                r = slice(nb * BLK + s, nb * BLK + s + 1)
                for hh, c in enumerate(heads):
                    ks = k_s[r, c]
                    pk, pv = None, None
                    for hf, lo in _live_halves(s):
                        diff = bh[hh][hf] - b_s[r, c]
                        dec = jnp.exp(diff if lo == 0 else jnp.where(rows >= lo, diff, NEG))
                        w = qh[hh][hf] * dec
                        a = jnp.sum(w * ks, axis=-1, keepdims=True)
                        da = jnp.sum(doh[hh][hf] * v_ref[r, c], axis=-1, keepdims=True)
                        if lo:
                            da = jnp.where(rows1 >= lo, da, 0.0)
                        dqh[hh][hf] = dqh[hh][hf] + (da * ks) * dec
                        pk = da * w if pk is None else pk + da * w
                        pv = a * doh[hh][hf] if pv is None else pv + a * doh[hh][hf]
                    pk_s[hh, s * SUB:(s + 1) * SUB, :] = pk
                    pv_s[hh, s * SUB:(s + 1) * SUB, :] = pv
            for hh, c in enumerate(heads):
                khi, klo = _split2(pk_s[hh])
                dk_s[blk, c] += _dot(sel_ref[...], khi) + _dot(sel_ref[...], klo)
                dv_s[blk, c] += _dot(sel_ref[...], pv_s[hh].astype(BF16))
                dq[hh] = jnp.concatenate(dqh[hh], axis=0)
                dq_s[blk, c] = dq[hh]
                db_s[blk, c] = qb[hh] * dq[hh] - kb[hh] * dk_s[blk, c]
                db_s[last, c] += dbc[hh]
        for hh in range(hb):
            dS_ref[hh] = dS[hh]

        dlf = _mm_exact(triu_ref[...], db_s[...])
        df = dlf / f - dk_s[...]
        d3_ref[0] = (dq_s[...] * _dsilu(qp)).astype(BF16)
        d3_ref[1] = (df * (1.0 - lb) * sg * (1.0 - sg)).astype(BF16)
        d3_ref[2] = dv_s[...].astype(BF16)
        dlb_ref[...] += _colsum(df * (1.0 - sg))

    rev = lambda i: nt - 1 - i
    out_blk = pl.BlockSpec((tm, hb * HEAD), lambda h, i: (rev(i), h))
    return pl.pallas_call(
        body, grid=(H // hb, nt),
        in_specs=_hgrn_specs(H, hb, tm, rev) + [pl.BlockSpec((None, hb, HEAD, HEAD), lambda h, i: (rev(i), h, 0, 0)),
                                                out_blk, _const_spec((tm, tm)), _const_spec((tm, tm)),
                                                _const_spec((BLK, BLK * SUB))],
        out_specs=[pl.BlockSpec((3, tm, hb * HEAD), lambda h, i: (0, rev(i), h)),
                   pl.BlockSpec((1, hb * HEAD), lambda h, i: (0, h))],
        out_shape=[jax.ShapeDtypeStruct((3, T, H * HEAD), BF16), jax.ShapeDtypeStruct((1, H * HEAD), F32)],
        scratch_shapes=[pltpu.VMEM((hb, HEAD, HEAD), F32), pltpu.VMEM((nblk * hb, HEAD, HEAD), F32)]
        + [pltpu.VMEM((tm, hb * HEAD), F32)] * 7 + [pltpu.VMEM((hb, BLK * SUB, HEAD), F32)] * 2, name=name,
        compiler_params=_cp("parallel", "arbitrary"))(proj, proj, proj, lb_logits, states, do, tril, triu, sel)


def hgrn_gate(o, proj, gn, *, name, tm=512):
    T, D = o.shape
    H = D // HEAD
    tm = min(tm, T)

    def body(o_ref, gp_ref, gn_ref, og_ref):
        gn_ = gn_ref[...]
        for h in range(H):
            c = slice(h * HEAD, (h + 1) * HEAD)
            oh = o_ref[:, c]
            r = lax.rsqrt(_rowmean(oh * oh) + EPS)
            og_ref[:, c] = ((oh * r) * gn_ * _silu(gp_ref[:, c])).astype(BF16)

    return pl.pallas_call(
        body, grid=(T // tm,),
        in_specs=[_row(tm, D), pl.BlockSpec((tm, D), lambda i: (i, 3)), _full(1, HEAD)],
        out_specs=_row(tm, D), out_shape=jax.ShapeDtypeStruct((T, D), BF16), name=name,
        compiler_params=_cp("parallel"))(o, proj, gn)


def hgrn_gate_bwd(dog, o, proj, gn, *, name, tm=512):
    T, D = o.shape
    H = D // HEAD
    tm = min(tm, T)

    def body(dog_ref, o_ref, gp_ref, gn_ref, do_ref, dgp_ref, dgn_ref):
        _acc_init(pl.program_id(0), dgn_ref)
        gn_ = gn_ref[...]
        for h in range(H):
            c = slice(h * HEAD, (h + 1) * HEAD)
            oh = o_ref[:, c]
            gp = gp_ref[:, c]
            dg = dog_ref[:, c]
            r = lax.rsqrt(_rowmean(oh * oh) + EPS)
            on = oh * r
            dgp_ref[:, c] = (dg * (on * gn_) * _dsilu(gp)).astype(BF16)
            don = dg * _silu(gp)
            dgn_ref[...] += _colsum(don * on)
            dn = don * gn_
            do_ref[:, c] = r * (dn - on * _rowmean(dn * on))

    return pl.pallas_call(
        body, grid=(T // tm,),
        in_specs=[_row(tm, D), _row(tm, D), pl.BlockSpec((tm, D), lambda i: (i, 3)), _full(1, HEAD)],
        out_specs=[_row(tm, D), _row(tm, D), _full(1, HEAD)],
        out_shape=[jax.ShapeDtypeStruct((T, D), F32), jax.ShapeDtypeStruct((T, D), BF16),
                   jax.ShapeDtypeStruct((1, HEAD), F32)], name=name,
        compiler_params=_cp("arbitrary"))(dog, o, proj, gn)


def _split2(x):
    hi = x.astype(BF16)
    return hi, (x - hi.astype(F32)).astype(BF16)


def ada_mod(c_all, ada_w, *, name):
    L, D, N = ada_w.shape
    B = c_all.shape[0]

    def body(c_ref, w_ref, o_ref):
        chi, clo = _split2(_silu(c_ref[...]))
        whi, wlo = _split2(w_ref[...])
        o_ref[...] = _dot(chi, whi) + _dot(chi, wlo) + _dot(clo, whi)

    return pl.pallas_call(
        body, grid=(L,), in_specs=[_full(B, D), pl.BlockSpec((None, D, N), lambda l: (l, 0, 0))],
        out_specs=pl.BlockSpec((None, B, N), lambda l: (l, 0, 0)),
        out_shape=jax.ShapeDtypeStruct((L, B, N), F32), name=name, compiler_params=_cp("parallel"))(c_all, ada_w)


def ada_wgrad(c_all_t, dmod, *, name, tr=256):
    D, B = c_all_t.shape
    L, _, N = dmod.shape
    tr = min(tr, D)

    def body(c_ref, d_ref, o_ref):
        cond = _silu(c_ref[...])
        acc = cond[:, 0:1] * d_ref[0:1, :]
        for b in range(1, B):
            acc = acc + cond[:, b:b + 1] * d_ref[b:b + 1, :]
        o_ref[...] = acc

    return pl.pallas_call(
        body, grid=(L, D // tr),
        in_specs=[pl.BlockSpec((tr, B), lambda l, r: (r, 0)), pl.BlockSpec((None, B, N), lambda l, r: (l, 0, 0))],
        out_specs=pl.BlockSpec((None, tr, N), lambda l, r: (l, r, 0)),
        out_shape=jax.ShapeDtypeStruct((L, D, N), F32), name=name,
        compiler_params=_cp("parallel", "parallel"))(c_all_t, dmod)


def sum_devices(parts, *, name):
    n, R, C = parts.shape

    def body(p_ref, o_ref):
        acc = p_ref[0]
        for d in range(1, n):
            acc = acc + p_ref[d]
        o_ref[...] = acc

    return pl.pallas_call(body, in_specs=[VMEM_SPEC], out_specs=VMEM_SPEC,
                          out_shape=jax.ShapeDtypeStruct((R, C), F32), name=name)(parts)


def lb_logits_grad(lb_logits, dlb, *, name):
    def body(lg_ref, d_ref, o_ref):
        l0, l1, l2 = lg_ref[0:1, :], lg_ref[1:2, :], lg_ref[2:3, :]
        m = jnp.maximum(jnp.maximum(l0, l1), l2)
        e0, e1, e2 = jnp.exp(l0 - m), jnp.exp(l1 - m), jnp.exp(l2 - m)
        z = e0 + e1 + e2
        p0, p1, p2 = e0 / z, e1 / z, e2 / z
        g = d_ref[...] * p0
        o_ref[0:1, :] = g * (1.0 - p0)
        o_ref[1:2, :] = -g * p1
        o_ref[2:3, :] = -g * p2

    return pl.pallas_call(body, in_specs=[VMEM_SPEC, VMEM_SPEC], out_specs=VMEM_SPEC,
                          out_shape=jax.ShapeDtypeStruct(lb_logits.shape, F32), name=name)(lb_logits, dlb)


def adamw(w, g, m, v, *, name, tr=256):
    R, C = w.shape
    tr = _tile(R, tr)

    def body(w_ref, g_ref, m_ref, v_ref, d_ref, nm_ref, nv_ref):
        gv = g_ref[...]
        nm = ADAM_B1 * m_ref[...] + (1.0 - ADAM_B1) * gv
        nv = ADAM_B2 * v_ref[...] + (1.0 - ADAM_B2) * (gv * gv)
        m_hat = nm / (1.0 - ADAM_B1 ** ADAM_STEP)
        v_hat = nv / (1.0 - ADAM_B2 ** ADAM_STEP)
        d_ref[...] = -ADAM_LR * (m_hat / (jnp.sqrt(v_hat) + ADAM_EPS) + ADAM_WD * w_ref[...])
        nm_ref[...] = nm
        nv_ref[...] = nv

    spec = pl.BlockSpec((tr, C), lambda i: (i, 0))
    return pl.pallas_call(
        body, grid=(R // tr,), in_specs=[spec] * 4, out_specs=[spec] * 3,
        out_shape=[jax.ShapeDtypeStruct((R, C), F32)] * 3, name=name, compiler_params=_cp("parallel"))(w, g, m, v)


def _place():
    return lax.axis_index("x"), lax.axis_index("y"), lax.axis_index("c")


def _flip(v, bit):
    return 1 - v if bit else v


def allgather_devices(v, *, name):
    R, C = v.shape

    def body(v_ref, out_ref, send_sems, recv_sems, local_sem):
        x, y, c = _place()
        me = 4 * x + 2 * y + c
        mine = pltpu.make_async_copy(v_ref, out_ref.at[me], local_sem)
        mine.start()
        sends = []
        for k in range(1, N_DEV):
            peer = (_flip(x, k & 4), _flip(y, k & 2), _flip(c, k & 1))
            cp = pltpu.make_async_remote_copy(src_ref=v_ref, dst_ref=out_ref.at[me], send_sem=send_sems.at[k - 1],
                                              recv_sem=recv_sems.at[k - 1], device_id=peer, device_id_type=MESH)
            cp.start()
            sends.append(cp)
        for k in range(1, N_DEV):
            px, py, pc = _flip(x, k & 4), _flip(y, k & 2), _flip(c, k & 1)
            pltpu.make_async_remote_copy(src_ref=v_ref, dst_ref=out_ref.at[4 * px + 2 * py + pc],
                                         send_sem=send_sems.at[k - 1], recv_sem=recv_sems.at[k - 1],
                                         device_id=(px, py, pc), device_id_type=MESH).wait_recv()
        for cp in sends:
            cp.wait_send()
        mine.wait()

    return pl.pallas_call(
        body, in_specs=[VMEM_SPEC], out_specs=VMEM_SPEC, out_shape=jax.ShapeDtypeStruct((N_DEV, R, C), v.dtype),
        scratch_shapes=[pltpu.SemaphoreType.DMA((N_DEV - 1,)), pltpu.SemaphoreType.DMA((N_DEV - 1,)),
                        pltpu.SemaphoreType.DMA], name=name)(v)


def _other_chips(x, y):
    return [(1 - x, y), (x, 1 - y), (1 - x, 1 - y)]


SEM = pl.BlockSpec(memory_space=pltpu.SEMAPHORE)
DATAFLOW = pltpu.SideEffectType.DATAFLOW_SIDE_EFFECTING


def _chip_copy(buf, a, j, q, c, chips, send_sems, recv_sems):
    px, py = chips[j]
    return pltpu.make_async_remote_copy(src_ref=buf.at[q, c], dst_ref=buf.at[q, c], send_sem=send_sems.at[3 * a + j],
                                        recv_sem=recv_sems.at[3 * a + j], device_id=(px, py, c), device_id_type=MESH)


def allgather_chips_start(bufs, *, name):
    n = len(bufs)

    def body(*refs):
        send_sems, recv_sems = refs[n], refs[n + 1]
        outs = refs[n + 2:2 * n + 2]
        token = refs[2 * n + 2]
        x, y, c = _place()
        chips = _other_chips(x, y)
        for a in range(n):
            for j in range(3):
                _chip_copy(outs[a], a, j, 2 * x + y, c, chips, send_sems, recv_sems).start()
        token[...] = jnp.zeros_like(token)

    res = pl.pallas_call(
        body, name=name, in_specs=[HBM] * n,
        out_specs=(SEM, SEM, *([HBM] * n), VMEM_SPEC),
        out_shape=(pltpu.SemaphoreType.DMA((3 * n,)), pltpu.SemaphoreType.DMA((3 * n,)),
                   *[pltpu.HBM(b.shape, b.dtype) for b in bufs], jax.ShapeDtypeStruct((SUB, LANE), F32)),
        input_output_aliases={a: a + 2 for a in range(n)},
        compiler_params=pltpu.CompilerParams(has_side_effects=DATAFLOW),
    )(*[pltpu.with_memory_space_constraint(b, pltpu.HBM) for b in bufs])
    return res[0], res[1], list(res[2:2 + n]), res[2 + n]


def allgather_chips_wait(send_sems, recv_sems, bufs, after, *, name):
    n = len(bufs)

    def body(*refs):
        ins = refs[:n]
        send_sems, recv_sems = refs[n], refs[n + 1]
        x, y, c = _place()
        chips = _other_chips(x, y)
        for a in range(n):
            for j, (px, py) in enumerate(chips):
                _chip_copy(ins[a], a, j, 2 * x + y, c, chips, send_sems, recv_sems).wait_send()
                _chip_copy(ins[a], a, j, 2 * px + py, c, chips, send_sems, recv_sems).wait_recv()

    return list(pl.pallas_call(
        body, name=name, in_specs=[HBM] * n + [SEM, SEM, pl.BlockSpec(memory_space=pl.ANY)],
        out_specs=[HBM] * n, out_shape=[pltpu.HBM(b.shape, b.dtype) for b in bufs],
        input_output_aliases={a: a for a in range(n)},
        compiler_params=pltpu.CompilerParams(has_side_effects=DATAFLOW),
    )(*bufs, send_sems, recv_sems, after))


def forward_to_sibling(bufs, *, name):
    n = len(bufs)

    def body(*refs):
        outs = refs[n:2 * n]
        send_sems, recv_sems = refs[2 * n:]
        x, y, c = _place()
        chips = _other_chips(x, y)

        def copy(a, j, half, to):
            px, py = chips[j]
            slab = outs[a].at[2 * px + py, half]
            return pltpu.make_async_remote_copy(src_ref=slab, dst_ref=slab, send_sem=send_sems.at[a, j],
                                                recv_sem=recv_sems.at[a, j], device_id=to, device_id_type=MESH)

        sends = [copy(a, j, c, (x, y, 1 - c)) for a in range(n) for j in range(3)]
        for cp in sends:
            cp.start()
        for a in range(n):
            for j in range(3):
                copy(a, j, 1 - c, (x, y, c)).wait_recv()
        for cp in sends:
            cp.wait_send()

    return pl.pallas_call(
        body, in_specs=[HBM] * n, out_specs=[HBM] * n,
        out_shape=[jax.ShapeDtypeStruct(b.shape, b.dtype) for b in bufs],
        input_output_aliases={a: a for a in range(n)},
        scratch_shapes=[pltpu.SemaphoreType.DMA((n, 3)), pltpu.SemaphoreType.DMA((n, 3))], name=name)(*bufs)


def pair_exchange(grads, *, name):
    n = len(grads)

    def body(*refs):
        ins, outs = refs[:n], refs[n:2 * n]
        send_sems, recv_sems = refs[2 * n:]
        x, y, c = _place()
        cps = [pltpu.make_async_remote_copy(src_ref=ins[a].at[1 - c], dst_ref=outs[a], send_sem=send_sems.at[a],
                                            recv_sem=recv_sems.at[a], device_id=(x, y, 1 - c), device_id_type=MESH)
               for a in range(n)]
        for cp in cps:
            cp.start()
        for cp in cps:
            cp.wait_recv()
        for cp in cps:
            cp.wait_send()

    return pl.pallas_call(
        body, in_specs=[HBM] * n, out_specs=[HBM] * n,
        out_shape=[jax.ShapeDtypeStruct(g.shape[1:], g.dtype) for g in grads],
        scratch_shapes=[pltpu.SemaphoreType.DMA((n,)), pltpu.SemaphoreType.DMA((n,))], name=name)(*grads)


def pair_add(g, other, c_idx, *, name, tr=256):
    _, Q, R, C = g.shape
    tr = _tile(R, tr)

    def body(c_ref, g_ref, o_ref, out_ref):
        out_ref[...] = (g_ref[...] + o_ref[...]).astype(BF16)

    return pl.pallas_call(
        body,
        grid_spec=pltpu.PrefetchScalarGridSpec(
            num_scalar_prefetch=1, grid=(Q, R // tr),
            in_specs=[pl.BlockSpec((None, None, tr, C), lambda q, r, c_ref: (c_ref[0], q, r, 0)),
                      pl.BlockSpec((None, tr, C), lambda q, r, c_ref: (q, r, 0))],
            out_specs=pl.BlockSpec((None, tr, C), lambda q, r, c_ref: (q, r, 0))),
        out_shape=jax.ShapeDtypeStruct((Q, R, C), BF16), name=name,
        compiler_params=_cp("parallel", "parallel"))(c_idx, g, other)


def chip_sum(sums, landed, qc_idx, *, name, tr=256):
    _, R, C = sums.shape
    tr = _tile(R, tr)

    def body(qc_ref, own_ref, l_ref, o_ref):
        acc = own_ref[...].astype(F32)
        for k in range(3):
            acc = acc + l_ref[k].astype(F32)
        o_ref[...] = acc

    return pl.pallas_call(
        body,
        grid_spec=pltpu.PrefetchScalarGridSpec(
            num_scalar_prefetch=1, grid=(R // tr,),
            in_specs=[pl.BlockSpec((None, tr, C), lambda r, qc: (qc[0], r, 0)),
                      pl.BlockSpec((3, tr, C), lambda r, qc: (0, r, 0))],
            out_specs=pl.BlockSpec((None, tr, C), lambda r, qc: (qc[1], r, 0))),
        out_shape=jax.ShapeDtypeStruct((2, R, C), F32), name=name,
        compiler_params=_cp("parallel"))(qc_idx, sums, landed)


def half_swap(bufs, *, name):
    n = len(bufs)

    def body(*refs):
        outs = refs[n:2 * n]
        send_sems, recv_sems = refs[2 * n:]
        x, y, c = _place()
        cps = [pltpu.make_async_remote_copy(src_ref=outs[a].at[c], dst_ref=outs[a].at[c], send_sem=send_sems.at[a],
                                            recv_sem=recv_sems.at[a], device_id=(x, y, 1 - c), device_id_type=MESH)
               for a in range(n)]
        for cp in cps:
            cp.start()
        for a in range(n):
            pltpu.make_async_remote_copy(src_ref=outs[a].at[c], dst_ref=outs[a].at[1 - c], send_sem=send_sems.at[a],
                                         recv_sem=recv_sems.at[a], device_id=(x, y, 1 - c),
                                         device_id_type=MESH).wait_recv()
        for cp in cps:
            cp.wait_send()

    return pl.pallas_call(
        body, in_specs=[HBM] * n, out_specs=[HBM] * n,
        out_shape=[jax.ShapeDtypeStruct(b.shape, b.dtype) for b in bufs],
        input_output_aliases={a: a for a in range(n)},
        scratch_shapes=[pltpu.SemaphoreType.DMA((n,)), pltpu.SemaphoreType.DMA((n,))], name=name)(*bufs)


def _exchange_copy(sums, landed, a, j, c, chips, send_sems, recv_sems):
    px, py = chips[j]
    return pltpu.make_async_remote_copy(src_ref=sums.at[2 * px + py], dst_ref=landed.at[j],
                                        send_sem=send_sems.at[3 * a + j], recv_sem=recv_sems.at[3 * a + j],
                                        device_id=(px, py, c), device_id_type=MESH)


def chip_exchange_start(sums, *, name):
    n = len(sums)
    landing = [lax.empty((3,) + s.shape[1:], s.dtype) for s in sums]

    def body(*refs):
        send_sems, recv_sems = refs[2 * n], refs[2 * n + 1]
        src, dst = refs[2 * n + 2:3 * n + 2], refs[3 * n + 2:4 * n + 2]
        token = refs[4 * n + 2]
        x, y, c = _place()
        chips = _other_chips(x, y)
        for a in range(n):
            for j in range(3):
                _exchange_copy(src[a], dst[a], a, j, c, chips, send_sems, recv_sems).start()
        token[...] = jnp.zeros_like(token)

    res = pl.pallas_call(
        body, name=name, in_specs=[HBM] * (2 * n),
        out_specs=(SEM, SEM, *([HBM] * (2 * n)), VMEM_SPEC),
        out_shape=(pltpu.SemaphoreType.DMA((3 * n,)), pltpu.SemaphoreType.DMA((3 * n,)),
                   *[pltpu.HBM(b.shape, b.dtype) for b in sums + landing], jax.ShapeDtypeStruct((SUB, LANE), F32)),
        input_output_aliases={a: a + 2 for a in range(2 * n)},
        compiler_params=pltpu.CompilerParams(has_side_effects=DATAFLOW),
    )(*[pltpu.with_memory_space_constraint(b, pltpu.HBM) for b in sums + landing])
    return res[0], res[1], list(res[2:2 + n]), list(res[2 + n:2 + 2 * n]), res[2 + 2 * n]


def chip_exchange_wait(send_sems, recv_sems, sums, landed, after, *, name):
    n = len(sums)

    def body(*refs):
        src, dst = refs[:n], refs[n:2 * n]
        send_sems, recv_sems = refs[2 * n], refs[2 * n + 1]
        x, y, c = _place()
        chips = _other_chips(x, y)
        for a in range(n):
            for j in range(3):
                cp = _exchange_copy(src[a], dst[a], a, j, c, chips, send_sems, recv_sems)
                cp.wait_send()
                cp.wait_recv()

    res = pl.pallas_call(
        body, name=name, in_specs=[HBM] * (2 * n) + [SEM, SEM, pl.BlockSpec(memory_space=pl.ANY)],
        out_specs=[HBM] * (2 * n), out_shape=[pltpu.HBM(b.shape, b.dtype) for b in sums + landed],
        input_output_aliases={a: a for a in range(2 * n)},
        compiler_params=pltpu.CompilerParams(has_side_effects=DATAFLOW),
    )(*sums, *landed, send_sems, recv_sems, after)
    return list(res[:n]), list(res[n:])


def pair_reduce(grads, c, tag):
    c_idx = c.astype(jnp.int32).reshape(1)
    others = pair_exchange(grads, name=f"grad_pair_exchange_{tag}")
    return [pair_add(g, o, c_idx, name=f"grad_pair_add_{tag}{a}") for a, (g, o) in enumerate(zip(grads, others))]


def finish_reduce(sums, landed, q, c, tag):
    qc_idx = jnp.stack([q, c]).astype(jnp.int32)
    return [chip_sum(s, l, qc_idx, name=f"grad_chip_sum_{tag}{a}") for a, (s, l) in enumerate(zip(sums, landed))]


def _ffn_forward(x, mod, pre_g, post_g, w_up, w_down, dw_w, dw_b, tag, tgt=None):
    sh, sc, gate = mod
    h = prenorm(x, pre_g, sc, sh, name=f"{tag}_prenorm")
    u0 = mm_nn(h, w_up, name=f"{tag}_up", out_dtype=BF16, perm=_ffn_perm)
    z, ab = ffn_act(u0, dw_w, dw_b, name=f"{tag}_act")
    y = mm_nn(z, w_down, name=f"{tag}_down")
    if tgt is None:
        out = post_residual(x, y, post_g, gate, name=f"{tag}_post")
    else:
        out = post_residual_loss(x, y, post_g, gate, tgt, name=f"{tag}_post_loss")
    return out, (x, h, u0, ab, z, y)


def _ffn_backward(dx, saved, mod, pre_g, post_g, w_up, w_down, dw_w, dw_b, tag):
    x, h, u0, ab, z, y = saved
    sh, sc, gate = mod
    dy, dgate, dpost, _ = post_bwd(dx, y, post_g, gate, name=f"{tag}_post_bwd")
    dz = mm_nt(dy, w_down, name=f"{tag}_down_dx", out_dtype=BF16)
    g_down = mm_tn(z, dy, name=f"{tag}_down_dw", J=2, block="a", row_chips=2)
    du0, dconv = ffn_act_bwd(dz, u0, ab, dw_w, name=f"{tag}_act_bwd")
    dh = mm_nt(du0, w_up, name=f"{tag}_up_dx", perm=_ffn_perm)
    g_up = mm_tn(h, du0, name=f"{tag}_up_dw", J=4, block="b", perm=_ffn_perm)
    dx_in, dsh, dsc, dpre = prenorm_bwd(dh, x, dx, pre_g, sc, name=f"{tag}_prenorm_bwd")
    nb = u0.shape[1] // 4
    dconv = dconv[:, 0].reshape(4, 2, 2, nb).transpose(0, 2, 1, 3).reshape(4, 4 * nb)
    return dx_in, dict(dsh=dsh, dsc=dsc, dgate=dgate, dpre=dpre, dpost=dpost, g_up=g_up, g_down=g_down,
                       d_dw_w=dconv[0:FFN_W], d_dw_b=dconv[3:4])


def _local_step(x, tgt, mods, P, late_weights=None, grads_ready=None):
    m0, m1 = mods
    h1 = prenorm(x, P["pre_mix_g"][0:1], m0[1], m0[0], name="hgrn_prenorm")
    proj = mm_nn(h1, P["hgrn_w_in"], name="hgrn_in")
    o, states = hgrn_scan(proj, P["hgrn_lb_logits"], name="hgrn_scan")
    og = hgrn_gate(o, proj, P["hgrn_gnorm_g"], name="hgrn_gate")
    y1 = mm_nn(og, P["hgrn_w_out"], name="hgrn_out")
    x1 = post_residual(x, y1, P["post_mix_g"][0:1], m0[2], name="hgrn_post")
    if late_weights is not None:
        P = {**P, **late_weights(x1)}
    x2, ffn0 = _ffn_forward(x1, m0[3:6], P["pre_ffn_g"][0:1], P["post_ffn_g"][0:1], P["ffn_w_up"][0],
                            P["ffn_w_down"][0], P["ffn_dw_w"][0], P["ffn_dw_b"][0:1], "ffn0")
    h3 = prenorm(x2, P["pre_mix_g"][1:2], m1[1], m1[0], name="conv_prenorm")
    u = mm_nn(h3, P["conv_w_in"], name="conv_in", bias=P["conv_b_in"])
    s, cv = conv_act(u, P["conv_dw_w"], P["conv_dw_b"], P["conv_ln_g"], P["conv_ln_b"], name="conv_act")
    y3 = mm_nn(s, P["conv_w_out"], name="conv_out", bias=P["conv_b_out"])
    x3 = post_residual(x2, y3, P["post_mix_g"][1:2], m1[2], name="conv_post")
    (dx4, lcols), ffn1 = _ffn_forward(x3, m1[3:6], P["pre_ffn_g"][1:2], P["post_ffn_g"][1:2], P["ffn_w_up"][1],
                                      P["ffn_w_down"][1], P["ffn_dw_w"][1], P["ffn_dw_b"][1:2], "ffn1", tgt)
    dx3, f1 = _ffn_backward(dx4, ffn1, m1[3:6], P["pre_ffn_g"][1:2], P["post_ffn_g"][1:2], P["ffn_w_up"][1],
                            P["ffn_w_down"][1], P["ffn_dw_w"][1], P["ffn_dw_b"][1:2], "ffn1")
    dy3, dg1_1, dpostmix1, d_b_out = post_bwd(dx3, y3, P["post_mix_g"][1:2], m1[2], name="conv_post_bwd")
    ds = mm_nt(dy3, P["conv_w_out"], name="conv_out_dx")
    g_conv_out = mm_tn(s, dy3, name="conv_out_dw", J=1, block="a", row_chips=4)
    dcv, d_ln_g, d_ln_b, d_dw_b = conv_norm_bwd(ds, cv, P["conv_ln_g"], P["conv_ln_b"], name="conv_norm_bwd")
    du, d_dw_w, d_b_in = conv_glu_bwd(dcv, u, P["conv_dw_w"], name="conv_glu_bwd")
    dh3 = mm_nt(du, P["conv_w_in"], name="conv_in_dx")
    g_conv_in = mm_tn(h3, du, name="conv_in_dw", J=2, block="b", col_chips=2)
    dx2, dsh1_1, dsc1_1, dpremix1 = prenorm_bwd(dh3, x2, dx3, P["pre_mix_g"][1:2], m1[1], name="conv_prenorm_bwd")
    if grads_ready is not None:
        token = grads_ready("l1", [g_conv_in, g_conv_out, f1["g_up"], f1["g_down"]])
        m0 = tuple(m + token[0:1, 0:1] for m in m0)
    dx1, f0 = _ffn_backward(dx2, ffn0, m0[3:6], P["pre_ffn_g"][0:1], P["post_ffn_g"][0:1], P["ffn_w_up"][0],
                            P["ffn_w_down"][0], P["ffn_dw_w"][0], P["ffn_dw_b"][0:1], "ffn0")
    if grads_ready is not None:
        token = grads_ready("f0", [f0["g_up"], f0["g_down"]])
        m0 = tuple(m + token[0:1, 0:1] for m in m0)
    dy1, dg1_0, dpostmix0, _ = post_bwd(dx1, y1, P["post_mix_g"][0:1], m0[2], name="hgrn_post_bwd")
    dog = mm_nt(dy1, P["hgrn_w_out"], name="hgrn_out_dx")
    g_hgrn_out = mm_tn(og, dy1, name="hgrn_out_dw", J=1, block="a", row_chips=4)
    do, dgp, d_gn = hgrn_gate_bwd(dog, o, proj, P["hgrn_gnorm_g"], name="hgrn_gate_bwd")
    d3, dlb = hgrn_scan_bwd(proj, P["hgrn_lb_logits"], states, do, name="hgrn_scan_bwd")
    g_hgrn_in = mm_tn_parts(h1, d3, dgp, name="hgrn_in_dw")
    token = grads_ready("hg", [g_hgrn_in, g_hgrn_out]) if grads_ready is not None else None
    dh1 = mm_nt_parts(d3, dgp, P["hgrn_w_in"], name="hgrn_in_dx", after=token)
    dx0, dsh1_0, dsc1_0, dpremix0 = prenorm_bwd(dh1, x, dx1, P["pre_mix_g"][0:1], m0[1], name="hgrn_prenorm_bwd")

    dmod = jnp.stack([
        jnp.concatenate([dsh1_0, dsc1_0, dg1_0, f0["dsh"], f0["dsc"], f0["dgate"]], axis=1)[0],
        jnp.concatenate([dsh1_1, dsc1_1, dg1_1, f1["dsh"], f1["dsc"], f1["dgate"]], axis=1)[0]])
    small = dict(
        loss=lcols,
        pre_mix_g=jnp.concatenate([dpremix0, dpremix1]), post_mix_g=jnp.concatenate([dpostmix0, dpostmix1]),
        pre_ffn_g=jnp.concatenate([f0["dpre"], f1["dpre"]]), post_ffn_g=jnp.concatenate([f0["dpost"], f1["dpost"]]),
        lb=dlb, hgrn_gnorm_g=d_gn, ffn_dw_b=jnp.concatenate([f0["d_dw_b"], f1["d_dw_b"]]), dmod=dmod,
        conv_b_in=d_b_in, conv_dw_w=d_dw_w[0:CONV_W], conv_dw_b=d_dw_b, conv_ln_g=d_ln_g, conv_ln_b=d_ln_b,
        conv_b_out=d_b_out, ffn_dw_w=jnp.stack([f0["d_dw_w"], f1["d_dw_w"]]))
    big = [g_hgrn_in, g_hgrn_out, g_conv_in, g_conv_out, f0["g_up"], f1["g_up"], f0["g_down"], f1["g_down"]]
    return dx0, small, big


def _pack(parts, rows=8):
    flat = jnp.concatenate([p.reshape(-1).astype(F32) for p in parts])
    per = rows * 128
    pad = (-flat.shape[0]) % per
    return jnp.pad(flat, (0, pad)).reshape(rows, -1)


def _unpack(flat, shapes):
    out, off = [], 0
    for s in shapes:
        n = 1
        for d in s:
            n *= d
        out.append(flat[..., off:off + n].reshape(flat.shape[:-1] + tuple(s)))
        off += n
    return out


def _from_chips(stacked, axis):
    moved = jnp.moveaxis(stacked, 0, axis)
    shape = list(moved.shape)
    return moved.reshape(shape[:axis] + [shape[axis] * shape[axis + 1]] + shape[axis + 2:])


def _my_shard(full, axis, q):
    n = full.shape[axis] // N_CHIPS
    return lax.dynamic_slice_in_dim(full, q * n, n, axis=axis)


def kernel(x, c, ada_w, ada_b, pre_mix_g, post_mix_g, pre_ffn_g, post_ffn_g, hgrn_w_in, hgrn_lb_logits, hgrn_gnorm_g, hgrn_w_out, conv_w_in, conv_b_in, conv_dw_w, conv_dw_b, conv_ln_g, conv_ln_b, conv_w_out, conv_b_out, ffn_w_up, ffn_dw_w, ffn_dw_b, ffn_w_down, loss_target, m_ada_w, m_ada_b, m_pre_mix_g, m_post_mix_g, m_pre_ffn_g, m_post_ffn_g, m_hgrn_w_in, m_hgrn_lb_logits, m_hgrn_gnorm_g, m_hgrn_w_out, m_conv_w_in, m_conv_b_in, m_conv_dw_w, m_conv_dw_b, m_conv_ln_g, m_conv_ln_b, m_conv_w_out, m_conv_b_out, m_ffn_w_up, m_ffn_dw_w, m_ffn_dw_b, m_ffn_w_down, v_ada_w, v_ada_b, v_pre_mix_g, v_post_mix_g, v_pre_ffn_g, v_post_ffn_g, v_hgrn_w_in, v_hgrn_lb_logits, v_hgrn_gnorm_g, v_hgrn_w_out, v_conv_w_in, v_conv_b_in, v_conv_dw_w, v_conv_dw_b, v_conv_ln_g, v_conv_ln_b, v_conv_w_out, v_conv_b_out, v_ffn_w_up, v_ffn_dw_w, v_ffn_dw_b, v_ffn_w_down):
    W = dict(ada_w=ada_w, ada_b=ada_b, pre_mix_g=pre_mix_g, post_mix_g=post_mix_g, pre_ffn_g=pre_ffn_g,
             post_ffn_g=post_ffn_g, hgrn_w_in=hgrn_w_in, hgrn_lb_logits=hgrn_lb_logits, hgrn_gnorm_g=hgrn_gnorm_g,
             hgrn_w_out=hgrn_w_out, conv_w_in=conv_w_in, conv_b_in=conv_b_in, conv_dw_w=conv_dw_w,
             conv_dw_b=conv_dw_b, conv_ln_g=conv_ln_g, conv_ln_b=conv_ln_b, conv_w_out=conv_w_out,
             conv_b_out=conv_b_out, ffn_w_up=ffn_w_up, ffn_dw_w=ffn_dw_w, ffn_dw_b=ffn_dw_b, ffn_w_down=ffn_w_down)
    M = dict(ada_w=m_ada_w, ada_b=m_ada_b, pre_mix_g=m_pre_mix_g, post_mix_g=m_post_mix_g, pre_ffn_g=m_pre_ffn_g,
             post_ffn_g=m_post_ffn_g, hgrn_w_in=m_hgrn_w_in, hgrn_lb_logits=m_hgrn_lb_logits,
             hgrn_gnorm_g=m_hgrn_gnorm_g, hgrn_w_out=m_hgrn_w_out, conv_w_in=m_conv_w_in, conv_b_in=m_conv_b_in,
             conv_dw_w=m_conv_dw_w, conv_dw_b=m_conv_dw_b, conv_ln_g=m_conv_ln_g, conv_ln_b=m_conv_ln_b,
             conv_w_out=m_conv_w_out, conv_b_out=m_conv_b_out, ffn_w_up=m_ffn_w_up, ffn_dw_w=m_ffn_dw_w,
             ffn_dw_b=m_ffn_dw_b, ffn_w_down=m_ffn_w_down)
    V = dict(ada_w=v_ada_w, ada_b=v_ada_b, pre_mix_g=v_pre_mix_g, post_mix_g=v_post_mix_g, pre_ffn_g=v_pre_ffn_g,
             post_ffn_g=v_post_ffn_g, hgrn_w_in=v_hgrn_w_in, hgrn_lb_logits=v_hgrn_lb_logits,
             hgrn_gnorm_g=v_hgrn_gnorm_g, hgrn_w_out=v_hgrn_w_out, conv_w_in=v_conv_w_in, conv_b_in=v_conv_b_in,
             conv_dw_w=v_conv_dw_w, conv_dw_b=v_conv_dw_b, conv_ln_g=v_conv_ln_g, conv_ln_b=v_conv_ln_b,
             conv_w_out=v_conv_w_out, conv_b_out=v_conv_b_out, ffn_w_up=v_ffn_w_up, ffn_dw_w=v_ffn_dw_w,
             ffn_dw_b=v_ffn_dw_b, ffn_w_down=v_ffn_w_down)
    names = list(W)
    xi, yi, ci = lax.axis_index("x"), lax.axis_index("y"), lax.axis_index("c")
    q = 2 * xi + yi
    me = 2 * q + ci
    D = x.shape[-1]
    L = ada_w.shape[0]

    small_w = ["conv_b_in", "conv_dw_w", "conv_dw_b", "conv_ln_g", "conv_ln_b", "conv_b_out", "ffn_dw_w"]
    small_axis = dict(conv_b_in=1, conv_dw_w=2, conv_dw_b=1, conv_ln_g=1, conv_ln_b=1, conv_b_out=1, ffn_dw_w=2)
    packed = _pack([c] + [W[n] for n in small_w])

    def halves(w):
        shard = w.astype(BF16).reshape(1, 2, w.shape[0] // 2, w.shape[1])
        buf = lax.empty((N_CHIPS,) + shard.shape[1:], BF16)
        return lax.dynamic_update_slice_in_dim(buf, shard, q, axis=0)

    hg_send, hg_recv, hg_bufs, hg_token = allgather_chips_start([halves(hgrn_w_in[0]), halves(hgrn_w_out[0])],
                                                                name="gather_hgrn_weights_start")
    packed, _ = lax.optimization_barrier((packed, hg_token))
    gathered = allgather_devices(packed, name="gather_small_params").reshape(N_DEV, -1)
    c_all = gathered[:, 0:D]
    per_chip = gathered.reshape(N_CHIPS, 2, -1)[:, 0, D:]
    parts = _unpack(per_chip, [W[n].shape for n in small_w])
    P = {n: _from_chips(p, small_axis[n]) for n, p in zip(small_w, parts)}
    P["conv_dw_w"] = P["conv_dw_w"][0]
    for n in ("pre_mix_g", "post_mix_g", "pre_ffn_g", "post_ffn_g", "hgrn_lb_logits", "hgrn_gnorm_g", "ffn_dw_b"):
        P[n] = W[n]

    modp = ada_mod(c_all, ada_w, name="ada_mod")
    ncol = modp.shape[-1]
    mod_all = allgather_devices(modp.reshape(L * N_DEV, ncol), name="gather_mod")
    mod_all = mod_all.reshape(N_CHIPS, 2, L, N_DEV, ncol)[:, 0]
    mod_me = lax.dynamic_index_in_dim(mod_all, me, axis=2, keepdims=False)
    mod = mod_me.transpose(1, 0, 2).reshape(L, N_CHIPS * ncol) + ada_b
    mods = [tuple(mod[l:l + 1, k * D:(k + 1) * D] for k in range(6)) for l in range(L)]

    stack = lambda t: t.reshape(N_CHIPS, t.shape[1] * t.shape[2], t.shape[3])
    rowsh = lambda t: t.reshape(1, N_CHIPS * t.shape[1] * t.shape[2], t.shape[3])
    pairs = lambda t: t.reshape(2, 2, t.shape[1], t.shape[2]).transpose(0, 2, 1, 3).reshape(2, t.shape[1], 2 * t.shape[2])
    g = forward_to_sibling(allgather_chips_wait(hg_send, hg_recv, hg_bufs, mod, name="gather_hgrn_weights_wait"),
                           name="gather_hgrn_weights_forward")
    P["hgrn_w_in"], P["hgrn_w_out"] = stack(g[0]), rowsh(g[1])
    late_shards = [conv_w_in[0], conv_w_out[0], ffn_w_up[0], ffn_w_up[1], ffn_w_down[0], ffn_w_down[1]]
    late_bufs, _, _ = lax.optimization_barrier(([halves(w) for w in late_shards], g, mod))
    send_sems, recv_sems, bufs, token = allgather_chips_start(late_bufs, name="gather_weights_start")
    mods[0] = tuple(m + token[0:1, 0:1] for m in mods[0])

    def late_weights(x1):
        landed = allgather_chips_wait(send_sems, recv_sems, bufs, x1, name="gather_weights_wait")
        g = forward_to_sibling(landed, name="gather_weights_forward")
        return dict(conv_w_in=pairs(stack(g[0])), conv_w_out=rowsh(g[1]), ffn_w_up=[stack(g[2]), stack(g[3])],
                    ffn_w_down=[rowsh(g[4]), rowsh(g[5])])

    in_flight = {}

    def grads_ready(tag, grads):
        sums = pair_reduce(grads, ci, f"{tag}_")
        send, recv, sums, landing, tok = chip_exchange_start(sums, name=f"grad_chip_exchange_start_{tag}")
        in_flight[tag] = (send, recv, sums, landing)
        return tok

    grad_x, small, big = _local_step(x[0], loss_target[0], mods, P, late_weights, grads_ready)

    small_names = list(small)
    gs = allgather_devices(_pack([small[n] for n in small_names]), name="gather_small_grads")
    dmod_all = _unpack(gs.reshape(N_DEV, -1), [small[n].shape for n in small_names])[small_names.index("dmod")]
    tot = sum_devices(gs, name="sum_small_grads").reshape(1, -1)
    S = dict(zip(small_names, _unpack(tot, [small[n].shape for n in small_names])))
    S = {n: v[0] for n, v in S.items()}
    loss = 0.5 * jnp.sum(S["loss"]) / D

    G = {}
    dmod_q = lax.dynamic_slice_in_dim(dmod_all, q * ncol, ncol, axis=2)
    G["ada_w"] = ada_wgrad(c_all.T, dmod_q.transpose(1, 0, 2), name="ada_wgrad")
    G["ada_b"] = S["dmod"]
    for n in ("pre_mix_g", "post_mix_g", "pre_ffn_g", "post_ffn_g", "hgrn_gnorm_g", "ffn_dw_b"):
        G[n] = S[n]
    G["hgrn_lb_logits"] = lb_logits_grad(hgrn_lb_logits, S["lb"], name="lb_logits_grad")
    G["conv_b_in"] = _my_shard(S["conv_b_in"], 1, q)
    G["conv_dw_w"] = _my_shard(S["conv_dw_w"], 1, q)[None]
    for n in ("conv_dw_b", "conv_ln_g", "conv_ln_b", "conv_b_out"):
        G[n] = _my_shard(S[n], 1, q)
    G["ffn_dw_w"] = _my_shard(S["ffn_dw_w"], 2, q)

    halves = []
    for tag in ("f0", "l1"):
        sums_t, landed_t = chip_exchange_wait(*in_flight[tag], grad_x, name=f"grad_chip_exchange_wait_{tag}")
        halves += finish_reduce(sums_t, landed_t, q, ci, f"{tag}_")
    red = [f.reshape(2 * f.shape[1], f.shape[2]) for f in half_swap(halves, name="grad_half_swap")]
    G["conv_w_in"], G["conv_w_out"] = red[2][None], red[3][None]
    G["ffn_w_up"] = jnp.stack([red[0], red[4]])
    G["ffn_w_down"] = jnp.stack([red[1], red[5]])

    delta, new_m, new_v = {}, {}, {}

    def adamw_matrix(n):
        shp = W[n].shape
        two = lambda t: t.reshape(-1, shp[-1])
        d_, m_, v_ = adamw(two(W[n]), two(G[n]), two(M[n]), two(V[n]), name=f"adamw_{n}")
        delta[n], new_m[n], new_v[n] = d_.reshape(shp), m_.reshape(shp), v_.reshape(shp)

    big_names = ["ada_w", "hgrn_w_in", "hgrn_w_out", "conv_w_in", "conv_w_out", "ffn_w_up", "ffn_w_down"]
    for n in ("ada_w", "conv_w_in", "conv_w_out", "ffn_w_up", "ffn_w_down"):
        adamw_matrix(n)
    sums_h, landed_h = chip_exchange_wait(*in_flight["hg"], delta["ffn_w_down"], name="grad_chip_exchange_wait_hg")
    red_h = half_swap(finish_reduce(sums_h, landed_h, q, ci, "hg_"), name="grad_half_swap_hg")
    G["hgrn_w_in"], G["hgrn_w_out"] = [f.reshape(1, 2 * f.shape[1], f.shape[2]) for f in red_h]
    for n in ("hgrn_w_in", "hgrn_w_out"):
        adamw_matrix(n)
    rest = [n for n in names if n not in big_names]
    d_, m_, v_ = adamw(_pack([W[n] for n in rest]), _pack([G[n] for n in rest]), _pack([M[n] for n in rest]),
                       _pack([V[n] for n in rest]), name="adamw_small")
    shapes = [W[n].shape for n in rest]
    for n, a, b_, c_ in zip(rest, _unpack(d_.reshape(-1), shapes), _unpack(m_.reshape(-1), shapes),
                            _unpack(v_.reshape(-1), shapes)):
        delta[n], new_m[n], new_v[n] = a, b_, c_

    return (loss, grad_x[None], *[G[n].reshape(W[n].shape) for n in names], *[delta[n] for n in names],
            *[new_m[n] for n in names], *[new_v[n] for n in names])
```

```python
import jax
import jax.numpy as jnp
from jax import lax
from jax.experimental import pallas as pl
from jax.experimental.pallas import tpu as pltpu

F32 = jnp.float32
BF16 = jnp.bfloat16
EPS = 1e-6
HEAD = 128
BLK = 16
NEG = -1e30
CONV_W = 31
FFN_W = 3
N_CHIPS = 4
N_DEV = 8
SUB = 8
LANE = 128
V7X_VMEM_LIMIT = 56 * 1024 * 1024
MESH = pl.DeviceIdType.MESH
HBM = pl.BlockSpec(memory_space=pltpu.HBM)
VMEM_SPEC = pl.BlockSpec(memory_space=pltpu.VMEM)

ADAM_LR = 0.001
ADAM_B1 = 0.9
ADAM_B2 = 0.999
ADAM_EPS = 1e-08
ADAM_WD = 0.01
ADAM_STEP = 10


def _cp(*sem):
    return pltpu.CompilerParams(dimension_semantics=sem, vmem_limit_bytes=V7X_VMEM_LIMIT)


def _sig(x):
    return 0.5 * jnp.tanh(0.5 * x) + 0.5


def _silu(x):
    return x * _sig(x)


def _dsilu(x):
    s = _sig(x)
    return s * (1.0 + x * (1.0 - s))


def _dot(a, b):
    return jnp.dot(a, b, preferred_element_type=F32)


def _dot_nt(a, b):
    return lax.dot_general(a, b, (((1,), (1,)), ((), ())), preferred_element_type=F32)


def _dot_tn(a, b):
    return lax.dot_general(a, b, (((0,), (0,)), ((), ())), preferred_element_type=F32)


def _colsum(x):
    return jnp.sum(x, axis=0, keepdims=True)


def _rowmean(x):
    return jnp.mean(x, axis=-1, keepdims=True)


def _ffn_perm(j):
    return (j % 2) * 2 + j // 2


def _tile(n, pref):
    if n <= pref:
        return n
    t = pref - pref % 8
    while n % t:
        t -= 8
    return t


def mm_nn(a, w, *, name, bias=None, out_dtype=F32, perm=None, tm=1024):
    T, K = a.shape
    J, _, nb = w.shape
    tm = min(tm, T)
    col = (lambda j: j) if perm is None else perm

    def body(a_ref, w_ref, *rest):
        acc = _dot(a_ref[...], w_ref[...])
        if bias is not None:
            acc = acc + rest[0][...]
        rest[-1][...] = acc.astype(out_dtype)

    in_specs = [pl.BlockSpec((tm, K), lambda j, i: (i, 0)), pl.BlockSpec((None, K, nb), lambda j, i: (j, 0, 0))]
    args = [a, w]
    if bias is not None:
        in_specs.append(pl.BlockSpec((1, nb), lambda j, i: (0, j)))
        args.append(bias)
    return pl.pallas_call(
        body, grid=(J, T // tm), in_specs=in_specs,
        out_specs=pl.BlockSpec((tm, nb), lambda j, i: (i, col(j))),
        out_shape=jax.ShapeDtypeStruct((T, J * nb), out_dtype), name=name,
        compiler_params=_cp("parallel", "parallel"))(*args)


def mm_nt(a, w, *, name, out_dtype=F32, perm=None, tm=1024, after=None):
    T = a.shape[0]
    J, K, nb = w.shape
    tm = min(tm, T)
    col = (lambda j: j) if perm is None else perm
    deps = [] if after is None else [after]

    def body(a_ref, w_ref, *rest):
        o_ref, acc_ref = rest[len(deps):]
        j = pl.program_id(1)

        @pl.when(j == 0)
        def _():
            acc_ref[...] = jnp.zeros_like(acc_ref)

        acc_ref[...] += _dot_nt(a_ref[...], w_ref[...])

        @pl.when(j == J - 1)
        def _():
            o_ref[...] = acc_ref[...].astype(out_dtype)

    return pl.pallas_call(
        body, grid=(T // tm, J),
        in_specs=[pl.BlockSpec((tm, nb), lambda i, j: (i, col(j))), pl.BlockSpec((None, K, nb), lambda i, j: (j, 0, 0))]
        + [pl.BlockSpec(memory_space=pl.ANY)] * len(deps),
        out_specs=pl.BlockSpec((tm, K), lambda i, j: (i, 0)),
        out_shape=jax.ShapeDtypeStruct((T, K), out_dtype),
        scratch_shapes=[pltpu.VMEM((tm, K), F32)], name=name,
        compiler_params=_cp("parallel", "arbitrary"))(a, w, *deps)


def mm_tn(a, b, *, name, J, block, row_chips=1, col_chips=1, perm=None, tk=1024):
    T = a.shape[0]
    tk = min(tk, T)
    col = (lambda j: j) if perm is None else perm
    if block == "b":
        rows, nb = a.shape[1], b.shape[1] // J
        a_spec = pl.BlockSpec((tk, rows), lambda j, t: (t, 0))
        b_spec = pl.BlockSpec((tk, nb), lambda j, t: (t, col(j)))
    else:
        rows, nb = a.shape[1] // J, b.shape[1]
        a_spec = pl.BlockSpec((tk, rows), lambda j, t: (t, col(j)))
        b_spec = pl.BlockSpec((tk, nb), lambda j, t: (t, 0))
    rh = rows // (2 * row_chips)
    nc = nb // col_chips
    chips = [(rc, cc) for rc in range(row_chips) for cc in range(col_chips)]

    def body(a_ref, b_ref, o_ref):
        @pl.when(pl.program_id(1) == 0)
        def _():
            o_ref[...] = jnp.zeros_like(o_ref)

        acc = _dot_tn(a_ref[...], b_ref[...])
        for ch, (rc, cc) in enumerate(chips):
            for hf in range(2):
                r0 = (rc * 2 + hf) * rh
                o_ref[hf, ch] += acc[r0:r0 + rh, cc * nc:(cc + 1) * nc]

    return pl.pallas_call(
        body, grid=(J, T // tk), in_specs=[a_spec, b_spec],
        out_specs=pl.BlockSpec((2, len(chips), rh, nc), lambda j, t: (0, j, 0, 0)),
        out_shape=jax.ShapeDtypeStruct((2, J * len(chips), rh, nc), F32), name=name,
        compiler_params=_cp("parallel", "arbitrary"))(a, b)


def mm_nt_parts(s3, g, w, *, name, tm=1024, after=None):
    n3, T, nb = s3.shape
    J, K, _ = w.shape
    tm = min(tm, T)
    deps = [] if after is None else [after]

    def body(s_ref, g_ref, w_ref, *rest):
        o_ref, acc_ref = rest[len(deps):]
        j = pl.program_id(1)

        @pl.when(j == 0)
        def _():
            acc_ref[...] = jnp.zeros_like(acc_ref)

        @pl.when(j < n3)
        def _():
            acc_ref[...] += _dot_nt(s_ref[...], w_ref[...])

        @pl.when(j == n3)
        def _():
            acc_ref[...] += _dot_nt(g_ref[...], w_ref[...])

        @pl.when(j == J - 1)
        def _():
            o_ref[...] = acc_ref[...]

    return pl.pallas_call(
        body, grid=(T // tm, J),
        in_specs=[pl.BlockSpec((None, tm, nb), lambda i, j: (jnp.minimum(j, n3 - 1), i, 0)),
                  pl.BlockSpec((tm, nb), lambda i, j: (i, 0)), pl.BlockSpec((None, K, nb), lambda i, j: (j, 0, 0))]
        + [pl.BlockSpec(memory_space=pl.ANY)] * len(deps),
        out_specs=pl.BlockSpec((tm, K), lambda i, j: (i, 0)), out_shape=jax.ShapeDtypeStruct((T, K), F32),
        scratch_shapes=[pltpu.VMEM((tm, K), F32)], name=name,
        compiler_params=_cp("parallel", "arbitrary"))(s3, g, w, *deps)


def mm_tn_parts(a, s3, g, *, name, tk=1024):
    n3, T, nb = s3.shape
    J = n3 + 1
    tk = min(tk, T)
    rows = a.shape[1]
    rh = rows // 2

    def body(a_ref, s_ref, g_ref, o_ref):
        j = pl.program_id(0)

        @pl.when(pl.program_id(1) == 0)
        def _():
            o_ref[...] = jnp.zeros_like(o_ref)

        def add(b_ref):
            acc = _dot_tn(a_ref[...], b_ref[...])
            for hf in range(2):
                o_ref[hf, 0] += acc[hf * rh:(hf + 1) * rh, :]

        pl.when(j < n3)(lambda: add(s_ref))
        pl.when(j == n3)(lambda: add(g_ref))

    return pl.pallas_call(
        body, grid=(J, T // tk),
        in_specs=[pl.BlockSpec((tk, rows), lambda j, t: (t, 0)),
                  pl.BlockSpec((None, tk, nb), lambda j, t: (jnp.minimum(j, n3 - 1), t, 0)),
                  pl.BlockSpec((tk, nb), lambda j, t: (t, 0))],
        out_specs=pl.BlockSpec((2, 1, rh, nb), lambda j, t: (0, j, 0, 0)),
        out_shape=jax.ShapeDtypeStruct((2, J, rh, nb), F32), name=name,
        compiler_params=_cp("parallel", "arbitrary"))(a, s3, g)


def _row(tm, w):
    return pl.BlockSpec((tm, w), lambda i: (i, 0))


def _full(r, w):
    return pl.BlockSpec((r, w), lambda i: (0, 0))


def _acc_init(i, *refs):
    @pl.when(i == 0)
    def _():
        for r in refs:
            r[...] = jnp.zeros_like(r)


def prenorm(x, g, sc, sh, *, name, tm=512):
    T, D = x.shape
    tm = min(tm, T)

    def body(x_ref, g_ref, sc_ref, sh_ref, h_ref):
        xv = x_ref[...]
        r = lax.rsqrt(_rowmean(xv * xv) + EPS)
        h_ref[...] = ((xv * r) * g_ref[...] * (1.0 + sc_ref[...]) + sh_ref[...]).astype(BF16)

    return pl.pallas_call(
        body, grid=(T // tm,), in_specs=[_row(tm, D), _full(1, D), _full(1, D), _full(1, D)],
        out_specs=_row(tm, D), out_shape=jax.ShapeDtypeStruct((T, D), BF16), name=name,
        compiler_params=_cp("parallel"))(x, g, sc, sh)


def post_residual(x, y, g, gate, *, name, tm=512):
    T, D = x.shape
    tm = min(tm, T)

    def body(x_ref, y_ref, g_ref, gate_ref, o_ref):
        yv = y_ref[...]
        r = lax.rsqrt(_rowmean(yv * yv) + EPS)
        o_ref[...] = x_ref[...] + gate_ref[...] * ((yv * r) * g_ref[...])

    return pl.pallas_call(
        body, grid=(T // tm,), in_specs=[_row(tm, D), _row(tm, D), _full(1, D), _full(1, D)],
        out_specs=_row(tm, D), out_shape=jax.ShapeDtypeStruct((T, D), F32), name=name,
        compiler_params=_cp("parallel"))(x, y, g, gate)


def post_residual_loss(x, y, g, gate, tgt, *, name, tm=512):
    T, D = x.shape
    tm = min(tm, T)

    def body(x_ref, y_ref, g_ref, gate_ref, t_ref, dx_ref, l_ref):
        _acc_init(pl.program_id(0), l_ref)
        yv = y_ref[...]
        r = lax.rsqrt(_rowmean(yv * yv) + EPS)
        e = x_ref[...] + gate_ref[...] * ((yv * r) * g_ref[...]) - t_ref[...]
        dx_ref[...] = e * (1.0 / D)
        l_ref[...] += _colsum(e * e)

    return pl.pallas_call(
        body, grid=(T // tm,), in_specs=[_row(tm, D), _row(tm, D), _full(1, D), _full(1, D), _row(tm, D)],
        out_specs=[_row(tm, D), _full(1, D)],
        out_shape=[jax.ShapeDtypeStruct((T, D), F32), jax.ShapeDtypeStruct((1, D), F32)], name=name,
        compiler_params=_cp("arbitrary"))(x, y, g, gate, tgt)


def post_bwd(dx, y, g, gate, *, name, tm=512):
    T, D = dx.shape
    tm = min(tm, T)

    def body(dx_ref, y_ref, g_ref, gate_ref, dy_ref, dgate_ref, dg_ref, dbias_ref):
        _acc_init(pl.program_id(0), dgate_ref, dg_ref, dbias_ref)
        yv = y_ref[...]
        dxv = dx_ref[...]
        r = lax.rsqrt(_rowmean(yv * yv) + EPS)
        yn = yv * r
        gv = g_ref[...]
        gt = gate_ref[...]
        dgate_ref[...] += _colsum(dxv * (yn * gv))
        dg_ref[...] += _colsum(dxv * gt * yn)
        dyn = dxv * gt * gv
        dy = r * (dyn - yn * _rowmean(dyn * yn))
        dbias_ref[...] += _colsum(dy)
        dy_ref[...] = dy.astype(BF16)

    return pl.pallas_call(
        body, grid=(T // tm,), in_specs=[_row(tm, D), _row(tm, D), _full(1, D), _full(1, D)],
        out_specs=[_row(tm, D), _full(1, D), _full(1, D), _full(1, D)],
        out_shape=[jax.ShapeDtypeStruct((T, D), BF16)] + [jax.ShapeDtypeStruct((1, D), F32)] * 3, name=name,
        compiler_params=_cp("arbitrary"))(dx, y, g, gate)


def prenorm_bwd(dh, x, dres, g, sc, *, name, tm=512):
    T, D = x.shape
    tm = min(tm, T)

    def body(dh_ref, x_ref, dres_ref, g_ref, sc_ref, dx_ref, dsh_ref, dsc_ref, dg_ref):
        _acc_init(pl.program_id(0), dsh_ref, dsc_ref, dg_ref)
        xv = x_ref[...]
        dhv = dh_ref[...]
        r = lax.rsqrt(_rowmean(xv * xv) + EPS)
        xn = xv * r
        gv = g_ref[...]
        one_sc = 1.0 + sc_ref[...]
        dsh_ref[...] += _colsum(dhv)
        dsc_ref[...] += _colsum(dhv * (xn * gv))
        dg_ref[...] += _colsum(dhv * one_sc * xn)
        dxn = dhv * one_sc * gv
        dx_ref[...] = dres_ref[...] + r * (dxn - xn * _rowmean(dxn * xn))

    return pl.pallas_call(
        body, grid=(T // tm,), in_specs=[_row(tm, D), _row(tm, D), _row(tm, D), _full(1, D), _full(1, D)],
        out_specs=[_row(tm, D), _full(1, D), _full(1, D), _full(1, D)],
        out_shape=[jax.ShapeDtypeStruct((T, D), F32)] + [jax.ShapeDtypeStruct((1, D), F32)] * 3, name=name,
        compiler_params=_cp("arbitrary"))(dh, x, dres, g, sc)


HALO = 16


def _shift_helpers():
    rid = lax.broadcasted_iota(jnp.int32, (SUB, LANE), 0)

    def down(cur, prev, k):
        return pltpu.roll(jnp.where(rid >= SUB - k, prev, cur), k, 0)

    def up(cur, nxt, k):
        return pltpu.roll(jnp.where(rid < k, nxt, cur), SUB - k, 0)

    return down, up


def _ffn_sides(c, nb, wa_ref, wb_ref, ba_ref, bb_ref):
    cols = slice(c * LANE, (c + 1) * LANE)
    return [(cols, [wa_ref[k:k + 1, cols] for k in range(FFN_W)], ba_ref[:, cols]),
            (slice(nb + c * LANE, nb + (c + 1) * LANE), [wb_ref[k:k + 1, cols] for k in range(FFN_W)],
             bb_ref[:, cols])]


def _ffn_specs(tm, nb, hb, idx):
    return [pl.BlockSpec((tm, 2 * nb), lambda jc, i: (idx(i), jc)),
            pl.BlockSpec((HALO, 2 * nb), lambda jc, i: (jnp.maximum(idx(i) * hb - 1, 0), jc)),
            pl.BlockSpec((FFN_W, nb), lambda jc, i: (0, jc)),
            pl.BlockSpec((FFN_W, nb), lambda jc, i: (0, jc + 2)),
            pl.BlockSpec((1, nb), lambda jc, i: (0, jc)),
            pl.BlockSpec((1, nb), lambda jc, i: (0, jc + 2))]


def ffn_act(u0p, dw_w, dw_b, *, name, tm=256):
    T, W = u0p.shape
    nb = W // 4
    tm = min(tm, T)
    unroll = 4
    rows16 = 2 * SUB

    def body(u_ref, halo_ref, wa_ref, wb_ref, ba_ref, bb_ref, z_ref, ab_ref):
        i = pl.program_id(1)
        down, _ = _shift_helpers()
        for c in range(nb // LANE):
            cols = slice(c * LANE, (c + 1) * LANE)
            side = _ffn_sides(c, nb, wa_ref, wb_ref, ba_ref, bb_ref)

            def rows(j, prev):
                prev = list(prev)
                for m in range(unroll):
                    r0 = pl.multiple_of((j * unroll + m) * rows16, rows16)
                    x = [u_ref[pl.ds(r0, rows16), cs].astype(F32) for cs, _, _ in side]
                    conv = [[None, None], [None, None]]
                    for hf in range(2):
                        for n, (_, w, b) in enumerate(side):
                            cur = x[n][hf * SUB:(hf + 1) * SUB, :]
                            conv[n][hf] = b + w[2] * cur + w[1] * down(cur, prev[n], 1) + w[0] * down(cur, prev[n], 2)
                            prev[n] = cur
                    a, b = [jnp.concatenate(conv[n], axis=0) for n in range(2)]
                    z_ref[pl.ds(r0, rows16), cols] = (_silu(a) * b).astype(BF16)
                    ab_ref[pl.ds(r0, rows16), side[0][0]] = a.astype(BF16)
                    ab_ref[pl.ds(r0, rows16), side[1][0]] = b.astype(BF16)
                return tuple(prev)

            first = [jnp.where(i == 0, 0.0, halo_ref[:, cs].astype(F32)[SUB:2 * SUB, :]) for cs, _, _ in side]
            lax.fori_loop(0, tm // (rows16 * unroll), rows, tuple(first))

    return pl.pallas_call(
        body, grid=(2, T // tm), in_specs=_ffn_specs(tm, nb, tm // HALO, lambda i: i),
        out_specs=[pl.BlockSpec((tm, nb), lambda jc, i: (i, jc)), pl.BlockSpec((tm, 2 * nb), lambda jc, i: (i, jc))],
        out_shape=[jax.ShapeDtypeStruct((T, 2 * nb), BF16), jax.ShapeDtypeStruct((T, W), BF16)], name=name,
        compiler_params=_cp("parallel", "arbitrary"))(u0p, u0p, dw_w, dw_w, dw_b, dw_b)


def ffn_act_bwd(dz, u0p, ab, dw_w, *, name, tm=256):
    T, W = u0p.shape
    nb = W // 4
    tm = min(tm, T)
    nt = T // tm
    unroll = 4
    rows16 = 2 * SUB
    n_it = tm // (rows16 * unroll)

    def body(dz_ref, u_ref, ab_ref, wa_ref, wb_ref, du0_ref, dw_ref, carry):
        i = pl.program_id(1)
        _acc_init(i, dw_ref)
        _, up = _shift_helpers()
        for c in range(nb // LANE):
            cols = slice(c * LANE, (c + 1) * LANE)
            side = [(cols, [wa_ref[k:k + 1, cols] for k in range(FFN_W)]),
                    (slice(nb + c * LANE, nb + (c + 1) * LANE), [wb_ref[k:k + 1, cols] for k in range(FFN_W)])]

            def rows(j, st):
                nxt, acc = list(st[0:2]), list(st[2:10])
                for m in range(unroll):
                    r0 = pl.multiple_of(((n_it - 1 - j) * unroll + unroll - 1 - m) * rows16, rows16)
                    dzv = dz_ref[pl.ds(r0, rows16), cols].astype(F32)
                    a, b = [ab_ref[pl.ds(r0, rows16), cs].astype(F32) for cs, _ in side]
                    x = [u_ref[pl.ds(r0, rows16), cs].astype(F32) for cs, _ in side]
                    sa = _sig(a)
                    d16 = [dzv * b * (sa * (1.0 + a * (1.0 - sa))), dzv * (a * sa)]
                    out = [[None, None], [None, None]]
                    for hf in (1, 0):
                        half = slice(hf * SUB, (hf + 1) * SUB)
                        for n in range(2):
                            w = side[n][1]
                            d = d16[n][half, :]
                            u = x[n][half, :]
                            up1, up2 = up(d, nxt[n], 1), up(d, nxt[n], 2)
                            acc[4 * n + 0] = acc[4 * n + 0] + up2 * u
                            acc[4 * n + 1] = acc[4 * n + 1] + up1 * u
                            acc[4 * n + 2] = acc[4 * n + 2] + d * u
                            acc[4 * n + 3] = acc[4 * n + 3] + d
                            out[n][hf] = w[2] * d + w[1] * up1 + w[0] * up2
                            nxt[n] = d
                    for n in range(2):
                        du0_ref[pl.ds(r0, rows16), side[n][0]] = jnp.concatenate(out[n], axis=0).astype(BF16)
                return (*nxt, *acc)

            init = [jnp.where(i == 0, 0.0, carry[:, cs]) for cs, _ in side] + [jnp.zeros((SUB, LANE), F32)] * 8
            st = lax.fori_loop(0, n_it, rows, tuple(init))
            for n in range(2):
                carry[:, side[n][0]] = st[n]
                for k in range(4):
                    dw_ref[k, :, side[n][0]] += st[2 + 4 * n + k]

        @pl.when(i == nt - 1)
        def _():
            for k in range(4):
                dw_ref[k, 0:1, :] = _colsum(dw_ref[k])

    rev = lambda i: nt - 1 - i
    wide = pl.BlockSpec((tm, 2 * nb), lambda jc, i: (rev(i), jc))
    return pl.pallas_call(
        body, grid=(2, nt),
        in_specs=[pl.BlockSpec((tm, nb), lambda jc, i: (rev(i), jc)), wide, wide,
                  pl.BlockSpec((FFN_W, nb), lambda jc, i: (0, jc)), pl.BlockSpec((FFN_W, nb), lambda jc, i: (0, jc + 2))],
        out_specs=[wide, pl.BlockSpec((4, SUB, 2 * nb), lambda jc, i: (0, 0, jc))],
        out_shape=[jax.ShapeDtypeStruct((T, W), BF16), jax.ShapeDtypeStruct((4, SUB, W), F32)],
        scratch_shapes=[pltpu.VMEM((SUB, 2 * nb), F32)], name=name,
        compiler_params=_cp("parallel", "arbitrary"))(dz, u0p, ab, dw_w, dw_w)


CHALO = 32
CCOL = 256


def _phase_copies(buf, shifted, tm):
    n = tm + CHALO - SUB
    for p in range(1, SUB):
        shifted[p - 1, 0:n, :] = buf[p:p + n, :]


def _shifted(buf, shifted, r, tm, c0):
    m, p = divmod(r, SUB)
    src = buf if p == 0 else shifted.at[p - 1]
    return src[m * SUB:m * SUB + tm, c0:c0 + CCOL]


def conv_act(u, dw_w, dw_b, ln_g, ln_b, *, name, tm=128):
    T, D2 = u.shape
    D = D2 // 2
    tm = min(tm, T)
    hb = tm // CHALO

    def body(u_ref, halo_ref, w_ref, b_ref, g_ref, be_ref, s_ref, cv_ref, gbuf, gsh):
        i = pl.program_id(0)
        hv = halo_ref[...]
        gbuf[0:CHALO, :] = jnp.where(i == 0, 0.0, hv[:, 0:D] * _sig(hv[:, D:D2]))
        uv = u_ref[...]
        gbuf[CHALO:CHALO + tm, :] = uv[:, 0:D] * _sig(uv[:, D:D2])
        _phase_copies(gbuf, gsh, tm)
        for c0 in range(0, D, CCOL):
            acc = jnp.zeros((tm, CCOL), F32) + b_ref[:, c0:c0 + CCOL]
            for k in range(CONV_W):
                acc = acc + w_ref[k:k + 1, c0:c0 + CCOL] * _shifted(gbuf, gsh, CHALO - (CONV_W - 1) + k, tm, c0)
            cv_ref[:, c0:c0 + CCOL] = acc
        cv = cv_ref[...]
        mu = _rowmean(cv)
        xc = cv - mu
        nh = xc * lax.rsqrt(_rowmean(xc * xc) + EPS)
        s_ref[...] = _silu(nh * g_ref[...] + be_ref[...]).astype(BF16)

    return pl.pallas_call(
        body, grid=(T // tm,),
        in_specs=[_row(tm, D2), pl.BlockSpec((CHALO, D2), lambda i: (jnp.maximum(i * hb - 1, 0), 0)),
                  _full(CONV_W, D), _full(1, D), _full(1, D), _full(1, D)],
        out_specs=[_row(tm, D), _row(tm, D)],
        out_shape=[jax.ShapeDtypeStruct((T, D), BF16), jax.ShapeDtypeStruct((T, D), F32)],
        scratch_shapes=[pltpu.VMEM((tm + CHALO, D), F32), pltpu.VMEM((SUB - 1, tm + CHALO, D), F32)], name=name,
        compiler_params=_cp("arbitrary"))(u, u, dw_w, dw_b, ln_g, ln_b)


def conv_norm_bwd(ds, cv, ln_g, ln_b, *, name, tm=512):
    T, D = cv.shape
    tm = min(tm, T)

    def body(ds_ref, cv_ref, g_ref, be_ref, dcv_ref, dg_ref, dbe_ref, dcb_ref):
        _acc_init(pl.program_id(0), dg_ref, dbe_ref, dcb_ref)
        cv_ = cv_ref[...]
        mu = _rowmean(cv_)
        xc = cv_ - mu
        rstd = lax.rsqrt(_rowmean(xc * xc) + EPS)
        nh = xc * rstd
        gv = g_ref[...]
        dln = ds_ref[...] * _dsilu(nh * gv + be_ref[...])
        dg_ref[...] += _colsum(dln * nh)
        dbe_ref[...] += _colsum(dln)
        dnh = dln * gv
        dcv = rstd * (dnh - _rowmean(dnh) - nh * _rowmean(dnh * nh))
        dcb_ref[...] += _colsum(dcv)
        dcv_ref[...] = dcv

    return pl.pallas_call(
        body, grid=(T // tm,), in_specs=[_row(tm, D), _row(tm, D), _full(1, D), _full(1, D)],
        out_specs=[_row(tm, D), _full(1, D), _full(1, D), _full(1, D)],
        out_shape=[jax.ShapeDtypeStruct((T, D), F32)] + [jax.ShapeDtypeStruct((1, D), F32)] * 3, name=name,
        compiler_params=_cp("arbitrary"))(ds, cv, ln_g, ln_b)


def conv_glu_bwd(dcv, u, dw_w, *, name, tm=128):
    T, D2 = u.shape
    D = D2 // 2
    tm = min(tm, T)
    nt = T // tm
    hb = tm // CHALO

    def body(dcv_ref, dnext_ref, u_ref, w_ref, du_ref, dw_ref, dbin_ref, dbuf, dsh):
        i = pl.program_id(0)
        _acc_init(i, dw_ref, dbin_ref)
        uv = u_ref[...]
        av = uv[:, 0:D]
        sg = _sig(uv[:, D:D2])
        glu = av * sg
        dbuf[0:tm, :] = dcv_ref[...]
        dbuf[tm:tm + CHALO, :] = jnp.where(i == nt - 1, 0.0, dnext_ref[...])
        _phase_copies(dbuf, dsh, tm)
        for c0 in range(0, D, CCOL):
            glu_c = glu[:, c0:c0 + CCOL]
            acc = jnp.zeros((tm, CCOL), F32)
            for k in range(CONV_W):
                moved = _shifted(dbuf, dsh, CONV_W - 1 - k, tm, c0)
                dw_ref[k:k + 1, c0:c0 + CCOL] += _colsum(moved * glu_c)
                acc = acc + w_ref[k:k + 1, c0:c0 + CCOL] * moved
            a_c = av[:, c0:c0 + CCOL]
            s_c = sg[:, c0:c0 + CCOL]
            da = acc * s_c
            dgt = acc * a_c * s_c * (1.0 - s_c)
            dbin_ref[:, c0:c0 + CCOL] += _colsum(da)
            dbin_ref[:, D + c0:D + c0 + CCOL] += _colsum(dgt)
            du_ref[:, c0:c0 + CCOL] = da.astype(BF16)
            du_ref[:, D + c0:D + c0 + CCOL] = dgt.astype(BF16)

    return pl.pallas_call(
        body, grid=(nt,),
        in_specs=[_row(tm, D), pl.BlockSpec((CHALO, D), lambda i: (jnp.minimum((i + 1) * hb, T // CHALO - 1), 0)),
                  _row(tm, D2), _full(CONV_W, D)],
        out_specs=[_row(tm, D2), _full(CHALO, D), _full(1, D2)],
        out_shape=[jax.ShapeDtypeStruct((T, D2), BF16), jax.ShapeDtypeStruct((CHALO, D), F32),
                   jax.ShapeDtypeStruct((1, D2), F32)],
        scratch_shapes=[pltpu.VMEM((tm + CHALO, D), F32), pltpu.VMEM((SUB - 1, tm + CHALO, D), F32)],
        name=name, compiler_params=_cp("arbitrary"))(dcv, dcv, u, dw_w)


HB = 8


def _lb0(lg_ref):
    l0, l1, l2 = lg_ref[0:1, :], lg_ref[1:2, :], lg_ref[2:3, :]
    m = jnp.maximum(jnp.maximum(l0, l1), l2)
    e0 = jnp.exp(l0 - m)
    return e0 / (e0 + jnp.exp(l1 - m) + jnp.exp(l2 - m))


def _mm_exact(m01, x):
    hi = x.astype(BF16)
    r1 = x - hi.astype(F32)
    mid = r1.astype(BF16)
    lo = (r1 - mid.astype(F32)).astype(BF16)
    return _dot(m01, hi) + _dot(m01, mid) + _dot(m01, lo)


def _block_tri(tm):
    r = jnp.arange(tm)[:, None]
    c = jnp.arange(tm)[None, :]
    same = (r // BLK) == (c // BLK)
    return (same & (c <= r)).astype(BF16), (same & (c >= r)).astype(BF16)


def _halves(x):
    return [x[0:SUB, :], x[SUB:BLK, :]]


def _live_halves(s):
    return ([(0, s)] if s < SUB else []) + [(1, max(s - SUB, 0))]


def _const_spec(shape):
    return pl.BlockSpec(shape, lambda h, i: (0, 0))


def _hgrn_specs(H, hb, tm, idx):
    g = H // hb
    return [pl.BlockSpec((tm, hb * HEAD), lambda h, i: (idx(i), h)),
            pl.BlockSpec((tm, hb * HEAD), lambda h, i: (idx(i), g + h)),
            pl.BlockSpec((tm, hb * HEAD), lambda h, i: (idx(i), 2 * g + h)),
            pl.BlockSpec((3, hb * HEAD), lambda h, i: (0, h))]


def hgrn_scan(proj, lb_logits, *, name, tm=128):
    T = proj.shape[0]
    H = proj.shape[1] // (4 * HEAD)
    hb = min(HB, H)
    tm = min(tm, T)
    nt = T // tm
    nblk = tm // BLK
    tril, _ = _block_tri(tm)
    heads = [slice(hh * HEAD, (hh + 1) * HEAD) for hh in range(hb)]

    def body(qp_ref, fz_ref, v_ref, lg_ref, tril_ref, o_ref, st_ref, S_ref, q_s, k_s, b_s):
        @pl.when(pl.program_id(1) == 0)
        def _():
            S_ref[...] = jnp.zeros_like(S_ref)

        st_ref[...] = S_ref[...]
        lb = _lb0(lg_ref)
        f = lb + (1.0 - lb) * _sig(fz_ref[...])
        q_s[...] = _silu(qp_ref[...])
        k_s[...] = 1.0 - f
        b_s[...] = _mm_exact(tril_ref[...], jnp.log(f))
        rows = lax.broadcasted_iota(jnp.int32, (BLK, HEAD), 0)
        S = [S_ref[hh] for hh in range(hb)]
        for nb in range(nblk):
            blk = slice(nb * BLK, (nb + 1) * BLK)
            last = slice(nb * BLK + BLK - 1, nb * BLK + BLK)
            qb = [q_s[blk, c] for c in heads]
            bb = [b_s[blk, c] for c in heads]
            o = [_dot_nt((qb[hh] * jnp.exp(bb[hh])).astype(BF16), S[hh].astype(BF16)) for hh in range(hb)]
            for hh, c in enumerate(heads):
                bc = b_s[last, c]
                kd = k_s[blk, c] * jnp.exp(bc - bb[hh])
                S[hh] = S[hh] * jnp.exp(bc) + _dot_tn(v_ref[blk, c].astype(BF16), kd.astype(BF16))
            for s in range(BLK):
                r = slice(nb * BLK + s, nb * BLK + s + 1)
                for hh, c in enumerate(heads):
                    dec = jnp.exp(jnp.where(rows >= s, bb[hh] - b_s[r, c], NEG))
                    a = jnp.sum(qb[hh] * k_s[r, c] * dec, axis=-1, keepdims=True)
                    o[hh] = o[hh] + a * v_ref[r, c]
            for hh, c in enumerate(heads):
                o_ref[blk, c] = o[hh]
        for hh in range(hb):
            S_ref[hh] = S[hh]

    return pl.pallas_call(
        body, grid=(H // hb, nt),
        in_specs=_hgrn_specs(H, hb, tm, lambda i: i) + [_const_spec((tm, tm))],
        out_specs=[pl.BlockSpec((tm, hb * HEAD), lambda h, i: (i, h)),
                   pl.BlockSpec((None, hb, HEAD, HEAD), lambda h, i: (i, h, 0, 0))],
        out_shape=[jax.ShapeDtypeStruct((T, H * HEAD), F32), jax.ShapeDtypeStruct((nt, H, HEAD, HEAD), F32)],
        scratch_shapes=[pltpu.VMEM((hb, HEAD, HEAD), F32)] + [pltpu.VMEM((tm, hb * HEAD), F32)] * 3, name=name,
        compiler_params=_cp("parallel", "arbitrary"))(proj, proj, proj, lb_logits, tril)


def hgrn_scan_bwd(proj, lb_logits, states, do, *, name, tm=128):
    T = proj.shape[0]
    H = proj.shape[1] // (4 * HEAD)
    hb = min(HB, H)
    tm = min(tm, T)
    nt = T // tm
    nblk = tm // BLK
    tril, triu = _block_tri(tm)
    sel = (jnp.arange(BLK * SUB)[None, :] // SUB == jnp.arange(BLK)[:, None]).astype(BF16)
    heads = [slice(hh * HEAD, (hh + 1) * HEAD) for hh in range(hb)]

    def body(qp_ref, fz_ref, v_ref, lg_ref, st_ref, do_ref, tril_ref, triu_ref, sel_ref, d3_ref, dlb_ref,
             dS_ref, Sb_ref, q_s, k_s, b_s, dq_s, dk_s, dv_s, db_s, pk_s, pv_s):
        i = pl.program_id(1)

        @pl.when(i == 0)
        def _():
            dS_ref[...] = jnp.zeros_like(dS_ref)
            dlb_ref[...] = jnp.zeros_like(dlb_ref)

        lb = _lb0(lg_ref)
        qp = qp_ref[...]
        sg = _sig(fz_ref[...])
        f = lb + (1.0 - lb) * sg
        q_s[...] = _silu(qp)
        k_s[...] = 1.0 - f
        b_s[...] = _mm_exact(tril_ref[...], jnp.log(f))
        rows = lax.broadcasted_iota(jnp.int32, (SUB, HEAD), 0)
        rows1 = lax.broadcasted_iota(jnp.int32, (SUB, 1), 0)

        S = [st_ref[hh] for hh in range(hb)]
        for nb in range(nblk):
            blk = slice(nb * BLK, (nb + 1) * BLK)
            last = slice(nb * BLK + BLK - 1, nb * BLK + BLK)
            for hh, c in enumerate(heads):
                Sb_ref[nb * hb + hh] = S[hh]
                if nb < nblk - 1:
                    bc = b_s[last, c]
                    kd = k_s[blk, c] * jnp.exp(bc - b_s[blk, c])
                    S[hh] = S[hh] * jnp.exp(bc) + _dot_tn(v_ref[blk, c].astype(BF16), kd.astype(BF16))

        dS = [dS_ref[hh] for hh in range(hb)]
        for nb in reversed(range(nblk)):
            blk = slice(nb * BLK, (nb + 1) * BLK)
            last = slice(nb * BLK + BLK - 1, nb * BLK + BLK)
            qb, kb, bb, dob, dq, dbc, ebc = [], [], [], [], [], [], []
            for hh, c in enumerate(heads):
                S0 = Sb_ref[nb * hb + hh]
                qb.append(q_s[blk, c])
                kb.append(k_s[blk, c])
                bb.append(b_s[blk, c])
                dob.append(do_ref[blk, c])
                bc = b_s[last, c]
                eb = jnp.exp(bb[hh])
                ekd = jnp.exp(bc - bb[hh])
                ebc.append(jnp.exp(bc))
                dS16 = dS[hh].astype(BF16)
                dob16 = dob[hh].astype(BF16)
                dq.append(_dot(dob16, S0.astype(BF16)) * eb)
                dki = _dot(v_ref[blk, c].astype(BF16), dS16) * ekd
                dk_s[blk, c] = dki
                dv_s[blk, c] = _dot_nt((kb[hh] * ekd).astype(BF16), dS16)
                dbc.append(_colsum(dS[hh] * S0) * ebc[hh] + _colsum(kb[hh] * dki))
                dS[hh] = dS[hh] * ebc[hh] + _dot_tn(dob16, (qb[hh] * eb).astype(BF16))
            qh, bh, doh, dqh = [[_halves(t[hh]) for hh in range(hb)] for t in (qb, bb, dob, dq)]
            for s in range(BLK):
                r = slice(nb * BLK + s, nb * BLK + s + 1)
                for hh, c in enumerate(heads):
                    ks = k_s[r, c]
                    pk, pv = None, None
                    for hf, lo in _live_halves(s):
                        diff = bh[hh][hf] - b_s[r, c]
                        dec = jnp.exp(diff if lo == 0 else jnp.where(rows >= lo, diff, NEG))
                        w = qh[hh][hf] * dec
                        a = jnp.sum(w * ks, axis=-1, keepdims=True)
                        da = jnp.sum(doh[hh][hf] * v_ref[r, c], axis=-1, keepdims=True)
                        if lo:
                            da = jnp.where(rows1 >= lo, da, 0.0)
                        dqh[hh][hf] = dqh[hh][hf] + (da * ks) * dec
                        pk = da * w if pk is None else pk + da * w
                        pv = a * doh[hh][hf] if pv is None else pv + a * doh[hh][hf]
                    pk_s[hh, s * SUB:(s + 1) * SUB, :] = pk
                    pv_s[hh, s * SUB:(s + 1) * SUB, :] = pv
            for hh, c in enumerate(heads):
                khi, klo = _split2(pk_s[hh])
                dk_s[blk, c] += _dot(sel_ref[...], khi) + _dot(sel_ref[...], klo)
                dv_s[blk, c] += _dot(sel_ref[...], pv_s[hh].astype(BF16))
                dq[hh] = jnp.concatenate(dqh[hh], axis=0)
                dq_s[blk, c] = dq[hh]
                db_s[blk, c] = qb[hh] * dq[hh] - kb[hh] * dk_s[blk, c]
                db_s[last, c] += dbc[hh]
        for hh in range(hb):
            dS_ref[hh] = dS[hh]

        dlf = _mm_exact(triu_ref[...], db_s[...])
        df = dlf / f - dk_s[...]
        d3_ref[0] = (dq_s[...] * _dsilu(qp)).astype(BF16)
        d3_ref[1] = (df * (1.0 - lb) * sg * (1.0 - sg)).astype(BF16)
        d3_ref[2] = dv_s[...].astype(BF16)
        dlb_ref[...] += _colsum(df * (1.0 - sg))

    rev = lambda i: nt - 1 - i
    out_blk = pl.BlockSpec((tm, hb * HEAD), lambda h, i: (rev(i), h))
    return pl.pallas_call(
        body, grid=(H // hb, nt),
        in_specs=_hgrn_specs(H, hb, tm, rev) + [pl.BlockSpec((None, hb, HEAD, HEAD), lambda h, i: (rev(i), h, 0, 0)),
                                                out_blk, _const_spec((tm, tm)), _const_spec((tm, tm)),
                                                _const_spec((BLK, BLK * SUB))],
        out_specs=[pl.BlockSpec((3, tm, hb * HEAD), lambda h, i: (0, rev(i), h)),
                   pl.BlockSpec((1, hb * HEAD), lambda h, i: (0, h))],
        out_shape=[jax.ShapeDtypeStruct((3, T, H * HEAD), BF16), jax.ShapeDtypeStruct((1, H * HEAD), F32)],
        scratch_shapes=[pltpu.VMEM((hb, HEAD, HEAD), F32), pltpu.VMEM((nblk * hb, HEAD, HEAD), F32)]
        + [pltpu.VMEM((tm, hb * HEAD), F32)] * 7 + [pltpu.VMEM((hb, BLK * SUB, HEAD), F32)] * 2, name=name,
        compiler_params=_cp("parallel", "arbitrary"))(proj, proj, proj, lb_logits, states, do, tril, triu, sel)


def hgrn_gate(o, proj, gn, *, name, tm=512):
    T, D = o.shape
    H = D // HEAD
    tm = min(tm, T)

    def body(o_ref, gp_ref, gn_ref, og_ref):
        gn_ = gn_ref[...]
        for h in range(H):
            c = slice(h * HEAD, (h + 1) * HEAD)
            oh = o_ref[:, c]
            r = lax.rsqrt(_rowmean(oh * oh) + EPS)
            og_ref[:, c] = ((oh * r) * gn_ * _silu(gp_ref[:, c])).astype(BF16)

    return pl.pallas_call(
        body, grid=(T // tm,),
        in_specs=[_row(tm, D), pl.BlockSpec((tm, D), lambda i: (i, 3)), _full(1, HEAD)],
        out_specs=_row(tm, D), out_shape=jax.ShapeDtypeStruct((T, D), BF16), name=name,
        compiler_params=_cp("parallel"))(o, proj, gn)


def hgrn_gate_bwd(dog, o, proj, gn, *, name, tm=512):
    T, D = o.shape
    H = D // HEAD
    tm = min(tm, T)

    def body(dog_ref, o_ref, gp_ref, gn_ref, do_ref, dgp_ref, dgn_ref):
        _acc_init(pl.program_id(0), dgn_ref)
        gn_ = gn_ref[...]
        for h in range(H):
            c = slice(h * HEAD, (h + 1) * HEAD)
            oh = o_ref[:, c]
            gp = gp_ref[:, c]
            dg = dog_ref[:, c]
            r = lax.rsqrt(_rowmean(oh * oh) + EPS)
            on = oh * r
            dgp_ref[:, c] = (dg * (on * gn_) * _dsilu(gp)).astype(BF16)
            don = dg * _silu(gp)
            dgn_ref[...] += _colsum(don * on)
            dn = don * gn_
            do_ref[:, c] = r * (dn - on * _rowmean(dn * on))

    return pl.pallas_call(
        body, grid=(T // tm,),
        in_specs=[_row(tm, D), _row(tm, D), pl.BlockSpec((tm, D), lambda i: (i, 3)), _full(1, HEAD)],
        out_specs=[_row(tm, D), _row(tm, D), _full(1, HEAD)],
        out_shape=[jax.ShapeDtypeStruct((T, D), F32), jax.ShapeDtypeStruct((T, D), BF16),
                   jax.ShapeDtypeStruct((1, HEAD), F32)], name=name,
        compiler_params=_cp("arbitrary"))(dog, o, proj, gn)


def _split2(x):
    hi = x.astype(BF16)
    return hi, (x - hi.astype(F32)).astype(BF16)


def ada_mod(c_all, ada_w, *, name):
    L, D, N = ada_w.shape
    B = c_all.shape[0]

    def body(c_ref, w_ref, o_ref):
        chi, clo = _split2(_silu(c_ref[...]))
        whi, wlo = _split2(w_ref[...])
        o_ref[...] = _dot(chi, whi) + _dot(chi, wlo) + _dot(clo, whi)

    return pl.pallas_call(
        body, grid=(L,), in_specs=[_full(B, D), pl.BlockSpec((None, D, N), lambda l: (l, 0, 0))],
        out_specs=pl.BlockSpec((None, B, N), lambda l: (l, 0, 0)),
        out_shape=jax.ShapeDtypeStruct((L, B, N), F32), name=name, compiler_params=_cp("parallel"))(c_all, ada_w)


def ada_wgrad(c_all_t, dmod, *, name, tr=256):
    D, B = c_all_t.shape
    L, _, N = dmod.shape
    tr = min(tr, D)

    def body(c_ref, d_ref, o_ref):
        cond = _silu(c_ref[...])
        acc = cond[:, 0:1] * d_ref[0:1, :]
        for b in range(1, B):
            acc = acc + cond[:, b:b + 1] * d_ref[b:b + 1, :]
        o_ref[...] = acc

    return pl.pallas_call(
        body, grid=(L, D // tr),
        in_specs=[pl.BlockSpec((tr, B), lambda l, r: (r, 0)), pl.BlockSpec((None, B, N), lambda l, r: (l, 0, 0))],
        out_specs=pl.BlockSpec((None, tr, N), lambda l, r: (l, r, 0)),
        out_shape=jax.ShapeDtypeStruct((L, D, N), F32), name=name,
        compiler_params=_cp("parallel", "parallel"))(c_all_t, dmod)


def sum_devices(parts, *, name):
    n, R, C = parts.shape

    def body(p_ref, o_ref):
        acc = p_ref[0]
        for d in range(1, n):
            acc = acc + p_ref[d]
        o_ref[...] = acc

    return pl.pallas_call(body, in_specs=[VMEM_SPEC], out_specs=VMEM_SPEC,
                          out_shape=jax.ShapeDtypeStruct((R, C), F32), name=name)(parts)


def lb_logits_grad(lb_logits, dlb, *, name):
    def body(lg_ref, d_ref, o_ref):
        l0, l1, l2 = lg_ref[0:1, :], lg_ref[1:2, :], lg_ref[2:3, :]
        m = jnp.maximum(jnp.maximum(l0, l1), l2)
        e0, e1, e2 = jnp.exp(l0 - m), jnp.exp(l1 - m), jnp.exp(l2 - m)
        z = e0 + e1 + e2
        p0, p1, p2 = e0 / z, e1 / z, e2 / z
        g = d_ref[...] * p0
        o_ref[0:1, :] = g * (1.0 - p0)
        o_ref[1:2, :] = -g * p1
        o_ref[2:3, :] = -g * p2

    return pl.pallas_call(body, in_specs=[VMEM_SPEC, VMEM_SPEC], out_specs=VMEM_SPEC,
                          out_shape=jax.ShapeDtypeStruct(lb_logits.shape, F32), name=name)(lb_logits, dlb)


def adamw(w, g, m, v, *, name, tr=256, after=None):
    R, C = w.shape
    tr = _tile(R, tr)
    deps = [] if after is None else [after]

    def body(w_ref, g_ref, m_ref, v_ref, *rest):
        d_ref, nm_ref, nv_ref = rest[len(deps):]
        gv = g_ref[...]
        nm = ADAM_B1 * m_ref[...] + (1.0 - ADAM_B1) * gv
        nv = ADAM_B2 * v_ref[...] + (1.0 - ADAM_B2) * (gv * gv)
        m_hat = nm / (1.0 - ADAM_B1 ** ADAM_STEP)
        v_hat = nv / (1.0 - ADAM_B2 ** ADAM_STEP)
        d_ref[...] = -ADAM_LR * (m_hat / (jnp.sqrt(v_hat) + ADAM_EPS) + ADAM_WD * w_ref[...])
        nm_ref[...] = nm
        nv_ref[...] = nv

    spec = pl.BlockSpec((tr, C), lambda i: (i, 0))
    return pl.pallas_call(
        body, grid=(R // tr,), in_specs=[spec] * 4 + [pl.BlockSpec(memory_space=pl.ANY)] * len(deps), out_specs=[spec] * 3,
        out_shape=[jax.ShapeDtypeStruct((R, C), F32)] * 3, name=name,
        compiler_params=_cp("parallel"))(w, g, m, v, *deps)


def _place():
    return lax.axis_index("x"), lax.axis_index("y"), lax.axis_index("c")


def _flip(v, bit):
    return 1 - v if bit else v


def allgather_devices(v, *, name):
    R, C = v.shape

    def body(v_ref, out_ref, send_sems, recv_sems, local_sem):
        x, y, c = _place()
        me = 4 * x + 2 * y + c
        mine = pltpu.make_async_copy(v_ref, out_ref.at[me], local_sem)
        mine.start()
        sends = []
        for k in range(1, N_DEV):
            peer = (_flip(x, k & 4), _flip(y, k & 2), _flip(c, k & 1))
            cp = pltpu.make_async_remote_copy(src_ref=v_ref, dst_ref=out_ref.at[me], send_sem=send_sems.at[k - 1],
                                              recv_sem=recv_sems.at[k - 1], device_id=peer, device_id_type=MESH)
            cp.start()
            sends.append(cp)
        for k in range(1, N_DEV):
            px, py, pc = _flip(x, k & 4), _flip(y, k & 2), _flip(c, k & 1)
            pltpu.make_async_remote_copy(src_ref=v_ref, dst_ref=out_ref.at[4 * px + 2 * py + pc],
                                         send_sem=send_sems.at[k - 1], recv_sem=recv_sems.at[k - 1],
                                         device_id=(px, py, pc), device_id_type=MESH).wait_recv()
        for cp in sends:
            cp.wait_send()
        mine.wait()

    return pl.pallas_call(
        body, in_specs=[VMEM_SPEC], out_specs=VMEM_SPEC, out_shape=jax.ShapeDtypeStruct((N_DEV, R, C), v.dtype),
        scratch_shapes=[pltpu.SemaphoreType.DMA((N_DEV - 1,)), pltpu.SemaphoreType.DMA((N_DEV - 1,)),
                        pltpu.SemaphoreType.DMA], name=name)(v)


def _other_chips(x, y):
    return [(1 - x, y), (x, 1 - y), (1 - x, 1 - y)]


SEM = pl.BlockSpec(memory_space=pltpu.SEMAPHORE)
DATAFLOW = pltpu.SideEffectType.DATAFLOW_SIDE_EFFECTING


def _chip_copy(buf, a, j, q, c, chips, send_sems, recv_sems):
    px, py = chips[j]
    return pltpu.make_async_remote_copy(src_ref=buf.at[q, c], dst_ref=buf.at[q, c], send_sem=send_sems.at[3 * a + j],
                                        recv_sem=recv_sems.at[3 * a + j], device_id=(px, py, c), device_id_type=MESH)


def allgather_chips_start(bufs, *, name):
    n = len(bufs)

    def body(*refs):
        send_sems, recv_sems = refs[n], refs[n + 1]
        outs = refs[n + 2:2 * n + 2]
        token = refs[2 * n + 2]
        x, y, c = _place()
        chips = _other_chips(x, y)
        for a in range(n):
            for j in range(3):
                _chip_copy(outs[a], a, j, 2 * x + y, c, chips, send_sems, recv_sems).start()
        token[...] = jnp.zeros_like(token)

    res = pl.pallas_call(
        body, name=name, in_specs=[HBM] * n,
        out_specs=(SEM, SEM, *([HBM] * n), VMEM_SPEC),
        out_shape=(pltpu.SemaphoreType.DMA((3 * n,)), pltpu.SemaphoreType.DMA((3 * n,)),
                   *[pltpu.HBM(b.shape, b.dtype) for b in bufs], jax.ShapeDtypeStruct((SUB, LANE), F32)),
        input_output_aliases={a: a + 2 for a in range(n)},
        compiler_params=pltpu.CompilerParams(has_side_effects=DATAFLOW),
    )(*[pltpu.with_memory_space_constraint(b, pltpu.HBM) for b in bufs])
    return res[0], res[1], list(res[2:2 + n]), res[2 + n]


def allgather_chips_wait(send_sems, recv_sems, bufs, after, *, name):
    n = len(bufs)

    def body(*refs):
        ins = refs[:n]
        send_sems, recv_sems = refs[n], refs[n + 1]
        x, y, c = _place()
        chips = _other_chips(x, y)
        for a in range(n):
            for j, (px, py) in enumerate(chips):
                _chip_copy(ins[a], a, j, 2 * x + y, c, chips, send_sems, recv_sems).wait_send()
                _chip_copy(ins[a], a, j, 2 * px + py, c, chips, send_sems, recv_sems).wait_recv()

    return list(pl.pallas_call(
        body, name=name, in_specs=[HBM] * n + [SEM, SEM, pl.BlockSpec(memory_space=pl.ANY)],
        out_specs=[HBM] * n, out_shape=[pltpu.HBM(b.shape, b.dtype) for b in bufs],
        input_output_aliases={a: a for a in range(n)},
        compiler_params=pltpu.CompilerParams(has_side_effects=DATAFLOW),
    )(*bufs, send_sems, recv_sems, after))


def forward_to_sibling(bufs, *, name):
    n = len(bufs)

    def body(*refs):
        outs = refs[n:2 * n]
        send_sems, recv_sems = refs[2 * n:]
        x, y, c = _place()
        chips = _other_chips(x, y)

        def copy(a, j, half, to):
            px, py = chips[j]
            slab = outs[a].at[2 * px + py, half]
            return pltpu.make_async_remote_copy(src_ref=slab, dst_ref=slab, send_sem=send_sems.at[a, j],
                                                recv_sem=recv_sems.at[a, j], device_id=to, device_id_type=MESH)

        sends = [copy(a, j, c, (x, y, 1 - c)) for a in range(n) for j in range(3)]
        for cp in sends:
            cp.start()
        for a in range(n):
            for j in range(3):
                copy(a, j, 1 - c, (x, y, c)).wait_recv()
        for cp in sends:
            cp.wait_send()

    return pl.pallas_call(
        body, in_specs=[HBM] * n, out_specs=[HBM] * n,
        out_shape=[jax.ShapeDtypeStruct(b.shape, b.dtype) for b in bufs],
        input_output_aliases={a: a for a in range(n)},
        scratch_shapes=[pltpu.SemaphoreType.DMA((n, 3)), pltpu.SemaphoreType.DMA((n, 3))], name=name)(*bufs)


def pair_add(g, other, c_idx, *, name, tr=256):
    _, Q, R, C = g.shape
    tr = _tile(R, tr)

    def body(c_ref, g_ref, o_ref, out_ref):
        out_ref[...] = (g_ref[...] + o_ref[...]).astype(BF16)

    return pl.pallas_call(
        body,
        grid_spec=pltpu.PrefetchScalarGridSpec(
            num_scalar_prefetch=1, grid=(Q, R // tr),
            in_specs=[pl.BlockSpec((None, None, tr, C), lambda q, r, c_ref: (c_ref[0], q, r, 0)),
                      pl.BlockSpec((None, tr, C), lambda q, r, c_ref: (q, r, 0))],
            out_specs=pl.BlockSpec((None, tr, C), lambda q, r, c_ref: (q, r, 0))),
        out_shape=jax.ShapeDtypeStruct((Q, R, C), BF16), name=name,
        compiler_params=_cp("parallel", "parallel"))(c_idx, g, other)


def chip_sum(sums, landed, qc_idx, *, name, tr=256):
    _, R, C = sums.shape
    tr = _tile(R, tr)

    def body(qc_ref, own_ref, l_ref, o_ref):
        acc = own_ref[...].astype(F32)
        for k in range(3):
            acc = acc + l_ref[k].astype(F32)
        o_ref[...] = acc

    return pl.pallas_call(
        body,
        grid_spec=pltpu.PrefetchScalarGridSpec(
            num_scalar_prefetch=1, grid=(R // tr,),
            in_specs=[pl.BlockSpec((None, tr, C), lambda r, qc: (qc[0], r, 0)),
                      pl.BlockSpec((3, tr, C), lambda r, qc: (0, r, 0))],
            out_specs=pl.BlockSpec((None, tr, C), lambda r, qc: (qc[1], r, 0))),
        out_shape=jax.ShapeDtypeStruct((2, R, C), F32), name=name,
        compiler_params=_cp("parallel"))(qc_idx, sums, landed)


def half_swap(bufs, *, name):
    n = len(bufs)

    def body(*refs):
        outs = refs[n:2 * n]
        send_sems, recv_sems = refs[2 * n:]
        x, y, c = _place()
        cps = [pltpu.make_async_remote_copy(src_ref=outs[a].at[c], dst_ref=outs[a].at[c], send_sem=send_sems.at[a],
                                            recv_sem=recv_sems.at[a], device_id=(x, y, 1 - c), device_id_type=MESH)
               for a in range(n)]
        for cp in cps:
            cp.start()
        for a in range(n):
            pltpu.make_async_remote_copy(src_ref=outs[a].at[c], dst_ref=outs[a].at[1 - c], send_sem=send_sems.at[a],
                                         recv_sem=recv_sems.at[a], device_id=(x, y, 1 - c),
                                         device_id_type=MESH).wait_recv()
        for cp in cps:
            cp.wait_send()

    return pl.pallas_call(
        body, in_specs=[HBM] * n, out_specs=[HBM] * n,
        out_shape=[jax.ShapeDtypeStruct(b.shape, b.dtype) for b in bufs],
        input_output_aliases={a: a for a in range(n)},
        scratch_shapes=[pltpu.SemaphoreType.DMA((n,)), pltpu.SemaphoreType.DMA((n,))], name=name)(*bufs)


def _chip_copies(src, dst, send_sems, recv_sems):
    x, y, c = _place()
    return [pltpu.make_async_remote_copy(src_ref=src[a].at[2 * px + py], dst_ref=dst[a].at[j],
                                         send_sem=send_sems.at[3 * a + j], recv_sem=recv_sems.at[3 * a + j],
                                         device_id=(px, py, c), device_id_type=MESH)
            for a in range(len(src)) for j, (px, py) in enumerate(_other_chips(x, y))]


def _pair_copies(src, dst, send_sems, recv_sems):
    x, y, c = _place()
    return [pltpu.make_async_remote_copy(src_ref=src[a].at[1 - c], dst_ref=dst[a], send_sem=send_sems.at[a],
                                         recv_sem=recv_sems.at[a], device_id=(x, y, 1 - c), device_id_type=MESH)
            for a in range(len(src))]


def exchange_start(src, landing, copies, n_sems, *, name):
    n = len(src)

    def body(*refs):
        send_sems, recv_sems = refs[2 * n], refs[2 * n + 1]
        for cp in copies(refs[2 * n + 2:3 * n + 2], refs[3 * n + 2:4 * n + 2], send_sems, recv_sems):
            cp.start()
        token = refs[4 * n + 2]
        token[...] = jnp.zeros_like(token)

    res = pl.pallas_call(
        body, name=name, in_specs=[HBM] * (2 * n),
        out_specs=(SEM, SEM, *([HBM] * (2 * n)), VMEM_SPEC),
        out_shape=(pltpu.SemaphoreType.DMA((n_sems,)), pltpu.SemaphoreType.DMA((n_sems,)),
                   *[pltpu.HBM(b.shape, b.dtype) for b in src + landing], jax.ShapeDtypeStruct((SUB, LANE), F32)),
        input_output_aliases={a: a + 2 for a in range(2 * n)},
        compiler_params=pltpu.CompilerParams(has_side_effects=DATAFLOW),
    )(*[pltpu.with_memory_space_constraint(b, pltpu.HBM) for b in src + landing])
    return res[0], res[1], list(res[2:2 + n]), list(res[2 + n:2 + 2 * n]), res[2 + 2 * n]


def exchange_wait(send_sems, recv_sems, src, landed, copies, after, *, name):
    n = len(src)

    def body(*refs):
        for cp in copies(refs[:n], refs[n:2 * n], refs[2 * n], refs[2 * n + 1]):
            cp.wait_send()
            cp.wait_recv()

    res = pl.pallas_call(
        body, name=name, in_specs=[HBM] * (2 * n) + [SEM, SEM, pl.BlockSpec(memory_space=pl.ANY)],
        out_specs=[HBM] * (2 * n), out_shape=[pltpu.HBM(b.shape, b.dtype) for b in src + landed],
        input_output_aliases={a: a for a in range(2 * n)},
        compiler_params=pltpu.CompilerParams(has_side_effects=DATAFLOW),
    )(*src, *landed, send_sems, recv_sems, after)
    return list(res[:n]), list(res[n:])


def finish_reduce(sums, landed, q, c, tag):
    qc_idx = jnp.stack([q, c]).astype(jnp.int32)
    return [chip_sum(s, l, qc_idx, name=f"grad_chip_sum_{tag}{a}") for a, (s, l) in enumerate(zip(sums, landed))]


def _ffn_forward(x, mod, pre_g, post_g, w_up, w_down, dw_w, dw_b, tag, tgt=None):
    sh, sc, gate = mod
    h = prenorm(x, pre_g, sc, sh, name=f"{tag}_prenorm")
    u0 = mm_nn(h, w_up, name=f"{tag}_up", out_dtype=BF16, perm=_ffn_perm)
    z, ab = ffn_act(u0, dw_w, dw_b, name=f"{tag}_act")
    y = mm_nn(z, w_down, name=f"{tag}_down")
    if tgt is None:
        out = post_residual(x, y, post_g, gate, name=f"{tag}_post")
    else:
        out = post_residual_loss(x, y, post_g, gate, tgt, name=f"{tag}_post_loss")
    return out, (x, h, u0, ab, z, y)


def _ffn_backward(dx, saved, mod, pre_g, post_g, w_up, w_down, dw_w, dw_b, tag):
    x, h, u0, ab, z, y = saved
    sh, sc, gate = mod
    dy, dgate, dpost, _ = post_bwd(dx, y, post_g, gate, name=f"{tag}_post_bwd")
    dz = mm_nt(dy, w_down, name=f"{tag}_down_dx", out_dtype=BF16)
    g_down = mm_tn(z, dy, name=f"{tag}_down_dw", J=2, block="a", row_chips=2)
    du0, dconv = ffn_act_bwd(dz, u0, ab, dw_w, name=f"{tag}_act_bwd")
    dh = mm_nt(du0, w_up, name=f"{tag}_up_dx", perm=_ffn_perm)
    g_up = mm_tn(h, du0, name=f"{tag}_up_dw", J=4, block="b", perm=_ffn_perm)
    dx_in, dsh, dsc, dpre = prenorm_bwd(dh, x, dx, pre_g, sc, name=f"{tag}_prenorm_bwd")
    nb = u0.shape[1] // 4
    dconv = dconv[:, 0].reshape(4, 2, 2, nb).transpose(0, 2, 1, 3).reshape(4, 4 * nb)
    return dx_in, dict(dsh=dsh, dsc=dsc, dgate=dgate, dpre=dpre, dpost=dpost, g_up=g_up, g_down=g_down,
                       d_dw_w=dconv[0:FFN_W], d_dw_b=dconv[3:4])


def _local_step(x, tgt, mods, P, late_weights=None, grads_ready=None):
    m0, m1 = mods
    h1 = prenorm(x, P["pre_mix_g"][0:1], m0[1], m0[0], name="hgrn_prenorm")
    proj = mm_nn(h1, P["hgrn_w_in"], name="hgrn_in")
    o, states = hgrn_scan(proj, P["hgrn_lb_logits"], name="hgrn_scan")
    og = hgrn_gate(o, proj, P["hgrn_gnorm_g"], name="hgrn_gate")
    y1 = mm_nn(og, P["hgrn_w_out"], name="hgrn_out")
    x1 = post_residual(x, y1, P["post_mix_g"][0:1], m0[2], name="hgrn_post")
    if late_weights is not None:
        P = {**P, **late_weights(x1)}
    x2, ffn0 = _ffn_forward(x1, m0[3:6], P["pre_ffn_g"][0:1], P["post_ffn_g"][0:1], P["ffn_w_up"][0],
                            P["ffn_w_down"][0], P["ffn_dw_w"][0], P["ffn_dw_b"][0:1], "ffn0")
    h3 = prenorm(x2, P["pre_mix_g"][1:2], m1[1], m1[0], name="conv_prenorm")
    u = mm_nn(h3, P["conv_w_in"], name="conv_in", bias=P["conv_b_in"])
    s, cv = conv_act(u, P["conv_dw_w"], P["conv_dw_b"], P["conv_ln_g"], P["conv_ln_b"], name="conv_act")
    y3 = mm_nn(s, P["conv_w_out"], name="conv_out", bias=P["conv_b_out"])
    x3 = post_residual(x2, y3, P["post_mix_g"][1:2], m1[2], name="conv_post")
    (dx4, lcols), ffn1 = _ffn_forward(x3, m1[3:6], P["pre_ffn_g"][1:2], P["post_ffn_g"][1:2], P["ffn_w_up"][1],
                                      P["ffn_w_down"][1], P["ffn_dw_w"][1], P["ffn_dw_b"][1:2], "ffn1", tgt)
    dx3, f1 = _ffn_backward(dx4, ffn1, m1[3:6], P["pre_ffn_g"][1:2], P["post_ffn_g"][1:2], P["ffn_w_up"][1],
                            P["ffn_w_down"][1], P["ffn_dw_w"][1], P["ffn_dw_b"][1:2], "ffn1")
    dy3, dg1_1, dpostmix1, d_b_out = post_bwd(dx3, y3, P["post_mix_g"][1:2], m1[2], name="conv_post_bwd")
    ds = mm_nt(dy3, P["conv_w_out"], name="conv_out_dx")
    g_conv_out = mm_tn(s, dy3, name="conv_out_dw", J=1, block="a", row_chips=4)
    dcv, d_ln_g, d_ln_b, d_dw_b = conv_norm_bwd(ds, cv, P["conv_ln_g"], P["conv_ln_b"], name="conv_norm_bwd")
    du, d_dw_w, d_b_in = conv_glu_bwd(dcv, u, P["conv_dw_w"], name="conv_glu_bwd")
    dh3 = mm_nt(du, P["conv_w_in"], name="conv_in_dx")
    g_conv_in = mm_tn(h3, du, name="conv_in_dw", J=2, block="b", col_chips=2)
    dx2, dsh1_1, dsc1_1, dpremix1 = prenorm_bwd(dh3, x2, dx3, P["pre_mix_g"][1:2], m1[1], name="conv_prenorm_bwd")
    if grads_ready is not None:
        token = grads_ready("l1", [g_conv_in, g_conv_out, f1["g_up"], f1["g_down"]])
        m0 = tuple(m + token[0:1, 0:1] for m in m0)
    dx1, f0 = _ffn_backward(dx2, ffn0, m0[3:6], P["pre_ffn_g"][0:1], P["post_ffn_g"][0:1], P["ffn_w_up"][0],
                            P["ffn_w_down"][0], P["ffn_dw_w"][0], P["ffn_dw_b"][0:1], "ffn0")
    if grads_ready is not None:
        token = grads_ready("f0", [f0["g_up"], f0["g_down"]])
        m0 = tuple(m + token[0:1, 0:1] for m in m0)
    dy1, dg1_0, dpostmix0, _ = post_bwd(dx1, y1, P["post_mix_g"][0:1], m0[2], name="hgrn_post_bwd")
    dog = mm_nt(dy1, P["hgrn_w_out"], name="hgrn_out_dx")
    g_hgrn_out = mm_tn(og, dy1, name="hgrn_out_dw", J=1, block="a", row_chips=4)
    do, dgp, d_gn = hgrn_gate_bwd(dog, o, proj, P["hgrn_gnorm_g"], name="hgrn_gate_bwd")
    d3, dlb = hgrn_scan_bwd(proj, P["hgrn_lb_logits"], states, do, name="hgrn_scan_bwd")
    g_hgrn_in = mm_tn_parts(h1, d3, dgp, name="hgrn_in_dw")
    token = grads_ready("hg", [g_hgrn_in, g_hgrn_out]) if grads_ready is not None else None
    dh1 = mm_nt_parts(d3, dgp, P["hgrn_w_in"], name="hgrn_in_dx", after=token)
    dx0, dsh1_0, dsc1_0, dpremix0 = prenorm_bwd(dh1, x, dx1, P["pre_mix_g"][0:1], m0[1], name="hgrn_prenorm_bwd")

    dmod = jnp.stack([
        jnp.concatenate([dsh1_0, dsc1_0, dg1_0, f0["dsh"], f0["dsc"], f0["dgate"]], axis=1)[0],
        jnp.concatenate([dsh1_1, dsc1_1, dg1_1, f1["dsh"], f1["dsc"], f1["dgate"]], axis=1)[0]])
    small = dict(
        loss=lcols,
        pre_mix_g=jnp.concatenate([dpremix0, dpremix1]), post_mix_g=jnp.concatenate([dpostmix0, dpostmix1]),
        pre_ffn_g=jnp.concatenate([f0["dpre"], f1["dpre"]]), post_ffn_g=jnp.concatenate([f0["dpost"], f1["dpost"]]),
        lb=dlb, hgrn_gnorm_g=d_gn, ffn_dw_b=jnp.concatenate([f0["d_dw_b"], f1["d_dw_b"]]), dmod=dmod,
        conv_b_in=d_b_in, conv_dw_w=d_dw_w[0:CONV_W], conv_dw_b=d_dw_b, conv_ln_g=d_ln_g, conv_ln_b=d_ln_b,
        conv_b_out=d_b_out, ffn_dw_w=jnp.stack([f0["d_dw_w"], f1["d_dw_w"]]))
    big = [g_hgrn_in, g_hgrn_out, g_conv_in, g_conv_out, f0["g_up"], f1["g_up"], f0["g_down"], f1["g_down"]]
    return dx0, small, big


def _pack(parts, rows=8):
    flat = jnp.concatenate([p.reshape(-1).astype(F32) for p in parts])
    per = rows * 128
    pad = (-flat.shape[0]) % per
    return jnp.pad(flat, (0, pad)).reshape(rows, -1)


def _unpack(flat, shapes):
    out, off = [], 0
    for s in shapes:
        n = 1
        for d in s:
            n *= d
        out.append(flat[..., off:off + n].reshape(flat.shape[:-1] + tuple(s)))
        off += n
    return out


def _from_chips(stacked, axis):
    moved = jnp.moveaxis(stacked, 0, axis)
    shape = list(moved.shape)
    return moved.reshape(shape[:axis] + [shape[axis] * shape[axis + 1]] + shape[axis + 2:])


def _my_shard(full, axis, q):
    n = full.shape[axis] // N_CHIPS
    return lax.dynamic_slice_in_dim(full, q * n, n, axis=axis)


def kernel(x, c, ada_w, ada_b, pre_mix_g, post_mix_g, pre_ffn_g, post_ffn_g, hgrn_w_in, hgrn_lb_logits, hgrn_gnorm_g, hgrn_w_out, conv_w_in, conv_b_in, conv_dw_w, conv_dw_b, conv_ln_g, conv_ln_b, conv_w_out, conv_b_out, ffn_w_up, ffn_dw_w, ffn_dw_b, ffn_w_down, loss_target, m_ada_w, m_ada_b, m_pre_mix_g, m_post_mix_g, m_pre_ffn_g, m_post_ffn_g, m_hgrn_w_in, m_hgrn_lb_logits, m_hgrn_gnorm_g, m_hgrn_w_out, m_conv_w_in, m_conv_b_in, m_conv_dw_w, m_conv_dw_b, m_conv_ln_g, m_conv_ln_b, m_conv_w_out, m_conv_b_out, m_ffn_w_up, m_ffn_dw_w, m_ffn_dw_b, m_ffn_w_down, v_ada_w, v_ada_b, v_pre_mix_g, v_post_mix_g, v_pre_ffn_g, v_post_ffn_g, v_hgrn_w_in, v_hgrn_lb_logits, v_hgrn_gnorm_g, v_hgrn_w_out, v_conv_w_in, v_conv_b_in, v_conv_dw_w, v_conv_dw_b, v_conv_ln_g, v_conv_ln_b, v_conv_w_out, v_conv_b_out, v_ffn_w_up, v_ffn_dw_w, v_ffn_dw_b, v_ffn_w_down):
    W = dict(ada_w=ada_w, ada_b=ada_b, pre_mix_g=pre_mix_g, post_mix_g=post_mix_g, pre_ffn_g=pre_ffn_g,
             post_ffn_g=post_ffn_g, hgrn_w_in=hgrn_w_in, hgrn_lb_logits=hgrn_lb_logits, hgrn_gnorm_g=hgrn_gnorm_g,
             hgrn_w_out=hgrn_w_out, conv_w_in=conv_w_in, conv_b_in=conv_b_in, conv_dw_w=conv_dw_w,
             conv_dw_b=conv_dw_b, conv_ln_g=conv_ln_g, conv_ln_b=conv_ln_b, conv_w_out=conv_w_out,
             conv_b_out=conv_b_out, ffn_w_up=ffn_w_up, ffn_dw_w=ffn_dw_w, ffn_dw_b=ffn_dw_b, ffn_w_down=ffn_w_down)
    M = dict(ada_w=m_ada_w, ada_b=m_ada_b, pre_mix_g=m_pre_mix_g, post_mix_g=m_post_mix_g, pre_ffn_g=m_pre_ffn_g,
             post_ffn_g=m_post_ffn_g, hgrn_w_in=m_hgrn_w_in, hgrn_lb_logits=m_hgrn_lb_logits,
             hgrn_gnorm_g=m_hgrn_gnorm_g, hgrn_w_out=m_hgrn_w_out, conv_w_in=m_conv_w_in, conv_b_in=m_conv_b_in,
             conv_dw_w=m_conv_dw_w, conv_dw_b=m_conv_dw_b, conv_ln_g=m_conv_ln_g, conv_ln_b=m_conv_ln_b,
             conv_w_out=m_conv_w_out, conv_b_out=m_conv_b_out, ffn_w_up=m_ffn_w_up, ffn_dw_w=m_ffn_dw_w,
             ffn_dw_b=m_ffn_dw_b, ffn_w_down=m_ffn_w_down)
    V = dict(ada_w=v_ada_w, ada_b=v_ada_b, pre_mix_g=v_pre_mix_g, post_mix_g=v_post_mix_g, pre_ffn_g=v_pre_ffn_g,
             post_ffn_g=v_post_ffn_g, hgrn_w_in=v_hgrn_w_in, hgrn_lb_logits=v_hgrn_lb_logits,
             hgrn_gnorm_g=v_hgrn_gnorm_g, hgrn_w_out=v_hgrn_w_out, conv_w_in=v_conv_w_in, conv_b_in=v_conv_b_in,
             conv_dw_w=v_conv_dw_w, conv_dw_b=v_conv_dw_b, conv_ln_g=v_conv_ln_g, conv_ln_b=v_conv_ln_b,
             conv_w_out=v_conv_w_out, conv_b_out=v_conv_b_out, ffn_w_up=v_ffn_w_up, ffn_dw_w=v_ffn_dw_w,
             ffn_dw_b=v_ffn_dw_b, ffn_w_down=v_ffn_w_down)
    names = list(W)
    xi, yi, ci = lax.axis_index("x"), lax.axis_index("y"), lax.axis_index("c")
    q = 2 * xi + yi
    me = 2 * q + ci
    D = x.shape[-1]
    L = ada_w.shape[0]

    small_w = ["conv_b_in", "conv_dw_w", "conv_dw_b", "conv_ln_g", "conv_ln_b", "conv_b_out", "ffn_dw_w"]
    small_axis = dict(conv_b_in=1, conv_dw_w=2, conv_dw_b=1, conv_ln_g=1, conv_ln_b=1, conv_b_out=1, ffn_dw_w=2)
    packed = _pack([c] + [W[n] for n in small_w])

    def halves(w):
        shard = w.astype(BF16).reshape(1, 2, w.shape[0] // 2, w.shape[1])
        buf = lax.empty((N_CHIPS,) + shard.shape[1:], BF16)
        return lax.dynamic_update_slice_in_dim(buf, shard, q, axis=0)

    hg_send, hg_recv, hg_bufs, hg_token = allgather_chips_start([halves(hgrn_w_in[0]), halves(hgrn_w_out[0])],
                                                                name="gather_hgrn_weights_start")
    packed, _ = lax.optimization_barrier((packed, hg_token))
    gathered = allgather_devices(packed, name="gather_small_params").reshape(N_DEV, -1)
    c_all = gathered[:, 0:D]
    per_chip = gathered.reshape(N_CHIPS, 2, -1)[:, 0, D:]
    parts = _unpack(per_chip, [W[n].shape for n in small_w])
    P = {n: _from_chips(p, small_axis[n]) for n, p in zip(small_w, parts)}
    P["conv_dw_w"] = P["conv_dw_w"][0]
    for n in ("pre_mix_g", "post_mix_g", "pre_ffn_g", "post_ffn_g", "hgrn_lb_logits", "hgrn_gnorm_g", "ffn_dw_b"):
        P[n] = W[n]

    modp = ada_mod(c_all, ada_w, name="ada_mod")
    ncol = modp.shape[-1]
    mod_all = allgather_devices(modp.reshape(L * N_DEV, ncol), name="gather_mod")
    mod_all = mod_all.reshape(N_CHIPS, 2, L, N_DEV, ncol)[:, 0]
    mod_me = lax.dynamic_index_in_dim(mod_all, me, axis=2, keepdims=False)
    mod = mod_me.transpose(1, 0, 2).reshape(L, N_CHIPS * ncol) + ada_b
    mods = [tuple(mod[l:l + 1, k * D:(k + 1) * D] for k in range(6)) for l in range(L)]

    stack = lambda t: t.reshape(N_CHIPS, t.shape[1] * t.shape[2], t.shape[3])
    rowsh = lambda t: t.reshape(1, N_CHIPS * t.shape[1] * t.shape[2], t.shape[3])
    pairs = lambda t: t.reshape(2, 2, t.shape[1], t.shape[2]).transpose(0, 2, 1, 3).reshape(2, t.shape[1], 2 * t.shape[2])
    g = forward_to_sibling(allgather_chips_wait(hg_send, hg_recv, hg_bufs, mod, name="gather_hgrn_weights_wait"),
                           name="gather_hgrn_weights_forward")
    P["hgrn_w_in"], P["hgrn_w_out"] = stack(g[0]), rowsh(g[1])
    late_shards = [conv_w_in[0], conv_w_out[0], ffn_w_up[0], ffn_w_up[1], ffn_w_down[0], ffn_w_down[1]]
    late_bufs, _, _ = lax.optimization_barrier(([halves(w) for w in late_shards], g, mod))
    send_sems, recv_sems, bufs, token = allgather_chips_start(late_bufs, name="gather_weights_start")
    mods[0] = tuple(m + token[0:1, 0:1] for m in mods[0])

    def late_weights(x1):
        landed = allgather_chips_wait(send_sems, recv_sems, bufs, x1, name="gather_weights_wait")
        g = forward_to_sibling(landed, name="gather_weights_forward")
        return dict(conv_w_in=pairs(stack(g[0])), conv_w_out=rowsh(g[1]), ffn_w_up=[stack(g[2]), stack(g[3])],
                    ffn_w_down=[rowsh(g[4]), rowsh(g[5])])

    c_idx = ci.astype(jnp.int32).reshape(1)
    pending, in_flight = {}, {}

    def chip_stage(after):
        tag, (send, recv, grads, landing) = pending.popitem()
        grads, others = exchange_wait(send, recv, grads, landing, _pair_copies, after, name=f"grad_pair_wait_{tag}")
        sums = [pair_add(g_, o_, c_idx, name=f"grad_pair_add_{tag}_{a}") for a, (g_, o_) in enumerate(zip(grads, others))]
        landing = [lax.empty((3,) + s_.shape[1:], s_.dtype) for s_ in sums]
        send, recv, sums, landing, tok = exchange_start(sums, landing, _chip_copies, 3 * len(sums),
                                                        name=f"grad_chip_exchange_start_{tag}")
        in_flight[tag] = (send, recv, sums, landing)
        return tok

    def grads_ready(tag, grads):
        tok = chip_stage(grads[0]) if pending else 0.0
        landing = [lax.empty(g_.shape[1:], g_.dtype) for g_ in grads]
        send, recv, grads, landing, tok2 = exchange_start(grads, landing, _pair_copies, len(grads),
                                                          name=f"grad_pair_start_{tag}")
        pending[tag] = (send, recv, grads, landing)
        return tok + tok2

    grad_x, small, big = _local_step(x[0], loss_target[0], mods, P, late_weights, grads_ready)

    small_names = list(small)
    gs = allgather_devices(_pack([small[n] for n in small_names]), name="gather_small_grads")
    dmod_all = _unpack(gs.reshape(N_DEV, -1), [small[n].shape for n in small_names])[small_names.index("dmod")]
    tot = sum_devices(gs, name="sum_small_grads").reshape(1, -1)
    S = dict(zip(small_names, _unpack(tot, [small[n].shape for n in small_names])))
    S = {n: v[0] for n, v in S.items()}
    loss = 0.5 * jnp.sum(S["loss"]) / D

    G = {}
    dmod_q = lax.dynamic_slice_in_dim(dmod_all, q * ncol, ncol, axis=2)
    G["ada_w"] = ada_wgrad(c_all.T, dmod_q.transpose(1, 0, 2), name="ada_wgrad")
    G["ada_b"] = S["dmod"]
    for n in ("pre_mix_g", "post_mix_g", "pre_ffn_g", "post_ffn_g", "hgrn_gnorm_g", "ffn_dw_b"):
        G[n] = S[n]
    G["hgrn_lb_logits"] = lb_logits_grad(hgrn_lb_logits, S["lb"], name="lb_logits_grad")
    G["conv_b_in"] = _my_shard(S["conv_b_in"], 1, q)
    G["conv_dw_w"] = _my_shard(S["conv_dw_w"], 1, q)[None]
    for n in ("conv_dw_b", "conv_ln_g", "conv_ln_b", "conv_b_out"):
        G[n] = _my_shard(S[n], 1, q)
    G["ffn_dw_w"] = _my_shard(S["ffn_dw_w"], 2, q)

    tok_hg = chip_stage(grad_x)
    halves = []
    for tag in ("f0", "l1"):
        sums_t, landed_t = exchange_wait(*in_flight[tag], _chip_copies, grad_x, name=f"grad_chip_exchange_wait_{tag}")
        halves += finish_reduce(sums_t, landed_t, q, ci, f"{tag}_")
    red = [f.reshape(2 * f.shape[1], f.shape[2]) for f in half_swap(halves, name="grad_half_swap")]
    G["conv_w_in"], G["conv_w_out"] = red[2][None], red[3][None]
    G["ffn_w_up"] = jnp.stack([red[0], red[4]])
    G["ffn_w_down"] = jnp.stack([red[1], red[5]])

    delta, new_m, new_v = {}, {}, {}

    def adamw_matrix(n, after=None):
        shp = W[n].shape
        two = lambda t: t.reshape(-1, shp[-1])
        d_, m_, v_ = adamw(two(W[n]), two(G[n]), two(M[n]), two(V[n]), name=f"adamw_{n}", after=after)
        delta[n], new_m[n], new_v[n] = d_.reshape(shp), m_.reshape(shp), v_.reshape(shp)

    big_names = ["ada_w", "hgrn_w_in", "hgrn_w_out", "conv_w_in", "conv_w_out", "ffn_w_up", "ffn_w_down"]
    for n in ("ada_w", "conv_w_in", "conv_w_out", "ffn_w_up", "ffn_w_down"):
        adamw_matrix(n, after=tok_hg)
    sums_h, landed_h = exchange_wait(*in_flight["hg"], _chip_copies, delta["ffn_w_down"], name="grad_chip_exchange_wait_hg")
    red_h = half_swap(finish_reduce(sums_h, landed_h, q, ci, "hg_"), name="grad_half_swap_hg")
    G["hgrn_w_in"], G["hgrn_w_out"] = [f.reshape(1, 2 * f.shape[1], f.shape[2]) for f in red_h]
    for n in ("hgrn_w_in", "hgrn_w_out"):
        adamw_matrix(n)
    rest = [n for n in names if n not in big_names]
    d_, m_, v_ = adamw(_pack([W[n] for n in rest]), _pack([G[n] for n in rest]), _pack([M[n] for n in rest]),
                       _pack([V[n] for n in rest]), name="adamw_small")
    shapes = [W[n].shape for n in rest]
    for n, a, b_, c_ in zip(rest, _unpack(d_.reshape(-1), shapes), _unpack(m_.reshape(-1), shapes),
                            _unpack(v_.reshape(-1), shapes)):
        delta[n], new_m[n], new_v[n] = a, b_, c_

    return (loss, grad_x[None], *[G[n].reshape(W[n].shape) for n in names], *[delta[n] for n in names],
            *[new_m[n] for n in names], *[new_v[n] for n in names])
```

```python
import jax
import jax.numpy as jnp
from jax import lax
from jax.experimental import pallas as pl
from jax.experimental.pallas import tpu as pltpu

F32 = jnp.float32
BF16 = jnp.bfloat16
EPS = 1e-6
HEAD = 128
BLK = 16
NEG = -1e30
CONV_W = 31
FFN_W = 3
N_CHIPS = 4
N_DEV = 8
SUB = 8
LANE = 128
V7X_VMEM_LIMIT = 56 * 1024 * 1024
MESH = pl.DeviceIdType.MESH
HBM = pl.BlockSpec(memory_space=pltpu.HBM)
VMEM_SPEC = pl.BlockSpec(memory_space=pltpu.VMEM)

ADAM_LR = 0.001
ADAM_B1 = 0.9
ADAM_B2 = 0.999
ADAM_EPS = 1e-08
ADAM_WD = 0.01
ADAM_STEP = 10


def _cp(*sem):
    return pltpu.CompilerParams(dimension_semantics=sem, vmem_limit_bytes=V7X_VMEM_LIMIT)


def _sig(x):
    return 0.5 * jnp.tanh(0.5 * x) + 0.5


def _silu(x):
    return x * _sig(x)


def _dsilu(x):
    s = _sig(x)
    return s * (1.0 + x * (1.0 - s))


def _dot(a, b):
    return jnp.dot(a, b, preferred_element_type=F32)


def _dot_nt(a, b):
    return lax.dot_general(a, b, (((1,), (1,)), ((), ())), preferred_element_type=F32)


def _dot_tn(a, b):
    return lax.dot_general(a, b, (((0,), (0,)), ((), ())), preferred_element_type=F32)


def _colsum(x):
    return jnp.sum(x, axis=0, keepdims=True)


def _rowmean(x):
    return jnp.mean(x, axis=-1, keepdims=True)


def _ffn_perm(j):
    return (j % 2) * 2 + j // 2


def _tile(n, pref):
    if n <= pref:
        return n
    t = pref - pref % 8
    while n % t:
        t -= 8
    return t


def mm_nn(a, w, *, name, bias=None, out_dtype=F32, perm=None, tm=1024):
    T, K = a.shape
    J, _, nb = w.shape
    tm = min(tm, T)
    col = (lambda j: j) if perm is None else perm

    def body(a_ref, w_ref, *rest):
        acc = _dot(a_ref[...], w_ref[...])
        if bias is not None:
            acc = acc + rest[0][...]
        rest[-1][...] = acc.astype(out_dtype)

    in_specs = [pl.BlockSpec((tm, K), lambda j, i: (i, 0)), pl.BlockSpec((None, K, nb), lambda j, i: (j, 0, 0))]
    args = [a, w]
    if bias is not None:
        in_specs.append(pl.BlockSpec((1, nb), lambda j, i: (0, j)))
        args.append(bias)
    return pl.pallas_call(
        body, grid=(J, T // tm), in_specs=in_specs,
        out_specs=pl.BlockSpec((tm, nb), lambda j, i: (i, col(j))),
        out_shape=jax.ShapeDtypeStruct((T, J * nb), out_dtype), name=name,
        compiler_params=_cp("parallel", "parallel"))(*args)


def mm_nt(a, w, *, name, out_dtype=F32, perm=None, tm=1024, after=None):
    T = a.shape[0]
    J, K, nb = w.shape
    tm = min(tm, T)
    col = (lambda j: j) if perm is None else perm
    deps = [] if after is None else [after]

    def body(a_ref, w_ref, *rest):
        o_ref, acc_ref = rest[len(deps):]
        j = pl.program_id(1)

        @pl.when(j == 0)
        def _():
            acc_ref[...] = jnp.zeros_like(acc_ref)

        acc_ref[...] += _dot_nt(a_ref[...], w_ref[...])

        @pl.when(j == J - 1)
        def _():
            o_ref[...] = acc_ref[...].astype(out_dtype)

    return pl.pallas_call(
        body, grid=(T // tm, J),
        in_specs=[pl.BlockSpec((tm, nb), lambda i, j: (i, col(j))), pl.BlockSpec((None, K, nb), lambda i, j: (j, 0, 0))]
        + [pl.BlockSpec(memory_space=pl.ANY)] * len(deps),
        out_specs=pl.BlockSpec((tm, K), lambda i, j: (i, 0)),
        out_shape=jax.ShapeDtypeStruct((T, K), out_dtype),
        scratch_shapes=[pltpu.VMEM((tm, K), F32)], name=name,
        compiler_params=_cp("parallel", "arbitrary"))(a, w, *deps)


def mm_tn(a, b, *, name, J, block, row_chips=1, col_chips=1, perm=None, tk=1024):
    T = a.shape[0]
    tk = min(tk, T)
    col = (lambda j: j) if perm is None else perm
    if block == "b":
        rows, nb = a.shape[1], b.shape[1] // J
        a_spec = pl.BlockSpec((tk, rows), lambda j, t: (t, 0))
        b_spec = pl.BlockSpec((tk, nb), lambda j, t: (t, col(j)))
    else:
        rows, nb = a.shape[1] // J, b.shape[1]
        a_spec = pl.BlockSpec((tk, rows), lambda j, t: (t, col(j)))
        b_spec = pl.BlockSpec((tk, nb), lambda j, t: (t, 0))
    rh = rows // (2 * row_chips)
    nc = nb // col_chips
    chips = [(rc, cc) for rc in range(row_chips) for cc in range(col_chips)]

    def body(a_ref, b_ref, o_ref):
        @pl.when(pl.program_id(1) == 0)
        def _():
            o_ref[...] = jnp.zeros_like(o_ref)

        acc = _dot_tn(a_ref[...], b_ref[...])
        for ch, (rc, cc) in enumerate(chips):
            for hf in range(2):
                r0 = (rc * 2 + hf) * rh
                o_ref[hf, ch] += acc[r0:r0 + rh, cc * nc:(cc + 1) * nc]

    return pl.pallas_call(
        body, grid=(J, T // tk), in_specs=[a_spec, b_spec],
        out_specs=pl.BlockSpec((2, len(chips), rh, nc), lambda j, t: (0, j, 0, 0)),
        out_shape=jax.ShapeDtypeStruct((2, J * len(chips), rh, nc), F32), name=name,
        compiler_params=_cp("parallel", "arbitrary"))(a, b)


def mm_nt_parts(s3, g, w, *, name, tm=1024, after=None):
    n3, T, nb = s3.shape
    J, K, _ = w.shape
    tm = min(tm, T)
    deps = [] if after is None else [after]

    def body(s_ref, g_ref, w_ref, *rest):
        o_ref, acc_ref = rest[len(deps):]
        j = pl.program_id(1)

        @pl.when(j == 0)
        def _():
            acc_ref[...] = jnp.zeros_like(acc_ref)

        @pl.when(j < n3)
        def _():
            acc_ref[...] += _dot_nt(s_ref[...], w_ref[...])

        @pl.when(j == n3)
        def _():
            acc_ref[...] += _dot_nt(g_ref[...], w_ref[...])

        @pl.when(j == J - 1)
        def _():
            o_ref[...] = acc_ref[...]

    return pl.pallas_call(
        body, grid=(T // tm, J),
        in_specs=[pl.BlockSpec((None, tm, nb), lambda i, j: (jnp.minimum(j, n3 - 1), i, 0)),
                  pl.BlockSpec((tm, nb), lambda i, j: (i, 0)), pl.BlockSpec((None, K, nb), lambda i, j: (j, 0, 0))]
        + [pl.BlockSpec(memory_space=pl.ANY)] * len(deps),
        out_specs=pl.BlockSpec((tm, K), lambda i, j: (i, 0)), out_shape=jax.ShapeDtypeStruct((T, K), F32),
        scratch_shapes=[pltpu.VMEM((tm, K), F32)], name=name,
        compiler_params=_cp("parallel", "arbitrary"))(s3, g, w, *deps)


def mm_tn_parts(a, s3, g, *, name, tk=1024):
    n3, T, nb = s3.shape
    J = n3 + 1
    tk = min(tk, T)
    rows = a.shape[1]
    rh = rows // 2

    def body(a_ref, s_ref, g_ref, o_ref):
        j = pl.program_id(0)

        @pl.when(pl.program_id(1) == 0)
        def _():
            o_ref[...] = jnp.zeros_like(o_ref)

        def add(b_ref):
            acc = _dot_tn(a_ref[...], b_ref[...])
            for hf in range(2):
                o_ref[hf, 0] += acc[hf * rh:(hf + 1) * rh, :]

        pl.when(j < n3)(lambda: add(s_ref))
        pl.when(j == n3)(lambda: add(g_ref))

    return pl.pallas_call(
        body, grid=(J, T // tk),
        in_specs=[pl.BlockSpec((tk, rows), lambda j, t: (t, 0)),
                  pl.BlockSpec((None, tk, nb), lambda j, t: (jnp.minimum(j, n3 - 1), t, 0)),
                  pl.BlockSpec((tk, nb), lambda j, t: (t, 0))],
        out_specs=pl.BlockSpec((2, 1, rh, nb), lambda j, t: (0, j, 0, 0)),
        out_shape=jax.ShapeDtypeStruct((2, J, rh, nb), F32), name=name,
        compiler_params=_cp("parallel", "arbitrary"))(a, s3, g)


def _row(tm, w):
    return pl.BlockSpec((tm, w), lambda i: (i, 0))


def _full(r, w):
    return pl.BlockSpec((r, w), lambda i: (0, 0))


def _acc_init(i, *refs):
    @pl.when(i == 0)
    def _():
        for r in refs:
            r[...] = jnp.zeros_like(r)


def prenorm(x, g, sc, sh, *, name, tm=512):
    T, D = x.shape
    tm = min(tm, T)

    def body(x_ref, g_ref, sc_ref, sh_ref, h_ref):
        xv = x_ref[...]
        r = lax.rsqrt(_rowmean(xv * xv) + EPS)
        h_ref[...] = ((xv * r) * g_ref[...] * (1.0 + sc_ref[...]) + sh_ref[...]).astype(BF16)

    return pl.pallas_call(
        body, grid=(T // tm,), in_specs=[_row(tm, D), _full(1, D), _full(1, D), _full(1, D)],
        out_specs=_row(tm, D), out_shape=jax.ShapeDtypeStruct((T, D), BF16), name=name,
        compiler_params=_cp("parallel"))(x, g, sc, sh)


def post_residual(x, y, g, gate, *, name, tm=512):
    T, D = x.shape
    tm = min(tm, T)

    def body(x_ref, y_ref, g_ref, gate_ref, o_ref):
        yv = y_ref[...]
        r = lax.rsqrt(_rowmean(yv * yv) + EPS)
        o_ref[...] = x_ref[...] + gate_ref[...] * ((yv * r) * g_ref[...])

    return pl.pallas_call(
        body, grid=(T // tm,), in_specs=[_row(tm, D), _row(tm, D), _full(1, D), _full(1, D)],
        out_specs=_row(tm, D), out_shape=jax.ShapeDtypeStruct((T, D), F32), name=name,
        compiler_params=_cp("parallel"))(x, y, g, gate)


def post_residual_loss(x, y, g, gate, tgt, *, name, tm=512):
    T, D = x.shape
    tm = min(tm, T)

    def body(x_ref, y_ref, g_ref, gate_ref, t_ref, dx_ref, l_ref):
        _acc_init(pl.program_id(0), l_ref)
        yv = y_ref[...]
        r = lax.rsqrt(_rowmean(yv * yv) + EPS)
        e = x_ref[...] + gate_ref[...] * ((yv * r) * g_ref[...]) - t_ref[...]
        dx_ref[...] = e * (1.0 / D)
        l_ref[...] += _colsum(e * e)

    return pl.pallas_call(
        body, grid=(T // tm,), in_specs=[_row(tm, D), _row(tm, D), _full(1, D), _full(1, D), _row(tm, D)],
        out_specs=[_row(tm, D), _full(1, D)],
        out_shape=[jax.ShapeDtypeStruct((T, D), F32), jax.ShapeDtypeStruct((1, D), F32)], name=name,
        compiler_params=_cp("arbitrary"))(x, y, g, gate, tgt)


def post_bwd(dx, y, g, gate, *, name, tm=512):
    T, D = dx.shape
    tm = min(tm, T)

    def body(dx_ref, y_ref, g_ref, gate_ref, dy_ref, dgate_ref, dg_ref, dbias_ref):
        _acc_init(pl.program_id(0), dgate_ref, dg_ref, dbias_ref)
        yv = y_ref[...]
        dxv = dx_ref[...]
        r = lax.rsqrt(_rowmean(yv * yv) + EPS)
        yn = yv * r
        gv = g_ref[...]
        gt = gate_ref[...]
        dgate_ref[...] += _colsum(dxv * (yn * gv))
        dg_ref[...] += _colsum(dxv * gt * yn)
        dyn = dxv * gt * gv
        dy = r * (dyn - yn * _rowmean(dyn * yn))
        dbias_ref[...] += _colsum(dy)
        dy_ref[...] = dy.astype(BF16)

    return pl.pallas_call(
        body, grid=(T // tm,), in_specs=[_row(tm, D), _row(tm, D), _full(1, D), _full(1, D)],
        out_specs=[_row(tm, D), _full(1, D), _full(1, D), _full(1, D)],
        out_shape=[jax.ShapeDtypeStruct((T, D), BF16)] + [jax.ShapeDtypeStruct((1, D), F32)] * 3, name=name,
        compiler_params=_cp("arbitrary"))(dx, y, g, gate)


def prenorm_bwd(dh, x, dres, g, sc, *, name, tm=512):
    T, D = x.shape
    tm = min(tm, T)

    def body(dh_ref, x_ref, dres_ref, g_ref, sc_ref, dx_ref, dsh_ref, dsc_ref, dg_ref):
        _acc_init(pl.program_id(0), dsh_ref, dsc_ref, dg_ref)
        xv = x_ref[...]
        dhv = dh_ref[...]
        r = lax.rsqrt(_rowmean(xv * xv) + EPS)
        xn = xv * r
        gv = g_ref[...]
        one_sc = 1.0 + sc_ref[...]
        dsh_ref[...] += _colsum(dhv)
        dsc_ref[...] += _colsum(dhv * (xn * gv))
        dg_ref[...] += _colsum(dhv * one_sc * xn)
        dxn = dhv * one_sc * gv
        dx_ref[...] = dres_ref[...] + r * (dxn - xn * _rowmean(dxn * xn))

    return pl.pallas_call(
        body, grid=(T // tm,), in_specs=[_row(tm, D), _row(tm, D), _row(tm, D), _full(1, D), _full(1, D)],
        out_specs=[_row(tm, D), _full(1, D), _full(1, D), _full(1, D)],
        out_shape=[jax.ShapeDtypeStruct((T, D), F32)] + [jax.ShapeDtypeStruct((1, D), F32)] * 3, name=name,
        compiler_params=_cp("arbitrary"))(dh, x, dres, g, sc)


HALO = 16


def _shift_helpers():
    rid = lax.broadcasted_iota(jnp.int32, (SUB, LANE), 0)

    def down(cur, prev, k):
        return pltpu.roll(jnp.where(rid >= SUB - k, prev, cur), k, 0)

    def up(cur, nxt, k):
        return pltpu.roll(jnp.where(rid < k, nxt, cur), SUB - k, 0)

    return down, up


def _ffn_sides(c, nb, wa_ref, wb_ref, ba_ref, bb_ref):
    cols = slice(c * LANE, (c + 1) * LANE)
    return [(cols, [wa_ref[k:k + 1, cols] for k in range(FFN_W)], ba_ref[:, cols]),
            (slice(nb + c * LANE, nb + (c + 1) * LANE), [wb_ref[k:k + 1, cols] for k in range(FFN_W)],
             bb_ref[:, cols])]


def _ffn_specs(tm, nb, hb, idx):
    return [pl.BlockSpec((tm, 2 * nb), lambda jc, i: (idx(i), jc)),
            pl.BlockSpec((HALO, 2 * nb), lambda jc, i: (jnp.maximum(idx(i) * hb - 1, 0), jc)),
            pl.BlockSpec((FFN_W, nb), lambda jc, i: (0, jc)),
            pl.BlockSpec((FFN_W, nb), lambda jc, i: (0, jc + 2)),
            pl.BlockSpec((1, nb), lambda jc, i: (0, jc)),
            pl.BlockSpec((1, nb), lambda jc, i: (0, jc + 2))]


def ffn_act(u0p, dw_w, dw_b, *, name, tm=256):
    T, W = u0p.shape
    nb = W // 4
    tm = min(tm, T)
    unroll = 4
    rows16 = 2 * SUB

    def body(u_ref, halo_ref, wa_ref, wb_ref, ba_ref, bb_ref, z_ref, ab_ref):
        i = pl.program_id(1)
        down, _ = _shift_helpers()
        for c in range(nb // LANE):
            cols = slice(c * LANE, (c + 1) * LANE)
            side = _ffn_sides(c, nb, wa_ref, wb_ref, ba_ref, bb_ref)

            def rows(j, prev):
                prev = list(prev)
                for m in range(unroll):
                    r0 = pl.multiple_of((j * unroll + m) * rows16, rows16)
                    x = [u_ref[pl.ds(r0, rows16), cs].astype(F32) for cs, _, _ in side]
                    conv = [[None, None], [None, None]]
                    for hf in range(2):
                        for n, (_, w, b) in enumerate(side):
                            cur = x[n][hf * SUB:(hf + 1) * SUB, :]
                            conv[n][hf] = b + w[2] * cur + w[1] * down(cur, prev[n], 1) + w[0] * down(cur, prev[n], 2)
                            prev[n] = cur
                    a, b = [jnp.concatenate(conv[n], axis=0) for n in range(2)]
                    z_ref[pl.ds(r0, rows16), cols] = (_silu(a) * b).astype(BF16)
                    ab_ref[pl.ds(r0, rows16), side[0][0]] = a.astype(BF16)
                    ab_ref[pl.ds(r0, rows16), side[1][0]] = b.astype(BF16)
                return tuple(prev)

            first = [jnp.where(i == 0, 0.0, halo_ref[:, cs].astype(F32)[SUB:2 * SUB, :]) for cs, _, _ in side]
            lax.fori_loop(0, tm // (rows16 * unroll), rows, tuple(first))

    return pl.pallas_call(
        body, grid=(2, T // tm), in_specs=_ffn_specs(tm, nb, tm // HALO, lambda i: i),
        out_specs=[pl.BlockSpec((tm, nb), lambda jc, i: (i, jc)), pl.BlockSpec((tm, 2 * nb), lambda jc, i: (i, jc))],
        out_shape=[jax.ShapeDtypeStruct((T, 2 * nb), BF16), jax.ShapeDtypeStruct((T, W), BF16)], name=name,
        compiler_params=_cp("parallel", "arbitrary"))(u0p, u0p, dw_w, dw_w, dw_b, dw_b)


def ffn_act_bwd(dz, u0p, ab, dw_w, *, name, tm=256):
    T, W = u0p.shape
    nb = W // 4
    tm = min(tm, T)
    nt = T // tm
    unroll = 4
    rows16 = 2 * SUB
    n_it = tm // (rows16 * unroll)

    def body(dz_ref, u_ref, ab_ref, wa_ref, wb_ref, du0_ref, dw_ref, carry):
        i = pl.program_id(1)
        _acc_init(i, dw_ref)
        _, up = _shift_helpers()
        for c in range(nb // LANE):
            cols = slice(c * LANE, (c + 1) * LANE)
            side = [(cols, [wa_ref[k:k + 1, cols] for k in range(FFN_W)]),
                    (slice(nb + c * LANE, nb + (c + 1) * LANE), [wb_ref[k:k + 1, cols] for k in range(FFN_W)])]

            def rows(j, st):
                nxt, acc = list(st[0:2]), list(st[2:10])
                for m in range(unroll):
                    r0 = pl.multiple_of(((n_it - 1 - j) * unroll + unroll - 1 - m) * rows16, rows16)
                    dzv = dz_ref[pl.ds(r0, rows16), cols].astype(F32)
                    a, b = [ab_ref[pl.ds(r0, rows16), cs].astype(F32) for cs, _ in side]
                    x = [u_ref[pl.ds(r0, rows16), cs].astype(F32) for cs, _ in side]
                    sa = _sig(a)
                    d16 = [dzv * b * (sa * (1.0 + a * (1.0 - sa))), dzv * (a * sa)]
                    out = [[None, None], [None, None]]
                    for hf in (1, 0):
                        half = slice(hf * SUB, (hf + 1) * SUB)
                        for n in range(2):
                            w = side[n][1]
                            d = d16[n][half, :]
                            u = x[n][half, :]
                            up1, up2 = up(d, nxt[n], 1), up(d, nxt[n], 2)
                            acc[4 * n + 0] = acc[4 * n + 0] + up2 * u
                            acc[4 * n + 1] = acc[4 * n + 1] + up1 * u
                            acc[4 * n + 2] = acc[4 * n + 2] + d * u
                            acc[4 * n + 3] = acc[4 * n + 3] + d
                            out[n][hf] = w[2] * d + w[1] * up1 + w[0] * up2
                            nxt[n] = d
                    for n in range(2):
                        du0_ref[pl.ds(r0, rows16), side[n][0]] = jnp.concatenate(out[n], axis=0).astype(BF16)
                return (*nxt, *acc)

            init = [jnp.where(i == 0, 0.0, carry[:, cs]) for cs, _ in side] + [jnp.zeros((SUB, LANE), F32)] * 8
            st = lax.fori_loop(0, n_it, rows, tuple(init))
            for n in range(2):
                carry[:, side[n][0]] = st[n]
                for k in range(4):
                    dw_ref[k, :, side[n][0]] += st[2 + 4 * n + k]

        @pl.when(i == nt - 1)
        def _():
            for k in range(4):
                dw_ref[k, 0:1, :] = _colsum(dw_ref[k])

    rev = lambda i: nt - 1 - i
    wide = pl.BlockSpec((tm, 2 * nb), lambda jc, i: (rev(i), jc))
    return pl.pallas_call(
        body, grid=(2, nt),
        in_specs=[pl.BlockSpec((tm, nb), lambda jc, i: (rev(i), jc)), wide, wide,
                  pl.BlockSpec((FFN_W, nb), lambda jc, i: (0, jc)), pl.BlockSpec((FFN_W, nb), lambda jc, i: (0, jc + 2))],
        out_specs=[wide, pl.BlockSpec((4, SUB, 2 * nb), lambda jc, i: (0, 0, jc))],
        out_shape=[jax.ShapeDtypeStruct((T, W), BF16), jax.ShapeDtypeStruct((4, SUB, W), F32)],
        scratch_shapes=[pltpu.VMEM((SUB, 2 * nb), F32)], name=name,
        compiler_params=_cp("parallel", "arbitrary"))(dz, u0p, ab, dw_w, dw_w)


CHALO = 32
CCOL = 256


def _phase_copies(buf, shifted, tm):
    n = tm + CHALO - SUB
    for p in range(1, SUB):
        shifted[p - 1, 0:n, :] = buf[p:p + n, :]


def _shifted(buf, shifted, r, tm, c0):
    m, p = divmod(r, SUB)
    src = buf if p == 0 else shifted.at[p - 1]
    return src[m * SUB:m * SUB + tm, c0:c0 + CCOL]


def conv_act(u, dw_w, dw_b, ln_g, ln_b, *, name, tm=128):
    T, D2 = u.shape
    D = D2 // 2
    tm = min(tm, T)
    hb = tm // CHALO

    def body(u_ref, halo_ref, w_ref, b_ref, g_ref, be_ref, s_ref, cv_ref, gbuf, gsh):
        i = pl.program_id(0)
        hv = halo_ref[...]
        gbuf[0:CHALO, :] = jnp.where(i == 0, 0.0, hv[:, 0:D] * _sig(hv[:, D:D2]))
        uv = u_ref[...]
        gbuf[CHALO:CHALO + tm, :] = uv[:, 0:D] * _sig(uv[:, D:D2])
        _phase_copies(gbuf, gsh, tm)
        for c0 in range(0, D, CCOL):
            acc = jnp.zeros((tm, CCOL), F32) + b_ref[:, c0:c0 + CCOL]
            for k in range(CONV_W):
                acc = acc + w_ref[k:k + 1, c0:c0 + CCOL] * _shifted(gbuf, gsh, CHALO - (CONV_W - 1) + k, tm, c0)
            cv_ref[:, c0:c0 + CCOL] = acc
        cv = cv_ref[...]
        mu = _rowmean(cv)
        xc = cv - mu
        nh = xc * lax.rsqrt(_rowmean(xc * xc) + EPS)
        s_ref[...] = _silu(nh * g_ref[...] + be_ref[...]).astype(BF16)

    return pl.pallas_call(
        body, grid=(T // tm,),
        in_specs=[_row(tm, D2), pl.BlockSpec((CHALO, D2), lambda i: (jnp.maximum(i * hb - 1, 0), 0)),
                  _full(CONV_W, D), _full(1, D), _full(1, D), _full(1, D)],
        out_specs=[_row(tm, D), _row(tm, D)],
        out_shape=[jax.ShapeDtypeStruct((T, D), BF16), jax.ShapeDtypeStruct((T, D), F32)],
        scratch_shapes=[pltpu.VMEM((tm + CHALO, D), F32), pltpu.VMEM((SUB - 1, tm + CHALO, D), F32)], name=name,
        compiler_params=_cp("arbitrary"))(u, u, dw_w, dw_b, ln_g, ln_b)


def conv_norm_bwd(ds, cv, ln_g, ln_b, *, name, tm=512):
    T, D = cv.shape
    tm = min(tm, T)

    def body(ds_ref, cv_ref, g_ref, be_ref, dcv_ref, dg_ref, dbe_ref, dcb_ref):
        _acc_init(pl.program_id(0), dg_ref, dbe_ref, dcb_ref)
        cv_ = cv_ref[...]
        mu = _rowmean(cv_)
        xc = cv_ - mu
        rstd = lax.rsqrt(_rowmean(xc * xc) + EPS)
        nh = xc * rstd
        gv = g_ref[...]
        dln = ds_ref[...] * _dsilu(nh * gv + be_ref[...])
        dg_ref[...] += _colsum(dln * nh)
        dbe_ref[...] += _colsum(dln)
        dnh = dln * gv
        dcv = rstd * (dnh - _rowmean(dnh) - nh * _rowmean(dnh * nh))
        dcb_ref[...] += _colsum(dcv)
        dcv_ref[...] = dcv

    return pl.pallas_call(
        body, grid=(T // tm,), in_specs=[_row(tm, D), _row(tm, D), _full(1, D), _full(1, D)],
        out_specs=[_row(tm, D), _full(1, D), _full(1, D), _full(1, D)],
        out_shape=[jax.ShapeDtypeStruct((T, D), F32)] + [jax.ShapeDtypeStruct((1, D), F32)] * 3, name=name,
        compiler_params=_cp("arbitrary"))(ds, cv, ln_g, ln_b)


def conv_glu_bwd(dcv, u, dw_w, *, name, tm=128):
    T, D2 = u.shape
    D = D2 // 2
    tm = min(tm, T)
    nt = T // tm
    hb = tm // CHALO

    def body(dcv_ref, dnext_ref, u_ref, w_ref, du_ref, dw_ref, dbin_ref, dbuf, dsh):
        i = pl.program_id(0)
        _acc_init(i, dw_ref, dbin_ref)
        uv = u_ref[...]
        av = uv[:, 0:D]
        sg = _sig(uv[:, D:D2])
        glu = av * sg
        dbuf[0:tm, :] = dcv_ref[...]
        dbuf[tm:tm + CHALO, :] = jnp.where(i == nt - 1, 0.0, dnext_ref[...])
        _phase_copies(dbuf, dsh, tm)
        for c0 in range(0, D, CCOL):
            glu_c = glu[:, c0:c0 + CCOL]
            acc = jnp.zeros((tm, CCOL), F32)
            for k in range(CONV_W):
                moved = _shifted(dbuf, dsh, CONV_W - 1 - k, tm, c0)
                dw_ref[k:k + 1, c0:c0 + CCOL] += _colsum(moved * glu_c)
                acc = acc + w_ref[k:k + 1, c0:c0 + CCOL] * moved
            a_c = av[:, c0:c0 + CCOL]
            s_c = sg[:, c0:c0 + CCOL]
            da = acc * s_c
            dgt = acc * a_c * s_c * (1.0 - s_c)
            dbin_ref[:, c0:c0 + CCOL] += _colsum(da)
            dbin_ref[:, D + c0:D + c0 + CCOL] += _colsum(dgt)
            du_ref[:, c0:c0 + CCOL] = da.astype(BF16)
            du_ref[:, D + c0:D + c0 + CCOL] = dgt.astype(BF16)

    return pl.pallas_call(
        body, grid=(nt,),
        in_specs=[_row(tm, D), pl.BlockSpec((CHALO, D), lambda i: (jnp.minimum((i + 1) * hb, T // CHALO - 1), 0)),
                  _row(tm, D2), _full(CONV_W, D)],
        out_specs=[_row(tm, D2), _full(CHALO, D), _full(1, D2)],
        out_shape=[jax.ShapeDtypeStruct((T, D2), BF16), jax.ShapeDtypeStruct((CHALO, D), F32),
                   jax.ShapeDtypeStruct((1, D2), F32)],
        scratch_shapes=[pltpu.VMEM((tm + CHALO, D), F32), pltpu.VMEM((SUB - 1, tm + CHALO, D), F32)],
        name=name, compiler_params=_cp("arbitrary"))(dcv, dcv, u, dw_w)


HB = 8


def _lb0(lg_ref):
    l0, l1, l2 = lg_ref[0:1, :], lg_ref[1:2, :], lg_ref[2:3, :]
    m = jnp.maximum(jnp.maximum(l0, l1), l2)
    e0 = jnp.exp(l0 - m)
    return e0 / (e0 + jnp.exp(l1 - m) + jnp.exp(l2 - m))


def _mm_exact(m01, x):
    hi = x.astype(BF16)
    r1 = x - hi.astype(F32)
    mid = r1.astype(BF16)
    lo = (r1 - mid.astype(F32)).astype(BF16)
    return _dot(m01, hi) + _dot(m01, mid) + _dot(m01, lo)


def _block_tri(tm):
    r = jnp.arange(tm)[:, None]
    c = jnp.arange(tm)[None, :]
    same = (r // BLK) == (c // BLK)
    return (same & (c <= r)).astype(BF16), (same & (c >= r)).astype(BF16)


def _halves(x):
    return [x[0:SUB, :], x[SUB:BLK, :]]


def _live_halves(s):
    return ([(0, s)] if s < SUB else []) + [(1, max(s - SUB, 0))]


def _const_spec(shape):
    return pl.BlockSpec(shape, lambda h, i: (0, 0))


def _hgrn_specs(H, hb, tm, idx):
    g = H // hb
    return [pl.BlockSpec((tm, hb * HEAD), lambda h, i: (idx(i), h)),
            pl.BlockSpec((tm, hb * HEAD), lambda h, i: (idx(i), g + h)),
            pl.BlockSpec((tm, hb * HEAD), lambda h, i: (idx(i), 2 * g + h)),
            pl.BlockSpec((3, hb * HEAD), lambda h, i: (0, h))]


def hgrn_scan(proj, lb_logits, *, name, tm=128):
    T = proj.shape[0]
    H = proj.shape[1] // (4 * HEAD)
    hb = min(HB, H)
    tm = min(tm, T)
    nt = T // tm
    nblk = tm // BLK
    tril, _ = _block_tri(tm)
    heads = [slice(hh * HEAD, (hh + 1) * HEAD) for hh in range(hb)]

    def body(qp_ref, fz_ref, v_ref, lg_ref, tril_ref, o_ref, st_ref, S_ref, q_s, k_s, b_s):
        @pl.when(pl.program_id(1) == 0)
        def _():
            S_ref[...] = jnp.zeros_like(S_ref)

        st_ref[...] = S_ref[...]
        lb = _lb0(lg_ref)
        f = lb + (1.0 - lb) * _sig(fz_ref[...])
        q_s[...] = _silu(qp_ref[...])
        k_s[...] = 1.0 - f
        b_s[...] = _mm_exact(tril_ref[...], jnp.log(f))
        rows = lax.broadcasted_iota(jnp.int32, (BLK, HEAD), 0)
        S = [S_ref[hh] for hh in range(hb)]
        for nb in range(nblk):
            blk = slice(nb * BLK, (nb + 1) * BLK)
            last = slice(nb * BLK + BLK - 1, nb * BLK + BLK)
            qb = [q_s[blk, c] for c in heads]
            bb = [b_s[blk, c] for c in heads]
            o = [_dot_nt((qb[hh] * jnp.exp(bb[hh])).astype(BF16), S[hh].astype(BF16)) for hh in range(hb)]
            for hh, c in enumerate(heads):
                bc = b_s[last, c]
                kd = k_s[blk, c] * jnp.exp(bc - bb[hh])
                S[hh] = S[hh] * jnp.exp(bc) + _dot_tn(v_ref[blk, c].astype(BF16), kd.astype(BF16))
            for s in range(BLK):
                r = slice(nb * BLK + s, nb * BLK + s + 1)
                for hh, c in enumerate(heads):
                    dec = jnp.exp(jnp.where(rows >= s, bb[hh] - b_s[r, c], NEG))
                    a = jnp.sum(qb[hh] * k_s[r, c] * dec, axis=-1, keepdims=True)
                    o[hh] = o[hh] + a * v_ref[r, c]
            for hh, c in enumerate(heads):
                o_ref[blk, c] = o[hh]
        for hh in range(hb):
            S_ref[hh] = S[hh]

    return pl.pallas_call(
        body, grid=(H // hb, nt),
        in_specs=_hgrn_specs(H, hb, tm, lambda i: i) + [_const_spec((tm, tm))],
        out_specs=[pl.BlockSpec((tm, hb * HEAD), lambda h, i: (i, h)),
                   pl.BlockSpec((None, hb, HEAD, HEAD), lambda h, i: (i, h, 0, 0))],
        out_shape=[jax.ShapeDtypeStruct((T, H * HEAD), F32), jax.ShapeDtypeStruct((nt, H, HEAD, HEAD), F32)],
        scratch_shapes=[pltpu.VMEM((hb, HEAD, HEAD), F32)] + [pltpu.VMEM((tm, hb * HEAD), F32)] * 3, name=name,
        compiler_params=_cp("parallel", "arbitrary"))(proj, proj, proj, lb_logits, tril)


def hgrn_scan_bwd(proj, lb_logits, states, do, *, name, tm=128):
    T = proj.shape[0]
    H = proj.shape[1] // (4 * HEAD)
    hb = min(HB, H)
    tm = min(tm, T)
    nt = T // tm
    nblk = tm // BLK
    tril, triu = _block_tri(tm)
    sel = (jnp.arange(BLK * SUB)[None, :] // SUB == jnp.arange(BLK)[:, None]).astype(BF16)
    heads = [slice(hh * HEAD, (hh + 1) * HEAD) for hh in range(hb)]

    def body(qp_ref, fz_ref, v_ref, lg_ref, st_ref, do_ref, tril_ref, triu_ref, sel_ref, d3_ref, dlb_ref,
             dS_ref, Sb_ref, q_s, k_s, b_s, dq_s, dk_s, dv_s, db_s, pk_s, pv_s):
        i = pl.program_id(1)

        @pl.when(i == 0)
        def _():
            dS_ref[...] = jnp.zeros_like(dS_ref)
            dlb_ref[...] = jnp.zeros_like(dlb_ref)

        lb = _lb0(lg_ref)
        qp = qp_ref[...]
        sg = _sig(fz_ref[...])
        f = lb + (1.0 - lb) * sg
        q_s[...] = _silu(qp)
        k_s[...] = 1.0 - f
        b_s[...] = _mm_exact(tril_ref[...], jnp.log(f))
        rows = lax.broadcasted_iota(jnp.int32, (SUB, HEAD), 0)
        rows1 = lax.broadcasted_iota(jnp.int32, (SUB, 1), 0)

        S = [st_ref[hh] for hh in range(hb)]
        for nb in range(nblk):
            blk = slice(nb * BLK, (nb + 1) * BLK)
            last = slice(nb * BLK + BLK - 1, nb * BLK + BLK)
            for hh, c in enumerate(heads):
                Sb_ref[nb * hb + hh] = S[hh]
                if nb < nblk - 1:
                    bc = b_s[last, c]
                    kd = k_s[blk, c] * jnp.exp(bc - b_s[blk, c])
                    S[hh] = S[hh] * jnp.exp(bc) + _dot_tn(v_ref[blk, c].astype(BF16), kd.astype(BF16))

        dS = [dS_ref[hh] for hh in range(hb)]
        for nb in reversed(range(nblk)):
            blk = slice(nb * BLK, (nb + 1) * BLK)
            last = slice(nb * BLK + BLK - 1, nb * BLK + BLK)
            qb, kb, bb, dob, dq, dbc, ebc = [], [], [], [], [], [], []
            for hh, c in enumerate(heads):
                S0 = Sb_ref[nb * hb + hh]
                qb.append(q_s[blk, c])
                kb.append(k_s[blk, c])
                bb.append(b_s[blk, c])
                dob.append(do_ref[blk, c])
                bc = b_s[last, c]
                eb = jnp.exp(bb[hh])
                ekd = jnp.exp(bc - bb[hh])
                ebc.append(jnp.exp(bc))
                dS16 = dS[hh].astype(BF16)
                dob16 = dob[hh].astype(BF16)
                dq.append(_dot(dob16, S0.astype(BF16)) * eb)
                dki = _dot(v_ref[blk, c].astype(BF16), dS16) * ekd
                dk_s[blk, c] = dki
                dv_s[blk, c] = _dot_nt((kb[hh] * ekd).astype(BF16), dS16)
                dbc.append(_colsum(dS[hh] * S0) * ebc[hh] + _colsum(kb[hh] * dki))
                dS[hh] = dS[hh] * ebc[hh] + _dot_tn(dob16, (qb[hh] * eb).astype(BF16))
            qh, bh, doh, dqh = [[_halves(t[hh]) for hh in range(hb)] for t in (qb, bb, dob, dq)]
            for s in range(BLK):
                r = slice(nb * BLK + s, nb * BLK + s + 1)
                for hh, c in enumerate(heads):
                    ks = k_s[r, c]
                    pk, pv = None, None
                    for hf, lo in _live_halves(s):
                        diff = bh[hh][hf] - b_s[r, c]
                        dec = jnp.exp(diff if lo == 0 else jnp.where(rows >= lo, diff, NEG))
                        w = qh[hh][hf] * dec
                        a = jnp.sum(w * ks, axis=-1, keepdims=True)
                        da = jnp.sum(doh[hh][hf] * v_ref[r, c], axis=-1, keepdims=True)
                        if lo:
                            da = jnp.where(rows1 >= lo, da, 0.0)
                        dqh[hh][hf] = dqh[hh][hf] + (da * ks) * dec
                        pk = da * w if pk is None else pk + da * w
                        pv = a * doh[hh][hf] if pv is None else pv + a * doh[hh][hf]
                    pk_s[hh, s * SUB:(s + 1) * SUB, :] = pk
                    pv_s[hh, s * SUB:(s + 1) * SUB, :] = pv
            for hh, c in enumerate(heads):
                khi, klo = _split2(pk_s[hh])
                dk_s[blk, c] += _dot(sel_ref[...], khi) + _dot(sel_ref[...], klo)
                dv_s[blk, c] += _dot(sel_ref[...], pv_s[hh].astype(BF16))
                dq[hh] = jnp.concatenate(dqh[hh], axis=0)
                dq_s[blk, c] = dq[hh]
                db_s[blk, c] = qb[hh] * dq[hh] - kb[hh] * dk_s[blk, c]
                db_s[last, c] += dbc[hh]
        for hh in range(hb):
            dS_ref[hh] = dS[hh]

        dlf = _mm_exact(triu_ref[...], db_s[...])
        df = dlf / f - dk_s[...]
        d3_ref[0] = (dq_s[...] * _dsilu(qp)).astype(BF16)
        d3_ref[1] = (df * (1.0 - lb) * sg * (1.0 - sg)).astype(BF16)
        d3_ref[2] = dv_s[...].astype(BF16)
        dlb_ref[...] += _colsum(df * (1.0 - sg))

    rev = lambda i: nt - 1 - i
    out_blk = pl.BlockSpec((tm, hb * HEAD), lambda h, i: (rev(i), h))
    return pl.pallas_call(
        body, grid=(H // hb, nt),
        in_specs=_hgrn_specs(H, hb, tm, rev) + [pl.BlockSpec((None, hb, HEAD, HEAD), lambda h, i: (rev(i), h, 0, 0)),
                                                out_blk, _const_spec((tm, tm)), _const_spec((tm, tm)),
                                                _const_spec((BLK, BLK * SUB))],
        out_specs=[pl.BlockSpec((3, tm, hb * HEAD), lambda h, i: (0, rev(i), h)),
                   pl.BlockSpec((1, hb * HEAD), lambda h, i: (0, h))],
        out_shape=[jax.ShapeDtypeStruct((3, T, H * HEAD), BF16), jax.ShapeDtypeStruct((1, H * HEAD), F32)],
        scratch_shapes=[pltpu.VMEM((hb, HEAD, HEAD), F32), pltpu.VMEM((nblk * hb, HEAD, HEAD), F32)]
        + [pltpu.VMEM((tm, hb * HEAD), F32)] * 7 + [pltpu.VMEM((hb, BLK * SUB, HEAD), F32)] * 2, name=name,
        compiler_params=_cp("parallel", "arbitrary"))(proj, proj, proj, lb_logits, states, do, tril, triu, sel)


def hgrn_gate(o, proj, gn, *, name, tm=512):
    T, D = o.shape
    H = D // HEAD
    tm = min(tm, T)

    def body(o_ref, gp_ref, gn_ref, og_ref):
        gn_ = gn_ref[...]
        for h in range(H):
            c = slice(h * HEAD, (h + 1) * HEAD)
            oh = o_ref[:, c]
            r = lax.rsqrt(_rowmean(oh * oh) + EPS)
            og_ref[:, c] = ((oh * r) * gn_ * _silu(gp_ref[:, c])).astype(BF16)

    return pl.pallas_call(
        body, grid=(T // tm,),
        in_specs=[_row(tm, D), pl.BlockSpec((tm, D), lambda i: (i, 3)), _full(1, HEAD)],
        out_specs=_row(tm, D), out_shape=jax.ShapeDtypeStruct((T, D), BF16), name=name,
        compiler_params=_cp("parallel"))(o, proj, gn)


def hgrn_gate_bwd(dog, o, proj, gn, *, name, tm=512):
    T, D = o.shape
    H = D // HEAD
    tm = min(tm, T)

    def body(dog_ref, o_ref, gp_ref, gn_ref, do_ref, dgp_ref, dgn_ref):
        _acc_init(pl.program_id(0), dgn_ref)
        gn_ = gn_ref[...]
        for h in range(H):
            c = slice(h * HEAD, (h + 1) * HEAD)
            oh = o_ref[:, c]
            gp = gp_ref[:, c]
            dg = dog_ref[:, c]
            r = lax.rsqrt(_rowmean(oh * oh) + EPS)
            on = oh * r
            dgp_ref[:, c] = (dg * (on * gn_) * _dsilu(gp)).astype(BF16)
            don = dg * _silu(gp)
            dgn_ref[...] += _colsum(don * on)
            dn = don * gn_
            do_ref[:, c] = r * (dn - on * _rowmean(dn * on))

    return pl.pallas_call(
        body, grid=(T // tm,),
        in_specs=[_row(tm, D), _row(tm, D), pl.BlockSpec((tm, D), lambda i: (i, 3)), _full(1, HEAD)],
        out_specs=[_row(tm, D), _row(tm, D), _full(1, HEAD)],
        out_shape=[jax.ShapeDtypeStruct((T, D), F32), jax.ShapeDtypeStruct((T, D), BF16),
                   jax.ShapeDtypeStruct((1, HEAD), F32)], name=name,
        compiler_params=_cp("arbitrary"))(dog, o, proj, gn)


def _split2(x):
    hi = x.astype(BF16)
    return hi, (x - hi.astype(F32)).astype(BF16)


def ada_mod(c_all, ada_w, *, name):
    L, D, N = ada_w.shape
    B = c_all.shape[0]

    def body(c_ref, w_ref, o_ref):
        chi, clo = _split2(_silu(c_ref[...]))
        whi, wlo = _split2(w_ref[...])
        o_ref[...] = _dot(chi, whi) + _dot(chi, wlo) + _dot(clo, whi)

    return pl.pallas_call(
        body, grid=(L,), in_specs=[_full(B, D), pl.BlockSpec((None, D, N), lambda l: (l, 0, 0))],
        out_specs=pl.BlockSpec((None, B, N), lambda l: (l, 0, 0)),
        out_shape=jax.ShapeDtypeStruct((L, B, N), F32), name=name, compiler_params=_cp("parallel"))(c_all, ada_w)


def ada_wgrad(c_all_t, dmod, *, name, tr=256):
    D, B = c_all_t.shape
    L, _, N = dmod.shape
    tr = min(tr, D)

    def body(c_ref, d_ref, o_ref):
        cond = _silu(c_ref[...])
        acc = cond[:, 0:1] * d_ref[0:1, :]
        for b in range(1, B):
            acc = acc + cond[:, b:b + 1] * d_ref[b:b + 1, :]
        o_ref[...] = acc

    return pl.pallas_call(
        body, grid=(L, D // tr),
        in_specs=[pl.BlockSpec((tr, B), lambda l, r: (r, 0)), pl.BlockSpec((None, B, N), lambda l, r: (l, 0, 0))],
        out_specs=pl.BlockSpec((None, tr, N), lambda l, r: (l, r, 0)),
        out_shape=jax.ShapeDtypeStruct((L, D, N), F32), name=name,
        compiler_params=_cp("parallel", "parallel"))(c_all_t, dmod)


def sum_devices(parts, *, name):
    n, R, C = parts.shape

    def body(p_ref, o_ref):
        acc = p_ref[0]
        for d in range(1, n):
            acc = acc + p_ref[d]
        o_ref[...] = acc

    return pl.pallas_call(body, in_specs=[VMEM_SPEC], out_specs=VMEM_SPEC,
                          out_shape=jax.ShapeDtypeStruct((R, C), F32), name=name)(parts)


def lb_logits_grad(lb_logits, dlb, *, name):
    def body(lg_ref, d_ref, o_ref):
        l0, l1, l2 = lg_ref[0:1, :], lg_ref[1:2, :], lg_ref[2:3, :]
        m = jnp.maximum(jnp.maximum(l0, l1), l2)
        e0, e1, e2 = jnp.exp(l0 - m), jnp.exp(l1 - m), jnp.exp(l2 - m)
        z = e0 + e1 + e2
        p0, p1, p2 = e0 / z, e1 / z, e2 / z
        g = d_ref[...] * p0
        o_ref[0:1, :] = g * (1.0 - p0)
        o_ref[1:2, :] = -g * p1
        o_ref[2:3, :] = -g * p2

    return pl.pallas_call(body, in_specs=[VMEM_SPEC, VMEM_SPEC], out_specs=VMEM_SPEC,
                          out_shape=jax.ShapeDtypeStruct(lb_logits.shape, F32), name=name)(lb_logits, dlb)


def adamw(w, g, m, v, *, name, tr=256, after=None):
    R, C = w.shape
    tr = _tile(R, tr)
    deps = [] if after is None else [after]

    def body(w_ref, g_ref, m_ref, v_ref, *rest):
        d_ref, nm_ref, nv_ref = rest[len(deps):]
        gv = g_ref[...]
        nm = ADAM_B1 * m_ref[...] + (1.0 - ADAM_B1) * gv
        nv = ADAM_B2 * v_ref[...] + (1.0 - ADAM_B2) * (gv * gv)
        m_hat = nm / (1.0 - ADAM_B1 ** ADAM_STEP)
        v_hat = nv / (1.0 - ADAM_B2 ** ADAM_STEP)
        d_ref[...] = -ADAM_LR * (m_hat / (jnp.sqrt(v_hat) + ADAM_EPS) + ADAM_WD * w_ref[...])
        nm_ref[...] = nm
        nv_ref[...] = nv

    spec = pl.BlockSpec((tr, C), lambda i: (i, 0))
    return pl.pallas_call(
        body, grid=(R // tr,), in_specs=[spec] * 4 + [pl.BlockSpec(memory_space=pl.ANY)] * len(deps), out_specs=[spec] * 3,
        out_shape=[jax.ShapeDtypeStruct((R, C), F32)] * 3, name=name,
        compiler_params=_cp("parallel"))(w, g, m, v, *deps)


def _place():
    return lax.axis_index("x"), lax.axis_index("y"), lax.axis_index("c")


def _flip(v, bit):
    return 1 - v if bit else v


def allgather_devices(v, *, name):
    R, C = v.shape

    def body(v_ref, out_ref, send_sems, recv_sems, local_sem):
        x, y, c = _place()
        me = 4 * x + 2 * y + c
        mine = pltpu.make_async_copy(v_ref, out_ref.at[me], local_sem)
        mine.start()
        sends = []
        for k in range(1, N_DEV):
            peer = (_flip(x, k & 4), _flip(y, k & 2), _flip(c, k & 1))
            cp = pltpu.make_async_remote_copy(src_ref=v_ref, dst_ref=out_ref.at[me], send_sem=send_sems.at[k - 1],
                                              recv_sem=recv_sems.at[k - 1], device_id=peer, device_id_type=MESH)
            cp.start()
            sends.append(cp)
        for k in range(1, N_DEV):
            px, py, pc = _flip(x, k & 4), _flip(y, k & 2), _flip(c, k & 1)
            pltpu.make_async_remote_copy(src_ref=v_ref, dst_ref=out_ref.at[4 * px + 2 * py + pc],
                                         send_sem=send_sems.at[k - 1], recv_sem=recv_sems.at[k - 1],
                                         device_id=(px, py, pc), device_id_type=MESH).wait_recv()
        for cp in sends:
            cp.wait_send()
        mine.wait()

    return pl.pallas_call(
        body, in_specs=[VMEM_SPEC], out_specs=VMEM_SPEC, out_shape=jax.ShapeDtypeStruct((N_DEV, R, C), v.dtype),
        scratch_shapes=[pltpu.SemaphoreType.DMA((N_DEV - 1,)), pltpu.SemaphoreType.DMA((N_DEV - 1,)),
                        pltpu.SemaphoreType.DMA], name=name)(v)


def _other_chips(x, y):
    return [(1 - x, y), (x, 1 - y), (1 - x, 1 - y)]


SEM = pl.BlockSpec(memory_space=pltpu.SEMAPHORE)
DATAFLOW = pltpu.SideEffectType.DATAFLOW_SIDE_EFFECTING


def _chip_copy(buf, a, j, q, c, chips, send_sems, recv_sems):
    px, py = chips[j]
    return pltpu.make_async_remote_copy(src_ref=buf.at[q, c], dst_ref=buf.at[q, c], send_sem=send_sems.at[3 * a + j],
                                        recv_sem=recv_sems.at[3 * a + j], device_id=(px, py, c), device_id_type=MESH)


def allgather_chips_start(bufs, *, name):
    n = len(bufs)

    def body(*refs):
        send_sems, recv_sems = refs[n], refs[n + 1]
        outs = refs[n + 2:2 * n + 2]
        token = refs[2 * n + 2]
        x, y, c = _place()
        chips = _other_chips(x, y)
        for a in range(n):
            for j in range(3):
                _chip_copy(outs[a], a, j, 2 * x + y, c, chips, send_sems, recv_sems).start()
        token[...] = jnp.zeros_like(token)

    res = pl.pallas_call(
        body, name=name, in_specs=[HBM] * n,
        out_specs=(SEM, SEM, *([HBM] * n), VMEM_SPEC),
        out_shape=(pltpu.SemaphoreType.DMA((3 * n,)), pltpu.SemaphoreType.DMA((3 * n,)),
                   *[pltpu.HBM(b.shape, b.dtype) for b in bufs], jax.ShapeDtypeStruct((SUB, LANE), F32)),
        input_output_aliases={a: a + 2 for a in range(n)},
        compiler_params=pltpu.CompilerParams(has_side_effects=DATAFLOW),
    )(*[pltpu.with_memory_space_constraint(b, pltpu.HBM) for b in bufs])
    return res[0], res[1], list(res[2:2 + n]), res[2 + n]


def allgather_chips_wait(send_sems, recv_sems, bufs, after, *, name):
    n = len(bufs)

    def body(*refs):
        ins = refs[:n]
        send_sems, recv_sems = refs[n], refs[n + 1]
        x, y, c = _place()
        chips = _other_chips(x, y)
        for a in range(n):
            for j, (px, py) in enumerate(chips):
                _chip_copy(ins[a], a, j, 2 * x + y, c, chips, send_sems, recv_sems).wait_send()
                _chip_copy(ins[a], a, j, 2 * px + py, c, chips, send_sems, recv_sems).wait_recv()

    return list(pl.pallas_call(
        body, name=name, in_specs=[HBM] * n + [SEM, SEM, pl.BlockSpec(memory_space=pl.ANY)],
        out_specs=[HBM] * n, out_shape=[pltpu.HBM(b.shape, b.dtype) for b in bufs],
        input_output_aliases={a: a for a in range(n)},
        compiler_params=pltpu.CompilerParams(has_side_effects=DATAFLOW),
    )(*bufs, send_sems, recv_sems, after))


def forward_to_sibling(bufs, *, name):
    n = len(bufs)

    def body(*refs):
        outs = refs[n:2 * n]
        send_sems, recv_sems = refs[2 * n:]
        x, y, c = _place()
        chips = _other_chips(x, y)

        def copy(a, j, half, to):
            px, py = chips[j]
            slab = outs[a].at[2 * px + py, half]
            return pltpu.make_async_remote_copy(src_ref=slab, dst_ref=slab, send_sem=send_sems.at[a, j],
                                                recv_sem=recv_sems.at[a, j], device_id=to, device_id_type=MESH)

        sends = [copy(a, j, c, (x, y, 1 - c)) for a in range(n) for j in range(3)]
        for cp in sends:
            cp.start()
        for a in range(n):
            for j in range(3):
                copy(a, j, 1 - c, (x, y, c)).wait_recv()
        for cp in sends:
            cp.wait_send()

    return pl.pallas_call(
        body, in_specs=[HBM] * n, out_specs=[HBM] * n,
        out_shape=[jax.ShapeDtypeStruct(b.shape, b.dtype) for b in bufs],
        input_output_aliases={a: a for a in range(n)},
        scratch_shapes=[pltpu.SemaphoreType.DMA((n, 3)), pltpu.SemaphoreType.DMA((n, 3))], name=name)(*bufs)


def pair_add(g, other, c_idx, *, name, tr=256):
    _, Q, R, C = g.shape
    tr = _tile(R, tr)

    def body(c_ref, g_ref, o_ref, out_ref):
        out_ref[...] = (g_ref[...] + o_ref[...]).astype(BF16)

    return pl.pallas_call(
        body,
        grid_spec=pltpu.PrefetchScalarGridSpec(
            num_scalar_prefetch=1, grid=(Q, R // tr),
            in_specs=[pl.BlockSpec((None, None, tr, C), lambda q, r, c_ref: (c_ref[0], q, r, 0)),
                      pl.BlockSpec((None, tr, C), lambda q, r, c_ref: (q, r, 0))],
            out_specs=pl.BlockSpec((None, tr, C), lambda q, r, c_ref: (q, r, 0))),
        out_shape=jax.ShapeDtypeStruct((Q, R, C), BF16), name=name,
        compiler_params=_cp("parallel", "parallel"))(c_idx, g, other)


def chip_sum(sums, landed, qc_idx, *, name, tr=256):
    _, R, C = sums.shape
    tr = _tile(R, tr)

    def body(qc_ref, own_ref, l_ref, o_ref):
        acc = own_ref[...].astype(F32)
        for k in range(3):
            acc = acc + l_ref[k].astype(F32)
        o_ref[...] = acc

    return pl.pallas_call(
        body,
        grid_spec=pltpu.PrefetchScalarGridSpec(
            num_scalar_prefetch=1, grid=(R // tr,),
            in_specs=[pl.BlockSpec((None, tr, C), lambda r, qc: (qc[0], r, 0)),
                      pl.BlockSpec((3, tr, C), lambda r, qc: (0, r, 0))],
            out_specs=pl.BlockSpec((None, tr, C), lambda r, qc: (qc[1], r, 0))),
        out_shape=jax.ShapeDtypeStruct((2, R, C), F32), name=name,
        compiler_params=_cp("parallel"))(qc_idx, sums, landed)


def half_swap(bufs, *, name):
    n = len(bufs)

    def body(*refs):
        outs = refs[n:2 * n]
        send_sems, recv_sems = refs[2 * n:]
        x, y, c = _place()
        cps = [pltpu.make_async_remote_copy(src_ref=outs[a].at[c], dst_ref=outs[a].at[c], send_sem=send_sems.at[a],
                                            recv_sem=recv_sems.at[a], device_id=(x, y, 1 - c), device_id_type=MESH)
               for a in range(n)]
        for cp in cps:
            cp.start()
        for a in range(n):
            pltpu.make_async_remote_copy(src_ref=outs[a].at[c], dst_ref=outs[a].at[1 - c], send_sem=send_sems.at[a],
                                         recv_sem=recv_sems.at[a], device_id=(x, y, 1 - c),
                                         device_id_type=MESH).wait_recv()
        for cp in cps:
            cp.wait_send()

    return pl.pallas_call(
        body, in_specs=[HBM] * n, out_specs=[HBM] * n,
        out_shape=[jax.ShapeDtypeStruct(b.shape, b.dtype) for b in bufs],
        input_output_aliases={a: a for a in range(n)},
        scratch_shapes=[pltpu.SemaphoreType.DMA((n,)), pltpu.SemaphoreType.DMA((n,))], name=name)(*bufs)


def _chip_copies(src, dst, send_sems, recv_sems):
    x, y, c = _place()
    return [pltpu.make_async_remote_copy(src_ref=src[a].at[2 * px + py], dst_ref=dst[a].at[j],
                                         send_sem=send_sems.at[3 * a + j], recv_sem=recv_sems.at[3 * a + j],
                                         device_id=(px, py, c), device_id_type=MESH)
            for a in range(len(src)) for j, (px, py) in enumerate(_other_chips(x, y))]


def _pair_copies(src, dst, send_sems, recv_sems):
    x, y, c = _place()
    return [pltpu.make_async_remote_copy(src_ref=src[a].at[1 - c], dst_ref=dst[a], send_sem=send_sems.at[a],
                                         recv_sem=recv_sems.at[a], device_id=(x, y, 1 - c), device_id_type=MESH)
            for a in range(len(src))]


def _device_copies(src, dst, send_sems, recv_sems):
    x, y, c = _place()
    mine = src[0].at[4 * x + 2 * y + c]
    return [pltpu.make_async_remote_copy(src_ref=mine, dst_ref=mine, send_sem=send_sems.at[k - 1],
                                         recv_sem=recv_sems.at[k - 1],
                                         device_id=(_flip(x, k & 4), _flip(y, k & 2), _flip(c, k & 1)), device_id_type=MESH)
            for k in range(1, N_DEV)]


def exchange_start(src, landing, copies, n_sems, *, name):
    n, m = len(src), len(src) + len(landing)

    def body(*refs):
        send_sems, recv_sems = refs[m], refs[m + 1]
        for cp in copies(refs[m + 2:m + 2 + n], refs[m + 2 + n:2 * m + 2], send_sems, recv_sems):
            cp.start()
        token = refs[2 * m + 2]
        token[...] = jnp.zeros_like(token)

    res = pl.pallas_call(
        body, name=name, in_specs=[HBM] * m,
        out_specs=(SEM, SEM, *([HBM] * m), VMEM_SPEC),
        out_shape=(pltpu.SemaphoreType.DMA((n_sems,)), pltpu.SemaphoreType.DMA((n_sems,)),
                   *[pltpu.HBM(b.shape, b.dtype) for b in src + landing], jax.ShapeDtypeStruct((SUB, LANE), F32)),
        input_output_aliases={a: a + 2 for a in range(m)},
        compiler_params=pltpu.CompilerParams(has_side_effects=DATAFLOW),
    )(*[pltpu.with_memory_space_constraint(b, pltpu.HBM) for b in src + landing])
    return res[0], res[1], list(res[2:2 + n]), list(res[2 + n:2 + m]), res[2 + m]


def exchange_wait(send_sems, recv_sems, src, landed, copies, after, *, name):
    n, m = len(src), len(src) + len(landed)

    def body(*refs):
        for cp in copies(refs[:n], refs[n:m], refs[m], refs[m + 1]):
            cp.wait_send()
            cp.wait_recv()

    res = pl.pallas_call(
        body, name=name, in_specs=[HBM] * m + [SEM, SEM, pl.BlockSpec(memory_space=pl.ANY)],
        out_specs=[HBM] * m, out_shape=[pltpu.HBM(b.shape, b.dtype) for b in src + landed],
        input_output_aliases={a: a for a in range(m)},
        compiler_params=pltpu.CompilerParams(has_side_effects=DATAFLOW),
    )(*src, *landed, send_sems, recv_sems, after)
    return list(res[:n]), list(res[n:])


def finish_reduce(sums, landed, q, c, tag):
    qc_idx = jnp.stack([q, c]).astype(jnp.int32)
    return [chip_sum(s, l, qc_idx, name=f"grad_chip_sum_{tag}{a}") for a, (s, l) in enumerate(zip(sums, landed))]


def _ffn_forward(x, mod, pre_g, post_g, w_up, w_down, dw_w, dw_b, tag, tgt=None):
    sh, sc, gate = mod
    h = prenorm(x, pre_g, sc, sh, name=f"{tag}_prenorm")
    u0 = mm_nn(h, w_up, name=f"{tag}_up", out_dtype=BF16, perm=_ffn_perm)
    z, ab = ffn_act(u0, dw_w, dw_b, name=f"{tag}_act")
    y = mm_nn(z, w_down, name=f"{tag}_down")
    if tgt is None:
        out = post_residual(x, y, post_g, gate, name=f"{tag}_post")
    else:
        out = post_residual_loss(x, y, post_g, gate, tgt, name=f"{tag}_post_loss")
    return out, (x, h, u0, ab, z, y)


def _ffn_backward(dx, saved, mod, pre_g, post_g, w_up, w_down, dw_w, dw_b, tag):
    x, h, u0, ab, z, y = saved
    sh, sc, gate = mod
    dy, dgate, dpost, _ = post_bwd(dx, y, post_g, gate, name=f"{tag}_post_bwd")
    dz = mm_nt(dy, w_down, name=f"{tag}_down_dx", out_dtype=BF16)
    g_down = mm_tn(z, dy, name=f"{tag}_down_dw", J=2, block="a", row_chips=2)
    du0, dconv = ffn_act_bwd(dz, u0, ab, dw_w, name=f"{tag}_act_bwd")
    dh = mm_nt(du0, w_up, name=f"{tag}_up_dx", perm=_ffn_perm)
    g_up = mm_tn(h, du0, name=f"{tag}_up_dw", J=4, block="b", perm=_ffn_perm)
    dx_in, dsh, dsc, dpre = prenorm_bwd(dh, x, dx, pre_g, sc, name=f"{tag}_prenorm_bwd")
    nb = u0.shape[1] // 4
    dconv = dconv[:, 0].reshape(4, 2, 2, nb).transpose(0, 2, 1, 3).reshape(4, 4 * nb)
    return dx_in, dict(dsh=dsh, dsc=dsc, dgate=dgate, dpre=dpre, dpost=dpost, g_up=g_up, g_down=g_down,
                       d_dw_w=dconv[0:FFN_W], d_dw_b=dconv[3:4])


def _local_step(x, tgt, mods, P, late_weights=None, grads_ready=None):
    m0, m1 = mods
    h1 = prenorm(x, P["pre_mix_g"][0:1], m0[1], m0[0], name="hgrn_prenorm")
    proj = mm_nn(h1, P["hgrn_w_in"], name="hgrn_in")
    o, states = hgrn_scan(proj, P["hgrn_lb_logits"], name="hgrn_scan")
    og = hgrn_gate(o, proj, P["hgrn_gnorm_g"], name="hgrn_gate")
    y1 = mm_nn(og, P["hgrn_w_out"], name="hgrn_out")
    x1 = post_residual(x, y1, P["post_mix_g"][0:1], m0[2], name="hgrn_post")
    if late_weights is not None:
        P = {**P, **late_weights(x1)}
    x2, ffn0 = _ffn_forward(x1, m0[3:6], P["pre_ffn_g"][0:1], P["post_ffn_g"][0:1], P["ffn_w_up"][0],
                            P["ffn_w_down"][0], P["ffn_dw_w"][0], P["ffn_dw_b"][0:1], "ffn0")
    h3 = prenorm(x2, P["pre_mix_g"][1:2], m1[1], m1[0], name="conv_prenorm")
    u = mm_nn(h3, P["conv_w_in"], name="conv_in", bias=P["conv_b_in"])
    s, cv = conv_act(u, P["conv_dw_w"], P["conv_dw_b"], P["conv_ln_g"], P["conv_ln_b"], name="conv_act")
    y3 = mm_nn(s, P["conv_w_out"], name="conv_out", bias=P["conv_b_out"])
    x3 = post_residual(x2, y3, P["post_mix_g"][1:2], m1[2], name="conv_post")
    (dx4, lcols), ffn1 = _ffn_forward(x3, m1[3:6], P["pre_ffn_g"][1:2], P["post_ffn_g"][1:2], P["ffn_w_up"][1],
                                      P["ffn_w_down"][1], P["ffn_dw_w"][1], P["ffn_dw_b"][1:2], "ffn1", tgt)
    dx3, f1 = _ffn_backward(dx4, ffn1, m1[3:6], P["pre_ffn_g"][1:2], P["post_ffn_g"][1:2], P["ffn_w_up"][1],
                            P["ffn_w_down"][1], P["ffn_dw_w"][1], P["ffn_dw_b"][1:2], "ffn1")
    dy3, dg1_1, dpostmix1, d_b_out = post_bwd(dx3, y3, P["post_mix_g"][1:2], m1[2], name="conv_post_bwd")
    ds = mm_nt(dy3, P["conv_w_out"], name="conv_out_dx")
    g_conv_out = mm_tn(s, dy3, name="conv_out_dw", J=1, block="a", row_chips=4)
    dcv, d_ln_g, d_ln_b, d_dw_b = conv_norm_bwd(ds, cv, P["conv_ln_g"], P["conv_ln_b"], name="conv_norm_bwd")
    du, d_dw_w, d_b_in = conv_glu_bwd(dcv, u, P["conv_dw_w"], name="conv_glu_bwd")
    dh3 = mm_nt(du, P["conv_w_in"], name="conv_in_dx")
    g_conv_in = mm_tn(h3, du, name="conv_in_dw", J=2, block="b", col_chips=2)
    dx2, dsh1_1, dsc1_1, dpremix1 = prenorm_bwd(dh3, x2, dx3, P["pre_mix_g"][1:2], m1[1], name="conv_prenorm_bwd")
    if grads_ready is not None:
        token = grads_ready("l1", [g_conv_in, g_conv_out, f1["g_up"], f1["g_down"]])
        m0 = tuple(m + token[0:1, 0:1] for m in m0)
    dx1, f0 = _ffn_backward(dx2, ffn0, m0[3:6], P["pre_ffn_g"][0:1], P["post_ffn_g"][0:1], P["ffn_w_up"][0],
                            P["ffn_w_down"][0], P["ffn_dw_w"][0], P["ffn_dw_b"][0:1], "ffn0")
    if grads_ready is not None:
        token = grads_ready("f0", [f0["g_up"], f0["g_down"]])
        m0 = tuple(m + token[0:1, 0:1] for m in m0)
    dy1, dg1_0, dpostmix0, _ = post_bwd(dx1, y1, P["post_mix_g"][0:1], m0[2], name="hgrn_post_bwd")
    dog = mm_nt(dy1, P["hgrn_w_out"], name="hgrn_out_dx")
    g_hgrn_out = mm_tn(og, dy1, name="hgrn_out_dw", J=1, block="a", row_chips=4)
    do, dgp, d_gn = hgrn_gate_bwd(dog, o, proj, P["hgrn_gnorm_g"], name="hgrn_gate_bwd")
    d3, dlb = hgrn_scan_bwd(proj, P["hgrn_lb_logits"], states, do, name="hgrn_scan_bwd")
    g_hgrn_in = mm_tn_parts(h1, d3, dgp, name="hgrn_in_dw")
    token = grads_ready("hg", [g_hgrn_in, g_hgrn_out]) if grads_ready is not None else None
    dh1 = mm_nt_parts(d3, dgp, P["hgrn_w_in"], name="hgrn_in_dx", after=token)
    dx0, dsh1_0, dsc1_0, dpremix0 = prenorm_bwd(dh1, x, dx1, P["pre_mix_g"][0:1], m0[1], name="hgrn_prenorm_bwd")

    dmod = jnp.stack([
        jnp.concatenate([dsh1_0, dsc1_0, dg1_0, f0["dsh"], f0["dsc"], f0["dgate"]], axis=1)[0],
        jnp.concatenate([dsh1_1, dsc1_1, dg1_1, f1["dsh"], f1["dsc"], f1["dgate"]], axis=1)[0]])
    small = dict(
        loss=lcols,
        pre_mix_g=jnp.concatenate([dpremix0, dpremix1]), post_mix_g=jnp.concatenate([dpostmix0, dpostmix1]),
        pre_ffn_g=jnp.concatenate([f0["dpre"], f1["dpre"]]), post_ffn_g=jnp.concatenate([f0["dpost"], f1["dpost"]]),
        lb=dlb, hgrn_gnorm_g=d_gn, ffn_dw_b=jnp.concatenate([f0["d_dw_b"], f1["d_dw_b"]]), dmod=dmod,
        conv_b_in=d_b_in, conv_dw_w=d_dw_w[0:CONV_W], conv_dw_b=d_dw_b, conv_ln_g=d_ln_g, conv_ln_b=d_ln_b,
        conv_b_out=d_b_out, ffn_dw_w=jnp.stack([f0["d_dw_w"], f1["d_dw_w"]]))
    big = [g_hgrn_in, g_hgrn_out, g_conv_in, g_conv_out, f0["g_up"], f1["g_up"], f0["g_down"], f1["g_down"]]
    return dx0, small, big


def _pack(parts, rows=8):
    flat = jnp.concatenate([p.reshape(-1).astype(F32) for p in parts])
    per = rows * 128
    pad = (-flat.shape[0]) % per
    return jnp.pad(flat, (0, pad)).reshape(rows, -1)


def _unpack(flat, shapes):
    out, off = [], 0
    for s in shapes:
        n = 1
        for d in s:
            n *= d
        out.append(flat[..., off:off + n].reshape(flat.shape[:-1] + tuple(s)))
        off += n
    return out


def _from_chips(stacked, axis):
    moved = jnp.moveaxis(stacked, 0, axis)
    shape = list(moved.shape)
    return moved.reshape(shape[:axis] + [shape[axis] * shape[axis + 1]] + shape[axis + 2:])


def _my_shard(full, axis, q):
    n = full.shape[axis] // N_CHIPS
    return lax.dynamic_slice_in_dim(full, q * n, n, axis=axis)


def kernel(x, c, ada_w, ada_b, pre_mix_g, post_mix_g, pre_ffn_g, post_ffn_g, hgrn_w_in, hgrn_lb_logits, hgrn_gnorm_g, hgrn_w_out, conv_w_in, conv_b_in, conv_dw_w, conv_dw_b, conv_ln_g, conv_ln_b, conv_w_out, conv_b_out, ffn_w_up, ffn_dw_w, ffn_dw_b, ffn_w_down, loss_target, m_ada_w, m_ada_b, m_pre_mix_g, m_post_mix_g, m_pre_ffn_g, m_post_ffn_g, m_hgrn_w_in, m_hgrn_lb_logits, m_hgrn_gnorm_g, m_hgrn_w_out, m_conv_w_in, m_conv_b_in, m_conv_dw_w, m_conv_dw_b, m_conv_ln_g, m_conv_ln_b, m_conv_w_out, m_conv_b_out, m_ffn_w_up, m_ffn_dw_w, m_ffn_dw_b, m_ffn_w_down, v_ada_w, v_ada_b, v_pre_mix_g, v_post_mix_g, v_pre_ffn_g, v_post_ffn_g, v_hgrn_w_in, v_hgrn_lb_logits, v_hgrn_gnorm_g, v_hgrn_w_out, v_conv_w_in, v_conv_b_in, v_conv_dw_w, v_conv_dw_b, v_conv_ln_g, v_conv_ln_b, v_conv_w_out, v_conv_b_out, v_ffn_w_up, v_ffn_dw_w, v_ffn_dw_b, v_ffn_w_down):
    W = dict(ada_w=ada_w, ada_b=ada_b, pre_mix_g=pre_mix_g, post_mix_g=post_mix_g, pre_ffn_g=pre_ffn_g,
             post_ffn_g=post_ffn_g, hgrn_w_in=hgrn_w_in, hgrn_lb_logits=hgrn_lb_logits, hgrn_gnorm_g=hgrn_gnorm_g,
             hgrn_w_out=hgrn_w_out, conv_w_in=conv_w_in, conv_b_in=conv_b_in, conv_dw_w=conv_dw_w,
             conv_dw_b=conv_dw_b, conv_ln_g=conv_ln_g, conv_ln_b=conv_ln_b, conv_w_out=conv_w_out,
             conv_b_out=conv_b_out, ffn_w_up=ffn_w_up, ffn_dw_w=ffn_dw_w, ffn_dw_b=ffn_dw_b, ffn_w_down=ffn_w_down)
    M = dict(ada_w=m_ada_w, ada_b=m_ada_b, pre_mix_g=m_pre_mix_g, post_mix_g=m_post_mix_g, pre_ffn_g=m_pre_ffn_g,
             post_ffn_g=m_post_ffn_g, hgrn_w_in=m_hgrn_w_in, hgrn_lb_logits=m_hgrn_lb_logits,
             hgrn_gnorm_g=m_hgrn_gnorm_g, hgrn_w_out=m_hgrn_w_out, conv_w_in=m_conv_w_in, conv_b_in=m_conv_b_in,
             conv_dw_w=m_conv_dw_w, conv_dw_b=m_conv_dw_b, conv_ln_g=m_conv_ln_g, conv_ln_b=m_conv_ln_b,
             conv_w_out=m_conv_w_out, conv_b_out=m_conv_b_out, ffn_w_up=m_ffn_w_up, ffn_dw_w=m_ffn_dw_w,
             ffn_dw_b=m_ffn_dw_b, ffn_w_down=m_ffn_w_down)
    V = dict(ada_w=v_ada_w, ada_b=v_ada_b, pre_mix_g=v_pre_mix_g, post_mix_g=v_post_mix_g, pre_ffn_g=v_pre_ffn_g,
             post_ffn_g=v_post_ffn_g, hgrn_w_in=v_hgrn_w_in, hgrn_lb_logits=v_hgrn_lb_logits,
             hgrn_gnorm_g=v_hgrn_gnorm_g, hgrn_w_out=v_hgrn_w_out, conv_w_in=v_conv_w_in, conv_b_in=v_conv_b_in,
             conv_dw_w=v_conv_dw_w, conv_dw_b=v_conv_dw_b, conv_ln_g=v_conv_ln_g, conv_ln_b=v_conv_ln_b,
             conv_w_out=v_conv_w_out, conv_b_out=v_conv_b_out, ffn_w_up=v_ffn_w_up, ffn_dw_w=v_ffn_dw_w,
             ffn_dw_b=v_ffn_dw_b, ffn_w_down=v_ffn_w_down)
    names = list(W)
    xi, yi, ci = lax.axis_index("x"), lax.axis_index("y"), lax.axis_index("c")
    q = 2 * xi + yi
    me = 2 * q + ci
    D = x.shape[-1]
    L = ada_w.shape[0]

    small_w = ["conv_b_in", "conv_dw_w", "conv_dw_b", "conv_ln_g", "conv_ln_b", "conv_b_out", "ffn_dw_w"]
    small_axis = dict(conv_b_in=1, conv_dw_w=2, conv_dw_b=1, conv_ln_g=1, conv_ln_b=1, conv_b_out=1, ffn_dw_w=2)
    packed = _pack([c] + [W[n] for n in small_w])

    def halves(w):
        shard = w.astype(BF16).reshape(1, 2, w.shape[0] // 2, w.shape[1])
        buf = lax.empty((N_CHIPS,) + shard.shape[1:], BF16)
        return lax.dynamic_update_slice_in_dim(buf, shard, q, axis=0)

    hg_send, hg_recv, hg_bufs, hg_token = allgather_chips_start([halves(hgrn_w_in[0]), halves(hgrn_w_out[0])],
                                                                name="gather_hgrn_weights_start")
    packed, _ = lax.optimization_barrier((packed, hg_token))
    gathered = allgather_devices(packed, name="gather_small_params").reshape(N_DEV, -1)
    c_all = gathered[:, 0:D]
    per_chip = gathered.reshape(N_CHIPS, 2, -1)[:, 0, D:]
    parts = _unpack(per_chip, [W[n].shape for n in small_w])
    P = {n: _from_chips(p, small_axis[n]) for n, p in zip(small_w, parts)}
    P["conv_dw_w"] = P["conv_dw_w"][0]
    for n in ("pre_mix_g", "post_mix_g", "pre_ffn_g", "post_ffn_g", "hgrn_lb_logits", "hgrn_gnorm_g", "ffn_dw_b"):
        P[n] = W[n]

    modp = ada_mod(c_all, ada_w, name="ada_mod")
    ncol = modp.shape[-1]
    mod_all = allgather_devices(modp.reshape(L * N_DEV, ncol), name="gather_mod")
    mod_all = mod_all.reshape(N_CHIPS, 2, L, N_DEV, ncol)[:, 0]
    mod_me = lax.dynamic_index_in_dim(mod_all, me, axis=2, keepdims=False)
    mod = mod_me.transpose(1, 0, 2).reshape(L, N_CHIPS * ncol) + ada_b
    mods = [tuple(mod[l:l + 1, k * D:(k + 1) * D] for k in range(6)) for l in range(L)]

    stack = lambda t: t.reshape(N_CHIPS, t.shape[1] * t.shape[2], t.shape[3])
    rowsh = lambda t: t.reshape(1, N_CHIPS * t.shape[1] * t.shape[2], t.shape[3])
    pairs = lambda t: t.reshape(2, 2, t.shape[1], t.shape[2]).transpose(0, 2, 1, 3).reshape(2, t.shape[1], 2 * t.shape[2])
    g = forward_to_sibling(allgather_chips_wait(hg_send, hg_recv, hg_bufs, mod, name="gather_hgrn_weights_wait"),
                           name="gather_hgrn_weights_forward")
    P["hgrn_w_in"], P["hgrn_w_out"] = stack(g[0]), rowsh(g[1])
    late_shards = [conv_w_in[0], conv_w_out[0], ffn_w_up[0], ffn_w_up[1], ffn_w_down[0], ffn_w_down[1]]
    late_bufs, _, _ = lax.optimization_barrier(([halves(w) for w in late_shards], g, mod))
    send_sems, recv_sems, bufs, token = allgather_chips_start(late_bufs, name="gather_weights_start")
    mods[0] = tuple(m + token[0:1, 0:1] for m in mods[0])

    def late_weights(x1):
        landed = allgather_chips_wait(send_sems, recv_sems, bufs, x1, name="gather_weights_wait")
        g = forward_to_sibling(landed, name="gather_weights_forward")
        return dict(conv_w_in=pairs(stack(g[0])), conv_w_out=rowsh(g[1]), ffn_w_up=[stack(g[2]), stack(g[3])],
                    ffn_w_down=[rowsh(g[4]), rowsh(g[5])])

    c_idx = ci.astype(jnp.int32).reshape(1)
    pending, in_flight = {}, {}

    def chip_stage(after):
        tag, (send, recv, grads, landing) = pending.popitem()
        grads, others = exchange_wait(send, recv, grads, landing, _pair_copies, after, name=f"grad_pair_wait_{tag}")
        sums = [pair_add(g_, o_, c_idx, name=f"grad_pair_add_{tag}_{a}") for a, (g_, o_) in enumerate(zip(grads, others))]
        landing = [lax.empty((3,) + s_.shape[1:], s_.dtype) for s_ in sums]
        send, recv, sums, landing, tok = exchange_start(sums, landing, _chip_copies, 3 * len(sums),
                                                        name=f"grad_chip_exchange_start_{tag}")
        in_flight[tag] = (send, recv, sums, landing)
        return tok

    def grads_ready(tag, grads):
        tok = chip_stage(grads[0]) if pending else 0.0
        landing = [lax.empty(g_.shape[1:], g_.dtype) for g_ in grads]
        send, recv, grads, landing, tok2 = exchange_start(grads, landing, _pair_copies, len(grads),
                                                          name=f"grad_pair_start_{tag}")
        pending[tag] = (send, recv, grads, landing)
        return tok + tok2

    grad_x, small, big = _local_step(x[0], loss_target[0], mods, P, late_weights, grads_ready)

    small_names = list(small)
    packed_g = _pack([small[n] for n in small_names])
    gs_buf = lax.dynamic_update_slice_in_dim(lax.empty((N_DEV,) + packed_g.shape, F32), packed_g[None], me, axis=0)
    sg_send, sg_recv, gs_buf, _, tok_sg = exchange_start([gs_buf], [], _device_copies, N_DEV - 1,
                                                         name="gather_small_grads_start")

    tok_hg = chip_stage(tok_sg)
    G = {}
    halves = []
    for tag in ("f0", "l1"):
        sums_t, landed_t = exchange_wait(*in_flight[tag], _chip_copies, grad_x, name=f"grad_chip_exchange_wait_{tag}")
        halves += finish_reduce(sums_t, landed_t, q, ci, f"{tag}_")
    red = [f.reshape(2 * f.shape[1], f.shape[2]) for f in half_swap(halves, name="grad_half_swap")]
    G["conv_w_in"], G["conv_w_out"] = red[2][None], red[3][None]
    G["ffn_w_up"] = jnp.stack([red[0], red[4]])
    G["ffn_w_down"] = jnp.stack([red[1], red[5]])

    delta, new_m, new_v = {}, {}, {}

    def adamw_matrix(n, after=None):
        shp = W[n].shape
        two = lambda t: t.reshape(-1, shp[-1])
        d_, m_, v_ = adamw(two(W[n]), two(G[n]), two(M[n]), two(V[n]), name=f"adamw_{n}", after=after)
        delta[n], new_m[n], new_v[n] = d_.reshape(shp), m_.reshape(shp), v_.reshape(shp)

    big_names = ["ada_w", "hgrn_w_in", "hgrn_w_out", "conv_w_in", "conv_w_out", "ffn_w_up", "ffn_w_down"]
    for n in ("conv_w_in", "conv_w_out", "ffn_w_up", "ffn_w_down"):
        adamw_matrix(n, after=tok_hg)

    (gs,), _ = exchange_wait(sg_send, sg_recv, gs_buf, [], _device_copies, delta["ffn_w_down"],
                             name="gather_small_grads_wait")
    dmod_all = _unpack(gs.reshape(N_DEV, -1), [small[n].shape for n in small_names])[small_names.index("dmod")]
    tot = sum_devices(gs, name="sum_small_grads").reshape(1, -1)
    S = dict(zip(small_names, _unpack(tot, [small[n].shape for n in small_names])))
    S = {n: v[0] for n, v in S.items()}
    loss = 0.5 * jnp.sum(S["loss"]) / D

    dmod_q = lax.dynamic_slice_in_dim(dmod_all, q * ncol, ncol, axis=2)
    G["ada_w"] = ada_wgrad(c_all.T, dmod_q.transpose(1, 0, 2), name="ada_wgrad")
    G["ada_b"] = S["dmod"]
    for n in ("pre_mix_g", "post_mix_g", "pre_ffn_g", "post_ffn_g", "hgrn_gnorm_g", "ffn_dw_b"):
        G[n] = S[n]
    G["hgrn_lb_logits"] = lb_logits_grad(hgrn_lb_logits, S["lb"], name="lb_logits_grad")
    G["conv_b_in"] = _my_shard(S["conv_b_in"], 1, q)
    G["conv_dw_w"] = _my_shard(S["conv_dw_w"], 1, q)[None]
    for n in ("conv_dw_b", "conv_ln_g", "conv_ln_b", "conv_b_out"):
        G[n] = _my_shard(S[n], 1, q)
    G["ffn_dw_w"] = _my_shard(S["ffn_dw_w"], 2, q)
    adamw_matrix("ada_w")

    sums_h, landed_h = exchange_wait(*in_flight["hg"], _chip_copies, delta["ada_w"], name="grad_chip_exchange_wait_hg")
    red_h = half_swap(finish_reduce(sums_h, landed_h, q, ci, "hg_"), name="grad_half_swap_hg")
    G["hgrn_w_in"], G["hgrn_w_out"] = [f.reshape(1, 2 * f.shape[1], f.shape[2]) for f in red_h]
    for n in ("hgrn_w_in", "hgrn_w_out"):
        adamw_matrix(n)
    rest = [n for n in names if n not in big_names]
    d_, m_, v_ = adamw(_pack([W[n] for n in rest]), _pack([G[n] for n in rest]), _pack([M[n] for n in rest]),
                       _pack([V[n] for n in rest]), name="adamw_small")
    shapes = [W[n].shape for n in rest]
    for n, a, b_, c_ in zip(rest, _unpack(d_.reshape(-1), shapes), _unpack(m_.reshape(-1), shapes),
                            _unpack(v_.reshape(-1), shapes)):
        delta[n], new_m[n], new_v[n] = a, b_, c_

    return (loss, grad_x[None], *[G[n].reshape(W[n].shape) for n in names], *[delta[n] for n in names],
            *[new_m[n] for n in names], *[new_v[n] for n in names])
```

```python
import jax
import jax.numpy as jnp
from jax import lax
from jax.experimental import pallas as pl
from jax.experimental.pallas import tpu as pltpu

F32 = jnp.float32
BF16 = jnp.bfloat16
EPS = 1e-6
HEAD = 128
BLK = 16
NEG = -1e30
CONV_W = 31
FFN_W = 3
N_CHIPS = 4
N_DEV = 8
SUB = 8
LANE = 128
V7X_VMEM_LIMIT = 56 * 1024 * 1024
MESH = pl.DeviceIdType.MESH
HBM = pl.BlockSpec(memory_space=pltpu.HBM)
VMEM_SPEC = pl.BlockSpec(memory_space=pltpu.VMEM)

ADAM_LR = 0.001
ADAM_B1 = 0.9
ADAM_B2 = 0.999
ADAM_EPS = 1e-08
ADAM_WD = 0.01
ADAM_STEP = 10


def _cp(*sem):
    return pltpu.CompilerParams(dimension_semantics=sem, vmem_limit_bytes=V7X_VMEM_LIMIT)


def _sig(x):
    return 0.5 * jnp.tanh(0.5 * x) + 0.5


def _silu(x):
    return x * _sig(x)


def _dsilu(x):
    s = _sig(x)
    return s * (1.0 + x * (1.0 - s))


def _dot(a, b):
    return jnp.dot(a, b, preferred_element_type=F32)


def _dot_nt(a, b):
    return lax.dot_general(a, b, (((1,), (1,)), ((), ())), preferred_element_type=F32)


def _dot_tn(a, b):
    return lax.dot_general(a, b, (((0,), (0,)), ((), ())), preferred_element_type=F32)


def _colsum(x):
    return jnp.sum(x, axis=0, keepdims=True)


def _rowmean(x):
    return jnp.mean(x, axis=-1, keepdims=True)


def _ffn_perm(j):
    return (j % 2) * 2 + j // 2


def _tile(n, pref):
    if n <= pref:
        return n
    t = pref - pref % 8
    while n % t:
        t -= 8
    return t


def mm_nn(a, w, *, name, bias=None, out_dtype=F32, perm=None, tm=1024):
    T, K = a.shape
    J, _, nb = w.shape
    tm = min(tm, T)
    col = (lambda j: j) if perm is None else perm

    def body(a_ref, w_ref, *rest):
        acc = _dot(a_ref[...], w_ref[...])
        if bias is not None:
            acc = acc + rest[0][...]
        rest[-1][...] = acc.astype(out_dtype)

    in_specs = [pl.BlockSpec((tm, K), lambda j, i: (i, 0)), pl.BlockSpec((None, K, nb), lambda j, i: (j, 0, 0))]
    args = [a, w]
    if bias is not None:
        in_specs.append(pl.BlockSpec((1, nb), lambda j, i: (0, j)))
        args.append(bias)
    return pl.pallas_call(
        body, grid=(J, T // tm), in_specs=in_specs,
        out_specs=pl.BlockSpec((tm, nb), lambda j, i: (i, col(j))),
        out_shape=jax.ShapeDtypeStruct((T, J * nb), out_dtype), name=name,
        compiler_params=_cp("parallel", "parallel"))(*args)


def mm_nt(a, w, *, name, out_dtype=F32, perm=None, tm=1024, after=None):
    T = a.shape[0]
    J, K, nb = w.shape
    tm = min(tm, T)
    col = (lambda j: j) if perm is None else perm
    deps = [] if after is None else [after]

    def body(a_ref, w_ref, *rest):
        o_ref, acc_ref = rest[len(deps):]
        j = pl.program_id(1)

        @pl.when(j == 0)
        def _():
            acc_ref[...] = jnp.zeros_like(acc_ref)

        acc_ref[...] += _dot_nt(a_ref[...], w_ref[...])

        @pl.when(j == J - 1)
        def _():
            o_ref[...] = acc_ref[...].astype(out_dtype)

    return pl.pallas_call(
        body, grid=(T // tm, J),
        in_specs=[pl.BlockSpec((tm, nb), lambda i, j: (i, col(j))), pl.BlockSpec((None, K, nb), lambda i, j: (j, 0, 0))]
        + [pl.BlockSpec(memory_space=pl.ANY)] * len(deps),
        out_specs=pl.BlockSpec((tm, K), lambda i, j: (i, 0)),
        out_shape=jax.ShapeDtypeStruct((T, K), out_dtype),
        scratch_shapes=[pltpu.VMEM((tm, K), F32)], name=name,
        compiler_params=_cp("parallel", "arbitrary"))(a, w, *deps)


def mm_tn(a, b, *, name, J, block, row_chips=1, col_chips=1, perm=None, tk=1024):
    T = a.shape[0]
    tk = min(tk, T)
    col = (lambda j: j) if perm is None else perm
    if block == "b":
        rows, nb = a.shape[1], b.shape[1] // J
        a_spec = pl.BlockSpec((tk, rows), lambda j, t: (t, 0))
        b_spec = pl.BlockSpec((tk, nb), lambda j, t: (t, col(j)))
    else:
        rows, nb = a.shape[1] // J, b.shape[1]
        a_spec = pl.BlockSpec((tk, rows), lambda j, t: (t, col(j)))
        b_spec = pl.BlockSpec((tk, nb), lambda j, t: (t, 0))
    rh = rows // (2 * row_chips)
    nc = nb // col_chips
    chips = [(rc, cc) for rc in range(row_chips) for cc in range(col_chips)]

    def body(a_ref, b_ref, o_ref):
        @pl.when(pl.program_id(1) == 0)
        def _():
            o_ref[...] = jnp.zeros_like(o_ref)

        acc = _dot_tn(a_ref[...], b_ref[...])
        for ch, (rc, cc) in enumerate(chips):
            for hf in range(2):
                r0 = (rc * 2 + hf) * rh
                o_ref[hf, ch] += acc[r0:r0 + rh, cc * nc:(cc + 1) * nc]

    return pl.pallas_call(
        body, grid=(J, T // tk), in_specs=[a_spec, b_spec],
        out_specs=pl.BlockSpec((2, len(chips), rh, nc), lambda j, t: (0, j, 0, 0)),
        out_shape=jax.ShapeDtypeStruct((2, J * len(chips), rh, nc), F32), name=name,
        compiler_params=_cp("parallel", "arbitrary"))(a, b)


def mm_nt_parts(s3, g, w, *, name, tm=1024, after=None):
    n3, T, nb = s3.shape
    J, K, _ = w.shape
    tm = min(tm, T)
    deps = [] if after is None else [after]

    def body(s_ref, g_ref, w_ref, *rest):
        o_ref, acc_ref = rest[len(deps):]
        j = pl.program_id(1)

        @pl.when(j == 0)
        def _():
            acc_ref[...] = jnp.zeros_like(acc_ref)

        @pl.when(j < n3)
        def _():
            acc_ref[...] += _dot_nt(s_ref[...], w_ref[...])

        @pl.when(j == n3)
        def _():
            acc_ref[...] += _dot_nt(g_ref[...], w_ref[...])

        @pl.when(j == J - 1)
        def _():
            o_ref[...] = acc_ref[...]

    return pl.pallas_call(
        body, grid=(T // tm, J),
        in_specs=[pl.BlockSpec((None, tm, nb), lambda i, j: (jnp.minimum(j, n3 - 1), i, 0)),
                  pl.BlockSpec((tm, nb), lambda i, j: (i, 0)), pl.BlockSpec((None, K, nb), lambda i, j: (j, 0, 0))]
        + [pl.BlockSpec(memory_space=pl.ANY)] * len(deps),
        out_specs=pl.BlockSpec((tm, K), lambda i, j: (i, 0)), out_shape=jax.ShapeDtypeStruct((T, K), F32),
        scratch_shapes=[pltpu.VMEM((tm, K), F32)], name=name,
        compiler_params=_cp("parallel", "arbitrary"))(s3, g, w, *deps)


def mm_tn_parts(a, s3, g, *, name, tk=1024):
    n3, T, nb = s3.shape
    J = n3 + 1
    tk = min(tk, T)
    rows = a.shape[1]
    rh = rows // 2

    def body(a_ref, s_ref, g_ref, o_ref):
        j = pl.program_id(0)

        @pl.when(pl.program_id(1) == 0)
        def _():
            o_ref[...] = jnp.zeros_like(o_ref)

        def add(b_ref):
            acc = _dot_tn(a_ref[...], b_ref[...])
            for hf in range(2):
                o_ref[hf, 0] += acc[hf * rh:(hf + 1) * rh, :]

        pl.when(j < n3)(lambda: add(s_ref))
        pl.when(j == n3)(lambda: add(g_ref))

    return pl.pallas_call(
        body, grid=(J, T // tk),
        in_specs=[pl.BlockSpec((tk, rows), lambda j, t: (t, 0)),
                  pl.BlockSpec((None, tk, nb), lambda j, t: (jnp.minimum(j, n3 - 1), t, 0)),
                  pl.BlockSpec((tk, nb), lambda j, t: (t, 0))],
        out_specs=pl.BlockSpec((2, 1, rh, nb), lambda j, t: (0, j, 0, 0)),
        out_shape=jax.ShapeDtypeStruct((2, J, rh, nb), F32), name=name,
        compiler_params=_cp("parallel", "arbitrary"))(a, s3, g)


def _row(tm, w):
    return pl.BlockSpec((tm, w), lambda i: (i, 0))


def _full(r, w):
    return pl.BlockSpec((r, w), lambda i: (0, 0))


def _acc_init(i, *refs):
    @pl.when(i == 0)
    def _():
        for r in refs:
            r[...] = jnp.zeros_like(r)


def prenorm(x, g, sc, sh, *, name, tm=512):
    T, D = x.shape
    tm = min(tm, T)

    def body(x_ref, g_ref, sc_ref, sh_ref, h_ref):
        xv = x_ref[...]
        r = lax.rsqrt(_rowmean(xv * xv) + EPS)
        h_ref[...] = ((xv * r) * g_ref[...] * (1.0 + sc_ref[...]) + sh_ref[...]).astype(BF16)

    return pl.pallas_call(
        body, grid=(T // tm,), in_specs=[_row(tm, D), _full(1, D), _full(1, D), _full(1, D)],
        out_specs=_row(tm, D), out_shape=jax.ShapeDtypeStruct((T, D), BF16), name=name,
        compiler_params=_cp("parallel"))(x, g, sc, sh)


def post_residual_prenorm(x, y, g, gate, g2, sc2, sh2, *, name, tm=512):
    T, D = x.shape
    tm = min(tm, T)

    def body(x_ref, y_ref, g_ref, gate_ref, g2_ref, sc2_ref, sh2_ref, o_ref, h_ref):
        yv = y_ref[...]
        r = lax.rsqrt(_rowmean(yv * yv) + EPS)
        out = x_ref[...] + gate_ref[...] * ((yv * r) * g_ref[...])
        o_ref[...] = out
        r2 = lax.rsqrt(_rowmean(out * out) + EPS)
        h_ref[...] = ((out * r2) * g2_ref[...] * (1.0 + sc2_ref[...]) + sh2_ref[...]).astype(BF16)

    return pl.pallas_call(
        body, grid=(T // tm,), in_specs=[_row(tm, D), _row(tm, D)] + [_full(1, D)] * 5,
        out_specs=[_row(tm, D), _row(tm, D)],
        out_shape=[jax.ShapeDtypeStruct((T, D), F32), jax.ShapeDtypeStruct((T, D), BF16)], name=name,
        compiler_params=_cp("parallel"))(x, y, g, gate, g2, sc2, sh2)


def post_residual_loss(x, y, g, gate, tgt, *, name, tm=512):
    T, D = x.shape
    tm = min(tm, T)

    def body(x_ref, y_ref, g_ref, gate_ref, t_ref, dx_ref, l_ref):
        _acc_init(pl.program_id(0), l_ref)
        yv = y_ref[...]
        r = lax.rsqrt(_rowmean(yv * yv) + EPS)
        e = x_ref[...] + gate_ref[...] * ((yv * r) * g_ref[...]) - t_ref[...]
        dx_ref[...] = e * (1.0 / D)
        l_ref[...] += _colsum(e * e)

    return pl.pallas_call(
        body, grid=(T // tm,), in_specs=[_row(tm, D), _row(tm, D), _full(1, D), _full(1, D), _row(tm, D)],
        out_specs=[_row(tm, D), _full(1, D)],
        out_shape=[jax.ShapeDtypeStruct((T, D), F32), jax.ShapeDtypeStruct((1, D), F32)], name=name,
        compiler_params=_cp("arbitrary"))(x, y, g, gate, tgt)


def post_bwd(dx, y, g, gate, *, name, tm=512):
    T, D = dx.shape
    tm = min(tm, T)

    def body(dx_ref, y_ref, g_ref, gate_ref, dy_ref, dgate_ref, dg_ref, dbias_ref):
        _acc_init(pl.program_id(0), dgate_ref, dg_ref, dbias_ref)
        yv = y_ref[...]
        dxv = dx_ref[...]
        r = lax.rsqrt(_rowmean(yv * yv) + EPS)
        yn = yv * r
        gv = g_ref[...]
        gt = gate_ref[...]
        dgate_ref[...] += _colsum(dxv * (yn * gv))
        dg_ref[...] += _colsum(dxv * gt * yn)
        dyn = dxv * gt * gv
        dy = r * (dyn - yn * _rowmean(dyn * yn))
        dbias_ref[...] += _colsum(dy)
        dy_ref[...] = dy.astype(BF16)

    return pl.pallas_call(
        body, grid=(T // tm,), in_specs=[_row(tm, D), _row(tm, D), _full(1, D), _full(1, D)],
        out_specs=[_row(tm, D), _full(1, D), _full(1, D), _full(1, D)],
        out_shape=[jax.ShapeDtypeStruct((T, D), BF16)] + [jax.ShapeDtypeStruct((1, D), F32)] * 3, name=name,
        compiler_params=_cp("arbitrary"))(dx, y, g, gate)


def prenorm_bwd(dh, x, dres, g, sc, *, name, tm=512):
    T, D = x.shape
    tm = min(tm, T)

    def body(dh_ref, x_ref, dres_ref, g_ref, sc_ref, dx_ref, dsh_ref, dsc_ref, dg_ref):
        _acc_init(pl.program_id(0), dsh_ref, dsc_ref, dg_ref)
        xv = x_ref[...]
        dhv = dh_ref[...]
        r = lax.rsqrt(_rowmean(xv * xv) + EPS)
        xn = xv * r
        gv = g_ref[...]
        one_sc = 1.0 + sc_ref[...]
        dsh_ref[...] += _colsum(dhv)
        dsc_ref[...] += _colsum(dhv * (xn * gv))
        dg_ref[...] += _colsum(dhv * one_sc * xn)
        dxn = dhv * one_sc * gv
        dx_ref[...] = dres_ref[...] + r * (dxn - xn * _rowmean(dxn * xn))

    return pl.pallas_call(
        body, grid=(T // tm,), in_specs=[_row(tm, D), _row(tm, D), _row(tm, D), _full(1, D), _full(1, D)],
        out_specs=[_row(tm, D), _full(1, D), _full(1, D), _full(1, D)],
        out_shape=[jax.ShapeDtypeStruct((T, D), F32)] + [jax.ShapeDtypeStruct((1, D), F32)] * 3, name=name,
        compiler_params=_cp("arbitrary"))(dh, x, dres, g, sc)


HALO = 16


def _shift_helpers():
    rid = lax.broadcasted_iota(jnp.int32, (SUB, LANE), 0)

    def down(cur, prev, k):
        return pltpu.roll(jnp.where(rid >= SUB - k, prev, cur), k, 0)

    def up(cur, nxt, k):
        return pltpu.roll(jnp.where(rid < k, nxt, cur), SUB - k, 0)

    return down, up


def _ffn_sides(c, nb, wa_ref, wb_ref, ba_ref, bb_ref):
    cols = slice(c * LANE, (c + 1) * LANE)
    return [(cols, [wa_ref[k:k + 1, cols] for k in range(FFN_W)], ba_ref[:, cols]),
            (slice(nb + c * LANE, nb + (c + 1) * LANE), [wb_ref[k:k + 1, cols] for k in range(FFN_W)],
             bb_ref[:, cols])]


def _ffn_specs(tm, nb, hb, idx):
    return [pl.BlockSpec((tm, 2 * nb), lambda jc, i: (idx(i), jc)),
            pl.BlockSpec((HALO, 2 * nb), lambda jc, i: (jnp.maximum(idx(i) * hb - 1, 0), jc)),
            pl.BlockSpec((FFN_W, nb), lambda jc, i: (0, jc)),
            pl.BlockSpec((FFN_W, nb), lambda jc, i: (0, jc + 2)),
            pl.BlockSpec((1, nb), lambda jc, i: (0, jc)),
            pl.BlockSpec((1, nb), lambda jc, i: (0, jc + 2))]


def ffn_act(u0p, dw_w, dw_b, *, name, tm=256):
    T, W = u0p.shape
    nb = W // 4
    tm = min(tm, T)
    unroll = 4
    rows16 = 2 * SUB

    def body(u_ref, halo_ref, wa_ref, wb_ref, ba_ref, bb_ref, z_ref, ab_ref):
        i = pl.program_id(1)
        down, _ = _shift_helpers()
        for c in range(nb // LANE):
            cols = slice(c * LANE, (c + 1) * LANE)
            side = _ffn_sides(c, nb, wa_ref, wb_ref, ba_ref, bb_ref)

            def rows(j, prev):
                prev = list(prev)
                for m in range(unroll):
                    r0 = pl.multiple_of((j * unroll + m) * rows16, rows16)
                    x = [u_ref[pl.ds(r0, rows16), cs].astype(F32) for cs, _, _ in side]
                    conv = [[None, None], [None, None]]
                    for hf in range(2):
                        for n, (_, w, b) in enumerate(side):
                            cur = x[n][hf * SUB:(hf + 1) * SUB, :]
                            conv[n][hf] = b + w[2] * cur + w[1] * down(cur, prev[n], 1) + w[0] * down(cur, prev[n], 2)
                            prev[n] = cur
                    a, b = [jnp.concatenate(conv[n], axis=0) for n in range(2)]
                    z_ref[pl.ds(r0, rows16), cols] = (_silu(a) * b).astype(BF16)
                    ab_ref[pl.ds(r0, rows16), side[0][0]] = a.astype(BF16)
                    ab_ref[pl.ds(r0, rows16), side[1][0]] = b.astype(BF16)
                return tuple(prev)

            first = [jnp.where(i == 0, 0.0, halo_ref[:, cs].astype(F32)[SUB:2 * SUB, :]) for cs, _, _ in side]
            lax.fori_loop(0, tm // (rows16 * unroll), rows, tuple(first))

    return pl.pallas_call(
        body, grid=(2, T // tm), in_specs=_ffn_specs(tm, nb, tm // HALO, lambda i: i),
        out_specs=[pl.BlockSpec((tm, nb), lambda jc, i: (i, jc)), pl.BlockSpec((tm, 2 * nb), lambda jc, i: (i, jc))],
        out_shape=[jax.ShapeDtypeStruct((T, 2 * nb), BF16), jax.ShapeDtypeStruct((T, W), BF16)], name=name,
        compiler_params=_cp("parallel", "arbitrary"))(u0p, u0p, dw_w, dw_w, dw_b, dw_b)


def ffn_act_bwd(dz, u0p, ab, dw_w, *, name, tm=256):
    T, W = u0p.shape
    nb = W // 4
    tm = min(tm, T)
    nt = T // tm
    unroll = 4
    rows16 = 2 * SUB
    n_it = tm // (rows16 * unroll)

    def body(dz_ref, u_ref, ab_ref, wa_ref, wb_ref, du0_ref, dw_ref, carry):
        i = pl.program_id(1)
        _acc_init(i, dw_ref)
        _, up = _shift_helpers()
        for c in range(nb // LANE):
            cols = slice(c * LANE, (c + 1) * LANE)
            side = [(cols, [wa_ref[k:k + 1, cols] for k in range(FFN_W)]),
                    (slice(nb + c * LANE, nb + (c + 1) * LANE), [wb_ref[k:k + 1, cols] for k in range(FFN_W)])]

            def rows(j, st):
                nxt, acc = list(st[0:2]), list(st[2:10])
                for m in range(unroll):
                    r0 = pl.multiple_of(((n_it - 1 - j) * unroll + unroll - 1 - m) * rows16, rows16)
                    dzv = dz_ref[pl.ds(r0, rows16), cols].astype(F32)
                    a, b = [ab_ref[pl.ds(r0, rows16), cs].astype(F32) for cs, _ in side]
                    x = [u_ref[pl.ds(r0, rows16), cs].astype(F32) for cs, _ in side]
                    sa = _sig(a)
                    d16 = [dzv * b * (sa * (1.0 + a * (1.0 - sa))), dzv * (a * sa)]
                    out = [[None, None], [None, None]]
                    for hf in (1, 0):
                        half = slice(hf * SUB, (hf + 1) * SUB)
                        for n in range(2):
                            w = side[n][1]
                            d = d16[n][half, :]
                            u = x[n][half, :]
                            up1, up2 = up(d, nxt[n], 1), up(d, nxt[n], 2)
                            acc[4 * n + 0] = acc[4 * n + 0] + up2 * u
                            acc[4 * n + 1] = acc[4 * n + 1] + up1 * u
                            acc[4 * n + 2] = acc[4 * n + 2] + d * u
                            acc[4 * n + 3] = acc[4 * n + 3] + d
                            out[n][hf] = w[2] * d + w[1] * up1 + w[0] * up2
                            nxt[n] = d
                    for n in range(2):
                        du0_ref[pl.ds(r0, rows16), side[n][0]] = jnp.concatenate(out[n], axis=0).astype(BF16)
                return (*nxt, *acc)

            init = [jnp.where(i == 0, 0.0, carry[:, cs]) for cs, _ in side] + [jnp.zeros((SUB, LANE), F32)] * 8
            st = lax.fori_loop(0, n_it, rows, tuple(init))
            for n in range(2):
                carry[:, side[n][0]] = st[n]
                for k in range(4):
                    dw_ref[k, :, side[n][0]] += st[2 + 4 * n + k]

        @pl.when(i == nt - 1)
        def _():
            for k in range(4):
                dw_ref[k, 0:1, :] = _colsum(dw_ref[k])

    rev = lambda i: nt - 1 - i
    wide = pl.BlockSpec((tm, 2 * nb), lambda jc, i: (rev(i), jc))
    return pl.pallas_call(
        body, grid=(2, nt),
        in_specs=[pl.BlockSpec((tm, nb), lambda jc, i: (rev(i), jc)), wide, wide,
                  pl.BlockSpec((FFN_W, nb), lambda jc, i: (0, jc)), pl.BlockSpec((FFN_W, nb), lambda jc, i: (0, jc + 2))],
        out_specs=[wide, pl.BlockSpec((4, SUB, 2 * nb), lambda jc, i: (0, 0, jc))],
        out_shape=[jax.ShapeDtypeStruct((T, W), BF16), jax.ShapeDtypeStruct((4, SUB, W), F32)],
        scratch_shapes=[pltpu.VMEM((SUB, 2 * nb), F32)], name=name,
        compiler_params=_cp("parallel", "arbitrary"))(dz, u0p, ab, dw_w, dw_w)


CHALO = 32
CCOL = 256


def _phase_copies(buf, shifted, tm):
    n = tm + CHALO - SUB
    for p in range(1, SUB):
        shifted[p - 1, 0:n, :] = buf[p:p + n, :]


def _shifted(buf, shifted, r, tm, c0):
    m, p = divmod(r, SUB)
    src = buf if p == 0 else shifted.at[p - 1]
    return src[m * SUB:m * SUB + tm, c0:c0 + CCOL]


def conv_act(u, dw_w, dw_b, ln_g, ln_b, *, name, tm=128):
    T, D2 = u.shape
    D = D2 // 2
    tm = min(tm, T)
    hb = tm // CHALO

    def body(u_ref, halo_ref, w_ref, b_ref, g_ref, be_ref, s_ref, cv_ref, gbuf, gsh):
        i = pl.program_id(0)
        hv = halo_ref[...]
        gbuf[0:CHALO, :] = jnp.where(i == 0, 0.0, hv[:, 0:D] * _sig(hv[:, D:D2]))
        uv = u_ref[...]
        gbuf[CHALO:CHALO + tm, :] = uv[:, 0:D] * _sig(uv[:, D:D2])
        _phase_copies(gbuf, gsh, tm)
        for c0 in range(0, D, CCOL):
            acc = jnp.zeros((tm, CCOL), F32) + b_ref[:, c0:c0 + CCOL]
            for k in range(CONV_W):
                acc = acc + w_ref[k:k + 1, c0:c0 + CCOL] * _shifted(gbuf, gsh, CHALO - (CONV_W - 1) + k, tm, c0)
            cv_ref[:, c0:c0 + CCOL] = acc
        cv = cv_ref[...]
        mu = _rowmean(cv)
        xc = cv - mu
        nh = xc * lax.rsqrt(_rowmean(xc * xc) + EPS)
        s_ref[...] = _silu(nh * g_ref[...] + be_ref[...]).astype(BF16)

    return pl.pallas_call(
        body, grid=(T // tm,),
        in_specs=[_row(tm, D2), pl.BlockSpec((CHALO, D2), lambda i: (jnp.maximum(i * hb - 1, 0), 0)),
                  _full(CONV_W, D), _full(1, D), _full(1, D), _full(1, D)],
        out_specs=[_row(tm, D), _row(tm, D)],
        out_shape=[jax.ShapeDtypeStruct((T, D), BF16), jax.ShapeDtypeStruct((T, D), F32)],
        scratch_shapes=[pltpu.VMEM((tm + CHALO, D), F32), pltpu.VMEM((SUB - 1, tm + CHALO, D), F32)], name=name,
        compiler_params=_cp("arbitrary"))(u, u, dw_w, dw_b, ln_g, ln_b)


def conv_norm_bwd(ds, cv, ln_g, ln_b, *, name, tm=512):
    T, D = cv.shape
    tm = min(tm, T)

    def body(ds_ref, cv_ref, g_ref, be_ref, dcv_ref, dg_ref, dbe_ref, dcb_ref):
        _acc_init(pl.program_id(0), dg_ref, dbe_ref, dcb_ref)
        cv_ = cv_ref[...]
        mu = _rowmean(cv_)
        xc = cv_ - mu
        rstd = lax.rsqrt(_rowmean(xc * xc) + EPS)
        nh = xc * rstd
        gv = g_ref[...]
        dln = ds_ref[...] * _dsilu(nh * gv + be_ref[...])
        dg_ref[...] += _colsum(dln * nh)
        dbe_ref[...] += _colsum(dln)
        dnh = dln * gv
        dcv = rstd * (dnh - _rowmean(dnh) - nh * _rowmean(dnh * nh))
        dcb_ref[...] += _colsum(dcv)
        dcv_ref[...] = dcv

    return pl.pallas_call(
        body, grid=(T // tm,), in_specs=[_row(tm, D), _row(tm, D), _full(1, D), _full(1, D)],
        out_specs=[_row(tm, D), _full(1, D), _full(1, D), _full(1, D)],
        out_shape=[jax.ShapeDtypeStruct((T, D), F32)] + [jax.ShapeDtypeStruct((1, D), F32)] * 3, name=name,
        compiler_params=_cp("arbitrary"))(ds, cv, ln_g, ln_b)


def conv_glu_bwd(dcv, u, dw_w, *, name, tm=128):
    T, D2 = u.shape
    D = D2 // 2
    tm = min(tm, T)
    nt = T // tm
    hb = tm // CHALO

    def body(dcv_ref, dnext_ref, u_ref, w_ref, du_ref, dw_ref, dbin_ref, dbuf, dsh):
        i = pl.program_id(0)
        _acc_init(i, dw_ref, dbin_ref)
        uv = u_ref[...]
        av = uv[:, 0:D]
        sg = _sig(uv[:, D:D2])
        glu = av * sg
        dbuf[0:tm, :] = dcv_ref[...]
        dbuf[tm:tm + CHALO, :] = jnp.where(i == nt - 1, 0.0, dnext_ref[...])
        _phase_copies(dbuf, dsh, tm)
        for c0 in range(0, D, CCOL):
            glu_c = glu[:, c0:c0 + CCOL]
            acc = jnp.zeros((tm, CCOL), F32)
            for k in range(CONV_W):
                moved = _shifted(dbuf, dsh, CONV_W - 1 - k, tm, c0)
                dw_ref[k:k + 1, c0:c0 + CCOL] += _colsum(moved * glu_c)
                acc = acc + w_ref[k:k + 1, c0:c0 + CCOL] * moved
            a_c = av[:, c0:c0 + CCOL]
            s_c = sg[:, c0:c0 + CCOL]
            da = acc * s_c
            dgt = acc * a_c * s_c * (1.0 - s_c)
            dbin_ref[:, c0:c0 + CCOL] += _colsum(da)
            dbin_ref[:, D + c0:D + c0 + CCOL] += _colsum(dgt)
            du_ref[:, c0:c0 + CCOL] = da.astype(BF16)
            du_ref[:, D + c0:D + c0 + CCOL] = dgt.astype(BF16)

    return pl.pallas_call(
        body, grid=(nt,),
        in_specs=[_row(tm, D), pl.BlockSpec((CHALO, D), lambda i: (jnp.minimum((i + 1) * hb, T // CHALO - 1), 0)),
                  _row(tm, D2), _full(CONV_W, D)],
        out_specs=[_row(tm, D2), _full(CHALO, D), _full(1, D2)],
        out_shape=[jax.ShapeDtypeStruct((T, D2), BF16), jax.ShapeDtypeStruct((CHALO, D), F32),
                   jax.ShapeDtypeStruct((1, D2), F32)],
        scratch_shapes=[pltpu.VMEM((tm + CHALO, D), F32), pltpu.VMEM((SUB - 1, tm + CHALO, D), F32)],
        name=name, compiler_params=_cp("arbitrary"))(dcv, dcv, u, dw_w)


HB = 8


def _lb0(lg_ref):
    l0, l1, l2 = lg_ref[0:1, :], lg_ref[1:2, :], lg_ref[2:3, :]
    m = jnp.maximum(jnp.maximum(l0, l1), l2)
    e0 = jnp.exp(l0 - m)
    return e0 / (e0 + jnp.exp(l1 - m) + jnp.exp(l2 - m))


def _mm_exact(m01, x):
    hi = x.astype(BF16)
    r1 = x - hi.astype(F32)
    mid = r1.astype(BF16)
    lo = (r1 - mid.astype(F32)).astype(BF16)
    return _dot(m01, hi) + _dot(m01, mid) + _dot(m01, lo)


def _block_tri(tm):
    r = jnp.arange(tm)[:, None]
    c = jnp.arange(tm)[None, :]
    same = (r // BLK) == (c // BLK)
    return (same & (c <= r)).astype(BF16), (same & (c >= r)).astype(BF16)


def _halves(x):
    return [x[0:SUB, :], x[SUB:BLK, :]]


def _live_halves(s):
    return ([(0, s)] if s < SUB else []) + [(1, max(s - SUB, 0))]


def _const_spec(shape):
    return pl.BlockSpec(shape, lambda h, i: (0, 0))


def _hgrn_specs(H, hb, tm, idx):
    g = H // hb
    return [pl.BlockSpec((tm, hb * HEAD), lambda h, i: (idx(i), h)),
            pl.BlockSpec((tm, hb * HEAD), lambda h, i: (idx(i), g + h)),
            pl.BlockSpec((tm, hb * HEAD), lambda h, i: (idx(i), 2 * g + h)),
            pl.BlockSpec((3, hb * HEAD), lambda h, i: (0, h))]


def hgrn_scan(proj, lb_logits, *, name, tm=128):
    T = proj.shape[0]
    H = proj.shape[1] // (4 * HEAD)
    hb = min(HB, H)
    tm = min(tm, T)
    nt = T // tm
    nblk = tm // BLK
    tril, _ = _block_tri(tm)
    heads = [slice(hh * HEAD, (hh + 1) * HEAD) for hh in range(hb)]

    def body(qp_ref, fz_ref, v_ref, lg_ref, tril_ref, o_ref, st_ref, S_ref, q_s, k_s, b_s):
        @pl.when(pl.program_id(1) == 0)
        def _():
            S_ref[...] = jnp.zeros_like(S_ref)

        st_ref[...] = S_ref[...]
        lb = _lb0(lg_ref)
        f = lb + (1.0 - lb) * _sig(fz_ref[...])
        q_s[...] = _silu(qp_ref[...])
        k_s[...] = 1.0 - f
        b_s[...] = _mm_exact(tril_ref[...], jnp.log(f))
        rows = lax.broadcasted_iota(jnp.int32, (BLK, HEAD), 0)
        S = [S_ref[hh] for hh in range(hb)]
        for nb in range(nblk):
            blk = slice(nb * BLK, (nb + 1) * BLK)
            last = slice(nb * BLK + BLK - 1, nb * BLK + BLK)
            qb = [q_s[blk, c] for c in heads]
            bb = [b_s[blk, c] for c in heads]
            o = [_dot_nt((qb[hh] * jnp.exp(bb[hh])).astype(BF16), S[hh].astype(BF16)) for hh in range(hb)]
            for hh, c in enumerate(heads):
                bc = b_s[last, c]
                kd = k_s[blk, c] * jnp.exp(bc - bb[hh])
                S[hh] = S[hh] * jnp.exp(bc) + _dot_tn(v_ref[blk, c].astype(BF16), kd.astype(BF16))
            for s in range(BLK):
                r = slice(nb * BLK + s, nb * BLK + s + 1)
                for hh, c in enumerate(heads):
                    dec = jnp.exp(jnp.where(rows >= s, bb[hh] - b_s[r, c], NEG))
                    a = jnp.sum(qb[hh] * k_s[r, c] * dec, axis=-1, keepdims=True)
                    o[hh] = o[hh] + a * v_ref[r, c]
            for hh, c in enumerate(heads):
                o_ref[blk, c] = o[hh]
        for hh in range(hb):
            S_ref[hh] = S[hh]

    return pl.pallas_call(
        body, grid=(H // hb, nt),
        in_specs=_hgrn_specs(H, hb, tm, lambda i: i) + [_const_spec((tm, tm))],
        out_specs=[pl.BlockSpec((tm, hb * HEAD), lambda h, i: (i, h)),
                   pl.BlockSpec((None, hb, HEAD, HEAD), lambda h, i: (i, h, 0, 0))],
        out_shape=[jax.ShapeDtypeStruct((T, H * HEAD), F32), jax.ShapeDtypeStruct((nt, H, HEAD, HEAD), F32)],
        scratch_shapes=[pltpu.VMEM((hb, HEAD, HEAD), F32)] + [pltpu.VMEM((tm, hb * HEAD), F32)] * 3, name=name,
        compiler_params=_cp("parallel", "arbitrary"))(proj, proj, proj, lb_logits, tril)


def hgrn_scan_bwd(proj, lb_logits, states, do, *, name, tm=128):
    T = proj.shape[0]
    H = proj.shape[1] // (4 * HEAD)
    hb = min(HB, H)
    tm = min(tm, T)
    nt = T // tm
    nblk = tm // BLK
    tril, triu = _block_tri(tm)
    sel = (jnp.arange(BLK * SUB)[None, :] // SUB == jnp.arange(BLK)[:, None]).astype(BF16)
    heads = [slice(hh * HEAD, (hh + 1) * HEAD) for hh in range(hb)]

    def body(qp_ref, fz_ref, v_ref, lg_ref, st_ref, do_ref, tril_ref, triu_ref, sel_ref, d3_ref, dlb_ref,
             dS_ref, Sb_ref, q_s, k_s, b_s, dq_s, dk_s, dv_s, db_s, pk_s, pv_s):
        i = pl.program_id(1)

        @pl.when(i == 0)
        def _():
            dS_ref[...] = jnp.zeros_like(dS_ref)
            dlb_ref[...] = jnp.zeros_like(dlb_ref)

        lb = _lb0(lg_ref)
        qp = qp_ref[...]
        sg = _sig(fz_ref[...])
        f = lb + (1.0 - lb) * sg
        q_s[...] = _silu(qp)
        k_s[...] = 1.0 - f
        b_s[...] = _mm_exact(tril_ref[...], jnp.log(f))
        rows = lax.broadcasted_iota(jnp.int32, (SUB, HEAD), 0)
        rows1 = lax.broadcasted_iota(jnp.int32, (SUB, 1), 0)

        S = [st_ref[hh] for hh in range(hb)]
        for nb in range(nblk):
            blk = slice(nb * BLK, (nb + 1) * BLK)
            last = slice(nb * BLK + BLK - 1, nb * BLK + BLK)
            for hh, c in enumerate(heads):
                Sb_ref[nb * hb + hh] = S[hh]
                if nb < nblk - 1:
                    bc = b_s[last, c]
                    kd = k_s[blk, c] * jnp.exp(bc - b_s[blk, c])
                    S[hh] = S[hh] * jnp.exp(bc) + _dot_tn(v_ref[blk, c].astype(BF16), kd.astype(BF16))

        dS = [dS_ref[hh] for hh in range(hb)]
        for nb in reversed(range(nblk)):
            blk = slice(nb * BLK, (nb + 1) * BLK)
            last = slice(nb * BLK + BLK - 1, nb * BLK + BLK)
            qb, kb, bb, dob, dq, dbc, ebc = [], [], [], [], [], [], []
            for hh, c in enumerate(heads):
                S0 = Sb_ref[nb * hb + hh]
                qb.append(q_s[blk, c])
                kb.append(k_s[blk, c])
                bb.append(b_s[blk, c])
                dob.append(do_ref[blk, c])
                bc = b_s[last, c]
                eb = jnp.exp(bb[hh])
                ekd = jnp.exp(bc - bb[hh])
                ebc.append(jnp.exp(bc))
                dS16 = dS[hh].astype(BF16)
                dob16 = dob[hh].astype(BF16)
                dq.append(_dot(dob16, S0.astype(BF16)) * eb)
                dki = _dot(v_ref[blk, c].astype(BF16), dS16) * ekd
                dk_s[blk, c] = dki
                dv_s[blk, c] = _dot_nt((kb[hh] * ekd).astype(BF16), dS16)
                dbc.append(_colsum(dS[hh] * S0) * ebc[hh] + _colsum(kb[hh] * dki))
                dS[hh] = dS[hh] * ebc[hh] + _dot_tn(dob16, (qb[hh] * eb).astype(BF16))
            qh, bh, doh, dqh = [[_halves(t[hh]) for hh in range(hb)] for t in (qb, bb, dob, dq)]
            for s in range(BLK):
                r = slice(nb * BLK + s, nb * BLK + s + 1)
                for hh, c in enumerate(heads):
                    ks = k_s[r, c]
                    pk, pv = None, None
                    for hf, lo in _live_halves(s):
                        diff = bh[hh][hf] - b_s[r, c]
                        dec = jnp.exp(diff if lo == 0 else jnp.where(rows >= lo, diff, NEG))
                        w = qh[hh][hf] * dec
                        a = jnp.sum(w * ks, axis=-1, keepdims=True)
                        da = jnp.sum(doh[hh][hf] * v_ref[r, c], axis=-1, keepdims=True)
                        if lo:
                            da = jnp.where(rows1 >= lo, da, 0.0)
                        dqh[hh][hf] = dqh[hh][hf] + (da * ks) * dec
                        pk = da * w if pk is None else pk + da * w
                        pv = a * doh[hh][hf] if pv is None else pv + a * doh[hh][hf]
                    pk_s[hh, s * SUB:(s + 1) * SUB, :] = pk
                    pv_s[hh, s * SUB:(s + 1) * SUB, :] = pv
            for hh, c in enumerate(heads):
                khi, klo = _split2(pk_s[hh])
                dk_s[blk, c] += _dot(sel_ref[...], khi) + _dot(sel_ref[...], klo)
                dv_s[blk, c] += _dot(sel_ref[...], pv_s[hh].astype(BF16))
                dq[hh] = jnp.concatenate(dqh[hh], axis=0)
                dq_s[blk, c] = dq[hh]
                db_s[blk, c] = qb[hh] * dq[hh] - kb[hh] * dk_s[blk, c]
                db_s[last, c] += dbc[hh]
        for hh in range(hb):
            dS_ref[hh] = dS[hh]

        dlf = _mm_exact(triu_ref[...], db_s[...])
        df = dlf / f - dk_s[...]
        d3_ref[0] = (dq_s[...] * _dsilu(qp)).astype(BF16)
        d3_ref[1] = (df * (1.0 - lb) * sg * (1.0 - sg)).astype(BF16)
        d3_ref[2] = dv_s[...].astype(BF16)
        dlb_ref[...] += _colsum(df * (1.0 - sg))

    rev = lambda i: nt - 1 - i
    out_blk = pl.BlockSpec((tm, hb * HEAD), lambda h, i: (rev(i), h))
    return pl.pallas_call(
        body, grid=(H // hb, nt),
        in_specs=_hgrn_specs(H, hb, tm, rev) + [pl.BlockSpec((None, hb, HEAD, HEAD), lambda h, i: (rev(i), h, 0, 0)),
                                                out_blk, _const_spec((tm, tm)), _const_spec((tm, tm)),
                                                _const_spec((BLK, BLK * SUB))],
        out_specs=[pl.BlockSpec((3, tm, hb * HEAD), lambda h, i: (0, rev(i), h)),
                   pl.BlockSpec((1, hb * HEAD), lambda h, i: (0, h))],
        out_shape=[jax.ShapeDtypeStruct((3, T, H * HEAD), BF16), jax.ShapeDtypeStruct((1, H * HEAD), F32)],
        scratch_shapes=[pltpu.VMEM((hb, HEAD, HEAD), F32), pltpu.VMEM((nblk * hb, HEAD, HEAD), F32)]
        + [pltpu.VMEM((tm, hb * HEAD), F32)] * 7 + [pltpu.VMEM((hb, BLK * SUB, HEAD), F32)] * 2, name=name,
        compiler_params=_cp("parallel", "arbitrary"))(proj, proj, proj, lb_logits, states, do, tril, triu, sel)


def hgrn_gate(o, proj, gn, *, name, tm=512):
    T, D = o.shape
    H = D // HEAD
    tm = min(tm, T)

    def body(o_ref, gp_ref, gn_ref, og_ref):
        gn_ = gn_ref[...]
        for h in range(H):
            c = slice(h * HEAD, (h + 1) * HEAD)
            oh = o_ref[:, c]
            r = lax.rsqrt(_rowmean(oh * oh) + EPS)
            og_ref[:, c] = ((oh * r) * gn_ * _silu(gp_ref[:, c])).astype(BF16)

    return pl.pallas_call(
        body, grid=(T // tm,),
        in_specs=[_row(tm, D), pl.BlockSpec((tm, D), lambda i: (i, 3)), _full(1, HEAD)],
        out_specs=_row(tm, D), out_shape=jax.ShapeDtypeStruct((T, D), BF16), name=name,
        compiler_params=_cp("parallel"))(o, proj, gn)


def hgrn_gate_bwd(dog, o, proj, gn, *, name, tm=512):
    T, D = o.shape
    H = D // HEAD
    tm = min(tm, T)

    def body(dog_ref, o_ref, gp_ref, gn_ref, do_ref, dgp_ref, dgn_ref):
        _acc_init(pl.program_id(0), dgn_ref)
        gn_ = gn_ref[...]
        for h in range(H):
            c = slice(h * HEAD, (h + 1) * HEAD)
            oh = o_ref[:, c]
            gp = gp_ref[:, c]
            dg = dog_ref[:, c]
            r = lax.rsqrt(_rowmean(oh * oh) + EPS)
            on = oh * r
            dgp_ref[:, c] = (dg * (on * gn_) * _dsilu(gp)).astype(BF16)
            don = dg * _silu(gp)
            dgn_ref[...] += _colsum(don * on)
            dn = don * gn_
            do_ref[:, c] = r * (dn - on * _rowmean(dn * on))

    return pl.pallas_call(
        body, grid=(T // tm,),
        in_specs=[_row(tm, D), _row(tm, D), pl.BlockSpec((tm, D), lambda i: (i, 3)), _full(1, HEAD)],
        out_specs=[_row(tm, D), _row(tm, D), _full(1, HEAD)],
        out_shape=[jax.ShapeDtypeStruct((T, D), F32), jax.ShapeDtypeStruct((T, D), BF16),
                   jax.ShapeDtypeStruct((1, HEAD), F32)], name=name,
        compiler_params=_cp("arbitrary"))(dog, o, proj, gn)


def _split2(x):
    hi = x.astype(BF16)
    return hi, (x - hi.astype(F32)).astype(BF16)


def ada_mod(c_all, ada_w, *, name):
    L, D, N = ada_w.shape
    B = c_all.shape[0]

    def body(c_ref, w_ref, o_ref):
        chi, clo = _split2(_silu(c_ref[...]))
        whi, wlo = _split2(w_ref[...])
        o_ref[...] = _dot(chi, whi) + _dot(chi, wlo) + _dot(clo, whi)

    return pl.pallas_call(
        body, grid=(L,), in_specs=[_full(B, D), pl.BlockSpec((None, D, N), lambda l: (l, 0, 0))],
        out_specs=pl.BlockSpec((None, B, N), lambda l: (l, 0, 0)),
        out_shape=jax.ShapeDtypeStruct((L, B, N), F32), name=name, compiler_params=_cp("parallel"))(c_all, ada_w)


def ada_wgrad(c_all_t, dmod, *, name, tr=256):
    D, B = c_all_t.shape
    L, _, N = dmod.shape
    tr = min(tr, D)

    def body(c_ref, d_ref, o_ref):
        cond = _silu(c_ref[...])
        acc = cond[:, 0:1] * d_ref[0:1, :]
        for b in range(1, B):
            acc = acc + cond[:, b:b + 1] * d_ref[b:b + 1, :]
        o_ref[...] = acc

    return pl.pallas_call(
        body, grid=(L, D // tr),
        in_specs=[pl.BlockSpec((tr, B), lambda l, r: (r, 0)), pl.BlockSpec((None, B, N), lambda l, r: (l, 0, 0))],
        out_specs=pl.BlockSpec((None, tr, N), lambda l, r: (l, r, 0)),
        out_shape=jax.ShapeDtypeStruct((L, D, N), F32), name=name,
        compiler_params=_cp("parallel", "parallel"))(c_all_t, dmod)


def sum_devices(parts, *, name):
    n, R, C = parts.shape

    def body(p_ref, o_ref):
        acc = p_ref[0]
        for d in range(1, n):
            acc = acc + p_ref[d]
        o_ref[...] = acc

    return pl.pallas_call(body, in_specs=[VMEM_SPEC], out_specs=VMEM_SPEC,
                          out_shape=jax.ShapeDtypeStruct((R, C), F32), name=name)(parts)


def lb_logits_grad(lb_logits, dlb, *, name):
    def body(lg_ref, d_ref, o_ref):
        l0, l1, l2 = lg_ref[0:1, :], lg_ref[1:2, :], lg_ref[2:3, :]
        m = jnp.maximum(jnp.maximum(l0, l1), l2)
        e0, e1, e2 = jnp.exp(l0 - m), jnp.exp(l1 - m), jnp.exp(l2 - m)
        z = e0 + e1 + e2
        p0, p1, p2 = e0 / z, e1 / z, e2 / z
        g = d_ref[...] * p0
        o_ref[0:1, :] = g * (1.0 - p0)
        o_ref[1:2, :] = -g * p1
        o_ref[2:3, :] = -g * p2

    return pl.pallas_call(body, in_specs=[VMEM_SPEC, VMEM_SPEC], out_specs=VMEM_SPEC,
                          out_shape=jax.ShapeDtypeStruct(lb_logits.shape, F32), name=name)(lb_logits, dlb)


def adamw(w, g, m, v, *, name, tr=256, after=None):
    R, C = w.shape
    tr = _tile(R, tr)
    deps = [] if after is None else [after]

    def body(w_ref, g_ref, m_ref, v_ref, *rest):
        d_ref, nm_ref, nv_ref = rest[len(deps):]
        gv = g_ref[...]
        nm = ADAM_B1 * m_ref[...] + (1.0 - ADAM_B1) * gv
        nv = ADAM_B2 * v_ref[...] + (1.0 - ADAM_B2) * (gv * gv)
        m_hat = nm / (1.0 - ADAM_B1 ** ADAM_STEP)
        v_hat = nv / (1.0 - ADAM_B2 ** ADAM_STEP)
        d_ref[...] = -ADAM_LR * (m_hat / (jnp.sqrt(v_hat) + ADAM_EPS) + ADAM_WD * w_ref[...])
        nm_ref[...] = nm
        nv_ref[...] = nv

    spec = pl.BlockSpec((tr, C), lambda i: (i, 0))
    return pl.pallas_call(
        body, grid=(R // tr,), in_specs=[spec] * 4 + [pl.BlockSpec(memory_space=pl.ANY)] * len(deps), out_specs=[spec] * 3,
        out_shape=[jax.ShapeDtypeStruct((R, C), F32)] * 3, name=name,
        compiler_params=_cp("parallel"))(w, g, m, v, *deps)


def _place():
    return lax.axis_index("x"), lax.axis_index("y"), lax.axis_index("c")


def _flip(v, bit):
    return 1 - v if bit else v


def allgather_devices(v, *, name):
    R, C = v.shape

    def body(v_ref, out_ref, send_sems, recv_sems, local_sem):
        x, y, c = _place()
        me = 4 * x + 2 * y + c
        mine = pltpu.make_async_copy(v_ref, out_ref.at[me], local_sem)
        mine.start()
        sends = []
        for k in range(1, N_DEV):
            peer = (_flip(x, k & 4), _flip(y, k & 2), _flip(c, k & 1))
            cp = pltpu.make_async_remote_copy(src_ref=v_ref, dst_ref=out_ref.at[me], send_sem=send_sems.at[k - 1],
                                              recv_sem=recv_sems.at[k - 1], device_id=peer, device_id_type=MESH)
            cp.start()
            sends.append(cp)
        for k in range(1, N_DEV):
            px, py, pc = _flip(x, k & 4), _flip(y, k & 2), _flip(c, k & 1)
            pltpu.make_async_remote_copy(src_ref=v_ref, dst_ref=out_ref.at[4 * px + 2 * py + pc],
                                         send_sem=send_sems.at[k - 1], recv_sem=recv_sems.at[k - 1],
                                         device_id=(px, py, pc), device_id_type=MESH).wait_recv()
        for cp in sends:
            cp.wait_send()
        mine.wait()

    return pl.pallas_call(
        body, in_specs=[VMEM_SPEC], out_specs=VMEM_SPEC, out_shape=jax.ShapeDtypeStruct((N_DEV, R, C), v.dtype),
        scratch_shapes=[pltpu.SemaphoreType.DMA((N_DEV - 1,)), pltpu.SemaphoreType.DMA((N_DEV - 1,)),
                        pltpu.SemaphoreType.DMA], name=name)(v)


def _other_chips(x, y):
    return [(1 - x, y), (x, 1 - y), (1 - x, 1 - y)]


SEM = pl.BlockSpec(memory_space=pltpu.SEMAPHORE)
DATAFLOW = pltpu.SideEffectType.DATAFLOW_SIDE_EFFECTING


def _chip_copy(buf, a, j, q, c, chips, send_sems, recv_sems):
    px, py = chips[j]
    return pltpu.make_async_remote_copy(src_ref=buf.at[q, c], dst_ref=buf.at[q, c], send_sem=send_sems.at[3 * a + j],
                                        recv_sem=recv_sems.at[3 * a + j], device_id=(px, py, c), device_id_type=MESH)


def allgather_chips_start(bufs, *, name):
    n = len(bufs)

    def body(*refs):
        send_sems, recv_sems = refs[n], refs[n + 1]
        outs = refs[n + 2:2 * n + 2]
        token = refs[2 * n + 2]
        x, y, c = _place()
        chips = _other_chips(x, y)
        for a in range(n):
            for j in range(3):
                _chip_copy(outs[a], a, j, 2 * x + y, c, chips, send_sems, recv_sems).start()
        token[...] = jnp.zeros_like(token)

    res = pl.pallas_call(
        body, name=name, in_specs=[HBM] * n,
        out_specs=(SEM, SEM, *([HBM] * n), VMEM_SPEC),
        out_shape=(pltpu.SemaphoreType.DMA((3 * n,)), pltpu.SemaphoreType.DMA((3 * n,)),
                   *[pltpu.HBM(b.shape, b.dtype) for b in bufs], jax.ShapeDtypeStruct((SUB, LANE), F32)),
        input_output_aliases={a: a + 2 for a in range(n)},
        compiler_params=pltpu.CompilerParams(has_side_effects=DATAFLOW),
    )(*[pltpu.with_memory_space_constraint(b, pltpu.HBM) for b in bufs])
    return res[0], res[1], list(res[2:2 + n]), res[2 + n]


def allgather_chips_wait(send_sems, recv_sems, bufs, after, *, name):
    n = len(bufs)

    def body(*refs):
        ins = refs[:n]
        send_sems, recv_sems = refs[n], refs[n + 1]
        x, y, c = _place()
        chips = _other_chips(x, y)
        for a in range(n):
            for j, (px, py) in enumerate(chips):
                _chip_copy(ins[a], a, j, 2 * x + y, c, chips, send_sems, recv_sems).wait_send()
                _chip_copy(ins[a], a, j, 2 * px + py, c, chips, send_sems, recv_sems).wait_recv()

    return list(pl.pallas_call(
        body, name=name, in_specs=[HBM] * n + [SEM, SEM, pl.BlockSpec(memory_space=pl.ANY)],
        out_specs=[HBM] * n, out_shape=[pltpu.HBM(b.shape, b.dtype) for b in bufs],
        input_output_aliases={a: a for a in range(n)},
        compiler_params=pltpu.CompilerParams(has_side_effects=DATAFLOW),
    )(*bufs, send_sems, recv_sems, after))


def forward_to_sibling(bufs, *, name):
    n = len(bufs)

    def body(*refs):
        outs = refs[n:2 * n]
        send_sems, recv_sems = refs[2 * n:]
        x, y, c = _place()
        chips = _other_chips(x, y)

        def copy(a, j, half, to):
            px, py = chips[j]
            slab = outs[a].at[2 * px + py, half]
            return pltpu.make_async_remote_copy(src_ref=slab, dst_ref=slab, send_sem=send_sems.at[a, j],
                                                recv_sem=recv_sems.at[a, j], device_id=to, device_id_type=MESH)

        sends = [copy(a, j, c, (x, y, 1 - c)) for a in range(n) for j in range(3)]
        for cp in sends:
            cp.start()
        for a in range(n):
            for j in range(3):
                copy(a, j, 1 - c, (x, y, c)).wait_recv()
        for cp in sends:
            cp.wait_send()

    return pl.pallas_call(
        body, in_specs=[HBM] * n, out_specs=[HBM] * n,
        out_shape=[jax.ShapeDtypeStruct(b.shape, b.dtype) for b in bufs],
        input_output_aliases={a: a for a in range(n)},
        scratch_shapes=[pltpu.SemaphoreType.DMA((n, 3)), pltpu.SemaphoreType.DMA((n, 3))], name=name)(*bufs)


def pair_add(g, other, c_idx, *, name, tr=256):
    _, Q, R, C = g.shape
    tr = _tile(R, tr)

    def body(c_ref, g_ref, o_ref, out_ref):
        out_ref[...] = (g_ref[...] + o_ref[...]).astype(BF16)

    return pl.pallas_call(
        body,
        grid_spec=pltpu.PrefetchScalarGridSpec(
            num_scalar_prefetch=1, grid=(Q, R // tr),
            in_specs=[pl.BlockSpec((None, None, tr, C), lambda q, r, c_ref: (c_ref[0], q, r, 0)),
                      pl.BlockSpec((None, tr, C), lambda q, r, c_ref: (q, r, 0))],
            out_specs=pl.BlockSpec((None, tr, C), lambda q, r, c_ref: (q, r, 0))),
        out_shape=jax.ShapeDtypeStruct((Q, R, C), BF16), name=name,
        compiler_params=_cp("parallel", "parallel"))(c_idx, g, other)


def chip_sum(sums, landed, qc_idx, *, name, tr=256):
    _, R, C = sums.shape
    tr = _tile(R, tr)

    def body(qc_ref, own_ref, l_ref, o_ref):
        acc = own_ref[...].astype(F32)
        for k in range(3):
            acc = acc + l_ref[k].astype(F32)
        o_ref[...] = acc

    return pl.pallas_call(
        body,
        grid_spec=pltpu.PrefetchScalarGridSpec(
            num_scalar_prefetch=1, grid=(R // tr,),
            in_specs=[pl.BlockSpec((None, tr, C), lambda r, qc: (qc[0], r, 0)),
                      pl.BlockSpec((3, tr, C), lambda r, qc: (0, r, 0))],
            out_specs=pl.BlockSpec((None, tr, C), lambda r, qc: (qc[1], r, 0))),
        out_shape=jax.ShapeDtypeStruct((2, R, C), F32), name=name,
        compiler_params=_cp("parallel"))(qc_idx, sums, landed)


def half_swap(bufs, *, name):
    n = len(bufs)

    def body(*refs):
        outs = refs[n:2 * n]
        send_sems, recv_sems = refs[2 * n:]
        x, y, c = _place()
        cps = [pltpu.make_async_remote_copy(src_ref=outs[a].at[c], dst_ref=outs[a].at[c], send_sem=send_sems.at[a],
                                            recv_sem=recv_sems.at[a], device_id=(x, y, 1 - c), device_id_type=MESH)
               for a in range(n)]
        for cp in cps:
            cp.start()
        for a in range(n):
            pltpu.make_async_remote_copy(src_ref=outs[a].at[c], dst_ref=outs[a].at[1 - c], send_sem=send_sems.at[a],
                                         recv_sem=recv_sems.at[a], device_id=(x, y, 1 - c),
                                         device_id_type=MESH).wait_recv()
        for cp in cps:
            cp.wait_send()

    return pl.pallas_call(
        body, in_specs=[HBM] * n, out_specs=[HBM] * n,
        out_shape=[jax.ShapeDtypeStruct(b.shape, b.dtype) for b in bufs],
        input_output_aliases={a: a for a in range(n)},
        scratch_shapes=[pltpu.SemaphoreType.DMA((n,)), pltpu.SemaphoreType.DMA((n,))], name=name)(*bufs)


def _chip_copies(src, dst, send_sems, recv_sems):
    x, y, c = _place()
    return [pltpu.make_async_remote_copy(src_ref=src[a].at[2 * px + py], dst_ref=dst[a].at[j],
                                         send_sem=send_sems.at[3 * a + j], recv_sem=recv_sems.at[3 * a + j],
                                         device_id=(px, py, c), device_id_type=MESH)
            for a in range(len(src)) for j, (px, py) in enumerate(_other_chips(x, y))]


def _pair_copies(src, dst, send_sems, recv_sems):
    x, y, c = _place()
    return [pltpu.make_async_remote_copy(src_ref=src[a].at[1 - c], dst_ref=dst[a], send_sem=send_sems.at[a],
                                         recv_sem=recv_sems.at[a], device_id=(x, y, 1 - c), device_id_type=MESH)
            for a in range(len(src))]


def _device_copies(src, dst, send_sems, recv_sems):
    x, y, c = _place()
    mine = src[0].at[4 * x + 2 * y + c]
    return [pltpu.make_async_remote_copy(src_ref=mine, dst_ref=mine, send_sem=send_sems.at[k - 1],
                                         recv_sem=recv_sems.at[k - 1],
                                         device_id=(_flip(x, k & 4), _flip(y, k & 2), _flip(c, k & 1)), device_id_type=MESH)
            for k in range(1, N_DEV)]


def exchange_start(src, landing, copies, n_sems, *, name):
    n, m = len(src), len(src) + len(landing)

    def body(*refs):
        send_sems, recv_sems = refs[m], refs[m + 1]
        for cp in copies(refs[m + 2:m + 2 + n], refs[m + 2 + n:2 * m + 2], send_sems, recv_sems):
            cp.start()
        token = refs[2 * m + 2]
        token[...] = jnp.zeros_like(token)

    res = pl.pallas_call(
        body, name=name, in_specs=[HBM] * m,
        out_specs=(SEM, SEM, *([HBM] * m), VMEM_SPEC),
        out_shape=(pltpu.SemaphoreType.DMA((n_sems,)), pltpu.SemaphoreType.DMA((n_sems,)),
                   *[pltpu.HBM(b.shape, b.dtype) for b in src + landing], jax.ShapeDtypeStruct((SUB, LANE), F32)),
        input_output_aliases={a: a + 2 for a in range(m)},
        compiler_params=pltpu.CompilerParams(has_side_effects=DATAFLOW),
    )(*[pltpu.with_memory_space_constraint(b, pltpu.HBM) for b in src + landing])
    return res[0], res[1], list(res[2:2 + n]), list(res[2 + n:2 + m]), res[2 + m]


def exchange_wait(send_sems, recv_sems, src, landed, copies, after, *, name):
    n, m = len(src), len(src) + len(landed)

    def body(*refs):
        for cp in copies(refs[:n], refs[n:m], refs[m], refs[m + 1]):
            cp.wait_send()
            cp.wait_recv()

    res = pl.pallas_call(
        body, name=name, in_specs=[HBM] * m + [SEM, SEM, pl.BlockSpec(memory_space=pl.ANY)],
        out_specs=[HBM] * m, out_shape=[pltpu.HBM(b.shape, b.dtype) for b in src + landed],
        input_output_aliases={a: a for a in range(m)},
        compiler_params=pltpu.CompilerParams(has_side_effects=DATAFLOW),
    )(*src, *landed, send_sems, recv_sems, after)
    return list(res[:n]), list(res[n:])


def finish_reduce(sums, landed, q, c, tag):
    qc_idx = jnp.stack([q, c]).astype(jnp.int32)
    return [chip_sum(s, l, qc_idx, name=f"grad_chip_sum_{tag}{a}") for a, (s, l) in enumerate(zip(sums, landed))]


def _ffn_forward(x, h, mod, post_g, w_up, w_down, dw_w, dw_b, tag, next_norm=None, tgt=None):
    _, _, gate = mod
    u0 = mm_nn(h, w_up, name=f"{tag}_up", out_dtype=BF16, perm=_ffn_perm)
    z, ab = ffn_act(u0, dw_w, dw_b, name=f"{tag}_act")
    y = mm_nn(z, w_down, name=f"{tag}_down")
    if tgt is None:
        out = post_residual_prenorm(x, y, post_g, gate, *next_norm, name=f"{tag}_post")
    else:
        out = post_residual_loss(x, y, post_g, gate, tgt, name=f"{tag}_post_loss")
    return out, (x, h, u0, ab, z, y)


def _ffn_backward(dx, saved, mod, pre_g, post_g, w_up, w_down, dw_w, dw_b, tag):
    x, h, u0, ab, z, y = saved
    sh, sc, gate = mod
    dy, dgate, dpost, _ = post_bwd(dx, y, post_g, gate, name=f"{tag}_post_bwd")
    dz = mm_nt(dy, w_down, name=f"{tag}_down_dx", out_dtype=BF16)
    g_down = mm_tn(z, dy, name=f"{tag}_down_dw", J=2, block="a", row_chips=2)
    du0, dconv = ffn_act_bwd(dz, u0, ab, dw_w, name=f"{tag}_act_bwd")
    dh = mm_nt(du0, w_up, name=f"{tag}_up_dx", perm=_ffn_perm)
    g_up = mm_tn(h, du0, name=f"{tag}_up_dw", J=4, block="b", perm=_ffn_perm)
    dx_in, dsh, dsc, dpre = prenorm_bwd(dh, x, dx, pre_g, sc, name=f"{tag}_prenorm_bwd")
    nb = u0.shape[1] // 4
    dconv = dconv[:, 0].reshape(4, 2, 2, nb).transpose(0, 2, 1, 3).reshape(4, 4 * nb)
    return dx_in, dict(dsh=dsh, dsc=dsc, dgate=dgate, dpre=dpre, dpost=dpost, g_up=g_up, g_down=g_down,
                       d_dw_w=dconv[0:FFN_W], d_dw_b=dconv[3:4])


def _local_step(x, tgt, mods, P, late_weights=None, grads_ready=None):
    m0, m1 = mods
    h1 = prenorm(x, P["pre_mix_g"][0:1], m0[1], m0[0], name="hgrn_prenorm")
    proj = mm_nn(h1, P["hgrn_w_in"], name="hgrn_in")
    o, states = hgrn_scan(proj, P["hgrn_lb_logits"], name="hgrn_scan")
    og = hgrn_gate(o, proj, P["hgrn_gnorm_g"], name="hgrn_gate")
    y1 = mm_nn(og, P["hgrn_w_out"], name="hgrn_out")
    x1, h_f0 = post_residual_prenorm(x, y1, P["post_mix_g"][0:1], m0[2], P["pre_ffn_g"][0:1], m0[4], m0[3],
                                     name="hgrn_post")
    if late_weights is not None:
        P = {**P, **late_weights(x1)}
    (x2, h3), ffn0 = _ffn_forward(x1, h_f0, m0[3:6], P["post_ffn_g"][0:1], P["ffn_w_up"][0], P["ffn_w_down"][0],
                                  P["ffn_dw_w"][0], P["ffn_dw_b"][0:1], "ffn0",
                                  next_norm=(P["pre_mix_g"][1:2], m1[1], m1[0]))
    u = mm_nn(h3, P["conv_w_in"], name="conv_in", bias=P["conv_b_in"])
    s, cv = conv_act(u, P["conv_dw_w"], P["conv_dw_b"], P["conv_ln_g"], P["conv_ln_b"], name="conv_act")
    y3 = mm_nn(s, P["conv_w_out"], name="conv_out", bias=P["conv_b_out"])
    x3, h_f1 = post_residual_prenorm(x2, y3, P["post_mix_g"][1:2], m1[2], P["pre_ffn_g"][1:2], m1[4], m1[3],
                                     name="conv_post")
    (dx4, lcols), ffn1 = _ffn_forward(x3, h_f1, m1[3:6], P["post_ffn_g"][1:2], P["ffn_w_up"][1], P["ffn_w_down"][1],
                                      P["ffn_dw_w"][1], P["ffn_dw_b"][1:2], "ffn1", tgt=tgt)
    dx3, f1 = _ffn_backward(dx4, ffn1, m1[3:6], P["pre_ffn_g"][1:2], P["post_ffn_g"][1:2], P["ffn_w_up"][1],
                            P["ffn_w_down"][1], P["ffn_dw_w"][1], P["ffn_dw_b"][1:2], "ffn1")
    dy3, dg1_1, dpostmix1, d_b_out = post_bwd(dx3, y3, P["post_mix_g"][1:2], m1[2], name="conv_post_bwd")
    ds = mm_nt(dy3, P["conv_w_out"], name="conv_out_dx")
    g_conv_out = mm_tn(s, dy3, name="conv_out_dw", J=1, block="a", row_chips=4)
    dcv, d_ln_g, d_ln_b, d_dw_b = conv_norm_bwd(ds, cv, P["conv_ln_g"], P["conv_ln_b"], name="conv_norm_bwd")
    du, d_dw_w, d_b_in = conv_glu_bwd(dcv, u, P["conv_dw_w"], name="conv_glu_bwd")
    dh3 = mm_nt(du, P["conv_w_in"], name="conv_in_dx")
    g_conv_in = mm_tn(h3, du, name="conv_in_dw", J=2, block="b", col_chips=2)
    dx2, dsh1_1, dsc1_1, dpremix1 = prenorm_bwd(dh3, x2, dx3, P["pre_mix_g"][1:2], m1[1], name="conv_prenorm_bwd")
    if grads_ready is not None:
        token = grads_ready("l1", [g_conv_in, g_conv_out, f1["g_up"], f1["g_down"]])
        m0 = tuple(m + token[0:1, 0:1] for m in m0)
    dx1, f0 = _ffn_backward(dx2, ffn0, m0[3:6], P["pre_ffn_g"][0:1], P["post_ffn_g"][0:1], P["ffn_w_up"][0],
                            P["ffn_w_down"][0], P["ffn_dw_w"][0], P["ffn_dw_b"][0:1], "ffn0")
    if grads_ready is not None:
        token = grads_ready("f0", [f0["g_up"], f0["g_down"]])
        m0 = tuple(m + token[0:1, 0:1] for m in m0)
    dy1, dg1_0, dpostmix0, _ = post_bwd(dx1, y1, P["post_mix_g"][0:1], m0[2], name="hgrn_post_bwd")
    dog = mm_nt(dy1, P["hgrn_w_out"], name="hgrn_out_dx")
    g_hgrn_out = mm_tn(og, dy1, name="hgrn_out_dw", J=1, block="a", row_chips=4)
    do, dgp, d_gn = hgrn_gate_bwd(dog, o, proj, P["hgrn_gnorm_g"], name="hgrn_gate_bwd")
    d3, dlb = hgrn_scan_bwd(proj, P["hgrn_lb_logits"], states, do, name="hgrn_scan_bwd")
    g_hgrn_in = mm_tn_parts(h1, d3, dgp, name="hgrn_in_dw")
    token = grads_ready("hg", [g_hgrn_in, g_hgrn_out]) if grads_ready is not None else None
    dh1 = mm_nt_parts(d3, dgp, P["hgrn_w_in"], name="hgrn_in_dx", after=token)
    dx0, dsh1_0, dsc1_0, dpremix0 = prenorm_bwd(dh1, x, dx1, P["pre_mix_g"][0:1], m0[1], name="hgrn_prenorm_bwd")

    dmod = jnp.stack([
        jnp.concatenate([dsh1_0, dsc1_0, dg1_0, f0["dsh"], f0["dsc"], f0["dgate"]], axis=1)[0],
        jnp.concatenate([dsh1_1, dsc1_1, dg1_1, f1["dsh"], f1["dsc"], f1["dgate"]], axis=1)[0]])
    small = dict(
        loss=lcols,
        pre_mix_g=jnp.concatenate([dpremix0, dpremix1]), post_mix_g=jnp.concatenate([dpostmix0, dpostmix1]),
        pre_ffn_g=jnp.concatenate([f0["dpre"], f1["dpre"]]), post_ffn_g=jnp.concatenate([f0["dpost"], f1["dpost"]]),
        lb=dlb, hgrn_gnorm_g=d_gn, ffn_dw_b=jnp.concatenate([f0["d_dw_b"], f1["d_dw_b"]]), dmod=dmod,
        conv_b_in=d_b_in, conv_dw_w=d_dw_w[0:CONV_W], conv_dw_b=d_dw_b, conv_ln_g=d_ln_g, conv_ln_b=d_ln_b,
        conv_b_out=d_b_out, ffn_dw_w=jnp.stack([f0["d_dw_w"], f1["d_dw_w"]]))
    big = [g_hgrn_in, g_hgrn_out, g_conv_in, g_conv_out, f0["g_up"], f1["g_up"], f0["g_down"], f1["g_down"]]
    return dx0, small, big


def _pack(parts, rows=8):
    flat = jnp.concatenate([p.reshape(-1).astype(F32) for p in parts])
    per = rows * 128
    pad = (-flat.shape[0]) % per
    return jnp.pad(flat, (0, pad)).reshape(rows, -1)


def _unpack(flat, shapes):
    out, off = [], 0
    for s in shapes:
        n = 1
        for d in s:
            n *= d
        out.append(flat[..., off:off + n].reshape(flat.shape[:-1] + tuple(s)))
        off += n
    return out


def _from_chips(stacked, axis):
    moved = jnp.moveaxis(stacked, 0, axis)
    shape = list(moved.shape)
    return moved.reshape(shape[:axis] + [shape[axis] * shape[axis + 1]] + shape[axis + 2:])


def _my_shard(full, axis, q):
    n = full.shape[axis] // N_CHIPS
    return lax.dynamic_slice_in_dim(full, q * n, n, axis=axis)


def kernel(x, c, ada_w, ada_b, pre_mix_g, post_mix_g, pre_ffn_g, post_ffn_g, hgrn_w_in, hgrn_lb_logits, hgrn_gnorm_g, hgrn_w_out, conv_w_in, conv_b_in, conv_dw_w, conv_dw_b, conv_ln_g, conv_ln_b, conv_w_out, conv_b_out, ffn_w_up, ffn_dw_w, ffn_dw_b, ffn_w_down, loss_target, m_ada_w, m_ada_b, m_pre_mix_g, m_post_mix_g, m_pre_ffn_g, m_post_ffn_g, m_hgrn_w_in, m_hgrn_lb_logits, m_hgrn_gnorm_g, m_hgrn_w_out, m_conv_w_in, m_conv_b_in, m_conv_dw_w, m_conv_dw_b, m_conv_ln_g, m_conv_ln_b, m_conv_w_out, m_conv_b_out, m_ffn_w_up, m_ffn_dw_w, m_ffn_dw_b, m_ffn_w_down, v_ada_w, v_ada_b, v_pre_mix_g, v_post_mix_g, v_pre_ffn_g, v_post_ffn_g, v_hgrn_w_in, v_hgrn_lb_logits, v_hgrn_gnorm_g, v_hgrn_w_out, v_conv_w_in, v_conv_b_in, v_conv_dw_w, v_conv_dw_b, v_conv_ln_g, v_conv_ln_b, v_conv_w_out, v_conv_b_out, v_ffn_w_up, v_ffn_dw_w, v_ffn_dw_b, v_ffn_w_down):
    W = dict(ada_w=ada_w, ada_b=ada_b, pre_mix_g=pre_mix_g, post_mix_g=post_mix_g, pre_ffn_g=pre_ffn_g,
             post_ffn_g=post_ffn_g, hgrn_w_in=hgrn_w_in, hgrn_lb_logits=hgrn_lb_logits, hgrn_gnorm_g=hgrn_gnorm_g,
             hgrn_w_out=hgrn_w_out, conv_w_in=conv_w_in, conv_b_in=conv_b_in, conv_dw_w=conv_dw_w,
             conv_dw_b=conv_dw_b, conv_ln_g=conv_ln_g, conv_ln_b=conv_ln_b, conv_w_out=conv_w_out,
             conv_b_out=conv_b_out, ffn_w_up=ffn_w_up, ffn_dw_w=ffn_dw_w, ffn_dw_b=ffn_dw_b, ffn_w_down=ffn_w_down)
    M = dict(ada_w=m_ada_w, ada_b=m_ada_b, pre_mix_g=m_pre_mix_g, post_mix_g=m_post_mix_g, pre_ffn_g=m_pre_ffn_g,
             post_ffn_g=m_post_ffn_g, hgrn_w_in=m_hgrn_w_in, hgrn_lb_logits=m_hgrn_lb_logits,
             hgrn_gnorm_g=m_hgrn_gnorm_g, hgrn_w_out=m_hgrn_w_out, conv_w_in=m_conv_w_in, conv_b_in=m_conv_b_in,
             conv_dw_w=m_conv_dw_w, conv_dw_b=m_conv_dw_b, conv_ln_g=m_conv_ln_g, conv_ln_b=m_conv_ln_b,
             conv_w_out=m_conv_w_out, conv_b_out=m_conv_b_out, ffn_w_up=m_ffn_w_up, ffn_dw_w=m_ffn_dw_w,
             ffn_dw_b=m_ffn_dw_b, ffn_w_down=m_ffn_w_down)
    V = dict(ada_w=v_ada_w, ada_b=v_ada_b, pre_mix_g=v_pre_mix_g, post_mix_g=v_post_mix_g, pre_ffn_g=v_pre_ffn_g,
             post_ffn_g=v_post_ffn_g, hgrn_w_in=v_hgrn_w_in, hgrn_lb_logits=v_hgrn_lb_logits,
             hgrn_gnorm_g=v_hgrn_gnorm_g, hgrn_w_out=v_hgrn_w_out, conv_w_in=v_conv_w_in, conv_b_in=v_conv_b_in,
             conv_dw_w=v_conv_dw_w, conv_dw_b=v_conv_dw_b, conv_ln_g=v_conv_ln_g, conv_ln_b=v_conv_ln_b,
             conv_w_out=v_conv_w_out, conv_b_out=v_conv_b_out, ffn_w_up=v_ffn_w_up, ffn_dw_w=v_ffn_dw_w,
             ffn_dw_b=v_ffn_dw_b, ffn_w_down=v_ffn_w_down)
    names = list(W)
    xi, yi, ci = lax.axis_index("x"), lax.axis_index("y"), lax.axis_index("c")
    q = 2 * xi + yi
    me = 2 * q + ci
    D = x.shape[-1]
    L = ada_w.shape[0]

    small_w = ["conv_b_in", "conv_dw_w", "conv_dw_b", "conv_ln_g", "conv_ln_b", "conv_b_out", "ffn_dw_w"]
    small_axis = dict(conv_b_in=1, conv_dw_w=2, conv_dw_b=1, conv_ln_g=1, conv_ln_b=1, conv_b_out=1, ffn_dw_w=2)
    packed = _pack([c] + [W[n] for n in small_w])

    def halves(w):
        shard = w.astype(BF16).reshape(1, 2, w.shape[0] // 2, w.shape[1])
        buf = lax.empty((N_CHIPS,) + shard.shape[1:], BF16)
        return lax.dynamic_update_slice_in_dim(buf, shard, q, axis=0)

    hg_send, hg_recv, hg_bufs, hg_token = allgather_chips_start([halves(hgrn_w_in[0]), halves(hgrn_w_out[0])],
                                                                name="gather_hgrn_weights_start")
    packed, _ = lax.optimization_barrier((packed, hg_token))
    gathered = allgather_devices(packed, name="gather_small_params").reshape(N_DEV, -1)
    c_all = gathered[:, 0:D]
    per_chip = gathered.reshape(N_CHIPS, 2, -1)[:, 0, D:]
    parts = _unpack(per_chip, [W[n].shape for n in small_w])
    P = {n: _from_chips(p, small_axis[n]) for n, p in zip(small_w, parts)}
    P["conv_dw_w"] = P["conv_dw_w"][0]
    for n in ("pre_mix_g", "post_mix_g", "pre_ffn_g", "post_ffn_g", "hgrn_lb_logits", "hgrn_gnorm_g", "ffn_dw_b"):
        P[n] = W[n]

    modp = ada_mod(c_all, ada_w, name="ada_mod")
    ncol = modp.shape[-1]
    mod_all = allgather_devices(modp.reshape(L * N_DEV, ncol), name="gather_mod")
    mod_all = mod_all.reshape(N_CHIPS, 2, L, N_DEV, ncol)[:, 0]
    mod_me = lax.dynamic_index_in_dim(mod_all, me, axis=2, keepdims=False)
    mod = mod_me.transpose(1, 0, 2).reshape(L, N_CHIPS * ncol) + ada_b
    mods = [tuple(mod[l:l + 1, k * D:(k + 1) * D] for k in range(6)) for l in range(L)]

    stack = lambda t: t.reshape(N_CHIPS, t.shape[1] * t.shape[2], t.shape[3])
    rowsh = lambda t: t.reshape(1, N_CHIPS * t.shape[1] * t.shape[2], t.shape[3])
    pairs = lambda t: t.reshape(2, 2, t.shape[1], t.shape[2]).transpose(0, 2, 1, 3).reshape(2, t.shape[1], 2 * t.shape[2])
    g = forward_to_sibling(allgather_chips_wait(hg_send, hg_recv, hg_bufs, mod, name="gather_hgrn_weights_wait"),
                           name="gather_hgrn_weights_forward")
    P["hgrn_w_in"], P["hgrn_w_out"] = stack(g[0]), rowsh(g[1])
    late_shards = [conv_w_in[0], conv_w_out[0], ffn_w_up[0], ffn_w_up[1], ffn_w_down[0], ffn_w_down[1]]
    late_bufs, _, _ = lax.optimization_barrier(([halves(w) for w in late_shards], g, mod))
    send_sems, recv_sems, bufs, token = allgather_chips_start(late_bufs, name="gather_weights_start")
    mods[0] = tuple(m + token[0:1, 0:1] for m in mods[0])

    def late_weights(x1):
        landed = allgather_chips_wait(send_sems, recv_sems, bufs, x1, name="gather_weights_wait")
        g = forward_to_sibling(landed, name="gather_weights_forward")
        return dict(conv_w_in=pairs(stack(g[0])), conv_w_out=rowsh(g[1]), ffn_w_up=[stack(g[2]), stack(g[3])],
                    ffn_w_down=[rowsh(g[4]), rowsh(g[5])])

    c_idx = ci.astype(jnp.int32).reshape(1)
    pending, in_flight = {}, {}

    def chip_stage(after):
        tag, (send, recv, grads, landing) = pending.popitem()
        grads, others = exchange_wait(send, recv, grads, landing, _pair_copies, after, name=f"grad_pair_wait_{tag}")
        sums = [pair_add(g_, o_, c_idx, name=f"grad_pair_add_{tag}_{a}") for a, (g_, o_) in enumerate(zip(grads, others))]
        landing = [lax.empty((3,) + s_.shape[1:], s_.dtype) for s_ in sums]
        send, recv, sums, landing, tok = exchange_start(sums, landing, _chip_copies, 3 * len(sums),
                                                        name=f"grad_chip_exchange_start_{tag}")
        in_flight[tag] = (send, recv, sums, landing)
        return tok

    def grads_ready(tag, grads):
        tok = chip_stage(grads[0]) if pending else 0.0
        landing = [lax.empty(g_.shape[1:], g_.dtype) for g_ in grads]
        send, recv, grads, landing, tok2 = exchange_start(grads, landing, _pair_copies, len(grads),
                                                          name=f"grad_pair_start_{tag}")
        pending[tag] = (send, recv, grads, landing)
        return tok + tok2

    grad_x, small, big = _local_step(x[0], loss_target[0], mods, P, late_weights, grads_ready)

    small_names = list(small)
    packed_g = _pack([small[n] for n in small_names])
    gs_buf = lax.dynamic_update_slice_in_dim(lax.empty((N_DEV,) + packed_g.shape, F32), packed_g[None], me, axis=0)
    sg_send, sg_recv, gs_buf, _, tok_sg = exchange_start([gs_buf], [], _device_copies, N_DEV - 1,
                                                         name="gather_small_grads_start")

    tok_hg = chip_stage(tok_sg)
    G = {}
    halves = []
    for tag in ("f0", "l1"):
        sums_t, landed_t = exchange_wait(*in_flight[tag], _chip_copies, grad_x, name=f"grad_chip_exchange_wait_{tag}")
        halves += finish_reduce(sums_t, landed_t, q, ci, f"{tag}_")
    red = [f.reshape(2 * f.shape[1], f.shape[2]) for f in half_swap(halves, name="grad_half_swap")]
    G["conv_w_in"], G["conv_w_out"] = red[2][None], red[3][None]
    G["ffn_w_up"] = jnp.stack([red[0], red[4]])
    G["ffn_w_down"] = jnp.stack([red[1], red[5]])

    delta, new_m, new_v = {}, {}, {}

    def adamw_matrix(n, after=None):
        shp = W[n].shape
        two = lambda t: t.reshape(-1, shp[-1])
        d_, m_, v_ = adamw(two(W[n]), two(G[n]), two(M[n]), two(V[n]), name=f"adamw_{n}", after=after)
        delta[n], new_m[n], new_v[n] = d_.reshape(shp), m_.reshape(shp), v_.reshape(shp)

    big_names = ["ada_w", "hgrn_w_in", "hgrn_w_out", "conv_w_in", "conv_w_out", "ffn_w_up", "ffn_w_down"]
    for n in ("conv_w_in", "conv_w_out", "ffn_w_up", "ffn_w_down"):
        adamw_matrix(n, after=tok_hg)

    (gs,), _ = exchange_wait(sg_send, sg_recv, gs_buf, [], _device_copies, delta["ffn_w_down"],
                             name="gather_small_grads_wait")
    dmod_all = _unpack(gs.reshape(N_DEV, -1), [small[n].shape for n in small_names])[small_names.index("dmod")]
    tot = sum_devices(gs, name="sum_small_grads").reshape(1, -1)
    S = dict(zip(small_names, _unpack(tot, [small[n].shape for n in small_names])))
    S = {n: v[0] for n, v in S.items()}
    loss = 0.5 * jnp.sum(S["loss"]) / D

    dmod_q = lax.dynamic_slice_in_dim(dmod_all, q * ncol, ncol, axis=2)
    G["ada_w"] = ada_wgrad(c_all.T, dmod_q.transpose(1, 0, 2), name="ada_wgrad")
    G["ada_b"] = S["dmod"]
    for n in ("pre_mix_g", "post_mix_g", "pre_ffn_g", "post_ffn_g", "hgrn_gnorm_g", "ffn_dw_b"):
        G[n] = S[n]
    G["hgrn_lb_logits"] = lb_logits_grad(hgrn_lb_logits, S["lb"], name="lb_logits_grad")
    G["conv_b_in"] = _my_shard(S["conv_b_in"], 1, q)
    G["conv_dw_w"] = _my_shard(S["conv_dw_w"], 1, q)[None]
    for n in ("conv_dw_b", "conv_ln_g", "conv_ln_b", "conv_b_out"):
        G[n] = _my_shard(S[n], 1, q)
    G["ffn_dw_w"] = _my_shard(S["ffn_dw_w"], 2, q)
    adamw_matrix("ada_w")

    sums_h, landed_h = exchange_wait(*in_flight["hg"], _chip_copies, delta["ada_w"], name="grad_chip_exchange_wait_hg")
    red_h = half_swap(finish_reduce(sums_h, landed_h, q, ci, "hg_"), name="grad_half_swap_hg")
    G["hgrn_w_in"], G["hgrn_w_out"] = [f.reshape(1, 2 * f.shape[1], f.shape[2]) for f in red_h]
    for n in ("hgrn_w_in", "hgrn_w_out"):
        adamw_matrix(n)
    rest = [n for n in names if n not in big_names]
    d_, m_, v_ = adamw(_pack([W[n] for n in rest]), _pack([G[n] for n in rest]), _pack([M[n] for n in rest]),
                       _pack([V[n] for n in rest]), name="adamw_small")
    shapes = [W[n].shape for n in rest]
    for n, a, b_, c_ in zip(rest, _unpack(d_.reshape(-1), shapes), _unpack(m_.reshape(-1), shapes),
                            _unpack(v_.reshape(-1), shapes)):
        delta[n], new_m[n], new_v[n] = a, b_, c_

    return (loss, grad_x[None], *[G[n].reshape(W[n].shape) for n in names], *[delta[n] for n in names],
            *[new_m[n] for n in names], *[new_v[n] for n in names])
```

```python
import jax
import jax.numpy as jnp
from jax import lax
from jax.experimental import pallas as pl
from jax.experimental.pallas import tpu as pltpu

F32 = jnp.float32
BF16 = jnp.bfloat16
EPS = 1e-6
HEAD = 128
BLK = 16
NEG = -1e30
CONV_W = 31
FFN_W = 3
N_CHIPS = 4
N_DEV = 8
SUB = 8
LANE = 128
V7X_VMEM_LIMIT = 56 * 1024 * 1024
MESH = pl.DeviceIdType.MESH
HBM = pl.BlockSpec(memory_space=pltpu.HBM)
VMEM_SPEC = pl.BlockSpec(memory_space=pltpu.VMEM)

ADAM_LR = 0.001
ADAM_B1 = 0.9
ADAM_B2 = 0.999
ADAM_EPS = 1e-08
ADAM_WD = 0.01
ADAM_STEP = 10


def _cp(*sem):
    return pltpu.CompilerParams(dimension_semantics=sem, vmem_limit_bytes=V7X_VMEM_LIMIT)


def _sig(x):
    return 0.5 * jnp.tanh(0.5 * x) + 0.5


def _silu(x):
    return x * _sig(x)


def _dsilu(x):
    s = _sig(x)
    return s * (1.0 + x * (1.0 - s))


def _dot(a, b):
    return jnp.dot(a, b, preferred_element_type=F32)


def _dot_nt(a, b):
    return lax.dot_general(a, b, (((1,), (1,)), ((), ())), preferred_element_type=F32)


def _dot_tn(a, b):
    return lax.dot_general(a, b, (((0,), (0,)), ((), ())), preferred_element_type=F32)


def _colsum(x):
    return jnp.sum(x, axis=0, keepdims=True)


def _rowmean(x):
    return jnp.mean(x, axis=-1, keepdims=True)


def _ffn_perm(j):
    return (j % 2) * 2 + j // 2


def _tile(n, pref):
    if n <= pref:
        return n
    t = pref - pref % 8
    while n % t:
        t -= 8
    return t


def mm_nn(a, w, *, name, bias=None, out_dtype=F32, perm=None, tm=1024):
    T, K = a.shape
    J, _, nb = w.shape
    tm = min(tm, T)
    col = (lambda j: j) if perm is None else perm

    def body(a_ref, w_ref, *rest):
        acc = _dot(a_ref[...], w_ref[...])
        if bias is not None:
            acc = acc + rest[0][...]
        rest[-1][...] = acc.astype(out_dtype)

    in_specs = [pl.BlockSpec((tm, K), lambda j, i: (i, 0)), pl.BlockSpec((None, K, nb), lambda j, i: (j, 0, 0))]
    args = [a, w]
    if bias is not None:
        in_specs.append(pl.BlockSpec((1, nb), lambda j, i: (0, j)))
        args.append(bias)
    return pl.pallas_call(
        body, grid=(J, T // tm), in_specs=in_specs,
        out_specs=pl.BlockSpec((tm, nb), lambda j, i: (i, col(j))),
        out_shape=jax.ShapeDtypeStruct((T, J * nb), out_dtype), name=name,
        compiler_params=_cp("parallel", "parallel"))(*args)


def mm_nt(a, w, *, name, out_dtype=F32, perm=None, tm=1024, after=None):
    T = a.shape[0]
    J, K, nb = w.shape
    tm = min(tm, T)
    col = (lambda j: j) if perm is None else perm
    deps = [] if after is None else [after]

    def body(a_ref, w_ref, *rest):
        o_ref, acc_ref = rest[len(deps):]
        j = pl.program_id(1)

        @pl.when(j == 0)
        def _():
            acc_ref[...] = jnp.zeros_like(acc_ref)

        acc_ref[...] += _dot_nt(a_ref[...], w_ref[...])

        @pl.when(j == J - 1)
        def _():
            o_ref[...] = acc_ref[...].astype(out_dtype)

    return pl.pallas_call(
        body, grid=(T // tm, J),
        in_specs=[pl.BlockSpec((tm, nb), lambda i, j: (i, col(j))), pl.BlockSpec((None, K, nb), lambda i, j: (j, 0, 0))]
        + [pl.BlockSpec(memory_space=pl.ANY)] * len(deps),
        out_specs=pl.BlockSpec((tm, K), lambda i, j: (i, 0)),
        out_shape=jax.ShapeDtypeStruct((T, K), out_dtype),
        scratch_shapes=[pltpu.VMEM((tm, K), F32)], name=name,
        compiler_params=_cp("parallel", "arbitrary"))(a, w, *deps)


def mm_tn(a, b, *, name, J, block, row_chips=1, col_chips=1, perm=None, tk=1024):
    T = a.shape[0]
    tk = min(tk, T)
    col = (lambda j: j) if perm is None else perm
    if block == "b":
        rows, nb = a.shape[1], b.shape[1] // J
        a_spec = pl.BlockSpec((tk, rows), lambda j, t: (t, 0))
        b_spec = pl.BlockSpec((tk, nb), lambda j, t: (t, col(j)))
    else:
        rows, nb = a.shape[1] // J, b.shape[1]
        a_spec = pl.BlockSpec((tk, rows), lambda j, t: (t, col(j)))
        b_spec = pl.BlockSpec((tk, nb), lambda j, t: (t, 0))
    rh = rows // (2 * row_chips)
    nc = nb // col_chips
    chips = [(rc, cc) for rc in range(row_chips) for cc in range(col_chips)]

    def body(a_ref, b_ref, o_ref):
        @pl.when(pl.program_id(1) == 0)
        def _():
            o_ref[...] = jnp.zeros_like(o_ref)

        acc = _dot_tn(a_ref[...], b_ref[...])
        for ch, (rc, cc) in enumerate(chips):
            for hf in range(2):
                r0 = (rc * 2 + hf) * rh
                o_ref[hf, ch] += acc[r0:r0 + rh, cc * nc:(cc + 1) * nc]

    return pl.pallas_call(
        body, grid=(J, T // tk), in_specs=[a_spec, b_spec],
        out_specs=pl.BlockSpec((2, len(chips), rh, nc), lambda j, t: (0, j, 0, 0)),
        out_shape=jax.ShapeDtypeStruct((2, J * len(chips), rh, nc), F32), name=name,
        compiler_params=_cp("parallel", "arbitrary"))(a, b)


def mm_nt_parts(s3, g, w, *, name, tm=1024, after=None):
    n3, T, nb = s3.shape
    J, K, _ = w.shape
    tm = min(tm, T)
    deps = [] if after is None else [after]

    def body(s_ref, g_ref, w_ref, *rest):
        o_ref, acc_ref = rest[len(deps):]
        j = pl.program_id(1)

        @pl.when(j == 0)
        def _():
            acc_ref[...] = jnp.zeros_like(acc_ref)

        @pl.when(j < n3)
        def _():
            acc_ref[...] += _dot_nt(s_ref[...], w_ref[...])

        @pl.when(j == n3)
        def _():
            acc_ref[...] += _dot_nt(g_ref[...], w_ref[...])

        @pl.when(j == J - 1)
        def _():
            o_ref[...] = acc_ref[...]

    return pl.pallas_call(
        body, grid=(T // tm, J),
        in_specs=[pl.BlockSpec((None, tm, nb), lambda i, j: (jnp.minimum(j, n3 - 1), i, 0)),
                  pl.BlockSpec((tm, nb), lambda i, j: (i, 0)), pl.BlockSpec((None, K, nb), lambda i, j: (j, 0, 0))]
        + [pl.BlockSpec(memory_space=pl.ANY)] * len(deps),
        out_specs=pl.BlockSpec((tm, K), lambda i, j: (i, 0)), out_shape=jax.ShapeDtypeStruct((T, K), F32),
        scratch_shapes=[pltpu.VMEM((tm, K), F32)], name=name,
        compiler_params=_cp("parallel", "arbitrary"))(s3, g, w, *deps)


def mm_tn_parts(a, s3, g, *, name, tk=1024):
    n3, T, nb = s3.shape
    J = n3 + 1
    tk = min(tk, T)
    rows = a.shape[1]
    rh = rows // 2

    def body(a_ref, s_ref, g_ref, o_ref):
        j = pl.program_id(0)

        @pl.when(pl.program_id(1) == 0)
        def _():
            o_ref[...] = jnp.zeros_like(o_ref)

        def add(b_ref):
            acc = _dot_tn(a_ref[...], b_ref[...])
            for hf in range(2):
                o_ref[hf, 0] += acc[hf * rh:(hf + 1) * rh, :]

        pl.when(j < n3)(lambda: add(s_ref))
        pl.when(j == n3)(lambda: add(g_ref))

    return pl.pallas_call(
        body, grid=(J, T // tk),
        in_specs=[pl.BlockSpec((tk, rows), lambda j, t: (t, 0)),
                  pl.BlockSpec((None, tk, nb), lambda j, t: (jnp.minimum(j, n3 - 1), t, 0)),
                  pl.BlockSpec((tk, nb), lambda j, t: (t, 0))],
        out_specs=pl.BlockSpec((2, 1, rh, nb), lambda j, t: (0, j, 0, 0)),
        out_shape=jax.ShapeDtypeStruct((2, J, rh, nb), F32), name=name,
        compiler_params=_cp("parallel", "arbitrary"))(a, s3, g)


def _row(tm, w):
    return pl.BlockSpec((tm, w), lambda i: (i, 0))


def _full(r, w):
    return pl.BlockSpec((r, w), lambda i: (0, 0))


def _acc_init(i, *refs):
    @pl.when(i == 0)
    def _():
        for r in refs:
            r[...] = jnp.zeros_like(r)


def prenorm(x, g, sc, sh, *, name, tm=512):
    T, D = x.shape
    tm = min(tm, T)

    def body(x_ref, g_ref, sc_ref, sh_ref, h_ref):
        xv = x_ref[...]
        r = lax.rsqrt(_rowmean(xv * xv) + EPS)
        h_ref[...] = ((xv * r) * g_ref[...] * (1.0 + sc_ref[...]) + sh_ref[...]).astype(BF16)

    return pl.pallas_call(
        body, grid=(T // tm,), in_specs=[_row(tm, D), _full(1, D), _full(1, D), _full(1, D)],
        out_specs=_row(tm, D), out_shape=jax.ShapeDtypeStruct((T, D), BF16), name=name,
        compiler_params=_cp("parallel"))(x, g, sc, sh)


def post_residual_prenorm(x, y, g, gate, g2, sc2, sh2, *, name, tm=512):
    T, D = x.shape
    tm = min(tm, T)

    def body(x_ref, y_ref, g_ref, gate_ref, g2_ref, sc2_ref, sh2_ref, o_ref, h_ref):
        yv = y_ref[...]
        r = lax.rsqrt(_rowmean(yv * yv) + EPS)
        out = x_ref[...] + gate_ref[...] * ((yv * r) * g_ref[...])
        o_ref[...] = out
        r2 = lax.rsqrt(_rowmean(out * out) + EPS)
        h_ref[...] = ((out * r2) * g2_ref[...] * (1.0 + sc2_ref[...]) + sh2_ref[...]).astype(BF16)

    return pl.pallas_call(
        body, grid=(T // tm,), in_specs=[_row(tm, D), _row(tm, D)] + [_full(1, D)] * 5,
        out_specs=[_row(tm, D), _row(tm, D)],
        out_shape=[jax.ShapeDtypeStruct((T, D), F32), jax.ShapeDtypeStruct((T, D), BF16)], name=name,
        compiler_params=_cp("parallel"))(x, y, g, gate, g2, sc2, sh2)


def post_residual_loss(x, y, g, gate, tgt, *, name, tm=512):
    T, D = x.shape
    tm = min(tm, T)

    def body(x_ref, y_ref, g_ref, gate_ref, t_ref, dx_ref, l_ref):
        _acc_init(pl.program_id(0), l_ref)
        yv = y_ref[...]
        r = lax.rsqrt(_rowmean(yv * yv) + EPS)
        e = x_ref[...] + gate_ref[...] * ((yv * r) * g_ref[...]) - t_ref[...]
        dx_ref[...] = e * (1.0 / D)
        l_ref[...] += _colsum(e * e)

    return pl.pallas_call(
        body, grid=(T // tm,), in_specs=[_row(tm, D), _row(tm, D), _full(1, D), _full(1, D), _row(tm, D)],
        out_specs=[_row(tm, D), _full(1, D)],
        out_shape=[jax.ShapeDtypeStruct((T, D), F32), jax.ShapeDtypeStruct((1, D), F32)], name=name,
        compiler_params=_cp("arbitrary"))(x, y, g, gate, tgt)


def post_bwd(dx, y, g, gate, *, name, tm=512):
    T, D = dx.shape
    tm = min(tm, T)

    def body(dx_ref, y_ref, g_ref, gate_ref, dy_ref, dgate_ref, dg_ref, dbias_ref):
        _acc_init(pl.program_id(0), dgate_ref, dg_ref, dbias_ref)
        yv = y_ref[...]
        dxv = dx_ref[...]
        r = lax.rsqrt(_rowmean(yv * yv) + EPS)
        yn = yv * r
        gv = g_ref[...]
        gt = gate_ref[...]
        dgate_ref[...] += _colsum(dxv * (yn * gv))
        dg_ref[...] += _colsum(dxv * gt * yn)
        dyn = dxv * gt * gv
        dy = r * (dyn - yn * _rowmean(dyn * yn))
        dbias_ref[...] += _colsum(dy)
        dy_ref[...] = dy.astype(BF16)

    return pl.pallas_call(
        body, grid=(T // tm,), in_specs=[_row(tm, D), _row(tm, D), _full(1, D), _full(1, D)],
        out_specs=[_row(tm, D), _full(1, D), _full(1, D), _full(1, D)],
        out_shape=[jax.ShapeDtypeStruct((T, D), BF16)] + [jax.ShapeDtypeStruct((1, D), F32)] * 3, name=name,
        compiler_params=_cp("arbitrary"))(dx, y, g, gate)


def prenorm_bwd(dh, x, dres, g, sc, *, name, tm=512):
    T, D = x.shape
    tm = min(tm, T)

    def body(dh_ref, x_ref, dres_ref, g_ref, sc_ref, dx_ref, dsh_ref, dsc_ref, dg_ref):
        _acc_init(pl.program_id(0), dsh_ref, dsc_ref, dg_ref)
        xv = x_ref[...]
        dhv = dh_ref[...]
        r = lax.rsqrt(_rowmean(xv * xv) + EPS)
        xn = xv * r
        gv = g_ref[...]
        one_sc = 1.0 + sc_ref[...]
        dsh_ref[...] += _colsum(dhv)
        dsc_ref[...] += _colsum(dhv * (xn * gv))
        dg_ref[...] += _colsum(dhv * one_sc * xn)
        dxn = dhv * one_sc * gv
        dx_ref[...] = dres_ref[...] + r * (dxn - xn * _rowmean(dxn * xn))

    return pl.pallas_call(
        body, grid=(T // tm,), in_specs=[_row(tm, D), _row(tm, D), _row(tm, D), _full(1, D), _full(1, D)],
        out_specs=[_row(tm, D), _full(1, D), _full(1, D), _full(1, D)],
        out_shape=[jax.ShapeDtypeStruct((T, D), F32)] + [jax.ShapeDtypeStruct((1, D), F32)] * 3, name=name,
        compiler_params=_cp("arbitrary"))(dh, x, dres, g, sc)


def prenorm_post_bwd(dh, x, dres, g, sc, y, g_post, gate, *, name, tm=512):
    T, D = x.shape
    tm = min(tm, T)

    def body(dh_ref, x_ref, dres_ref, g_ref, sc_ref, y_ref, gp_ref, gate_ref,
             dx_ref, dsh_ref, dsc_ref, dg_ref, dy_ref, dgate_ref, dgp_ref, dbias_ref):
        _acc_init(pl.program_id(0), dsh_ref, dsc_ref, dg_ref, dgate_ref, dgp_ref, dbias_ref)
        xv = x_ref[...]
        dhv = dh_ref[...]
        r = lax.rsqrt(_rowmean(xv * xv) + EPS)
        xn = xv * r
        gv = g_ref[...]
        one_sc = 1.0 + sc_ref[...]
        dsh_ref[...] += _colsum(dhv)
        dsc_ref[...] += _colsum(dhv * (xn * gv))
        dg_ref[...] += _colsum(dhv * one_sc * xn)
        dxn = dhv * one_sc * gv
        dxv = dres_ref[...] + r * (dxn - xn * _rowmean(dxn * xn))
        dx_ref[...] = dxv
        yv = y_ref[...]
        ry = lax.rsqrt(_rowmean(yv * yv) + EPS)
        yn = yv * ry
        gp = gp_ref[...]
        gt = gate_ref[...]
        dgate_ref[...] += _colsum(dxv * (yn * gp))
        dgp_ref[...] += _colsum(dxv * gt * yn)
        dyn = dxv * gt * gp
        dy = ry * (dyn - yn * _rowmean(dyn * yn))
        dbias_ref[...] += _colsum(dy)
        dy_ref[...] = dy.astype(BF16)

    vec = jax.ShapeDtypeStruct((1, D), F32)
    return pl.pallas_call(
        body, grid=(T // tm,),
        in_specs=[_row(tm, D)] * 3 + [_full(1, D)] * 2 + [_row(tm, D)] + [_full(1, D)] * 2,
        out_specs=[_row(tm, D)] + [_full(1, D)] * 3 + [_row(tm, D)] + [_full(1, D)] * 3,
        out_shape=[jax.ShapeDtypeStruct((T, D), F32), vec, vec, vec, jax.ShapeDtypeStruct((T, D), BF16), vec, vec, vec],
        name=name, compiler_params=_cp("arbitrary"))(dh, x, dres, g, sc, y, g_post, gate)


HALO = 16


def _shift_helpers():
    rid = lax.broadcasted_iota(jnp.int32, (SUB, LANE), 0)

    def down(cur, prev, k):
        return pltpu.roll(jnp.where(rid >= SUB - k, prev, cur), k, 0)

    def up(cur, nxt, k):
        return pltpu.roll(jnp.where(rid < k, nxt, cur), SUB - k, 0)

    return down, up


def _ffn_sides(c, nb, wa_ref, wb_ref, ba_ref, bb_ref):
    cols = slice(c * LANE, (c + 1) * LANE)
    return [(cols, [wa_ref[k:k + 1, cols] for k in range(FFN_W)], ba_ref[:, cols]),
            (slice(nb + c * LANE, nb + (c + 1) * LANE), [wb_ref[k:k + 1, cols] for k in range(FFN_W)],
             bb_ref[:, cols])]


def _ffn_specs(tm, nb, hb, idx):
    return [pl.BlockSpec((tm, 2 * nb), lambda jc, i: (idx(i), jc)),
            pl.BlockSpec((HALO, 2 * nb), lambda jc, i: (jnp.maximum(idx(i) * hb - 1, 0), jc)),
            pl.BlockSpec((FFN_W, nb), lambda jc, i: (0, jc)),
            pl.BlockSpec((FFN_W, nb), lambda jc, i: (0, jc + 2)),
            pl.BlockSpec((1, nb), lambda jc, i: (0, jc)),
            pl.BlockSpec((1, nb), lambda jc, i: (0, jc + 2))]


def ffn_act(u0p, dw_w, dw_b, *, name, tm=256):
    T, W = u0p.shape
    nb = W // 4
    tm = min(tm, T)
    unroll = 4
    rows16 = 2 * SUB

    def body(u_ref, halo_ref, wa_ref, wb_ref, ba_ref, bb_ref, z_ref, ab_ref):
        i = pl.program_id(1)
        down, _ = _shift_helpers()
        for c in range(nb // LANE):
            cols = slice(c * LANE, (c + 1) * LANE)
            side = _ffn_sides(c, nb, wa_ref, wb_ref, ba_ref, bb_ref)

            def rows(j, prev):
                prev = list(prev)
                for m in range(unroll):
                    r0 = pl.multiple_of((j * unroll + m) * rows16, rows16)
                    x = [u_ref[pl.ds(r0, rows16), cs].astype(F32) for cs, _, _ in side]
                    conv = [[None, None], [None, None]]
                    for hf in range(2):
                        for n, (_, w, b) in enumerate(side):
                            cur = x[n][hf * SUB:(hf + 1) * SUB, :]
                            conv[n][hf] = b + w[2] * cur + w[1] * down(cur, prev[n], 1) + w[0] * down(cur, prev[n], 2)
                            prev[n] = cur
                    a, b = [jnp.concatenate(conv[n], axis=0) for n in range(2)]
                    z_ref[pl.ds(r0, rows16), cols] = (_silu(a) * b).astype(BF16)
                    ab_ref[pl.ds(r0, rows16), side[0][0]] = a.astype(BF16)
                    ab_ref[pl.ds(r0, rows16), side[1][0]] = b.astype(BF16)
                return tuple(prev)

            first = [jnp.where(i == 0, 0.0, halo_ref[:, cs].astype(F32)[SUB:2 * SUB, :]) for cs, _, _ in side]
            lax.fori_loop(0, tm // (rows16 * unroll), rows, tuple(first))

    return pl.pallas_call(
        body, grid=(2, T // tm), in_specs=_ffn_specs(tm, nb, tm // HALO, lambda i: i),
        out_specs=[pl.BlockSpec((tm, nb), lambda jc, i: (i, jc)), pl.BlockSpec((tm, 2 * nb), lambda jc, i: (i, jc))],
        out_shape=[jax.ShapeDtypeStruct((T, 2 * nb), BF16), jax.ShapeDtypeStruct((T, W), BF16)], name=name,
        compiler_params=_cp("parallel", "arbitrary"))(u0p, u0p, dw_w, dw_w, dw_b, dw_b)


def ffn_act_bwd(dz, u0p, ab, dw_w, *, name, tm=256):
    T, W = u0p.shape
    nb = W // 4
    tm = min(tm, T)
    nt = T // tm
    unroll = 4
    rows16 = 2 * SUB
    n_it = tm // (rows16 * unroll)

    def body(dz_ref, u_ref, ab_ref, wa_ref, wb_ref, du0_ref, dw_ref, carry):
        i = pl.program_id(1)
        _acc_init(i, dw_ref)
        _, up = _shift_helpers()
        for c in range(nb // LANE):
            cols = slice(c * LANE, (c + 1) * LANE)
            side = [(cols, [wa_ref[k:k + 1, cols] for k in range(FFN_W)]),
                    (slice(nb + c * LANE, nb + (c + 1) * LANE), [wb_ref[k:k + 1, cols] for k in range(FFN_W)])]

            def rows(j, st):
                nxt, acc = list(st[0:2]), list(st[2:10])
                for m in range(unroll):
                    r0 = pl.multiple_of(((n_it - 1 - j) * unroll + unroll - 1 - m) * rows16, rows16)
                    dzv = dz_ref[pl.ds(r0, rows16), cols].astype(F32)
                    a, b = [ab_ref[pl.ds(r0, rows16), cs].astype(F32) for cs, _ in side]
                    x = [u_ref[pl.ds(r0, rows16), cs].astype(F32) for cs, _ in side]
                    sa = _sig(a)
                    d16 = [dzv * b * (sa * (1.0 + a * (1.0 - sa))), dzv * (a * sa)]
                    out = [[None, None], [None, None]]
                    for hf in (1, 0):
                        half = slice(hf * SUB, (hf + 1) * SUB)
                        for n in range(2):
                            w = side[n][1]
                            d = d16[n][half, :]
                            u = x[n][half, :]
                            up1, up2 = up(d, nxt[n], 1), up(d, nxt[n], 2)
                            acc[4 * n + 0] = acc[4 * n + 0] + up2 * u
                            acc[4 * n + 1] = acc[4 * n + 1] + up1 * u
                            acc[4 * n + 2] = acc[4 * n + 2] + d * u
                            acc[4 * n + 3] = acc[4 * n + 3] + d
                            out[n][hf] = w[2] * d + w[1] * up1 + w[0] * up2
                            nxt[n] = d
                    for n in range(2):
                        du0_ref[pl.ds(r0, rows16), side[n][0]] = jnp.concatenate(out[n], axis=0).astype(BF16)
                return (*nxt, *acc)

            init = [jnp.where(i == 0, 0.0, carry[:, cs]) for cs, _ in side] + [jnp.zeros((SUB, LANE), F32)] * 8
            st = lax.fori_loop(0, n_it, rows, tuple(init))
            for n in range(2):
                carry[:, side[n][0]] = st[n]
                for k in range(4):
                    dw_ref[k, :, side[n][0]] += st[2 + 4 * n + k]

        @pl.when(i == nt - 1)
        def _():
            for k in range(4):
                dw_ref[k, 0:1, :] = _colsum(dw_ref[k])

    rev = lambda i: nt - 1 - i
    wide = pl.BlockSpec((tm, 2 * nb), lambda jc, i: (rev(i), jc))
    return pl.pallas_call(
        body, grid=(2, nt),
        in_specs=[pl.BlockSpec((tm, nb), lambda jc, i: (rev(i), jc)), wide, wide,
                  pl.BlockSpec((FFN_W, nb), lambda jc, i: (0, jc)), pl.BlockSpec((FFN_W, nb), lambda jc, i: (0, jc + 2))],
        out_specs=[wide, pl.BlockSpec((4, SUB, 2 * nb), lambda jc, i: (0, 0, jc))],
        out_shape=[jax.ShapeDtypeStruct((T, W), BF16), jax.ShapeDtypeStruct((4, SUB, W), F32)],
        scratch_shapes=[pltpu.VMEM((SUB, 2 * nb), F32)], name=name,
        compiler_params=_cp("parallel", "arbitrary"))(dz, u0p, ab, dw_w, dw_w)


CHALO = 32
CCOL = 256


def _phase_copies(buf, shifted, tm):
    n = tm + CHALO - SUB
    for p in range(1, SUB):
        shifted[p - 1, 0:n, :] = buf[p:p + n, :]


def _shifted(buf, shifted, r, tm, c0):
    m, p = divmod(r, SUB)
    src = buf if p == 0 else shifted.at[p - 1]
    return src[m * SUB:m * SUB + tm, c0:c0 + CCOL]


def conv_act(u, dw_w, dw_b, ln_g, ln_b, *, name, tm=128):
    T, D2 = u.shape
    D = D2 // 2
    tm = min(tm, T)
    hb = tm // CHALO

    def body(u_ref, halo_ref, w_ref, b_ref, g_ref, be_ref, s_ref, cv_ref, gbuf, gsh):
        i = pl.program_id(0)
        hv = halo_ref[...]
        gbuf[0:CHALO, :] = jnp.where(i == 0, 0.0, hv[:, 0:D] * _sig(hv[:, D:D2]))
        uv = u_ref[...]
        gbuf[CHALO:CHALO + tm, :] = uv[:, 0:D] * _sig(uv[:, D:D2])
        _phase_copies(gbuf, gsh, tm)
        for c0 in range(0, D, CCOL):
            acc = jnp.zeros((tm, CCOL), F32) + b_ref[:, c0:c0 + CCOL]
            for k in range(CONV_W):
                acc = acc + w_ref[k:k + 1, c0:c0 + CCOL] * _shifted(gbuf, gsh, CHALO - (CONV_W - 1) + k, tm, c0)
            cv_ref[:, c0:c0 + CCOL] = acc
        cv = cv_ref[...]
        mu = _rowmean(cv)
        xc = cv - mu
        nh = xc * lax.rsqrt(_rowmean(xc * xc) + EPS)
        s_ref[...] = _silu(nh * g_ref[...] + be_ref[...]).astype(BF16)

    return pl.pallas_call(
        body, grid=(T // tm,),
        in_specs=[_row(tm, D2), pl.BlockSpec((CHALO, D2), lambda i: (jnp.maximum(i * hb - 1, 0), 0)),
                  _full(CONV_W, D), _full(1, D), _full(1, D), _full(1, D)],
        out_specs=[_row(tm, D), _row(tm, D)],
        out_shape=[jax.ShapeDtypeStruct((T, D), BF16), jax.ShapeDtypeStruct((T, D), F32)],
        scratch_shapes=[pltpu.VMEM((tm + CHALO, D), F32), pltpu.VMEM((SUB - 1, tm + CHALO, D), F32)], name=name,
        compiler_params=_cp("arbitrary"))(u, u, dw_w, dw_b, ln_g, ln_b)


def conv_norm_bwd(ds, cv, ln_g, ln_b, *, name, tm=512):
    T, D = cv.shape
    tm = min(tm, T)

    def body(ds_ref, cv_ref, g_ref, be_ref, dcv_ref, dg_ref, dbe_ref, dcb_ref):
        _acc_init(pl.program_id(0), dg_ref, dbe_ref, dcb_ref)
        cv_ = cv_ref[...]
        mu = _rowmean(cv_)
        xc = cv_ - mu
        rstd = lax.rsqrt(_rowmean(xc * xc) + EPS)
        nh = xc * rstd
        gv = g_ref[...]
        dln = ds_ref[...] * _dsilu(nh * gv + be_ref[...])
        dg_ref[...] += _colsum(dln * nh)
        dbe_ref[...] += _colsum(dln)
        dnh = dln * gv
        dcv = rstd * (dnh - _rowmean(dnh) - nh * _rowmean(dnh * nh))
        dcb_ref[...] += _colsum(dcv)
        dcv_ref[...] = dcv

    return pl.pallas_call(
        body, grid=(T // tm,), in_specs=[_row(tm, D), _row(tm, D), _full(1, D), _full(1, D)],
        out_specs=[_row(tm, D), _full(1, D), _full(1, D), _full(1, D)],
        out_shape=[jax.ShapeDtypeStruct((T, D), F32)] + [jax.ShapeDtypeStruct((1, D), F32)] * 3, name=name,
        compiler_params=_cp("arbitrary"))(ds, cv, ln_g, ln_b)


def conv_glu_bwd(dcv, u, dw_w, *, name, tm=128):
    T, D2 = u.shape
    D = D2 // 2
    tm = min(tm, T)
    nt = T // tm
    hb = tm // CHALO

    def body(dcv_ref, dnext_ref, u_ref, w_ref, du_ref, dw_ref, dbin_ref, dbuf, dsh):
        i = pl.program_id(0)
        _acc_init(i, dw_ref, dbin_ref)
        uv = u_ref[...]
        av = uv[:, 0:D]
        sg = _sig(uv[:, D:D2])
        glu = av * sg
        dbuf[0:tm, :] = dcv_ref[...]
        dbuf[tm:tm + CHALO, :] = jnp.where(i == nt - 1, 0.0, dnext_ref[...])
        _phase_copies(dbuf, dsh, tm)
        for c0 in range(0, D, CCOL):
            glu_c = glu[:, c0:c0 + CCOL]
            acc = jnp.zeros((tm, CCOL), F32)
            for k in range(CONV_W):
                moved = _shifted(dbuf, dsh, CONV_W - 1 - k, tm, c0)
                dw_ref[k:k + 1, c0:c0 + CCOL] += _colsum(moved * glu_c)
                acc = acc + w_ref[k:k + 1, c0:c0 + CCOL] * moved
            a_c = av[:, c0:c0 + CCOL]
            s_c = sg[:, c0:c0 + CCOL]
            da = acc * s_c
            dgt = acc * a_c * s_c * (1.0 - s_c)
            dbin_ref[:, c0:c0 + CCOL] += _colsum(da)
            dbin_ref[:, D + c0:D + c0 + CCOL] += _colsum(dgt)
            du_ref[:, c0:c0 + CCOL] = da.astype(BF16)
            du_ref[:, D + c0:D + c0 + CCOL] = dgt.astype(BF16)

    return pl.pallas_call(
        body, grid=(nt,),
        in_specs=[_row(tm, D), pl.BlockSpec((CHALO, D), lambda i: (jnp.minimum((i + 1) * hb, T // CHALO - 1), 0)),
                  _row(tm, D2), _full(CONV_W, D)],
        out_specs=[_row(tm, D2), _full(CHALO, D), _full(1, D2)],
        out_shape=[jax.ShapeDtypeStruct((T, D2), BF16), jax.ShapeDtypeStruct((CHALO, D), F32),
                   jax.ShapeDtypeStruct((1, D2), F32)],
        scratch_shapes=[pltpu.VMEM((tm + CHALO, D), F32), pltpu.VMEM((SUB - 1, tm + CHALO, D), F32)],
        name=name, compiler_params=_cp("arbitrary"))(dcv, dcv, u, dw_w)


HB = 8


def _lb0(lg_ref):
    l0, l1, l2 = lg_ref[0:1, :], lg_ref[1:2, :], lg_ref[2:3, :]
    m = jnp.maximum(jnp.maximum(l0, l1), l2)
    e0 = jnp.exp(l0 - m)
    return e0 / (e0 + jnp.exp(l1 - m) + jnp.exp(l2 - m))


def _mm_exact(m01, x):
    hi = x.astype(BF16)
    r1 = x - hi.astype(F32)
    mid = r1.astype(BF16)
    lo = (r1 - mid.astype(F32)).astype(BF16)
    return _dot(m01, hi) + _dot(m01, mid) + _dot(m01, lo)


def _block_tri(tm):
    r = jnp.arange(tm)[:, None]
    c = jnp.arange(tm)[None, :]
    same = (r // BLK) == (c // BLK)
    return (same & (c <= r)).astype(BF16), (same & (c >= r)).astype(BF16)


def _halves(x):
    return [x[0:SUB, :], x[SUB:BLK, :]]


def _live_halves(s):
    return ([(0, s)] if s < SUB else []) + [(1, max(s - SUB, 0))]


def _const_spec(shape):
    return pl.BlockSpec(shape, lambda h, i: (0, 0))


def _hgrn_specs(H, hb, tm, idx):
    g = H // hb
    return [pl.BlockSpec((tm, hb * HEAD), lambda h, i: (idx(i), h)),
            pl.BlockSpec((tm, hb * HEAD), lambda h, i: (idx(i), g + h)),
            pl.BlockSpec((tm, hb * HEAD), lambda h, i: (idx(i), 2 * g + h)),
            pl.BlockSpec((3, hb * HEAD), lambda h, i: (0, h))]


def hgrn_scan(proj, lb_logits, *, name, tm=128):
    T = proj.shape[0]
    H = proj.shape[1] // (4 * HEAD)
    hb = min(HB, H)
    tm = min(tm, T)
    nt = T // tm
    nblk = tm // BLK
    tril, _ = _block_tri(tm)
    heads = [slice(hh * HEAD, (hh + 1) * HEAD) for hh in range(hb)]

    def body(qp_ref, fz_ref, v_ref, lg_ref, tril_ref, o_ref, st_ref, S_ref, q_s, k_s, b_s):
        @pl.when(pl.program_id(1) == 0)
        def _():
            S_ref[...] = jnp.zeros_like(S_ref)

        st_ref[...] = S_ref[...]
        lb = _lb0(lg_ref)
        f = lb + (1.0 - lb) * _sig(fz_ref[...])
        q_s[...] = _silu(qp_ref[...])
        k_s[...] = 1.0 - f
        b_s[...] = _mm_exact(tril_ref[...], jnp.log(f))
        rows = lax.broadcasted_iota(jnp.int32, (BLK, HEAD), 0)
        S = [S_ref[hh] for hh in range(hb)]
        for nb in range(nblk):
            blk = slice(nb * BLK, (nb + 1) * BLK)
            last = slice(nb * BLK + BLK - 1, nb * BLK + BLK)
            qb = [q_s[blk, c] for c in heads]
            bb = [b_s[blk, c] for c in heads]
            o = [_dot_nt((qb[hh] * jnp.exp(bb[hh])).astype(BF16), S[hh].astype(BF16)) for hh in range(hb)]
            for hh, c in enumerate(heads):
                bc = b_s[last, c]
                kd = k_s[blk, c] * jnp.exp(bc - bb[hh])
                S[hh] = S[hh] * jnp.exp(bc) + _dot_tn(v_ref[blk, c].astype(BF16), kd.astype(BF16))
            for s in range(BLK):
                r = slice(nb * BLK + s, nb * BLK + s + 1)
                for hh, c in enumerate(heads):
                    dec = jnp.exp(jnp.where(rows >= s, bb[hh] - b_s[r, c], NEG))
                    a = jnp.sum(qb[hh] * k_s[r, c] * dec, axis=-1, keepdims=True)
                    o[hh] = o[hh] + a * v_ref[r, c]
            for hh, c in enumerate(heads):
                o_ref[blk, c] = o[hh]
        for hh in range(hb):
            S_ref[hh] = S[hh]

    return pl.pallas_call(
        body, grid=(H // hb, nt),
        in_specs=_hgrn_specs(H, hb, tm, lambda i: i) + [_const_spec((tm, tm))],
        out_specs=[pl.BlockSpec((tm, hb * HEAD), lambda h, i: (i, h)),
                   pl.BlockSpec((None, hb, HEAD, HEAD), lambda h, i: (i, h, 0, 0))],
        out_shape=[jax.ShapeDtypeStruct((T, H * HEAD), F32), jax.ShapeDtypeStruct((nt, H, HEAD, HEAD), F32)],
        scratch_shapes=[pltpu.VMEM((hb, HEAD, HEAD), F32)] + [pltpu.VMEM((tm, hb * HEAD), F32)] * 3, name=name,
        compiler_params=_cp("parallel", "arbitrary"))(proj, proj, proj, lb_logits, tril)


def hgrn_scan_bwd(proj, lb_logits, states, do, *, name, tm=128):
    T = proj.shape[0]
    H = proj.shape[1] // (4 * HEAD)
    hb = min(HB, H)
    tm = min(tm, T)
    nt = T // tm
    nblk = tm // BLK
    tril, triu = _block_tri(tm)
    sel = (jnp.arange(BLK * SUB)[None, :] // SUB == jnp.arange(BLK)[:, None]).astype(BF16)
    heads = [slice(hh * HEAD, (hh + 1) * HEAD) for hh in range(hb)]

    def body(qp_ref, fz_ref, v_ref, lg_ref, st_ref, do_ref, tril_ref, triu_ref, sel_ref, d3_ref, dlb_ref,
             dS_ref, Sb_ref, q_s, k_s, b_s, dq_s, dk_s, dv_s, db_s, pk_s, pv_s):
        i = pl.program_id(1)

        @pl.when(i == 0)
        def _():
            dS_ref[...] = jnp.zeros_like(dS_ref)
            dlb_ref[...] = jnp.zeros_like(dlb_ref)

        lb = _lb0(lg_ref)
        qp = qp_ref[...]
        sg = _sig(fz_ref[...])
        f = lb + (1.0 - lb) * sg
        q_s[...] = _silu(qp)
        k_s[...] = 1.0 - f
        b_s[...] = _mm_exact(tril_ref[...], jnp.log(f))
        rows = lax.broadcasted_iota(jnp.int32, (SUB, HEAD), 0)
        rows1 = lax.broadcasted_iota(jnp.int32, (SUB, 1), 0)

        S = [st_ref[hh] for hh in range(hb)]
        for nb in range(nblk):
            blk = slice(nb * BLK, (nb + 1) * BLK)
            last = slice(nb * BLK + BLK - 1, nb * BLK + BLK)
            for hh, c in enumerate(heads):
                Sb_ref[nb * hb + hh] = S[hh]
                if nb < nblk - 1:
                    bc = b_s[last, c]
                    kd = k_s[blk, c] * jnp.exp(bc - b_s[blk, c])
                    S[hh] = S[hh] * jnp.exp(bc) + _dot_tn(v_ref[blk, c].astype(BF16), kd.astype(BF16))

        dS = [dS_ref[hh] for hh in range(hb)]
        for nb in reversed(range(nblk)):
            blk = slice(nb * BLK, (nb + 1) * BLK)
            last = slice(nb * BLK + BLK - 1, nb * BLK + BLK)
            qb, kb, bb, dob, dq, dbc, ebc = [], [], [], [], [], [], []
            for hh, c in enumerate(heads):
                S0 = Sb_ref[nb * hb + hh]
                qb.append(q_s[blk, c])
                kb.append(k_s[blk, c])
                bb.append(b_s[blk, c])
                dob.append(do_ref[blk, c])
                bc = b_s[last, c]
                eb = jnp.exp(bb[hh])
                ekd = jnp.exp(bc - bb[hh])
                ebc.append(jnp.exp(bc))
                dS16 = dS[hh].astype(BF16)
                dob16 = dob[hh].astype(BF16)
                dq.append(_dot(dob16, S0.astype(BF16)) * eb)
                dki = _dot(v_ref[blk, c].astype(BF16), dS16) * ekd
                dk_s[blk, c] = dki
                dv_s[blk, c] = _dot_nt((kb[hh] * ekd).astype(BF16), dS16)
                dbc.append(_colsum(dS[hh] * S0) * ebc[hh] + _colsum(kb[hh] * dki))
                dS[hh] = dS[hh] * ebc[hh] + _dot_tn(dob16, (qb[hh] * eb).astype(BF16))
            qh, bh, doh, dqh = [[_halves(t[hh]) for hh in range(hb)] for t in (qb, bb, dob, dq)]
            for s in range(BLK):
                r = slice(nb * BLK + s, nb * BLK + s + 1)
                for hh, c in enumerate(heads):
                    ks = k_s[r, c]
                    pk, pv = None, None
                    for hf, lo in _live_halves(s):
                        diff = bh[hh][hf] - b_s[r, c]
                        dec = jnp.exp(diff if lo == 0 else jnp.where(rows >= lo, diff, NEG))
                        w = qh[hh][hf] * dec
                        a = jnp.sum(w * ks, axis=-1, keepdims=True)
                        da = jnp.sum(doh[hh][hf] * v_ref[r, c], axis=-1, keepdims=True)
                        if lo:
                            da = jnp.where(rows1 >= lo, da, 0.0)
                        dqh[hh][hf] = dqh[hh][hf] + (da * ks) * dec
                        pk = da * w if pk is None else pk + da * w
                        pv = a * doh[hh][hf] if pv is None else pv + a * doh[hh][hf]
                    pk_s[hh, s * SUB:(s + 1) * SUB, :] = pk
                    pv_s[hh, s * SUB:(s + 1) * SUB, :] = pv
            for hh, c in enumerate(heads):
                khi, klo = _split2(pk_s[hh])
                dk_s[blk, c] += _dot(sel_ref[...], khi) + _dot(sel_ref[...], klo)
                dv_s[blk, c] += _dot(sel_ref[...], pv_s[hh].astype(BF16))
                dq[hh] = jnp.concatenate(dqh[hh], axis=0)
                dq_s[blk, c] = dq[hh]
                db_s[blk, c] = qb[hh] * dq[hh] - kb[hh] * dk_s[blk, c]
                db_s[last, c] += dbc[hh]
        for hh in range(hb):
            dS_ref[hh] = dS[hh]

        dlf = _mm_exact(triu_ref[...], db_s[...])
        df = dlf / f - dk_s[...]
        d3_ref[0] = (dq_s[...] * _dsilu(qp)).astype(BF16)
        d3_ref[1] = (df * (1.0 - lb) * sg * (1.0 - sg)).astype(BF16)
        d3_ref[2] = dv_s[...].astype(BF16)
        dlb_ref[...] += _colsum(df * (1.0 - sg))

    rev = lambda i: nt - 1 - i
    out_blk = pl.BlockSpec((tm, hb * HEAD), lambda h, i: (rev(i), h))
    return pl.pallas_call(
        body, grid=(H // hb, nt),
        in_specs=_hgrn_specs(H, hb, tm, rev) + [pl.BlockSpec((None, hb, HEAD, HEAD), lambda h, i: (rev(i), h, 0, 0)),
                                                out_blk, _const_spec((tm, tm)), _const_spec((tm, tm)),
                                                _const_spec((BLK, BLK * SUB))],
        out_specs=[pl.BlockSpec((3, tm, hb * HEAD), lambda h, i: (0, rev(i), h)),
                   pl.BlockSpec((1, hb * HEAD), lambda h, i: (0, h))],
        out_shape=[jax.ShapeDtypeStruct((3, T, H * HEAD), BF16), jax.ShapeDtypeStruct((1, H * HEAD), F32)],
        scratch_shapes=[pltpu.VMEM((hb, HEAD, HEAD), F32), pltpu.VMEM((nblk * hb, HEAD, HEAD), F32)]
        + [pltpu.VMEM((tm, hb * HEAD), F32)] * 7 + [pltpu.VMEM((hb, BLK * SUB, HEAD), F32)] * 2, name=name,
        compiler_params=_cp("parallel", "arbitrary"))(proj, proj, proj, lb_logits, states, do, tril, triu, sel)


def hgrn_gate(o, proj, gn, *, name, tm=512):
    T, D = o.shape
    H = D // HEAD
    tm = min(tm, T)

    def body(o_ref, gp_ref, gn_ref, og_ref):
        gn_ = gn_ref[...]
        for h in range(H):
            c = slice(h * HEAD, (h + 1) * HEAD)
            oh = o_ref[:, c]
            r = lax.rsqrt(_rowmean(oh * oh) + EPS)
            og_ref[:, c] = ((oh * r) * gn_ * _silu(gp_ref[:, c])).astype(BF16)

    return pl.pallas_call(
        body, grid=(T // tm,),
        in_specs=[_row(tm, D), pl.BlockSpec((tm, D), lambda i: (i, 3)), _full(1, HEAD)],
        out_specs=_row(tm, D), out_shape=jax.ShapeDtypeStruct((T, D), BF16), name=name,
        compiler_params=_cp("parallel"))(o, proj, gn)


def hgrn_gate_bwd(dog, o, proj, gn, *, name, tm=512):
    T, D = o.shape
    H = D // HEAD
    tm = min(tm, T)

    def body(dog_ref, o_ref, gp_ref, gn_ref, do_ref, dgp_ref, dgn_ref):
        _acc_init(pl.program_id(0), dgn_ref)
        gn_ = gn_ref[...]
        for h in range(H):
            c = slice(h * HEAD, (h + 1) * HEAD)
            oh = o_ref[:, c]
            gp = gp_ref[:, c]
            dg = dog_ref[:, c]
            r = lax.rsqrt(_rowmean(oh * oh) + EPS)
            on = oh * r
            dgp_ref[:, c] = (dg * (on * gn_) * _dsilu(gp)).astype(BF16)
            don = dg * _silu(gp)
            dgn_ref[...] += _colsum(don * on)
            dn = don * gn_
            do_ref[:, c] = r * (dn - on * _rowmean(dn * on))

    return pl.pallas_call(
        body, grid=(T // tm,),
        in_specs=[_row(tm, D), _row(tm, D), pl.BlockSpec((tm, D), lambda i: (i, 3)), _full(1, HEAD)],
        out_specs=[_row(tm, D), _row(tm, D), _full(1, HEAD)],
        out_shape=[jax.ShapeDtypeStruct((T, D), F32), jax.ShapeDtypeStruct((T, D), BF16),
                   jax.ShapeDtypeStruct((1, HEAD), F32)], name=name,
        compiler_params=_cp("arbitrary"))(dog, o, proj, gn)


def _split2(x):
    hi = x.astype(BF16)
    return hi, (x - hi.astype(F32)).astype(BF16)


def ada_mod(c_all, ada_w, *, name):
    L, D, N = ada_w.shape
    B = c_all.shape[0]

    def body(c_ref, w_ref, o_ref):
        chi, clo = _split2(_silu(c_ref[...]))
        whi, wlo = _split2(w_ref[...])
        o_ref[...] = _dot(chi, whi) + _dot(chi, wlo) + _dot(clo, whi)

    return pl.pallas_call(
        body, grid=(L,), in_specs=[_full(B, D), pl.BlockSpec((None, D, N), lambda l: (l, 0, 0))],
        out_specs=pl.BlockSpec((None, B, N), lambda l: (l, 0, 0)),
        out_shape=jax.ShapeDtypeStruct((L, B, N), F32), name=name, compiler_params=_cp("parallel"))(c_all, ada_w)


def ada_wgrad(c_all_t, dmod, *, name, tr=256):
    D, B = c_all_t.shape
    L, _, N = dmod.shape
    tr = min(tr, D)

    def body(c_ref, d_ref, o_ref):
        cond = _silu(c_ref[...])
        acc = cond[:, 0:1] * d_ref[0:1, :]
        for b in range(1, B):
            acc = acc + cond[:, b:b + 1] * d_ref[b:b + 1, :]
        o_ref[...] = acc

    return pl.pallas_call(
        body, grid=(L, D // tr),
        in_specs=[pl.BlockSpec((tr, B), lambda l, r: (r, 0)), pl.BlockSpec((None, B, N), lambda l, r: (l, 0, 0))],
        out_specs=pl.BlockSpec((None, tr, N), lambda l, r: (l, r, 0)),
        out_shape=jax.ShapeDtypeStruct((L, D, N), F32), name=name,
        compiler_params=_cp("parallel", "parallel"))(c_all_t, dmod)


def sum_devices(parts, *, name):
    n, R, C = parts.shape

    def body(p_ref, o_ref):
        acc = p_ref[0]
        for d in range(1, n):
            acc = acc + p_ref[d]
        o_ref[...] = acc

    return pl.pallas_call(body, in_specs=[VMEM_SPEC], out_specs=VMEM_SPEC,
                          out_shape=jax.ShapeDtypeStruct((R, C), F32), name=name)(parts)


def lb_logits_grad(lb_logits, dlb, *, name):
    def body(lg_ref, d_ref, o_ref):
        l0, l1, l2 = lg_ref[0:1, :], lg_ref[1:2, :], lg_ref[2:3, :]
        m = jnp.maximum(jnp.maximum(l0, l1), l2)
        e0, e1, e2 = jnp.exp(l0 - m), jnp.exp(l1 - m), jnp.exp(l2 - m)
        z = e0 + e1 + e2
        p0, p1, p2 = e0 / z, e1 / z, e2 / z
        g = d_ref[...] * p0
        o_ref[0:1, :] = g * (1.0 - p0)
        o_ref[1:2, :] = -g * p1
        o_ref[2:3, :] = -g * p2

    return pl.pallas_call(body, in_specs=[VMEM_SPEC, VMEM_SPEC], out_specs=VMEM_SPEC,
                          out_shape=jax.ShapeDtypeStruct(lb_logits.shape, F32), name=name)(lb_logits, dlb)


def adamw(w, g, m, v, *, name, tr=256, after=None):
    R, C = w.shape
    tr = _tile(R, tr)
    deps = [] if after is None else [after]

    def body(w_ref, g_ref, m_ref, v_ref, *rest):
        d_ref, nm_ref, nv_ref = rest[len(deps):]
        gv = g_ref[...]
        nm = ADAM_B1 * m_ref[...] + (1.0 - ADAM_B1) * gv
        nv = ADAM_B2 * v_ref[...] + (1.0 - ADAM_B2) * (gv * gv)
        m_hat = nm / (1.0 - ADAM_B1 ** ADAM_STEP)
        v_hat = nv / (1.0 - ADAM_B2 ** ADAM_STEP)
        d_ref[...] = -ADAM_LR * (m_hat / (jnp.sqrt(v_hat) + ADAM_EPS) + ADAM_WD * w_ref[...])
        nm_ref[...] = nm
        nv_ref[...] = nv

    spec = pl.BlockSpec((tr, C), lambda i: (i, 0))
    return pl.pallas_call(
        body, grid=(R // tr,), in_specs=[spec] * 4 + [pl.BlockSpec(memory_space=pl.ANY)] * len(deps), out_specs=[spec] * 3,
        out_shape=[jax.ShapeDtypeStruct((R, C), F32)] * 3, name=name,
        compiler_params=_cp("parallel"))(w, g, m, v, *deps)


def _place():
    return lax.axis_index("x"), lax.axis_index("y"), lax.axis_index("c")


def _flip(v, bit):
    return 1 - v if bit else v


def allgather_devices(v, *, name):
    R, C = v.shape

    def body(v_ref, out_ref, send_sems, recv_sems, local_sem):
        x, y, c = _place()
        me = 4 * x + 2 * y + c
        mine = pltpu.make_async_copy(v_ref, out_ref.at[me], local_sem)
        mine.start()
        sends = []
        for k in range(1, N_DEV):
            peer = (_flip(x, k & 4), _flip(y, k & 2), _flip(c, k & 1))
            cp = pltpu.make_async_remote_copy(src_ref=v_ref, dst_ref=out_ref.at[me], send_sem=send_sems.at[k - 1],
                                              recv_sem=recv_sems.at[k - 1], device_id=peer, device_id_type=MESH)
            cp.start()
            sends.append(cp)
        for k in range(1, N_DEV):
            px, py, pc = _flip(x, k & 4), _flip(y, k & 2), _flip(c, k & 1)
            pltpu.make_async_remote_copy(src_ref=v_ref, dst_ref=out_ref.at[4 * px + 2 * py + pc],
                                         send_sem=send_sems.at[k - 1], recv_sem=recv_sems.at[k - 1],
                                         device_id=(px, py, pc), device_id_type=MESH).wait_recv()
        for cp in sends:
            cp.wait_send()
        mine.wait()

    return pl.pallas_call(
        body, in_specs=[VMEM_SPEC], out_specs=VMEM_SPEC, out_shape=jax.ShapeDtypeStruct((N_DEV, R, C), v.dtype),
        scratch_shapes=[pltpu.SemaphoreType.DMA((N_DEV - 1,)), pltpu.SemaphoreType.DMA((N_DEV - 1,)),
                        pltpu.SemaphoreType.DMA], name=name)(v)


def _other_chips(x, y):
    return [(1 - x, y), (x, 1 - y), (1 - x, 1 - y)]


SEM = pl.BlockSpec(memory_space=pltpu.SEMAPHORE)
DATAFLOW = pltpu.SideEffectType.DATAFLOW_SIDE_EFFECTING


def _chip_copy(buf, a, j, q, c, chips, send_sems, recv_sems):
    px, py = chips[j]
    return pltpu.make_async_remote_copy(src_ref=buf.at[q, c], dst_ref=buf.at[q, c], send_sem=send_sems.at[3 * a + j],
                                        recv_sem=recv_sems.at[3 * a + j], device_id=(px, py, c), device_id_type=MESH)


def allgather_chips_start(bufs, *, name):
    n = len(bufs)

    def body(*refs):
        send_sems, recv_sems = refs[n], refs[n + 1]
        outs = refs[n + 2:2 * n + 2]
        token = refs[2 * n + 2]
        x, y, c = _place()
        chips = _other_chips(x, y)
        for a in range(n):
            for j in range(3):
                _chip_copy(outs[a], a, j, 2 * x + y, c, chips, send_sems, recv_sems).start()
        token[...] = jnp.zeros_like(token)

    res = pl.pallas_call(
        body, name=name, in_specs=[HBM] * n,
        out_specs=(SEM, SEM, *([HBM] * n), VMEM_SPEC),
        out_shape=(pltpu.SemaphoreType.DMA((3 * n,)), pltpu.SemaphoreType.DMA((3 * n,)),
                   *[pltpu.HBM(b.shape, b.dtype) for b in bufs], jax.ShapeDtypeStruct((SUB, LANE), F32)),
        input_output_aliases={a: a + 2 for a in range(n)},
        compiler_params=pltpu.CompilerParams(has_side_effects=DATAFLOW),
    )(*[pltpu.with_memory_space_constraint(b, pltpu.HBM) for b in bufs])
    return res[0], res[1], list(res[2:2 + n]), res[2 + n]


def allgather_chips_wait(send_sems, recv_sems, bufs, after, *, name):
    n = len(bufs)

    def body(*refs):
        ins = refs[:n]
        send_sems, recv_sems = refs[n], refs[n + 1]
        x, y, c = _place()
        chips = _other_chips(x, y)
        for a in range(n):
            for j, (px, py) in enumerate(chips):
                _chip_copy(ins[a], a, j, 2 * x + y, c, chips, send_sems, recv_sems).wait_send()
                _chip_copy(ins[a], a, j, 2 * px + py, c, chips, send_sems, recv_sems).wait_recv()

    return list(pl.pallas_call(
        body, name=name, in_specs=[HBM] * n + [SEM, SEM, pl.BlockSpec(memory_space=pl.ANY)],
        out_specs=[HBM] * n, out_shape=[pltpu.HBM(b.shape, b.dtype) for b in bufs],
        input_output_aliases={a: a for a in range(n)},
        compiler_params=pltpu.CompilerParams(has_side_effects=DATAFLOW),
    )(*bufs, send_sems, recv_sems, after))


def forward_to_sibling(bufs, *, name):
    n = len(bufs)

    def body(*refs):
        outs = refs[n:2 * n]
        send_sems, recv_sems = refs[2 * n:]
        x, y, c = _place()
        chips = _other_chips(x, y)

        def copy(a, j, half, to):
            px, py = chips[j]
            slab = outs[a].at[2 * px + py, half]
            return pltpu.make_async_remote_copy(src_ref=slab, dst_ref=slab, send_sem=send_sems.at[a, j],
                                                recv_sem=recv_sems.at[a, j], device_id=to, device_id_type=MESH)

        sends = [copy(a, j, c, (x, y, 1 - c)) for a in range(n) for j in range(3)]
        for cp in sends:
            cp.start()
        for a in range(n):
            for j in range(3):
                copy(a, j, 1 - c, (x, y, c)).wait_recv()
        for cp in sends:
            cp.wait_send()

    return pl.pallas_call(
        body, in_specs=[HBM] * n, out_specs=[HBM] * n,
        out_shape=[jax.ShapeDtypeStruct(b.shape, b.dtype) for b in bufs],
        input_output_aliases={a: a for a in range(n)},
        scratch_shapes=[pltpu.SemaphoreType.DMA((n, 3)), pltpu.SemaphoreType.DMA((n, 3))], name=name)(*bufs)


def pair_add(g, other, c_idx, *, name, tr=256):
    _, Q, R, C = g.shape
    tr = _tile(R, tr)

    def body(c_ref, g_ref, o_ref, out_ref):
        out_ref[...] = (g_ref[...] + o_ref[...]).astype(BF16)

    return pl.pallas_call(
        body,
        grid_spec=pltpu.PrefetchScalarGridSpec(
            num_scalar_prefetch=1, grid=(Q, R // tr),
            in_specs=[pl.BlockSpec((None, None, tr, C), lambda q, r, c_ref: (c_ref[0], q, r, 0)),
                      pl.BlockSpec((None, tr, C), lambda q, r, c_ref: (q, r, 0))],
            out_specs=pl.BlockSpec((None, tr, C), lambda q, r, c_ref: (q, r, 0))),
        out_shape=jax.ShapeDtypeStruct((Q, R, C), BF16), name=name,
        compiler_params=_cp("parallel", "parallel"))(c_idx, g, other)


def chip_sum(sums, landed, qc_idx, *, name, tr=256):
    _, R, C = sums.shape
    tr = _tile(R, tr)

    def body(qc_ref, own_ref, l_ref, o_ref):
        acc = own_ref[...].astype(F32)
        for k in range(3):
            acc = acc + l_ref[k].astype(F32)
        o_ref[...] = acc

    return pl.pallas_call(
        body,
        grid_spec=pltpu.PrefetchScalarGridSpec(
            num_scalar_prefetch=1, grid=(R // tr,),
            in_specs=[pl.BlockSpec((None, tr, C), lambda r, qc: (qc[0], r, 0)),
                      pl.BlockSpec((3, tr, C), lambda r, qc: (0, r, 0))],
            out_specs=pl.BlockSpec((None, tr, C), lambda r, qc: (qc[1], r, 0))),
        out_shape=jax.ShapeDtypeStruct((2, R, C), F32), name=name,
        compiler_params=_cp("parallel"))(qc_idx, sums, landed)


def half_swap(bufs, *, name):
    n = len(bufs)

    def body(*refs):
        outs = refs[n:2 * n]
        send_sems, recv_sems = refs[2 * n:]
        x, y, c = _place()
        cps = [pltpu.make_async_remote_copy(src_ref=outs[a].at[c], dst_ref=outs[a].at[c], send_sem=send_sems.at[a],
                                            recv_sem=recv_sems.at[a], device_id=(x, y, 1 - c), device_id_type=MESH)
               for a in range(n)]
        for cp in cps:
            cp.start()
        for a in range(n):
            pltpu.make_async_remote_copy(src_ref=outs[a].at[c], dst_ref=outs[a].at[1 - c], send_sem=send_sems.at[a],
                                         recv_sem=recv_sems.at[a], device_id=(x, y, 1 - c),
                                         device_id_type=MESH).wait_recv()
        for cp in cps:
            cp.wait_send()

    return pl.pallas_call(
        body, in_specs=[HBM] * n, out_specs=[HBM] * n,
        out_shape=[jax.ShapeDtypeStruct(b.shape, b.dtype) for b in bufs],
        input_output_aliases={a: a for a in range(n)},
        scratch_shapes=[pltpu.SemaphoreType.DMA((n,)), pltpu.SemaphoreType.DMA((n,))], name=name)(*bufs)


def _chip_copies(src, dst, send_sems, recv_sems):
    x, y, c = _place()
    return [pltpu.make_async_remote_copy(src_ref=src[a].at[2 * px + py], dst_ref=dst[a].at[j],
                                         send_sem=send_sems.at[3 * a + j], recv_sem=recv_sems.at[3 * a + j],
                                         device_id=(px, py, c), device_id_type=MESH)
            for a in range(len(src)) for j, (px, py) in enumerate(_other_chips(x, y))]


def _pair_copies(src, dst, send_sems, recv_sems):
    x, y, c = _place()
    return [pltpu.make_async_remote_copy(src_ref=src[a].at[1 - c], dst_ref=dst[a], send_sem=send_sems.at[a],
                                         recv_sem=recv_sems.at[a], device_id=(x, y, 1 - c), device_id_type=MESH)
            for a in range(len(src))]


def _device_copies(src, dst, send_sems, recv_sems):
    x, y, c = _place()
    mine = src[0].at[4 * x + 2 * y + c]
    return [pltpu.make_async_remote_copy(src_ref=mine, dst_ref=mine, send_sem=send_sems.at[k - 1],
                                         recv_sem=recv_sems.at[k - 1],
                                         device_id=(_flip(x, k & 4), _flip(y, k & 2), _flip(c, k & 1)), device_id_type=MESH)
            for k in range(1, N_DEV)]


def exchange_start(src, landing, copies, n_sems, *, name):
    n, m = len(src), len(src) + len(landing)

    def body(*refs):
        send_sems, recv_sems = refs[m], refs[m + 1]
        for cp in copies(refs[m + 2:m + 2 + n], refs[m + 2 + n:2 * m + 2], send_sems, recv_sems):
            cp.start()
        token = refs[2 * m + 2]
        token[...] = jnp.zeros_like(token)

    res = pl.pallas_call(
        body, name=name, in_specs=[HBM] * m,
        out_specs=(SEM, SEM, *([HBM] * m), VMEM_SPEC),
        out_shape=(pltpu.SemaphoreType.DMA((n_sems,)), pltpu.SemaphoreType.DMA((n_sems,)),
                   *[pltpu.HBM(b.shape, b.dtype) for b in src + landing], jax.ShapeDtypeStruct((SUB, LANE), F32)),
        input_output_aliases={a: a + 2 for a in range(m)},
        compiler_params=pltpu.CompilerParams(has_side_effects=DATAFLOW),
    )(*[pltpu.with_memory_space_constraint(b, pltpu.HBM) for b in src + landing])
    return res[0], res[1], list(res[2:2 + n]), list(res[2 + n:2 + m]), res[2 + m]


def exchange_wait(send_sems, recv_sems, src, landed, copies, after, *, name):
    n, m = len(src), len(src) + len(landed)

    def body(*refs):
        for cp in copies(refs[:n], refs[n:m], refs[m], refs[m + 1]):
            cp.wait_send()
            cp.wait_recv()

    res = pl.pallas_call(
        body, name=name, in_specs=[HBM] * m + [SEM, SEM, pl.BlockSpec(memory_space=pl.ANY)],
        out_specs=[HBM] * m, out_shape=[pltpu.HBM(b.shape, b.dtype) for b in src + landed],
        input_output_aliases={a: a for a in range(m)},
        compiler_params=pltpu.CompilerParams(has_side_effects=DATAFLOW),
    )(*src, *landed, send_sems, recv_sems, after)
    return list(res[:n]), list(res[n:])


def finish_reduce(sums, landed, q, c, tag):
    qc_idx = jnp.stack([q, c]).astype(jnp.int32)
    return [chip_sum(s, l, qc_idx, name=f"grad_chip_sum_{tag}{a}") for a, (s, l) in enumerate(zip(sums, landed))]


def _ffn_forward(x, h, mod, post_g, w_up, w_down, dw_w, dw_b, tag, next_norm=None, tgt=None):
    _, _, gate = mod
    u0 = mm_nn(h, w_up, name=f"{tag}_up", out_dtype=BF16, perm=_ffn_perm)
    z, ab = ffn_act(u0, dw_w, dw_b, name=f"{tag}_act")
    y = mm_nn(z, w_down, name=f"{tag}_down")
    if tgt is None:
        out = post_residual_prenorm(x, y, post_g, gate, *next_norm, name=f"{tag}_post")
    else:
        out = post_residual_loss(x, y, post_g, gate, tgt, name=f"{tag}_post_loss")
    return out, (x, h, u0, ab, z, y)


def _ffn_backward(dx, entry, saved, mod, pre_g, post_g, w_up, w_down, dw_w, tag, before):
    x, h, u0, ab, z, y = saved
    _, sc, gate = mod
    dy, dgate, dpost = entry if entry is not None else post_bwd(dx, y, post_g, gate, name=f"{tag}_post_bwd")[:3]
    dz = mm_nt(dy, w_down, name=f"{tag}_down_dx", out_dtype=BF16)
    g_down = mm_tn(z, dy, name=f"{tag}_down_dw", J=2, block="a", row_chips=2)
    du0, dconv = ffn_act_bwd(dz, u0, ab, dw_w, name=f"{tag}_act_bwd")
    dh = mm_nt(du0, w_up, name=f"{tag}_up_dx", perm=_ffn_perm)
    g_up = mm_tn(h, du0, name=f"{tag}_up_dw", J=4, block="b", perm=_ffn_perm)
    dx_in, dsh, dsc, dpre, *prev = prenorm_post_bwd(dh, x, dx, pre_g, sc, *before, name=f"{tag}_prenorm_bwd")
    nb = u0.shape[1] // 4
    dconv = dconv[:, 0].reshape(4, 2, 2, nb).transpose(0, 2, 1, 3).reshape(4, 4 * nb)
    return dx_in, dict(dsh=dsh, dsc=dsc, dgate=dgate, dpre=dpre, dpost=dpost, g_up=g_up, g_down=g_down,
                       d_dw_w=dconv[0:FFN_W], d_dw_b=dconv[3:4]), prev


def _local_step(x, tgt, mods, P, late_weights=None, grads_ready=None):
    m0, m1 = mods
    h1 = prenorm(x, P["pre_mix_g"][0:1], m0[1], m0[0], name="hgrn_prenorm")
    proj = mm_nn(h1, P["hgrn_w_in"], name="hgrn_in")
    o, states = hgrn_scan(proj, P["hgrn_lb_logits"], name="hgrn_scan")
    og = hgrn_gate(o, proj, P["hgrn_gnorm_g"], name="hgrn_gate")
    y1 = mm_nn(og, P["hgrn_w_out"], name="hgrn_out")
    x1, h_f0 = post_residual_prenorm(x, y1, P["post_mix_g"][0:1], m0[2], P["pre_ffn_g"][0:1], m0[4], m0[3],
                                     name="hgrn_post")
    if late_weights is not None:
        P = {**P, **late_weights(x1)}
    (x2, h3), ffn0 = _ffn_forward(x1, h_f0, m0[3:6], P["post_ffn_g"][0:1], P["ffn_w_up"][0], P["ffn_w_down"][0],
                                  P["ffn_dw_w"][0], P["ffn_dw_b"][0:1], "ffn0",
                                  next_norm=(P["pre_mix_g"][1:2], m1[1], m1[0]))
    u = mm_nn(h3, P["conv_w_in"], name="conv_in", bias=P["conv_b_in"])
    s, cv = conv_act(u, P["conv_dw_w"], P["conv_dw_b"], P["conv_ln_g"], P["conv_ln_b"], name="conv_act")
    y3 = mm_nn(s, P["conv_w_out"], name="conv_out", bias=P["conv_b_out"])
    x3, h_f1 = post_residual_prenorm(x2, y3, P["post_mix_g"][1:2], m1[2], P["pre_ffn_g"][1:2], m1[4], m1[3],
                                     name="conv_post")
    (dx4, lcols), ffn1 = _ffn_forward(x3, h_f1, m1[3:6], P["post_ffn_g"][1:2], P["ffn_w_up"][1], P["ffn_w_down"][1],
                                      P["ffn_dw_w"][1], P["ffn_dw_b"][1:2], "ffn1", tgt=tgt)
    dx3, f1, (dy3, dg1_1, dpostmix1, d_b_out) = _ffn_backward(
        dx4, None, ffn1, m1[3:6], P["pre_ffn_g"][1:2], P["post_ffn_g"][1:2], P["ffn_w_up"][1], P["ffn_w_down"][1],
        P["ffn_dw_w"][1], "ffn1", before=(y3, P["post_mix_g"][1:2], m1[2]))
    ds = mm_nt(dy3, P["conv_w_out"], name="conv_out_dx")
    g_conv_out = mm_tn(s, dy3, name="conv_out_dw", J=1, block="a", row_chips=4)
    dcv, d_ln_g, d_ln_b, d_dw_b = conv_norm_bwd(ds, cv, P["conv_ln_g"], P["conv_ln_b"], name="conv_norm_bwd")
    du, d_dw_w, d_b_in = conv_glu_bwd(dcv, u, P["conv_dw_w"], name="conv_glu_bwd")
    dh3 = mm_nt(du, P["conv_w_in"], name="conv_in_dx")
    g_conv_in = mm_tn(h3, du, name="conv_in_dw", J=2, block="b", col_chips=2)
    if grads_ready is not None:
        token = grads_ready("l1", [g_conv_in, g_conv_out, f1["g_up"], f1["g_down"]])
        m0 = tuple(m + token[0:1, 0:1] for m in m0)
    dx2, dsh1_1, dsc1_1, dpremix1, *entry_f0 = prenorm_post_bwd(
        dh3, x2, dx3, P["pre_mix_g"][1:2], m1[1], ffn0[5], P["post_ffn_g"][0:1], m0[5], name="conv_prenorm_bwd")
    dx1, f0, (dy1, dg1_0, dpostmix0, _) = _ffn_backward(
        dx2, entry_f0[:3], ffn0, m0[3:6], P["pre_ffn_g"][0:1], P["post_ffn_g"][0:1], P["ffn_w_up"][0],
        P["ffn_w_down"][0], P["ffn_dw_w"][0], "ffn0", before=(y1, P["post_mix_g"][0:1], m0[2]))
    token = grads_ready("f0", [f0["g_up"], f0["g_down"]]) if grads_ready is not None else None
    dog = mm_nt(dy1, P["hgrn_w_out"], name="hgrn_out_dx", after=token)
    g_hgrn_out = mm_tn(og, dy1, name="hgrn_out_dw", J=1, block="a", row_chips=4)
    do, dgp, d_gn = hgrn_gate_bwd(dog, o, proj, P["hgrn_gnorm_g"], name="hgrn_gate_bwd")
    d3, dlb = hgrn_scan_bwd(proj, P["hgrn_lb_logits"], states, do, name="hgrn_scan_bwd")
    g_hgrn_in = mm_tn_parts(h1, d3, dgp, name="hgrn_in_dw")
    token = grads_ready("hg", [g_hgrn_in, g_hgrn_out]) if grads_ready is not None else None
    dh1 = mm_nt_parts(d3, dgp, P["hgrn_w_in"], name="hgrn_in_dx", after=token)
    dx0, dsh1_0, dsc1_0, dpremix0 = prenorm_bwd(dh1, x, dx1, P["pre_mix_g"][0:1], m0[1], name="hgrn_prenorm_bwd")

    dmod = jnp.stack([
        jnp.concatenate([dsh1_0, dsc1_0, dg1_0, f0["dsh"], f0["dsc"], f0["dgate"]], axis=1)[0],
        jnp.concatenate([dsh1_1, dsc1_1, dg1_1, f1["dsh"], f1["dsc"], f1["dgate"]], axis=1)[0]])
    small = dict(
        loss=lcols,
        pre_mix_g=jnp.concatenate([dpremix0, dpremix1]), post_mix_g=jnp.concatenate([dpostmix0, dpostmix1]),
        pre_ffn_g=jnp.concatenate([f0["dpre"], f1["dpre"]]), post_ffn_g=jnp.concatenate([f0["dpost"], f1["dpost"]]),
        lb=dlb, hgrn_gnorm_g=d_gn, ffn_dw_b=jnp.concatenate([f0["d_dw_b"], f1["d_dw_b"]]), dmod=dmod,
        conv_b_in=d_b_in, conv_dw_w=d_dw_w[0:CONV_W], conv_dw_b=d_dw_b, conv_ln_g=d_ln_g, conv_ln_b=d_ln_b,
        conv_b_out=d_b_out, ffn_dw_w=jnp.stack([f0["d_dw_w"], f1["d_dw_w"]]))
    big = [g_hgrn_in, g_hgrn_out, g_conv_in, g_conv_out, f0["g_up"], f1["g_up"], f0["g_down"], f1["g_down"]]
    return dx0, small, big


def _pack(parts, rows=8):
    flat = jnp.concatenate([p.reshape(-1).astype(F32) for p in parts])
    per = rows * 128
    pad = (-flat.shape[0]) % per
    return jnp.pad(flat, (0, pad)).reshape(rows, -1)


def _unpack(flat, shapes):
    out, off = [], 0
    for s in shapes:
        n = 1
        for d in s:
            n *= d
        out.append(flat[..., off:off + n].reshape(flat.shape[:-1] + tuple(s)))
        off += n
    return out


def _from_chips(stacked, axis):
    moved = jnp.moveaxis(stacked, 0, axis)
    shape = list(moved.shape)
    return moved.reshape(shape[:axis] + [shape[axis] * shape[axis + 1]] + shape[axis + 2:])


def _my_shard(full, axis, q):
    n = full.shape[axis] // N_CHIPS
    return lax.dynamic_slice_in_dim(full, q * n, n, axis=axis)


def kernel(x, c, ada_w, ada_b, pre_mix_g, post_mix_g, pre_ffn_g, post_ffn_g, hgrn_w_in, hgrn_lb_logits, hgrn_gnorm_g, hgrn_w_out, conv_w_in, conv_b_in, conv_dw_w, conv_dw_b, conv_ln_g, conv_ln_b, conv_w_out, conv_b_out, ffn_w_up, ffn_dw_w, ffn_dw_b, ffn_w_down, loss_target, m_ada_w, m_ada_b, m_pre_mix_g, m_post_mix_g, m_pre_ffn_g, m_post_ffn_g, m_hgrn_w_in, m_hgrn_lb_logits, m_hgrn_gnorm_g, m_hgrn_w_out, m_conv_w_in, m_conv_b_in, m_conv_dw_w, m_conv_dw_b, m_conv_ln_g, m_conv_ln_b, m_conv_w_out, m_conv_b_out, m_ffn_w_up, m_ffn_dw_w, m_ffn_dw_b, m_ffn_w_down, v_ada_w, v_ada_b, v_pre_mix_g, v_post_mix_g, v_pre_ffn_g, v_post_ffn_g, v_hgrn_w_in, v_hgrn_lb_logits, v_hgrn_gnorm_g, v_hgrn_w_out, v_conv_w_in, v_conv_b_in, v_conv_dw_w, v_conv_dw_b, v_conv_ln_g, v_conv_ln_b, v_conv_w_out, v_conv_b_out, v_ffn_w_up, v_ffn_dw_w, v_ffn_dw_b, v_ffn_w_down):
    W = dict(ada_w=ada_w, ada_b=ada_b, pre_mix_g=pre_mix_g, post_mix_g=post_mix_g, pre_ffn_g=pre_ffn_g,
             post_ffn_g=post_ffn_g, hgrn_w_in=hgrn_w_in, hgrn_lb_logits=hgrn_lb_logits, hgrn_gnorm_g=hgrn_gnorm_g,
             hgrn_w_out=hgrn_w_out, conv_w_in=conv_w_in, conv_b_in=conv_b_in, conv_dw_w=conv_dw_w,
             conv_dw_b=conv_dw_b, conv_ln_g=conv_ln_g, conv_ln_b=conv_ln_b, conv_w_out=conv_w_out,
             conv_b_out=conv_b_out, ffn_w_up=ffn_w_up, ffn_dw_w=ffn_dw_w, ffn_dw_b=ffn_dw_b, ffn_w_down=ffn_w_down)
    M = dict(ada_w=m_ada_w, ada_b=m_ada_b, pre_mix_g=m_pre_mix_g, post_mix_g=m_post_mix_g, pre_ffn_g=m_pre_ffn_g,
             post_ffn_g=m_post_ffn_g, hgrn_w_in=m_hgrn_w_in, hgrn_lb_logits=m_hgrn_lb_logits,
             hgrn_gnorm_g=m_hgrn_gnorm_g, hgrn_w_out=m_hgrn_w_out, conv_w_in=m_conv_w_in, conv_b_in=m_conv_b_in,
             conv_dw_w=m_conv_dw_w, conv_dw_b=m_conv_dw_b, conv_ln_g=m_conv_ln_g, conv_ln_b=m_conv_ln_b,
             conv_w_out=m_conv_w_out, conv_b_out=m_conv_b_out, ffn_w_up=m_ffn_w_up, ffn_dw_w=m_ffn_dw_w,
             ffn_dw_b=m_ffn_dw_b, ffn_w_down=m_ffn_w_down)
    V = dict(ada_w=v_ada_w, ada_b=v_ada_b, pre_mix_g=v_pre_mix_g, post_mix_g=v_post_mix_g, pre_ffn_g=v_pre_ffn_g,
             post_ffn_g=v_post_ffn_g, hgrn_w_in=v_hgrn_w_in, hgrn_lb_logits=v_hgrn_lb_logits,
             hgrn_gnorm_g=v_hgrn_gnorm_g, hgrn_w_out=v_hgrn_w_out, conv_w_in=v_conv_w_in, conv_b_in=v_conv_b_in,
             conv_dw_w=v_conv_dw_w, conv_dw_b=v_conv_dw_b, conv_ln_g=v_conv_ln_g, conv_ln_b=v_conv_ln_b,
             conv_w_out=v_conv_w_out, conv_b_out=v_conv_b_out, ffn_w_up=v_ffn_w_up, ffn_dw_w=v_ffn_dw_w,
             ffn_dw_b=v_ffn_dw_b, ffn_w_down=v_ffn_w_down)
    names = list(W)
    xi, yi, ci = lax.axis_index("x"), lax.axis_index("y"), lax.axis_index("c")
    q = 2 * xi + yi
    me = 2 * q + ci
    D = x.shape[-1]
    L = ada_w.shape[0]

    small_w = ["conv_b_in", "conv_dw_w", "conv_dw_b", "conv_ln_g", "conv_ln_b", "conv_b_out", "ffn_dw_w"]
    small_axis = dict(conv_b_in=1, conv_dw_w=2, conv_dw_b=1, conv_ln_g=1, conv_ln_b=1, conv_b_out=1, ffn_dw_w=2)
    packed = _pack([c] + [W[n] for n in small_w])

    def halves(w):
        shard = w.astype(BF16).reshape(1, 2, w.shape[0] // 2, w.shape[1])
        buf = lax.empty((N_CHIPS,) + shard.shape[1:], BF16)
        return lax.dynamic_update_slice_in_dim(buf, shard, q, axis=0)

    hg_send, hg_recv, hg_bufs, hg_token = allgather_chips_start([halves(hgrn_w_in[0]), halves(hgrn_w_out[0])],
                                                                name="gather_hgrn_weights_start")
    packed, _ = lax.optimization_barrier((packed, hg_token))
    gathered = allgather_devices(packed, name="gather_small_params").reshape(N_DEV, -1)
    c_all = gathered[:, 0:D]
    per_chip = gathered.reshape(N_CHIPS, 2, -1)[:, 0, D:]
    parts = _unpack(per_chip, [W[n].shape for n in small_w])
    P = {n: _from_chips(p, small_axis[n]) for n, p in zip(small_w, parts)}
    P["conv_dw_w"] = P["conv_dw_w"][0]
    for n in ("pre_mix_g", "post_mix_g", "pre_ffn_g", "post_ffn_g", "hgrn_lb_logits", "hgrn_gnorm_g", "ffn_dw_b"):
        P[n] = W[n]

    modp = ada_mod(c_all, ada_w, name="ada_mod")
    ncol = modp.shape[-1]
    mod_all = allgather_devices(modp.reshape(L * N_DEV, ncol), name="gather_mod")
    mod_all = mod_all.reshape(N_CHIPS, 2, L, N_DEV, ncol)[:, 0]
    mod_me = lax.dynamic_index_in_dim(mod_all, me, axis=2, keepdims=False)
    mod = mod_me.transpose(1, 0, 2).reshape(L, N_CHIPS * ncol) + ada_b
    mods = [tuple(mod[l:l + 1, k * D:(k + 1) * D] for k in range(6)) for l in range(L)]

    stack = lambda t: t.reshape(N_CHIPS, t.shape[1] * t.shape[2], t.shape[3])
    rowsh = lambda t: t.reshape(1, N_CHIPS * t.shape[1] * t.shape[2], t.shape[3])
    pairs = lambda t: t.reshape(2, 2, t.shape[1], t.shape[2]).transpose(0, 2, 1, 3).reshape(2, t.shape[1], 2 * t.shape[2])
    g = forward_to_sibling(allgather_chips_wait(hg_send, hg_recv, hg_bufs, mod, name="gather_hgrn_weights_wait"),
                           name="gather_hgrn_weights_forward")
    P["hgrn_w_in"], P["hgrn_w_out"] = stack(g[0]), rowsh(g[1])
    late_shards = [conv_w_in[0], conv_w_out[0], ffn_w_up[0], ffn_w_up[1], ffn_w_down[0], ffn_w_down[1]]
    late_bufs, _, _ = lax.optimization_barrier(([halves(w) for w in late_shards], g, mod))
    send_sems, recv_sems, bufs, token = allgather_chips_start(late_bufs, name="gather_weights_start")
    mods[0] = tuple(m + token[0:1, 0:1] for m in mods[0])

    def late_weights(x1):
        landed = allgather_chips_wait(send_sems, recv_sems, bufs, x1, name="gather_weights_wait")
        g = forward_to_sibling(landed, name="gather_weights_forward")
        return dict(conv_w_in=pairs(stack(g[0])), conv_w_out=rowsh(g[1]), ffn_w_up=[stack(g[2]), stack(g[3])],
                    ffn_w_down=[rowsh(g[4]), rowsh(g[5])])

    c_idx = ci.astype(jnp.int32).reshape(1)
    pending, in_flight = {}, {}

    def chip_stage(after):
        tag, (send, recv, grads, landing) = pending.popitem()
        grads, others = exchange_wait(send, recv, grads, landing, _pair_copies, after, name=f"grad_pair_wait_{tag}")
        sums = [pair_add(g_, o_, c_idx, name=f"grad_pair_add_{tag}_{a}") for a, (g_, o_) in enumerate(zip(grads, others))]
        landing = [lax.empty((3,) + s_.shape[1:], s_.dtype) for s_ in sums]
        send, recv, sums, landing, tok = exchange_start(sums, landing, _chip_copies, 3 * len(sums),
                                                        name=f"grad_chip_exchange_start_{tag}")
        in_flight[tag] = (send, recv, sums, landing)
        return tok

    def grads_ready(tag, grads):
        tok = chip_stage(grads[0]) if pending else 0.0
        landing = [lax.empty(g_.shape[1:], g_.dtype) for g_ in grads]
        send, recv, grads, landing, tok2 = exchange_start(grads, landing, _pair_copies, len(grads),
                                                          name=f"grad_pair_start_{tag}")
        pending[tag] = (send, recv, grads, landing)
        return tok + tok2

    grad_x, small, big = _local_step(x[0], loss_target[0], mods, P, late_weights, grads_ready)

    small_names = list(small)
    packed_g = _pack([small[n] for n in small_names])
    gs_buf = lax.dynamic_update_slice_in_dim(lax.empty((N_DEV,) + packed_g.shape, F32), packed_g[None], me, axis=0)
    sg_send, sg_recv, gs_buf, _, tok_sg = exchange_start([gs_buf], [], _device_copies, N_DEV - 1,
                                                         name="gather_small_grads_start")

    tok_hg = chip_stage(tok_sg)
    G = {}
    halves = []
    for tag in ("f0", "l1"):
        sums_t, landed_t = exchange_wait(*in_flight[tag], _chip_copies, grad_x, name=f"grad_chip_exchange_wait_{tag}")
        halves += finish_reduce(sums_t, landed_t, q, ci, f"{tag}_")
    red = [f.reshape(2 * f.shape[1], f.shape[2]) for f in half_swap(halves, name="grad_half_swap")]
    G["conv_w_in"], G["conv_w_out"] = red[2][None], red[3][None]
    G["ffn_w_up"] = jnp.stack([red[0], red[4]])
    G["ffn_w_down"] = jnp.stack([red[1], red[5]])

    delta, new_m, new_v = {}, {}, {}

    def adamw_matrix(n, after=None):
        shp = W[n].shape
        two = lambda t: t.reshape(-1, shp[-1])
        d_, m_, v_ = adamw(two(W[n]), two(G[n]), two(M[n]), two(V[n]), name=f"adamw_{n}", after=after)
        delta[n], new_m[n], new_v[n] = d_.reshape(shp), m_.reshape(shp), v_.reshape(shp)

    big_names = ["ada_w", "hgrn_w_in", "hgrn_w_out", "conv_w_in", "conv_w_out", "ffn_w_up", "ffn_w_down"]
    for n in ("conv_w_in", "conv_w_out", "ffn_w_up", "ffn_w_down"):
        adamw_matrix(n, after=tok_hg)

    (gs,), _ = exchange_wait(sg_send, sg_recv, gs_buf, [], _device_copies, delta["ffn_w_down"],
                             name="gather_small_grads_wait")
    dmod_all = _unpack(gs.reshape(N_DEV, -1), [small[n].shape for n in small_names])[small_names.index("dmod")]
    tot = sum_devices(gs, name="sum_small_grads").reshape(1, -1)
    S = dict(zip(small_names, _unpack(tot, [small[n].shape for n in small_names])))
    S = {n: v[0] for n, v in S.items()}
    loss = 0.5 * jnp.sum(S["loss"]) / D

    dmod_q = lax.dynamic_slice_in_dim(dmod_all, q * ncol, ncol, axis=2)
    G["ada_w"] = ada_wgrad(c_all.T, dmod_q.transpose(1, 0, 2), name="ada_wgrad")
    G["ada_b"] = S["dmod"]
    for n in ("pre_mix_g", "post_mix_g", "pre_ffn_g", "post_ffn_g", "hgrn_gnorm_g", "ffn_dw_b"):
        G[n] = S[n]
    G["hgrn_lb_logits"] = lb_logits_grad(hgrn_lb_logits, S["lb"], name="lb_logits_grad")
    G["conv_b_in"] = _my_shard(S["conv_b_in"], 1, q)
    G["conv_dw_w"] = _my_shard(S["conv_dw_w"], 1, q)[None]
    for n in ("conv_dw_b", "conv_ln_g", "conv_ln_b", "conv_b_out"):
        G[n] = _my_shard(S[n], 1, q)
    G["ffn_dw_w"] = _my_shard(S["ffn_dw_w"], 2, q)
    adamw_matrix("ada_w")

    sums_h, landed_h = exchange_wait(*in_flight["hg"], _chip_copies, delta["ada_w"], name="grad_chip_exchange_wait_hg")
    red_h = half_swap(finish_reduce(sums_h, landed_h, q, ci, "hg_"), name="grad_half_swap_hg")
    G["hgrn_w_in"], G["hgrn_w_out"] = [f.reshape(1, 2 * f.shape[1], f.shape[2]) for f in red_h]
    for n in ("hgrn_w_in", "hgrn_w_out"):
        adamw_matrix(n)
    rest = [n for n in names if n not in big_names]
    d_, m_, v_ = adamw(_pack([W[n] for n in rest]), _pack([G[n] for n in rest]), _pack([M[n] for n in rest]),
                       _pack([V[n] for n in rest]), name="adamw_small")
    shapes = [W[n].shape for n in rest]
    for n, a, b_, c_ in zip(rest, _unpack(d_.reshape(-1), shapes), _unpack(m_.reshape(-1), shapes),
                            _unpack(v_.reshape(-1), shapes)):
        delta[n], new_m[n], new_v[n] = a, b_, c_

    return (loss, grad_x[None], *[G[n].reshape(W[n].shape) for n in names], *[delta[n] for n in names],
            *[new_m[n] for n in names], *[new_v[n] for n in names])
```

```python
import jax
import jax.numpy as jnp
from jax import lax
from jax.experimental import pallas as pl
from jax.experimental.pallas import tpu as pltpu

F32 = jnp.float32
BF16 = jnp.bfloat16
EPS = 1e-6
HEAD = 128
BLK = 16
NEG = -1e30
CONV_W = 31
FFN_W = 3
N_CHIPS = 4
N_DEV = 8
SUB = 8
LANE = 128
V7X_VMEM_LIMIT = 56 * 1024 * 1024
MESH = pl.DeviceIdType.MESH
HBM = pl.BlockSpec(memory_space=pltpu.HBM)
VMEM_SPEC = pl.BlockSpec(memory_space=pltpu.VMEM)

ADAM_LR = 0.001
ADAM_B1 = 0.9
ADAM_B2 = 0.999
ADAM_EPS = 1e-08
ADAM_WD = 0.01
ADAM_STEP = 10


def _cp(*sem):
    return pltpu.CompilerParams(dimension_semantics=sem, vmem_limit_bytes=V7X_VMEM_LIMIT)


def _sig(x):
    return 0.5 * jnp.tanh(0.5 * x) + 0.5


def _silu(x):
    return x * _sig(x)


def _dsilu(x):
    s = _sig(x)
    return s * (1.0 + x * (1.0 - s))


def _dot(a, b):
    return jnp.dot(a, b, preferred_element_type=F32)


def _dot_nt(a, b):
    return lax.dot_general(a, b, (((1,), (1,)), ((), ())), preferred_element_type=F32)


def _dot_tn(a, b):
    return lax.dot_general(a, b, (((0,), (0,)), ((), ())), preferred_element_type=F32)


def _colsum(x):
    return jnp.sum(x, axis=0, keepdims=True)


def _rowmean(x):
    return jnp.mean(x, axis=-1, keepdims=True)


def _ffn_perm(j):
    return (j % 2) * 2 + j // 2


def _tile(n, pref):
    if n <= pref:
        return n
    t = pref - pref % 8
    while n % t:
        t -= 8
    return t


def mm_nn(a, w, *, name, bias=None, out_dtype=F32, perm=None, tm=1024):
    T, K = a.shape
    J, _, nb = w.shape
    tm = min(tm, T)
    col = (lambda j: j) if perm is None else perm

    def body(a_ref, w_ref, *rest):
        acc = _dot(a_ref[...], w_ref[...])
        if bias is not None:
            acc = acc + rest[0][...]
        rest[-1][...] = acc.astype(out_dtype)

    in_specs = [pl.BlockSpec((tm, K), lambda j, i: (i, 0)), pl.BlockSpec((None, K, nb), lambda j, i: (j, 0, 0))]
    args = [a, w]
    if bias is not None:
        in_specs.append(pl.BlockSpec((1, nb), lambda j, i: (0, j)))
        args.append(bias)
    return pl.pallas_call(
        body, grid=(J, T // tm), in_specs=in_specs,
        out_specs=pl.BlockSpec((tm, nb), lambda j, i: (i, col(j))),
        out_shape=jax.ShapeDtypeStruct((T, J * nb), out_dtype), name=name,
        compiler_params=_cp("parallel", "parallel"))(*args)


def mm_nt(a, w, *, name, out_dtype=F32, perm=None, tm=1024, after=None):
    T = a.shape[0]
    J, K, nb = w.shape
    tm = min(tm, T)
    col = (lambda j: j) if perm is None else perm
    deps = [] if after is None else [after]

    def body(a_ref, w_ref, *rest):
        o_ref, acc_ref = rest[len(deps):]
        j = pl.program_id(1)

        @pl.when(j == 0)
        def _():
            acc_ref[...] = jnp.zeros_like(acc_ref)

        acc_ref[...] += _dot_nt(a_ref[...], w_ref[...])

        @pl.when(j == J - 1)
        def _():
            o_ref[...] = acc_ref[...].astype(out_dtype)

    return pl.pallas_call(
        body, grid=(T // tm, J),
        in_specs=[pl.BlockSpec((tm, nb), lambda i, j: (i, col(j))), pl.BlockSpec((None, K, nb), lambda i, j: (j, 0, 0))]
        + [pl.BlockSpec(memory_space=pl.ANY)] * len(deps),
        out_specs=pl.BlockSpec((tm, K), lambda i, j: (i, 0)),
        out_shape=jax.ShapeDtypeStruct((T, K), out_dtype),
        scratch_shapes=[pltpu.VMEM((tm, K), F32)], name=name,
        compiler_params=_cp("parallel", "arbitrary"))(a, w, *deps)


def mm_tn(a, b, *, name, J, block, row_chips=1, col_chips=1, perm=None, tk=1024):
    T = a.shape[0]
    tk = min(tk, T)
    col = (lambda j: j) if perm is None else perm
    if block == "b":
        rows, nb = a.shape[1], b.shape[1] // J
        a_spec = pl.BlockSpec((tk, rows), lambda j, t: (t, 0))
        b_spec = pl.BlockSpec((tk, nb), lambda j, t: (t, col(j)))
    else:
        rows, nb = a.shape[1] // J, b.shape[1]
        a_spec = pl.BlockSpec((tk, rows), lambda j, t: (t, col(j)))
        b_spec = pl.BlockSpec((tk, nb), lambda j, t: (t, 0))
    rh = rows // (2 * row_chips)
    nc = nb // col_chips
    chips = [(rc, cc) for rc in range(row_chips) for cc in range(col_chips)]

    def body(a_ref, b_ref, o_ref):
        @pl.when(pl.program_id(1) == 0)
        def _():
            o_ref[...] = jnp.zeros_like(o_ref)

        acc = _dot_tn(a_ref[...], b_ref[...])
        for ch, (rc, cc) in enumerate(chips):
            for hf in range(2):
                r0 = (rc * 2 + hf) * rh
                o_ref[hf, ch] += acc[r0:r0 + rh, cc * nc:(cc + 1) * nc]

    return pl.pallas_call(
        body, grid=(J, T // tk), in_specs=[a_spec, b_spec],
        out_specs=pl.BlockSpec((2, len(chips), rh, nc), lambda j, t: (0, j, 0, 0)),
        out_shape=jax.ShapeDtypeStruct((2, J * len(chips), rh, nc), F32), name=name,
        compiler_params=_cp("parallel", "arbitrary"))(a, b)


def mm_nt_parts(s3, g, w, *, name, tm=1024, after=None):
    n3, T, nb = s3.shape
    J, K, _ = w.shape
    tm = min(tm, T)
    deps = [] if after is None else [after]

    def body(s_ref, g_ref, w_ref, *rest):
        o_ref, acc_ref = rest[len(deps):]
        j = pl.program_id(1)

        @pl.when(j == 0)
        def _():
            acc_ref[...] = jnp.zeros_like(acc_ref)

        @pl.when(j < n3)
        def _():
            acc_ref[...] += _dot_nt(s_ref[...], w_ref[...])

        @pl.when(j == n3)
        def _():
            acc_ref[...] += _dot_nt(g_ref[...], w_ref[...])

        @pl.when(j == J - 1)
        def _():
            o_ref[...] = acc_ref[...]

    return pl.pallas_call(
        body, grid=(T // tm, J),
        in_specs=[pl.BlockSpec((None, tm, nb), lambda i, j: (jnp.minimum(j, n3 - 1), i, 0)),
                  pl.BlockSpec((tm, nb), lambda i, j: (i, 0)), pl.BlockSpec((None, K, nb), lambda i, j: (j, 0, 0))]
        + [pl.BlockSpec(memory_space=pl.ANY)] * len(deps),
        out_specs=pl.BlockSpec((tm, K), lambda i, j: (i, 0)), out_shape=jax.ShapeDtypeStruct((T, K), F32),
        scratch_shapes=[pltpu.VMEM((tm, K), F32)], name=name,
        compiler_params=_cp("parallel", "arbitrary"))(s3, g, w, *deps)


def mm_tn_parts(a, s3, g, *, name, tk=1024):
    n3, T, nb = s3.shape
    J = n3 + 1
    tk = min(tk, T)
    rows = a.shape[1]
    rh = rows // 2

    def body(a_ref, s_ref, g_ref, o_ref):
        j = pl.program_id(0)

        @pl.when(pl.program_id(1) == 0)
        def _():
            o_ref[...] = jnp.zeros_like(o_ref)

        def add(b_ref):
            acc = _dot_tn(a_ref[...], b_ref[...])
            for hf in range(2):
                o_ref[hf, 0] += acc[hf * rh:(hf + 1) * rh, :]

        pl.when(j < n3)(lambda: add(s_ref))
        pl.when(j == n3)(lambda: add(g_ref))

    return pl.pallas_call(
        body, grid=(J, T // tk),
        in_specs=[pl.BlockSpec((tk, rows), lambda j, t: (t, 0)),
                  pl.BlockSpec((None, tk, nb), lambda j, t: (jnp.minimum(j, n3 - 1), t, 0)),
                  pl.BlockSpec((tk, nb), lambda j, t: (t, 0))],
        out_specs=pl.BlockSpec((2, 1, rh, nb), lambda j, t: (0, j, 0, 0)),
        out_shape=jax.ShapeDtypeStruct((2, J, rh, nb), F32), name=name,
        compiler_params=_cp("parallel", "arbitrary"))(a, s3, g)


def _row(tm, w):
    return pl.BlockSpec((tm, w), lambda i: (i, 0))


def _full(r, w):
    return pl.BlockSpec((r, w), lambda i: (0, 0))


def _acc_init(i, *refs):
    @pl.when(i == 0)
    def _():
        for r in refs:
            r[...] = jnp.zeros_like(r)


def prenorm(x, g, sc, sh, *, name, tm=512):
    T, D = x.shape
    tm = min(tm, T)

    def body(x_ref, g_ref, sc_ref, sh_ref, h_ref):
        xv = x_ref[...]
        r = lax.rsqrt(_rowmean(xv * xv) + EPS)
        h_ref[...] = ((xv * r) * g_ref[...] * (1.0 + sc_ref[...]) + sh_ref[...]).astype(BF16)

    return pl.pallas_call(
        body, grid=(T // tm,), in_specs=[_row(tm, D), _full(1, D), _full(1, D), _full(1, D)],
        out_specs=_row(tm, D), out_shape=jax.ShapeDtypeStruct((T, D), BF16), name=name,
        compiler_params=_cp("parallel"))(x, g, sc, sh)


def post_residual_prenorm(x, y, g, gate, g2, sc2, sh2, *, name, tm=512):
    T, D = x.shape
    tm = min(tm, T)

    def body(x_ref, y_ref, g_ref, gate_ref, g2_ref, sc2_ref, sh2_ref, o_ref, h_ref):
        yv = y_ref[...]
        r = lax.rsqrt(_rowmean(yv * yv) + EPS)
        out = x_ref[...] + gate_ref[...] * ((yv * r) * g_ref[...])
        o_ref[...] = out
        r2 = lax.rsqrt(_rowmean(out * out) + EPS)
        h_ref[...] = ((out * r2) * g2_ref[...] * (1.0 + sc2_ref[...]) + sh2_ref[...]).astype(BF16)

    return pl.pallas_call(
        body, grid=(T // tm,), in_specs=[_row(tm, D), _row(tm, D)] + [_full(1, D)] * 5,
        out_specs=[_row(tm, D), _row(tm, D)],
        out_shape=[jax.ShapeDtypeStruct((T, D), F32), jax.ShapeDtypeStruct((T, D), BF16)], name=name,
        compiler_params=_cp("parallel"))(x, y, g, gate, g2, sc2, sh2)


def post_residual_loss(x, y, g, gate, tgt, *, name, tm=512):
    T, D = x.shape
    tm = min(tm, T)

    def body(x_ref, y_ref, g_ref, gate_ref, t_ref, dx_ref, l_ref, dy_ref, dgate_ref, dg_ref):
        _acc_init(pl.program_id(0), l_ref, dgate_ref, dg_ref)
        yv = y_ref[...]
        r = lax.rsqrt(_rowmean(yv * yv) + EPS)
        yn = yv * r
        gv = g_ref[...]
        gt = gate_ref[...]
        e = x_ref[...] + gt * (yn * gv) - t_ref[...]
        dxv = e * (1.0 / D)
        dx_ref[...] = dxv
        l_ref[...] += _colsum(e * e)
        dgate_ref[...] += _colsum(dxv * (yn * gv))
        dg_ref[...] += _colsum(dxv * gt * yn)
        dyn = dxv * gt * gv
        dy_ref[...] = (r * (dyn - yn * _rowmean(dyn * yn))).astype(BF16)

    vec = jax.ShapeDtypeStruct((1, D), F32)
    return pl.pallas_call(
        body, grid=(T // tm,), in_specs=[_row(tm, D), _row(tm, D), _full(1, D), _full(1, D), _row(tm, D)],
        out_specs=[_row(tm, D), _full(1, D), _row(tm, D), _full(1, D), _full(1, D)],
        out_shape=[jax.ShapeDtypeStruct((T, D), F32), vec, jax.ShapeDtypeStruct((T, D), BF16), vec, vec], name=name,
        compiler_params=_cp("arbitrary"))(x, y, g, gate, tgt)


def prenorm_bwd(dh, x, dres, g, sc, *, name, tm=512):
    T, D = x.shape
    tm = min(tm, T)

    def body(dh_ref, x_ref, dres_ref, g_ref, sc_ref, dx_ref, dsh_ref, dsc_ref, dg_ref):
        _acc_init(pl.program_id(0), dsh_ref, dsc_ref, dg_ref)
        xv = x_ref[...]
        dhv = dh_ref[...]
        r = lax.rsqrt(_rowmean(xv * xv) + EPS)
        xn = xv * r
        gv = g_ref[...]
        one_sc = 1.0 + sc_ref[...]
        dsh_ref[...] += _colsum(dhv)
        dsc_ref[...] += _colsum(dhv * (xn * gv))
        dg_ref[...] += _colsum(dhv * one_sc * xn)
        dxn = dhv * one_sc * gv
        dx_ref[...] = dres_ref[...] + r * (dxn - xn * _rowmean(dxn * xn))

    return pl.pallas_call(
        body, grid=(T // tm,), in_specs=[_row(tm, D), _row(tm, D), _row(tm, D), _full(1, D), _full(1, D)],
        out_specs=[_row(tm, D), _full(1, D), _full(1, D), _full(1, D)],
        out_shape=[jax.ShapeDtypeStruct((T, D), F32)] + [jax.ShapeDtypeStruct((1, D), F32)] * 3, name=name,
        compiler_params=_cp("arbitrary"))(dh, x, dres, g, sc)


def prenorm_post_bwd(dh, x, dres, g, sc, y, g_post, gate, *, name, tm=512):
    T, D = x.shape
    tm = min(tm, T)

    def body(dh_ref, x_ref, dres_ref, g_ref, sc_ref, y_ref, gp_ref, gate_ref,
             dx_ref, dsh_ref, dsc_ref, dg_ref, dy_ref, dgate_ref, dgp_ref, dbias_ref):
        _acc_init(pl.program_id(0), dsh_ref, dsc_ref, dg_ref, dgate_ref, dgp_ref, dbias_ref)
        xv = x_ref[...]
        dhv = dh_ref[...]
        r = lax.rsqrt(_rowmean(xv * xv) + EPS)
        xn = xv * r
        gv = g_ref[...]
        one_sc = 1.0 + sc_ref[...]
        dsh_ref[...] += _colsum(dhv)
        dsc_ref[...] += _colsum(dhv * (xn * gv))
        dg_ref[...] += _colsum(dhv * one_sc * xn)
        dxn = dhv * one_sc * gv
        dxv = dres_ref[...] + r * (dxn - xn * _rowmean(dxn * xn))
        dx_ref[...] = dxv
        yv = y_ref[...]
        ry = lax.rsqrt(_rowmean(yv * yv) + EPS)
        yn = yv * ry
        gp = gp_ref[...]
        gt = gate_ref[...]
        dgate_ref[...] += _colsum(dxv * (yn * gp))
        dgp_ref[...] += _colsum(dxv * gt * yn)
        dyn = dxv * gt * gp
        dy = ry * (dyn - yn * _rowmean(dyn * yn))
        dbias_ref[...] += _colsum(dy)
        dy_ref[...] = dy.astype(BF16)

    vec = jax.ShapeDtypeStruct((1, D), F32)
    return pl.pallas_call(
        body, grid=(T // tm,),
        in_specs=[_row(tm, D)] * 3 + [_full(1, D)] * 2 + [_row(tm, D)] + [_full(1, D)] * 2,
        out_specs=[_row(tm, D)] + [_full(1, D)] * 3 + [_row(tm, D)] + [_full(1, D)] * 3,
        out_shape=[jax.ShapeDtypeStruct((T, D), F32), vec, vec, vec, jax.ShapeDtypeStruct((T, D), BF16), vec, vec, vec],
        name=name, compiler_params=_cp("arbitrary"))(dh, x, dres, g, sc, y, g_post, gate)


HALO = 16


def _shift_helpers():
    rid = lax.broadcasted_iota(jnp.int32, (SUB, LANE), 0)

    def down(cur, prev, k):
        return pltpu.roll(jnp.where(rid >= SUB - k, prev, cur), k, 0)

    def up(cur, nxt, k):
        return pltpu.roll(jnp.where(rid < k, nxt, cur), SUB - k, 0)

    return down, up


def _ffn_sides(c, nb, wa_ref, wb_ref, ba_ref, bb_ref):
    cols = slice(c * LANE, (c + 1) * LANE)
    return [(cols, [wa_ref[k:k + 1, cols] for k in range(FFN_W)], ba_ref[:, cols]),
            (slice(nb + c * LANE, nb + (c + 1) * LANE), [wb_ref[k:k + 1, cols] for k in range(FFN_W)],
             bb_ref[:, cols])]


def _ffn_specs(tm, nb, hb, idx):
    return [pl.BlockSpec((tm, 2 * nb), lambda jc, i: (idx(i), jc)),
            pl.BlockSpec((HALO, 2 * nb), lambda jc, i: (jnp.maximum(idx(i) * hb - 1, 0), jc)),
            pl.BlockSpec((FFN_W, nb), lambda jc, i: (0, jc)),
            pl.BlockSpec((FFN_W, nb), lambda jc, i: (0, jc + 2)),
            pl.BlockSpec((1, nb), lambda jc, i: (0, jc)),
            pl.BlockSpec((1, nb), lambda jc, i: (0, jc + 2))]


def ffn_act(u0p, dw_w, dw_b, *, name, tm=256):
    T, W = u0p.shape
    nb = W // 4
    tm = min(tm, T)
    unroll = 4
    rows16 = 2 * SUB

    def body(u_ref, halo_ref, wa_ref, wb_ref, ba_ref, bb_ref, z_ref, ab_ref):
        i = pl.program_id(1)
        down, _ = _shift_helpers()
        for c in range(nb // LANE):
            cols = slice(c * LANE, (c + 1) * LANE)
            side = _ffn_sides(c, nb, wa_ref, wb_ref, ba_ref, bb_ref)

            def rows(j, prev):
                prev = list(prev)
                for m in range(unroll):
                    r0 = pl.multiple_of((j * unroll + m) * rows16, rows16)
                    x = [u_ref[pl.ds(r0, rows16), cs].astype(F32) for cs, _, _ in side]
                    conv = [[None, None], [None, None]]
                    for hf in range(2):
                        for n, (_, w, b) in enumerate(side):
                            cur = x[n][hf * SUB:(hf + 1) * SUB, :]
                            conv[n][hf] = b + w[2] * cur + w[1] * down(cur, prev[n], 1) + w[0] * down(cur, prev[n], 2)
                            prev[n] = cur
                    a, b = [jnp.concatenate(conv[n], axis=0) for n in range(2)]
                    z_ref[pl.ds(r0, rows16), cols] = (_silu(a) * b).astype(BF16)
                    ab_ref[pl.ds(r0, rows16), side[0][0]] = a.astype(BF16)
                    ab_ref[pl.ds(r0, rows16), side[1][0]] = b.astype(BF16)
                return tuple(prev)

            first = [jnp.where(i == 0, 0.0, halo_ref[:, cs].astype(F32)[SUB:2 * SUB, :]) for cs, _, _ in side]
            lax.fori_loop(0, tm // (rows16 * unroll), rows, tuple(first))

    return pl.pallas_call(
        body, grid=(2, T // tm), in_specs=_ffn_specs(tm, nb, tm // HALO, lambda i: i),
        out_specs=[pl.BlockSpec((tm, nb), lambda jc, i: (i, jc)), pl.BlockSpec((tm, 2 * nb), lambda jc, i: (i, jc))],
        out_shape=[jax.ShapeDtypeStruct((T, 2 * nb), BF16), jax.ShapeDtypeStruct((T, W), BF16)], name=name,
        compiler_params=_cp("parallel", "arbitrary"))(u0p, u0p, dw_w, dw_w, dw_b, dw_b)


def ffn_act_bwd(dz, u0p, ab, dw_w, *, name, tm=256):
    T, W = u0p.shape
    nb = W // 4
    tm = min(tm, T)
    nt = T // tm
    unroll = 4
    rows16 = 2 * SUB
    n_it = tm // (rows16 * unroll)

    def body(dz_ref, u_ref, ab_ref, wa_ref, wb_ref, du0_ref, dw_ref, carry):
        i = pl.program_id(1)
        _acc_init(i, dw_ref)
        _, up = _shift_helpers()
        for c in range(nb // LANE):
            cols = slice(c * LANE, (c + 1) * LANE)
            side = [(cols, [wa_ref[k:k + 1, cols] for k in range(FFN_W)]),
                    (slice(nb + c * LANE, nb + (c + 1) * LANE), [wb_ref[k:k + 1, cols] for k in range(FFN_W)])]

            def rows(j, st):
                nxt, acc = list(st[0:2]), list(st[2:10])
                for m in range(unroll):
                    r0 = pl.multiple_of(((n_it - 1 - j) * unroll + unroll - 1 - m) * rows16, rows16)
                    dzv = dz_ref[pl.ds(r0, rows16), cols].astype(F32)
                    a, b = [ab_ref[pl.ds(r0, rows16), cs].astype(F32) for cs, _ in side]
                    x = [u_ref[pl.ds(r0, rows16), cs].astype(F32) for cs, _ in side]
                    sa = _sig(a)
                    d16 = [dzv * b * (sa * (1.0 + a * (1.0 - sa))), dzv * (a * sa)]
                    out = [[None, None], [None, None]]
                    for hf in (1, 0):
                        half = slice(hf * SUB, (hf + 1) * SUB)
                        for n in range(2):
                            w = side[n][1]
                            d = d16[n][half, :]
                            u = x[n][half, :]
                            up1, up2 = up(d, nxt[n], 1), up(d, nxt[n], 2)
                            acc[4 * n + 0] = acc[4 * n + 0] + up2 * u
                            acc[4 * n + 1] = acc[4 * n + 1] + up1 * u
                            acc[4 * n + 2] = acc[4 * n + 2] + d * u
                            acc[4 * n + 3] = acc[4 * n + 3] + d
                            out[n][hf] = w[2] * d + w[1] * up1 + w[0] * up2
                            nxt[n] = d
                    for n in range(2):
                        du0_ref[pl.ds(r0, rows16), side[n][0]] = jnp.concatenate(out[n], axis=0).astype(BF16)
                return (*nxt, *acc)

            init = [jnp.where(i == 0, 0.0, carry[:, cs]) for cs, _ in side] + [jnp.zeros((SUB, LANE), F32)] * 8
            st = lax.fori_loop(0, n_it, rows, tuple(init))
            for n in range(2):
                carry[:, side[n][0]] = st[n]
                for k in range(4):
                    dw_ref[k, :, side[n][0]] += st[2 + 4 * n + k]

        @pl.when(i == nt - 1)
        def _():
            for k in range(4):
                dw_ref[k, 0:1, :] = _colsum(dw_ref[k])

    rev = lambda i: nt - 1 - i
    wide = pl.BlockSpec((tm, 2 * nb), lambda jc, i: (rev(i), jc))
    return pl.pallas_call(
        body, grid=(2, nt),
        in_specs=[pl.BlockSpec((tm, nb), lambda jc, i: (rev(i), jc)), wide, wide,
                  pl.BlockSpec((FFN_W, nb), lambda jc, i: (0, jc)), pl.BlockSpec((FFN_W, nb), lambda jc, i: (0, jc + 2))],
        out_specs=[wide, pl.BlockSpec((4, SUB, 2 * nb), lambda jc, i: (0, 0, jc))],
        out_shape=[jax.ShapeDtypeStruct((T, W), BF16), jax.ShapeDtypeStruct((4, SUB, W), F32)],
        scratch_shapes=[pltpu.VMEM((SUB, 2 * nb), F32)], name=name,
        compiler_params=_cp("parallel", "arbitrary"))(dz, u0p, ab, dw_w, dw_w)


CHALO = 32
CCOL = 256


def _phase_copies(buf, shifted, tm):
    n = tm + CHALO - SUB
    for p in range(1, SUB):
        shifted[p - 1, 0:n, :] = buf[p:p + n, :]


def _shifted(buf, shifted, r, tm, c0):
    m, p = divmod(r, SUB)
    src = buf if p == 0 else shifted.at[p - 1]
    return src[m * SUB:m * SUB + tm, c0:c0 + CCOL]


def conv_act(u, dw_w, dw_b, ln_g, ln_b, *, name, tm=128):
    T, D2 = u.shape
    D = D2 // 2
    tm = min(tm, T)
    hb = tm // CHALO

    def body(u_ref, halo_ref, w_ref, b_ref, g_ref, be_ref, s_ref, cv_ref, gbuf, gsh):
        i = pl.program_id(0)
        hv = halo_ref[...]
        gbuf[0:CHALO, :] = jnp.where(i == 0, 0.0, hv[:, 0:D] * _sig(hv[:, D:D2]))
        uv = u_ref[...]
        gbuf[CHALO:CHALO + tm, :] = uv[:, 0:D] * _sig(uv[:, D:D2])
        _phase_copies(gbuf, gsh, tm)
        for c0 in range(0, D, CCOL):
            acc = jnp.zeros((tm, CCOL), F32) + b_ref[:, c0:c0 + CCOL]
            for k in range(CONV_W):
                acc = acc + w_ref[k:k + 1, c0:c0 + CCOL] * _shifted(gbuf, gsh, CHALO - (CONV_W - 1) + k, tm, c0)
            cv_ref[:, c0:c0 + CCOL] = acc
        cv = cv_ref[...]
        mu = _rowmean(cv)
        xc = cv - mu
        nh = xc * lax.rsqrt(_rowmean(xc * xc) + EPS)
        s_ref[...] = _silu(nh * g_ref[...] + be_ref[...]).astype(BF16)

    return pl.pallas_call(
        body, grid=(T // tm,),
        in_specs=[_row(tm, D2), pl.BlockSpec((CHALO, D2), lambda i: (jnp.maximum(i * hb - 1, 0), 0)),
                  _full(CONV_W, D), _full(1, D), _full(1, D), _full(1, D)],
        out_specs=[_row(tm, D), _row(tm, D)],
        out_shape=[jax.ShapeDtypeStruct((T, D), BF16), jax.ShapeDtypeStruct((T, D), F32)],
        scratch_shapes=[pltpu.VMEM((tm + CHALO, D), F32), pltpu.VMEM((SUB - 1, tm + CHALO, D), F32)], name=name,
        compiler_params=_cp("arbitrary"))(u, u, dw_w, dw_b, ln_g, ln_b)


def conv_norm_bwd(ds, cv, ln_g, ln_b, *, name, tm=512):
    T, D = cv.shape
    tm = min(tm, T)

    def body(ds_ref, cv_ref, g_ref, be_ref, dcv_ref, dg_ref, dbe_ref, dcb_ref):
        _acc_init(pl.program_id(0), dg_ref, dbe_ref, dcb_ref)
        cv_ = cv_ref[...]
        mu = _rowmean(cv_)
        xc = cv_ - mu
        rstd = lax.rsqrt(_rowmean(xc * xc) + EPS)
        nh = xc * rstd
        gv = g_ref[...]
        dln = ds_ref[...] * _dsilu(nh * gv + be_ref[...])
        dg_ref[...] += _colsum(dln * nh)
        dbe_ref[...] += _colsum(dln)
        dnh = dln * gv
        dcv = rstd * (dnh - _rowmean(dnh) - nh * _rowmean(dnh * nh))
        dcb_ref[...] += _colsum(dcv)
        dcv_ref[...] = dcv

    return pl.pallas_call(
        body, grid=(T // tm,), in_specs=[_row(tm, D), _row(tm, D), _full(1, D), _full(1, D)],
        out_specs=[_row(tm, D), _full(1, D), _full(1, D), _full(1, D)],
        out_shape=[jax.ShapeDtypeStruct((T, D), F32)] + [jax.ShapeDtypeStruct((1, D), F32)] * 3, name=name,
        compiler_params=_cp("arbitrary"))(ds, cv, ln_g, ln_b)


def conv_glu_bwd(dcv, u, dw_w, *, name, tm=128):
    T, D2 = u.shape
    D = D2 // 2
    tm = min(tm, T)
    nt = T // tm
    hb = tm // CHALO

    def body(dcv_ref, dnext_ref, u_ref, w_ref, du_ref, dw_ref, dbin_ref, dbuf, dsh):
        i = pl.program_id(0)
        _acc_init(i, dw_ref, dbin_ref)
        uv = u_ref[...]
        av = uv[:, 0:D]
        sg = _sig(uv[:, D:D2])
        glu = av * sg
        dbuf[0:tm, :] = dcv_ref[...]
        dbuf[tm:tm + CHALO, :] = jnp.where(i == nt - 1, 0.0, dnext_ref[...])
        _phase_copies(dbuf, dsh, tm)
        for c0 in range(0, D, CCOL):
            glu_c = glu[:, c0:c0 + CCOL]
            acc = jnp.zeros((tm, CCOL), F32)
            for k in range(CONV_W):
                moved = _shifted(dbuf, dsh, CONV_W - 1 - k, tm, c0)
                dw_ref[k:k + 1, c0:c0 + CCOL] += _colsum(moved * glu_c)
                acc = acc + w_ref[k:k + 1, c0:c0 + CCOL] * moved
            a_c = av[:, c0:c0 + CCOL]
            s_c = sg[:, c0:c0 + CCOL]
            da = acc * s_c
            dgt = acc * a_c * s_c * (1.0 - s_c)
            dbin_ref[:, c0:c0 + CCOL] += _colsum(da)
            dbin_ref[:, D + c0:D + c0 + CCOL] += _colsum(dgt)
            du_ref[:, c0:c0 + CCOL] = da.astype(BF16)
            du_ref[:, D + c0:D + c0 + CCOL] = dgt.astype(BF16)

    return pl.pallas_call(
        body, grid=(nt,),
        in_specs=[_row(tm, D), pl.BlockSpec((CHALO, D), lambda i: (jnp.minimum((i + 1) * hb, T // CHALO - 1), 0)),
                  _row(tm, D2), _full(CONV_W, D)],
        out_specs=[_row(tm, D2), _full(CHALO, D), _full(1, D2)],
        out_shape=[jax.ShapeDtypeStruct((T, D2), BF16), jax.ShapeDtypeStruct((CHALO, D), F32),
                   jax.ShapeDtypeStruct((1, D2), F32)],
        scratch_shapes=[pltpu.VMEM((tm + CHALO, D), F32), pltpu.VMEM((SUB - 1, tm + CHALO, D), F32)],
        name=name, compiler_params=_cp("arbitrary"))(dcv, dcv, u, dw_w)


HB = 8


def _lb0(lg_ref):
    l0, l1, l2 = lg_ref[0:1, :], lg_ref[1:2, :], lg_ref[2:3, :]
    m = jnp.maximum(jnp.maximum(l0, l1), l2)
    e0 = jnp.exp(l0 - m)
    return e0 / (e0 + jnp.exp(l1 - m) + jnp.exp(l2 - m))


def _mm_exact(m01, x):
    hi = x.astype(BF16)
    r1 = x - hi.astype(F32)
    mid = r1.astype(BF16)
    lo = (r1 - mid.astype(F32)).astype(BF16)
    return _dot(m01, hi) + _dot(m01, mid) + _dot(m01, lo)


def _block_tri(tm):
    r = jnp.arange(tm)[:, None]
    c = jnp.arange(tm)[None, :]
    same = (r // BLK) == (c // BLK)
    return (same & (c <= r)).astype(BF16), (same & (c >= r)).astype(BF16)


def _halves(x):
    return [x[0:SUB, :], x[SUB:BLK, :]]


def _live_halves(s):
    return ([(0, s)] if s < SUB else []) + [(1, max(s - SUB, 0))]


def _const_spec(shape):
    return pl.BlockSpec(shape, lambda h, i: (0, 0))


def _hgrn_specs(H, hb, tm, idx):
    g = H // hb
    return [pl.BlockSpec((tm, hb * HEAD), lambda h, i: (idx(i), h)),
            pl.BlockSpec((tm, hb * HEAD), lambda h, i: (idx(i), g + h)),
            pl.BlockSpec((tm, hb * HEAD), lambda h, i: (idx(i), 2 * g + h)),
            pl.BlockSpec((3, hb * HEAD), lambda h, i: (0, h))]


def hgrn_scan(proj, lb_logits, *, name, tm=128):
    T = proj.shape[0]
    H = proj.shape[1] // (4 * HEAD)
    hb = min(HB, H)
    tm = min(tm, T)
    nt = T // tm
    nblk = tm // BLK
    tril, _ = _block_tri(tm)
    heads = [slice(hh * HEAD, (hh + 1) * HEAD) for hh in range(hb)]

    def body(qp_ref, fz_ref, v_ref, lg_ref, tril_ref, o_ref, st_ref, S_ref, q_s, k_s, b_s):
        @pl.when(pl.program_id(1) == 0)
        def _():
            S_ref[...] = jnp.zeros_like(S_ref)

        st_ref[...] = S_ref[...]
        lb = _lb0(lg_ref)
        f = lb + (1.0 - lb) * _sig(fz_ref[...])
        q_s[...] = _silu(qp_ref[...])
        k_s[...] = 1.0 - f
        b_s[...] = _mm_exact(tril_ref[...], jnp.log(f))
        rows = lax.broadcasted_iota(jnp.int32, (BLK, HEAD), 0)
        S = [S_ref[hh] for hh in range(hb)]
        for nb in range(nblk):
            blk = slice(nb * BLK, (nb + 1) * BLK)
            last = slice(nb * BLK + BLK - 1, nb * BLK + BLK)
            qb = [q_s[blk, c] for c in heads]
            bb = [b_s[blk, c] for c in heads]
            o = [_dot_nt((qb[hh] * jnp.exp(bb[hh])).astype(BF16), S[hh].astype(BF16)) for hh in range(hb)]
            for hh, c in enumerate(heads):
                bc = b_s[last, c]
                kd = k_s[blk, c] * jnp.exp(bc - bb[hh])
                S[hh] = S[hh] * jnp.exp(bc) + _dot_tn(v_ref[blk, c].astype(BF16), kd.astype(BF16))
            for s in range(BLK):
                r = slice(nb * BLK + s, nb * BLK + s + 1)
                for hh, c in enumerate(heads):
                    dec = jnp.exp(jnp.where(rows >= s, bb[hh] - b_s[r, c], NEG))
                    a = jnp.sum(qb[hh] * k_s[r, c] * dec, axis=-1, keepdims=True)
                    o[hh] = o[hh] + a * v_ref[r, c]
            for hh, c in enumerate(heads):
                o_ref[blk, c] = o[hh]
        for hh in range(hb):
            S_ref[hh] = S[hh]

    return pl.pallas_call(
        body, grid=(H // hb, nt),
        in_specs=_hgrn_specs(H, hb, tm, lambda i: i) + [_const_spec((tm, tm))],
        out_specs=[pl.BlockSpec((tm, hb * HEAD), lambda h, i: (i, h)),
                   pl.BlockSpec((None, hb, HEAD, HEAD), lambda h, i: (i, h, 0, 0))],
        out_shape=[jax.ShapeDtypeStruct((T, H * HEAD), F32), jax.ShapeDtypeStruct((nt, H, HEAD, HEAD), F32)],
        scratch_shapes=[pltpu.VMEM((hb, HEAD, HEAD), F32)] + [pltpu.VMEM((tm, hb * HEAD), F32)] * 3, name=name,
        compiler_params=_cp("parallel", "arbitrary"))(proj, proj, proj, lb_logits, tril)


def hgrn_scan_bwd(proj, lb_logits, states, do, *, name, tm=128):
    T = proj.shape[0]
    H = proj.shape[1] // (4 * HEAD)
    hb = min(HB, H)
    tm = min(tm, T)
    nt = T // tm
    nblk = tm // BLK
    tril, triu = _block_tri(tm)
    sel = (jnp.arange(BLK * SUB)[None, :] // SUB == jnp.arange(BLK)[:, None]).astype(BF16)
    heads = [slice(hh * HEAD, (hh + 1) * HEAD) for hh in range(hb)]

    def body(qp_ref, fz_ref, v_ref, lg_ref, st_ref, do_ref, tril_ref, triu_ref, sel_ref, d3_ref, dlb_ref,
             dS_ref, Sb_ref, q_s, k_s, b_s, dq_s, dk_s, dv_s, db_s, pk_s, pv_s):
        i = pl.program_id(1)

        @pl.when(i == 0)
        def _():
            dS_ref[...] = jnp.zeros_like(dS_ref)
            dlb_ref[...] = jnp.zeros_like(dlb_ref)

        lb = _lb0(lg_ref)
        qp = qp_ref[...]
        sg = _sig(fz_ref[...])
        f = lb + (1.0 - lb) * sg
        q_s[...] = _silu(qp)
        k_s[...] = 1.0 - f
        b_s[...] = _mm_exact(tril_ref[...], jnp.log(f))
        rows = lax.broadcasted_iota(jnp.int32, (SUB, HEAD), 0)
        rows1 = lax.broadcasted_iota(jnp.int32, (SUB, 1), 0)

        S = [st_ref[hh] for hh in range(hb)]
        for nb in range(nblk):
            blk = slice(nb * BLK, (nb + 1) * BLK)
            last = slice(nb * BLK + BLK - 1, nb * BLK + BLK)
            for hh, c in enumerate(heads):
                Sb_ref[nb * hb + hh] = S[hh]
                if nb < nblk - 1:
                    bc = b_s[last, c]
                    kd = k_s[blk, c] * jnp.exp(bc - b_s[blk, c])
                    S[hh] = S[hh] * jnp.exp(bc) + _dot_tn(v_ref[blk, c].astype(BF16), kd.astype(BF16))

        dS = [dS_ref[hh] for hh in range(hb)]
        for nb in reversed(range(nblk)):
            blk = slice(nb * BLK, (nb + 1) * BLK)
            last = slice(nb * BLK + BLK - 1, nb * BLK + BLK)
            qb, kb, bb, dob, dq, dbc, ebc = [], [], [], [], [], [], []
            for hh, c in enumerate(heads):
                S0 = Sb_ref[nb * hb + hh]
                qb.append(q_s[blk, c])
                kb.append(k_s[blk, c])
                bb.append(b_s[blk, c])
                dob.append(do_ref[blk, c])
                bc = b_s[last, c]
                eb = jnp.exp(bb[hh])
                ekd = jnp.exp(bc - bb[hh])
                ebc.append(jnp.exp(bc))
                dS16 = dS[hh].astype(BF16)
                dob16 = dob[hh].astype(BF16)
                dq.append(_dot(dob16, S0.astype(BF16)) * eb)
                dki = _dot(v_ref[blk, c].astype(BF16), dS16) * ekd
                dk_s[blk, c] = dki
                dv_s[blk, c] = _dot_nt((kb[hh] * ekd).astype(BF16), dS16)
                dbc.append(_colsum(dS[hh] * S0) * ebc[hh] + _colsum(kb[hh] * dki))
                dS[hh] = dS[hh] * ebc[hh] + _dot_tn(dob16, (qb[hh] * eb).astype(BF16))
            qh, bh, doh, dqh = [[_halves(t[hh]) for hh in range(hb)] for t in (qb, bb, dob, dq)]
            for s in range(BLK):
                r = slice(nb * BLK + s, nb * BLK + s + 1)
                for hh, c in enumerate(heads):
                    ks = k_s[r, c]
                    pk, pv = None, None
                    for hf, lo in _live_halves(s):
                        diff = bh[hh][hf] - b_s[r, c]
                        dec = jnp.exp(diff if lo == 0 else jnp.where(rows >= lo, diff, NEG))
                        w = qh[hh][hf] * dec
                        a = jnp.sum(w * ks, axis=-1, keepdims=True)
                        da = jnp.sum(doh[hh][hf] * v_ref[r, c], axis=-1, keepdims=True)
                        if lo:
                            da = jnp.where(rows1 >= lo, da, 0.0)
                        dqh[hh][hf] = dqh[hh][hf] + (da * ks) * dec
                        pk = da * w if pk is None else pk + da * w
                        pv = a * doh[hh][hf] if pv is None else pv + a * doh[hh][hf]
                    pk_s[hh, s * SUB:(s + 1) * SUB, :] = pk
                    pv_s[hh, s * SUB:(s + 1) * SUB, :] = pv
            for hh, c in enumerate(heads):
                khi, klo = _split2(pk_s[hh])
                dk_s[blk, c] += _dot(sel_ref[...], khi) + _dot(sel_ref[...], klo)
                dv_s[blk, c] += _dot(sel_ref[...], pv_s[hh].astype(BF16))
                dq[hh] = jnp.concatenate(dqh[hh], axis=0)
                dq_s[blk, c] = dq[hh]
                db_s[blk, c] = qb[hh] * dq[hh] - kb[hh] * dk_s[blk, c]
                db_s[last, c] += dbc[hh]
        for hh in range(hb):
            dS_ref[hh] = dS[hh]

        dlf = _mm_exact(triu_ref[...], db_s[...])
        df = dlf / f - dk_s[...]
        d3_ref[0] = (dq_s[...] * _dsilu(qp)).astype(BF16)
        d3_ref[1] = (df * (1.0 - lb) * sg * (1.0 - sg)).astype(BF16)
        d3_ref[2] = dv_s[...].astype(BF16)
        dlb_ref[...] += _colsum(df * (1.0 - sg))

    rev = lambda i: nt - 1 - i
    out_blk = pl.BlockSpec((tm, hb * HEAD), lambda h, i: (rev(i), h))
    return pl.pallas_call(
        body, grid=(H // hb, nt),
        in_specs=_hgrn_specs(H, hb, tm, rev) + [pl.BlockSpec((None, hb, HEAD, HEAD), lambda h, i: (rev(i), h, 0, 0)),
                                                out_blk, _const_spec((tm, tm)), _const_spec((tm, tm)),
                                                _const_spec((BLK, BLK * SUB))],
        out_specs=[pl.BlockSpec((3, tm, hb * HEAD), lambda h, i: (0, rev(i), h)),
                   pl.BlockSpec((1, hb * HEAD), lambda h, i: (0, h))],
        out_shape=[jax.ShapeDtypeStruct((3, T, H * HEAD), BF16), jax.ShapeDtypeStruct((1, H * HEAD), F32)],
        scratch_shapes=[pltpu.VMEM((hb, HEAD, HEAD), F32), pltpu.VMEM((nblk * hb, HEAD, HEAD), F32)]
        + [pltpu.VMEM((tm, hb * HEAD), F32)] * 7 + [pltpu.VMEM((hb, BLK * SUB, HEAD), F32)] * 2, name=name,
        compiler_params=_cp("parallel", "arbitrary"))(proj, proj, proj, lb_logits, states, do, tril, triu, sel)


def hgrn_gate(o, proj, gn, *, name, tm=512):
    T, D = o.shape
    H = D // HEAD
    tm = min(tm, T)

    def body(o_ref, gp_ref, gn_ref, og_ref):
        gn_ = gn_ref[...]
        for h in range(H):
            c = slice(h * HEAD, (h + 1) * HEAD)
            oh = o_ref[:, c]
            r = lax.rsqrt(_rowmean(oh * oh) + EPS)
            og_ref[:, c] = ((oh * r) * gn_ * _silu(gp_ref[:, c])).astype(BF16)

    return pl.pallas_call(
        body, grid=(T // tm,),
        in_specs=[_row(tm, D), pl.BlockSpec((tm, D), lambda i: (i, 3)), _full(1, HEAD)],
        out_specs=_row(tm, D), out_shape=jax.ShapeDtypeStruct((T, D), BF16), name=name,
        compiler_params=_cp("parallel"))(o, proj, gn)


def hgrn_gate_bwd(dog, o, proj, gn, *, name, tm=512):
    T, D = o.shape
    H = D // HEAD
    tm = min(tm, T)

    def body(dog_ref, o_ref, gp_ref, gn_ref, do_ref, dgp_ref, dgn_ref):
        _acc_init(pl.program_id(0), dgn_ref)
        gn_ = gn_ref[...]
        for h in range(H):
            c = slice(h * HEAD, (h + 1) * HEAD)
            oh = o_ref[:, c]
            gp = gp_ref[:, c]
            dg = dog_ref[:, c]
            r = lax.rsqrt(_rowmean(oh * oh) + EPS)
            on = oh * r
            dgp_ref[:, c] = (dg * (on * gn_) * _dsilu(gp)).astype(BF16)
            don = dg * _silu(gp)
            dgn_ref[...] += _colsum(don * on)
            dn = don * gn_
            do_ref[:, c] = r * (dn - on * _rowmean(dn * on))

    return pl.pallas_call(
        body, grid=(T // tm,),
        in_specs=[_row(tm, D), _row(tm, D), pl.BlockSpec((tm, D), lambda i: (i, 3)), _full(1, HEAD)],
        out_specs=[_row(tm, D), _row(tm, D), _full(1, HEAD)],
        out_shape=[jax.ShapeDtypeStruct((T, D), F32), jax.ShapeDtypeStruct((T, D), BF16),
                   jax.ShapeDtypeStruct((1, HEAD), F32)], name=name,
        compiler_params=_cp("arbitrary"))(dog, o, proj, gn)


def _split2(x):
    hi = x.astype(BF16)
    return hi, (x - hi.astype(F32)).astype(BF16)


def ada_mod(c_all, ada_w, *, name):
    L, D, N = ada_w.shape
    B = c_all.shape[0]

    def body(c_ref, w_ref, o_ref):
        chi, clo = _split2(_silu(c_ref[...]))
        whi, wlo = _split2(w_ref[...])
        o_ref[...] = _dot(chi, whi) + _dot(chi, wlo) + _dot(clo, whi)

    return pl.pallas_call(
        body, grid=(L,), in_specs=[_full(B, D), pl.BlockSpec((None, D, N), lambda l: (l, 0, 0))],
        out_specs=pl.BlockSpec((None, B, N), lambda l: (l, 0, 0)),
        out_shape=jax.ShapeDtypeStruct((L, B, N), F32), name=name, compiler_params=_cp("parallel"))(c_all, ada_w)


def ada_wgrad(c_all_t, dmod, *, name, tr=256):
    D, B = c_all_t.shape
    L, _, N = dmod.shape
    tr = min(tr, D)

    def body(c_ref, d_ref, o_ref):
        cond = _silu(c_ref[...])
        acc = cond[:, 0:1] * d_ref[0:1, :]
        for b in range(1, B):
            acc = acc + cond[:, b:b + 1] * d_ref[b:b + 1, :]
        o_ref[...] = acc

    return pl.pallas_call(
        body, grid=(L, D // tr),
        in_specs=[pl.BlockSpec((tr, B), lambda l, r: (r, 0)), pl.BlockSpec((None, B, N), lambda l, r: (l, 0, 0))],
        out_specs=pl.BlockSpec((None, tr, N), lambda l, r: (l, r, 0)),
        out_shape=jax.ShapeDtypeStruct((L, D, N), F32), name=name,
        compiler_params=_cp("parallel", "parallel"))(c_all_t, dmod)


def sum_devices(parts, *, name):
    n, R, C = parts.shape

    def body(p_ref, o_ref):
        acc = p_ref[0]
        for d in range(1, n):
            acc = acc + p_ref[d]
        o_ref[...] = acc

    return pl.pallas_call(body, in_specs=[VMEM_SPEC], out_specs=VMEM_SPEC,
                          out_shape=jax.ShapeDtypeStruct((R, C), F32), name=name)(parts)


def lb_logits_grad(lb_logits, dlb, *, name):
    def body(lg_ref, d_ref, o_ref):
        l0, l1, l2 = lg_ref[0:1, :], lg_ref[1:2, :], lg_ref[2:3, :]
        m = jnp.maximum(jnp.maximum(l0, l1), l2)
        e0, e1, e2 = jnp.exp(l0 - m), jnp.exp(l1 - m), jnp.exp(l2 - m)
        z = e0 + e1 + e2
        p0, p1, p2 = e0 / z, e1 / z, e2 / z
        g = d_ref[...] * p0
        o_ref[0:1, :] = g * (1.0 - p0)
        o_ref[1:2, :] = -g * p1
        o_ref[2:3, :] = -g * p2

    return pl.pallas_call(body, in_specs=[VMEM_SPEC, VMEM_SPEC], out_specs=VMEM_SPEC,
                          out_shape=jax.ShapeDtypeStruct(lb_logits.shape, F32), name=name)(lb_logits, dlb)


def adamw(w, g, m, v, *, name, tr=256, after=None):
    R, C = w.shape
    tr = _tile(R, tr)
    deps = [] if after is None else [after]

    def body(w_ref, g_ref, m_ref, v_ref, *rest):
        d_ref, nm_ref, nv_ref = rest[len(deps):]
        gv = g_ref[...]
        nm = ADAM_B1 * m_ref[...] + (1.0 - ADAM_B1) * gv
        nv = ADAM_B2 * v_ref[...] + (1.0 - ADAM_B2) * (gv * gv)
        m_hat = nm / (1.0 - ADAM_B1 ** ADAM_STEP)
        v_hat = nv / (1.0 - ADAM_B2 ** ADAM_STEP)
        d_ref[...] = -ADAM_LR * (m_hat / (jnp.sqrt(v_hat) + ADAM_EPS) + ADAM_WD * w_ref[...])
        nm_ref[...] = nm
        nv_ref[...] = nv

    spec = pl.BlockSpec((tr, C), lambda i: (i, 0))
    return pl.pallas_call(
        body, grid=(R // tr,), in_specs=[spec] * 4 + [pl.BlockSpec(memory_space=pl.ANY)] * len(deps), out_specs=[spec] * 3,
        out_shape=[jax.ShapeDtypeStruct((R, C), F32)] * 3, name=name,
        compiler_params=_cp("parallel"))(w, g, m, v, *deps)


def _place():
    return lax.axis_index("x"), lax.axis_index("y"), lax.axis_index("c")


def _flip(v, bit):
    return 1 - v if bit else v


def allgather_devices(v, *, name):
    R, C = v.shape

    def body(v_ref, out_ref, send_sems, recv_sems, local_sem):
        x, y, c = _place()
        me = 4 * x + 2 * y + c
        mine = pltpu.make_async_copy(v_ref, out_ref.at[me], local_sem)
        mine.start()
        sends = []
        for k in range(1, N_DEV):
            peer = (_flip(x, k & 4), _flip(y, k & 2), _flip(c, k & 1))
            cp = pltpu.make_async_remote_copy(src_ref=v_ref, dst_ref=out_ref.at[me], send_sem=send_sems.at[k - 1],
                                              recv_sem=recv_sems.at[k - 1], device_id=peer, device_id_type=MESH)
            cp.start()
            sends.append(cp)
        for k in range(1, N_DEV):
            px, py, pc = _flip(x, k & 4), _flip(y, k & 2), _flip(c, k & 1)
            pltpu.make_async_remote_copy(src_ref=v_ref, dst_ref=out_ref.at[4 * px + 2 * py + pc],
                                         send_sem=send_sems.at[k - 1], recv_sem=recv_sems.at[k - 1],
                                         device_id=(px, py, pc), device_id_type=MESH).wait_recv()
        for cp in sends:
            cp.wait_send()
        mine.wait()

    return pl.pallas_call(
        body, in_specs=[VMEM_SPEC], out_specs=VMEM_SPEC, out_shape=jax.ShapeDtypeStruct((N_DEV, R, C), v.dtype),
        scratch_shapes=[pltpu.SemaphoreType.DMA((N_DEV - 1,)), pltpu.SemaphoreType.DMA((N_DEV - 1,)),
                        pltpu.SemaphoreType.DMA], name=name)(v)


def _other_chips(x, y):
    return [(1 - x, y), (x, 1 - y), (1 - x, 1 - y)]


SEM = pl.BlockSpec(memory_space=pltpu.SEMAPHORE)
DATAFLOW = pltpu.SideEffectType.DATAFLOW_SIDE_EFFECTING


def _chip_copy(buf, a, j, q, c, chips, send_sems, recv_sems):
    px, py = chips[j]
    return pltpu.make_async_remote_copy(src_ref=buf.at[q, c], dst_ref=buf.at[q, c], send_sem=send_sems.at[3 * a + j],
                                        recv_sem=recv_sems.at[3 * a + j], device_id=(px, py, c), device_id_type=MESH)


def allgather_chips_start(bufs, *, name):
    n = len(bufs)

    def body(*refs):
        send_sems, recv_sems = refs[n], refs[n + 1]
        outs = refs[n + 2:2 * n + 2]
        token = refs[2 * n + 2]
        x, y, c = _place()
        chips = _other_chips(x, y)
        for a in range(n):
            for j in range(3):
                _chip_copy(outs[a], a, j, 2 * x + y, c, chips, send_sems, recv_sems).start()
        token[...] = jnp.zeros_like(token)

    res = pl.pallas_call(
        body, name=name, in_specs=[HBM] * n,
        out_specs=(SEM, SEM, *([HBM] * n), VMEM_SPEC),
        out_shape=(pltpu.SemaphoreType.DMA((3 * n,)), pltpu.SemaphoreType.DMA((3 * n,)),
                   *[pltpu.HBM(b.shape, b.dtype) for b in bufs], jax.ShapeDtypeStruct((SUB, LANE), F32)),
        input_output_aliases={a: a + 2 for a in range(n)},
        compiler_params=pltpu.CompilerParams(has_side_effects=DATAFLOW),
    )(*[pltpu.with_memory_space_constraint(b, pltpu.HBM) for b in bufs])
    return res[0], res[1], list(res[2:2 + n]), res[2 + n]


def allgather_chips_wait(send_sems, recv_sems, bufs, after, *, name):
    n = len(bufs)

    def body(*refs):
        ins = refs[:n]
        send_sems, recv_sems = refs[n], refs[n + 1]
        x, y, c = _place()
        chips = _other_chips(x, y)
        for a in range(n):
            for j, (px, py) in enumerate(chips):
                _chip_copy(ins[a], a, j, 2 * x + y, c, chips, send_sems, recv_sems).wait_send()
                _chip_copy(ins[a], a, j, 2 * px + py, c, chips, send_sems, recv_sems).wait_recv()

    return list(pl.pallas_call(
        body, name=name, in_specs=[HBM] * n + [SEM, SEM, pl.BlockSpec(memory_space=pl.ANY)],
        out_specs=[HBM] * n, out_shape=[pltpu.HBM(b.shape, b.dtype) for b in bufs],
        input_output_aliases={a: a for a in range(n)},
        compiler_params=pltpu.CompilerParams(has_side_effects=DATAFLOW),
    )(*bufs, send_sems, recv_sems, after))


def forward_to_sibling(bufs, *, name):
    n = len(bufs)

    def body(*refs):
        outs = refs[n:2 * n]
        send_sems, recv_sems = refs[2 * n:]
        x, y, c = _place()
        chips = _other_chips(x, y)

        def copy(a, j, half, to):
            px, py = chips[j]
            slab = outs[a].at[2 * px + py, half]
            return pltpu.make_async_remote_copy(src_ref=slab, dst_ref=slab, send_sem=send_sems.at[a, j],
                                                recv_sem=recv_sems.at[a, j], device_id=to, device_id_type=MESH)

        sends = [copy(a, j, c, (x, y, 1 - c)) for a in range(n) for j in range(3)]
        for cp in sends:
            cp.start()
        for a in range(n):
            for j in range(3):
                copy(a, j, 1 - c, (x, y, c)).wait_recv()
        for cp in sends:
            cp.wait_send()

    return pl.pallas_call(
        body, in_specs=[HBM] * n, out_specs=[HBM] * n,
        out_shape=[jax.ShapeDtypeStruct(b.shape, b.dtype) for b in bufs],
        input_output_aliases={a: a for a in range(n)},
        scratch_shapes=[pltpu.SemaphoreType.DMA((n, 3)), pltpu.SemaphoreType.DMA((n, 3))], name=name)(*bufs)


def pair_add(g, other, c_idx, *, name, tr=256):
    _, Q, R, C = g.shape
    tr = _tile(R, tr)

    def body(c_ref, g_ref, o_ref, out_ref):
        out_ref[...] = (g_ref[...] + o_ref[...]).astype(BF16)

    return pl.pallas_call(
        body,
        grid_spec=pltpu.PrefetchScalarGridSpec(
            num_scalar_prefetch=1, grid=(Q, R // tr),
            in_specs=[pl.BlockSpec((None, None, tr, C), lambda q, r, c_ref: (c_ref[0], q, r, 0)),
                      pl.BlockSpec((None, tr, C), lambda q, r, c_ref: (q, r, 0))],
            out_specs=pl.BlockSpec((None, tr, C), lambda q, r, c_ref: (q, r, 0))),
        out_shape=jax.ShapeDtypeStruct((Q, R, C), BF16), name=name,
        compiler_params=_cp("parallel", "parallel"))(c_idx, g, other)


def chip_sum(sums, landed, qc_idx, *, name, tr=256):
    _, R, C = sums.shape
    tr = _tile(R, tr)

    def body(qc_ref, own_ref, l_ref, o_ref):
        acc = own_ref[...].astype(F32)
        for k in range(3):
            acc = acc + l_ref[k].astype(F32)
        o_ref[...] = acc

    return pl.pallas_call(
        body,
        grid_spec=pltpu.PrefetchScalarGridSpec(
            num_scalar_prefetch=1, grid=(R // tr,),
            in_specs=[pl.BlockSpec((None, tr, C), lambda r, qc: (qc[0], r, 0)),
                      pl.BlockSpec((3, tr, C), lambda r, qc: (0, r, 0))],
            out_specs=pl.BlockSpec((None, tr, C), lambda r, qc: (qc[1], r, 0))),
        out_shape=jax.ShapeDtypeStruct((2, R, C), F32), name=name,
        compiler_params=_cp("parallel"))(qc_idx, sums, landed)


def half_swap(bufs, *, name):
    n = len(bufs)

    def body(*refs):
        outs = refs[n:2 * n]
        send_sems, recv_sems = refs[2 * n:]
        x, y, c = _place()
        cps = [pltpu.make_async_remote_copy(src_ref=outs[a].at[c], dst_ref=outs[a].at[c], send_sem=send_sems.at[a],
                                            recv_sem=recv_sems.at[a], device_id=(x, y, 1 - c), device_id_type=MESH)
               for a in range(n)]
        for cp in cps:
            cp.start()
        for a in range(n):
            pltpu.make_async_remote_copy(src_ref=outs[a].at[c], dst_ref=outs[a].at[1 - c], send_sem=send_sems.at[a],
                                         recv_sem=recv_sems.at[a], device_id=(x, y, 1 - c),
                                         device_id_type=MESH).wait_recv()
        for cp in cps:
            cp.wait_send()

    return pl.pallas_call(
        body, in_specs=[HBM] * n, out_specs=[HBM] * n,
        out_shape=[jax.ShapeDtypeStruct(b.shape, b.dtype) for b in bufs],
        input_output_aliases={a: a for a in range(n)},
        scratch_shapes=[pltpu.SemaphoreType.DMA((n,)), pltpu.SemaphoreType.DMA((n,))], name=name)(*bufs)


def _chip_copies(src, dst, send_sems, recv_sems):
    x, y, c = _place()
    return [pltpu.make_async_remote_copy(src_ref=src[a].at[2 * px + py], dst_ref=dst[a].at[j],
                                         send_sem=send_sems.at[3 * a + j], recv_sem=recv_sems.at[3 * a + j],
                                         device_id=(px, py, c), device_id_type=MESH)
            for a in range(len(src)) for j, (px, py) in enumerate(_other_chips(x, y))]


def _pair_copies(src, dst, send_sems, recv_sems):
    x, y, c = _place()
    return [pltpu.make_async_remote_copy(src_ref=src[a].at[1 - c], dst_ref=dst[a], send_sem=send_sems.at[a],
                                         recv_sem=recv_sems.at[a], device_id=(x, y, 1 - c), device_id_type=MESH)
            for a in range(len(src))]


def _device_copies(src, dst, send_sems, recv_sems):
    x, y, c = _place()
    mine = src[0].at[4 * x + 2 * y + c]
    return [pltpu.make_async_remote_copy(src_ref=mine, dst_ref=mine, send_sem=send_sems.at[k - 1],
                                         recv_sem=recv_sems.at[k - 1],
                                         device_id=(_flip(x, k & 4), _flip(y, k & 2), _flip(c, k & 1)), device_id_type=MESH)
            for k in range(1, N_DEV)]


def exchange_start(src, landing, copies, n_sems, *, name):
    n, m = len(src), len(src) + len(landing)

    def body(*refs):
        send_sems, recv_sems = refs[m], refs[m + 1]
        for cp in copies(refs[m + 2:m + 2 + n], refs[m + 2 + n:2 * m + 2], send_sems, recv_sems):
            cp.start()
        token = refs[2 * m + 2]
        token[...] = jnp.zeros_like(token)

    res = pl.pallas_call(
        body, name=name, in_specs=[HBM] * m,
        out_specs=(SEM, SEM, *([HBM] * m), VMEM_SPEC),
        out_shape=(pltpu.SemaphoreType.DMA((n_sems,)), pltpu.SemaphoreType.DMA((n_sems,)),
                   *[pltpu.HBM(b.shape, b.dtype) for b in src + landing], jax.ShapeDtypeStruct((SUB, LANE), F32)),
        input_output_aliases={a: a + 2 for a in range(m)},
        compiler_params=pltpu.CompilerParams(has_side_effects=DATAFLOW),
    )(*[pltpu.with_memory_space_constraint(b, pltpu.HBM) for b in src + landing])
    return res[0], res[1], list(res[2:2 + n]), list(res[2 + n:2 + m]), res[2 + m]


def exchange_wait(send_sems, recv_sems, src, landed, copies, after, *, name):
    n, m = len(src), len(src) + len(landed)

    def body(*refs):
        for cp in copies(refs[:n], refs[n:m], refs[m], refs[m + 1]):
            cp.wait_send()
            cp.wait_recv()

    res = pl.pallas_call(
        body, name=name, in_specs=[HBM] * m + [SEM, SEM, pl.BlockSpec(memory_space=pl.ANY)],
        out_specs=[HBM] * m, out_shape=[pltpu.HBM(b.shape, b.dtype) for b in src + landed],
        input_output_aliases={a: a for a in range(m)},
        compiler_params=pltpu.CompilerParams(has_side_effects=DATAFLOW),
    )(*src, *landed, send_sems, recv_sems, after)
    return list(res[:n]), list(res[n:])


def finish_reduce(sums, landed, q, c, tag):
    qc_idx = jnp.stack([q, c]).astype(jnp.int32)
    return [chip_sum(s, l, qc_idx, name=f"grad_chip_sum_{tag}{a}") for a, (s, l) in enumerate(zip(sums, landed))]


def _ffn_forward(x, h, mod, post_g, w_up, w_down, dw_w, dw_b, tag, next_norm=None, tgt=None):
    _, _, gate = mod
    u0 = mm_nn(h, w_up, name=f"{tag}_up", out_dtype=BF16, perm=_ffn_perm)
    z, ab = ffn_act(u0, dw_w, dw_b, name=f"{tag}_act")
    y = mm_nn(z, w_down, name=f"{tag}_down")
    if tgt is None:
        out = post_residual_prenorm(x, y, post_g, gate, *next_norm, name=f"{tag}_post")
    else:
        out = post_residual_loss(x, y, post_g, gate, tgt, name=f"{tag}_post_loss")
    return out, (x, h, u0, ab, z, y)


def _ffn_backward(dx, entry, saved, mod, pre_g, post_g, w_up, w_down, dw_w, tag, before):
    x, h, u0, ab, z, y = saved
    _, sc, gate = mod
    dy, dgate, dpost = entry
    dz = mm_nt(dy, w_down, name=f"{tag}_down_dx", out_dtype=BF16)
    g_down = mm_tn(z, dy, name=f"{tag}_down_dw", J=2, block="a", row_chips=2)
    du0, dconv = ffn_act_bwd(dz, u0, ab, dw_w, name=f"{tag}_act_bwd")
    dh = mm_nt(du0, w_up, name=f"{tag}_up_dx", perm=_ffn_perm)
    g_up = mm_tn(h, du0, name=f"{tag}_up_dw", J=4, block="b", perm=_ffn_perm)
    dx_in, dsh, dsc, dpre, *prev = prenorm_post_bwd(dh, x, dx, pre_g, sc, *before, name=f"{tag}_prenorm_bwd")
    nb = u0.shape[1] // 4
    dconv = dconv[:, 0].reshape(4, 2, 2, nb).transpose(0, 2, 1, 3).reshape(4, 4 * nb)
    return dx_in, dict(dsh=dsh, dsc=dsc, dgate=dgate, dpre=dpre, dpost=dpost, g_up=g_up, g_down=g_down,
                       d_dw_w=dconv[0:FFN_W], d_dw_b=dconv[3:4]), prev


def _local_step(x, tgt, mods, P, late_weights=None, grads_ready=None):
    m0, m1 = mods
    h1 = prenorm(x, P["pre_mix_g"][0:1], m0[1], m0[0], name="hgrn_prenorm")
    proj = mm_nn(h1, P["hgrn_w_in"], name="hgrn_in")
    o, states = hgrn_scan(proj, P["hgrn_lb_logits"], name="hgrn_scan")
    og = hgrn_gate(o, proj, P["hgrn_gnorm_g"], name="hgrn_gate")
    y1 = mm_nn(og, P["hgrn_w_out"], name="hgrn_out")
    x1, h_f0 = post_residual_prenorm(x, y1, P["post_mix_g"][0:1], m0[2], P["pre_ffn_g"][0:1], m0[4], m0[3],
                                     name="hgrn_post")
    if late_weights is not None:
        P = {**P, **late_weights(x1)}
    (x2, h3), ffn0 = _ffn_forward(x1, h_f0, m0[3:6], P["post_ffn_g"][0:1], P["ffn_w_up"][0], P["ffn_w_down"][0],
                                  P["ffn_dw_w"][0], P["ffn_dw_b"][0:1], "ffn0",
                                  next_norm=(P["pre_mix_g"][1:2], m1[1], m1[0]))
    u = mm_nn(h3, P["conv_w_in"], name="conv_in", bias=P["conv_b_in"])
    s, cv = conv_act(u, P["conv_dw_w"], P["conv_dw_b"], P["conv_ln_g"], P["conv_ln_b"], name="conv_act")
    y3 = mm_nn(s, P["conv_w_out"], name="conv_out", bias=P["conv_b_out"])
    x3, h_f1 = post_residual_prenorm(x2, y3, P["post_mix_g"][1:2], m1[2], P["pre_ffn_g"][1:2], m1[4], m1[3],
                                     name="conv_post")
    (dx4, lcols, *entry_f1), ffn1 = _ffn_forward(x3, h_f1, m1[3:6], P["post_ffn_g"][1:2], P["ffn_w_up"][1], P["ffn_w_down"][1],
                                      P["ffn_dw_w"][1], P["ffn_dw_b"][1:2], "ffn1", tgt=tgt)
    dx3, f1, (dy3, dg1_1, dpostmix1, d_b_out) = _ffn_backward(
        dx4, entry_f1, ffn1, m1[3:6], P["pre_ffn_g"][1:2], P["post_ffn_g"][1:2], P["ffn_w_up"][1], P["ffn_w_down"][1],
        P["ffn_dw_w"][1], "ffn1", before=(y3, P["post_mix_g"][1:2], m1[2]))
    ds = mm_nt(dy3, P["conv_w_out"], name="conv_out_dx")
    g_conv_out = mm_tn(s, dy3, name="conv_out_dw", J=1, block="a", row_chips=4)
    dcv, d_ln_g, d_ln_b, d_dw_b = conv_norm_bwd(ds, cv, P["conv_ln_g"], P["conv_ln_b"], name="conv_norm_bwd")
    du, d_dw_w, d_b_in = conv_glu_bwd(dcv, u, P["conv_dw_w"], name="conv_glu_bwd")
    dh3 = mm_nt(du, P["conv_w_in"], name="conv_in_dx")
    g_conv_in = mm_tn(h3, du, name="conv_in_dw", J=2, block="b", col_chips=2)
    if grads_ready is not None:
        token = grads_ready("l1", [g_conv_in, g_conv_out, f1["g_up"], f1["g_down"]])
        m0 = tuple(m + token[0:1, 0:1] for m in m0)
    dx2, dsh1_1, dsc1_1, dpremix1, *entry_f0 = prenorm_post_bwd(
        dh3, x2, dx3, P["pre_mix_g"][1:2], m1[1], ffn0[5], P["post_ffn_g"][0:1], m0[5], name="conv_prenorm_bwd")
    dx1, f0, (dy1, dg1_0, dpostmix0, _) = _ffn_backward(
        dx2, entry_f0[:3], ffn0, m0[3:6], P["pre_ffn_g"][0:1], P["post_ffn_g"][0:1], P["ffn_w_up"][0],
        P["ffn_w_down"][0], P["ffn_dw_w"][0], "ffn0", before=(y1, P["post_mix_g"][0:1], m0[2]))
    token = grads_ready("f0", [f0["g_up"], f0["g_down"]]) if grads_ready is not None else None
    dog = mm_nt(dy1, P["hgrn_w_out"], name="hgrn_out_dx", after=token)
    g_hgrn_out = mm_tn(og, dy1, name="hgrn_out_dw", J=1, block="a", row_chips=4)
    do, dgp, d_gn = hgrn_gate_bwd(dog, o, proj, P["hgrn_gnorm_g"], name="hgrn_gate_bwd")
    d3, dlb = hgrn_scan_bwd(proj, P["hgrn_lb_logits"], states, do, name="hgrn_scan_bwd")
    g_hgrn_in = mm_tn_parts(h1, d3, dgp, name="hgrn_in_dw")
    token = grads_ready("hg", [g_hgrn_in, g_hgrn_out]) if grads_ready is not None else None
    dh1 = mm_nt_parts(d3, dgp, P["hgrn_w_in"], name="hgrn_in_dx", after=token)
    dx0, dsh1_0, dsc1_0, dpremix0 = prenorm_bwd(dh1, x, dx1, P["pre_mix_g"][0:1], m0[1], name="hgrn_prenorm_bwd")

    dmod = jnp.stack([
        jnp.concatenate([dsh1_0, dsc1_0, dg1_0, f0["dsh"], f0["dsc"], f0["dgate"]], axis=1)[0],
        jnp.concatenate([dsh1_1, dsc1_1, dg1_1, f1["dsh"], f1["dsc"], f1["dgate"]], axis=1)[0]])
    small = dict(
        loss=lcols,
        pre_mix_g=jnp.concatenate([dpremix0, dpremix1]), post_mix_g=jnp.concatenate([dpostmix0, dpostmix1]),
        pre_ffn_g=jnp.concatenate([f0["dpre"], f1["dpre"]]), post_ffn_g=jnp.concatenate([f0["dpost"], f1["dpost"]]),
        lb=dlb, hgrn_gnorm_g=d_gn, ffn_dw_b=jnp.concatenate([f0["d_dw_b"], f1["d_dw_b"]]), dmod=dmod,
        conv_b_in=d_b_in, conv_dw_w=d_dw_w[0:CONV_W], conv_dw_b=d_dw_b, conv_ln_g=d_ln_g, conv_ln_b=d_ln_b,
        conv_b_out=d_b_out, ffn_dw_w=jnp.stack([f0["d_dw_w"], f1["d_dw_w"]]))
    big = [g_hgrn_in, g_hgrn_out, g_conv_in, g_conv_out, f0["g_up"], f1["g_up"], f0["g_down"], f1["g_down"]]
    return dx0, small, big


def _pack(parts, rows=8):
    flat = jnp.concatenate([p.reshape(-1).astype(F32) for p in parts])
    per = rows * 128
    pad = (-flat.shape[0]) % per
    return jnp.pad(flat, (0, pad)).reshape(rows, -1)


def _unpack(flat, shapes):
    out, off = [], 0
    for s in shapes:
        n = 1
        for d in s:
            n *= d
        out.append(flat[..., off:off + n].reshape(flat.shape[:-1] + tuple(s)))
        off += n
    return out


def _from_chips(stacked, axis):
    moved = jnp.moveaxis(stacked, 0, axis)
    shape = list(moved.shape)
    return moved.reshape(shape[:axis] + [shape[axis] * shape[axis + 1]] + shape[axis + 2:])


def _my_shard(full, axis, q):
    n = full.shape[axis] // N_CHIPS
    return lax.dynamic_slice_in_dim(full, q * n, n, axis=axis)


def kernel(x, c, ada_w, ada_b, pre_mix_g, post_mix_g, pre_ffn_g, post_ffn_g, hgrn_w_in, hgrn_lb_logits, hgrn_gnorm_g, hgrn_w_out, conv_w_in, conv_b_in, conv_dw_w, conv_dw_b, conv_ln_g, conv_ln_b, conv_w_out, conv_b_out, ffn_w_up, ffn_dw_w, ffn_dw_b, ffn_w_down, loss_target, m_ada_w, m_ada_b, m_pre_mix_g, m_post_mix_g, m_pre_ffn_g, m_post_ffn_g, m_hgrn_w_in, m_hgrn_lb_logits, m_hgrn_gnorm_g, m_hgrn_w_out, m_conv_w_in, m_conv_b_in, m_conv_dw_w, m_conv_dw_b, m_conv_ln_g, m_conv_ln_b, m_conv_w_out, m_conv_b_out, m_ffn_w_up, m_ffn_dw_w, m_ffn_dw_b, m_ffn_w_down, v_ada_w, v_ada_b, v_pre_mix_g, v_post_mix_g, v_pre_ffn_g, v_post_ffn_g, v_hgrn_w_in, v_hgrn_lb_logits, v_hgrn_gnorm_g, v_hgrn_w_out, v_conv_w_in, v_conv_b_in, v_conv_dw_w, v_conv_dw_b, v_conv_ln_g, v_conv_ln_b, v_conv_w_out, v_conv_b_out, v_ffn_w_up, v_ffn_dw_w, v_ffn_dw_b, v_ffn_w_down):
    W = dict(ada_w=ada_w, ada_b=ada_b, pre_mix_g=pre_mix_g, post_mix_g=post_mix_g, pre_ffn_g=pre_ffn_g,
             post_ffn_g=post_ffn_g, hgrn_w_in=hgrn_w_in, hgrn_lb_logits=hgrn_lb_logits, hgrn_gnorm_g=hgrn_gnorm_g,
             hgrn_w_out=hgrn_w_out, conv_w_in=conv_w_in, conv_b_in=conv_b_in, conv_dw_w=conv_dw_w,
             conv_dw_b=conv_dw_b, conv_ln_g=conv_ln_g, conv_ln_b=conv_ln_b, conv_w_out=conv_w_out,
             conv_b_out=conv_b_out, ffn_w_up=ffn_w_up, ffn_dw_w=ffn_dw_w, ffn_dw_b=ffn_dw_b, ffn_w_down=ffn_w_down)
    M = dict(ada_w=m_ada_w, ada_b=m_ada_b, pre_mix_g=m_pre_mix_g, post_mix_g=m_post_mix_g, pre_ffn_g=m_pre_ffn_g,
             post_ffn_g=m_post_ffn_g, hgrn_w_in=m_hgrn_w_in, hgrn_lb_logits=m_hgrn_lb_logits,
             hgrn_gnorm_g=m_hgrn_gnorm_g, hgrn_w_out=m_hgrn_w_out, conv_w_in=m_conv_w_in, conv_b_in=m_conv_b_in,
             conv_dw_w=m_conv_dw_w, conv_dw_b=m_conv_dw_b, conv_ln_g=m_conv_ln_g, conv_ln_b=m_conv_ln_b,
             conv_w_out=m_conv_w_out, conv_b_out=m_conv_b_out, ffn_w_up=m_ffn_w_up, ffn_dw_w=m_ffn_dw_w,
             ffn_dw_b=m_ffn_dw_b, ffn_w_down=m_ffn_w_down)
    V = dict(ada_w=v_ada_w, ada_b=v_ada_b, pre_mix_g=v_pre_mix_g, post_mix_g=v_post_mix_g, pre_ffn_g=v_pre_ffn_g,
             post_ffn_g=v_post_ffn_g, hgrn_w_in=v_hgrn_w_in, hgrn_lb_logits=v_hgrn_lb_logits,
             hgrn_gnorm_g=v_hgrn_gnorm_g, hgrn_w_out=v_hgrn_w_out, conv_w_in=v_conv_w_in, conv_b_in=v_conv_b_in,
             conv_dw_w=v_conv_dw_w, conv_dw_b=v_conv_dw_b, conv_ln_g=v_conv_ln_g, conv_ln_b=v_conv_ln_b,
             conv_w_out=v_conv_w_out, conv_b_out=v_conv_b_out, ffn_w_up=v_ffn_w_up, ffn_dw_w=v_ffn_dw_w,
             ffn_dw_b=v_ffn_dw_b, ffn_w_down=v_ffn_w_down)
    names = list(W)
    xi, yi, ci = lax.axis_index("x"), lax.axis_index("y"), lax.axis_index("c")
    q = 2 * xi + yi
    me = 2 * q + ci
    D = x.shape[-1]
    L = ada_w.shape[0]

    small_w = ["conv_b_in", "conv_dw_w", "conv_dw_b", "conv_ln_g", "conv_ln_b", "conv_b_out", "ffn_dw_w"]
    small_axis = dict(conv_b_in=1, conv_dw_w=2, conv_dw_b=1, conv_ln_g=1, conv_ln_b=1, conv_b_out=1, ffn_dw_w=2)
    packed = _pack([c] + [W[n] for n in small_w])

    def halves(w):
        shard = w.astype(BF16).reshape(1, 2, w.shape[0] // 2, w.shape[1])
        buf = lax.empty((N_CHIPS,) + shard.shape[1:], BF16)
        return lax.dynamic_update_slice_in_dim(buf, shard, q, axis=0)

    hg_send, hg_recv, hg_bufs, hg_token = allgather_chips_start([halves(hgrn_w_in[0]), halves(hgrn_w_out[0])],
                                                                name="gather_hgrn_weights_start")
    packed, _ = lax.optimization_barrier((packed, hg_token))
    gathered = allgather_devices(packed, name="gather_small_params").reshape(N_DEV, -1)
    c_all = gathered[:, 0:D]
    per_chip = gathered.reshape(N_CHIPS, 2, -1)[:, 0, D:]
    parts = _unpack(per_chip, [W[n].shape for n in small_w])
    P = {n: _from_chips(p, small_axis[n]) for n, p in zip(small_w, parts)}
    P["conv_dw_w"] = P["conv_dw_w"][0]
    for n in ("pre_mix_g", "post_mix_g", "pre_ffn_g", "post_ffn_g", "hgrn_lb_logits", "hgrn_gnorm_g", "ffn_dw_b"):
        P[n] = W[n]

    modp = ada_mod(c_all, ada_w, name="ada_mod")
    ncol = modp.shape[-1]
    mod_all = allgather_devices(modp.reshape(L * N_DEV, ncol), name="gather_mod")
    mod_all = mod_all.reshape(N_CHIPS, 2, L, N_DEV, ncol)[:, 0]
    mod_me = lax.dynamic_index_in_dim(mod_all, me, axis=2, keepdims=False)
    mod = mod_me.transpose(1, 0, 2).reshape(L, N_CHIPS * ncol) + ada_b
    mods = [tuple(mod[l:l + 1, k * D:(k + 1) * D] for k in range(6)) for l in range(L)]

    stack = lambda t: t.reshape(N_CHIPS, t.shape[1] * t.shape[2], t.shape[3])
    rowsh = lambda t: t.reshape(1, N_CHIPS * t.shape[1] * t.shape[2], t.shape[3])
    pairs = lambda t: t.reshape(2, 2, t.shape[1], t.shape[2]).transpose(0, 2, 1, 3).reshape(2, t.shape[1], 2 * t.shape[2])
    g = forward_to_sibling(allgather_chips_wait(hg_send, hg_recv, hg_bufs, mod, name="gather_hgrn_weights_wait"),
                           name="gather_hgrn_weights_forward")
    P["hgrn_w_in"], P["hgrn_w_out"] = stack(g[0]), rowsh(g[1])
    late_shards = [conv_w_in[0], conv_w_out[0], ffn_w_up[0], ffn_w_up[1], ffn_w_down[0], ffn_w_down[1]]
    late_bufs, _, _ = lax.optimization_barrier(([halves(w) for w in late_shards], g, mod))
    send_sems, recv_sems, bufs, token = allgather_chips_start(late_bufs, name="gather_weights_start")
    mods[0] = tuple(m + token[0:1, 0:1] for m in mods[0])

    def late_weights(x1):
        landed = allgather_chips_wait(send_sems, recv_sems, bufs, x1, name="gather_weights_wait")
        g = forward_to_sibling(landed, name="gather_weights_forward")
        return dict(conv_w_in=pairs(stack(g[0])), conv_w_out=rowsh(g[1]), ffn_w_up=[stack(g[2]), stack(g[3])],
                    ffn_w_down=[rowsh(g[4]), rowsh(g[5])])

    c_idx = ci.astype(jnp.int32).reshape(1)
    pending, in_flight = {}, {}

    def chip_stage(after):
        tag, (send, recv, grads, landing) = pending.popitem()
        grads, others = exchange_wait(send, recv, grads, landing, _pair_copies, after, name=f"grad_pair_wait_{tag}")
        sums = [pair_add(g_, o_, c_idx, name=f"grad_pair_add_{tag}_{a}") for a, (g_, o_) in enumerate(zip(grads, others))]
        landing = [lax.empty((3,) + s_.shape[1:], s_.dtype) for s_ in sums]
        send, recv, sums, landing, tok = exchange_start(sums, landing, _chip_copies, 3 * len(sums),
                                                        name=f"grad_chip_exchange_start_{tag}")
        in_flight[tag] = (send, recv, sums, landing)
        return tok

    def grads_ready(tag, grads):
        tok = chip_stage(grads[0]) if pending else 0.0
        landing = [lax.empty(g_.shape[1:], g_.dtype) for g_ in grads]
        send, recv, grads, landing, tok2 = exchange_start(grads, landing, _pair_copies, len(grads),
                                                          name=f"grad_pair_start_{tag}")
        pending[tag] = (send, recv, grads, landing)
        return tok + tok2

    grad_x, small, big = _local_step(x[0], loss_target[0], mods, P, late_weights, grads_ready)

    small_names = list(small)
    packed_g = _pack([small[n] for n in small_names])
    gs_buf = lax.dynamic_update_slice_in_dim(lax.empty((N_DEV,) + packed_g.shape, F32), packed_g[None], me, axis=0)
    sg_send, sg_recv, gs_buf, _, tok_sg = exchange_start([gs_buf], [], _device_copies, N_DEV - 1,
                                                         name="gather_small_grads_start")

    tok_hg = chip_stage(tok_sg)
    G = {}
    halves = []
    for tag in ("f0", "l1"):
        sums_t, landed_t = exchange_wait(*in_flight[tag], _chip_copies, grad_x, name=f"grad_chip_exchange_wait_{tag}")
        halves += finish_reduce(sums_t, landed_t, q, ci, f"{tag}_")
    red = [f.reshape(2 * f.shape[1], f.shape[2]) for f in half_swap(halves, name="grad_half_swap")]
    G["conv_w_in"], G["conv_w_out"] = red[2][None], red[3][None]
    G["ffn_w_up"] = jnp.stack([red[0], red[4]])
    G["ffn_w_down"] = jnp.stack([red[1], red[5]])

    delta, new_m, new_v = {}, {}, {}

    def adamw_matrix(n, after=None):
        shp = W[n].shape
        two = lambda t: t.reshape(-1, shp[-1])
        d_, m_, v_ = adamw(two(W[n]), two(G[n]), two(M[n]), two(V[n]), name=f"adamw_{n}", after=after)
        delta[n], new_m[n], new_v[n] = d_.reshape(shp), m_.reshape(shp), v_.reshape(shp)

    big_names = ["ada_w", "hgrn_w_in", "hgrn_w_out", "conv_w_in", "conv_w_out", "ffn_w_up", "ffn_w_down"]
    for n in ("conv_w_in", "conv_w_out", "ffn_w_up", "ffn_w_down"):
        adamw_matrix(n, after=tok_hg)

    (gs,), _ = exchange_wait(sg_send, sg_recv, gs_buf, [], _device_copies, delta["ffn_w_down"],
                             name="gather_small_grads_wait")
    dmod_all = _unpack(gs.reshape(N_DEV, -1), [small[n].shape for n in small_names])[small_names.index("dmod")]
    tot = sum_devices(gs, name="sum_small_grads").reshape(1, -1)
    S = dict(zip(small_names, _unpack(tot, [small[n].shape for n in small_names])))
    S = {n: v[0] for n, v in S.items()}
    loss = 0.5 * jnp.sum(S["loss"]) / D

    dmod_q = lax.dynamic_slice_in_dim(dmod_all, q * ncol, ncol, axis=2)
    G["ada_w"] = ada_wgrad(c_all.T, dmod_q.transpose(1, 0, 2), name="ada_wgrad")
    G["ada_b"] = S["dmod"]
    for n in ("pre_mix_g", "post_mix_g", "pre_ffn_g", "post_ffn_g", "hgrn_gnorm_g", "ffn_dw_b"):
        G[n] = S[n]
    G["hgrn_lb_logits"] = lb_logits_grad(hgrn_lb_logits, S["lb"], name="lb_logits_grad")
    G["conv_b_in"] = _my_shard(S["conv_b_in"], 1, q)
    G["conv_dw_w"] = _my_shard(S["conv_dw_w"], 1, q)[None]
    for n in ("conv_dw_b", "conv_ln_g", "conv_ln_b", "conv_b_out"):
        G[n] = _my_shard(S[n], 1, q)
    G["ffn_dw_w"] = _my_shard(S["ffn_dw_w"], 2, q)
    adamw_matrix("ada_w")

    sums_h, landed_h = exchange_wait(*in_flight["hg"], _chip_copies, delta["ada_w"], name="grad_chip_exchange_wait_hg")
    red_h = half_swap(finish_reduce(sums_h, landed_h, q, ci, "hg_"), name="grad_half_swap_hg")
    G["hgrn_w_in"], G["hgrn_w_out"] = [f.reshape(1, 2 * f.shape[1], f.shape[2]) for f in red_h]
    for n in ("hgrn_w_in", "hgrn_w_out"):
        adamw_matrix(n)
    rest = [n for n in names if n not in big_names]
    d_, m_, v_ = adamw(_pack([W[n] for n in rest]), _pack([G[n] for n in rest]), _pack([M[n] for n in rest]),
                       _pack([V[n] for n in rest]), name="adamw_small")
    shapes = [W[n].shape for n in rest]
    for n, a, b_, c_ in zip(rest, _unpack(d_.reshape(-1), shapes), _unpack(m_.reshape(-1), shapes),
                            _unpack(v_.reshape(-1), shapes)):
        delta[n], new_m[n], new_v[n] = a, b_, c_

    return (loss, grad_x[None], *[G[n].reshape(W[n].shape) for n in names], *[delta[n] for n in names],
            *[new_m[n] for n in names], *[new_v[n] for n in names])
```

```python
import jax
import jax.numpy as jnp
from jax import lax
from jax.experimental import pallas as pl
from jax.experimental.pallas import tpu as pltpu

F32 = jnp.float32
BF16 = jnp.bfloat16
EPS = 1e-6
HEAD = 128
BLK = 16
NEG = -1e30
CONV_W = 31
FFN_W = 3
N_CHIPS = 4
N_DEV = 8
SUB = 8
LANE = 128
V7X_VMEM_LIMIT = 56 * 1024 * 1024
MESH = pl.DeviceIdType.MESH
HBM = pl.BlockSpec(memory_space=pltpu.HBM)
VMEM_SPEC = pl.BlockSpec(memory_space=pltpu.VMEM)

ADAM_LR = 0.001
ADAM_B1 = 0.9
ADAM_B2 = 0.999
ADAM_EPS = 1e-08
ADAM_WD = 0.01
ADAM_STEP = 10


def _cp(*sem):
    return pltpu.CompilerParams(dimension_semantics=sem, vmem_limit_bytes=V7X_VMEM_LIMIT)


def _sig(x):
    return 0.5 * jnp.tanh(0.5 * x) + 0.5


def _silu(x):
    return x * _sig(x)


def _dsilu(x):
    s = _sig(x)
    return s * (1.0 + x * (1.0 - s))


def _dot(a, b):
    return jnp.dot(a, b, preferred_element_type=F32)


def _dot_nt(a, b):
    return lax.dot_general(a, b, (((1,), (1,)), ((), ())), preferred_element_type=F32)


def _dot_tn(a, b):
    return lax.dot_general(a, b, (((0,), (0,)), ((), ())), preferred_element_type=F32)


def _colsum(x):
    return jnp.sum(x, axis=0, keepdims=True)


def _rowmean(x):
    return jnp.mean(x, axis=-1, keepdims=True)


def _ffn_perm(j):
    return (j % 2) * 2 + j // 2


def _tile(n, pref):
    if n <= pref:
        return n
    t = pref - pref % 8
    while n % t:
        t -= 8
    return t


def mm_nn(a, w, *, name, bias=None, out_dtype=F32, perm=None, tm=1024, after=None):
    T, K = a.shape
    J, _, nb = w.shape
    tm = min(tm, T)
    col = (lambda j: j) if perm is None else perm

    def body(a_ref, w_ref, *rest):
        acc = _dot(a_ref[...], w_ref[...])
        if bias is not None:
            acc = acc + rest[0][...]
        rest[-1][...] = acc.astype(out_dtype)

    in_specs = [pl.BlockSpec((tm, K), lambda j, i: (i, 0)), pl.BlockSpec((None, K, nb), lambda j, i: (j, 0, 0))]
    args = [a, w]
    if bias is not None:
        in_specs.append(pl.BlockSpec((1, nb), lambda j, i: (0, j)))
        args.append(bias)
    if after is not None:
        in_specs.append(pl.BlockSpec(memory_space=pl.ANY))
        args.append(after)
    return pl.pallas_call(
        body, grid=(J, T // tm), in_specs=in_specs,
        out_specs=pl.BlockSpec((tm, nb), lambda j, i: (i, col(j))),
        out_shape=jax.ShapeDtypeStruct((T, J * nb), out_dtype), name=name,
        compiler_params=_cp("parallel", "parallel"))(*args)


def mm_nt(a, w, *, name, out_dtype=F32, perm=None, tm=1024, after=None):
    T = a.shape[0]
    J, K, nb = w.shape
    tm = min(tm, T)
    col = (lambda j: j) if perm is None else perm
    deps = [] if after is None else [after]

    def body(a_ref, w_ref, *rest):
        o_ref, acc_ref = rest[len(deps):]
        j = pl.program_id(1)

        @pl.when(j == 0)
        def _():
            acc_ref[...] = jnp.zeros_like(acc_ref)

        acc_ref[...] += _dot_nt(a_ref[...], w_ref[...])

        @pl.when(j == J - 1)
        def _():
            o_ref[...] = acc_ref[...].astype(out_dtype)

    return pl.pallas_call(
        body, grid=(T // tm, J),
        in_specs=[pl.BlockSpec((tm, nb), lambda i, j: (i, col(j))), pl.BlockSpec((None, K, nb), lambda i, j: (j, 0, 0))]
        + [pl.BlockSpec(memory_space=pl.ANY)] * len(deps),
        out_specs=pl.BlockSpec((tm, K), lambda i, j: (i, 0)),
        out_shape=jax.ShapeDtypeStruct((T, K), out_dtype),
        scratch_shapes=[pltpu.VMEM((tm, K), F32)], name=name,
        compiler_params=_cp("parallel", "arbitrary"))(a, w, *deps)


def mm_tn(a, b, *, name, J, block, row_chips=1, col_chips=1, perm=None, tk=1024):
    T = a.shape[0]
    tk = min(tk, T)
    col = (lambda j: j) if perm is None else perm
    if block == "b":
        rows, nb = a.shape[1], b.shape[1] // J
        a_spec = pl.BlockSpec((tk, rows), lambda j, t: (t, 0))
        b_spec = pl.BlockSpec((tk, nb), lambda j, t: (t, col(j)))
    else:
        rows, nb = a.shape[1] // J, b.shape[1]
        a_spec = pl.BlockSpec((tk, rows), lambda j, t: (t, col(j)))
        b_spec = pl.BlockSpec((tk, nb), lambda j, t: (t, 0))
    rh = rows // (2 * row_chips)
    nc = nb // col_chips
    chips = [(rc, cc) for rc in range(row_chips) for cc in range(col_chips)]

    def body(a_ref, b_ref, o_ref):
        @pl.when(pl.program_id(1) == 0)
        def _():
            o_ref[...] = jnp.zeros_like(o_ref)

        acc = _dot_tn(a_ref[...], b_ref[...])
        for ch, (rc, cc) in enumerate(chips):
            for hf in range(2):
                r0 = (rc * 2 + hf) * rh
                o_ref[hf, ch] += acc[r0:r0 + rh, cc * nc:(cc + 1) * nc]

    return pl.pallas_call(
        body, grid=(J, T // tk), in_specs=[a_spec, b_spec],
        out_specs=pl.BlockSpec((2, len(chips), rh, nc), lambda j, t: (0, j, 0, 0)),
        out_shape=jax.ShapeDtypeStruct((2, J * len(chips), rh, nc), F32), name=name,
        compiler_params=_cp("parallel", "arbitrary"))(a, b)


def mm_nt_parts(s3, g, w, *, name, tm=1024, after=None):
    n3, T, nb = s3.shape
    J, K, _ = w.shape
    tm = min(tm, T)
    deps = [] if after is None else [after]

    def body(s_ref, g_ref, w_ref, *rest):
        o_ref, acc_ref = rest[len(deps):]
        j = pl.program_id(1)

        @pl.when(j == 0)
        def _():
            acc_ref[...] = jnp.zeros_like(acc_ref)

        @pl.when(j < n3)
        def _():
            acc_ref[...] += _dot_nt(s_ref[...], w_ref[...])

        @pl.when(j == n3)
        def _():
            acc_ref[...] += _dot_nt(g_ref[...], w_ref[...])

        @pl.when(j == J - 1)
        def _():
            o_ref[...] = acc_ref[...]

    return pl.pallas_call(
        body, grid=(T // tm, J),
        in_specs=[pl.BlockSpec((None, tm, nb), lambda i, j: (jnp.minimum(j, n3 - 1), i, 0)),
                  pl.BlockSpec((tm, nb), lambda i, j: (i, 0)), pl.BlockSpec((None, K, nb), lambda i, j: (j, 0, 0))]
        + [pl.BlockSpec(memory_space=pl.ANY)] * len(deps),
        out_specs=pl.BlockSpec((tm, K), lambda i, j: (i, 0)), out_shape=jax.ShapeDtypeStruct((T, K), F32),
        scratch_shapes=[pltpu.VMEM((tm, K), F32)], name=name,
        compiler_params=_cp("parallel", "arbitrary"))(s3, g, w, *deps)


def mm_tn_parts(a, s3, g, *, name, tk=1024):
    n3, T, nb = s3.shape
    J = n3 + 1
    tk = min(tk, T)
    rows = a.shape[1]
    rh = rows // 2

    def body(a_ref, s_ref, g_ref, o_ref):
        j = pl.program_id(0)

        @pl.when(pl.program_id(1) == 0)
        def _():
            o_ref[...] = jnp.zeros_like(o_ref)

        def add(b_ref):
            acc = _dot_tn(a_ref[...], b_ref[...])
            for hf in range(2):
                o_ref[hf, 0] += acc[hf * rh:(hf + 1) * rh, :]

        pl.when(j < n3)(lambda: add(s_ref))
        pl.when(j == n3)(lambda: add(g_ref))

    return pl.pallas_call(
        body, grid=(J, T // tk),
        in_specs=[pl.BlockSpec((tk, rows), lambda j, t: (t, 0)),
                  pl.BlockSpec((None, tk, nb), lambda j, t: (jnp.minimum(j, n3 - 1), t, 0)),
                  pl.BlockSpec((tk, nb), lambda j, t: (t, 0))],
        out_specs=pl.BlockSpec((2, 1, rh, nb), lambda j, t: (0, j, 0, 0)),
        out_shape=jax.ShapeDtypeStruct((2, J, rh, nb), F32), name=name,
        compiler_params=_cp("parallel", "arbitrary"))(a, s3, g)


def _row(tm, w):
    return pl.BlockSpec((tm, w), lambda i: (i, 0))


def _full(r, w):
    return pl.BlockSpec((r, w), lambda i: (0, 0))


def _acc_init(i, *refs):
    @pl.when(i == 0)
    def _():
        for r in refs:
            r[...] = jnp.zeros_like(r)


def prenorm(x, g, sc, sh, *, name, tm=512):
    T, D = x.shape
    tm = min(tm, T)

    def body(x_ref, g_ref, sc_ref, sh_ref, h_ref):
        xv = x_ref[...]
        r = lax.rsqrt(_rowmean(xv * xv) + EPS)
        h_ref[...] = ((xv * r) * g_ref[...] * (1.0 + sc_ref[...]) + sh_ref[...]).astype(BF16)

    return pl.pallas_call(
        body, grid=(T // tm,), in_specs=[_row(tm, D), _full(1, D), _full(1, D), _full(1, D)],
        out_specs=_row(tm, D), out_shape=jax.ShapeDtypeStruct((T, D), BF16), name=name,
        compiler_params=_cp("parallel"))(x, g, sc, sh)


def post_residual_prenorm(x, y, g, gate, g2, sc2, sh2, *, name, tm=512):
    T, D = x.shape
    tm = min(tm, T)

    def body(x_ref, y_ref, g_ref, gate_ref, g2_ref, sc2_ref, sh2_ref, o_ref, h_ref):
        yv = y_ref[...]
        r = lax.rsqrt(_rowmean(yv * yv) + EPS)
        out = x_ref[...] + gate_ref[...] * ((yv * r) * g_ref[...])
        o_ref[...] = out
        r2 = lax.rsqrt(_rowmean(out * out) + EPS)
        h_ref[...] = ((out * r2) * g2_ref[...] * (1.0 + sc2_ref[...]) + sh2_ref[...]).astype(BF16)

    return pl.pallas_call(
        body, grid=(T // tm,), in_specs=[_row(tm, D), _row(tm, D)] + [_full(1, D)] * 5,
        out_specs=[_row(tm, D), _row(tm, D)],
        out_shape=[jax.ShapeDtypeStruct((T, D), F32), jax.ShapeDtypeStruct((T, D), BF16)], name=name,
        compiler_params=_cp("parallel"))(x, y, g, gate, g2, sc2, sh2)


def post_residual_loss(x, y, g, gate, tgt, *, name, tm=512):
    T, D = x.shape
    tm = min(tm, T)

    def body(x_ref, y_ref, g_ref, gate_ref, t_ref, dx_ref, l_ref, dy_ref, dgate_ref, dg_ref):
        _acc_init(pl.program_id(0), l_ref, dgate_ref, dg_ref)
        yv = y_ref[...]
        r = lax.rsqrt(_rowmean(yv * yv) + EPS)
        yn = yv * r
        gv = g_ref[...]
        gt = gate_ref[...]
        e = x_ref[...] + gt * (yn * gv) - t_ref[...]
        dxv = e * (1.0 / D)
        dx_ref[...] = dxv
        l_ref[...] += _colsum(e * e)
        dgate_ref[...] += _colsum(dxv * (yn * gv))
        dg_ref[...] += _colsum(dxv * gt * yn)
        dyn = dxv * gt * gv
        dy_ref[...] = (r * (dyn - yn * _rowmean(dyn * yn))).astype(BF16)

    vec = jax.ShapeDtypeStruct((1, D), F32)
    return pl.pallas_call(
        body, grid=(T // tm,), in_specs=[_row(tm, D), _row(tm, D), _full(1, D), _full(1, D), _row(tm, D)],
        out_specs=[_row(tm, D), _full(1, D), _row(tm, D), _full(1, D), _full(1, D)],
        out_shape=[jax.ShapeDtypeStruct((T, D), F32), vec, jax.ShapeDtypeStruct((T, D), BF16), vec, vec], name=name,
        compiler_params=_cp("arbitrary"))(x, y, g, gate, tgt)


def prenorm_bwd(dh, x, dres, g, sc, *, name, tm=512):
    T, D = x.shape
    tm = min(tm, T)

    def body(dh_ref, x_ref, dres_ref, g_ref, sc_ref, dx_ref, dsh_ref, dsc_ref, dg_ref):
        _acc_init(pl.program_id(0), dsh_ref, dsc_ref, dg_ref)
        xv = x_ref[...]
        dhv = dh_ref[...]
        r = lax.rsqrt(_rowmean(xv * xv) + EPS)
        xn = xv * r
        gv = g_ref[...]
        one_sc = 1.0 + sc_ref[...]
        dsh_ref[...] += _colsum(dhv)
        dsc_ref[...] += _colsum(dhv * (xn * gv))
        dg_ref[...] += _colsum(dhv * one_sc * xn)
        dxn = dhv * one_sc * gv
        dx_ref[...] = dres_ref[...] + r * (dxn - xn * _rowmean(dxn * xn))

    return pl.pallas_call(
        body, grid=(T // tm,), in_specs=[_row(tm, D), _row(tm, D), _row(tm, D), _full(1, D), _full(1, D)],
        out_specs=[_row(tm, D), _full(1, D), _full(1, D), _full(1, D)],
        out_shape=[jax.ShapeDtypeStruct((T, D), F32)] + [jax.ShapeDtypeStruct((1, D), F32)] * 3, name=name,
        compiler_params=_cp("arbitrary"))(dh, x, dres, g, sc)


def prenorm_post_bwd(dh, x, dres, g, sc, y, g_post, gate, *, name, tm=512):
    T, D = x.shape
    tm = min(tm, T)

    def body(dh_ref, x_ref, dres_ref, g_ref, sc_ref, y_ref, gp_ref, gate_ref,
             dx_ref, dsh_ref, dsc_ref, dg_ref, dy_ref, dgate_ref, dgp_ref, dbias_ref):
        _acc_init(pl.program_id(0), dsh_ref, dsc_ref, dg_ref, dgate_ref, dgp_ref, dbias_ref)
        xv = x_ref[...]
        dhv = dh_ref[...]
        r = lax.rsqrt(_rowmean(xv * xv) + EPS)
        xn = xv * r
        gv = g_ref[...]
        one_sc = 1.0 + sc_ref[...]
        dsh_ref[...] += _colsum(dhv)
        dsc_ref[...] += _colsum(dhv * (xn * gv))
        dg_ref[...] += _colsum(dhv * one_sc * xn)
        dxn = dhv * one_sc * gv
        dxv = dres_ref[...] + r * (dxn - xn * _rowmean(dxn * xn))
        dx_ref[...] = dxv
        yv = y_ref[...]
        ry = lax.rsqrt(_rowmean(yv * yv) + EPS)
        yn = yv * ry
        gp = gp_ref[...]
        gt = gate_ref[...]
        dgate_ref[...] += _colsum(dxv * (yn * gp))
        dgp_ref[...] += _colsum(dxv * gt * yn)
        dyn = dxv * gt * gp
        dy = ry * (dyn - yn * _rowmean(dyn * yn))
        dbias_ref[...] += _colsum(dy)
        dy_ref[...] = dy.astype(BF16)

    vec = jax.ShapeDtypeStruct((1, D), F32)
    return pl.pallas_call(
        body, grid=(T // tm,),
        in_specs=[_row(tm, D)] * 3 + [_full(1, D)] * 2 + [_row(tm, D)] + [_full(1, D)] * 2,
        out_specs=[_row(tm, D)] + [_full(1, D)] * 3 + [_row(tm, D)] + [_full(1, D)] * 3,
        out_shape=[jax.ShapeDtypeStruct((T, D), F32), vec, vec, vec, jax.ShapeDtypeStruct((T, D), BF16), vec, vec, vec],
        name=name, compiler_params=_cp("arbitrary"))(dh, x, dres, g, sc, y, g_post, gate)


HALO = 16


def _shift_helpers():
    rid = lax.broadcasted_iota(jnp.int32, (SUB, LANE), 0)

    def down(cur, prev, k):
        return pltpu.roll(jnp.where(rid >= SUB - k, prev, cur), k, 0)

    def up(cur, nxt, k):
        return pltpu.roll(jnp.where(rid < k, nxt, cur), SUB - k, 0)

    return down, up


def _ffn_sides(c, nb, wa_ref, wb_ref, ba_ref, bb_ref):
    cols = slice(c * LANE, (c + 1) * LANE)
    return [(cols, [wa_ref[k:k + 1, cols] for k in range(FFN_W)], ba_ref[:, cols]),
            (slice(nb + c * LANE, nb + (c + 1) * LANE), [wb_ref[k:k + 1, cols] for k in range(FFN_W)],
             bb_ref[:, cols])]


def _ffn_specs(tm, nb, hb, idx):
    return [pl.BlockSpec((tm, 2 * nb), lambda jc, i: (idx(i), jc)),
            pl.BlockSpec((HALO, 2 * nb), lambda jc, i: (jnp.maximum(idx(i) * hb - 1, 0), jc)),
            pl.BlockSpec((FFN_W, nb), lambda jc, i: (0, jc)),
            pl.BlockSpec((FFN_W, nb), lambda jc, i: (0, jc + 2)),
            pl.BlockSpec((1, nb), lambda jc, i: (0, jc)),
            pl.BlockSpec((1, nb), lambda jc, i: (0, jc + 2))]


def ffn_act(u0p, dw_w, dw_b, *, name, tm=256):
    T, W = u0p.shape
    nb = W // 4
    tm = min(tm, T)
    unroll = 4
    rows16 = 2 * SUB

    def body(u_ref, halo_ref, wa_ref, wb_ref, ba_ref, bb_ref, z_ref, ab_ref):
        i = pl.program_id(1)
        down, _ = _shift_helpers()
        for c in range(nb // LANE):
            cols = slice(c * LANE, (c + 1) * LANE)
            side = _ffn_sides(c, nb, wa_ref, wb_ref, ba_ref, bb_ref)

            def rows(j, prev):
                prev = list(prev)
                for m in range(unroll):
                    r0 = pl.multiple_of((j * unroll + m) * rows16, rows16)
                    x = [u_ref[pl.ds(r0, rows16), cs].astype(F32) for cs, _, _ in side]
                    conv = [[None, None], [None, None]]
                    for hf in range(2):
                        for n, (_, w, b) in enumerate(side):
                            cur = x[n][hf * SUB:(hf + 1) * SUB, :]
                            conv[n][hf] = b + w[2] * cur + w[1] * down(cur, prev[n], 1) + w[0] * down(cur, prev[n], 2)
                            prev[n] = cur
                    a, b = [jnp.concatenate(conv[n], axis=0) for n in range(2)]
                    z_ref[pl.ds(r0, rows16), cols] = (_silu(a) * b).astype(BF16)
                    ab_ref[pl.ds(r0, rows16), side[0][0]] = a.astype(BF16)
                    ab_ref[pl.ds(r0, rows16), side[1][0]] = b.astype(BF16)
                return tuple(prev)

            first = [jnp.where(i == 0, 0.0, halo_ref[:, cs].astype(F32)[SUB:2 * SUB, :]) for cs, _, _ in side]
            lax.fori_loop(0, tm // (rows16 * unroll), rows, tuple(first))

    return pl.pallas_call(
        body, grid=(2, T // tm), in_specs=_ffn_specs(tm, nb, tm // HALO, lambda i: i),
        out_specs=[pl.BlockSpec((tm, nb), lambda jc, i: (i, jc)), pl.BlockSpec((tm, 2 * nb), lambda jc, i: (i, jc))],
        out_shape=[jax.ShapeDtypeStruct((T, 2 * nb), BF16), jax.ShapeDtypeStruct((T, W), BF16)], name=name,
        compiler_params=_cp("parallel", "arbitrary"))(u0p, u0p, dw_w, dw_w, dw_b, dw_b)


def ffn_act_bwd(dz, u0p, ab, dw_w, *, name, tm=256):
    T, W = u0p.shape
    nb = W // 4
    tm = min(tm, T)
    nt = T // tm
    unroll = 4
    rows16 = 2 * SUB
    n_it = tm // (rows16 * unroll)

    def body(dz_ref, u_ref, ab_ref, wa_ref, wb_ref, du0_ref, dw_ref, carry):
        i = pl.program_id(1)
        _acc_init(i, dw_ref)
        _, up = _shift_helpers()
        for c in range(nb // LANE):
            cols = slice(c * LANE, (c + 1) * LANE)
            side = [(cols, [wa_ref[k:k + 1, cols] for k in range(FFN_W)]),
                    (slice(nb + c * LANE, nb + (c + 1) * LANE), [wb_ref[k:k + 1, cols] for k in range(FFN_W)])]

            def rows(j, st):
                nxt, acc = list(st[0:2]), list(st[2:10])
                for m in range(unroll):
                    r0 = pl.multiple_of(((n_it - 1 - j) * unroll + unroll - 1 - m) * rows16, rows16)
                    dzv = dz_ref[pl.ds(r0, rows16), cols].astype(F32)
                    a, b = [ab_ref[pl.ds(r0, rows16), cs].astype(F32) for cs, _ in side]
                    x = [u_ref[pl.ds(r0, rows16), cs].astype(F32) for cs, _ in side]
                    sa = _sig(a)
                    d16 = [dzv * b * (sa * (1.0 + a * (1.0 - sa))), dzv * (a * sa)]
                    out = [[None, None], [None, None]]
                    for hf in (1, 0):
                        half = slice(hf * SUB, (hf + 1) * SUB)
                        for n in range(2):
                            w = side[n][1]
                            d = d16[n][half, :]
                            u = x[n][half, :]
                            up1, up2 = up(d, nxt[n], 1), up(d, nxt[n], 2)
                            acc[4 * n + 0] = acc[4 * n + 0] + up2 * u
                            acc[4 * n + 1] = acc[4 * n + 1] + up1 * u
                            acc[4 * n + 2] = acc[4 * n + 2] + d * u
                            acc[4 * n + 3] = acc[4 * n + 3] + d
                            out[n][hf] = w[2] * d + w[1] * up1 + w[0] * up2
                            nxt[n] = d
                    for n in range(2):
                        du0_ref[pl.ds(r0, rows16), side[n][0]] = jnp.concatenate(out[n], axis=0).astype(BF16)
                return (*nxt, *acc)

            init = [jnp.where(i == 0, 0.0, carry[:, cs]) for cs, _ in side] + [jnp.zeros((SUB, LANE), F32)] * 8
            st = lax.fori_loop(0, n_it, rows, tuple(init))
            for n in range(2):
                carry[:, side[n][0]] = st[n]
                for k in range(4):
                    dw_ref[k, :, side[n][0]] += st[2 + 4 * n + k]

        @pl.when(i == nt - 1)
        def _():
            for k in range(4):
                dw_ref[k, 0:1, :] = _colsum(dw_ref[k])

    rev = lambda i: nt - 1 - i
    wide = pl.BlockSpec((tm, 2 * nb), lambda jc, i: (rev(i), jc))
    return pl.pallas_call(
        body, grid=(2, nt),
        in_specs=[pl.BlockSpec((tm, nb), lambda jc, i: (rev(i), jc)), wide, wide,
                  pl.BlockSpec((FFN_W, nb), lambda jc, i: (0, jc)), pl.BlockSpec((FFN_W, nb), lambda jc, i: (0, jc + 2))],
        out_specs=[wide, pl.BlockSpec((4, SUB, 2 * nb), lambda jc, i: (0, 0, jc))],
        out_shape=[jax.ShapeDtypeStruct((T, W), BF16), jax.ShapeDtypeStruct((4, SUB, W), F32)],
        scratch_shapes=[pltpu.VMEM((SUB, 2 * nb), F32)], name=name,
        compiler_params=_cp("parallel", "arbitrary"))(dz, u0p, ab, dw_w, dw_w)


CHALO = 32
CCOL = 256


def _phase_copies(buf, shifted, tm):
    n = tm + CHALO - SUB
    for p in range(1, SUB):
        shifted[p - 1, 0:n, :] = buf[p:p + n, :]


def _shifted(buf, shifted, r, tm, c0):
    m, p = divmod(r, SUB)
    src = buf if p == 0 else shifted.at[p - 1]
    return src[m * SUB:m * SUB + tm, c0:c0 + CCOL]


def conv_act(u, dw_w, dw_b, ln_g, ln_b, *, name, tm=128):
    T, D2 = u.shape
    D = D2 // 2
    tm = min(tm, T)
    hb = tm // CHALO

    def body(u_ref, halo_ref, w_ref, b_ref, g_ref, be_ref, s_ref, cv_ref, gbuf, gsh):
        i = pl.program_id(0)
        hv = halo_ref[...]
        gbuf[0:CHALO, :] = jnp.where(i == 0, 0.0, hv[:, 0:D] * _sig(hv[:, D:D2]))
        uv = u_ref[...]
        gbuf[CHALO:CHALO + tm, :] = uv[:, 0:D] * _sig(uv[:, D:D2])
        _phase_copies(gbuf, gsh, tm)
        for c0 in range(0, D, CCOL):
            acc = jnp.zeros((tm, CCOL), F32) + b_ref[:, c0:c0 + CCOL]
            for k in range(CONV_W):
                acc = acc + w_ref[k:k + 1, c0:c0 + CCOL] * _shifted(gbuf, gsh, CHALO - (CONV_W - 1) + k, tm, c0)
            cv_ref[:, c0:c0 + CCOL] = acc
        cv = cv_ref[...]
        mu = _rowmean(cv)
        xc = cv - mu
        nh = xc * lax.rsqrt(_rowmean(xc * xc) + EPS)
        s_ref[...] = _silu(nh * g_ref[...] + be_ref[...]).astype(BF16)

    return pl.pallas_call(
        body, grid=(T // tm,),
        in_specs=[_row(tm, D2), pl.BlockSpec((CHALO, D2), lambda i: (jnp.maximum(i * hb - 1, 0), 0)),
                  _full(CONV_W, D), _full(1, D), _full(1, D), _full(1, D)],
        out_specs=[_row(tm, D), _row(tm, D)],
        out_shape=[jax.ShapeDtypeStruct((T, D), BF16), jax.ShapeDtypeStruct((T, D), F32)],
        scratch_shapes=[pltpu.VMEM((tm + CHALO, D), F32), pltpu.VMEM((SUB - 1, tm + CHALO, D), F32)], name=name,
        compiler_params=_cp("arbitrary"))(u, u, dw_w, dw_b, ln_g, ln_b)


def conv_norm_bwd(ds, cv, ln_g, ln_b, *, name, tm=512):
    T, D = cv.shape
    tm = min(tm, T)

    def body(ds_ref, cv_ref, g_ref, be_ref, dcv_ref, dg_ref, dbe_ref, dcb_ref):
        _acc_init(pl.program_id(0), dg_ref, dbe_ref, dcb_ref)
        cv_ = cv_ref[...]
        mu = _rowmean(cv_)
        xc = cv_ - mu
        rstd = lax.rsqrt(_rowmean(xc * xc) + EPS)
        nh = xc * rstd
        gv = g_ref[...]
        dln = ds_ref[...] * _dsilu(nh * gv + be_ref[...])
        dg_ref[...] += _colsum(dln * nh)
        dbe_ref[...] += _colsum(dln)
        dnh = dln * gv
        dcv = rstd * (dnh - _rowmean(dnh) - nh * _rowmean(dnh * nh))
        dcb_ref[...] += _colsum(dcv)
        dcv_ref[...] = dcv

    return pl.pallas_call(
        body, grid=(T // tm,), in_specs=[_row(tm, D), _row(tm, D), _full(1, D), _full(1, D)],
        out_specs=[_row(tm, D), _full(1, D), _full(1, D), _full(1, D)],
        out_shape=[jax.ShapeDtypeStruct((T, D), F32)] + [jax.ShapeDtypeStruct((1, D), F32)] * 3, name=name,
        compiler_params=_cp("arbitrary"))(ds, cv, ln_g, ln_b)


def conv_glu_bwd(dcv, u, dw_w, *, name, tm=128):
    T, D2 = u.shape
    D = D2 // 2
    tm = min(tm, T)
    nt = T // tm
    hb = tm // CHALO

    def body(dcv_ref, dnext_ref, u_ref, w_ref, du_ref, dw_ref, dbin_ref, dbuf, dsh):
        i = pl.program_id(0)
        _acc_init(i, dw_ref, dbin_ref)
        uv = u_ref[...]
        av = uv[:, 0:D]
        sg = _sig(uv[:, D:D2])
        glu = av * sg
        dbuf[0:tm, :] = dcv_ref[...]
        dbuf[tm:tm + CHALO, :] = jnp.where(i == nt - 1, 0.0, dnext_ref[...])
        _phase_copies(dbuf, dsh, tm)
        for c0 in range(0, D, CCOL):
            glu_c = glu[:, c0:c0 + CCOL]
            acc = jnp.zeros((tm, CCOL), F32)
            for k in range(CONV_W):
                moved = _shifted(dbuf, dsh, CONV_W - 1 - k, tm, c0)
                dw_ref[k:k + 1, c0:c0 + CCOL] += _colsum(moved * glu_c)
                acc = acc + w_ref[k:k + 1, c0:c0 + CCOL] * moved
            a_c = av[:, c0:c0 + CCOL]
            s_c = sg[:, c0:c0 + CCOL]
            da = acc * s_c
            dgt = acc * a_c * s_c * (1.0 - s_c)
            dbin_ref[:, c0:c0 + CCOL] += _colsum(da)
            dbin_ref[:, D + c0:D + c0 + CCOL] += _colsum(dgt)
            du_ref[:, c0:c0 + CCOL] = da.astype(BF16)
            du_ref[:, D + c0:D + c0 + CCOL] = dgt.astype(BF16)

    return pl.pallas_call(
        body, grid=(nt,),
        in_specs=[_row(tm, D), pl.BlockSpec((CHALO, D), lambda i: (jnp.minimum((i + 1) * hb, T // CHALO - 1), 0)),
                  _row(tm, D2), _full(CONV_W, D)],
        out_specs=[_row(tm, D2), _full(CHALO, D), _full(1, D2)],
        out_shape=[jax.ShapeDtypeStruct((T, D2), BF16), jax.ShapeDtypeStruct((CHALO, D), F32),
                   jax.ShapeDtypeStruct((1, D2), F32)],
        scratch_shapes=[pltpu.VMEM((tm + CHALO, D), F32), pltpu.VMEM((SUB - 1, tm + CHALO, D), F32)],
        name=name, compiler_params=_cp("arbitrary"))(dcv, dcv, u, dw_w)


HB = 8


def _lb0(lg_ref):
    l0, l1, l2 = lg_ref[0:1, :], lg_ref[1:2, :], lg_ref[2:3, :]
    m = jnp.maximum(jnp.maximum(l0, l1), l2)
    e0 = jnp.exp(l0 - m)
    return e0 / (e0 + jnp.exp(l1 - m) + jnp.exp(l2 - m))


def _mm_exact(m01, x):
    hi = x.astype(BF16)
    r1 = x - hi.astype(F32)
    mid = r1.astype(BF16)
    lo = (r1 - mid.astype(F32)).astype(BF16)
    return _dot(m01, hi) + _dot(m01, mid) + _dot(m01, lo)


def _block_tri(tm):
    r = jnp.arange(tm)[:, None]
    c = jnp.arange(tm)[None, :]
    same = (r // BLK) == (c // BLK)
    return (same & (c <= r)).astype(BF16), (same & (c >= r)).astype(BF16)


def _halves(x):
    return [x[0:SUB, :], x[SUB:BLK, :]]


def _live_halves(s):
    return ([(0, s)] if s < SUB else []) + [(1, max(s - SUB, 0))]


def _const_spec(shape):
    return pl.BlockSpec(shape, lambda h, i: (0, 0))


def _hgrn_specs(H, hb, tm, idx):
    g = H // hb
    return [pl.BlockSpec((tm, hb * HEAD), lambda h, i: (idx(i), h)),
            pl.BlockSpec((tm, hb * HEAD), lambda h, i: (idx(i), g + h)),
            pl.BlockSpec((tm, hb * HEAD), lambda h, i: (idx(i), 2 * g + h)),
            pl.BlockSpec((3, hb * HEAD), lambda h, i: (0, h))]


def hgrn_scan(proj, lb_logits, *, name, tm=128):
    T = proj.shape[0]
    H = proj.shape[1] // (4 * HEAD)
    hb = min(HB, H)
    tm = min(tm, T)
    nt = T // tm
    nblk = tm // BLK
    tril, _ = _block_tri(tm)
    heads = [slice(hh * HEAD, (hh + 1) * HEAD) for hh in range(hb)]

    def body(qp_ref, fz_ref, v_ref, lg_ref, tril_ref, o_ref, st_ref, S_ref, q_s, k_s, b_s):
        @pl.when(pl.program_id(1) == 0)
        def _():
            S_ref[...] = jnp.zeros_like(S_ref)

        st_ref[...] = S_ref[...]
        lb = _lb0(lg_ref)
        f = lb + (1.0 - lb) * _sig(fz_ref[...])
        q_s[...] = _silu(qp_ref[...])
        k_s[...] = 1.0 - f
        b_s[...] = _mm_exact(tril_ref[...], jnp.log(f))
        rows = lax.broadcasted_iota(jnp.int32, (BLK, HEAD), 0)
        S = [S_ref[hh] for hh in range(hb)]
        for nb in range(nblk):
            blk = slice(nb * BLK, (nb + 1) * BLK)
            last = slice(nb * BLK + BLK - 1, nb * BLK + BLK)
            qb = [q_s[blk, c] for c in heads]
            bb = [b_s[blk, c] for c in heads]
            o = [_dot_nt((qb[hh] * jnp.exp(bb[hh])).astype(BF16), S[hh].astype(BF16)) for hh in range(hb)]
            for hh, c in enumerate(heads):
                bc = b_s[last, c]
                kd = k_s[blk, c] * jnp.exp(bc - bb[hh])
                S[hh] = S[hh] * jnp.exp(bc) + _dot_tn(v_ref[blk, c].astype(BF16), kd.astype(BF16))
            for s in range(BLK):
                r = slice(nb * BLK + s, nb * BLK + s + 1)
                for hh, c in enumerate(heads):
                    dec = jnp.exp(jnp.where(rows >= s, bb[hh] - b_s[r, c], NEG))
                    a = jnp.sum(qb[hh] * k_s[r, c] * dec, axis=-1, keepdims=True)
                    o[hh] = o[hh] + a * v_ref[r, c]
            for hh, c in enumerate(heads):
                o_ref[blk, c] = o[hh]
        for hh in range(hb):
            S_ref[hh] = S[hh]

    return pl.pallas_call(
        body, grid=(H // hb, nt),
        in_specs=_hgrn_specs(H, hb, tm, lambda i: i) + [_const_spec((tm, tm))],
        out_specs=[pl.BlockSpec((tm, hb * HEAD), lambda h, i: (i, h)),
                   pl.BlockSpec((None, hb, HEAD, HEAD), lambda h, i: (i, h, 0, 0))],
        out_shape=[jax.ShapeDtypeStruct((T, H * HEAD), F32), jax.ShapeDtypeStruct((nt, H, HEAD, HEAD), F32)],
        scratch_shapes=[pltpu.VMEM((hb, HEAD, HEAD), F32)] + [pltpu.VMEM((tm, hb * HEAD), F32)] * 3, name=name,
        compiler_params=_cp("parallel", "arbitrary"))(proj, proj, proj, lb_logits, tril)


def hgrn_scan_bwd(proj, lb_logits, states, do, *, name, tm=128):
    T = proj.shape[0]
    H = proj.shape[1] // (4 * HEAD)
    hb = min(HB, H)
    tm = min(tm, T)
    nt = T // tm
    nblk = tm // BLK
    tril, triu = _block_tri(tm)
    sel = (jnp.arange(BLK * SUB)[None, :] // SUB == jnp.arange(BLK)[:, None]).astype(BF16)
    heads = [slice(hh * HEAD, (hh + 1) * HEAD) for hh in range(hb)]

    def body(qp_ref, fz_ref, v_ref, lg_ref, st_ref, do_ref, tril_ref, triu_ref, sel_ref, d3_ref, dlb_ref,
             dS_ref, Sb_ref, q_s, k_s, b_s, dq_s, dk_s, dv_s, db_s, pk_s, pv_s):
        i = pl.program_id(1)

        @pl.when(i == 0)
        def _():
            dS_ref[...] = jnp.zeros_like(dS_ref)
            dlb_ref[...] = jnp.zeros_like(dlb_ref)

        lb = _lb0(lg_ref)
        qp = qp_ref[...]
        sg = _sig(fz_ref[...])
        f = lb + (1.0 - lb) * sg
        q_s[...] = _silu(qp)
        k_s[...] = 1.0 - f
        b_s[...] = _mm_exact(tril_ref[...], jnp.log(f))
        rows = lax.broadcasted_iota(jnp.int32, (SUB, HEAD), 0)
        rows1 = lax.broadcasted_iota(jnp.int32, (SUB, 1), 0)

        S = [st_ref[hh] for hh in range(hb)]
        for nb in range(nblk):
            blk = slice(nb * BLK, (nb + 1) * BLK)
            last = slice(nb * BLK + BLK - 1, nb * BLK + BLK)
            for hh, c in enumerate(heads):
                Sb_ref[nb * hb + hh] = S[hh]
                if nb < nblk - 1:
                    bc = b_s[last, c]
                    kd = k_s[blk, c] * jnp.exp(bc - b_s[blk, c])
                    S[hh] = S[hh] * jnp.exp(bc) + _dot_tn(v_ref[blk, c].astype(BF16), kd.astype(BF16))

        dS = [dS_ref[hh] for hh in range(hb)]
        for nb in reversed(range(nblk)):
            blk = slice(nb * BLK, (nb + 1) * BLK)
            last = slice(nb * BLK + BLK - 1, nb * BLK + BLK)
            qb, kb, bb, dob, dq, dbc, ebc = [], [], [], [], [], [], []
            for hh, c in enumerate(heads):
                S0 = Sb_ref[nb * hb + hh]
                qb.append(q_s[blk, c])
                kb.append(k_s[blk, c])
                bb.append(b_s[blk, c])
                dob.append(do_ref[blk, c])
                bc = b_s[last, c]
                eb = jnp.exp(bb[hh])
                ekd = jnp.exp(bc - bb[hh])
                ebc.append(jnp.exp(bc))
                dS16 = dS[hh].astype(BF16)
                dob16 = dob[hh].astype(BF16)
                dq.append(_dot(dob16, S0.astype(BF16)) * eb)
                dki = _dot(v_ref[blk, c].astype(BF16), dS16) * ekd
                dk_s[blk, c] = dki
                dv_s[blk, c] = _dot_nt((kb[hh] * ekd).astype(BF16), dS16)
                dbc.append(_colsum(dS[hh] * S0) * ebc[hh] + _colsum(kb[hh] * dki))
                dS[hh] = dS[hh] * ebc[hh] + _dot_tn(dob16, (qb[hh] * eb).astype(BF16))
            qh, bh, doh, dqh = [[_halves(t[hh]) for hh in range(hb)] for t in (qb, bb, dob, dq)]
            for s in range(BLK):
                r = slice(nb * BLK + s, nb * BLK + s + 1)
                for hh, c in enumerate(heads):
                    ks = k_s[r, c]
                    pk, pv = None, None
                    for hf, lo in _live_halves(s):
                        diff = bh[hh][hf] - b_s[r, c]
                        dec = jnp.exp(diff if lo == 0 else jnp.where(rows >= lo, diff, NEG))
                        w = qh[hh][hf] * dec
                        a = jnp.sum(w * ks, axis=-1, keepdims=True)
                        da = jnp.sum(doh[hh][hf] * v_ref[r, c], axis=-1, keepdims=True)
                        if lo:
                            da = jnp.where(rows1 >= lo, da, 0.0)
                        dqh[hh][hf] = dqh[hh][hf] + (da * ks) * dec
                        pk = da * w if pk is None else pk + da * w
                        pv = a * doh[hh][hf] if pv is None else pv + a * doh[hh][hf]
                    pk_s[hh, s * SUB:(s + 1) * SUB, :] = pk
                    pv_s[hh, s * SUB:(s + 1) * SUB, :] = pv
            for hh, c in enumerate(heads):
                khi, klo = _split2(pk_s[hh])
                dk_s[blk, c] += _dot(sel_ref[...], khi) + _dot(sel_ref[...], klo)
                dv_s[blk, c] += _dot(sel_ref[...], pv_s[hh].astype(BF16))
                dq[hh] = jnp.concatenate(dqh[hh], axis=0)
                dq_s[blk, c] = dq[hh]
                db_s[blk, c] = qb[hh] * dq[hh] - kb[hh] * dk_s[blk, c]
                db_s[last, c] += dbc[hh]
        for hh in range(hb):
            dS_ref[hh] = dS[hh]

        dlf = _mm_exact(triu_ref[...], db_s[...])
        df = dlf / f - dk_s[...]
        d3_ref[0] = (dq_s[...] * _dsilu(qp)).astype(BF16)
        d3_ref[1] = (df * (1.0 - lb) * sg * (1.0 - sg)).astype(BF16)
        d3_ref[2] = dv_s[...].astype(BF16)
        dlb_ref[...] += _colsum(df * (1.0 - sg))

    rev = lambda i: nt - 1 - i
    out_blk = pl.BlockSpec((tm, hb * HEAD), lambda h, i: (rev(i), h))
    return pl.pallas_call(
        body, grid=(H // hb, nt),
        in_specs=_hgrn_specs(H, hb, tm, rev) + [pl.BlockSpec((None, hb, HEAD, HEAD), lambda h, i: (rev(i), h, 0, 0)),
                                                out_blk, _const_spec((tm, tm)), _const_spec((tm, tm)),
                                                _const_spec((BLK, BLK * SUB))],
        out_specs=[pl.BlockSpec((3, tm, hb * HEAD), lambda h, i: (0, rev(i), h)),
                   pl.BlockSpec((1, hb * HEAD), lambda h, i: (0, h))],
        out_shape=[jax.ShapeDtypeStruct((3, T, H * HEAD), BF16), jax.ShapeDtypeStruct((1, H * HEAD), F32)],
        scratch_shapes=[pltpu.VMEM((hb, HEAD, HEAD), F32), pltpu.VMEM((nblk * hb, HEAD, HEAD), F32)]
        + [pltpu.VMEM((tm, hb * HEAD), F32)] * 7 + [pltpu.VMEM((hb, BLK * SUB, HEAD), F32)] * 2, name=name,
        compiler_params=_cp("parallel", "arbitrary"))(proj, proj, proj, lb_logits, states, do, tril, triu, sel)


def hgrn_gate(o, proj, gn, *, name, tm=512):
    T, D = o.shape
    H = D // HEAD
    tm = min(tm, T)

    def body(o_ref, gp_ref, gn_ref, og_ref):
        gn_ = gn_ref[...]
        for h in range(H):
            c = slice(h * HEAD, (h + 1) * HEAD)
            oh = o_ref[:, c]
            r = lax.rsqrt(_rowmean(oh * oh) + EPS)
            og_ref[:, c] = ((oh * r) * gn_ * _silu(gp_ref[:, c])).astype(BF16)

    return pl.pallas_call(
        body, grid=(T // tm,),
        in_specs=[_row(tm, D), pl.BlockSpec((tm, D), lambda i: (i, 3)), _full(1, HEAD)],
        out_specs=_row(tm, D), out_shape=jax.ShapeDtypeStruct((T, D), BF16), name=name,
        compiler_params=_cp("parallel"))(o, proj, gn)


def hgrn_gate_bwd(dog, o, proj, gn, *, name, tm=512):
    T, D = o.shape
    H = D // HEAD
    tm = min(tm, T)

    def body(dog_ref, o_ref, gp_ref, gn_ref, do_ref, dgp_ref, dgn_ref):
        _acc_init(pl.program_id(0), dgn_ref)
        gn_ = gn_ref[...]
        for h in range(H):
            c = slice(h * HEAD, (h + 1) * HEAD)
            oh = o_ref[:, c]
            gp = gp_ref[:, c]
            dg = dog_ref[:, c]
            r = lax.rsqrt(_rowmean(oh * oh) + EPS)
            on = oh * r
            dgp_ref[:, c] = (dg * (on * gn_) * _dsilu(gp)).astype(BF16)
            don = dg * _silu(gp)
            dgn_ref[...] += _colsum(don * on)
            dn = don * gn_
            do_ref[:, c] = r * (dn - on * _rowmean(dn * on))

    return pl.pallas_call(
        body, grid=(T // tm,),
        in_specs=[_row(tm, D), _row(tm, D), pl.BlockSpec((tm, D), lambda i: (i, 3)), _full(1, HEAD)],
        out_specs=[_row(tm, D), _row(tm, D), _full(1, HEAD)],
        out_shape=[jax.ShapeDtypeStruct((T, D), F32), jax.ShapeDtypeStruct((T, D), BF16),
                   jax.ShapeDtypeStruct((1, HEAD), F32)], name=name,
        compiler_params=_cp("arbitrary"))(dog, o, proj, gn)


def _split2(x):
    hi = x.astype(BF16)
    return hi, (x - hi.astype(F32)).astype(BF16)


def ada_mod(c_all, ada_w, *, name):
    L, D, N = ada_w.shape
    B = c_all.shape[0]

    def body(c_ref, w_ref, o_ref):
        chi, clo = _split2(_silu(c_ref[...]))
        whi, wlo = _split2(w_ref[...])
        o_ref[...] = _dot(chi, whi) + _dot(chi, wlo) + _dot(clo, whi)

    return pl.pallas_call(
        body, grid=(L,), in_specs=[_full(B, D), pl.BlockSpec((None, D, N), lambda l: (l, 0, 0))],
        out_specs=pl.BlockSpec((None, B, N), lambda l: (l, 0, 0)),
        out_shape=jax.ShapeDtypeStruct((L, B, N), F32), name=name, compiler_params=_cp("parallel"))(c_all, ada_w)


def ada_wgrad(c_all_t, dmod, *, name, tr=256):
    D, B = c_all_t.shape
    L, _, N = dmod.shape
    tr = min(tr, D)

    def body(c_ref, d_ref, o_ref):
        cond = _silu(c_ref[...])
        acc = cond[:, 0:1] * d_ref[0:1, :]
        for b in range(1, B):
            acc = acc + cond[:, b:b + 1] * d_ref[b:b + 1, :]
        o_ref[...] = acc

    return pl.pallas_call(
        body, grid=(L, D // tr),
        in_specs=[pl.BlockSpec((tr, B), lambda l, r: (r, 0)), pl.BlockSpec((None, B, N), lambda l, r: (l, 0, 0))],
        out_specs=pl.BlockSpec((None, tr, N), lambda l, r: (l, r, 0)),
        out_shape=jax.ShapeDtypeStruct((L, D, N), F32), name=name,
        compiler_params=_cp("parallel", "parallel"))(c_all_t, dmod)


def sum_devices(parts, *, name):
    n, R, C = parts.shape

    def body(p_ref, o_ref):
        acc = p_ref[0]
        for d in range(1, n):
            acc = acc + p_ref[d]
        o_ref[...] = acc

    return pl.pallas_call(body, in_specs=[VMEM_SPEC], out_specs=VMEM_SPEC,
                          out_shape=jax.ShapeDtypeStruct((R, C), F32), name=name)(parts)


def lb_logits_grad(lb_logits, dlb, *, name):
    def body(lg_ref, d_ref, o_ref):
        l0, l1, l2 = lg_ref[0:1, :], lg_ref[1:2, :], lg_ref[2:3, :]
        m = jnp.maximum(jnp.maximum(l0, l1), l2)
        e0, e1, e2 = jnp.exp(l0 - m), jnp.exp(l1 - m), jnp.exp(l2 - m)
        z = e0 + e1 + e2
        p0, p1, p2 = e0 / z, e1 / z, e2 / z
        g = d_ref[...] * p0
        o_ref[0:1, :] = g * (1.0 - p0)
        o_ref[1:2, :] = -g * p1
        o_ref[2:3, :] = -g * p2

    return pl.pallas_call(body, in_specs=[VMEM_SPEC, VMEM_SPEC], out_specs=VMEM_SPEC,
                          out_shape=jax.ShapeDtypeStruct(lb_logits.shape, F32), name=name)(lb_logits, dlb)


def adamw(w, g, m, v, *, name, tr=256, after=None):
    R, C = w.shape
    tr = _tile(R, tr)
    deps = [] if after is None else [after]

    def body(w_ref, g_ref, m_ref, v_ref, *rest):
        d_ref, nm_ref, nv_ref = rest[len(deps):]
        gv = g_ref[...]
        nm = ADAM_B1 * m_ref[...] + (1.0 - ADAM_B1) * gv
        nv = ADAM_B2 * v_ref[...] + (1.0 - ADAM_B2) * (gv * gv)
        m_hat = nm / (1.0 - ADAM_B1 ** ADAM_STEP)
        v_hat = nv / (1.0 - ADAM_B2 ** ADAM_STEP)
        d_ref[...] = -ADAM_LR * (m_hat / (jnp.sqrt(v_hat) + ADAM_EPS) + ADAM_WD * w_ref[...])
        nm_ref[...] = nm
        nv_ref[...] = nv

    spec = pl.BlockSpec((tr, C), lambda i: (i, 0))
    return pl.pallas_call(
        body, grid=(R // tr,), in_specs=[spec] * 4 + [pl.BlockSpec(memory_space=pl.ANY)] * len(deps), out_specs=[spec] * 3,
        out_shape=[jax.ShapeDtypeStruct((R, C), F32)] * 3, name=name,
        compiler_params=_cp("parallel"))(w, g, m, v, *deps)


def _place():
    return lax.axis_index("x"), lax.axis_index("y"), lax.axis_index("c")


def _flip(v, bit):
    return 1 - v if bit else v


def allgather_devices(v, *, name):
    R, C = v.shape

    def body(v_ref, out_ref, send_sems, recv_sems, local_sem):
        x, y, c = _place()
        me = 4 * x + 2 * y + c
        mine = pltpu.make_async_copy(v_ref, out_ref.at[me], local_sem)
        mine.start()
        sends = []
        for k in range(1, N_DEV):
            peer = (_flip(x, k & 4), _flip(y, k & 2), _flip(c, k & 1))
            cp = pltpu.make_async_remote_copy(src_ref=v_ref, dst_ref=out_ref.at[me], send_sem=send_sems.at[k - 1],
                                              recv_sem=recv_sems.at[k - 1], device_id=peer, device_id_type=MESH)
            cp.start()
            sends.append(cp)
        for k in range(1, N_DEV):
            px, py, pc = _flip(x, k & 4), _flip(y, k & 2), _flip(c, k & 1)
            pltpu.make_async_remote_copy(src_ref=v_ref, dst_ref=out_ref.at[4 * px + 2 * py + pc],
                                         send_sem=send_sems.at[k - 1], recv_sem=recv_sems.at[k - 1],
                                         device_id=(px, py, pc), device_id_type=MESH).wait_recv()
        for cp in sends:
            cp.wait_send()
        mine.wait()

    return pl.pallas_call(
        body, in_specs=[VMEM_SPEC], out_specs=VMEM_SPEC, out_shape=jax.ShapeDtypeStruct((N_DEV, R, C), v.dtype),
        scratch_shapes=[pltpu.SemaphoreType.DMA((N_DEV - 1,)), pltpu.SemaphoreType.DMA((N_DEV - 1,)),
                        pltpu.SemaphoreType.DMA], name=name)(v)


def _other_chips(x, y):
    return [(1 - x, y), (x, 1 - y), (1 - x, 1 - y)]


SEM = pl.BlockSpec(memory_space=pltpu.SEMAPHORE)
DATAFLOW = pltpu.SideEffectType.DATAFLOW_SIDE_EFFECTING


def _chip_copy(buf, a, j, q, c, chips, send_sems, recv_sems):
    px, py = chips[j]
    return pltpu.make_async_remote_copy(src_ref=buf.at[q, c], dst_ref=buf.at[q, c], send_sem=send_sems.at[3 * a + j],
                                        recv_sem=recv_sems.at[3 * a + j], device_id=(px, py, c), device_id_type=MESH)


def allgather_chips_start(bufs, *, name):
    n = len(bufs)

    def body(*refs):
        send_sems, recv_sems = refs[n], refs[n + 1]
        outs = refs[n + 2:2 * n + 2]
        token = refs[2 * n + 2]
        x, y, c = _place()
        chips = _other_chips(x, y)
        for a in range(n):
            for j in range(3):
                _chip_copy(outs[a], a, j, 2 * x + y, c, chips, send_sems, recv_sems).start()
        token[...] = jnp.zeros_like(token)

    res = pl.pallas_call(
        body, name=name, in_specs=[HBM] * n,
        out_specs=(SEM, SEM, *([HBM] * n), VMEM_SPEC),
        out_shape=(pltpu.SemaphoreType.DMA((3 * n,)), pltpu.SemaphoreType.DMA((3 * n,)),
                   *[pltpu.HBM(b.shape, b.dtype) for b in bufs], jax.ShapeDtypeStruct((SUB, LANE), F32)),
        input_output_aliases={a: a + 2 for a in range(n)},
        compiler_params=pltpu.CompilerParams(has_side_effects=DATAFLOW),
    )(*[pltpu.with_memory_space_constraint(b, pltpu.HBM) for b in bufs])
    return res[0], res[1], list(res[2:2 + n]), res[2 + n]


def allgather_chips_wait(send_sems, recv_sems, bufs, after, *, name):
    n = len(bufs)

    def body(*refs):
        ins = refs[:n]
        send_sems, recv_sems = refs[n], refs[n + 1]
        x, y, c = _place()
        chips = _other_chips(x, y)
        for a in range(n):
            for j, (px, py) in enumerate(chips):
                _chip_copy(ins[a], a, j, 2 * x + y, c, chips, send_sems, recv_sems).wait_send()
                _chip_copy(ins[a], a, j, 2 * px + py, c, chips, send_sems, recv_sems).wait_recv()

    return list(pl.pallas_call(
        body, name=name, in_specs=[HBM] * n + [SEM, SEM, pl.BlockSpec(memory_space=pl.ANY)],
        out_specs=[HBM] * n, out_shape=[pltpu.HBM(b.shape, b.dtype) for b in bufs],
        input_output_aliases={a: a for a in range(n)},
        compiler_params=pltpu.CompilerParams(has_side_effects=DATAFLOW),
    )(*bufs, send_sems, recv_sems, after))


def forward_to_sibling(bufs, *, name):
    n = len(bufs)

    def body(*refs):
        outs = refs[n:2 * n]
        send_sems, recv_sems = refs[2 * n:]
        x, y, c = _place()
        chips = _other_chips(x, y)

        def copy(a, j, half, to):
            px, py = chips[j]
            slab = outs[a].at[2 * px + py, half]
            return pltpu.make_async_remote_copy(src_ref=slab, dst_ref=slab, send_sem=send_sems.at[a, j],
                                                recv_sem=recv_sems.at[a, j], device_id=to, device_id_type=MESH)

        sends = [copy(a, j, c, (x, y, 1 - c)) for a in range(n) for j in range(3)]
        for cp in sends:
            cp.start()
        for a in range(n):
            for j in range(3):
                copy(a, j, 1 - c, (x, y, c)).wait_recv()
        for cp in sends:
            cp.wait_send()

    return pl.pallas_call(
        body, in_specs=[HBM] * n, out_specs=[HBM] * n,
        out_shape=[jax.ShapeDtypeStruct(b.shape, b.dtype) for b in bufs],
        input_output_aliases={a: a for a in range(n)},
        scratch_shapes=[pltpu.SemaphoreType.DMA((n, 3)), pltpu.SemaphoreType.DMA((n, 3))], name=name)(*bufs)


def pair_add(g, other, c_idx, *, name, tr=256):
    _, Q, R, C = g.shape
    tr = _tile(R, tr)

    def body(c_ref, g_ref, o_ref, out_ref):
        out_ref[...] = (g_ref[...] + o_ref[...]).astype(BF16)

    return pl.pallas_call(
        body,
        grid_spec=pltpu.PrefetchScalarGridSpec(
            num_scalar_prefetch=1, grid=(Q, R // tr),
            in_specs=[pl.BlockSpec((None, None, tr, C), lambda q, r, c_ref: (c_ref[0], q, r, 0)),
                      pl.BlockSpec((None, tr, C), lambda q, r, c_ref: (q, r, 0))],
            out_specs=pl.BlockSpec((None, tr, C), lambda q, r, c_ref: (q, r, 0))),
        out_shape=jax.ShapeDtypeStruct((Q, R, C), BF16), name=name,
        compiler_params=_cp("parallel", "parallel"))(c_idx, g, other)


def chip_sum(sums, landed, qc_idx, *, name, tr=256):
    _, R, C = sums.shape
    tr = _tile(R, tr)

    def body(qc_ref, own_ref, l_ref, o_ref):
        acc = own_ref[...].astype(F32)
        for k in range(3):
            acc = acc + l_ref[k].astype(F32)
        o_ref[...] = acc

    return pl.pallas_call(
        body,
        grid_spec=pltpu.PrefetchScalarGridSpec(
            num_scalar_prefetch=1, grid=(R // tr,),
            in_specs=[pl.BlockSpec((None, tr, C), lambda r, qc: (qc[0], r, 0)),
                      pl.BlockSpec((3, tr, C), lambda r, qc: (0, r, 0))],
            out_specs=pl.BlockSpec((None, tr, C), lambda r, qc: (qc[1], r, 0))),
        out_shape=jax.ShapeDtypeStruct((2, R, C), F32), name=name,
        compiler_params=_cp("parallel"))(qc_idx, sums, landed)


def half_swap(bufs, *, name):
    n = len(bufs)

    def body(*refs):
        outs = refs[n:2 * n]
        send_sems, recv_sems = refs[2 * n:]
        x, y, c = _place()
        cps = [pltpu.make_async_remote_copy(src_ref=outs[a].at[c], dst_ref=outs[a].at[c], send_sem=send_sems.at[a],
                                            recv_sem=recv_sems.at[a], device_id=(x, y, 1 - c), device_id_type=MESH)
               for a in range(n)]
        for cp in cps:
            cp.start()
        for a in range(n):
            pltpu.make_async_remote_copy(src_ref=outs[a].at[c], dst_ref=outs[a].at[1 - c], send_sem=send_sems.at[a],
                                         recv_sem=recv_sems.at[a], device_id=(x, y, 1 - c),
                                         device_id_type=MESH).wait_recv()
        for cp in cps:
            cp.wait_send()

    return pl.pallas_call(
        body, in_specs=[HBM] * n, out_specs=[HBM] * n,
        out_shape=[jax.ShapeDtypeStruct(b.shape, b.dtype) for b in bufs],
        input_output_aliases={a: a for a in range(n)},
        scratch_shapes=[pltpu.SemaphoreType.DMA((n,)), pltpu.SemaphoreType.DMA((n,))], name=name)(*bufs)


def _chip_copies(src, dst, send_sems, recv_sems):
    x, y, c = _place()
    return [pltpu.make_async_remote_copy(src_ref=src[a].at[2 * px + py], dst_ref=dst[a].at[j],
                                         send_sem=send_sems.at[3 * a + j], recv_sem=recv_sems.at[3 * a + j],
                                         device_id=(px, py, c), device_id_type=MESH)
            for a in range(len(src)) for j, (px, py) in enumerate(_other_chips(x, y))]


def _pair_copies(src, dst, send_sems, recv_sems):
    x, y, c = _place()
    return [pltpu.make_async_remote_copy(src_ref=src[a].at[1 - c], dst_ref=dst[a], send_sem=send_sems.at[a],
                                         recv_sem=recv_sems.at[a], device_id=(x, y, 1 - c), device_id_type=MESH)
            for a in range(len(src))]


def _device_copies(src, dst, send_sems, recv_sems):
    x, y, c = _place()
    mine = src[0].at[4 * x + 2 * y + c]
    return [pltpu.make_async_remote_copy(src_ref=mine, dst_ref=mine, send_sem=send_sems.at[k - 1],
                                         recv_sem=recv_sems.at[k - 1],
                                         device_id=(_flip(x, k & 4), _flip(y, k & 2), _flip(c, k & 1)), device_id_type=MESH)
            for k in range(1, N_DEV)]


def exchange_start(src, landing, copies, n_sems, *, name):
    n, m = len(src), len(src) + len(landing)

    def body(*refs):
        send_sems, recv_sems = refs[m], refs[m + 1]
        for cp in copies(refs[m + 2:m + 2 + n], refs[m + 2 + n:2 * m + 2], send_sems, recv_sems):
            cp.start()
        token = refs[2 * m + 2]
        token[...] = jnp.zeros_like(token)

    res = pl.pallas_call(
        body, name=name, in_specs=[HBM] * m,
        out_specs=(SEM, SEM, *([HBM] * m), VMEM_SPEC),
        out_shape=(pltpu.SemaphoreType.DMA((n_sems,)), pltpu.SemaphoreType.DMA((n_sems,)),
                   *[pltpu.HBM(b.shape, b.dtype) for b in src + landing], jax.ShapeDtypeStruct((SUB, LANE), F32)),
        input_output_aliases={a: a + 2 for a in range(m)},
        compiler_params=pltpu.CompilerParams(has_side_effects=DATAFLOW),
    )(*[pltpu.with_memory_space_constraint(b, pltpu.HBM) for b in src + landing])
    return res[0], res[1], list(res[2:2 + n]), list(res[2 + n:2 + m]), res[2 + m]


def exchange_wait(send_sems, recv_sems, src, landed, copies, after, *, name):
    n, m = len(src), len(src) + len(landed)

    def body(*refs):
        for cp in copies(refs[:n], refs[n:m], refs[m], refs[m + 1]):
            cp.wait_send()
            cp.wait_recv()

    res = pl.pallas_call(
        body, name=name, in_specs=[HBM] * m + [SEM, SEM, pl.BlockSpec(memory_space=pl.ANY)],
        out_specs=[HBM] * m, out_shape=[pltpu.HBM(b.shape, b.dtype) for b in src + landed],
        input_output_aliases={a: a for a in range(m)},
        compiler_params=pltpu.CompilerParams(has_side_effects=DATAFLOW),
    )(*src, *landed, send_sems, recv_sems, after)
    return list(res[:n]), list(res[n:])


def finish_reduce(sums, landed, q, c, tag):
    qc_idx = jnp.stack([q, c]).astype(jnp.int32)
    return [chip_sum(s, l, qc_idx, name=f"grad_chip_sum_{tag}{a}") for a, (s, l) in enumerate(zip(sums, landed))]


def _ffn_forward(x, h, mod, post_g, w_up, w_down, dw_w, dw_b, tag, next_norm=None, tgt=None):
    _, _, gate = mod
    u0 = mm_nn(h, w_up, name=f"{tag}_up", out_dtype=BF16, perm=_ffn_perm)
    z, ab = ffn_act(u0, dw_w, dw_b, name=f"{tag}_act")
    y = mm_nn(z, w_down, name=f"{tag}_down")
    if tgt is None:
        out = post_residual_prenorm(x, y, post_g, gate, *next_norm, name=f"{tag}_post")
    else:
        out = post_residual_loss(x, y, post_g, gate, tgt, name=f"{tag}_post_loss")
    return out, (x, h, u0, ab, z, y)


def _ffn_backward(dx, entry, saved, mod, pre_g, post_g, w_up, w_down, dw_w, tag, before):
    x, h, u0, ab, z, y = saved
    _, sc, gate = mod
    dy, dgate, dpost = entry
    dz = mm_nt(dy, w_down, name=f"{tag}_down_dx", out_dtype=BF16)
    g_down = mm_tn(z, dy, name=f"{tag}_down_dw", J=2, block="a", row_chips=2)
    du0, dconv = ffn_act_bwd(dz, u0, ab, dw_w, name=f"{tag}_act_bwd")
    dh = mm_nt(du0, w_up, name=f"{tag}_up_dx", perm=_ffn_perm)
    g_up = mm_tn(h, du0, name=f"{tag}_up_dw", J=4, block="b", perm=_ffn_perm)
    dx_in, dsh, dsc, dpre, *prev = prenorm_post_bwd(dh, x, dx, pre_g, sc, *before, name=f"{tag}_prenorm_bwd")
    nb = u0.shape[1] // 4
    dconv = dconv[:, 0].reshape(4, 2, 2, nb).transpose(0, 2, 1, 3).reshape(4, 4 * nb)
    return dx_in, dict(dsh=dsh, dsc=dsc, dgate=dgate, dpre=dpre, dpost=dpost, g_up=g_up, g_down=g_down,
                       d_dw_w=dconv[0:FFN_W], d_dw_b=dconv[3:4]), prev


def _local_step(x, tgt, mods, P, first_weights=None, late_weights=None, grads_ready=None):
    m0, m1 = mods
    h1 = prenorm(x, P["pre_mix_g"][0:1], m0[1], m0[0], name="hgrn_prenorm")
    token = None
    if first_weights is not None:
        first, token = first_weights(h1)
        P = {**P, **first}
    proj = mm_nn(h1, P["hgrn_w_in"], name="hgrn_in", after=token)
    o, states = hgrn_scan(proj, P["hgrn_lb_logits"], name="hgrn_scan")
    og = hgrn_gate(o, proj, P["hgrn_gnorm_g"], name="hgrn_gate")
    y1 = mm_nn(og, P["hgrn_w_out"], name="hgrn_out")
    x1, h_f0 = post_residual_prenorm(x, y1, P["post_mix_g"][0:1], m0[2], P["pre_ffn_g"][0:1], m0[4], m0[3],
                                     name="hgrn_post")
    if late_weights is not None:
        P = {**P, **late_weights(x1)}
    (x2, h3), ffn0 = _ffn_forward(x1, h_f0, m0[3:6], P["post_ffn_g"][0:1], P["ffn_w_up"][0], P["ffn_w_down"][0],
                                  P["ffn_dw_w"][0], P["ffn_dw_b"][0:1], "ffn0",
                                  next_norm=(P["pre_mix_g"][1:2], m1[1], m1[0]))
    u = mm_nn(h3, P["conv_w_in"], name="conv_in", bias=P["conv_b_in"])
    s, cv = conv_act(u, P["conv_dw_w"], P["conv_dw_b"], P["conv_ln_g"], P["conv_ln_b"], name="conv_act")
    y3 = mm_nn(s, P["conv_w_out"], name="conv_out", bias=P["conv_b_out"])
    x3, h_f1 = post_residual_prenorm(x2, y3, P["post_mix_g"][1:2], m1[2], P["pre_ffn_g"][1:2], m1[4], m1[3],
                                     name="conv_post")
    (dx4, lcols, *entry_f1), ffn1 = _ffn_forward(x3, h_f1, m1[3:6], P["post_ffn_g"][1:2], P["ffn_w_up"][1], P["ffn_w_down"][1],
                                      P["ffn_dw_w"][1], P["ffn_dw_b"][1:2], "ffn1", tgt=tgt)
    dx3, f1, (dy3, dg1_1, dpostmix1, d_b_out) = _ffn_backward(
        dx4, entry_f1, ffn1, m1[3:6], P["pre_ffn_g"][1:2], P["post_ffn_g"][1:2], P["ffn_w_up"][1], P["ffn_w_down"][1],
        P["ffn_dw_w"][1], "ffn1", before=(y3, P["post_mix_g"][1:2], m1[2]))
    ds = mm_nt(dy3, P["conv_w_out"], name="conv_out_dx")
    g_conv_out = mm_tn(s, dy3, name="conv_out_dw", J=1, block="a", row_chips=4)
    dcv, d_ln_g, d_ln_b, d_dw_b = conv_norm_bwd(ds, cv, P["conv_ln_g"], P["conv_ln_b"], name="conv_norm_bwd")
    du, d_dw_w, d_b_in = conv_glu_bwd(dcv, u, P["conv_dw_w"], name="conv_glu_bwd")
    dh3 = mm_nt(du, P["conv_w_in"], name="conv_in_dx")
    g_conv_in = mm_tn(h3, du, name="conv_in_dw", J=2, block="b", col_chips=2)
    if grads_ready is not None:
        token = grads_ready("l1", [g_conv_in, g_conv_out, f1["g_up"], f1["g_down"]])
        m0 = tuple(m + token[0:1, 0:1] for m in m0)
    dx2, dsh1_1, dsc1_1, dpremix1, *entry_f0 = prenorm_post_bwd(
        dh3, x2, dx3, P["pre_mix_g"][1:2], m1[1], ffn0[5], P["post_ffn_g"][0:1], m0[5], name="conv_prenorm_bwd")
    dx1, f0, (dy1, dg1_0, dpostmix0, _) = _ffn_backward(
        dx2, entry_f0[:3], ffn0, m0[3:6], P["pre_ffn_g"][0:1], P["post_ffn_g"][0:1], P["ffn_w_up"][0],
        P["ffn_w_down"][0], P["ffn_dw_w"][0], "ffn0", before=(y1, P["post_mix_g"][0:1], m0[2]))
    token = grads_ready("f0", [f0["g_up"], f0["g_down"]]) if grads_ready is not None else None
    dog = mm_nt(dy1, P["hgrn_w_out"], name="hgrn_out_dx", after=token)
    g_hgrn_out = mm_tn(og, dy1, name="hgrn_out_dw", J=1, block="a", row_chips=4)
    do, dgp, d_gn = hgrn_gate_bwd(dog, o, proj, P["hgrn_gnorm_g"], name="hgrn_gate_bwd")
    d3, dlb = hgrn_scan_bwd(proj, P["hgrn_lb_logits"], states, do, name="hgrn_scan_bwd")
    g_hgrn_in = mm_tn_parts(h1, d3, dgp, name="hgrn_in_dw")
    token = grads_ready("hg", [g_hgrn_in, g_hgrn_out]) if grads_ready is not None else None
    dh1 = mm_nt_parts(d3, dgp, P["hgrn_w_in"], name="hgrn_in_dx", after=token)
    dx0, dsh1_0, dsc1_0, dpremix0 = prenorm_bwd(dh1, x, dx1, P["pre_mix_g"][0:1], m0[1], name="hgrn_prenorm_bwd")

    dmod = jnp.stack([
        jnp.concatenate([dsh1_0, dsc1_0, dg1_0, f0["dsh"], f0["dsc"], f0["dgate"]], axis=1)[0],
        jnp.concatenate([dsh1_1, dsc1_1, dg1_1, f1["dsh"], f1["dsc"], f1["dgate"]], axis=1)[0]])
    small = dict(
        loss=lcols,
        pre_mix_g=jnp.concatenate([dpremix0, dpremix1]), post_mix_g=jnp.concatenate([dpostmix0, dpostmix1]),
        pre_ffn_g=jnp.concatenate([f0["dpre"], f1["dpre"]]), post_ffn_g=jnp.concatenate([f0["dpost"], f1["dpost"]]),
        lb=dlb, hgrn_gnorm_g=d_gn, ffn_dw_b=jnp.concatenate([f0["d_dw_b"], f1["d_dw_b"]]), dmod=dmod,
        conv_b_in=d_b_in, conv_dw_w=d_dw_w[0:CONV_W], conv_dw_b=d_dw_b, conv_ln_g=d_ln_g, conv_ln_b=d_ln_b,
        conv_b_out=d_b_out, ffn_dw_w=jnp.stack([f0["d_dw_w"], f1["d_dw_w"]]))
    big = [g_hgrn_in, g_hgrn_out, g_conv_in, g_conv_out, f0["g_up"], f1["g_up"], f0["g_down"], f1["g_down"]]
    return dx0, small, big


def _pack(parts, rows=8):
    flat = jnp.concatenate([p.reshape(-1).astype(F32) for p in parts])
    per = rows * 128
    pad = (-flat.shape[0]) % per
    return jnp.pad(flat, (0, pad)).reshape(rows, -1)


def _unpack(flat, shapes):
    out, off = [], 0
    for s in shapes:
        n = 1
        for d in s:
            n *= d
        out.append(flat[..., off:off + n].reshape(flat.shape[:-1] + tuple(s)))
        off += n
    return out


def _from_chips(stacked, axis):
    moved = jnp.moveaxis(stacked, 0, axis)
    shape = list(moved.shape)
    return moved.reshape(shape[:axis] + [shape[axis] * shape[axis + 1]] + shape[axis + 2:])


def _my_shard(full, axis, q):
    n = full.shape[axis] // N_CHIPS
    return lax.dynamic_slice_in_dim(full, q * n, n, axis=axis)


def kernel(x, c, ada_w, ada_b, pre_mix_g, post_mix_g, pre_ffn_g, post_ffn_g, hgrn_w_in, hgrn_lb_logits, hgrn_gnorm_g, hgrn_w_out, conv_w_in, conv_b_in, conv_dw_w, conv_dw_b, conv_ln_g, conv_ln_b, conv_w_out, conv_b_out, ffn_w_up, ffn_dw_w, ffn_dw_b, ffn_w_down, loss_target, m_ada_w, m_ada_b, m_pre_mix_g, m_post_mix_g, m_pre_ffn_g, m_post_ffn_g, m_hgrn_w_in, m_hgrn_lb_logits, m_hgrn_gnorm_g, m_hgrn_w_out, m_conv_w_in, m_conv_b_in, m_conv_dw_w, m_conv_dw_b, m_conv_ln_g, m_conv_ln_b, m_conv_w_out, m_conv_b_out, m_ffn_w_up, m_ffn_dw_w, m_ffn_dw_b, m_ffn_w_down, v_ada_w, v_ada_b, v_pre_mix_g, v_post_mix_g, v_pre_ffn_g, v_post_ffn_g, v_hgrn_w_in, v_hgrn_lb_logits, v_hgrn_gnorm_g, v_hgrn_w_out, v_conv_w_in, v_conv_b_in, v_conv_dw_w, v_conv_dw_b, v_conv_ln_g, v_conv_ln_b, v_conv_w_out, v_conv_b_out, v_ffn_w_up, v_ffn_dw_w, v_ffn_dw_b, v_ffn_w_down):
    W = dict(ada_w=ada_w, ada_b=ada_b, pre_mix_g=pre_mix_g, post_mix_g=post_mix_g, pre_ffn_g=pre_ffn_g,
             post_ffn_g=post_ffn_g, hgrn_w_in=hgrn_w_in, hgrn_lb_logits=hgrn_lb_logits, hgrn_gnorm_g=hgrn_gnorm_g,
             hgrn_w_out=hgrn_w_out, conv_w_in=conv_w_in, conv_b_in=conv_b_in, conv_dw_w=conv_dw_w,
             conv_dw_b=conv_dw_b, conv_ln_g=conv_ln_g, conv_ln_b=conv_ln_b, conv_w_out=conv_w_out,
             conv_b_out=conv_b_out, ffn_w_up=ffn_w_up, ffn_dw_w=ffn_dw_w, ffn_dw_b=ffn_dw_b, ffn_w_down=ffn_w_down)
    M = dict(ada_w=m_ada_w, ada_b=m_ada_b, pre_mix_g=m_pre_mix_g, post_mix_g=m_post_mix_g, pre_ffn_g=m_pre_ffn_g,
             post_ffn_g=m_post_ffn_g, hgrn_w_in=m_hgrn_w_in, hgrn_lb_logits=m_hgrn_lb_logits,
             hgrn_gnorm_g=m_hgrn_gnorm_g, hgrn_w_out=m_hgrn_w_out, conv_w_in=m_conv_w_in, conv_b_in=m_conv_b_in,
             conv_dw_w=m_conv_dw_w, conv_dw_b=m_conv_dw_b, conv_ln_g=m_conv_ln_g, conv_ln_b=m_conv_ln_b,
             conv_w_out=m_conv_w_out, conv_b_out=m_conv_b_out, ffn_w_up=m_ffn_w_up, ffn_dw_w=m_ffn_dw_w,
             ffn_dw_b=m_ffn_dw_b, ffn_w_down=m_ffn_w_down)
    V = dict(ada_w=v_ada_w, ada_b=v_ada_b, pre_mix_g=v_pre_mix_g, post_mix_g=v_post_mix_g, pre_ffn_g=v_pre_ffn_g,
             post_ffn_g=v_post_ffn_g, hgrn_w_in=v_hgrn_w_in, hgrn_lb_logits=v_hgrn_lb_logits,
             hgrn_gnorm_g=v_hgrn_gnorm_g, hgrn_w_out=v_hgrn_w_out, conv_w_in=v_conv_w_in, conv_b_in=v_conv_b_in,
             conv_dw_w=v_conv_dw_w, conv_dw_b=v_conv_dw_b, conv_ln_g=v_conv_ln_g, conv_ln_b=v_conv_ln_b,
             conv_w_out=v_conv_w_out, conv_b_out=v_conv_b_out, ffn_w_up=v_ffn_w_up, ffn_dw_w=v_ffn_dw_w,
             ffn_dw_b=v_ffn_dw_b, ffn_w_down=v_ffn_w_down)
    names = list(W)
    xi, yi, ci = lax.axis_index("x"), lax.axis_index("y"), lax.axis_index("c")
    q = 2 * xi + yi
    me = 2 * q + ci
    D = x.shape[-1]
    L = ada_w.shape[0]

    small_w = ["conv_b_in", "conv_dw_w", "conv_dw_b", "conv_ln_g", "conv_ln_b", "conv_b_out", "ffn_dw_w"]
    small_axis = dict(conv_b_in=1, conv_dw_w=2, conv_dw_b=1, conv_ln_g=1, conv_ln_b=1, conv_b_out=1, ffn_dw_w=2)
    packed = _pack([c] + [W[n] for n in small_w])

    def halves(w):
        shard = w.astype(BF16).reshape(1, 2, w.shape[0] // 2, w.shape[1])
        buf = lax.empty((N_CHIPS,) + shard.shape[1:], BF16)
        return lax.dynamic_update_slice_in_dim(buf, shard, q, axis=0)

    hg_send, hg_recv, hg_bufs, hg_token = allgather_chips_start([halves(hgrn_w_in[0]), halves(hgrn_w_out[0])],
                                                                name="gather_hgrn_weights_start")
    packed, _ = lax.optimization_barrier((packed, hg_token))
    gathered = allgather_devices(packed, name="gather_small_params").reshape(N_DEV, -1)
    c_all = gathered[:, 0:D]
    per_chip = gathered.reshape(N_CHIPS, 2, -1)[:, 0, D:]
    parts = _unpack(per_chip, [W[n].shape for n in small_w])
    P = {n: _from_chips(p, small_axis[n]) for n, p in zip(small_w, parts)}
    P["conv_dw_w"] = P["conv_dw_w"][0]
    for n in ("pre_mix_g", "post_mix_g", "pre_ffn_g", "post_ffn_g", "hgrn_lb_logits", "hgrn_gnorm_g", "ffn_dw_b"):
        P[n] = W[n]

    modp = ada_mod(c_all, ada_w, name="ada_mod")
    ncol = modp.shape[-1]
    mod_all = allgather_devices(modp.reshape(L * N_DEV, ncol), name="gather_mod")
    mod_all = mod_all.reshape(N_CHIPS, 2, L, N_DEV, ncol)[:, 0]
    mod_me = lax.dynamic_index_in_dim(mod_all, me, axis=2, keepdims=False)
    mod = mod_me.transpose(1, 0, 2).reshape(L, N_CHIPS * ncol) + ada_b
    mods = [tuple(mod[l:l + 1, k * D:(k + 1) * D] for k in range(6)) for l in range(L)]

    stack = lambda t: t.reshape(N_CHIPS, t.shape[1] * t.shape[2], t.shape[3])
    rowsh = lambda t: t.reshape(1, N_CHIPS * t.shape[1] * t.shape[2], t.shape[3])
    pairs = lambda t: t.reshape(2, 2, t.shape[1], t.shape[2]).transpose(0, 2, 1, 3).reshape(2, t.shape[1], 2 * t.shape[2])
    late_shards = [conv_w_in[0], conv_w_out[0], ffn_w_up[0], ffn_w_up[1], ffn_w_down[0], ffn_w_down[1]]
    late = {}

    def first_weights(h1):
        g = forward_to_sibling(allgather_chips_wait(hg_send, hg_recv, hg_bufs, h1, name="gather_hgrn_weights_wait"),
                               name="gather_hgrn_weights_forward")
        late_bufs, _ = lax.optimization_barrier(([halves(w) for w in late_shards], g))
        late["send"], late["recv"], late["bufs"], token = allgather_chips_start(late_bufs, name="gather_weights_start")
        return dict(hgrn_w_in=stack(g[0]), hgrn_w_out=rowsh(g[1])), token

    def late_weights(x1):
        landed = allgather_chips_wait(late["send"], late["recv"], late["bufs"], x1, name="gather_weights_wait")
        g = forward_to_sibling(landed, name="gather_weights_forward")
        return dict(conv_w_in=pairs(stack(g[0])), conv_w_out=rowsh(g[1]), ffn_w_up=[stack(g[2]), stack(g[3])],
                    ffn_w_down=[rowsh(g[4]), rowsh(g[5])])

    c_idx = ci.astype(jnp.int32).reshape(1)
    pending, in_flight = {}, {}

    def chip_stage(after):
        tag, (send, recv, grads, landing) = pending.popitem()
        grads, others = exchange_wait(send, recv, grads, landing, _pair_copies, after, name=f"grad_pair_wait_{tag}")
        sums = [pair_add(g_, o_, c_idx, name=f"grad_pair_add_{tag}_{a}") for a, (g_, o_) in enumerate(zip(grads, others))]
        landing = [lax.empty((3,) + s_.shape[1:], s_.dtype) for s_ in sums]
        send, recv, sums, landing, tok = exchange_start(sums, landing, _chip_copies, 3 * len(sums),
                                                        name=f"grad_chip_exchange_start_{tag}")
        in_flight[tag] = (send, recv, sums, landing)
        return tok

    def grads_ready(tag, grads):
        tok = chip_stage(grads[0]) if pending else 0.0
        landing = [lax.empty(g_.shape[1:], g_.dtype) for g_ in grads]
        send, recv, grads, landing, tok2 = exchange_start(grads, landing, _pair_copies, len(grads),
                                                          name=f"grad_pair_start_{tag}")
        pending[tag] = (send, recv, grads, landing)
        return tok + tok2

    grad_x, small, big = _local_step(x[0], loss_target[0], mods, P, first_weights, late_weights, grads_ready)

    small_names = list(small)
    packed_g = _pack([small[n] for n in small_names])
    gs_buf = lax.dynamic_update_slice_in_dim(lax.empty((N_DEV,) + packed_g.shape, F32), packed_g[None], me, axis=0)
    sg_send, sg_recv, gs_buf, _, tok_sg = exchange_start([gs_buf], [], _device_copies, N_DEV - 1,
                                                         name="gather_small_grads_start")

    tok_hg = chip_stage(tok_sg)
    G = {}
    halves = []
    for tag in ("f0", "l1"):
        sums_t, landed_t = exchange_wait(*in_flight[tag], _chip_copies, grad_x, name=f"grad_chip_exchange_wait_{tag}")
        halves += finish_reduce(sums_t, landed_t, q, ci, f"{tag}_")
    red = [f.reshape(2 * f.shape[1], f.shape[2]) for f in half_swap(halves, name="grad_half_swap")]
    G["conv_w_in"], G["conv_w_out"] = red[2][None], red[3][None]
    G["ffn_w_up"] = jnp.stack([red[0], red[4]])
    G["ffn_w_down"] = jnp.stack([red[1], red[5]])

    delta, new_m, new_v = {}, {}, {}

    def adamw_matrix(n, after=None):
        shp = W[n].shape
        two = lambda t: t.reshape(-1, shp[-1])
        d_, m_, v_ = adamw(two(W[n]), two(G[n]), two(M[n]), two(V[n]), name=f"adamw_{n}", after=after)
        delta[n], new_m[n], new_v[n] = d_.reshape(shp), m_.reshape(shp), v_.reshape(shp)

    big_names = ["ada_w", "hgrn_w_in", "hgrn_w_out", "conv_w_in", "conv_w_out", "ffn_w_up", "ffn_w_down"]
    for n in ("conv_w_in", "conv_w_out", "ffn_w_up", "ffn_w_down"):
        adamw_matrix(n, after=tok_hg)

    (gs,), _ = exchange_wait(sg_send, sg_recv, gs_buf, [], _device_copies, delta["ffn_w_down"],
                             name="gather_small_grads_wait")
    dmod_all = _unpack(gs.reshape(N_DEV, -1), [small[n].shape for n in small_names])[small_names.index("dmod")]
    tot = sum_devices(gs, name="sum_small_grads").reshape(1, -1)
    S = dict(zip(small_names, _unpack(tot, [small[n].shape for n in small_names])))
    S = {n: v[0] for n, v in S.items()}
    loss = 0.5 * jnp.sum(S["loss"]) / D

    dmod_q = lax.dynamic_slice_in_dim(dmod_all, q * ncol, ncol, axis=2)
    G["ada_w"] = ada_wgrad(c_all.T, dmod_q.transpose(1, 0, 2), name="ada_wgrad")
    G["ada_b"] = S["dmod"]
    for n in ("pre_mix_g", "post_mix_g", "pre_ffn_g", "post_ffn_g", "hgrn_gnorm_g", "ffn_dw_b"):
        G[n] = S[n]
    G["hgrn_lb_logits"] = lb_logits_grad(hgrn_lb_logits, S["lb"], name="lb_logits_grad")
    G["conv_b_in"] = _my_shard(S["conv_b_in"], 1, q)
    G["conv_dw_w"] = _my_shard(S["conv_dw_w"], 1, q)[None]
    for n in ("conv_dw_b", "conv_ln_g", "conv_ln_b", "conv_b_out"):
        G[n] = _my_shard(S[n], 1, q)
    G["ffn_dw_w"] = _my_shard(S["ffn_dw_w"], 2, q)
    adamw_matrix("ada_w")

    sums_h, landed_h = exchange_wait(*in_flight["hg"], _chip_copies, delta["ada_w"], name="grad_chip_exchange_wait_hg")
    red_h = half_swap(finish_reduce(sums_h, landed_h, q, ci, "hg_"), name="grad_half_swap_hg")
    G["hgrn_w_in"], G["hgrn_w_out"] = [f.reshape(1, 2 * f.shape[1], f.shape[2]) for f in red_h]
    for n in ("hgrn_w_in", "hgrn_w_out"):
        adamw_matrix(n)
    rest = [n for n in names if n not in big_names]
    d_, m_, v_ = adamw(_pack([W[n] for n in rest]), _pack([G[n] for n in rest]), _pack([M[n] for n in rest]),
                       _pack([V[n] for n in rest]), name="adamw_small")
    shapes = [W[n].shape for n in rest]
    for n, a, b_, c_ in zip(rest, _unpack(d_.reshape(-1), shapes), _unpack(m_.reshape(-1), shapes),
                            _unpack(v_.reshape(-1), shapes)):
        delta[n], new_m[n], new_v[n] = a, b_, c_

    return (loss, grad_x[None], *[G[n].reshape(W[n].shape) for n in names], *[delta[n] for n in names],
            *[new_m[n] for n in names], *[new_v[n] for n in names])
```

```python
import jax
import jax.numpy as jnp
from jax import lax
from jax.experimental import pallas as pl
from jax.experimental.pallas import tpu as pltpu

F32 = jnp.float32
BF16 = jnp.bfloat16
EPS = 1e-6
HEAD = 128
BLK = 16
NEG = -1e30
CONV_W = 31
FFN_W = 3
N_CHIPS = 4
N_DEV = 8
SUB = 8
LANE = 128
V7X_VMEM_LIMIT = 56 * 1024 * 1024
MESH = pl.DeviceIdType.MESH
HBM = pl.BlockSpec(memory_space=pltpu.HBM)
VMEM_SPEC = pl.BlockSpec(memory_space=pltpu.VMEM)

ADAM_LR = 0.001
ADAM_B1 = 0.9
ADAM_B2 = 0.999
ADAM_EPS = 1e-08
ADAM_WD = 0.01
ADAM_STEP = 10


def _cp(*sem):
    return pltpu.CompilerParams(dimension_semantics=sem, vmem_limit_bytes=V7X_VMEM_LIMIT)


def _sig(x):
    return 0.5 * jnp.tanh(0.5 * x) + 0.5


def _silu(x):
    return x * _sig(x)


def _dsilu(x):
    s = _sig(x)
    return s * (1.0 + x * (1.0 - s))


def _dot(a, b):
    return jnp.dot(a, b, preferred_element_type=F32)


def _dot_nt(a, b):
    return lax.dot_general(a, b, (((1,), (1,)), ((), ())), preferred_element_type=F32)


def _dot_tn(a, b):
    return lax.dot_general(a, b, (((0,), (0,)), ((), ())), preferred_element_type=F32)


def _colsum(x):
    return jnp.sum(x, axis=0, keepdims=True)


def _rowmean(x):
    return jnp.mean(x, axis=-1, keepdims=True)


def _ffn_perm(j):
    return (j % 2) * 2 + j // 2


def _tile(n, pref):
    if n <= pref:
        return n
    t = pref - pref % 8
    while n % t:
        t -= 8
    return t


def mm_nn(a, w, *, name, bias=None, out_dtype=F32, perm=None, tm=1024, after=None):
    T, K = a.shape
    J, _, nb = w.shape
    tm = min(tm, T)
    col = (lambda j: j) if perm is None else perm

    def body(a_ref, w_ref, *rest):
        acc = _dot(a_ref[...], w_ref[...])
        if bias is not None:
            acc = acc + rest[0][...]
        rest[-1][...] = acc.astype(out_dtype)

    in_specs = [pl.BlockSpec((tm, K), lambda j, i: (i, 0)), pl.BlockSpec((None, K, nb), lambda j, i: (j, 0, 0))]
    args = [a, w]
    if bias is not None:
        in_specs.append(pl.BlockSpec((1, nb), lambda j, i: (0, j)))
        args.append(bias)
    if after is not None:
        in_specs.append(pl.BlockSpec(memory_space=pl.ANY))
        args.append(after)
    return pl.pallas_call(
        body, grid=(J, T // tm), in_specs=in_specs,
        out_specs=pl.BlockSpec((tm, nb), lambda j, i: (i, col(j))),
        out_shape=jax.ShapeDtypeStruct((T, J * nb), out_dtype), name=name,
        compiler_params=_cp("parallel", "parallel"))(*args)


def mm_nt(a, w, *, name, out_dtype=F32, perm=None, tm=1024, after=None):
    T = a.shape[0]
    J, K, nb = w.shape
    tm = min(tm, T)
    col = (lambda j: j) if perm is None else perm
    deps = [] if after is None else [after]

    def body(a_ref, w_ref, *rest):
        o_ref, acc_ref = rest[len(deps):]
        j = pl.program_id(1)

        @pl.when(j == 0)
        def _():
            acc_ref[...] = jnp.zeros_like(acc_ref)

        acc_ref[...] += _dot_nt(a_ref[...], w_ref[...])

        @pl.when(j == J - 1)
        def _():
            o_ref[...] = acc_ref[...].astype(out_dtype)

    return pl.pallas_call(
        body, grid=(T // tm, J),
        in_specs=[pl.BlockSpec((tm, nb), lambda i, j: (i, col(j))), pl.BlockSpec((None, K, nb), lambda i, j: (j, 0, 0))]
        + [pl.BlockSpec(memory_space=pl.ANY)] * len(deps),
        out_specs=pl.BlockSpec((tm, K), lambda i, j: (i, 0)),
        out_shape=jax.ShapeDtypeStruct((T, K), out_dtype),
        scratch_shapes=[pltpu.VMEM((tm, K), F32)], name=name,
        compiler_params=_cp("parallel", "arbitrary"))(a, w, *deps)


def mm_tn(a, b, *, name, J, block, row_chips=1, col_chips=1, perm=None, tk=1024):
    T = a.shape[0]
    tk = min(tk, T)
    col = (lambda j: j) if perm is None else perm
    if block == "b":
        rows, nb = a.shape[1], b.shape[1] // J
        a_spec = pl.BlockSpec((tk, rows), lambda j, t: (t, 0))
        b_spec = pl.BlockSpec((tk, nb), lambda j, t: (t, col(j)))
    else:
        rows, nb = a.shape[1] // J, b.shape[1]
        a_spec = pl.BlockSpec((tk, rows), lambda j, t: (t, col(j)))
        b_spec = pl.BlockSpec((tk, nb), lambda j, t: (t, 0))
    rh = rows // (2 * row_chips)
    nc = nb // col_chips
    chips = [(rc, cc) for rc in range(row_chips) for cc in range(col_chips)]

    def body(a_ref, b_ref, o_ref):
        @pl.when(pl.program_id(1) == 0)
        def _():
            o_ref[...] = jnp.zeros_like(o_ref)

        acc = _dot_tn(a_ref[...], b_ref[...])
        for ch, (rc, cc) in enumerate(chips):
            for hf in range(2):
                r0 = (rc * 2 + hf) * rh
                o_ref[hf, ch] += acc[r0:r0 + rh, cc * nc:(cc + 1) * nc]

    return pl.pallas_call(
        body, grid=(J, T // tk), in_specs=[a_spec, b_spec],
        out_specs=pl.BlockSpec((2, len(chips), rh, nc), lambda j, t: (0, j, 0, 0)),
        out_shape=jax.ShapeDtypeStruct((2, J * len(chips), rh, nc), F32), name=name,
        compiler_params=_cp("parallel", "arbitrary"))(a, b)


def mm_nt_parts(s3, g, w, *, name, tm=1024, after=None):
    n3, T, nb = s3.shape
    J, K, _ = w.shape
    tm = min(tm, T)
    deps = [] if after is None else [after]

    def body(s_ref, g_ref, w_ref, *rest):
        o_ref, acc_ref = rest[len(deps):]
        j = pl.program_id(1)

        @pl.when(j == 0)
        def _():
            acc_ref[...] = jnp.zeros_like(acc_ref)

        @pl.when(j < n3)
        def _():
            acc_ref[...] += _dot_nt(s_ref[...], w_ref[...])

        @pl.when(j == n3)
        def _():
            acc_ref[...] += _dot_nt(g_ref[...], w_ref[...])

        @pl.when(j == J - 1)
        def _():
            o_ref[...] = acc_ref[...]

    return pl.pallas_call(
        body, grid=(T // tm, J),
        in_specs=[pl.BlockSpec((None, tm, nb), lambda i, j: (jnp.minimum(j, n3 - 1), i, 0)),
                  pl.BlockSpec((tm, nb), lambda i, j: (i, 0)), pl.BlockSpec((None, K, nb), lambda i, j: (j, 0, 0))]
        + [pl.BlockSpec(memory_space=pl.ANY)] * len(deps),
        out_specs=pl.BlockSpec((tm, K), lambda i, j: (i, 0)), out_shape=jax.ShapeDtypeStruct((T, K), F32),
        scratch_shapes=[pltpu.VMEM((tm, K), F32)], name=name,
        compiler_params=_cp("parallel", "arbitrary"))(s3, g, w, *deps)


def mm_tn_parts(a, s3, g, *, name, tk=1024):
    n3, T, nb = s3.shape
    J = n3 + 1
    tk = min(tk, T)
    rows = a.shape[1]
    rh = rows // 2

    def body(a_ref, s_ref, g_ref, o_ref):
        j = pl.program_id(0)

        @pl.when(pl.program_id(1) == 0)
        def _():
            o_ref[...] = jnp.zeros_like(o_ref)

        def add(b_ref):
            acc = _dot_tn(a_ref[...], b_ref[...])
            for hf in range(2):
                o_ref[hf, 0] += acc[hf * rh:(hf + 1) * rh, :]

        pl.when(j < n3)(lambda: add(s_ref))
        pl.when(j == n3)(lambda: add(g_ref))

    return pl.pallas_call(
        body, grid=(J, T // tk),
        in_specs=[pl.BlockSpec((tk, rows), lambda j, t: (t, 0)),
                  pl.BlockSpec((None, tk, nb), lambda j, t: (jnp.minimum(j, n3 - 1), t, 0)),
                  pl.BlockSpec((tk, nb), lambda j, t: (t, 0))],
        out_specs=pl.BlockSpec((2, 1, rh, nb), lambda j, t: (0, j, 0, 0)),
        out_shape=jax.ShapeDtypeStruct((2, J, rh, nb), F32), name=name,
        compiler_params=_cp("parallel", "arbitrary"))(a, s3, g)


def _row(tm, w):
    return pl.BlockSpec((tm, w), lambda i: (i, 0))


def _full(r, w):
    return pl.BlockSpec((r, w), lambda i: (0, 0))


def _acc_init(i, *refs):
    @pl.when(i == 0)
    def _():
        for r in refs:
            r[...] = jnp.zeros_like(r)


def prenorm(x, g, sc, sh, *, name, tm=512):
    T, D = x.shape
    tm = min(tm, T)

    def body(x_ref, g_ref, sc_ref, sh_ref, h_ref):
        xv = x_ref[...]
        r = lax.rsqrt(_rowmean(xv * xv) + EPS)
        h_ref[...] = ((xv * r) * g_ref[...] * (1.0 + sc_ref[...]) + sh_ref[...]).astype(BF16)

    return pl.pallas_call(
        body, grid=(T // tm,), in_specs=[_row(tm, D), _full(1, D), _full(1, D), _full(1, D)],
        out_specs=_row(tm, D), out_shape=jax.ShapeDtypeStruct((T, D), BF16), name=name,
        compiler_params=_cp("parallel"))(x, g, sc, sh)


def post_residual_prenorm(x, y, g, gate, g2, sc2, sh2, *, name, tm=512):
    T, D = x.shape
    tm = min(tm, T)

    def body(x_ref, y_ref, g_ref, gate_ref, g2_ref, sc2_ref, sh2_ref, o_ref, h_ref):
        yv = y_ref[...]
        r = lax.rsqrt(_rowmean(yv * yv) + EPS)
        out = x_ref[...] + gate_ref[...] * ((yv * r) * g_ref[...])
        o_ref[...] = out
        r2 = lax.rsqrt(_rowmean(out * out) + EPS)
        h_ref[...] = ((out * r2) * g2_ref[...] * (1.0 + sc2_ref[...]) + sh2_ref[...]).astype(BF16)

    return pl.pallas_call(
        body, grid=(T // tm,), in_specs=[_row(tm, D), _row(tm, D)] + [_full(1, D)] * 5,
        out_specs=[_row(tm, D), _row(tm, D)],
        out_shape=[jax.ShapeDtypeStruct((T, D), F32), jax.ShapeDtypeStruct((T, D), BF16)], name=name,
        compiler_params=_cp("parallel"))(x, y, g, gate, g2, sc2, sh2)


def post_residual_loss(x, y, g, gate, tgt, *, name, tm=512):
    T, D = x.shape
    tm = min(tm, T)

    def body(x_ref, y_ref, g_ref, gate_ref, t_ref, dx_ref, l_ref, dy_ref, dgate_ref, dg_ref):
        _acc_init(pl.program_id(0), l_ref, dgate_ref, dg_ref)
        yv = y_ref[...]
        r = lax.rsqrt(_rowmean(yv * yv) + EPS)
        yn = yv * r
        gv = g_ref[...]
        gt = gate_ref[...]
        e = x_ref[...] + gt * (yn * gv) - t_ref[...]
        dxv = e * (1.0 / D)
        dx_ref[...] = dxv
        l_ref[...] += _colsum(e * e)
        dgate_ref[...] += _colsum(dxv * (yn * gv))
        dg_ref[...] += _colsum(dxv * gt * yn)
        dyn = dxv * gt * gv
        dy_ref[...] = (r * (dyn - yn * _rowmean(dyn * yn))).astype(BF16)

    vec = jax.ShapeDtypeStruct((1, D), F32)
    return pl.pallas_call(
        body, grid=(T // tm,), in_specs=[_row(tm, D), _row(tm, D), _full(1, D), _full(1, D), _row(tm, D)],
        out_specs=[_row(tm, D), _full(1, D), _row(tm, D), _full(1, D), _full(1, D)],
        out_shape=[jax.ShapeDtypeStruct((T, D), F32), vec, jax.ShapeDtypeStruct((T, D), BF16), vec, vec], name=name,
        compiler_params=_cp("arbitrary"))(x, y, g, gate, tgt)


def prenorm_bwd(dh, x, dres, g, sc, *, name, tm=512):
    T, D = x.shape
    tm = min(tm, T)

    def body(dh_ref, x_ref, dres_ref, g_ref, sc_ref, dx_ref, dsh_ref, dsc_ref, dg_ref):
        _acc_init(pl.program_id(0), dsh_ref, dsc_ref, dg_ref)
        xv = x_ref[...]
        dhv = dh_ref[...]
        r = lax.rsqrt(_rowmean(xv * xv) + EPS)
        xn = xv * r
        gv = g_ref[...]
        one_sc = 1.0 + sc_ref[...]
        dsh_ref[...] += _colsum(dhv)
        dsc_ref[...] += _colsum(dhv * (xn * gv))
        dg_ref[...] += _colsum(dhv * one_sc * xn)
        dxn = dhv * one_sc * gv
        dx_ref[...] = dres_ref[...] + r * (dxn - xn * _rowmean(dxn * xn))

    return pl.pallas_call(
        body, grid=(T // tm,), in_specs=[_row(tm, D), _row(tm, D), _row(tm, D), _full(1, D), _full(1, D)],
        out_specs=[_row(tm, D), _full(1, D), _full(1, D), _full(1, D)],
        out_shape=[jax.ShapeDtypeStruct((T, D), F32)] + [jax.ShapeDtypeStruct((1, D), F32)] * 3, name=name,
        compiler_params=_cp("arbitrary"))(dh, x, dres, g, sc)


def prenorm_post_bwd(dh, x, dres, g, sc, y, g_post, gate, *, name, tm=512):
    T, D = x.shape
    tm = min(tm, T)

    def body(dh_ref, x_ref, dres_ref, g_ref, sc_ref, y_ref, gp_ref, gate_ref,
             dx_ref, dsh_ref, dsc_ref, dg_ref, dy_ref, dgate_ref, dgp_ref, dbias_ref):
        _acc_init(pl.program_id(0), dsh_ref, dsc_ref, dg_ref, dgate_ref, dgp_ref, dbias_ref)
        xv = x_ref[...]
        dhv = dh_ref[...]
        r = lax.rsqrt(_rowmean(xv * xv) + EPS)
        xn = xv * r
        gv = g_ref[...]
        one_sc = 1.0 + sc_ref[...]
        dsh_ref[...] += _colsum(dhv)
        dsc_ref[...] += _colsum(dhv * (xn * gv))
        dg_ref[...] += _colsum(dhv * one_sc * xn)
        dxn = dhv * one_sc * gv
        dxv = dres_ref[...] + r * (dxn - xn * _rowmean(dxn * xn))
        dx_ref[...] = dxv
        yv = y_ref[...]
        ry = lax.rsqrt(_rowmean(yv * yv) + EPS)
        yn = yv * ry
        gp = gp_ref[...]
        gt = gate_ref[...]
        dgate_ref[...] += _colsum(dxv * (yn * gp))
        dgp_ref[...] += _colsum(dxv * gt * yn)
        dyn = dxv * gt * gp
        dy = ry * (dyn - yn * _rowmean(dyn * yn))
        dbias_ref[...] += _colsum(dy)
        dy_ref[...] = dy.astype(BF16)

    vec = jax.ShapeDtypeStruct((1, D), F32)
    return pl.pallas_call(
        body, grid=(T // tm,),
        in_specs=[_row(tm, D)] * 3 + [_full(1, D)] * 2 + [_row(tm, D)] + [_full(1, D)] * 2,
        out_specs=[_row(tm, D)] + [_full(1, D)] * 3 + [_row(tm, D)] + [_full(1, D)] * 3,
        out_shape=[jax.ShapeDtypeStruct((T, D), F32), vec, vec, vec, jax.ShapeDtypeStruct((T, D), BF16), vec, vec, vec],
        name=name, compiler_params=_cp("arbitrary"))(dh, x, dres, g, sc, y, g_post, gate)


HALO = 16


def _shift_helpers():
    rid = lax.broadcasted_iota(jnp.int32, (SUB, LANE), 0)

    def down(cur, prev, k):
        return pltpu.roll(jnp.where(rid >= SUB - k, prev, cur), k, 0)

    def up(cur, nxt, k):
        return pltpu.roll(jnp.where(rid < k, nxt, cur), SUB - k, 0)

    return down, up


def _ffn_sides(c, nb, wa_ref, wb_ref, ba_ref, bb_ref):
    cols = slice(c * LANE, (c + 1) * LANE)
    return [(cols, [wa_ref[k:k + 1, cols] for k in range(FFN_W)], ba_ref[:, cols]),
            (slice(nb + c * LANE, nb + (c + 1) * LANE), [wb_ref[k:k + 1, cols] for k in range(FFN_W)],
             bb_ref[:, cols])]


def _ffn_specs(tm, nb, hb, idx):
    return [pl.BlockSpec((tm, 2 * nb), lambda jc, i: (idx(i), jc)),
            pl.BlockSpec((HALO, 2 * nb), lambda jc, i: (jnp.maximum(idx(i) * hb - 1, 0), jc)),
            pl.BlockSpec((FFN_W, nb), lambda jc, i: (0, jc)),
            pl.BlockSpec((FFN_W, nb), lambda jc, i: (0, jc + 2)),
            pl.BlockSpec((1, nb), lambda jc, i: (0, jc)),
            pl.BlockSpec((1, nb), lambda jc, i: (0, jc + 2))]


def ffn_act(u0p, dw_w, dw_b, *, name, tm=256):
    T, W = u0p.shape
    nb = W // 4
    tm = min(tm, T)
    unroll = 4
    rows16 = 2 * SUB

    def body(u_ref, halo_ref, wa_ref, wb_ref, ba_ref, bb_ref, z_ref, ab_ref):
        i = pl.program_id(1)
        down, _ = _shift_helpers()
        for c in range(nb // LANE):
            cols = slice(c * LANE, (c + 1) * LANE)
            side = _ffn_sides(c, nb, wa_ref, wb_ref, ba_ref, bb_ref)

            def rows(j, prev):
                prev = list(prev)
                for m in range(unroll):
                    r0 = pl.multiple_of((j * unroll + m) * rows16, rows16)
                    x = [u_ref[pl.ds(r0, rows16), cs].astype(F32) for cs, _, _ in side]
                    conv = [[None, None], [None, None]]
                    for hf in range(2):
                        for n, (_, w, b) in enumerate(side):
                            cur = x[n][hf * SUB:(hf + 1) * SUB, :]
                            conv[n][hf] = b + w[2] * cur + w[1] * down(cur, prev[n], 1) + w[0] * down(cur, prev[n], 2)
                            prev[n] = cur
                    a, b = [jnp.concatenate(conv[n], axis=0) for n in range(2)]
                    z_ref[pl.ds(r0, rows16), cols] = (_silu(a) * b).astype(BF16)
                    ab_ref[pl.ds(r0, rows16), side[0][0]] = a.astype(BF16)
                    ab_ref[pl.ds(r0, rows16), side[1][0]] = b.astype(BF16)
                return tuple(prev)

            first = [jnp.where(i == 0, 0.0, halo_ref[:, cs].astype(F32)[SUB:2 * SUB, :]) for cs, _, _ in side]
            lax.fori_loop(0, tm // (rows16 * unroll), rows, tuple(first))

    return pl.pallas_call(
        body, grid=(2, T // tm), in_specs=_ffn_specs(tm, nb, tm // HALO, lambda i: i),
        out_specs=[pl.BlockSpec((tm, nb), lambda jc, i: (i, jc)), pl.BlockSpec((tm, 2 * nb), lambda jc, i: (i, jc))],
        out_shape=[jax.ShapeDtypeStruct((T, 2 * nb), BF16), jax.ShapeDtypeStruct((T, W), BF16)], name=name,
        compiler_params=_cp("parallel", "arbitrary"))(u0p, u0p, dw_w, dw_w, dw_b, dw_b)


def ffn_act_bwd(dz, u0p, ab, dw_w, *, name, tm=256):
    T, W = u0p.shape
    nb = W // 4
    tm = min(tm, T)
    nt = T // tm
    unroll = 4
    rows16 = 2 * SUB
    n_it = tm // (rows16 * unroll)

    def body(dz_ref, u_ref, ab_ref, wa_ref, wb_ref, du0_ref, dw_ref, carry):
        i = pl.program_id(1)
        _acc_init(i, dw_ref)
        _, up = _shift_helpers()
        for c in range(nb // LANE):
            cols = slice(c * LANE, (c + 1) * LANE)
            side = [(cols, [wa_ref[k:k + 1, cols] for k in range(FFN_W)]),
                    (slice(nb + c * LANE, nb + (c + 1) * LANE), [wb_ref[k:k + 1, cols] for k in range(FFN_W)])]

            def rows(j, st):
                nxt, acc = list(st[0:2]), list(st[2:10])
                for m in range(unroll):
                    r0 = pl.multiple_of(((n_it - 1 - j) * unroll + unroll - 1 - m) * rows16, rows16)
                    dzv = dz_ref[pl.ds(r0, rows16), cols].astype(F32)
                    a, b = [ab_ref[pl.ds(r0, rows16), cs].astype(F32) for cs, _ in side]
                    x = [u_ref[pl.ds(r0, rows16), cs].astype(F32) for cs, _ in side]
                    sa = _sig(a)
                    d16 = [dzv * b * (sa * (1.0 + a * (1.0 - sa))), dzv * (a * sa)]
                    out = [[None, None], [None, None]]
                    for hf in (1, 0):
                        half = slice(hf * SUB, (hf + 1) * SUB)
                        for n in range(2):
                            w = side[n][1]
                            d = d16[n][half, :]
                            u = x[n][half, :]
                            up1, up2 = up(d, nxt[n], 1), up(d, nxt[n], 2)
                            acc[4 * n + 0] = acc[4 * n + 0] + up2 * u
                            acc[4 * n + 1] = acc[4 * n + 1] + up1 * u
                            acc[4 * n + 2] = acc[4 * n + 2] + d * u
                            acc[4 * n + 3] = acc[4 * n + 3] + d
                            out[n][hf] = w[2] * d + w[1] * up1 + w[0] * up2
                            nxt[n] = d
                    for n in range(2):
                        du0_ref[pl.ds(r0, rows16), side[n][0]] = jnp.concatenate(out[n], axis=0).astype(BF16)
                return (*nxt, *acc)

            init = [jnp.where(i == 0, 0.0, carry[:, cs]) for cs, _ in side] + [jnp.zeros((SUB, LANE), F32)] * 8
            st = lax.fori_loop(0, n_it, rows, tuple(init))
            for n in range(2):
                carry[:, side[n][0]] = st[n]
                for k in range(4):
                    dw_ref[k, :, side[n][0]] += st[2 + 4 * n + k]

        @pl.when(i == nt - 1)
        def _():
            for k in range(4):
                dw_ref[k, 0:1, :] = _colsum(dw_ref[k])

    rev = lambda i: nt - 1 - i
    wide = pl.BlockSpec((tm, 2 * nb), lambda jc, i: (rev(i), jc))
    return pl.pallas_call(
        body, grid=(2, nt),
        in_specs=[pl.BlockSpec((tm, nb), lambda jc, i: (rev(i), jc)), wide, wide,
                  pl.BlockSpec((FFN_W, nb), lambda jc, i: (0, jc)), pl.BlockSpec((FFN_W, nb), lambda jc, i: (0, jc + 2))],
        out_specs=[wide, pl.BlockSpec((4, SUB, 2 * nb), lambda jc, i: (0, 0, jc))],
        out_shape=[jax.ShapeDtypeStruct((T, W), BF16), jax.ShapeDtypeStruct((4, SUB, W), F32)],
        scratch_shapes=[pltpu.VMEM((SUB, 2 * nb), F32)], name=name,
        compiler_params=_cp("parallel", "arbitrary"))(dz, u0p, ab, dw_w, dw_w)


CHALO = 32
CCOL = 256


def _phase_copies(buf, shifted, tm):
    n = tm + CHALO - SUB
    for p in range(1, SUB):
        shifted[p - 1, 0:n, :] = buf[p:p + n, :]


def _shifted(buf, shifted, r, tm, c0):
    m, p = divmod(r, SUB)
    src = buf if p == 0 else shifted.at[p - 1]
    return src[m * SUB:m * SUB + tm, c0:c0 + CCOL]


def conv_act(u, dw_w, dw_b, ln_g, ln_b, *, name, tm=128):
    T, D2 = u.shape
    D = D2 // 2
    tm = min(tm, T)
    hb = tm // CHALO

    def body(u_ref, halo_ref, w_ref, b_ref, g_ref, be_ref, s_ref, cv_ref, gbuf, gsh):
        i = pl.program_id(0)
        hv = halo_ref[...]
        gbuf[0:CHALO, :] = jnp.where(i == 0, 0.0, hv[:, 0:D] * _sig(hv[:, D:D2]))
        uv = u_ref[...]
        gbuf[CHALO:CHALO + tm, :] = uv[:, 0:D] * _sig(uv[:, D:D2])
        _phase_copies(gbuf, gsh, tm)
        for c0 in range(0, D, CCOL):
            acc = jnp.zeros((tm, CCOL), F32) + b_ref[:, c0:c0 + CCOL]
            for k in range(CONV_W):
                acc = acc + w_ref[k:k + 1, c0:c0 + CCOL] * _shifted(gbuf, gsh, CHALO - (CONV_W - 1) + k, tm, c0)
            cv_ref[:, c0:c0 + CCOL] = acc
        cv = cv_ref[...]
        mu = _rowmean(cv)
        xc = cv - mu
        nh = xc * lax.rsqrt(_rowmean(xc * xc) + EPS)
        s_ref[...] = _silu(nh * g_ref[...] + be_ref[...]).astype(BF16)

    return pl.pallas_call(
        body, grid=(T // tm,),
        in_specs=[_row(tm, D2), pl.BlockSpec((CHALO, D2), lambda i: (jnp.maximum(i * hb - 1, 0), 0)),
                  _full(CONV_W, D), _full(1, D), _full(1, D), _full(1, D)],
        out_specs=[_row(tm, D), _row(tm, D)],
        out_shape=[jax.ShapeDtypeStruct((T, D), BF16), jax.ShapeDtypeStruct((T, D), F32)],
        scratch_shapes=[pltpu.VMEM((tm + CHALO, D), F32), pltpu.VMEM((SUB - 1, tm + CHALO, D), F32)], name=name,
        compiler_params=_cp("arbitrary"))(u, u, dw_w, dw_b, ln_g, ln_b)


def conv_norm_bwd(ds, cv, ln_g, ln_b, *, name, tm=512):
    T, D = cv.shape
    tm = min(tm, T)

    def body(ds_ref, cv_ref, g_ref, be_ref, dcv_ref, dg_ref, dbe_ref, dcb_ref):
        _acc_init(pl.program_id(0), dg_ref, dbe_ref, dcb_ref)
        cv_ = cv_ref[...]
        mu = _rowmean(cv_)
        xc = cv_ - mu
        rstd = lax.rsqrt(_rowmean(xc * xc) + EPS)
        nh = xc * rstd
        gv = g_ref[...]
        dln = ds_ref[...] * _dsilu(nh * gv + be_ref[...])
        dg_ref[...] += _colsum(dln * nh)
        dbe_ref[...] += _colsum(dln)
        dnh = dln * gv
        dcv = rstd * (dnh - _rowmean(dnh) - nh * _rowmean(dnh * nh))
        dcb_ref[...] += _colsum(dcv)
        dcv_ref[...] = dcv

    return pl.pallas_call(
        body, grid=(T // tm,), in_specs=[_row(tm, D), _row(tm, D), _full(1, D), _full(1, D)],
        out_specs=[_row(tm, D), _full(1, D), _full(1, D), _full(1, D)],
        out_shape=[jax.ShapeDtypeStruct((T, D), F32)] + [jax.ShapeDtypeStruct((1, D), F32)] * 3, name=name,
        compiler_params=_cp("arbitrary"))(ds, cv, ln_g, ln_b)


def conv_glu_bwd(dcv, u, dw_w, *, name, tm=128):
    T, D2 = u.shape
    D = D2 // 2
    tm = min(tm, T)
    nt = T // tm
    hb = tm // CHALO

    def body(dcv_ref, dnext_ref, u_ref, w_ref, du_ref, dw_ref, dbin_ref, dbuf, dsh):
        i = pl.program_id(0)
        _acc_init(i, dw_ref, dbin_ref)
        uv = u_ref[...]
        av = uv[:, 0:D]
        sg = _sig(uv[:, D:D2])
        glu = av * sg
        dbuf[0:tm, :] = dcv_ref[...]
        dbuf[tm:tm + CHALO, :] = jnp.where(i == nt - 1, 0.0, dnext_ref[...])
        _phase_copies(dbuf, dsh, tm)
        for c0 in range(0, D, CCOL):
            glu_c = glu[:, c0:c0 + CCOL]
            acc = jnp.zeros((tm, CCOL), F32)
            for k in range(CONV_W):
                moved = _shifted(dbuf, dsh, CONV_W - 1 - k, tm, c0)
                dw_ref[k:k + 1, c0:c0 + CCOL] += _colsum(moved * glu_c)
                acc = acc + w_ref[k:k + 1, c0:c0 + CCOL] * moved
            a_c = av[:, c0:c0 + CCOL]
            s_c = sg[:, c0:c0 + CCOL]
            da = acc * s_c
            dgt = acc * a_c * s_c * (1.0 - s_c)
            dbin_ref[:, c0:c0 + CCOL] += _colsum(da)
            dbin_ref[:, D + c0:D + c0 + CCOL] += _colsum(dgt)
            du_ref[:, c0:c0 + CCOL] = da.astype(BF16)
            du_ref[:, D + c0:D + c0 + CCOL] = dgt.astype(BF16)

    return pl.pallas_call(
        body, grid=(nt,),
        in_specs=[_row(tm, D), pl.BlockSpec((CHALO, D), lambda i: (jnp.minimum((i + 1) * hb, T // CHALO - 1), 0)),
                  _row(tm, D2), _full(CONV_W, D)],
        out_specs=[_row(tm, D2), _full(CHALO, D), _full(1, D2)],
        out_shape=[jax.ShapeDtypeStruct((T, D2), BF16), jax.ShapeDtypeStruct((CHALO, D), F32),
                   jax.ShapeDtypeStruct((1, D2), F32)],
        scratch_shapes=[pltpu.VMEM((tm + CHALO, D), F32), pltpu.VMEM((SUB - 1, tm + CHALO, D), F32)],
        name=name, compiler_params=_cp("arbitrary"))(dcv, dcv, u, dw_w)


HB = 8


def _lb0(lg_ref):
    l0, l1, l2 = lg_ref[0:1, :], lg_ref[1:2, :], lg_ref[2:3, :]
    m = jnp.maximum(jnp.maximum(l0, l1), l2)
    e0 = jnp.exp(l0 - m)
    return e0 / (e0 + jnp.exp(l1 - m) + jnp.exp(l2 - m))


def _mm_exact(m01, x):
    hi = x.astype(BF16)
    r1 = x - hi.astype(F32)
    mid = r1.astype(BF16)
    lo = (r1 - mid.astype(F32)).astype(BF16)
    return _dot(m01, hi) + _dot(m01, mid) + _dot(m01, lo)


def _block_tri(tm):
    r = jnp.arange(tm)[:, None]
    c = jnp.arange(tm)[None, :]
    same = (r // BLK) == (c // BLK)
    return (same & (c <= r)).astype(BF16), (same & (c >= r)).astype(BF16)


def _halves(x):
    return [x[0:SUB, :], x[SUB:BLK, :]]


def _live_halves(s):
    return ([(0, s)] if s < SUB else []) + [(1, max(s - SUB, 0))]


def _const_spec(shape):
    return pl.BlockSpec(shape, lambda h, i: (0, 0))


def _hgrn_specs(H, hb, tm, idx):
    g = H // hb
    return [pl.BlockSpec((tm, hb * HEAD), lambda h, i: (idx(i), h)),
            pl.BlockSpec((tm, hb * HEAD), lambda h, i: (idx(i), g + h)),
            pl.BlockSpec((tm, hb * HEAD), lambda h, i: (idx(i), 2 * g + h)),
            pl.BlockSpec((3, hb * HEAD), lambda h, i: (0, h))]


def hgrn_scan(proj, lb_logits, *, name, tm=128):
    T = proj.shape[0]
    H = proj.shape[1] // (4 * HEAD)
    hb = min(HB, H)
    tm = min(tm, T)
    nt = T // tm
    nblk = tm // BLK
    tril, _ = _block_tri(tm)
    heads = [slice(hh * HEAD, (hh + 1) * HEAD) for hh in range(hb)]

    def body(qp_ref, fz_ref, v_ref, lg_ref, tril_ref, o_ref, st_ref, S_ref, q_s, k_s, b_s):
        @pl.when(pl.program_id(1) == 0)
        def _():
            S_ref[...] = jnp.zeros_like(S_ref)

        st_ref[...] = S_ref[...]
        lb = _lb0(lg_ref)
        f = lb + (1.0 - lb) * _sig(fz_ref[...])
        q_s[...] = _silu(qp_ref[...])
        k_s[...] = 1.0 - f
        b_s[...] = _mm_exact(tril_ref[...], jnp.log(f))
        rows = lax.broadcasted_iota(jnp.int32, (BLK, HEAD), 0)
        S = [S_ref[hh] for hh in range(hb)]
        for nb in range(nblk):
            blk = slice(nb * BLK, (nb + 1) * BLK)
            last = slice(nb * BLK + BLK - 1, nb * BLK + BLK)
            qb = [q_s[blk, c] for c in heads]
            bb = [b_s[blk, c] for c in heads]
            o = [_dot_nt((qb[hh] * jnp.exp(bb[hh])).astype(BF16), S[hh].astype(BF16)) for hh in range(hb)]
            for hh, c in enumerate(heads):
                bc = b_s[last, c]
                kd = k_s[blk, c] * jnp.exp(bc - bb[hh])
                S[hh] = S[hh] * jnp.exp(bc) + _dot_tn(v_ref[blk, c].astype(BF16), kd.astype(BF16))
            for s in range(BLK):
                r = slice(nb * BLK + s, nb * BLK + s + 1)
                for hh, c in enumerate(heads):
                    dec = jnp.exp(jnp.where(rows >= s, bb[hh] - b_s[r, c], NEG))
                    a = jnp.sum(qb[hh] * k_s[r, c] * dec, axis=-1, keepdims=True)
                    o[hh] = o[hh] + a * v_ref[r, c]
            for hh, c in enumerate(heads):
                o_ref[blk, c] = o[hh]
        for hh in range(hb):
            S_ref[hh] = S[hh]

    return pl.pallas_call(
        body, grid=(H // hb, nt),
        in_specs=_hgrn_specs(H, hb, tm, lambda i: i) + [_const_spec((tm, tm))],
        out_specs=[pl.BlockSpec((tm, hb * HEAD), lambda h, i: (i, h)),
                   pl.BlockSpec((None, hb, HEAD, HEAD), lambda h, i: (i, h, 0, 0))],
        out_shape=[jax.ShapeDtypeStruct((T, H * HEAD), F32), jax.ShapeDtypeStruct((nt, H, HEAD, HEAD), F32)],
        scratch_shapes=[pltpu.VMEM((hb, HEAD, HEAD), F32)] + [pltpu.VMEM((tm, hb * HEAD), F32)] * 3, name=name,
        compiler_params=_cp("parallel", "arbitrary"))(proj, proj, proj, lb_logits, tril)


def hgrn_scan_bwd(proj, lb_logits, states, do, *, name, tm=128):
    T = proj.shape[0]
    H = proj.shape[1] // (4 * HEAD)
    hb = min(HB, H)
    tm = min(tm, T)
    nt = T // tm
    nblk = tm // BLK
    tril, triu = _block_tri(tm)
    sel = (jnp.arange(BLK * SUB)[None, :] // SUB == jnp.arange(BLK)[:, None]).astype(BF16)
    heads = [slice(hh * HEAD, (hh + 1) * HEAD) for hh in range(hb)]

    def body(qp_ref, fz_ref, v_ref, lg_ref, st_ref, do_ref, tril_ref, triu_ref, sel_ref, d3_ref, dlb_ref,
             dS_ref, Sb_ref, q_s, k_s, b_s, dq_s, dk_s, dv_s, db_s, pk_s, pv_s):
        i = pl.program_id(1)

        @pl.when(i == 0)
        def _():
            dS_ref[...] = jnp.zeros_like(dS_ref)
            dlb_ref[...] = jnp.zeros_like(dlb_ref)

        lb = _lb0(lg_ref)
        qp = qp_ref[...]
        sg = _sig(fz_ref[...])
        f = lb + (1.0 - lb) * sg
        q_s[...] = _silu(qp)
        k_s[...] = 1.0 - f
        b_s[...] = _mm_exact(tril_ref[...], jnp.log(f))
        rows = lax.broadcasted_iota(jnp.int32, (SUB, HEAD), 0)
        rows1 = lax.broadcasted_iota(jnp.int32, (SUB, 1), 0)

        S = [st_ref[hh] for hh in range(hb)]
        for nb in range(nblk):
            blk = slice(nb * BLK, (nb + 1) * BLK)
            last = slice(nb * BLK + BLK - 1, nb * BLK + BLK)
            for hh, c in enumerate(heads):
                Sb_ref[nb * hb + hh] = S[hh]
                if nb < nblk - 1:
                    bc = b_s[last, c]
                    kd = k_s[blk, c] * jnp.exp(bc - b_s[blk, c])
                    S[hh] = S[hh] * jnp.exp(bc) + _dot_tn(v_ref[blk, c].astype(BF16), kd.astype(BF16))

        dS = [dS_ref[hh] for hh in range(hb)]
        for nb in reversed(range(nblk)):
            blk = slice(nb * BLK, (nb + 1) * BLK)
            last = slice(nb * BLK + BLK - 1, nb * BLK + BLK)
            qb, kb, bb, dob, dq, dbc, ebc = [], [], [], [], [], [], []
            for hh, c in enumerate(heads):
                S0 = Sb_ref[nb * hb + hh]
                qb.append(q_s[blk, c])
                kb.append(k_s[blk, c])
                bb.append(b_s[blk, c])
                dob.append(do_ref[blk, c])
                bc = b_s[last, c]
                eb = jnp.exp(bb[hh])
                ekd = jnp.exp(bc - bb[hh])
                ebc.append(jnp.exp(bc))
                dS16 = dS[hh].astype(BF16)
                dob16 = dob[hh].astype(BF16)
                dq.append(_dot(dob16, S0.astype(BF16)) * eb)
                dki = _dot(v_ref[blk, c].astype(BF16), dS16) * ekd
                dk_s[blk, c] = dki
                dv_s[blk, c] = _dot_nt((kb[hh] * ekd).astype(BF16), dS16)
                dbc.append(_colsum(dS[hh] * S0) * ebc[hh] + _colsum(kb[hh] * dki))
                dS[hh] = dS[hh] * ebc[hh] + _dot_tn(dob16, (qb[hh] * eb).astype(BF16))
            qh, bh, doh, dqh = [[_halves(t[hh]) for hh in range(hb)] for t in (qb, bb, dob, dq)]
            for s in range(BLK):
                r = slice(nb * BLK + s, nb * BLK + s + 1)
                for hh, c in enumerate(heads):
                    ks = k_s[r, c]
                    pk, pv = None, None
                    for hf, lo in _live_halves(s):
                        diff = bh[hh][hf] - b_s[r, c]
                        dec = jnp.exp(diff if lo == 0 else jnp.where(rows >= lo, diff, NEG))
                        w = qh[hh][hf] * dec
                        a = jnp.sum(w * ks, axis=-1, keepdims=True)
                        da = jnp.sum(doh[hh][hf] * v_ref[r, c], axis=-1, keepdims=True)
                        if lo:
                            da = jnp.where(rows1 >= lo, da, 0.0)
                        dqh[hh][hf] = dqh[hh][hf] + (da * ks) * dec
                        pk = da * w if pk is None else pk + da * w
                        pv = a * doh[hh][hf] if pv is None else pv + a * doh[hh][hf]
                    pk_s[hh, s * SUB:(s + 1) * SUB, :] = pk
                    pv_s[hh, s * SUB:(s + 1) * SUB, :] = pv
            for hh, c in enumerate(heads):
                khi, klo = _split2(pk_s[hh])
                dk_s[blk, c] += _dot(sel_ref[...], khi) + _dot(sel_ref[...], klo)
                dv_s[blk, c] += _dot(sel_ref[...], pv_s[hh].astype(BF16))
                dq[hh] = jnp.concatenate(dqh[hh], axis=0)
                dq_s[blk, c] = dq[hh]
                db_s[blk, c] = qb[hh] * dq[hh] - kb[hh] * dk_s[blk, c]
                db_s[last, c] += dbc[hh]
        for hh in range(hb):
            dS_ref[hh] = dS[hh]

        dlf = _mm_exact(triu_ref[...], db_s[...])
        df = dlf / f - dk_s[...]
        d3_ref[0] = (dq_s[...] * _dsilu(qp)).astype(BF16)
        d3_ref[1] = (df * (1.0 - lb) * sg * (1.0 - sg)).astype(BF16)
        d3_ref[2] = dv_s[...].astype(BF16)
        dlb_ref[...] += _colsum(df * (1.0 - sg))

    rev = lambda i: nt - 1 - i
    out_blk = pl.BlockSpec((tm, hb * HEAD), lambda h, i: (rev(i), h))
    return pl.pallas_call(
        body, grid=(H // hb, nt),
        in_specs=_hgrn_specs(H, hb, tm, rev) + [pl.BlockSpec((None, hb, HEAD, HEAD), lambda h, i: (rev(i), h, 0, 0)),
                                                out_blk, _const_spec((tm, tm)), _const_spec((tm, tm)),
                                                _const_spec((BLK, BLK * SUB))],
        out_specs=[pl.BlockSpec((3, tm, hb * HEAD), lambda h, i: (0, rev(i), h)),
                   pl.BlockSpec((1, hb * HEAD), lambda h, i: (0, h))],
        out_shape=[jax.ShapeDtypeStruct((3, T, H * HEAD), BF16), jax.ShapeDtypeStruct((1, H * HEAD), F32)],
        scratch_shapes=[pltpu.VMEM((hb, HEAD, HEAD), F32), pltpu.VMEM((nblk * hb, HEAD, HEAD), F32)]
        + [pltpu.VMEM((tm, hb * HEAD), F32)] * 7 + [pltpu.VMEM((hb, BLK * SUB, HEAD), F32)] * 2, name=name,
        compiler_params=_cp("parallel", "arbitrary"))(proj, proj, proj, lb_logits, states, do, tril, triu, sel)


def hgrn_gate(o, proj, gn, *, name, tm=512):
    T, D = o.shape
    H = D // HEAD
    tm = min(tm, T)

    def body(o_ref, gp_ref, gn_ref, og_ref):
        gn_ = gn_ref[...]
        for h in range(H):
            c = slice(h * HEAD, (h + 1) * HEAD)
            oh = o_ref[:, c]
            r = lax.rsqrt(_rowmean(oh * oh) + EPS)
            og_ref[:, c] = ((oh * r) * gn_ * _silu(gp_ref[:, c])).astype(BF16)

    return pl.pallas_call(
        body, grid=(T // tm,),
        in_specs=[_row(tm, D), pl.BlockSpec((tm, D), lambda i: (i, 3)), _full(1, HEAD)],
        out_specs=_row(tm, D), out_shape=jax.ShapeDtypeStruct((T, D), BF16), name=name,
        compiler_params=_cp("parallel"))(o, proj, gn)


def hgrn_gate_bwd(dog, o, proj, gn, *, name, tm=512):
    T, D = o.shape
    H = D // HEAD
    tm = min(tm, T)

    def body(dog_ref, o_ref, gp_ref, gn_ref, do_ref, dgp_ref, dgn_ref):
        _acc_init(pl.program_id(0), dgn_ref)
        gn_ = gn_ref[...]
        for h in range(H):
            c = slice(h * HEAD, (h + 1) * HEAD)
            oh = o_ref[:, c]
            gp = gp_ref[:, c]
            dg = dog_ref[:, c]
            r = lax.rsqrt(_rowmean(oh * oh) + EPS)
            on = oh * r
            dgp_ref[:, c] = (dg * (on * gn_) * _dsilu(gp)).astype(BF16)
            don = dg * _silu(gp)
            dgn_ref[...] += _colsum(don * on)
            dn = don * gn_
            do_ref[:, c] = r * (dn - on * _rowmean(dn * on))

    return pl.pallas_call(
        body, grid=(T // tm,),
        in_specs=[_row(tm, D), _row(tm, D), pl.BlockSpec((tm, D), lambda i: (i, 3)), _full(1, HEAD)],
        out_specs=[_row(tm, D), _row(tm, D), _full(1, HEAD)],
        out_shape=[jax.ShapeDtypeStruct((T, D), F32), jax.ShapeDtypeStruct((T, D), BF16),
                   jax.ShapeDtypeStruct((1, HEAD), F32)], name=name,
        compiler_params=_cp("arbitrary"))(dog, o, proj, gn)


def _split2(x):
    hi = x.astype(BF16)
    return hi, (x - hi.astype(F32)).astype(BF16)


def ada_mod(c_all, ada_w, *, name):
    L, D, N = ada_w.shape
    B = c_all.shape[0]

    def body(c_ref, w_ref, o_ref):
        chi, clo = _split2(_silu(c_ref[...]))
        whi, wlo = _split2(w_ref[...])
        o_ref[...] = _dot(chi, whi) + _dot(chi, wlo) + _dot(clo, whi)

    return pl.pallas_call(
        body, grid=(L,), in_specs=[_full(B, D), pl.BlockSpec((None, D, N), lambda l: (l, 0, 0))],
        out_specs=pl.BlockSpec((None, B, N), lambda l: (l, 0, 0)),
        out_shape=jax.ShapeDtypeStruct((L, B, N), F32), name=name, compiler_params=_cp("parallel"))(c_all, ada_w)


def ada_wgrad(c_all_t, dmod, *, name, tr=256):
    D, B = c_all_t.shape
    L, _, N = dmod.shape
    tr = min(tr, D)

    def body(c_ref, d_ref, o_ref):
        cond = _silu(c_ref[...])
        acc = cond[:, 0:1] * d_ref[0:1, :]
        for b in range(1, B):
            acc = acc + cond[:, b:b + 1] * d_ref[b:b + 1, :]
        o_ref[...] = acc

    return pl.pallas_call(
        body, grid=(L, D // tr),
        in_specs=[pl.BlockSpec((tr, B), lambda l, r: (r, 0)), pl.BlockSpec((None, B, N), lambda l, r: (l, 0, 0))],
        out_specs=pl.BlockSpec((None, tr, N), lambda l, r: (l, r, 0)),
        out_shape=jax.ShapeDtypeStruct((L, D, N), F32), name=name,
        compiler_params=_cp("parallel", "parallel"))(c_all_t, dmod)


def sum_devices(parts, *, name):
    n, R, C = parts.shape

    def body(p_ref, o_ref):
        acc = p_ref[0]
        for d in range(1, n):
            acc = acc + p_ref[d]
        o_ref[...] = acc

    return pl.pallas_call(body, in_specs=[VMEM_SPEC], out_specs=VMEM_SPEC,
                          out_shape=jax.ShapeDtypeStruct((R, C), F32), name=name)(parts)


def lb_logits_grad(lb_logits, dlb, *, name):
    def body(lg_ref, d_ref, o_ref):
        l0, l1, l2 = lg_ref[0:1, :], lg_ref[1:2, :], lg_ref[2:3, :]
        m = jnp.maximum(jnp.maximum(l0, l1), l2)
        e0, e1, e2 = jnp.exp(l0 - m), jnp.exp(l1 - m), jnp.exp(l2 - m)
        z = e0 + e1 + e2
        p0, p1, p2 = e0 / z, e1 / z, e2 / z
        g = d_ref[...] * p0
        o_ref[0:1, :] = g * (1.0 - p0)
        o_ref[1:2, :] = -g * p1
        o_ref[2:3, :] = -g * p2

    return pl.pallas_call(body, in_specs=[VMEM_SPEC, VMEM_SPEC], out_specs=VMEM_SPEC,
                          out_shape=jax.ShapeDtypeStruct(lb_logits.shape, F32), name=name)(lb_logits, dlb)


def adamw(w, g, m, v, *, name, tr=256, after=None):
    R, C = w.shape
    tr = _tile(R, tr)
    deps = [] if after is None else [after]

    def body(w_ref, g_ref, m_ref, v_ref, *rest):
        d_ref, nm_ref, nv_ref = rest[len(deps):]
        gv = g_ref[...]
        nm = ADAM_B1 * m_ref[...] + (1.0 - ADAM_B1) * gv
        nv = ADAM_B2 * v_ref[...] + (1.0 - ADAM_B2) * (gv * gv)
        m_hat = nm / (1.0 - ADAM_B1 ** ADAM_STEP)
        v_hat = nv / (1.0 - ADAM_B2 ** ADAM_STEP)
        d_ref[...] = -ADAM_LR * (m_hat / (jnp.sqrt(v_hat) + ADAM_EPS) + ADAM_WD * w_ref[...])
        nm_ref[...] = nm
        nv_ref[...] = nv

    spec = pl.BlockSpec((tr, C), lambda i: (i, 0))
    return pl.pallas_call(
        body, grid=(R // tr,), in_specs=[spec] * 4 + [pl.BlockSpec(memory_space=pl.ANY)] * len(deps), out_specs=[spec] * 3,
        out_shape=[jax.ShapeDtypeStruct((R, C), F32)] * 3, name=name,
        compiler_params=_cp("parallel"))(w, g, m, v, *deps)


def _place():
    return lax.axis_index("x"), lax.axis_index("y"), lax.axis_index("c")


def _flip(v, bit):
    return 1 - v if bit else v


def allgather_devices(v, *, name):
    R, C = v.shape

    def body(v_ref, out_ref, send_sems, recv_sems, local_sem):
        x, y, c = _place()
        me = 4 * x + 2 * y + c
        mine = pltpu.make_async_copy(v_ref, out_ref.at[me], local_sem)
        mine.start()
        sends = []
        for k in range(1, N_DEV):
            peer = (_flip(x, k & 4), _flip(y, k & 2), _flip(c, k & 1))
            cp = pltpu.make_async_remote_copy(src_ref=v_ref, dst_ref=out_ref.at[me], send_sem=send_sems.at[k - 1],
                                              recv_sem=recv_sems.at[k - 1], device_id=peer, device_id_type=MESH)
            cp.start()
            sends.append(cp)
        for k in range(1, N_DEV):
            px, py, pc = _flip(x, k & 4), _flip(y, k & 2), _flip(c, k & 1)
            pltpu.make_async_remote_copy(src_ref=v_ref, dst_ref=out_ref.at[4 * px + 2 * py + pc],
                                         send_sem=send_sems.at[k - 1], recv_sem=recv_sems.at[k - 1],
                                         device_id=(px, py, pc), device_id_type=MESH).wait_recv()
        for cp in sends:
            cp.wait_send()
        mine.wait()

    return pl.pallas_call(
        body, in_specs=[VMEM_SPEC], out_specs=VMEM_SPEC, out_shape=jax.ShapeDtypeStruct((N_DEV, R, C), v.dtype),
        scratch_shapes=[pltpu.SemaphoreType.DMA((N_DEV - 1,)), pltpu.SemaphoreType.DMA((N_DEV - 1,)),
                        pltpu.SemaphoreType.DMA], name=name)(v)


def _other_chips(x, y):
    return [(1 - x, y), (x, 1 - y), (1 - x, 1 - y)]


SEM = pl.BlockSpec(memory_space=pltpu.SEMAPHORE)
DATAFLOW = pltpu.SideEffectType.DATAFLOW_SIDE_EFFECTING


def _chip_copy(buf, a, j, q, c, chips, send_sems, recv_sems):
    px, py = chips[j]
    return pltpu.make_async_remote_copy(src_ref=buf.at[q, c], dst_ref=buf.at[q, c], send_sem=send_sems.at[3 * a + j],
                                        recv_sem=recv_sems.at[3 * a + j], device_id=(px, py, c), device_id_type=MESH)


def allgather_chips_start(bufs, *, name):
    n = len(bufs)

    def body(*refs):
        send_sems, recv_sems = refs[n], refs[n + 1]
        outs = refs[n + 2:2 * n + 2]
        token = refs[2 * n + 2]
        x, y, c = _place()
        chips = _other_chips(x, y)
        for a in range(n):
            for j in range(3):
                _chip_copy(outs[a], a, j, 2 * x + y, c, chips, send_sems, recv_sems).start()
        token[...] = jnp.zeros_like(token)

    res = pl.pallas_call(
        body, name=name, in_specs=[HBM] * n,
        out_specs=(SEM, SEM, *([HBM] * n), VMEM_SPEC),
        out_shape=(pltpu.SemaphoreType.DMA((3 * n,)), pltpu.SemaphoreType.DMA((3 * n,)),
                   *[pltpu.HBM(b.shape, b.dtype) for b in bufs], jax.ShapeDtypeStruct((SUB, LANE), F32)),
        input_output_aliases={a: a + 2 for a in range(n)},
        compiler_params=pltpu.CompilerParams(has_side_effects=DATAFLOW),
    )(*[pltpu.with_memory_space_constraint(b, pltpu.HBM) for b in bufs])
    return res[0], res[1], list(res[2:2 + n]), res[2 + n]


def allgather_chips_wait(send_sems, recv_sems, bufs, after, *, name):
    n = len(bufs)

    def body(*refs):
        ins = refs[:n]
        send_sems, recv_sems = refs[n], refs[n + 1]
        x, y, c = _place()
        chips = _other_chips(x, y)
        for a in range(n):
            for j, (px, py) in enumerate(chips):
                _chip_copy(ins[a], a, j, 2 * x + y, c, chips, send_sems, recv_sems).wait_send()
                _chip_copy(ins[a], a, j, 2 * px + py, c, chips, send_sems, recv_sems).wait_recv()

    return list(pl.pallas_call(
        body, name=name, in_specs=[HBM] * n + [SEM, SEM, pl.BlockSpec(memory_space=pl.ANY)],
        out_specs=[HBM] * n, out_shape=[pltpu.HBM(b.shape, b.dtype) for b in bufs],
        input_output_aliases={a: a for a in range(n)},
        compiler_params=pltpu.CompilerParams(has_side_effects=DATAFLOW),
    )(*bufs, send_sems, recv_sems, after))


def forward_to_sibling(bufs, *, name):
    n = len(bufs)

    def body(*refs):
        outs = refs[n:2 * n]
        send_sems, recv_sems = refs[2 * n:]
        x, y, c = _place()
        chips = _other_chips(x, y)

        def copy(a, j, half, to):
            px, py = chips[j]
            slab = outs[a].at[2 * px + py, half]
            return pltpu.make_async_remote_copy(src_ref=slab, dst_ref=slab, send_sem=send_sems.at[a, j],
                                                recv_sem=recv_sems.at[a, j], device_id=to, device_id_type=MESH)

        sends = [copy(a, j, c, (x, y, 1 - c)) for a in range(n) for j in range(3)]
        for cp in sends:
            cp.start()
        for a in range(n):
            for j in range(3):
                copy(a, j, 1 - c, (x, y, c)).wait_recv()
        for cp in sends:
            cp.wait_send()

    return pl.pallas_call(
        body, in_specs=[HBM] * n, out_specs=[HBM] * n,
        out_shape=[jax.ShapeDtypeStruct(b.shape, b.dtype) for b in bufs],
        input_output_aliases={a: a for a in range(n)},
        scratch_shapes=[pltpu.SemaphoreType.DMA((n, 3)), pltpu.SemaphoreType.DMA((n, 3))], name=name)(*bufs)


def pair_add(g, other, c_idx, *, name, tr=256):
    _, Q, R, C = g.shape
    tr = _tile(R, tr)

    def body(c_ref, g_ref, o_ref, out_ref):
        out_ref[...] = (g_ref[...] + o_ref[...]).astype(BF16)

    return pl.pallas_call(
        body,
        grid_spec=pltpu.PrefetchScalarGridSpec(
            num_scalar_prefetch=1, grid=(Q, R // tr),
            in_specs=[pl.BlockSpec((None, None, tr, C), lambda q, r, c_ref: (c_ref[0], q, r, 0)),
                      pl.BlockSpec((None, tr, C), lambda q, r, c_ref: (q, r, 0))],
            out_specs=pl.BlockSpec((None, tr, C), lambda q, r, c_ref: (q, r, 0))),
        out_shape=jax.ShapeDtypeStruct((Q, R, C), BF16), name=name,
        compiler_params=_cp("parallel", "parallel"))(c_idx, g, other)


def chip_sum(sums, landed, qc_idx, *, name, tr=256):
    _, R, C = sums.shape
    tr = _tile(R, tr)

    def body(qc_ref, own_ref, l_ref, o_ref):
        acc = own_ref[...].astype(F32)
        for k in range(3):
            acc = acc + l_ref[k].astype(F32)
        o_ref[...] = acc

    return pl.pallas_call(
        body,
        grid_spec=pltpu.PrefetchScalarGridSpec(
            num_scalar_prefetch=1, grid=(R // tr,),
            in_specs=[pl.BlockSpec((None, tr, C), lambda r, qc: (qc[0], r, 0)),
                      pl.BlockSpec((3, tr, C), lambda r, qc: (0, r, 0))],
            out_specs=pl.BlockSpec((None, tr, C), lambda r, qc: (qc[1], r, 0))),
        out_shape=jax.ShapeDtypeStruct((2, R, C), F32), name=name,
        compiler_params=_cp("parallel"))(qc_idx, sums, landed)


def half_swap(bufs, *, name):
    n = len(bufs)

    def body(*refs):
        outs = refs[n:2 * n]
        send_sems, recv_sems = refs[2 * n:]
        x, y, c = _place()
        cps = [pltpu.make_async_remote_copy(src_ref=outs[a].at[c], dst_ref=outs[a].at[c], send_sem=send_sems.at[a],
                                            recv_sem=recv_sems.at[a], device_id=(x, y, 1 - c), device_id_type=MESH)
               for a in range(n)]
        for cp in cps:
            cp.start()
        for a in range(n):
            pltpu.make_async_remote_copy(src_ref=outs[a].at[c], dst_ref=outs[a].at[1 - c], send_sem=send_sems.at[a],
                                         recv_sem=recv_sems.at[a], device_id=(x, y, 1 - c),
                                         device_id_type=MESH).wait_recv()
        for cp in cps:
            cp.wait_send()

    return pl.pallas_call(
        body, in_specs=[HBM] * n, out_specs=[HBM] * n,
        out_shape=[jax.ShapeDtypeStruct(b.shape, b.dtype) for b in bufs],
        input_output_aliases={a: a for a in range(n)},
        scratch_shapes=[pltpu.SemaphoreType.DMA((n,)), pltpu.SemaphoreType.DMA((n,))], name=name)(*bufs)


def _chip_copies(src, dst, send_sems, recv_sems):
    x, y, c = _place()
    return [pltpu.make_async_remote_copy(src_ref=src[a].at[2 * px + py], dst_ref=dst[a].at[j],
                                         send_sem=send_sems.at[3 * a + j], recv_sem=recv_sems.at[3 * a + j],
                                         device_id=(px, py, c), device_id_type=MESH)
            for a in range(len(src)) for j, (px, py) in enumerate(_other_chips(x, y))]


def _pair_copies(src, dst, send_sems, recv_sems):
    x, y, c = _place()
    return [pltpu.make_async_remote_copy(src_ref=src[a].at[1 - c], dst_ref=dst[a], send_sem=send_sems.at[a],
                                         recv_sem=recv_sems.at[a], device_id=(x, y, 1 - c), device_id_type=MESH)
            for a in range(len(src))]


def _device_copies(src, dst, send_sems, recv_sems):
    x, y, c = _place()
    mine = src[0].at[4 * x + 2 * y + c]
    return [pltpu.make_async_remote_copy(src_ref=mine, dst_ref=mine, send_sem=send_sems.at[k - 1],
                                         recv_sem=recv_sems.at[k - 1],
                                         device_id=(_flip(x, k & 4), _flip(y, k & 2), _flip(c, k & 1)), device_id_type=MESH)
            for k in range(1, N_DEV)]


def exchange_start(src, landing, copies, n_sems, *, name):
    n, m = len(src), len(src) + len(landing)

    def body(*refs):
        send_sems, recv_sems = refs[m], refs[m + 1]
        for cp in copies(refs[m + 2:m + 2 + n], refs[m + 2 + n:2 * m + 2], send_sems, recv_sems):
            cp.start()
        token = refs[2 * m + 2]
        token[...] = jnp.zeros_like(token)

    res = pl.pallas_call(
        body, name=name, in_specs=[HBM] * m,
        out_specs=(SEM, SEM, *([HBM] * m), VMEM_SPEC),
        out_shape=(pltpu.SemaphoreType.DMA((n_sems,)), pltpu.SemaphoreType.DMA((n_sems,)),
                   *[pltpu.HBM(b.shape, b.dtype) for b in src + landing], jax.ShapeDtypeStruct((SUB, LANE), F32)),
        input_output_aliases={a: a + 2 for a in range(m)},
        compiler_params=pltpu.CompilerParams(has_side_effects=DATAFLOW),
    )(*[pltpu.with_memory_space_constraint(b, pltpu.HBM) for b in src + landing])
    return res[0], res[1], list(res[2:2 + n]), list(res[2 + n:2 + m]), res[2 + m]


def exchange_wait(send_sems, recv_sems, src, landed, copies, after, *, name):
    n, m = len(src), len(src) + len(landed)

    def body(*refs):
        for cp in copies(refs[:n], refs[n:m], refs[m], refs[m + 1]):
            cp.wait_send()
            cp.wait_recv()

    res = pl.pallas_call(
        body, name=name, in_specs=[HBM] * m + [SEM, SEM, pl.BlockSpec(memory_space=pl.ANY)],
        out_specs=[HBM] * m, out_shape=[pltpu.HBM(b.shape, b.dtype) for b in src + landed],
        input_output_aliases={a: a for a in range(m)},
        compiler_params=pltpu.CompilerParams(has_side_effects=DATAFLOW),
    )(*src, *landed, send_sems, recv_sems, after)
    return list(res[:n]), list(res[n:])


def finish_reduce(sums, landed, q, c, tag):
    qc_idx = jnp.stack([q, c]).astype(jnp.int32)
    return [chip_sum(s, l, qc_idx, name=f"grad_chip_sum_{tag}{a}") for a, (s, l) in enumerate(zip(sums, landed))]


def _ffn_forward(x, h, mod, post_g, w_up, w_down, dw_w, dw_b, tag, next_norm=None, tgt=None):
    _, _, gate = mod
    u0 = mm_nn(h, w_up, name=f"{tag}_up", out_dtype=BF16, perm=_ffn_perm)
    z, ab = ffn_act(u0, dw_w, dw_b, name=f"{tag}_act")
    y = mm_nn(z, w_down, name=f"{tag}_down")
    if tgt is None:
        out = post_residual_prenorm(x, y, post_g, gate, *next_norm, name=f"{tag}_post")
    else:
        out = post_residual_loss(x, y, post_g, gate, tgt, name=f"{tag}_post_loss")
    return out, (x, h, u0, ab, z, y)


def _ffn_backward(dx, entry, saved, mod, pre_g, post_g, w_up, w_down, dw_w, tag, before):
    x, h, u0, ab, z, y = saved
    _, sc, gate = mod
    dy, dgate, dpost = entry
    dz = mm_nt(dy, w_down, name=f"{tag}_down_dx", out_dtype=BF16)
    g_down = mm_tn(z, dy, name=f"{tag}_down_dw", J=2, block="a", row_chips=2)
    du0, dconv = ffn_act_bwd(dz, u0, ab, dw_w, name=f"{tag}_act_bwd")
    dh = mm_nt(du0, w_up, name=f"{tag}_up_dx", perm=_ffn_perm)
    g_up = mm_tn(h, du0, name=f"{tag}_up_dw", J=4, block="b", perm=_ffn_perm)
    dx_in, dsh, dsc, dpre, *prev = prenorm_post_bwd(dh, x, dx, pre_g, sc, *before, name=f"{tag}_prenorm_bwd")
    nb = u0.shape[1] // 4
    dconv = dconv[:, 0].reshape(4, 2, 2, nb).transpose(0, 2, 1, 3).reshape(4, 4 * nb)
    return dx_in, dict(dsh=dsh, dsc=dsc, dgate=dgate, dpre=dpre, dpost=dpost, g_up=g_up, g_down=g_down,
                       d_dw_w=dconv[0:FFN_W], d_dw_b=dconv[3:4]), prev


def _local_step(x, tgt, mods, P, first_weights=None, late_weights=None, grads_ready=None):
    m0, m1 = mods
    h1 = prenorm(x, P["pre_mix_g"][0:1], m0[1], m0[0], name="hgrn_prenorm")
    token = None
    if first_weights is not None:
        first, token = first_weights(h1)
        P = {**P, **first}
    proj = mm_nn(h1, P["hgrn_w_in"], name="hgrn_in", after=token)
    o, states = hgrn_scan(proj, P["hgrn_lb_logits"], name="hgrn_scan")
    og = hgrn_gate(o, proj, P["hgrn_gnorm_g"], name="hgrn_gate")
    y1 = mm_nn(og, P["hgrn_w_out"], name="hgrn_out")
    x1, h_f0 = post_residual_prenorm(x, y1, P["post_mix_g"][0:1], m0[2], P["pre_ffn_g"][0:1], m0[4], m0[3],
                                     name="hgrn_post")
    if late_weights is not None:
        P = {**P, **late_weights(x1)}
    (x2, h3), ffn0 = _ffn_forward(x1, h_f0, m0[3:6], P["post_ffn_g"][0:1], P["ffn_w_up"][0], P["ffn_w_down"][0],
                                  P["ffn_dw_w"][0], P["ffn_dw_b"][0:1], "ffn0",
                                  next_norm=(P["pre_mix_g"][1:2], m1[1], m1[0]))
    u = mm_nn(h3, P["conv_w_in"], name="conv_in", bias=P["conv_b_in"])
    s, cv = conv_act(u, P["conv_dw_w"], P["conv_dw_b"], P["conv_ln_g"], P["conv_ln_b"], name="conv_act")
    y3 = mm_nn(s, P["conv_w_out"], name="conv_out", bias=P["conv_b_out"])
    x3, h_f1 = post_residual_prenorm(x2, y3, P["post_mix_g"][1:2], m1[2], P["pre_ffn_g"][1:2], m1[4], m1[3],
                                     name="conv_post")
    (dx4, lcols, *entry_f1), ffn1 = _ffn_forward(x3, h_f1, m1[3:6], P["post_ffn_g"][1:2], P["ffn_w_up"][1], P["ffn_w_down"][1],
                                      P["ffn_dw_w"][1], P["ffn_dw_b"][1:2], "ffn1", tgt=tgt)
    dx3, f1, (dy3, dg1_1, dpostmix1, d_b_out) = _ffn_backward(
        dx4, entry_f1, ffn1, m1[3:6], P["pre_ffn_g"][1:2], P["post_ffn_g"][1:2], P["ffn_w_up"][1], P["ffn_w_down"][1],
        P["ffn_dw_w"][1], "ffn1", before=(y3, P["post_mix_g"][1:2], m1[2]))
    ds = mm_nt(dy3, P["conv_w_out"], name="conv_out_dx")
    g_conv_out = mm_tn(s, dy3, name="conv_out_dw", J=1, block="a", row_chips=4)
    dcv, d_ln_g, d_ln_b, d_dw_b = conv_norm_bwd(ds, cv, P["conv_ln_g"], P["conv_ln_b"], name="conv_norm_bwd")
    du, d_dw_w, d_b_in = conv_glu_bwd(dcv, u, P["conv_dw_w"], name="conv_glu_bwd")
    dh3 = mm_nt(du, P["conv_w_in"], name="conv_in_dx")
    g_conv_in = mm_tn(h3, du, name="conv_in_dw", J=2, block="b", col_chips=2)
    if grads_ready is not None:
        token = grads_ready("l1", [g_conv_in, g_conv_out, f1["g_up"], f1["g_down"]])
        m0 = tuple(m + token[0:1, 0:1] for m in m0)
    dx2, dsh1_1, dsc1_1, dpremix1, *entry_f0 = prenorm_post_bwd(
        dh3, x2, dx3, P["pre_mix_g"][1:2], m1[1], ffn0[5], P["post_ffn_g"][0:1], m0[5], name="conv_prenorm_bwd")
    dx1, f0, (dy1, dg1_0, dpostmix0, _) = _ffn_backward(
        dx2, entry_f0[:3], ffn0, m0[3:6], P["pre_ffn_g"][0:1], P["post_ffn_g"][0:1], P["ffn_w_up"][0],
        P["ffn_w_down"][0], P["ffn_dw_w"][0], "ffn0", before=(y1, P["post_mix_g"][0:1], m0[2]))
    token = grads_ready("f0", [f0["g_up"], f0["g_down"]]) if grads_ready is not None else None
    dog = mm_nt(dy1, P["hgrn_w_out"], name="hgrn_out_dx", after=token)
    g_hgrn_out = mm_tn(og, dy1, name="hgrn_out_dw", J=1, block="a", row_chips=4)
    do, dgp, d_gn = hgrn_gate_bwd(dog, o, proj, P["hgrn_gnorm_g"], name="hgrn_gate_bwd")
    d3, dlb = hgrn_scan_bwd(proj, P["hgrn_lb_logits"], states, do, name="hgrn_scan_bwd")
    g_hgrn_in = mm_tn_parts(h1, d3, dgp, name="hgrn_in_dw")
    token = grads_ready("hg", [g_hgrn_in, g_hgrn_out]) if grads_ready is not None else None
    dh1 = mm_nt_parts(d3, dgp, P["hgrn_w_in"], name="hgrn_in_dx", after=token)
    dx0, dsh1_0, dsc1_0, dpremix0 = prenorm_bwd(dh1, x, dx1, P["pre_mix_g"][0:1], m0[1], name="hgrn_prenorm_bwd")

    dmod = jnp.stack([
        jnp.concatenate([dsh1_0, dsc1_0, dg1_0, f0["dsh"], f0["dsc"], f0["dgate"]], axis=1)[0],
        jnp.concatenate([dsh1_1, dsc1_1, dg1_1, f1["dsh"], f1["dsc"], f1["dgate"]], axis=1)[0]])
    small = dict(
        loss=lcols,
        pre_mix_g=jnp.concatenate([dpremix0, dpremix1]), post_mix_g=jnp.concatenate([dpostmix0, dpostmix1]),
        pre_ffn_g=jnp.concatenate([f0["dpre"], f1["dpre"]]), post_ffn_g=jnp.concatenate([f0["dpost"], f1["dpost"]]),
        lb=dlb, hgrn_gnorm_g=d_gn, ffn_dw_b=jnp.concatenate([f0["d_dw_b"], f1["d_dw_b"]]), dmod=dmod,
        conv_b_in=d_b_in, conv_dw_w=d_dw_w[0:CONV_W], conv_dw_b=d_dw_b, conv_ln_g=d_ln_g, conv_ln_b=d_ln_b,
        conv_b_out=d_b_out, ffn_dw_w=jnp.stack([f0["d_dw_w"], f1["d_dw_w"]]))
    big = [g_hgrn_in, g_hgrn_out, g_conv_in, g_conv_out, f0["g_up"], f1["g_up"], f0["g_down"], f1["g_down"]]
    return dx0, small, big


def _pack(parts, rows=8):
    flat = jnp.concatenate([p.reshape(-1).astype(F32) for p in parts])
    per = rows * 128
    pad = (-flat.shape[0]) % per
    return jnp.pad(flat, (0, pad)).reshape(rows, -1)


def _unpack(flat, shapes):
    out, off = [], 0
    for s in shapes:
        n = 1
        for d in s:
            n *= d
        out.append(flat[..., off:off + n].reshape(flat.shape[:-1] + tuple(s)))
        off += n
    return out


def _from_chips(stacked, axis):
    moved = jnp.moveaxis(stacked, 0, axis)
    shape = list(moved.shape)
    return moved.reshape(shape[:axis] + [shape[axis] * shape[axis + 1]] + shape[axis + 2:])


def _my_shard(full, axis, q):
    n = full.shape[axis] // N_CHIPS
    return lax.dynamic_slice_in_dim(full, q * n, n, axis=axis)


def kernel(x, c, ada_w, ada_b, pre_mix_g, post_mix_g, pre_ffn_g, post_ffn_g, hgrn_w_in, hgrn_lb_logits, hgrn_gnorm_g, hgrn_w_out, conv_w_in, conv_b_in, conv_dw_w, conv_dw_b, conv_ln_g, conv_ln_b, conv_w_out, conv_b_out, ffn_w_up, ffn_dw_w, ffn_dw_b, ffn_w_down, loss_target, m_ada_w, m_ada_b, m_pre_mix_g, m_post_mix_g, m_pre_ffn_g, m_post_ffn_g, m_hgrn_w_in, m_hgrn_lb_logits, m_hgrn_gnorm_g, m_hgrn_w_out, m_conv_w_in, m_conv_b_in, m_conv_dw_w, m_conv_dw_b, m_conv_ln_g, m_conv_ln_b, m_conv_w_out, m_conv_b_out, m_ffn_w_up, m_ffn_dw_w, m_ffn_dw_b, m_ffn_w_down, v_ada_w, v_ada_b, v_pre_mix_g, v_post_mix_g, v_pre_ffn_g, v_post_ffn_g, v_hgrn_w_in, v_hgrn_lb_logits, v_hgrn_gnorm_g, v_hgrn_w_out, v_conv_w_in, v_conv_b_in, v_conv_dw_w, v_conv_dw_b, v_conv_ln_g, v_conv_ln_b, v_conv_w_out, v_conv_b_out, v_ffn_w_up, v_ffn_dw_w, v_ffn_dw_b, v_ffn_w_down):
    W = dict(ada_w=ada_w, ada_b=ada_b, pre_mix_g=pre_mix_g, post_mix_g=post_mix_g, pre_ffn_g=pre_ffn_g,
             post_ffn_g=post_ffn_g, hgrn_w_in=hgrn_w_in, hgrn_lb_logits=hgrn_lb_logits, hgrn_gnorm_g=hgrn_gnorm_g,
             hgrn_w_out=hgrn_w_out, conv_w_in=conv_w_in, conv_b_in=conv_b_in, conv_dw_w=conv_dw_w,
             conv_dw_b=conv_dw_b, conv_ln_g=conv_ln_g, conv_ln_b=conv_ln_b, conv_w_out=conv_w_out,
             conv_b_out=conv_b_out, ffn_w_up=ffn_w_up, ffn_dw_w=ffn_dw_w, ffn_dw_b=ffn_dw_b, ffn_w_down=ffn_w_down)
    M = dict(ada_w=m_ada_w, ada_b=m_ada_b, pre_mix_g=m_pre_mix_g, post_mix_g=m_post_mix_g, pre_ffn_g=m_pre_ffn_g,
             post_ffn_g=m_post_ffn_g, hgrn_w_in=m_hgrn_w_in, hgrn_lb_logits=m_hgrn_lb_logits,
             hgrn_gnorm_g=m_hgrn_gnorm_g, hgrn_w_out=m_hgrn_w_out, conv_w_in=m_conv_w_in, conv_b_in=m_conv_b_in,
             conv_dw_w=m_conv_dw_w, conv_dw_b=m_conv_dw_b, conv_ln_g=m_conv_ln_g, conv_ln_b=m_conv_ln_b,
             conv_w_out=m_conv_w_out, conv_b_out=m_conv_b_out, ffn_w_up=m_ffn_w_up, ffn_dw_w=m_ffn_dw_w,
             ffn_dw_b=m_ffn_dw_b, ffn_w_down=m_ffn_w_down)
    V = dict(ada_w=v_ada_w, ada_b=v_ada_b, pre_mix_g=v_pre_mix_g, post_mix_g=v_post_mix_g, pre_ffn_g=v_pre_ffn_g,
             post_ffn_g=v_post_ffn_g, hgrn_w_in=v_hgrn_w_in, hgrn_lb_logits=v_hgrn_lb_logits,
             hgrn_gnorm_g=v_hgrn_gnorm_g, hgrn_w_out=v_hgrn_w_out, conv_w_in=v_conv_w_in, conv_b_in=v_conv_b_in,
             conv_dw_w=v_conv_dw_w, conv_dw_b=v_conv_dw_b, conv_ln_g=v_conv_ln_g, conv_ln_b=v_conv_ln_b,
             conv_w_out=v_conv_w_out, conv_b_out=v_conv_b_out, ffn_w_up=v_ffn_w_up, ffn_dw_w=v_ffn_dw_w,
             ffn_dw_b=v_ffn_dw_b, ffn_w_down=v_ffn_w_down)
    names = list(W)
    xi, yi, ci = lax.axis_index("x"), lax.axis_index("y"), lax.axis_index("c")
    q = 2 * xi + yi
    me = 2 * q + ci
    D = x.shape[-1]
    L = ada_w.shape[0]

    small_w = ["conv_b_in", "conv_dw_w", "conv_dw_b", "conv_ln_g", "conv_ln_b", "conv_b_out", "ffn_dw_w"]
    small_axis = dict(conv_b_in=1, conv_dw_w=2, conv_dw_b=1, conv_ln_g=1, conv_ln_b=1, conv_b_out=1, ffn_dw_w=2)
    packed = _pack([c] + [W[n] for n in small_w])

    def halves(w):
        shard = w.astype(BF16).reshape(1, 2, w.shape[0] // 2, w.shape[1])
        buf = lax.empty((N_CHIPS,) + shard.shape[1:], BF16)
        return lax.dynamic_update_slice_in_dim(buf, shard, q, axis=0)

    gathered = allgather_devices(packed, name="gather_small_params").reshape(N_DEV, -1)
    c_all = gathered[:, 0:D]
    per_chip = gathered.reshape(N_CHIPS, 2, -1)[:, 0, D:]
    parts = _unpack(per_chip, [W[n].shape for n in small_w])
    P = {n: _from_chips(p, small_axis[n]) for n, p in zip(small_w, parts)}
    P["conv_dw_w"] = P["conv_dw_w"][0]
    for n in ("pre_mix_g", "post_mix_g", "pre_ffn_g", "post_ffn_g", "hgrn_lb_logits", "hgrn_gnorm_g", "ffn_dw_b"):
        P[n] = W[n]

    modp = ada_mod(c_all, ada_w, name="ada_mod")
    ncol = modp.shape[-1]
    mod_all = allgather_devices(modp.reshape(L * N_DEV, ncol), name="gather_mod")
    mod_all = mod_all.reshape(N_CHIPS, 2, L, N_DEV, ncol)[:, 0]
    mod_me = lax.dynamic_index_in_dim(mod_all, me, axis=2, keepdims=False)
    mod = mod_me.transpose(1, 0, 2).reshape(L, N_CHIPS * ncol) + ada_b
    mods = [tuple(mod[l:l + 1, k * D:(k + 1) * D] for k in range(6)) for l in range(L)]

    hg_shards, _ = lax.optimization_barrier(([halves(hgrn_w_in[0]), halves(hgrn_w_out[0])], mod))
    hg_send, hg_recv, hg_bufs, hg_token = allgather_chips_start(hg_shards, name="gather_hgrn_weights_start")
    mods[0] = tuple(m + hg_token[0:1, 0:1] for m in mods[0])

    stack = lambda t: t.reshape(N_CHIPS, t.shape[1] * t.shape[2], t.shape[3])
    rowsh = lambda t: t.reshape(1, N_CHIPS * t.shape[1] * t.shape[2], t.shape[3])
    pairs = lambda t: t.reshape(2, 2, t.shape[1], t.shape[2]).transpose(0, 2, 1, 3).reshape(2, t.shape[1], 2 * t.shape[2])
    late_shards = [conv_w_in[0], conv_w_out[0], ffn_w_up[0], ffn_w_up[1], ffn_w_down[0], ffn_w_down[1]]
    late = {}

    def first_weights(h1):
        g = forward_to_sibling(allgather_chips_wait(hg_send, hg_recv, hg_bufs, h1, name="gather_hgrn_weights_wait"),
                               name="gather_hgrn_weights_forward")
        late_bufs, _ = lax.optimization_barrier(([halves(w) for w in late_shards], g))
        late["send"], late["recv"], late["bufs"], token = allgather_chips_start(late_bufs, name="gather_weights_start")
        return dict(hgrn_w_in=stack(g[0]), hgrn_w_out=rowsh(g[1])), token

    def late_weights(x1):
        landed = allgather_chips_wait(late["send"], late["recv"], late["bufs"], x1, name="gather_weights_wait")
        g = forward_to_sibling(landed, name="gather_weights_forward")
        return dict(conv_w_in=pairs(stack(g[0])), conv_w_out=rowsh(g[1]), ffn_w_up=[stack(g[2]), stack(g[3])],
                    ffn_w_down=[rowsh(g[4]), rowsh(g[5])])

    c_idx = ci.astype(jnp.int32).reshape(1)
    pending, in_flight = {}, {}

    def chip_stage(after):
        tag, (send, recv, grads, landing) = pending.popitem()
        grads, others = exchange_wait(send, recv, grads, landing, _pair_copies, after, name=f"grad_pair_wait_{tag}")
        sums = [pair_add(g_, o_, c_idx, name=f"grad_pair_add_{tag}_{a}") for a, (g_, o_) in enumerate(zip(grads, others))]
        landing = [lax.empty((3,) + s_.shape[1:], s_.dtype) for s_ in sums]
        send, recv, sums, landing, tok = exchange_start(sums, landing, _chip_copies, 3 * len(sums),
                                                        name=f"grad_chip_exchange_start_{tag}")
        in_flight[tag] = (send, recv, sums, landing)
        return tok

    def grads_ready(tag, grads):
        tok = chip_stage(grads[0]) if pending else 0.0
        landing = [lax.empty(g_.shape[1:], g_.dtype) for g_ in grads]
        send, recv, grads, landing, tok2 = exchange_start(grads, landing, _pair_copies, len(grads),
                                                          name=f"grad_pair_start_{tag}")
        pending[tag] = (send, recv, grads, landing)
        return tok + tok2

    grad_x, small, big = _local_step(x[0], loss_target[0], mods, P, first_weights, late_weights, grads_ready)

    small_names = list(small)
    packed_g = _pack([small[n] for n in small_names])
    gs_buf = lax.dynamic_update_slice_in_dim(lax.empty((N_DEV,) + packed_g.shape, F32), packed_g[None], me, axis=0)
    sg_send, sg_recv, gs_buf, _, tok_sg = exchange_start([gs_buf], [], _device_copies, N_DEV - 1,
                                                         name="gather_small_grads_start")

    tok_hg = chip_stage(tok_sg)
    G = {}
    halves = []
    for tag in ("f0", "l1"):
        sums_t, landed_t = exchange_wait(*in_flight[tag], _chip_copies, grad_x, name=f"grad_chip_exchange_wait_{tag}")
        halves += finish_reduce(sums_t, landed_t, q, ci, f"{tag}_")
    red = [f.reshape(2 * f.shape[1], f.shape[2]) for f in half_swap(halves, name="grad_half_swap")]
    G["conv_w_in"], G["conv_w_out"] = red[2][None], red[3][None]
    G["ffn_w_up"] = jnp.stack([red[0], red[4]])
    G["ffn_w_down"] = jnp.stack([red[1], red[5]])

    delta, new_m, new_v = {}, {}, {}

    def adamw_matrix(n, after=None):
        shp = W[n].shape
        two = lambda t: t.reshape(-1, shp[-1])
        d_, m_, v_ = adamw(two(W[n]), two(G[n]), two(M[n]), two(V[n]), name=f"adamw_{n}", after=after)
        delta[n], new_m[n], new_v[n] = d_.reshape(shp), m_.reshape(shp), v_.reshape(shp)

    big_names = ["ada_w", "hgrn_w_in", "hgrn_w_out", "conv_w_in", "conv_w_out", "ffn_w_up", "ffn_w_down"]
    for n in ("conv_w_in", "conv_w_out", "ffn_w_up", "ffn_w_down"):
        adamw_matrix(n, after=tok_hg)

    (gs,), _ = exchange_wait(sg_send, sg_recv, gs_buf, [], _device_copies, delta["ffn_w_down"],
                             name="gather_small_grads_wait")
    dmod_all = _unpack(gs.reshape(N_DEV, -1), [small[n].shape for n in small_names])[small_names.index("dmod")]
    tot = sum_devices(gs, name="sum_small_grads").reshape(1, -1)
    S = dict(zip(small_names, _unpack(tot, [small[n].shape for n in small_names])))
    S = {n: v[0] for n, v in S.items()}
    loss = 0.5 * jnp.sum(S["loss"]) / D

    dmod_q = lax.dynamic_slice_in_dim(dmod_all, q * ncol, ncol, axis=2)
    G["ada_w"] = ada_wgrad(c_all.T, dmod_q.transpose(1, 0, 2), name="ada_wgrad")
    G["ada_b"] = S["dmod"]
    for n in ("pre_mix_g", "post_mix_g", "pre_ffn_g", "post_ffn_g", "hgrn_gnorm_g", "ffn_dw_b"):
        G[n] = S[n]
    G["hgrn_lb_logits"] = lb_logits_grad(hgrn_lb_logits, S["lb"], name="lb_logits_grad")
    G["conv_b_in"] = _my_shard(S["conv_b_in"], 1, q)
    G["conv_dw_w"] = _my_shard(S["conv_dw_w"], 1, q)[None]
    for n in ("conv_dw_b", "conv_ln_g", "conv_ln_b", "conv_b_out"):
        G[n] = _my_shard(S[n], 1, q)
    G["ffn_dw_w"] = _my_shard(S["ffn_dw_w"], 2, q)
    adamw_matrix("ada_w")

    sums_h, landed_h = exchange_wait(*in_flight["hg"], _chip_copies, delta["ada_w"], name="grad_chip_exchange_wait_hg")
    red_h = half_swap(finish_reduce(sums_h, landed_h, q, ci, "hg_"), name="grad_half_swap_hg")
    G["hgrn_w_in"], G["hgrn_w_out"] = [f.reshape(1, 2 * f.shape[1], f.shape[2]) for f in red_h]
    for n in ("hgrn_w_in", "hgrn_w_out"):
        adamw_matrix(n)
    rest = [n for n in names if n not in big_names]
    d_, m_, v_ = adamw(_pack([W[n] for n in rest]), _pack([G[n] for n in rest]), _pack([M[n] for n in rest]),
                       _pack([V[n] for n in rest]), name="adamw_small")
    shapes = [W[n].shape for n in rest]
    for n, a, b_, c_ in zip(rest, _unpack(d_.reshape(-1), shapes), _unpack(m_.reshape(-1), shapes),
                            _unpack(v_.reshape(-1), shapes)):
        delta[n], new_m[n], new_v[n] = a, b_, c_

    return (loss, grad_x[None], *[G[n].reshape(W[n].shape) for n in names], *[delta[n] for n in names],
            *[new_m[n] for n in names], *[new_v[n] for n in names])
```

```python
import jax
import jax.numpy as jnp
from jax import lax
from jax.experimental import pallas as pl
from jax.experimental.pallas import tpu as pltpu

F32 = jnp.float32
BF16 = jnp.bfloat16
EPS = 1e-6
HEAD = 128
BLK = 16
NEG = -1e30
CONV_W = 31
FFN_W = 3
N_CHIPS = 4
N_DEV = 8
SUB = 8
LANE = 128
V7X_VMEM_LIMIT = 56 * 1024 * 1024
MESH = pl.DeviceIdType.MESH
HBM = pl.BlockSpec(memory_space=pltpu.HBM)
VMEM_SPEC = pl.BlockSpec(memory_space=pltpu.VMEM)

ADAM_LR = 0.001
ADAM_B1 = 0.9
ADAM_B2 = 0.999
ADAM_EPS = 1e-08
ADAM_WD = 0.01
ADAM_STEP = 10


def _cp(*sem):
    return pltpu.CompilerParams(dimension_semantics=sem, vmem_limit_bytes=V7X_VMEM_LIMIT)


def _sig(x):
    return 0.5 * jnp.tanh(0.5 * x) + 0.5


def _silu(x):
    return x * _sig(x)


def _dsilu(x):
    s = _sig(x)
    return s * (1.0 + x * (1.0 - s))


def _dot(a, b):
    return jnp.dot(a, b, preferred_element_type=F32)


def _dot_nt(a, b):
    return lax.dot_general(a, b, (((1,), (1,)), ((), ())), preferred_element_type=F32)


def _dot_tn(a, b):
    return lax.dot_general(a, b, (((0,), (0,)), ((), ())), preferred_element_type=F32)


def _colsum(x):
    return jnp.sum(x, axis=0, keepdims=True)


def _rowmean(x):
    return jnp.mean(x, axis=-1, keepdims=True)


def _ffn_perm(j):
    return (j % 2) * 2 + j // 2


def _tile(n, pref):
    if n <= pref:
        return n
    t = pref - pref % 8
    while n % t:
        t -= 8
    return t


def mm_nn(a, w, *, name, bias=None, out_dtype=F32, perm=None, tm=1024, after=None):
    T, K = a.shape
    J, _, nb = w.shape
    tm = min(tm, T)
    col = (lambda j: j) if perm is None else perm

    def body(a_ref, w_ref, *rest):
        acc = _dot(a_ref[...], w_ref[...])
        if bias is not None:
            acc = acc + rest[0][...]
        rest[-1][...] = acc.astype(out_dtype)

    in_specs = [pl.BlockSpec((tm, K), lambda j, i: (i, 0)), pl.BlockSpec((None, K, nb), lambda j, i: (j, 0, 0))]
    args = [a, w]
    if bias is not None:
        in_specs.append(pl.BlockSpec((1, nb), lambda j, i: (0, j)))
        args.append(bias)
    if after is not None:
        in_specs.append(pl.BlockSpec(memory_space=pl.ANY))
        args.append(after)
    return pl.pallas_call(
        body, grid=(J, T // tm), in_specs=in_specs,
        out_specs=pl.BlockSpec((tm, nb), lambda j, i: (i, col(j))),
        out_shape=jax.ShapeDtypeStruct((T, J * nb), out_dtype), name=name,
        compiler_params=_cp("parallel", "parallel"))(*args)


def mm_nt(a, w, *, name, out_dtype=F32, perm=None, tm=1024, after=None):
    T = a.shape[0]
    J, K, nb = w.shape
    tm = min(tm, T)
    col = (lambda j: j) if perm is None else perm
    deps = [] if after is None else [after]

    def body(a_ref, w_ref, *rest):
        o_ref, acc_ref = rest[len(deps):]
        j = pl.program_id(1)

        @pl.when(j == 0)
        def _():
            acc_ref[...] = jnp.zeros_like(acc_ref)

        acc_ref[...] += _dot_nt(a_ref[...], w_ref[...])

        @pl.when(j == J - 1)
        def _():
            o_ref[...] = acc_ref[...].astype(out_dtype)

    return pl.pallas_call(
        body, grid=(T // tm, J),
        in_specs=[pl.BlockSpec((tm, nb), lambda i, j: (i, col(j))), pl.BlockSpec((None, K, nb), lambda i, j: (j, 0, 0))]
        + [pl.BlockSpec(memory_space=pl.ANY)] * len(deps),
        out_specs=pl.BlockSpec((tm, K), lambda i, j: (i, 0)),
        out_shape=jax.ShapeDtypeStruct((T, K), out_dtype),
        scratch_shapes=[pltpu.VMEM((tm, K), F32)], name=name,
        compiler_params=_cp("parallel", "arbitrary"))(a, w, *deps)


def mm_tn(a, b, *, name, J, block, row_chips=1, col_chips=1, perm=None, tk=1024):
    T = a.shape[0]
    tk = min(tk, T)
    col = (lambda j: j) if perm is None else perm
    if block == "b":
        rows, nb = a.shape[1], b.shape[1] // J
        a_spec = pl.BlockSpec((tk, rows), lambda j, t: (t, 0))
        b_spec = pl.BlockSpec((tk, nb), lambda j, t: (t, col(j)))
    else:
        rows, nb = a.shape[1] // J, b.shape[1]
        a_spec = pl.BlockSpec((tk, rows), lambda j, t: (t, col(j)))
        b_spec = pl.BlockSpec((tk, nb), lambda j, t: (t, 0))
    rh = rows // (2 * row_chips)
    nc = nb // col_chips
    chips = [(rc, cc) for rc in range(row_chips) for cc in range(col_chips)]

    def body(a_ref, b_ref, o_ref):
        @pl.when(pl.program_id(1) == 0)
        def _():
            o_ref[...] = jnp.zeros_like(o_ref)

        acc = _dot_tn(a_ref[...], b_ref[...])
        for ch, (rc, cc) in enumerate(chips):
            for hf in range(2):
                r0 = (rc * 2 + hf) * rh
                o_ref[hf, ch] += acc[r0:r0 + rh, cc * nc:(cc + 1) * nc]

    return pl.pallas_call(
        body, grid=(J, T // tk), in_specs=[a_spec, b_spec],
        out_specs=pl.BlockSpec((2, len(chips), rh, nc), lambda j, t: (0, j, 0, 0)),
        out_shape=jax.ShapeDtypeStruct((2, J * len(chips), rh, nc), F32), name=name,
        compiler_params=_cp("parallel", "arbitrary"))(a, b)


def mm_nt_parts(s3, g, w, *, name, tm=1024, after=None):
    n3, T, nb = s3.shape
    J, K, _ = w.shape
    tm = min(tm, T)
    deps = [] if after is None else [after]

    def body(s_ref, g_ref, w_ref, *rest):
        o_ref, acc_ref = rest[len(deps):]
        j = pl.program_id(1)

        @pl.when(j == 0)
        def _():
            acc_ref[...] = jnp.zeros_like(acc_ref)

        @pl.when(j < n3)
        def _():
            acc_ref[...] += _dot_nt(s_ref[...], w_ref[...])

        @pl.when(j == n3)
        def _():
            acc_ref[...] += _dot_nt(g_ref[...], w_ref[...])

        @pl.when(j == J - 1)
        def _():
            o_ref[...] = acc_ref[...]

    return pl.pallas_call(
        body, grid=(T // tm, J),
        in_specs=[pl.BlockSpec((None, tm, nb), lambda i, j: (jnp.minimum(j, n3 - 1), i, 0)),
                  pl.BlockSpec((tm, nb), lambda i, j: (i, 0)), pl.BlockSpec((None, K, nb), lambda i, j: (j, 0, 0))]
        + [pl.BlockSpec(memory_space=pl.ANY)] * len(deps),
        out_specs=pl.BlockSpec((tm, K), lambda i, j: (i, 0)), out_shape=jax.ShapeDtypeStruct((T, K), F32),
        scratch_shapes=[pltpu.VMEM((tm, K), F32)], name=name,
        compiler_params=_cp("parallel", "arbitrary"))(s3, g, w, *deps)


def mm_tn_parts(a, s3, g, *, name, tk=1024):
    n3, T, nb = s3.shape
    J = n3 + 1
    tk = min(tk, T)
    rows = a.shape[1]
    rh = rows // 2

    def body(a_ref, s_ref, g_ref, o_ref):
        j = pl.program_id(0)

        @pl.when(pl.program_id(1) == 0)
        def _():
            o_ref[...] = jnp.zeros_like(o_ref)

        def add(b_ref):
            acc = _dot_tn(a_ref[...], b_ref[...])
            for hf in range(2):
                o_ref[hf, 0] += acc[hf * rh:(hf + 1) * rh, :]

        pl.when(j < n3)(lambda: add(s_ref))
        pl.when(j == n3)(lambda: add(g_ref))

    return pl.pallas_call(
        body, grid=(J, T // tk),
        in_specs=[pl.BlockSpec((tk, rows), lambda j, t: (t, 0)),
                  pl.BlockSpec((None, tk, nb), lambda j, t: (jnp.minimum(j, n3 - 1), t, 0)),
                  pl.BlockSpec((tk, nb), lambda j, t: (t, 0))],
        out_specs=pl.BlockSpec((2, 1, rh, nb), lambda j, t: (0, j, 0, 0)),
        out_shape=jax.ShapeDtypeStruct((2, J, rh, nb), F32), name=name,
        compiler_params=_cp("parallel", "arbitrary"))(a, s3, g)


def _row(tm, w):
    return pl.BlockSpec((tm, w), lambda i: (i, 0))


def _full(r, w):
    return pl.BlockSpec((r, w), lambda i: (0, 0))


def _acc_init(i, *refs):
    @pl.when(i == 0)
    def _():
        for r in refs:
            r[...] = jnp.zeros_like(r)


def prenorm(x, g, sc, sh, *, name, tm=512):
    T, D = x.shape
    tm = min(tm, T)

    def body(x_ref, g_ref, sc_ref, sh_ref, h_ref):
        xv = x_ref[...]
        r = lax.rsqrt(_rowmean(xv * xv) + EPS)
        h_ref[...] = ((xv * r) * g_ref[...] * (1.0 + sc_ref[...]) + sh_ref[...]).astype(BF16)

    return pl.pallas_call(
        body, grid=(T // tm,), in_specs=[_row(tm, D), _full(1, D), _full(1, D), _full(1, D)],
        out_specs=_row(tm, D), out_shape=jax.ShapeDtypeStruct((T, D), BF16), name=name,
        compiler_params=_cp("parallel"))(x, g, sc, sh)


def post_residual_prenorm(x, y, g, gate, g2, sc2, sh2, *, name, tm=512):
    T, D = x.shape
    tm = min(tm, T)

    def body(x_ref, y_ref, g_ref, gate_ref, g2_ref, sc2_ref, sh2_ref, o_ref, h_ref):
        yv = y_ref[...]
        r = lax.rsqrt(_rowmean(yv * yv) + EPS)
        out = x_ref[...] + gate_ref[...] * ((yv * r) * g_ref[...])
        o_ref[...] = out
        r2 = lax.rsqrt(_rowmean(out * out) + EPS)
        h_ref[...] = ((out * r2) * g2_ref[...] * (1.0 + sc2_ref[...]) + sh2_ref[...]).astype(BF16)

    return pl.pallas_call(
        body, grid=(T // tm,), in_specs=[_row(tm, D), _row(tm, D)] + [_full(1, D)] * 5,
        out_specs=[_row(tm, D), _row(tm, D)],
        out_shape=[jax.ShapeDtypeStruct((T, D), F32), jax.ShapeDtypeStruct((T, D), BF16)], name=name,
        compiler_params=_cp("parallel"))(x, y, g, gate, g2, sc2, sh2)


def post_residual_loss(x, y, g, gate, tgt, *, name, tm=512):
    T, D = x.shape
    tm = min(tm, T)

    def body(x_ref, y_ref, g_ref, gate_ref, t_ref, dx_ref, l_ref, dy_ref, dgate_ref, dg_ref):
        _acc_init(pl.program_id(0), l_ref, dgate_ref, dg_ref)
        yv = y_ref[...]
        r = lax.rsqrt(_rowmean(yv * yv) + EPS)
        yn = yv * r
        gv = g_ref[...]
        gt = gate_ref[...]
        e = x_ref[...] + gt * (yn * gv) - t_ref[...]
        dxv = e * (1.0 / D)
        dx_ref[...] = dxv
        l_ref[...] += _colsum(e * e)
        dgate_ref[...] += _colsum(dxv * (yn * gv))
        dg_ref[...] += _colsum(dxv * gt * yn)
        dyn = dxv * gt * gv
        dy_ref[...] = (r * (dyn - yn * _rowmean(dyn * yn))).astype(BF16)

    vec = jax.ShapeDtypeStruct((1, D), F32)
    return pl.pallas_call(
        body, grid=(T // tm,), in_specs=[_row(tm, D), _row(tm, D), _full(1, D), _full(1, D), _row(tm, D)],
        out_specs=[_row(tm, D), _full(1, D), _row(tm, D), _full(1, D), _full(1, D)],
        out_shape=[jax.ShapeDtypeStruct((T, D), F32), vec, jax.ShapeDtypeStruct((T, D), BF16), vec, vec], name=name,
        compiler_params=_cp("arbitrary"))(x, y, g, gate, tgt)


def prenorm_bwd(dh, x, dres, g, sc, *, name, tm=512):
    T, D = x.shape
    tm = min(tm, T)

    def body(dh_ref, x_ref, dres_ref, g_ref, sc_ref, dx_ref, dsh_ref, dsc_ref, dg_ref):
        _acc_init(pl.program_id(0), dsh_ref, dsc_ref, dg_ref)
        xv = x_ref[...]
        dhv = dh_ref[...]
        r = lax.rsqrt(_rowmean(xv * xv) + EPS)
        xn = xv * r
        gv = g_ref[...]
        one_sc = 1.0 + sc_ref[...]
        dsh_ref[...] += _colsum(dhv)
        dsc_ref[...] += _colsum(dhv * (xn * gv))
        dg_ref[...] += _colsum(dhv * one_sc * xn)
        dxn = dhv * one_sc * gv
        dx_ref[...] = dres_ref[...] + r * (dxn - xn * _rowmean(dxn * xn))

    return pl.pallas_call(
        body, grid=(T // tm,), in_specs=[_row(tm, D), _row(tm, D), _row(tm, D), _full(1, D), _full(1, D)],
        out_specs=[_row(tm, D), _full(1, D), _full(1, D), _full(1, D)],
        out_shape=[jax.ShapeDtypeStruct((T, D), F32)] + [jax.ShapeDtypeStruct((1, D), F32)] * 3, name=name,
        compiler_params=_cp("arbitrary"))(dh, x, dres, g, sc)


def prenorm_post_bwd(dh, x, dres, g, sc, y, g_post, gate, *, name, tm=512):
    T, D = x.shape
    tm = min(tm, T)

    def body(dh_ref, x_ref, dres_ref, g_ref, sc_ref, y_ref, gp_ref, gate_ref,
             dx_ref, dsh_ref, dsc_ref, dg_ref, dy_ref, dgate_ref, dgp_ref, dbias_ref):
        _acc_init(pl.program_id(0), dsh_ref, dsc_ref, dg_ref, dgate_ref, dgp_ref, dbias_ref)
        xv = x_ref[...]
        dhv = dh_ref[...]
        r = lax.rsqrt(_rowmean(xv * xv) + EPS)
        xn = xv * r
        gv = g_ref[...]
        one_sc = 1.0 + sc_ref[...]
        dsh_ref[...] += _colsum(dhv)
        dsc_ref[...] += _colsum(dhv * (xn * gv))
        dg_ref[...] += _colsum(dhv * one_sc * xn)
        dxn = dhv * one_sc * gv
        dxv = dres_ref[...] + r * (dxn - xn * _rowmean(dxn * xn))
        dx_ref[...] = dxv
        yv = y_ref[...]
        ry = lax.rsqrt(_rowmean(yv * yv) + EPS)
        yn = yv * ry
        gp = gp_ref[...]
        gt = gate_ref[...]
        dgate_ref[...] += _colsum(dxv * (yn * gp))
        dgp_ref[...] += _colsum(dxv * gt * yn)
        dyn = dxv * gt * gp
        dy = ry * (dyn - yn * _rowmean(dyn * yn))
        dbias_ref[...] += _colsum(dy)
        dy_ref[...] = dy.astype(BF16)

    vec = jax.ShapeDtypeStruct((1, D), F32)
    return pl.pallas_call(
        body, grid=(T // tm,),
        in_specs=[_row(tm, D)] * 3 + [_full(1, D)] * 2 + [_row(tm, D)] + [_full(1, D)] * 2,
        out_specs=[_row(tm, D)] + [_full(1, D)] * 3 + [_row(tm, D)] + [_full(1, D)] * 3,
        out_shape=[jax.ShapeDtypeStruct((T, D), F32), vec, vec, vec, jax.ShapeDtypeStruct((T, D), BF16), vec, vec, vec],
        name=name, compiler_params=_cp("arbitrary"))(dh, x, dres, g, sc, y, g_post, gate)


HALO = 16


def _shift_helpers():
    rid = lax.broadcasted_iota(jnp.int32, (SUB, LANE), 0)

    def down(cur, prev, k):
        return pltpu.roll(jnp.where(rid >= SUB - k, prev, cur), k, 0)

    def up(cur, nxt, k):
        return pltpu.roll(jnp.where(rid < k, nxt, cur), SUB - k, 0)

    return down, up


def _ffn_sides(c, nb, wa_ref, wb_ref, ba_ref, bb_ref):
    cols = slice(c * LANE, (c + 1) * LANE)
    return [(cols, [wa_ref[k:k + 1, cols] for k in range(FFN_W)], ba_ref[:, cols]),
            (slice(nb + c * LANE, nb + (c + 1) * LANE), [wb_ref[k:k + 1, cols] for k in range(FFN_W)],
             bb_ref[:, cols])]


def _ffn_specs(tm, nb, hb, idx):
    return [pl.BlockSpec((tm, 2 * nb), lambda jc, i: (idx(i), jc)),
            pl.BlockSpec((HALO, 2 * nb), lambda jc, i: (jnp.maximum(idx(i) * hb - 1, 0), jc)),
            pl.BlockSpec((FFN_W, nb), lambda jc, i: (0, jc)),
            pl.BlockSpec((FFN_W, nb), lambda jc, i: (0, jc + 2)),
            pl.BlockSpec((1, nb), lambda jc, i: (0, jc)),
            pl.BlockSpec((1, nb), lambda jc, i: (0, jc + 2))]


def ffn_act(u0p, dw_w, dw_b, *, name, tm=256):
    T, W = u0p.shape
    nb = W // 4
    tm = min(tm, T)
    unroll = 4
    rows16 = 2 * SUB

    def body(u_ref, halo_ref, wa_ref, wb_ref, ba_ref, bb_ref, z_ref, ab_ref):
        i = pl.program_id(1)
        down, _ = _shift_helpers()
        for c in range(nb // LANE):
            cols = slice(c * LANE, (c + 1) * LANE)
            side = _ffn_sides(c, nb, wa_ref, wb_ref, ba_ref, bb_ref)

            def rows(j, prev):
                prev = list(prev)
                for m in range(unroll):
                    r0 = pl.multiple_of((j * unroll + m) * rows16, rows16)
                    x = [u_ref[pl.ds(r0, rows16), cs].astype(F32) for cs, _, _ in side]
                    conv = [[None, None], [None, None]]
                    for hf in range(2):
                        for n, (_, w, b) in enumerate(side):
                            cur = x[n][hf * SUB:(hf + 1) * SUB, :]
                            conv[n][hf] = b + w[2] * cur + w[1] * down(cur, prev[n], 1) + w[0] * down(cur, prev[n], 2)
                            prev[n] = cur
                    a, b = [jnp.concatenate(conv[n], axis=0) for n in range(2)]
                    z_ref[pl.ds(r0, rows16), cols] = (_silu(a) * b).astype(BF16)
                    ab_ref[pl.ds(r0, rows16), side[0][0]] = a.astype(BF16)
                    ab_ref[pl.ds(r0, rows16), side[1][0]] = b.astype(BF16)
                return tuple(prev)

            first = [jnp.where(i == 0, 0.0, halo_ref[:, cs].astype(F32)[SUB:2 * SUB, :]) for cs, _, _ in side]
            lax.fori_loop(0, tm // (rows16 * unroll), rows, tuple(first))

    return pl.pallas_call(
        body, grid=(2, T // tm), in_specs=_ffn_specs(tm, nb, tm // HALO, lambda i: i),
        out_specs=[pl.BlockSpec((tm, nb), lambda jc, i: (i, jc)), pl.BlockSpec((tm, 2 * nb), lambda jc, i: (i, jc))],
        out_shape=[jax.ShapeDtypeStruct((T, 2 * nb), BF16), jax.ShapeDtypeStruct((T, W), BF16)], name=name,
        compiler_params=_cp("parallel", "arbitrary"))(u0p, u0p, dw_w, dw_w, dw_b, dw_b)


def ffn_act_bwd(dz, u0p, ab, dw_w, *, name, tm=256):
    T, W = u0p.shape
    nb = W // 4
    tm = min(tm, T)
    nt = T // tm
    unroll = 4
    rows16 = 2 * SUB
    n_it = tm // (rows16 * unroll)

    def body(dz_ref, u_ref, ab_ref, wa_ref, wb_ref, du0_ref, dw_ref, carry):
        i = pl.program_id(1)
        _acc_init(i, dw_ref)
        _, up = _shift_helpers()
        for c in range(nb // LANE):
            cols = slice(c * LANE, (c + 1) * LANE)
            side = [(cols, [wa_ref[k:k + 1, cols] for k in range(FFN_W)]),
                    (slice(nb + c * LANE, nb + (c + 1) * LANE), [wb_ref[k:k + 1, cols] for k in range(FFN_W)])]

            def rows(j, st):
                nxt, acc = list(st[0:2]), list(st[2:10])
                for m in range(unroll):
                    r0 = pl.multiple_of(((n_it - 1 - j) * unroll + unroll - 1 - m) * rows16, rows16)
                    dzv = dz_ref[pl.ds(r0, rows16), cols].astype(F32)
                    a, b = [ab_ref[pl.ds(r0, rows16), cs].astype(F32) for cs, _ in side]
                    x = [u_ref[pl.ds(r0, rows16), cs].astype(F32) for cs, _ in side]
                    sa = _sig(a)
                    d16 = [dzv * b * (sa * (1.0 + a * (1.0 - sa))), dzv * (a * sa)]
                    out = [[None, None], [None, None]]
                    for hf in (1, 0):
                        half = slice(hf * SUB, (hf + 1) * SUB)
                        for n in range(2):
                            w = side[n][1]
                            d = d16[n][half, :]
                            u = x[n][half, :]
                            up1, up2 = up(d, nxt[n], 1), up(d, nxt[n], 2)
                            acc[4 * n + 0] = acc[4 * n + 0] + up2 * u
                            acc[4 * n + 1] = acc[4 * n + 1] + up1 * u
                            acc[4 * n + 2] = acc[4 * n + 2] + d * u
                            acc[4 * n + 3] = acc[4 * n + 3] + d
                            out[n][hf] = w[2] * d + w[1] * up1 + w[0] * up2
                            nxt[n] = d
                    for n in range(2):
                        du0_ref[pl.ds(r0, rows16), side[n][0]] = jnp.concatenate(out[n], axis=0).astype(BF16)
                return (*nxt, *acc)

            init = [jnp.where(i == 0, 0.0, carry[:, cs]) for cs, _ in side] + [jnp.zeros((SUB, LANE), F32)] * 8
            st = lax.fori_loop(0, n_it, rows, tuple(init))
            for n in range(2):
                carry[:, side[n][0]] = st[n]
                for k in range(4):
                    dw_ref[k, :, side[n][0]] += st[2 + 4 * n + k]

        @pl.when(i == nt - 1)
        def _():
            for k in range(4):
                dw_ref[k, 0:1, :] = _colsum(dw_ref[k])

    rev = lambda i: nt - 1 - i
    wide = pl.BlockSpec((tm, 2 * nb), lambda jc, i: (rev(i), jc))
    return pl.pallas_call(
        body, grid=(2, nt),
        in_specs=[pl.BlockSpec((tm, nb), lambda jc, i: (rev(i), jc)), wide, wide,
                  pl.BlockSpec((FFN_W, nb), lambda jc, i: (0, jc)), pl.BlockSpec((FFN_W, nb), lambda jc, i: (0, jc + 2))],
        out_specs=[wide, pl.BlockSpec((4, SUB, 2 * nb), lambda jc, i: (0, 0, jc))],
        out_shape=[jax.ShapeDtypeStruct((T, W), BF16), jax.ShapeDtypeStruct((4, SUB, W), F32)],
        scratch_shapes=[pltpu.VMEM((SUB, 2 * nb), F32)], name=name,
        compiler_params=_cp("parallel", "arbitrary"))(dz, u0p, ab, dw_w, dw_w)


CHALO = 32
CCOL = 256


def _phase_copies(buf, shifted, tm):
    n = tm + CHALO - SUB
    for p in range(1, SUB):
        shifted[p - 1, 0:n, :] = buf[p:p + n, :]


def _shifted(buf, shifted, r, tm, c0):
    m, p = divmod(r, SUB)
    src = buf if p == 0 else shifted.at[p - 1]
    return src[m * SUB:m * SUB + tm, c0:c0 + CCOL]


def conv_act(u, dw_w, dw_b, ln_g, ln_b, *, name, tm=128):
    T, D2 = u.shape
    D = D2 // 2
    tm = min(tm, T)
    hb = tm // CHALO

    def body(u_ref, halo_ref, w_ref, b_ref, g_ref, be_ref, s_ref, cv_ref, gbuf, gsh):
        i = pl.program_id(0)
        hv = halo_ref[...]
        gbuf[0:CHALO, :] = jnp.where(i == 0, 0.0, hv[:, 0:D] * _sig(hv[:, D:D2]))
        uv = u_ref[...]
        gbuf[CHALO:CHALO + tm, :] = uv[:, 0:D] * _sig(uv[:, D:D2])
        _phase_copies(gbuf, gsh, tm)
        for c0 in range(0, D, CCOL):
            acc = jnp.zeros((tm, CCOL), F32) + b_ref[:, c0:c0 + CCOL]
            for k in range(CONV_W):
                acc = acc + w_ref[k:k + 1, c0:c0 + CCOL] * _shifted(gbuf, gsh, CHALO - (CONV_W - 1) + k, tm, c0)
            cv_ref[:, c0:c0 + CCOL] = acc
        cv = cv_ref[...]
        mu = _rowmean(cv)
        xc = cv - mu
        nh = xc * lax.rsqrt(_rowmean(xc * xc) + EPS)
        s_ref[...] = _silu(nh * g_ref[...] + be_ref[...]).astype(BF16)

    return pl.pallas_call(
        body, grid=(T // tm,),
        in_specs=[_row(tm, D2), pl.BlockSpec((CHALO, D2), lambda i: (jnp.maximum(i * hb - 1, 0), 0)),
                  _full(CONV_W, D), _full(1, D), _full(1, D), _full(1, D)],
        out_specs=[_row(tm, D), _row(tm, D)],
        out_shape=[jax.ShapeDtypeStruct((T, D), BF16), jax.ShapeDtypeStruct((T, D), F32)],
        scratch_shapes=[pltpu.VMEM((tm + CHALO, D), F32), pltpu.VMEM((SUB - 1, tm + CHALO, D), F32)], name=name,
        compiler_params=_cp("arbitrary"))(u, u, dw_w, dw_b, ln_g, ln_b)


def conv_norm_bwd(ds, cv, ln_g, ln_b, *, name, tm=512):
    T, D = cv.shape
    tm = min(tm, T)

    def body(ds_ref, cv_ref, g_ref, be_ref, dcv_ref, dg_ref, dbe_ref, dcb_ref):
        _acc_init(pl.program_id(0), dg_ref, dbe_ref, dcb_ref)
        cv_ = cv_ref[...]
        mu = _rowmean(cv_)
        xc = cv_ - mu
        rstd = lax.rsqrt(_rowmean(xc * xc) + EPS)
        nh = xc * rstd
        gv = g_ref[...]
        dln = ds_ref[...] * _dsilu(nh * gv + be_ref[...])
        dg_ref[...] += _colsum(dln * nh)
        dbe_ref[...] += _colsum(dln)
        dnh = dln * gv
        dcv = rstd * (dnh - _rowmean(dnh) - nh * _rowmean(dnh * nh))
        dcb_ref[...] += _colsum(dcv)
        dcv_ref[...] = dcv

    return pl.pallas_call(
        body, grid=(T // tm,), in_specs=[_row(tm, D), _row(tm, D), _full(1, D), _full(1, D)],
        out_specs=[_row(tm, D), _full(1, D), _full(1, D), _full(1, D)],
        out_shape=[jax.ShapeDtypeStruct((T, D), F32)] + [jax.ShapeDtypeStruct((1, D), F32)] * 3, name=name,
        compiler_params=_cp("arbitrary"))(ds, cv, ln_g, ln_b)


def conv_glu_bwd(dcv, u, dw_w, *, name, tm=128):
    T, D2 = u.shape
    D = D2 // 2
    tm = min(tm, T)
    nt = T // tm
    hb = tm // CHALO

    def body(dcv_ref, dnext_ref, u_ref, w_ref, du_ref, dw_ref, dbin_ref, dbuf, dsh):
        i = pl.program_id(0)
        _acc_init(i, dw_ref, dbin_ref)
        uv = u_ref[...]
        av = uv[:, 0:D]
        sg = _sig(uv[:, D:D2])
        glu = av * sg
        dbuf[0:tm, :] = dcv_ref[...]
        dbuf[tm:tm + CHALO, :] = jnp.where(i == nt - 1, 0.0, dnext_ref[...])
        _phase_copies(dbuf, dsh, tm)
        for c0 in range(0, D, CCOL):
            glu_c = glu[:, c0:c0 + CCOL]
            acc = jnp.zeros((tm, CCOL), F32)
            for k in range(CONV_W):
                moved = _shifted(dbuf, dsh, CONV_W - 1 - k, tm, c0)
                dw_ref[k:k + 1, c0:c0 + CCOL] += _colsum(moved * glu_c)
                acc = acc + w_ref[k:k + 1, c0:c0 + CCOL] * moved
            a_c = av[:, c0:c0 + CCOL]
            s_c = sg[:, c0:c0 + CCOL]
            da = acc * s_c
            dgt = acc * a_c * s_c * (1.0 - s_c)
            dbin_ref[:, c0:c0 + CCOL] += _colsum(da)
            dbin_ref[:, D + c0:D + c0 + CCOL] += _colsum(dgt)
            du_ref[:, c0:c0 + CCOL] = da.astype(BF16)
            du_ref[:, D + c0:D + c0 + CCOL] = dgt.astype(BF16)

    return pl.pallas_call(
        body, grid=(nt,),
        in_specs=[_row(tm, D), pl.BlockSpec((CHALO, D), lambda i: (jnp.minimum((i + 1) * hb, T // CHALO - 1), 0)),
                  _row(tm, D2), _full(CONV_W, D)],
        out_specs=[_row(tm, D2), _full(CHALO, D), _full(1, D2)],
        out_shape=[jax.ShapeDtypeStruct((T, D2), BF16), jax.ShapeDtypeStruct((CHALO, D), F32),
                   jax.ShapeDtypeStruct((1, D2), F32)],
        scratch_shapes=[pltpu.VMEM((tm + CHALO, D), F32), pltpu.VMEM((SUB - 1, tm + CHALO, D), F32)],
        name=name, compiler_params=_cp("arbitrary"))(dcv, dcv, u, dw_w)


HB = 8


def _lb0(lg_ref):
    l0, l1, l2 = lg_ref[0:1, :], lg_ref[1:2, :], lg_ref[2:3, :]
    m = jnp.maximum(jnp.maximum(l0, l1), l2)
    e0 = jnp.exp(l0 - m)
    return e0 / (e0 + jnp.exp(l1 - m) + jnp.exp(l2 - m))


def _mm_exact(m01, x):
    hi = x.astype(BF16)
    r1 = x - hi.astype(F32)
    mid = r1.astype(BF16)
    lo = (r1 - mid.astype(F32)).astype(BF16)
    return _dot(m01, hi) + _dot(m01, mid) + _dot(m01, lo)


def _block_tri(tm):
    r = jnp.arange(tm)[:, None]
    c = jnp.arange(tm)[None, :]
    same = (r // BLK) == (c // BLK)
    return (same & (c <= r)).astype(BF16), (same & (c >= r)).astype(BF16)


def _halves(x):
    return [x[0:SUB, :], x[SUB:BLK, :]]


def _live_halves(s):
    return ([(0, s)] if s < SUB else []) + [(1, max(s - SUB, 0))]


def _const_spec(shape):
    return pl.BlockSpec(shape, lambda h, i: (0, 0))


def _hgrn_specs(H, hb, tm, idx):
    g = H // hb
    return [pl.BlockSpec((tm, hb * HEAD), lambda h, i: (idx(i), h)),
            pl.BlockSpec((tm, hb * HEAD), lambda h, i: (idx(i), g + h)),
            pl.BlockSpec((tm, hb * HEAD), lambda h, i: (idx(i), 2 * g + h)),
            pl.BlockSpec((3, hb * HEAD), lambda h, i: (0, h))]


def hgrn_scan(proj, lb_logits, *, name, tm=128):
    T = proj.shape[0]
    H = proj.shape[1] // (4 * HEAD)
    hb = min(HB, H)
    tm = min(tm, T)
    nt = T // tm
    nblk = tm // BLK
    tril, _ = _block_tri(tm)
    heads = [slice(hh * HEAD, (hh + 1) * HEAD) for hh in range(hb)]

    def body(qp_ref, fz_ref, v_ref, lg_ref, tril_ref, o_ref, st_ref, S_ref, q_s, k_s, b_s):
        @pl.when(pl.program_id(1) == 0)
        def _():
            S_ref[...] = jnp.zeros_like(S_ref)

        st_ref[...] = S_ref[...]
        lb = _lb0(lg_ref)
        f = lb + (1.0 - lb) * _sig(fz_ref[...])
        q_s[...] = _silu(qp_ref[...])
        k_s[...] = 1.0 - f
        b_s[...] = _mm_exact(tril_ref[...], jnp.log(f))
        rows = lax.broadcasted_iota(jnp.int32, (BLK, HEAD), 0)
        S = [S_ref[hh] for hh in range(hb)]
        for nb in range(nblk):
            blk = slice(nb * BLK, (nb + 1) * BLK)
            last = slice(nb * BLK + BLK - 1, nb * BLK + BLK)
            qb = [q_s[blk, c] for c in heads]
            bb = [b_s[blk, c] for c in heads]
            o = [_dot_nt((qb[hh] * jnp.exp(bb[hh])).astype(BF16), S[hh].astype(BF16)) for hh in range(hb)]
            for hh, c in enumerate(heads):
                bc = b_s[last, c]
                kd = k_s[blk, c] * jnp.exp(bc - bb[hh])
                S[hh] = S[hh] * jnp.exp(bc) + _dot_tn(v_ref[blk, c].astype(BF16), kd.astype(BF16))
            for s in range(BLK):
                r = slice(nb * BLK + s, nb * BLK + s + 1)
                for hh, c in enumerate(heads):
                    dec = jnp.exp(jnp.where(rows >= s, bb[hh] - b_s[r, c], NEG))
                    a = jnp.sum(qb[hh] * k_s[r, c] * dec, axis=-1, keepdims=True)
                    o[hh] = o[hh] + a * v_ref[r, c]
            for hh, c in enumerate(heads):
                o_ref[blk, c] = o[hh]
        for hh in range(hb):
            S_ref[hh] = S[hh]

    return pl.pallas_call(
        body, grid=(H // hb, nt),
        in_specs=_hgrn_specs(H, hb, tm, lambda i: i) + [_const_spec((tm, tm))],
        out_specs=[pl.BlockSpec((tm, hb * HEAD), lambda h, i: (i, h)),
                   pl.BlockSpec((None, hb, HEAD, HEAD), lambda h, i: (i, h, 0, 0))],
        out_shape=[jax.ShapeDtypeStruct((T, H * HEAD), F32), jax.ShapeDtypeStruct((nt, H, HEAD, HEAD), F32)],
        scratch_shapes=[pltpu.VMEM((hb, HEAD, HEAD), F32)] + [pltpu.VMEM((tm, hb * HEAD), F32)] * 3, name=name,
        compiler_params=_cp("parallel", "arbitrary"))(proj, proj, proj, lb_logits, tril)


def hgrn_scan_bwd(proj, lb_logits, states, do, *, name, tm=128):
    T = proj.shape[0]
    H = proj.shape[1] // (4 * HEAD)
    hb = min(HB, H)
    tm = min(tm, T)
    nt = T // tm
    nblk = tm // BLK
    tril, triu = _block_tri(tm)
    sel = (jnp.arange(BLK * SUB)[None, :] // SUB == jnp.arange(BLK)[:, None]).astype(BF16)
    heads = [slice(hh * HEAD, (hh + 1) * HEAD) for hh in range(hb)]

    def body(qp_ref, fz_ref, v_ref, lg_ref, st_ref, do_ref, tril_ref, triu_ref, sel_ref, d3_ref, dlb_ref,
             dS_ref, Sb_ref, q_s, k_s, b_s, dq_s, dk_s, dv_s, db_s, pk_s, pv_s):
        i = pl.program_id(1)

        @pl.when(i == 0)
        def _():
            dS_ref[...] = jnp.zeros_like(dS_ref)
            dlb_ref[...] = jnp.zeros_like(dlb_ref)

        lb = _lb0(lg_ref)
        qp = qp_ref[...]
        sg = _sig(fz_ref[...])
        f = lb + (1.0 - lb) * sg
        q_s[...] = _silu(qp)
        k_s[...] = 1.0 - f
        b_s[...] = _mm_exact(tril_ref[...], jnp.log(f))
        rows = lax.broadcasted_iota(jnp.int32, (SUB, HEAD), 0)
        rows1 = lax.broadcasted_iota(jnp.int32, (SUB, 1), 0)

        S = [st_ref[hh] for hh in range(hb)]
        for nb in range(nblk):
            blk = slice(nb * BLK, (nb + 1) * BLK)
            last = slice(nb * BLK + BLK - 1, nb * BLK + BLK)
            for hh, c in enumerate(heads):
                Sb_ref[nb * hb + hh] = S[hh]
                if nb < nblk - 1:
                    bc = b_s[last, c]
                    kd = k_s[blk, c] * jnp.exp(bc - b_s[blk, c])
                    S[hh] = S[hh] * jnp.exp(bc) + _dot_tn(v_ref[blk, c].astype(BF16), kd.astype(BF16))

        dS = [dS_ref[hh] for hh in range(hb)]
        for nb in reversed(range(nblk)):
            blk = slice(nb * BLK, (nb + 1) * BLK)
            last = slice(nb * BLK + BLK - 1, nb * BLK + BLK)
            qb, kb, bb, dob, dq, dbc, ebc = [], [], [], [], [], [], []
            for hh, c in enumerate(heads):
                S0 = Sb_ref[nb * hb + hh]
                qb.append(q_s[blk, c])
                kb.append(k_s[blk, c])
                bb.append(b_s[blk, c])
                dob.append(do_ref[blk, c])
                bc = b_s[last, c]
                eb = jnp.exp(bb[hh])
                ekd = jnp.exp(bc - bb[hh])
                ebc.append(jnp.exp(bc))
                dS16 = dS[hh].astype(BF16)
                dob16 = dob[hh].astype(BF16)
                dq.append(_dot(dob16, S0.astype(BF16)) * eb)
                dki = _dot(v_ref[blk, c].astype(BF16), dS16) * ekd
                dk_s[blk, c] = dki
                dv_s[blk, c] = _dot_nt((kb[hh] * ekd).astype(BF16), dS16)
                dbc.append(_colsum(dS[hh] * S0) * ebc[hh] + _colsum(kb[hh] * dki))
                dS[hh] = dS[hh] * ebc[hh] + _dot_tn(dob16, (qb[hh] * eb).astype(BF16))
            qh, bh, doh, dqh = [[_halves(t[hh]) for hh in range(hb)] for t in (qb, bb, dob, dq)]
            for s in range(BLK):
                r = slice(nb * BLK + s, nb * BLK + s + 1)
                for hh, c in enumerate(heads):
                    ks = k_s[r, c]
                    pk, pv = None, None
                    for hf, lo in _live_halves(s):
                        diff = bh[hh][hf] - b_s[r, c]
                        dec = jnp.exp(diff if lo == 0 else jnp.where(rows >= lo, diff, NEG))
                        w = qh[hh][hf] * dec
                        a = jnp.sum(w * ks, axis=-1, keepdims=True)
                        da = jnp.sum(doh[hh][hf] * v_ref[r, c], axis=-1, keepdims=True)
                        if lo:
                            da = jnp.where(rows1 >= lo, da, 0.0)
                        dqh[hh][hf] = dqh[hh][hf] + (da * ks) * dec
                        pk = da * w if pk is None else pk + da * w
                        pv = a * doh[hh][hf] if pv is None else pv + a * doh[hh][hf]
                    pk_s[hh, s * SUB:(s + 1) * SUB, :] = pk
                    pv_s[hh, s * SUB:(s + 1) * SUB, :] = pv
            for hh, c in enumerate(heads):
                khi, klo = _split2(pk_s[hh])
                dk_s[blk, c] += _dot(sel_ref[...], khi) + _dot(sel_ref[...], klo)
                dv_s[blk, c] += _dot(sel_ref[...], pv_s[hh].astype(BF16))
                dq[hh] = jnp.concatenate(dqh[hh], axis=0)
                dq_s[blk, c] = dq[hh]
                db_s[blk, c] = qb[hh] * dq[hh] - kb[hh] * dk_s[blk, c]
                db_s[last, c] += dbc[hh]
        for hh in range(hb):
            dS_ref[hh] = dS[hh]

        dlf = _mm_exact(triu_ref[...], db_s[...])
        df = dlf / f - dk_s[...]
        d3_ref[0] = (dq_s[...] * _dsilu(qp)).astype(BF16)
        d3_ref[1] = (df * (1.0 - lb) * sg * (1.0 - sg)).astype(BF16)
        d3_ref[2] = dv_s[...].astype(BF16)
        dlb_ref[...] += _colsum(df * (1.0 - sg))

    rev = lambda i: nt - 1 - i
    out_blk = pl.BlockSpec((tm, hb * HEAD), lambda h, i: (rev(i), h))
    return pl.pallas_call(
        body, grid=(H // hb, nt),
        in_specs=_hgrn_specs(H, hb, tm, rev) + [pl.BlockSpec((None, hb, HEAD, HEAD), lambda h, i: (rev(i), h, 0, 0)),
                                                out_blk, _const_spec((tm, tm)), _const_spec((tm, tm)),
                                                _const_spec((BLK, BLK * SUB))],
        out_specs=[pl.BlockSpec((3, tm, hb * HEAD), lambda h, i: (0, rev(i), h)),
                   pl.BlockSpec((1, hb * HEAD), lambda h, i: (0, h))],
        out_shape=[jax.ShapeDtypeStruct((3, T, H * HEAD), BF16), jax.ShapeDtypeStruct((1, H * HEAD), F32)],
        scratch_shapes=[pltpu.VMEM((hb, HEAD, HEAD), F32), pltpu.VMEM((nblk * hb, HEAD, HEAD), F32)]
        + [pltpu.VMEM((tm, hb * HEAD), F32)] * 7 + [pltpu.VMEM((hb, BLK * SUB, HEAD), F32)] * 2, name=name,
        compiler_params=_cp("parallel", "arbitrary"))(proj, proj, proj, lb_logits, states, do, tril, triu, sel)


def hgrn_gate(o, proj, gn, *, name, tm=512):
    T, D = o.shape
    H = D // HEAD
    tm = min(tm, T)

    def body(o_ref, gp_ref, gn_ref, og_ref):
        gn_ = gn_ref[...]
        for h in range(H):
            c = slice(h * HEAD, (h + 1) * HEAD)
            oh = o_ref[:, c]
            r = lax.rsqrt(_rowmean(oh * oh) + EPS)
            og_ref[:, c] = ((oh * r) * gn_ * _silu(gp_ref[:, c])).astype(BF16)

    return pl.pallas_call(
        body, grid=(T // tm,),
        in_specs=[_row(tm, D), pl.BlockSpec((tm, D), lambda i: (i, 3)), _full(1, HEAD)],
        out_specs=_row(tm, D), out_shape=jax.ShapeDtypeStruct((T, D), BF16), name=name,
        compiler_params=_cp("parallel"))(o, proj, gn)


def hgrn_gate_bwd(dog, o, proj, gn, *, name, tm=512):
    T, D = o.shape
    H = D // HEAD
    tm = min(tm, T)

    def body(dog_ref, o_ref, gp_ref, gn_ref, do_ref, dgp_ref, dgn_ref):
        _acc_init(pl.program_id(0), dgn_ref)
        gn_ = gn_ref[...]
        for h in range(H):
            c = slice(h * HEAD, (h + 1) * HEAD)
            oh = o_ref[:, c]
            gp = gp_ref[:, c]
            dg = dog_ref[:, c]
            r = lax.rsqrt(_rowmean(oh * oh) + EPS)
            on = oh * r
            dgp_ref[:, c] = (dg * (on * gn_) * _dsilu(gp)).astype(BF16)
            don = dg * _silu(gp)
            dgn_ref[...] += _colsum(don * on)
            dn = don * gn_
            do_ref[:, c] = r * (dn - on * _rowmean(dn * on))

    return pl.pallas_call(
        body, grid=(T // tm,),
        in_specs=[_row(tm, D), _row(tm, D), pl.BlockSpec((tm, D), lambda i: (i, 3)), _full(1, HEAD)],
        out_specs=[_row(tm, D), _row(tm, D), _full(1, HEAD)],
        out_shape=[jax.ShapeDtypeStruct((T, D), F32), jax.ShapeDtypeStruct((T, D), BF16),
                   jax.ShapeDtypeStruct((1, HEAD), F32)], name=name,
        compiler_params=_cp("arbitrary"))(dog, o, proj, gn)


def _split2(x):
    hi = x.astype(BF16)
    return hi, (x - hi.astype(F32)).astype(BF16)


def ada_mod(c_all, ada_w, *, name):
    L, D, N = ada_w.shape
    B = c_all.shape[0]

    def body(c_ref, w_ref, o_ref):
        chi, clo = _split2(_silu(c_ref[...]))
        whi, wlo = _split2(w_ref[...])
        o_ref[...] = _dot(chi, whi) + _dot(chi, wlo) + _dot(clo, whi)

    return pl.pallas_call(
        body, grid=(L,), in_specs=[_full(B, D), pl.BlockSpec((None, D, N), lambda l: (l, 0, 0))],
        out_specs=pl.BlockSpec((None, B, N), lambda l: (l, 0, 0)),
        out_shape=jax.ShapeDtypeStruct((L, B, N), F32), name=name, compiler_params=_cp("parallel"))(c_all, ada_w)


def ada_wgrad(c_all_t, dmod, *, name, tr=256):
    D, B = c_all_t.shape
    L, _, N = dmod.shape
    tr = min(tr, D)

    def body(c_ref, d_ref, o_ref):
        cond = _silu(c_ref[...])
        acc = cond[:, 0:1] * d_ref[0:1, :]
        for b in range(1, B):
            acc = acc + cond[:, b:b + 1] * d_ref[b:b + 1, :]
        o_ref[...] = acc

    return pl.pallas_call(
        body, grid=(L, D // tr),
        in_specs=[pl.BlockSpec((tr, B), lambda l, r: (r, 0)), pl.BlockSpec((None, B, N), lambda l, r: (l, 0, 0))],
        out_specs=pl.BlockSpec((None, tr, N), lambda l, r: (l, r, 0)),
        out_shape=jax.ShapeDtypeStruct((L, D, N), F32), name=name,
        compiler_params=_cp("parallel", "parallel"))(c_all_t, dmod)


def sum_devices(parts, *, name):
    n, R, C = parts.shape

    def body(p_ref, o_ref):
        acc = p_ref[0]
        for d in range(1, n):
            acc = acc + p_ref[d]
        o_ref[...] = acc

    return pl.pallas_call(body, in_specs=[VMEM_SPEC], out_specs=VMEM_SPEC,
                          out_shape=jax.ShapeDtypeStruct((R, C), F32), name=name)(parts)


def lb_logits_grad(lb_logits, dlb, *, name):
    def body(lg_ref, d_ref, o_ref):
        l0, l1, l2 = lg_ref[0:1, :], lg_ref[1:2, :], lg_ref[2:3, :]
        m = jnp.maximum(jnp.maximum(l0, l1), l2)
        e0, e1, e2 = jnp.exp(l0 - m), jnp.exp(l1 - m), jnp.exp(l2 - m)
        z = e0 + e1 + e2
        p0, p1, p2 = e0 / z, e1 / z, e2 / z
        g = d_ref[...] * p0
        o_ref[0:1, :] = g * (1.0 - p0)
        o_ref[1:2, :] = -g * p1
        o_ref[2:3, :] = -g * p2

    return pl.pallas_call(body, in_specs=[VMEM_SPEC, VMEM_SPEC], out_specs=VMEM_SPEC,
                          out_shape=jax.ShapeDtypeStruct(lb_logits.shape, F32), name=name)(lb_logits, dlb)


def adamw(w, g, m, v, *, name, tr=256, after=None):
    R, C = w.shape
    tr = _tile(R, tr)
    deps = [] if after is None else [after]

    def body(w_ref, g_ref, m_ref, v_ref, *rest):
        d_ref, nm_ref, nv_ref = rest[len(deps):]
        gv = g_ref[...]
        nm = ADAM_B1 * m_ref[...] + (1.0 - ADAM_B1) * gv
        nv = ADAM_B2 * v_ref[...] + (1.0 - ADAM_B2) * (gv * gv)
        m_hat = nm / (1.0 - ADAM_B1 ** ADAM_STEP)
        v_hat = nv / (1.0 - ADAM_B2 ** ADAM_STEP)
        d_ref[...] = -ADAM_LR * (m_hat / (jnp.sqrt(v_hat) + ADAM_EPS) + ADAM_WD * w_ref[...])
        nm_ref[...] = nm
        nv_ref[...] = nv

    spec = pl.BlockSpec((tr, C), lambda i: (i, 0))
    return pl.pallas_call(
        body, grid=(R // tr,), in_specs=[spec] * 4 + [pl.BlockSpec(memory_space=pl.ANY)] * len(deps), out_specs=[spec] * 3,
        out_shape=[jax.ShapeDtypeStruct((R, C), F32)] * 3, name=name,
        compiler_params=_cp("parallel"))(w, g, m, v, *deps)


def _place():
    return lax.axis_index("x"), lax.axis_index("y"), lax.axis_index("c")


def _flip(v, bit):
    return 1 - v if bit else v


def allgather_devices(v, *, name):
    R, C = v.shape

    def body(v_ref, out_ref, send_sems, recv_sems, local_sem):
        x, y, c = _place()
        me = 4 * x + 2 * y + c
        mine = pltpu.make_async_copy(v_ref, out_ref.at[me], local_sem)
        mine.start()
        sends = []
        for k in range(1, N_DEV):
            peer = (_flip(x, k & 4), _flip(y, k & 2), _flip(c, k & 1))
            cp = pltpu.make_async_remote_copy(src_ref=v_ref, dst_ref=out_ref.at[me], send_sem=send_sems.at[k - 1],
                                              recv_sem=recv_sems.at[k - 1], device_id=peer, device_id_type=MESH)
            cp.start()
            sends.append(cp)
        for k in range(1, N_DEV):
            px, py, pc = _flip(x, k & 4), _flip(y, k & 2), _flip(c, k & 1)
            pltpu.make_async_remote_copy(src_ref=v_ref, dst_ref=out_ref.at[4 * px + 2 * py + pc],
                                         send_sem=send_sems.at[k - 1], recv_sem=recv_sems.at[k - 1],
                                         device_id=(px, py, pc), device_id_type=MESH).wait_recv()
        for cp in sends:
            cp.wait_send()
        mine.wait()

    return pl.pallas_call(
        body, in_specs=[VMEM_SPEC], out_specs=VMEM_SPEC, out_shape=jax.ShapeDtypeStruct((N_DEV, R, C), v.dtype),
        scratch_shapes=[pltpu.SemaphoreType.DMA((N_DEV - 1,)), pltpu.SemaphoreType.DMA((N_DEV - 1,)),
                        pltpu.SemaphoreType.DMA], name=name)(v)


def _other_chips(x, y):
    return [(1 - x, y), (x, 1 - y), (1 - x, 1 - y)]


SEM = pl.BlockSpec(memory_space=pltpu.SEMAPHORE)
DATAFLOW = pltpu.SideEffectType.DATAFLOW_SIDE_EFFECTING


def _chip_copy(buf, a, j, q, c, chips, send_sems, recv_sems):
    px, py = chips[j]
    return pltpu.make_async_remote_copy(src_ref=buf.at[q, c], dst_ref=buf.at[q, c], send_sem=send_sems.at[3 * a + j],
                                        recv_sem=recv_sems.at[3 * a + j], device_id=(px, py, c), device_id_type=MESH)


def allgather_chips_start(bufs, *, name):
    n = len(bufs)

    def body(*refs):
        send_sems, recv_sems = refs[n], refs[n + 1]
        outs = refs[n + 2:2 * n + 2]
        token = refs[2 * n + 2]
        x, y, c = _place()
        chips = _other_chips(x, y)
        for a in range(n):
            for j in range(3):
                _chip_copy(outs[a], a, j, 2 * x + y, c, chips, send_sems, recv_sems).start()
        token[...] = jnp.zeros_like(token)

    res = pl.pallas_call(
        body, name=name, in_specs=[HBM] * n,
        out_specs=(SEM, SEM, *([HBM] * n), VMEM_SPEC),
        out_shape=(pltpu.SemaphoreType.DMA((3 * n,)), pltpu.SemaphoreType.DMA((3 * n,)),
                   *[pltpu.HBM(b.shape, b.dtype) for b in bufs], jax.ShapeDtypeStruct((SUB, LANE), F32)),
        input_output_aliases={a: a + 2 for a in range(n)},
        compiler_params=pltpu.CompilerParams(has_side_effects=DATAFLOW),
    )(*[pltpu.with_memory_space_constraint(b, pltpu.HBM) for b in bufs])
    return res[0], res[1], list(res[2:2 + n]), res[2 + n]


def allgather_chips_wait(send_sems, recv_sems, bufs, after, *, name):
    n = len(bufs)

    def body(*refs):
        ins = refs[:n]
        send_sems, recv_sems = refs[n], refs[n + 1]
        x, y, c = _place()
        chips = _other_chips(x, y)
        for a in range(n):
            for j, (px, py) in enumerate(chips):
                _chip_copy(ins[a], a, j, 2 * x + y, c, chips, send_sems, recv_sems).wait_send()
                _chip_copy(ins[a], a, j, 2 * px + py, c, chips, send_sems, recv_sems).wait_recv()

    return list(pl.pallas_call(
        body, name=name, in_specs=[HBM] * n + [SEM, SEM, pl.BlockSpec(memory_space=pl.ANY)],
        out_specs=[HBM] * n, out_shape=[pltpu.HBM(b.shape, b.dtype) for b in bufs],
        input_output_aliases={a: a for a in range(n)},
        compiler_params=pltpu.CompilerParams(has_side_effects=DATAFLOW),
    )(*bufs, send_sems, recv_sems, after))


def forward_to_sibling(bufs, *, name):
    n = len(bufs)

    def body(*refs):
        outs = refs[n:2 * n]
        send_sems, recv_sems = refs[2 * n:]
        x, y, c = _place()
        chips = _other_chips(x, y)

        def copy(a, j, half, to):
            px, py = chips[j]
            slab = outs[a].at[2 * px + py, half]
            return pltpu.make_async_remote_copy(src_ref=slab, dst_ref=slab, send_sem=send_sems.at[a, j],
                                                recv_sem=recv_sems.at[a, j], device_id=to, device_id_type=MESH)

        sends = [copy(a, j, c, (x, y, 1 - c)) for a in range(n) for j in range(3)]
        for cp in sends:
            cp.start()
        for a in range(n):
            for j in range(3):
                copy(a, j, 1 - c, (x, y, c)).wait_recv()
        for cp in sends:
            cp.wait_send()

    return pl.pallas_call(
        body, in_specs=[HBM] * n, out_specs=[HBM] * n,
        out_shape=[jax.ShapeDtypeStruct(b.shape, b.dtype) for b in bufs],
        input_output_aliases={a: a for a in range(n)},
        scratch_shapes=[pltpu.SemaphoreType.DMA((n, 3)), pltpu.SemaphoreType.DMA((n, 3))], name=name)(*bufs)


def pair_add(g, other, c_idx, *, name, tr=256):
    _, Q, R, C = g.shape
    tr = _tile(R, tr)

    def body(c_ref, g_ref, o_ref, out_ref):
        out_ref[...] = (g_ref[...] + o_ref[...]).astype(BF16)

    return pl.pallas_call(
        body,
        grid_spec=pltpu.PrefetchScalarGridSpec(
            num_scalar_prefetch=1, grid=(Q, R // tr),
            in_specs=[pl.BlockSpec((None, None, tr, C), lambda q, r, c_ref: (c_ref[0], q, r, 0)),
                      pl.BlockSpec((None, tr, C), lambda q, r, c_ref: (q, r, 0))],
            out_specs=pl.BlockSpec((None, tr, C), lambda q, r, c_ref: (q, r, 0))),
        out_shape=jax.ShapeDtypeStruct((Q, R, C), BF16), name=name,
        compiler_params=_cp("parallel", "parallel"))(c_idx, g, other)


def chip_sum(sums, landed, qc_idx, *, name, tr=256):
    _, R, C = sums.shape
    tr = _tile(R, tr)

    def body(qc_ref, own_ref, l_ref, o_ref):
        acc = own_ref[...].astype(F32)
        for k in range(3):
            acc = acc + l_ref[k].astype(F32)
        o_ref[...] = acc

    return pl.pallas_call(
        body,
        grid_spec=pltpu.PrefetchScalarGridSpec(
            num_scalar_prefetch=1, grid=(R // tr,),
            in_specs=[pl.BlockSpec((None, tr, C), lambda r, qc: (qc[0], r, 0)),
                      pl.BlockSpec((3, tr, C), lambda r, qc: (0, r, 0))],
            out_specs=pl.BlockSpec((None, tr, C), lambda r, qc: (qc[1], r, 0))),
        out_shape=jax.ShapeDtypeStruct((2, R, C), F32), name=name,
        compiler_params=_cp("parallel"))(qc_idx, sums, landed)


def half_swap(bufs, *, name):
    n = len(bufs)

    def body(*refs):
        outs = refs[n:2 * n]
        send_sems, recv_sems = refs[2 * n:]
        x, y, c = _place()
        cps = [pltpu.make_async_remote_copy(src_ref=outs[a].at[c], dst_ref=outs[a].at[c], send_sem=send_sems.at[a],
                                            recv_sem=recv_sems.at[a], device_id=(x, y, 1 - c), device_id_type=MESH)
               for a in range(n)]
        for cp in cps:
            cp.start()
        for a in range(n):
            pltpu.make_async_remote_copy(src_ref=outs[a].at[c], dst_ref=outs[a].at[1 - c], send_sem=send_sems.at[a],
                                         recv_sem=recv_sems.at[a], device_id=(x, y, 1 - c),
                                         device_id_type=MESH).wait_recv()
        for cp in cps:
            cp.wait_send()

    return pl.pallas_call(
        body, in_specs=[HBM] * n, out_specs=[HBM] * n,
        out_shape=[jax.ShapeDtypeStruct(b.shape, b.dtype) for b in bufs],
        input_output_aliases={a: a for a in range(n)},
        scratch_shapes=[pltpu.SemaphoreType.DMA((n,)), pltpu.SemaphoreType.DMA((n,))], name=name)(*bufs)


def _chip_copies(src, dst, send_sems, recv_sems):
    x, y, c = _place()
    return [pltpu.make_async_remote_copy(src_ref=src[a].at[2 * px + py], dst_ref=dst[a].at[j],
                                         send_sem=send_sems.at[3 * a + j], recv_sem=recv_sems.at[3 * a + j],
                                         device_id=(px, py, c), device_id_type=MESH)
            for a in range(len(src)) for j, (px, py) in enumerate(_other_chips(x, y))]


def _pair_copies(src, dst, send_sems, recv_sems):
    x, y, c = _place()
    return [pltpu.make_async_remote_copy(src_ref=src[a].at[1 - c], dst_ref=dst[a], send_sem=send_sems.at[a],
                                         recv_sem=recv_sems.at[a], device_id=(x, y, 1 - c), device_id_type=MESH)
            for a in range(len(src))]


def _device_copies(src, dst, send_sems, recv_sems):
    x, y, c = _place()
    mine = src[0].at[4 * x + 2 * y + c]
    return [pltpu.make_async_remote_copy(src_ref=mine, dst_ref=mine, send_sem=send_sems.at[k - 1],
                                         recv_sem=recv_sems.at[k - 1],
                                         device_id=(_flip(x, k & 4), _flip(y, k & 2), _flip(c, k & 1)), device_id_type=MESH)
            for k in range(1, N_DEV)]


def exchange_start(src, landing, copies, n_sems, *, name):
    n, m = len(src), len(src) + len(landing)

    def body(*refs):
        send_sems, recv_sems = refs[m], refs[m + 1]
        for cp in copies(refs[m + 2:m + 2 + n], refs[m + 2 + n:2 * m + 2], send_sems, recv_sems):
            cp.start()
        token = refs[2 * m + 2]
        token[...] = jnp.zeros_like(token)

    res = pl.pallas_call(
        body, name=name, in_specs=[HBM] * m,
        out_specs=(SEM, SEM, *([HBM] * m), VMEM_SPEC),
        out_shape=(pltpu.SemaphoreType.DMA((n_sems,)), pltpu.SemaphoreType.DMA((n_sems,)),
                   *[pltpu.HBM(b.shape, b.dtype) for b in src + landing], jax.ShapeDtypeStruct((SUB, LANE), F32)),
        input_output_aliases={a: a + 2 for a in range(m)},
        compiler_params=pltpu.CompilerParams(has_side_effects=DATAFLOW),
    )(*[pltpu.with_memory_space_constraint(b, pltpu.HBM) for b in src + landing])
    return res[0], res[1], list(res[2:2 + n]), list(res[2 + n:2 + m]), res[2 + m]


def exchange_wait(send_sems, recv_sems, src, landed, copies, after, *, name):
    n, m = len(src), len(src) + len(landed)

    def body(*refs):
        for cp in copies(refs[:n], refs[n:m], refs[m], refs[m + 1]):
            cp.wait_send()
            cp.wait_recv()

    res = pl.pallas_call(
        body, name=name, in_specs=[HBM] * m + [SEM, SEM, pl.BlockSpec(memory_space=pl.ANY)],
        out_specs=[HBM] * m, out_shape=[pltpu.HBM(b.shape, b.dtype) for b in src + landed],
        input_output_aliases={a: a for a in range(m)},
        compiler_params=pltpu.CompilerParams(has_side_effects=DATAFLOW),
    )(*src, *landed, send_sems, recv_sems, after)
    return list(res[:n]), list(res[n:])


def finish_reduce(sums, landed, q, c, tag):
    qc_idx = jnp.stack([q, c]).astype(jnp.int32)
    return [chip_sum(s, l, qc_idx, name=f"grad_chip_sum_{tag}{a}") for a, (s, l) in enumerate(zip(sums, landed))]


def _ffn_forward(x, h, mod, post_g, w_up, w_down, dw_w, dw_b, tag, next_norm=None, tgt=None):
    _, _, gate = mod
    u0 = mm_nn(h, w_up, name=f"{tag}_up", out_dtype=BF16, perm=_ffn_perm)
    z, ab = ffn_act(u0, dw_w, dw_b, name=f"{tag}_act")
    y = mm_nn(z, w_down, name=f"{tag}_down")
    if tgt is None:
        out = post_residual_prenorm(x, y, post_g, gate, *next_norm, name=f"{tag}_post")
    else:
        out = post_residual_loss(x, y, post_g, gate, tgt, name=f"{tag}_post_loss")
    return out, (x, h, u0, ab, z, y)


def _ffn_backward(dx, entry, saved, mod, pre_g, post_g, w_up, w_down, dw_w, tag, before):
    x, h, u0, ab, z, y = saved
    _, sc, gate = mod
    dy, dgate, dpost = entry
    dz = mm_nt(dy, w_down, name=f"{tag}_down_dx", out_dtype=BF16)
    g_down = mm_tn(z, dy, name=f"{tag}_down_dw", J=2, block="a", row_chips=2)
    du0, dconv = ffn_act_bwd(dz, u0, ab, dw_w, name=f"{tag}_act_bwd")
    dh = mm_nt(du0, w_up, name=f"{tag}_up_dx", perm=_ffn_perm)
    g_up = mm_tn(h, du0, name=f"{tag}_up_dw", J=4, block="b", perm=_ffn_perm)
    dx_in, dsh, dsc, dpre, *prev = prenorm_post_bwd(dh, x, dx, pre_g, sc, *before, name=f"{tag}_prenorm_bwd")
    nb = u0.shape[1] // 4
    dconv = dconv[:, 0].reshape(4, 2, 2, nb).transpose(0, 2, 1, 3).reshape(4, 4 * nb)
    return dx_in, dict(dsh=dsh, dsc=dsc, dgate=dgate, dpre=dpre, dpost=dpost, g_up=g_up, g_down=g_down,
                       d_dw_w=dconv[0:FFN_W], d_dw_b=dconv[3:4]), prev


def _local_step(x, tgt, mods, P, first_weights=None, late_weights=None, grads_ready=None):
    m0, m1 = mods
    h1 = prenorm(x, P["pre_mix_g"][0:1], m0[1], m0[0], name="hgrn_prenorm")
    token = None
    if first_weights is not None:
        first, token = first_weights(h1)
        P = {**P, **first}
    proj = mm_nn(h1, P["hgrn_w_in"], name="hgrn_in", after=token)
    o, states = hgrn_scan(proj, P["hgrn_lb_logits"], name="hgrn_scan")
    og = hgrn_gate(o, proj, P["hgrn_gnorm_g"], name="hgrn_gate")
    y1 = mm_nn(og, P["hgrn_w_out"], name="hgrn_out")
    x1, h_f0 = post_residual_prenorm(x, y1, P["post_mix_g"][0:1], m0[2], P["pre_ffn_g"][0:1], m0[4], m0[3],
                                     name="hgrn_post")
    if late_weights is not None:
        P = {**P, **late_weights(x1)}
    (x2, h3), ffn0 = _ffn_forward(x1, h_f0, m0[3:6], P["post_ffn_g"][0:1], P["ffn_w_up"][0], P["ffn_w_down"][0],
                                  P["ffn_dw_w"][0], P["ffn_dw_b"][0:1], "ffn0",
                                  next_norm=(P["pre_mix_g"][1:2], m1[1], m1[0]))
    u = mm_nn(h3, P["conv_w_in"], name="conv_in", bias=P["conv_b_in"])
    s, cv = conv_act(u, P["conv_dw_w"], P["conv_dw_b"], P["conv_ln_g"], P["conv_ln_b"], name="conv_act")
    y3 = mm_nn(s, P["conv_w_out"], name="conv_out", bias=P["conv_b_out"])
    x3, h_f1 = post_residual_prenorm(x2, y3, P["post_mix_g"][1:2], m1[2], P["pre_ffn_g"][1:2], m1[4], m1[3],
                                     name="conv_post")
    (dx4, lcols, *entry_f1), ffn1 = _ffn_forward(x3, h_f1, m1[3:6], P["post_ffn_g"][1:2], P["ffn_w_up"][1], P["ffn_w_down"][1],
                                      P["ffn_dw_w"][1], P["ffn_dw_b"][1:2], "ffn1", tgt=tgt)
    dx3, f1, (dy3, dg1_1, dpostmix1, d_b_out) = _ffn_backward(
        dx4, entry_f1, ffn1, m1[3:6], P["pre_ffn_g"][1:2], P["post_ffn_g"][1:2], P["ffn_w_up"][1], P["ffn_w_down"][1],
        P["ffn_dw_w"][1], "ffn1", before=(y3, P["post_mix_g"][1:2], m1[2]))
    ds = mm_nt(dy3, P["conv_w_out"], name="conv_out_dx")
    g_conv_out = mm_tn(s, dy3, name="conv_out_dw", J=1, block="a", row_chips=4)
    dcv, d_ln_g, d_ln_b, d_dw_b = conv_norm_bwd(ds, cv, P["conv_ln_g"], P["conv_ln_b"], name="conv_norm_bwd")
    du, d_dw_w, d_b_in = conv_glu_bwd(dcv, u, P["conv_dw_w"], name="conv_glu_bwd")
    dh3 = mm_nt(du, P["conv_w_in"], name="conv_in_dx")
    g_conv_in = mm_tn(h3, du, name="conv_in_dw", J=2, block="b", col_chips=2)
    if grads_ready is not None:
        token = grads_ready("l1", [g_conv_in, g_conv_out, f1["g_up"], f1["g_down"]])
        m0 = tuple(m + token[0:1, 0:1] for m in m0)
    dx2, dsh1_1, dsc1_1, dpremix1, *entry_f0 = prenorm_post_bwd(
        dh3, x2, dx3, P["pre_mix_g"][1:2], m1[1], ffn0[5], P["post_ffn_g"][0:1], m0[5], name="conv_prenorm_bwd")
    dx1, f0, (dy1, dg1_0, dpostmix0, _) = _ffn_backward(
        dx2, entry_f0[:3], ffn0, m0[3:6], P["pre_ffn_g"][0:1], P["post_ffn_g"][0:1], P["ffn_w_up"][0],
        P["ffn_w_down"][0], P["ffn_dw_w"][0], "ffn0", before=(y1, P["post_mix_g"][0:1], m0[2]))
    token = grads_ready("f0", [f0["g_up"], f0["g_down"]]) if grads_ready is not None else None
    dog = mm_nt(dy1, P["hgrn_w_out"], name="hgrn_out_dx", after=token)
    g_hgrn_out = mm_tn(og, dy1, name="hgrn_out_dw", J=1, block="a", row_chips=4)
    do, dgp, d_gn = hgrn_gate_bwd(dog, o, proj, P["hgrn_gnorm_g"], name="hgrn_gate_bwd")
    d3, dlb = hgrn_scan_bwd(proj, P["hgrn_lb_logits"], states, do, name="hgrn_scan_bwd")
    g_hgrn_in = mm_tn_parts(h1, d3, dgp, name="hgrn_in_dw")
    token = grads_ready("hg", [g_hgrn_in, g_hgrn_out]) if grads_ready is not None else None
    dh1 = mm_nt_parts(d3, dgp, P["hgrn_w_in"], name="hgrn_in_dx", after=token)
    dx0, dsh1_0, dsc1_0, dpremix0 = prenorm_bwd(dh1, x, dx1, P["pre_mix_g"][0:1], m0[1], name="hgrn_prenorm_bwd")

    dmod = jnp.stack([
        jnp.concatenate([dsh1_0, dsc1_0, dg1_0, f0["dsh"], f0["dsc"], f0["dgate"]], axis=1)[0],
        jnp.concatenate([dsh1_1, dsc1_1, dg1_1, f1["dsh"], f1["dsc"], f1["dgate"]], axis=1)[0]])
    small = dict(
        loss=lcols,
        pre_mix_g=jnp.concatenate([dpremix0, dpremix1]), post_mix_g=jnp.concatenate([dpostmix0, dpostmix1]),
        pre_ffn_g=jnp.concatenate([f0["dpre"], f1["dpre"]]), post_ffn_g=jnp.concatenate([f0["dpost"], f1["dpost"]]),
        lb=dlb, hgrn_gnorm_g=d_gn, ffn_dw_b=jnp.concatenate([f0["d_dw_b"], f1["d_dw_b"]]), dmod=dmod,
        conv_b_in=d_b_in, conv_dw_w=d_dw_w[0:CONV_W], conv_dw_b=d_dw_b, conv_ln_g=d_ln_g, conv_ln_b=d_ln_b,
        conv_b_out=d_b_out, ffn_dw_w=jnp.stack([f0["d_dw_w"], f1["d_dw_w"]]))
    big = [g_hgrn_in, g_hgrn_out, g_conv_in, g_conv_out, f0["g_up"], f1["g_up"], f0["g_down"], f1["g_down"]]
    return dx0, small, big


def _pack(parts, rows=8):
    flat = jnp.concatenate([p.reshape(-1).astype(F32) for p in parts])
    per = rows * 128
    pad = (-flat.shape[0]) % per
    return jnp.pad(flat, (0, pad)).reshape(rows, -1)


def _unpack(flat, shapes):
    out, off = [], 0
    for s in shapes:
        n = 1
        for d in s:
            n *= d
        out.append(flat[..., off:off + n].reshape(flat.shape[:-1] + tuple(s)))
        off += n
    return out


def _from_chips(stacked, axis):
    moved = jnp.moveaxis(stacked, 0, axis)
    shape = list(moved.shape)
    return moved.reshape(shape[:axis] + [shape[axis] * shape[axis + 1]] + shape[axis + 2:])


def _my_shard(full, axis, q):
    n = full.shape[axis] // N_CHIPS
    return lax.dynamic_slice_in_dim(full, q * n, n, axis=axis)


def kernel(x, c, ada_w, ada_b, pre_mix_g, post_mix_g, pre_ffn_g, post_ffn_g, hgrn_w_in, hgrn_lb_logits, hgrn_gnorm_g, hgrn_w_out, conv_w_in, conv_b_in, conv_dw_w, conv_dw_b, conv_ln_g, conv_ln_b, conv_w_out, conv_b_out, ffn_w_up, ffn_dw_w, ffn_dw_b, ffn_w_down, loss_target, m_ada_w, m_ada_b, m_pre_mix_g, m_post_mix_g, m_pre_ffn_g, m_post_ffn_g, m_hgrn_w_in, m_hgrn_lb_logits, m_hgrn_gnorm_g, m_hgrn_w_out, m_conv_w_in, m_conv_b_in, m_conv_dw_w, m_conv_dw_b, m_conv_ln_g, m_conv_ln_b, m_conv_w_out, m_conv_b_out, m_ffn_w_up, m_ffn_dw_w, m_ffn_dw_b, m_ffn_w_down, v_ada_w, v_ada_b, v_pre_mix_g, v_post_mix_g, v_pre_ffn_g, v_post_ffn_g, v_hgrn_w_in, v_hgrn_lb_logits, v_hgrn_gnorm_g, v_hgrn_w_out, v_conv_w_in, v_conv_b_in, v_conv_dw_w, v_conv_dw_b, v_conv_ln_g, v_conv_ln_b, v_conv_w_out, v_conv_b_out, v_ffn_w_up, v_ffn_dw_w, v_ffn_dw_b, v_ffn_w_down):
    W = dict(ada_w=ada_w, ada_b=ada_b, pre_mix_g=pre_mix_g, post_mix_g=post_mix_g, pre_ffn_g=pre_ffn_g,
             post_ffn_g=post_ffn_g, hgrn_w_in=hgrn_w_in, hgrn_lb_logits=hgrn_lb_logits, hgrn_gnorm_g=hgrn_gnorm_g,
             hgrn_w_out=hgrn_w_out, conv_w_in=conv_w_in, conv_b_in=conv_b_in, conv_dw_w=conv_dw_w,
             conv_dw_b=conv_dw_b, conv_ln_g=conv_ln_g, conv_ln_b=conv_ln_b, conv_w_out=conv_w_out,
             conv_b_out=conv_b_out, ffn_w_up=ffn_w_up, ffn_dw_w=ffn_dw_w, ffn_dw_b=ffn_dw_b, ffn_w_down=ffn_w_down)
    M = dict(ada_w=m_ada_w, ada_b=m_ada_b, pre_mix_g=m_pre_mix_g, post_mix_g=m_post_mix_g, pre_ffn_g=m_pre_ffn_g,
             post_ffn_g=m_post_ffn_g, hgrn_w_in=m_hgrn_w_in, hgrn_lb_logits=m_hgrn_lb_logits,
             hgrn_gnorm_g=m_hgrn_gnorm_g, hgrn_w_out=m_hgrn_w_out, conv_w_in=m_conv_w_in, conv_b_in=m_conv_b_in,
             conv_dw_w=m_conv_dw_w, conv_dw_b=m_conv_dw_b, conv_ln_g=m_conv_ln_g, conv_ln_b=m_conv_ln_b,
             conv_w_out=m_conv_w_out, conv_b_out=m_conv_b_out, ffn_w_up=m_ffn_w_up, ffn_dw_w=m_ffn_dw_w,
             ffn_dw_b=m_ffn_dw_b, ffn_w_down=m_ffn_w_down)
    V = dict(ada_w=v_ada_w, ada_b=v_ada_b, pre_mix_g=v_pre_mix_g, post_mix_g=v_post_mix_g, pre_ffn_g=v_pre_ffn_g,
             post_ffn_g=v_post_ffn_g, hgrn_w_in=v_hgrn_w_in, hgrn_lb_logits=v_hgrn_lb_logits,
             hgrn_gnorm_g=v_hgrn_gnorm_g, hgrn_w_out=v_hgrn_w_out, conv_w_in=v_conv_w_in, conv_b_in=v_conv_b_in,
             conv_dw_w=v_conv_dw_w, conv_dw_b=v_conv_dw_b, conv_ln_g=v_conv_ln_g, conv_ln_b=v_conv_ln_b,
             conv_w_out=v_conv_w_out, conv_b_out=v_conv_b_out, ffn_w_up=v_ffn_w_up, ffn_dw_w=v_ffn_dw_w,
             ffn_dw_b=v_ffn_dw_b, ffn_w_down=v_ffn_w_down)
    names = list(W)
    xi, yi, ci = lax.axis_index("x"), lax.axis_index("y"), lax.axis_index("c")
    q = 2 * xi + yi
    me = 2 * q + ci
    D = x.shape[-1]
    L = ada_w.shape[0]

    small_w = ["conv_b_in", "conv_dw_w", "conv_dw_b", "conv_ln_g", "conv_ln_b", "conv_b_out", "ffn_dw_w"]
    small_axis = dict(conv_b_in=1, conv_dw_w=2, conv_dw_b=1, conv_ln_g=1, conv_ln_b=1, conv_b_out=1, ffn_dw_w=2)
    packed = _pack([c] + [W[n] for n in small_w])

    def halves(w):
        shard = w.astype(BF16).reshape(1, 2, w.shape[0] // 2, w.shape[1])
        buf = lax.empty((N_CHIPS,) + shard.shape[1:], BF16)
        return lax.dynamic_update_slice_in_dim(buf, shard, q, axis=0)

    gathered = allgather_devices(packed, name="gather_small_params").reshape(N_DEV, -1)
    c_all = gathered[:, 0:D]
    per_chip = gathered.reshape(N_CHIPS, 2, -1)[:, 0, D:]
    parts = _unpack(per_chip, [W[n].shape for n in small_w])
    P = {n: _from_chips(p, small_axis[n]) for n, p in zip(small_w, parts)}
    P["conv_dw_w"] = P["conv_dw_w"][0]
    for n in ("pre_mix_g", "post_mix_g", "pre_ffn_g", "post_ffn_g", "hgrn_lb_logits", "hgrn_gnorm_g", "ffn_dw_b"):
        P[n] = W[n]

    modp = ada_mod(c_all, ada_w, name="ada_mod")
    ncol = modp.shape[-1]
    mod_all = allgather_devices(modp.reshape(L * N_DEV, ncol), name="gather_mod")
    mod_all = mod_all.reshape(N_CHIPS, 2, L, N_DEV, ncol)[:, 0]
    mod_me = lax.dynamic_index_in_dim(mod_all, me, axis=2, keepdims=False)
    mod = mod_me.transpose(1, 0, 2).reshape(L, N_CHIPS * ncol) + ada_b
    mods = [tuple(mod[l:l + 1, k * D:(k + 1) * D] for k in range(6)) for l in range(L)]

    hg_shards, _ = lax.optimization_barrier(([halves(hgrn_w_in[0]), halves(hgrn_w_out[0])], modp))
    hg_send, hg_recv, hg_bufs, hg_token = allgather_chips_start(hg_shards, name="gather_hgrn_weights_start")
    mods[0] = tuple(m + hg_token[0:1, 0:1] for m in mods[0])

    stack = lambda t: t.reshape(N_CHIPS, t.shape[1] * t.shape[2], t.shape[3])
    rowsh = lambda t: t.reshape(1, N_CHIPS * t.shape[1] * t.shape[2], t.shape[3])
    pairs = lambda t: t.reshape(2, 2, t.shape[1], t.shape[2]).transpose(0, 2, 1, 3).reshape(2, t.shape[1], 2 * t.shape[2])
    late_shards = [conv_w_in[0], conv_w_out[0], ffn_w_up[0], ffn_w_up[1], ffn_w_down[0], ffn_w_down[1]]
    late = {}

    def first_weights(h1):
        g = forward_to_sibling(allgather_chips_wait(hg_send, hg_recv, hg_bufs, h1, name="gather_hgrn_weights_wait"),
                               name="gather_hgrn_weights_forward")
        late_bufs, _ = lax.optimization_barrier(([halves(w) for w in late_shards], g))
        late["send"], late["recv"], late["bufs"], token = allgather_chips_start(late_bufs, name="gather_weights_start")
        return dict(hgrn_w_in=stack(g[0]), hgrn_w_out=rowsh(g[1])), token

    def late_weights(x1):
        landed = allgather_chips_wait(late["send"], late["recv"], late["bufs"], x1, name="gather_weights_wait")
        g = forward_to_sibling(landed, name="gather_weights_forward")
        return dict(conv_w_in=pairs(stack(g[0])), conv_w_out=rowsh(g[1]), ffn_w_up=[stack(g[2]), stack(g[3])],
                    ffn_w_down=[rowsh(g[4]), rowsh(g[5])])

    c_idx = ci.astype(jnp.int32).reshape(1)
    pending, in_flight = {}, {}

    def chip_stage(after):
        tag, (send, recv, grads, landing) = pending.popitem()
        grads, others = exchange_wait(send, recv, grads, landing, _pair_copies, after, name=f"grad_pair_wait_{tag}")
        sums = [pair_add(g_, o_, c_idx, name=f"grad_pair_add_{tag}_{a}") for a, (g_, o_) in enumerate(zip(grads, others))]
        landing = [lax.empty((3,) + s_.shape[1:], s_.dtype) for s_ in sums]
        send, recv, sums, landing, tok = exchange_start(sums, landing, _chip_copies, 3 * len(sums),
                                                        name=f"grad_chip_exchange_start_{tag}")
        in_flight[tag] = (send, recv, sums, landing)
        return tok

    def grads_ready(tag, grads):
        tok = chip_stage(grads[0]) if pending else 0.0
        landing = [lax.empty(g_.shape[1:], g_.dtype) for g_ in grads]
        send, recv, grads, landing, tok2 = exchange_start(grads, landing, _pair_copies, len(grads),
                                                          name=f"grad_pair_start_{tag}")
        pending[tag] = (send, recv, grads, landing)
        return tok + tok2

    grad_x, small, big = _local_step(x[0], loss_target[0], mods, P, first_weights, late_weights, grads_ready)

    small_names = list(small)
    packed_g = _pack([small[n] for n in small_names])
    gs_buf = lax.dynamic_update_slice_in_dim(lax.empty((N_DEV,) + packed_g.shape, F32), packed_g[None], me, axis=0)
    sg_send, sg_recv, gs_buf, _, tok_sg = exchange_start([gs_buf], [], _device_copies, N_DEV - 1,
                                                         name="gather_small_grads_start")

    tok_hg = chip_stage(tok_sg)
    G = {}
    halves = []
    for tag in ("f0", "l1"):
        sums_t, landed_t = exchange_wait(*in_flight[tag], _chip_copies, grad_x, name=f"grad_chip_exchange_wait_{tag}")
        halves += finish_reduce(sums_t, landed_t, q, ci, f"{tag}_")
    red = [f.reshape(2 * f.shape[1], f.shape[2]) for f in half_swap(halves, name="grad_half_swap")]
    G["conv_w_in"], G["conv_w_out"] = red[2][None], red[3][None]
    G["ffn_w_up"] = jnp.stack([red[0], red[4]])
    G["ffn_w_down"] = jnp.stack([red[1], red[5]])

    delta, new_m, new_v = {}, {}, {}

    def adamw_matrix(n, after=None):
        shp = W[n].shape
        two = lambda t: t.reshape(-1, shp[-1])
        d_, m_, v_ = adamw(two(W[n]), two(G[n]), two(M[n]), two(V[n]), name=f"adamw_{n}", after=after)
        delta[n], new_m[n], new_v[n] = d_.reshape(shp), m_.reshape(shp), v_.reshape(shp)

    big_names = ["ada_w", "hgrn_w_in", "hgrn_w_out", "conv_w_in", "conv_w_out", "ffn_w_up", "ffn_w_down"]
    for n in ("conv_w_in", "conv_w_out", "ffn_w_up", "ffn_w_down"):
        adamw_matrix(n, after=tok_hg)

    (gs,), _ = exchange_wait(sg_send, sg_recv, gs_buf, [], _device_copies, delta["ffn_w_down"],
                             name="gather_small_grads_wait")
    dmod_all = _unpack(gs.reshape(N_DEV, -1), [small[n].shape for n in small_names])[small_names.index("dmod")]
    tot = sum_devices(gs, name="sum_small_grads").reshape(1, -1)
    S = dict(zip(small_names, _unpack(tot, [small[n].shape for n in small_names])))
    S = {n: v[0] for n, v in S.items()}
    loss = 0.5 * jnp.sum(S["loss"]) / D

    dmod_q = lax.dynamic_slice_in_dim(dmod_all, q * ncol, ncol, axis=2)
    G["ada_w"] = ada_wgrad(c_all.T, dmod_q.transpose(1, 0, 2), name="ada_wgrad")
    G["ada_b"] = S["dmod"]
    for n in ("pre_mix_g", "post_mix_g", "pre_ffn_g", "post_ffn_g", "hgrn_gnorm_g", "ffn_dw_b"):
        G[n] = S[n]
    G["hgrn_lb_logits"] = lb_logits_grad(hgrn_lb_logits, S["lb"], name="lb_logits_grad")
    G["conv_b_in"] = _my_shard(S["conv_b_in"], 1, q)
    G["conv_dw_w"] = _my_shard(S["conv_dw_w"], 1, q)[None]
    for n in ("conv_dw_b", "conv_ln_g", "conv_ln_b", "conv_b_out"):
        G[n] = _my_shard(S[n], 1, q)
    G["ffn_dw_w"] = _my_shard(S["ffn_dw_w"], 2, q)
    adamw_matrix("ada_w")

    sums_h, landed_h = exchange_wait(*in_flight["hg"], _chip_copies, delta["ada_w"], name="grad_chip_exchange_wait_hg")
    red_h = half_swap(finish_reduce(sums_h, landed_h, q, ci, "hg_"), name="grad_half_swap_hg")
    G["hgrn_w_in"], G["hgrn_w_out"] = [f.reshape(1, 2 * f.shape[1], f.shape[2]) for f in red_h]
    for n in ("hgrn_w_in", "hgrn_w_out"):
        adamw_matrix(n)
    rest = [n for n in names if n not in big_names]
    d_, m_, v_ = adamw(_pack([W[n] for n in rest]), _pack([G[n] for n in rest]), _pack([M[n] for n in rest]),
                       _pack([V[n] for n in rest]), name="adamw_small")
    shapes = [W[n].shape for n in rest]
    for n, a, b_, c_ in zip(rest, _unpack(d_.reshape(-1), shapes), _unpack(m_.reshape(-1), shapes),
                            _unpack(v_.reshape(-1), shapes)):
        delta[n], new_m[n], new_v[n] = a, b_, c_

    return (loss, grad_x[None], *[G[n].reshape(W[n].shape) for n in names], *[delta[n] for n in names],
            *[new_m[n] for n in names], *[new_v[n] for n in names])
```
